```python
import jax, jax.numpy as jnp
from jax import lax
import numpy as np

D_MODEL = 1024
BATCH = 4
SEQ = 4096
DEPTH = 1
DEC_BATCH = 128
DEC_SEQ = 1
PAST_LEN = 16384
PAGE_SIZE = 128

HEAD_DIM = 64
MIX_WIDTH = D_MODEL
A_WIDTH = MIX_WIDTH // 2
A_GROUPS = A_WIDTH // HEAD_DIM
CHUNK = 128
B_WIDTH = MIX_WIDTH - A_WIDTH
B_HEADS = B_WIDTH // HEAD_DIM
KV_HEADS = 2
Q_PER_KV = B_HEADS // KV_HEADS
KV_WIDTH = KV_HEADS * HEAD_DIM
WINDOW = 128
IN_WIDTH = 2 * A_WIDTH + B_WIDTH + 2 * KV_WIDTH
N_EXPERTS = 32
TOP_K = 4
D_FF = D_MODEL
SWIGLU_ALPHA = 1.702
SWIGLU_LIMIT = 7.0
PLE_DIM = 256
MOE_BLOCK = 128
EPS = 1e-5

kernel_name = 'hymba_gmlp_swa_sink_moe_ple_step'


def rmsnorm(x, g):
    xf = x.astype(jnp.float32)
    y = xf * lax.rsqrt(jnp.mean(xf * xf, axis=-1, keepdims=True) + EPS)
    return (y * g.astype(jnp.float32)).astype(x.dtype)


def layernorm(x, g, b):
    xf = x.astype(jnp.float32)
    mu = jnp.mean(xf, axis=-1, keepdims=True)
    xc = xf - mu
    y = xc * lax.rsqrt(jnp.mean(xc * xc, axis=-1, keepdims=True) + EPS)
    return (y * g.astype(jnp.float32) + b.astype(jnp.float32)).astype(x.dtype)


def alibi_slopes(n):
    return jnp.exp2(-8.0 * jnp.arange(1, n + 1, dtype=jnp.float32) / n)


def causal_spatial_weights(w_sp):
    mask = jnp.tril(jnp.ones((CHUNK, CHUNK), dtype=bool))
    return jnp.where(mask, w_sp, jnp.zeros_like(w_sp))


def chunk_mlp_prompt(u, v, w_sp, b_sp):
    b_, s_, _ = u.shape
    nc = s_ // CHUNK
    vc = v.reshape(b_, nc, CHUNK, A_GROUPS, HEAD_DIM)
    w = causal_spatial_weights(w_sp)
    mixed = jnp.einsum('gts,bcsgd->bctgd', w, vc) + b_sp.T[None, None, :, :, None]
    return u * mixed.reshape(b_, s_, A_WIDTH)


def chunk_mlp_sample(u, v, w_sp, b_sp):
    b_, n, _ = u.shape
    w = causal_spatial_weights(w_sp)[:, :n, :n]
    vg = v.reshape(b_, n, A_GROUPS, HEAD_DIM)
    mixed = jnp.einsum('gts,bsgd->btgd', w, vg) + b_sp[:, :n].T[None, :, :, None]
    return u * mixed.reshape(b_, n, A_WIDTH)


def sink_attention(q, k, v, dist, valid, sinks):
    b_, c_, lq = q.shape[:3]
    qg = q.reshape(b_, c_, lq, KV_HEADS, Q_PER_KV, HEAD_DIM)
    s = jnp.einsum('bcqkgd,bcskd->bckgqs', qg, k).astype(jnp.float32) * (HEAD_DIM ** -0.5)
    slopes = alibi_slopes(B_HEADS).reshape(KV_HEADS, Q_PER_KV, 1, 1)
    s = s - slopes * dist.astype(jnp.float32)[None, :, None, None]
    s = jnp.where(valid[None, :, None, None], s, jnp.finfo(jnp.float32).min)
    sink = jnp.broadcast_to(sinks.astype(jnp.float32).reshape(KV_HEADS, Q_PER_KV, 1, 1), s.shape[:-1] + (1,))
    p = jax.nn.softmax(jnp.concatenate([s, sink], axis=-1), axis=-1)[..., :-1]
    out = jnp.einsum('bckgqs,bcskd->bcqkgd', p.astype(v.dtype), v)
    return out.reshape(b_, c_, lq, B_WIDTH)


def swa_prompt(q, k, v, sinks):
    b_, s_ = q.shape[:2]
    nb = s_ // WINDOW
    qb = q.reshape(b_, nb, WINDOW, B_HEADS, HEAD_DIM)

    def with_prev(xs):
        xb = xs.reshape(b_, nb, WINDOW, KV_HEADS, HEAD_DIM)
        prev = jnp.pad(xb[:, :-1], ((0, 0), (1, 0), (0, 0), (0, 0), (0, 0)))
        return jnp.concatenate([prev, xb], axis=2)

    i = jnp.arange(WINDOW)[:, None]
    j = jnp.arange(2 * WINDOW)[None, :]
    dist = i + WINDOW - j
    blk = jnp.arange(nb)[:, None, None]
    valid = (dist >= 0) & (dist < WINDOW) & ((blk > 0) | (j >= WINDOW))
    out = sink_attention(qb, with_prev(k), with_prev(v), dist[None], valid, sinks)
    return out.reshape(b_, s_, B_WIDTH)


def swa_sample(q, k_new, v_new, cache_k, cache_v, sinks):
    w = cache_k.shape[1]
    n = q.shape[1]
    keys = jnp.concatenate([cache_k.astype(k_new.dtype), k_new], axis=1)
    vals = jnp.concatenate([cache_v.astype(v_new.dtype), v_new], axis=1)
    i = jnp.arange(n)[:, None]
    j = jnp.arange(w + n)[None, :]
    dist = i + w - j
    valid = (dist >= 0) & (dist < WINDOW)
    out = sink_attention(q[:, None], keys[:, None], vals[:, None], dist[None], valid[None], sinks)
    return out[:, 0], keys[:, -WINDOW:], vals[:, -WINDOW:]


def clamped_swiglu(hid):
    gate = jnp.minimum(hid[..., :D_FF], SWIGLU_LIMIT)
    up = jnp.clip(hid[..., D_FF:], -SWIGLU_LIMIT, SWIGLU_LIMIT)
    return (up + 1.0) * gate * jax.nn.sigmoid(SWIGLU_ALPHA * gate)


def moe(x, w_router, b_router, w_gu, b_gu, w_dn, b_dn):
    n, d = x.shape
    logits = x.astype(jnp.float32) @ w_router.astype(jnp.float32) + b_router.astype(jnp.float32)
    top_val, top_idx = lax.top_k(logits, TOP_K)
    gates = jax.nn.softmax(top_val, axis=-1)
    nk = n * TOP_K
    flat_e = top_idx.reshape(nk)
    flat_tok = jnp.arange(nk) // TOP_K
    flat_gate = gates.reshape(nk)
    order = jnp.argsort(flat_e)
    sorted_e = flat_e[order]
    counts = jnp.bincount(flat_e, length=N_EXPERTS)
    padded = (counts + MOE_BLOCK - 1) // MOE_BLOCK * MOE_BLOCK
    pad_end = jnp.cumsum(padded)
    pad_start = pad_end - padded
    start = jnp.cumsum(counts) - counts
    dest = pad_start[sorted_e] + jnp.arange(nk) - start[sorted_e]
    n_blocks = (nk + N_EXPERTS * (MOE_BLOCK - 1) + MOE_BLOCK - 1) // MOE_BLOCK
    n_rows = n_blocks * MOE_BLOCK
    row_tok = jnp.full((n_rows,), n, dtype=jnp.int32).at[dest].set(flat_tok[order].astype(jnp.int32))
    row_gate = jnp.zeros((n_rows,), x.dtype).at[dest].set(flat_gate[order].astype(x.dtype))
    block_e = jnp.minimum(jnp.searchsorted(pad_end, jnp.arange(n_blocks) * MOE_BLOCK, side='right'), N_EXPERTS - 1)
    x_pad = jnp.concatenate([x, jnp.zeros((1, d), x.dtype)], axis=0)
    xs = x_pad[row_tok].reshape(n_blocks, MOE_BLOCK, d)

    def expert_block(args):
        xb, e = args
        hid = xb @ w_gu[e] + b_gu[e]
        return clamped_swiglu(hid) @ w_dn[e] + b_dn[e]

    ys = lax.map(expert_block, (xs, block_e)).reshape(n_rows, d)
    out = jnp.zeros((n + 1, d), x.dtype).at[row_tok].add(ys * row_gate[:, None])
    return out[:n]


def layer_forward(h, ple, cache_k, cache_v, g_mix, w_in, ln_v_g, ln_v_b, w_sp, b_sp, sinks,
                  g_out_a, g_out_b, w_o, g_moe, w_router, b_router, w_gu, b_gu, w_dn, b_dn,
                  g_ple, w_ple_gate, w_ple_proj):
    b_, s_, _ = h.shape
    z = rmsnorm(h, g_mix) @ w_in
    u = jax.nn.gelu(z[..., :A_WIDTH])
    va = layernorm(jax.nn.gelu(z[..., A_WIDTH:2 * A_WIDTH]), ln_v_g, ln_v_b)
    o = 2 * A_WIDTH
    q = z[..., o:o + B_WIDTH].reshape(b_, s_, B_HEADS, HEAD_DIM)
    o = o + B_WIDTH
    k = z[..., o:o + KV_WIDTH].reshape(b_, s_, KV_HEADS, HEAD_DIM)
    vb = z[..., o + KV_WIDTH:].reshape(b_, s_, KV_HEADS, HEAD_DIM)
    if cache_k is None:
        ya = chunk_mlp_prompt(u, va, w_sp, b_sp)
        yb = swa_prompt(q, k, vb, sinks)
        new_k, new_v = k[:, -WINDOW:], vb[:, -WINDOW:]
    else:
        ya = chunk_mlp_sample(u, va, w_sp, b_sp)
        yb, new_k, new_v = swa_sample(q, k, vb, cache_k, cache_v, sinks)
    mix = jnp.concatenate([rmsnorm(ya, g_out_a), rmsnorm(yb, g_out_b)], axis=-1) @ w_o
    h = h + mix
    h = h + moe(rmsnorm(h, g_moe).reshape(b_ * s_, D_MODEL), w_router, b_router,
                w_gu, b_gu, w_dn, b_dn).reshape(b_, s_, D_MODEL)
    gate = jax.nn.sigmoid(rmsnorm(h, g_ple) @ w_ple_gate)
    h = h + gate * (ple.astype(h.dtype) @ w_ple_proj)
    return h, new_k, new_v, va


def setup_inputs(seed: int = 0) -> dict:
    key = jax.random.key(seed)
    ks = iter(jax.random.split(key, 32))

    def nrm(shape, scale):
        return scale * jax.random.normal(next(ks), shape, jnp.float32)

    L = DEPTH
    w = min(WINDOW, PAST_LEN)
    return {
        'x_prompt': nrm((BATCH, SEQ, D_MODEL), 1.0),
        'x_sample': nrm((DEC_BATCH, DEC_SEQ, D_MODEL), 1.0),
        'cache_swa_k': nrm((L, DEC_BATCH, w, KV_HEADS, HEAD_DIM), 1.0),
        'cache_swa_v': nrm((L, DEC_BATCH, w, KV_HEADS, HEAD_DIM), 1.0),
        'p_prompt': nrm((L, BATCH, SEQ, PLE_DIM), 1.0),
        'p_sample': nrm((L, DEC_BATCH, DEC_SEQ, PLE_DIM), 1.0),
        'g_mix': 1.0 + nrm((L, D_MODEL), 0.05),
        'w_in': nrm((L, D_MODEL, IN_WIDTH), D_MODEL ** -0.5),
        'ln_v_g': 1.0 + nrm((L, A_WIDTH), 0.05),
        'ln_v_b': nrm((L, A_WIDTH), 0.02),
        'w_sp': nrm((L, A_GROUPS, CHUNK, CHUNK), CHUNK ** -0.5),
        'b_sp': 1.0 + nrm((L, A_GROUPS, CHUNK), 0.05),
        'sinks': nrm((L, B_HEADS), 0.5),
        'g_out_a': 1.0 + nrm((L, A_WIDTH), 0.05),
        'g_out_b': 1.0 + nrm((L, B_WIDTH), 0.05),
        'w_o': nrm((L, MIX_WIDTH, D_MODEL), MIX_WIDTH ** -0.5),
        'g_moe': 1.0 + nrm((L, D_MODEL), 0.05),
        'w_router': nrm((L, D_MODEL, N_EXPERTS), D_MODEL ** -0.5),
        'b_router': nrm((L, N_EXPERTS), 0.01),
        'w_gu': nrm((L, N_EXPERTS, D_MODEL, 2 * D_FF), D_MODEL ** -0.5),
        'b_gu': nrm((L, N_EXPERTS, 2 * D_FF), 0.02),
        'w_dn': nrm((L, N_EXPERTS, D_FF, D_MODEL), D_FF ** -0.5),
        'b_dn': nrm((L, N_EXPERTS, D_MODEL), 0.02),
        'g_ple': 1.0 + nrm((L, D_MODEL), 0.05),
        'w_ple_gate': nrm((L, D_MODEL, D_MODEL), D_MODEL ** -0.5),
        'w_ple_proj': nrm((L, PLE_DIM, D_MODEL), PLE_DIM ** -0.5),
        'g_final': 1.0 + nrm((D_MODEL,), 0.05),
    }


def reference(x_prompt, x_sample, cache_swa_k, cache_swa_v, p_prompt, p_sample,
              g_mix, w_in, ln_v_g, ln_v_b, w_sp, b_sp, sinks, g_out_a, g_out_b, w_o,
              g_moe, w_router, b_router, w_gu, b_gu, w_dn, b_dn,
              g_ple, w_ple_gate, w_ple_proj, g_final):
    hp = x_prompt
    hs = x_sample
    pk, pv, sk, sv, scv = [], [], [], [], []
    for l in range(DEPTH):
        lw = (g_mix[l], w_in[l], ln_v_g[l], ln_v_b[l], w_sp[l], b_sp[l], sinks[l],
              g_out_a[l], g_out_b[l], w_o[l], g_moe[l], w_router[l], b_router[l],
              w_gu[l], b_gu[l], w_dn[l], b_dn[l], g_ple[l], w_ple_gate[l], w_ple_proj[l])
        hp, k_p, v_p, _ = layer_forward(hp, p_prompt[l], None, None, *lw)
        hs, k_s, v_s, va_s = layer_forward(hs, p_sample[l], cache_swa_k[l], cache_swa_v[l], *lw)
        pk.append(k_p)
        pv.append(v_p)
        sk.append(k_s)
        sv.append(v_s)
        scv.append(va_s)
    y_prompt = rmsnorm(hp, g_final)
    y_sample = rmsnorm(hs, g_final)
    return (y_prompt, y_sample, jnp.stack(pk), jnp.stack(pv), jnp.stack(sk), jnp.stack(sv), jnp.stack(scv))
```

```python
import math

import jax
import jax.numpy as jnp
from jax import lax
from jax.experimental import pallas as pl
from jax.experimental.pallas import tpu as pltpu

F32 = jnp.float32
BF16 = jnp.bfloat16

D_MODEL = 1024
BATCH = 4
SEQ = 4096
DEC_BATCH = 128
HEAD_DIM = 64
A_WIDTH = 512
B_WIDTH = 512
B_HEADS = 8
KV_WIDTH = 128
IN_WIDTH = 2 * A_WIDTH + B_WIDTH + 2 * KV_WIDTH
CHUNK = 128
N_EXPERTS = 32
TOP_K = 4
D_FF = 1024
SWIGLU_ALPHA = 1.702
SWIGLU_LIMIT = 7.0
PLE_DIM = 256
EPS = 1e-5

LANES = 128
ROW_BLOCK = 128
N_PROMPT = BATCH * SEQ
N_TOK = N_PROMPT + DEC_BATCH
TOK_TILE = 128
N_TILES = N_TOK // TOK_TILE
DISP_TILE = 256
TM = 512
N_PAD = ((N_TOK + TM - 1) // TM) * TM
N_PAD_TILES = N_PAD // TOK_TILE
N_BLOCKS = (N_TOK * TOP_K + N_EXPERTS * (ROW_BLOCK - 1) + ROW_BLOCK - 1) // ROW_BLOCK
N_ROWS = N_BLOCKS * ROW_BLOCK
WIN = 64
WIN_ALIGN = 16
SAMPLE_TILE = 32
NEG = -1e30
VMEM_LIMIT = 56 * 1024 * 1024


def _rms(x, g):
    return x * lax.rsqrt(jnp.mean(x * x, axis=-1, keepdims=True) + EPS) * g


def _gelu(x):
    c = math.sqrt(2.0 / math.pi)
    return x * (0.5 * (1.0 + jnp.tanh(c * (x + 0.044715 * (x * x * x)))))


def _layernorm(x, g, b):
    mu = jnp.mean(x, axis=-1, keepdims=True)
    xc = x - mu
    return xc * lax.rsqrt(jnp.mean(xc * xc, axis=-1, keepdims=True) + EPS) * g + b


def _lane_iota(shape):
    return lax.broadcasted_iota(jnp.int32, shape, len(shape) - 1)


def _route(xn2, wr_ref, br_ref):
    m = xn2.shape[0]
    xh = xn2.astype(BF16)
    xl = (xn2 - xh.astype(F32)).astype(BF16)
    r = jnp.dot(jnp.concatenate([xh, xl], axis=0), wr_ref[...], preferred_element_type=F32)
    r = r[:m] + r[m:]
    lane = _lane_iota((m, LANES))
    lane_f = lane.astype(F32)
    logits = jnp.where(lane < N_EXPERTS, r + pltpu.roll(r, LANES - N_EXPERTS, 1) + br_ref[...], NEG)
    work = logits
    sel = jnp.zeros((m, LANES), F32)
    top = None
    z = None
    for _ in range(TOP_K):
        mx = jnp.max(work, axis=-1, keepdims=True)
        first = jnp.min(jnp.where(work == mx, lane_f, float(LANES)), axis=-1, keepdims=True)
        hit = lane_f == first
        sel = jnp.where(hit, 1.0, sel)
        work = jnp.where(hit, NEG, work)
        if top is None:
            top = mx
            z = jnp.ones_like(mx)
        else:
            z = z + jnp.exp(mx - top)
    gates = jnp.where(sel > 0.0, jnp.exp(logits - top) / z, 0.0)
    return gates, sel


def _prompt_kernel(sinks_ref, x_ref, gmix_ref, win_ref, lng_ref, lnb_ref, wsp_ref, bsp_ref,
                   goa_ref, gob_ref, wo_ref, gmoe_ref, wr_ref, br_ref,
                   h1_ref, xn_ref, gm_ref, sm_ref, k_ref, v_ref,
                   z_s, kv_s, cat_s):
    g = pl.program_id(0)
    j = g % (SEQ // TM)

    @pl.when(g >= N_PROMPT // TM)
    def _():
        h1_ref[...] = jnp.zeros_like(h1_ref)
        xn_ref[...] = jnp.zeros_like(xn_ref)
        gm_ref[...] = jnp.zeros_like(gm_ref)
        sm_ref[...] = jnp.zeros_like(sm_ref)

    @pl.when(g < N_PROMPT // TM)
    def _():
        _prompt_tile(j, sinks_ref, x_ref, gmix_ref, win_ref, lng_ref, lnb_ref, wsp_ref, bsp_ref,
                     goa_ref, gob_ref, wo_ref, gmoe_ref, wr_ref, br_ref,
                     h1_ref, xn_ref, gm_ref, sm_ref, k_ref, v_ref, z_s, kv_s, cat_s)


def _prompt_tile(j, sinks_ref, x_ref, gmix_ref, win_ref, lng_ref, lnb_ref, wsp_ref, bsp_ref,
                 goa_ref, gob_ref, wo_ref, gmoe_ref, wr_ref, br_ref,
                 h1_ref, xn_ref, gm_ref, sm_ref, k_ref, v_ref, z_s, kv_s, cat_s):
    @pl.when(j == 0)
    def _():
        kv_s[0:CHUNK, :] = jnp.zeros((CHUNK, 2 * KV_WIDTH), F32)

    xn = _rms(x_ref[...], gmix_ref[...]).astype(BF16)
    z_s[...] = jnp.dot(xn, win_ref[...], preferred_element_type=F32)
    kv_s[CHUNK:, :] = z_s[:, 2 * A_WIDTH + B_WIDTH:]

    lane = _lane_iota((CHUNK, LANES))
    lo = lane < HEAD_DIM
    lane2 = _lane_iota((2 * CHUNK, LANES))
    lo2 = lane2 < HEAD_DIM
    qi = lax.broadcasted_iota(jnp.int32, (CHUNK, 2 * CHUNK), 0)
    kj = lax.broadcasted_iota(jnp.int32, (CHUNK, 2 * CHUNK), 1)
    dist_i = qi + CHUNK - kj
    dist = dist_i.astype(F32)
    band = (dist_i >= 0) & (dist_i < CHUNK)

    def chunk_body(c, carry):
        r0 = pl.multiple_of(c * CHUNK, CHUNK)
        rows = pl.ds(r0, CHUNK)
        u = _gelu(z_s[rows, 0:A_WIDTH])
        va = _layernorm(_gelu(z_s[rows, A_WIDTH:2 * A_WIDTH]), lng_ref[...], lnb_ref[...])
        vab = va.astype(BF16)
        slabs = []
        for p in range(A_WIDTH // LANES):
            slab = vab[:, p * LANES:(p + 1) * LANES]
            m0 = jnp.dot(wsp_ref[2 * p], slab, preferred_element_type=F32)
            m1 = jnp.dot(wsp_ref[2 * p + 1], slab, preferred_element_type=F32)
            slabs.append(jnp.where(lo, m0, m1))
        ya = u * (jnp.concatenate(slabs, axis=-1) + bsp_ref[...])
        ya_n = _rms(ya, goa_ref[...])
        k2 = kv_s[pl.ds(r0, 2 * CHUNK), 0:KV_WIDTH]
        v2 = kv_s[pl.ds(r0, 2 * CHUNK), KV_WIDTH:2 * KV_WIDTH]
        k2r = pltpu.roll(k2, HEAD_DIM, 1)
        v2r = pltpu.roll(v2, HEAD_DIM, 1)
        kd = (jnp.where(lo2, k2, k2r).astype(BF16), jnp.where(lo2, k2r, k2).astype(BF16))
        vd = (jnp.where(lo2, v2, v2r).astype(BF16), jnp.where(lo2, v2r, v2).astype(BF16))
        prev_ok = (j > 0) | (c > 0)
        valid = band & ((kj >= CHUNK) | prev_ok)
        yb_slabs = []
        for kv in range(2):
            q0 = z_s[rows, 2 * A_WIDTH + (2 * kv) * LANES:2 * A_WIDTH + (2 * kv + 1) * LANES]
            q1 = z_s[rows, 2 * A_WIDTH + (2 * kv + 1) * LANES:2 * A_WIDTH + (2 * kv + 2) * LANES]
            lhs = jnp.concatenate([jnp.where(lo, q0, 0.0), jnp.where(lo, 0.0, q0),
                                   jnp.where(lo, q1, 0.0), jnp.where(lo, 0.0, q1)], axis=0).astype(BF16)
            s_all = lax.dot_general(lhs, kd[kv], (((1,), (1,)), ((), ())), preferred_element_type=F32)
            probs = []
            for i in range(4):
                h = 4 * kv + i
                slope = 2.0 ** (-(h + 1))
                sink = sinks_ref[h]
                s = s_all[i * CHUNK:(i + 1) * CHUNK] * (HEAD_DIM ** -0.5) - slope * dist
                s = jnp.where(valid, s, NEG)
                mx = jnp.maximum(jnp.max(s, axis=-1, keepdims=True), sink)
                e = jnp.exp(s - mx)
                den = jnp.sum(e, axis=-1, keepdims=True) + jnp.exp(sink - mx)
                probs.append(e * (1.0 / den))
            pm = jnp.concatenate(probs, axis=0).astype(BF16)
            o = jnp.dot(pm, vd[kv], preferred_element_type=F32)
            yb_slabs.append(jnp.where(lo, o[0:CHUNK], o[CHUNK:2 * CHUNK]))
            yb_slabs.append(jnp.where(lo, o[2 * CHUNK:3 * CHUNK], o[3 * CHUNK:4 * CHUNK]))
        yb_n = _rms(jnp.concatenate(yb_slabs, axis=-1), gob_ref[...])
        cat_s[rows, 0:A_WIDTH] = ya_n.astype(BF16)
        cat_s[rows, A_WIDTH:] = yb_n.astype(BF16)
        return carry

    lax.fori_loop(0, TM // CHUNK, chunk_body, 0)

    kv_s[0:CHUNK, :] = kv_s[TM:TM + CHUNK, :]
    k_ref[...] = kv_s[TM:TM + CHUNK, 0:KV_WIDTH]
    v_ref[...] = kv_s[TM:TM + CHUNK, KV_WIDTH:]

    h1 = x_ref[...] + jnp.dot(cat_s[...], wo_ref[...], preferred_element_type=F32)
    h1_ref[...] = h1
    xn2 = _rms(h1, gmoe_ref[...])
    xn_ref[...] = xn2.astype(BF16)
    gates, sel = _route(xn2, wr_ref, br_ref)
    gm_ref[...] = gates
    sm_ref[...] = sel


def _full(shape):
    n = len(shape)
    return pl.BlockSpec(shape, lambda *_: (0,) * n)


def _prompt_call(x, sinks, gmix, win, lng, lnb, wsp, bsp, goa, gob, wo, gmoe, wr, br):
    real = N_PROMPT // TM
    row = lambda g: (g, 0)
    seq = lambda g: (jnp.minimum(g, real - 1) // (SEQ // TM), 0, 0)
    return pl.pallas_call(
        _prompt_kernel,
        grid=(N_PAD // TM,),
        in_specs=[
            pl.BlockSpec(memory_space=pltpu.SMEM),
            pl.BlockSpec((TM, D_MODEL), lambda g: (jnp.minimum(g, real - 1), 0)),
            _full((1, D_MODEL)), _full((D_MODEL, IN_WIDTH)), _full((1, A_WIDTH)), _full((1, A_WIDTH)),
            _full((8, CHUNK, CHUNK)), _full((CHUNK, A_WIDTH)), _full((1, A_WIDTH)), _full((1, B_WIDTH)),
            _full((D_MODEL, D_MODEL)), _full((1, D_MODEL)), _full((D_MODEL, LANES)), _full((1, LANES)),
        ],
        out_specs=[
            pl.BlockSpec((TM, D_MODEL), row),
            pl.BlockSpec((TM, D_MODEL), row),
            pl.BlockSpec((TM, LANES), row),
            pl.BlockSpec((TM, LANES), row),
            pl.BlockSpec((None, CHUNK, KV_WIDTH), seq),
            pl.BlockSpec((None, CHUNK, KV_WIDTH), seq),
        ],
        out_shape=[
            jax.ShapeDtypeStruct((N_PAD, D_MODEL), F32),
            jax.ShapeDtypeStruct((N_PAD, D_MODEL), BF16),
            jax.ShapeDtypeStruct((N_PAD, LANES), F32),
            jax.ShapeDtypeStruct((N_PAD, LANES), F32),
            jax.ShapeDtypeStruct((BATCH, CHUNK, KV_WIDTH), F32),
            jax.ShapeDtypeStruct((BATCH, CHUNK, KV_WIDTH), F32),
        ],
        scratch_shapes=[
            pltpu.VMEM((TM, IN_WIDTH), F32),
            pltpu.VMEM((TM + CHUNK, 2 * KV_WIDTH), F32),
            pltpu.VMEM((TM, D_MODEL), BF16),
        ],
        compiler_params=pltpu.CompilerParams(
            dimension_semantics=("arbitrary",), vmem_limit_bytes=VMEM_LIMIT),
        name="prompt_premoe",
    )(sinks, x, gmix, win, lng, lnb, wsp, bsp, goa, gob, wo, gmoe, wr, br)


def _sample_kernel(sinks_ref, x_ref, ck_ref, cv_ref, gmix_ref, win_ref, lng_ref, lnb_ref, w00_ref, b0_ref,
                   goa_ref, gob_ref, wo_ref, gmoe_ref, wr_ref, br_ref,
                   h1_in, xn_in, gm_in, sm_in,
                   h1_ref, xn_ref, gm_ref, sm_ref, nk_ref, nv_ref, va_ref):
    del h1_in, xn_in, gm_in, sm_in
    t = SAMPLE_TILE
    nkeys = t * CHUNK

    if True:
        x = x_ref[...]
        xn = _rms(x, gmix_ref[...]).astype(BF16)
        z = jnp.dot(xn, win_ref[...], preferred_element_type=F32)
        u = _gelu(z[:, 0:A_WIDTH])
        va = _layernorm(_gelu(z[:, A_WIDTH:2 * A_WIDTH]), lng_ref[...], lnb_ref[...])
        va_ref[...] = va
        ya_n = _rms(u * (w00_ref[...] * va + b0_ref[...]), goa_ref[...])

        knew = z[:, 2 * A_WIDTH + B_WIDTH:2 * A_WIDTH + B_WIDTH + KV_WIDTH]
        vnew = z[:, 2 * A_WIDTH + B_WIDTH + KV_WIDTH:]
        lane = _lane_iota((t, LANES))
        lo = lane < HEAD_DIM
        stacked = []
        for h in range(B_HEADS):
            q = z[:, 2 * A_WIDTH + (h // 2) * LANES:2 * A_WIDTH + (h // 2 + 1) * LANES]
            qh = jnp.where(lo if h % 2 == 0 else jnp.logical_not(lo), q, 0.0)
            if h % 2 != h // 4:
                qh = pltpu.roll(qh, HEAD_DIM, 1)
            stacked.append(qh)
        qs = jnp.concatenate(stacked, axis=0)
        rows = B_HEADS * t
        ridx = lax.broadcasted_iota(jnp.int32, (rows, 1), 0)
        slope = jnp.zeros((rows, 1), F32)
        sink = jnp.zeros((rows, 1), F32)
        for h in range(B_HEADS):
            in_h = (ridx >= h * t) & (ridx < (h + 1) * t)
            slope = jnp.where(in_h, 2.0 ** (-(h + 1)), slope)
            sink = jnp.where(in_h, sinks_ref[h], sink)
        s_c = lax.dot_general(qs.astype(BF16), ck_ref[...].astype(BF16), (((1,), (1,)), ((), ())),
                              preferred_element_type=F32)
        rsamp = lax.broadcasted_iota(jnp.int32, (rows, nkeys), 0) % t
        col = lax.broadcasted_iota(jnp.int32, (rows, nkeys), 1)
        pos = col % CHUNK
        own = ((col // CHUNK) == rsamp) & (pos >= 1)
        s_c = s_c * (HEAD_DIM ** -0.5) - slope * (CHUNK - pos).astype(F32)
        s_c = jnp.where(own, s_c, NEG)
        kn8 = jnp.concatenate([knew] * B_HEADS, axis=0)
        vn8 = jnp.concatenate([vnew] * B_HEADS, axis=0)
        s_n = jnp.sum(qs * kn8, axis=-1, keepdims=True) * (HEAD_DIM ** -0.5)
        mx = jnp.maximum(jnp.maximum(jnp.max(s_c, axis=-1, keepdims=True), s_n), sink)
        e_c = jnp.exp(s_c - mx)
        e_n = jnp.exp(s_n - mx)
        inv = 1.0 / (jnp.sum(e_c, axis=-1, keepdims=True) + e_n + jnp.exp(sink - mx))
        o = jnp.dot((e_c * inv).astype(BF16), cv_ref[...].astype(BF16), preferred_element_type=F32)
        o = o + (e_n * inv) * vn8
        yb_slabs = []
        for p in range(B_WIDTH // LANES):
            outs = []
            for half in range(2):
                h = 2 * p + half
                oh = o[h * t:(h + 1) * t]
                oh = jnp.where(lo if h // 4 == 0 else jnp.logical_not(lo), oh, 0.0)
                if half != h // 4:
                    oh = pltpu.roll(oh, HEAD_DIM, 1)
                outs.append(oh)
            yb_slabs.append(outs[0] + outs[1])
        yb_n = _rms(jnp.concatenate(yb_slabs, axis=-1), gob_ref[...])

        cat = jnp.concatenate([ya_n, yb_n], axis=-1).astype(BF16)
        h1 = x + jnp.dot(cat, wo_ref[...], preferred_element_type=F32)
        xn2 = _rms(h1, gmoe_ref[...])
        gates, sel = _route(xn2, wr_ref, br_ref)
        h1_ref[...] = h1
        xn_ref[...] = xn2.astype(BF16)
        gm_ref[...] = gates
        sm_ref[...] = sel

        nk_ref[...] = pltpu.roll(ck_ref[...], nkeys - 1, 0)
        nv_ref[...] = pltpu.roll(cv_ref[...], nkeys - 1, 0)
        for b in range(t):
            nk_ref[b * CHUNK + CHUNK - 1:b * CHUNK + CHUNK, :] = knew[b:b + 1, :]
            nv_ref[b * CHUNK + CHUNK - 1:b * CHUNK + CHUNK, :] = vnew[b:b + 1, :]


def _sample_call(x, ck, cv, sinks, gmix, win, lng, lnb, w00, b0, goa, gob, wo, gmoe, wr, br, h1, xn, gm, sm):
    t = SAMPLE_TILE
    steps = DEC_BATCH // t
    base = N_PROMPT // t
    inrow = lambda i: (i, 0)
    outrow = lambda i: (base + i, 0)
    anyspec = pl.BlockSpec(memory_space=pl.ANY)
    return pl.pallas_call(
        _sample_kernel,
        grid=(steps,),
        in_specs=[
            pl.BlockSpec(memory_space=pltpu.SMEM),
            pl.BlockSpec((t, D_MODEL), inrow),
            pl.BlockSpec((t * CHUNK, KV_WIDTH), inrow),
            pl.BlockSpec((t * CHUNK, KV_WIDTH), inrow),
            _full((1, D_MODEL)), _full((D_MODEL, IN_WIDTH)), _full((1, A_WIDTH)), _full((1, A_WIDTH)),
            _full((1, A_WIDTH)), _full((1, A_WIDTH)), _full((1, A_WIDTH)), _full((1, B_WIDTH)),
            _full((D_MODEL, D_MODEL)), _full((1, D_MODEL)), _full((D_MODEL, LANES)), _full((1, LANES)),
            anyspec, anyspec, anyspec, anyspec,
        ],
        out_specs=[
            pl.BlockSpec((t, D_MODEL), outrow),
            pl.BlockSpec((t, D_MODEL), outrow),
            pl.BlockSpec((t, LANES), outrow),
            pl.BlockSpec((t, LANES), outrow),
            pl.BlockSpec((t * CHUNK, KV_WIDTH), inrow),
            pl.BlockSpec((t * CHUNK, KV_WIDTH), inrow),
            pl.BlockSpec((t, A_WIDTH), inrow),
        ],
        out_shape=[
            jax.ShapeDtypeStruct((N_PAD, D_MODEL), F32),
            jax.ShapeDtypeStruct((N_PAD, D_MODEL), BF16),
            jax.ShapeDtypeStruct((N_PAD, LANES), F32),
            jax.ShapeDtypeStruct((N_PAD, LANES), F32),
            jax.ShapeDtypeStruct((DEC_BATCH * CHUNK, KV_WIDTH), F32),
            jax.ShapeDtypeStruct((DEC_BATCH * CHUNK, KV_WIDTH), F32),
            jax.ShapeDtypeStruct((DEC_BATCH, A_WIDTH), F32),
        ],
        input_output_aliases={16: 0, 17: 1, 18: 2, 19: 3},
        compiler_params=pltpu.CompilerParams(
            dimension_semantics=("arbitrary",), vmem_limit_bytes=VMEM_LIMIT),
        name="sample_premoe",
    )(sinks, x, ck, cv, gmix, win, lng, lnb, w00, b0, goa, gob, wo, gmoe, wr, br, h1, xn, gm, sm)


def _plan_kernel(sm_ref, gm_ref,
                 destm_ref, destt_ref, gatet_ref, blke_ref, tlo_ref, thi_ref, nblk_ref, astart_ref, nwin_ref,
                 base_s, pstart_s):
    ph = pl.program_id(0)
    i = pl.program_id(1)
    sel = sm_ref[...]
    cnt = jnp.sum(sel, axis=0, keepdims=True)
    lane = _lane_iota((1, LANES))

    @pl.when((ph == 0) & (i == 0))
    def _():
        base_s[...] = jnp.zeros_like(base_s)

    @pl.when(ph == 0)
    def _():
        base_s[...] += cnt

    @pl.when((ph == 1) & (i == 0))
    def _():
        counts = base_s[...]
        padded = jnp.floor((counts + (ROW_BLOCK - 1)) * (1.0 / ROW_BLOCK)) * ROW_BLOCK
        pend = padded
        for s in (1, 2, 4, 8, 16):
            pend = pend + jnp.where(lane >= s, pltpu.roll(pend, s, 1), 0.0)
        pstart_s[...] = pend - padded
        base_s[...] = jnp.zeros_like(base_s)
        brow = lax.broadcasted_iota(jnp.int32, (N_BLOCKS, LANES), 0).astype(F32) * ROW_BLOCK
        done = jnp.where((lane < N_EXPERTS) & (pend <= brow), 1.0, 0.0)
        be = jnp.minimum(jnp.sum(done, axis=-1, keepdims=True), N_EXPERTS - 1.0)
        blke_ref[...] = jnp.broadcast_to(be, (N_BLOCKS, LANES)).astype(jnp.int32)
        total = jnp.sum(jnp.where(lane == N_EXPERTS - 1, pend, 0.0), axis=-1, keepdims=True)
        nblk_ref[...] = jnp.broadcast_to(total * (1.0 / ROW_BLOCK), (8, LANES)).astype(jnp.int32)
        tlo_ref[...] = jnp.full((N_BLOCKS, LANES), 1 << 20, jnp.int32)
        thi_ref[...] = jnp.full((N_BLOCKS, LANES), -1, jnp.int32)
        astart_ref[...] = jnp.zeros_like(astart_ref)
        nwin_ref[...] = jnp.zeros_like(nwin_ref)

    @pl.when(ph == 1)
    def _():
        r = lax.broadcasted_iota(jnp.int32, (TOK_TILE, TOK_TILE), 0)
        c = lax.broadcasted_iota(jnp.int32, (TOK_TILE, TOK_TILE), 1)
        lower = jnp.where(c < r, 1.0, 0.0).astype(BF16)
        prefix = jnp.dot(lower, sel.astype(BF16), preferred_element_type=F32)
        start = pstart_s[...] + base_s[...]
        dest = jnp.where(sel > 0.0, prefix + start, -1.0)
        destm_ref[...] = dest
        destt_ref[...] = dest.T
        gatet_ref[...] = gm_ref[...].T
        has = (cnt > 0.0) & (lane < N_EXPERTS)
        blo = jnp.floor(start * (1.0 / ROW_BLOCK))
        bhi = jnp.floor((start + cnt - 1.0) * (1.0 / ROW_BLOCK))
        brow = lax.broadcasted_iota(jnp.int32, (N_BLOCKS, LANES), 0).astype(F32)
        hit = jnp.where(has & (brow >= blo) & (brow <= bhi), 1.0, 0.0)
        anyhit = jnp.max(hit, axis=-1, keepdims=True) > 0.0
        it = i // (DISP_TILE // TOK_TILE)
        tlo_ref[...] = jnp.where(anyhit, jnp.minimum(tlo_ref[...], it), tlo_ref[...])
        thi_ref[...] = jnp.where(anyhit, jnp.maximum(thi_ref[...], it), thi_ref[...])
        a = jnp.minimum(jnp.floor(start * (1.0 / WIN_ALIGN)) * WIN_ALIGN, float(N_ROWS - WIN))
        nw = jnp.where(has, jnp.floor((start + cnt - a + (WIN - 1)) * (1.0 / WIN)), 0.0)
        astart_ref[pl.ds(i, 1), :] = a.astype(jnp.int32)
        nwin_ref[pl.ds(i, 1), :] = nw.astype(jnp.int32)
        base_s[...] += cnt


def _plan_call(sm, gm):
    tile = lambda ph, i: (i * ph, 0)
    tile_t = lambda ph, i: (0, i * ph)
    nt8 = ((N_PAD_TILES + 7) // 8) * 8
    return pl.pallas_call(
        _plan_kernel,
        grid=(2, N_PAD_TILES),
        in_specs=[pl.BlockSpec((TOK_TILE, LANES), lambda ph, i: (i, 0)),
                  pl.BlockSpec((TOK_TILE, LANES), lambda ph, i: (i, 0))],
        out_specs=[
            pl.BlockSpec((TOK_TILE, LANES), tile),
            pl.BlockSpec((LANES, TOK_TILE), tile_t),
            pl.BlockSpec((LANES, TOK_TILE), tile_t),
            _full((N_BLOCKS, LANES)), _full((N_BLOCKS, LANES)), _full((N_BLOCKS, LANES)),
            _full((8, LANES)), _full((nt8, LANES)), _full((nt8, LANES)),
        ],
        out_shape=[
            jax.ShapeDtypeStruct((N_PAD, LANES), F32),
            jax.ShapeDtypeStruct((LANES, N_PAD), F32),
            jax.ShapeDtypeStruct((LANES, N_PAD), F32),
            jax.ShapeDtypeStruct((N_BLOCKS, LANES), jnp.int32),
            jax.ShapeDtypeStruct((N_BLOCKS, LANES), jnp.int32),
            jax.ShapeDtypeStruct((N_BLOCKS, LANES), jnp.int32),
            jax.ShapeDtypeStruct((8, LANES), jnp.int32),
            jax.ShapeDtypeStruct((nt8, LANES), jnp.int32),
            jax.ShapeDtypeStruct((nt8, LANES), jnp.int32),
        ],
        scratch_shapes=[pltpu.VMEM((1, LANES), F32), pltpu.VMEM((1, LANES), F32)],
        compiler_params=pltpu.CompilerParams(
            dimension_semantics=("arbitrary", "arbitrary"), vmem_limit_bytes=VMEM_LIMIT),
        name="moe_plan",
    )(sm, gm)


def _dispatch_kernel(blke_ref, tlo_ref, thi_ref, nblk_ref, xn_ref, destt_ref, gatet_ref,
                     xs_ref, rg_ref, acc_s, rg_s):
    b = pl.program_id(0)
    acc_s[...] = jnp.zeros_like(acc_s)
    rg_s[...] = jnp.zeros_like(rg_s)

    @pl.when(b < nblk_ref[0])
    def _():
        e = blke_ref[b]
        rowid = (b * ROW_BLOCK + lax.broadcasted_iota(jnp.int32, (ROW_BLOCK, 1), 0)).astype(F32)

        def body(t, carry):
            c0 = pl.multiple_of(t * DISP_TILE, DISP_TILE)
            drow = destt_ref[pl.ds(e, 1), pl.ds(c0, DISP_TILE)]
            grow = gatet_ref[pl.ds(e, 1), pl.ds(c0, DISP_TILE)]
            hit = drow == rowid
            onehot = jnp.where(hit, 1.0, 0.0).astype(BF16)
            acc_s[...] += jnp.dot(onehot, xn_ref[pl.ds(c0, DISP_TILE), :], preferred_element_type=F32)
            rg_s[...] += jnp.sum(jnp.where(hit, grow, 0.0), axis=-1, keepdims=True)
            return carry

        lax.fori_loop(tlo_ref[b], thi_ref[b] + 1, body, 0)

    xs_ref[...] = acc_s[...].astype(BF16)
    rg_ref[...] = rg_s[...]


def _dispatch_call(blke, tlo, thi, nblk, xn, destt, gatet):
    grid_spec = pltpu.PrefetchScalarGridSpec(
        num_scalar_prefetch=4,
        grid=(N_BLOCKS,),
        in_specs=[
            pl.BlockSpec((N_PAD, D_MODEL), lambda b, *_: (0, 0), pipeline_mode=pl.Buffered(1)),
            pl.BlockSpec((N_EXPERTS, N_PAD), lambda b, *_: (0, 0), pipeline_mode=pl.Buffered(1)),
            pl.BlockSpec((N_EXPERTS, N_PAD), lambda b, *_: (0, 0), pipeline_mode=pl.Buffered(1)),
        ],
        out_specs=[
            pl.BlockSpec((ROW_BLOCK, D_MODEL), lambda b, *_: (b, 0)),
            pl.BlockSpec((ROW_BLOCK, 1), lambda b, *_: (b, 0)),
        ],
        scratch_shapes=[pltpu.VMEM((ROW_BLOCK, D_MODEL), F32), pltpu.VMEM((ROW_BLOCK, 1), F32)],
    )
    return pl.pallas_call(
        _dispatch_kernel,
        grid_spec=grid_spec,
        out_shape=[jax.ShapeDtypeStruct((N_ROWS, D_MODEL), BF16),
                   jax.ShapeDtypeStruct((N_ROWS, 1), F32)],
        compiler_params=pltpu.CompilerParams(
            dimension_semantics=("arbitrary",), vmem_limit_bytes=VMEM_LIMIT),
        name="moe_dispatch",
    )(blke, tlo, thi, nblk, xn, destt, gatet)


def _expert_kernel(blke_ref, nblk_ref, xs_ref, rg_ref, wgu_ref, bgu_ref, wdn_ref, bdn_ref,
                   ys_ref, wgu_s, wdn_s):
    b = pl.program_id(0)
    prev = blke_ref[jnp.maximum(b - 1, 0)]
    fresh = (b == 0) | (blke_ref[b] != prev)

    @pl.when(fresh)
    def _():
        wgu_s[...] = wgu_ref[...].astype(BF16)
        wdn_s[...] = wdn_ref[...].astype(BF16)

    @pl.when(b < nblk_ref[0])
    def _():
        hid = jnp.dot(xs_ref[...], wgu_s[...], preferred_element_type=F32) + bgu_ref[...]
        gate = jnp.minimum(hid[:, :D_FF], SWIGLU_LIMIT)
        up = jnp.clip(hid[:, D_FF:], -SWIGLU_LIMIT, SWIGLU_LIMIT)
        act = (up + 1.0) * gate * jax.nn.sigmoid(SWIGLU_ALPHA * gate)
        y = jnp.dot(act.astype(BF16), wdn_s[...], preferred_element_type=F32) + bdn_ref[...]
        ys_ref[...] = (y * rg_ref[...]).astype(BF16)

    @pl.when(b >= nblk_ref[0])
    def _():
        ys_ref[...] = jnp.zeros_like(ys_ref)


def _expert_call(blke, nblk, xs, rg, wgu, bgu, wdn, bdn):
    grid_spec = pltpu.PrefetchScalarGridSpec(
        num_scalar_prefetch=2,
        grid=(N_BLOCKS,),
        in_specs=[
            pl.BlockSpec((ROW_BLOCK, D_MODEL), lambda b, be, nb: (b, 0)),
            pl.BlockSpec((ROW_BLOCK, 1), lambda b, be, nb: (b, 0)),
            pl.BlockSpec((None, D_MODEL, 2 * D_FF), lambda b, be, nb: (be[b], 0, 0)),
            pl.BlockSpec((None, 1, 2 * D_FF), lambda b, be, nb: (be[b], 0, 0)),
            pl.BlockSpec((None, D_FF, D_MODEL), lambda b, be, nb: (be[b], 0, 0)),
            pl.BlockSpec((None, 1, D_MODEL), lambda b, be, nb: (be[b], 0, 0)),
        ],
        out_specs=pl.BlockSpec((ROW_BLOCK, D_MODEL), lambda b, be, nb: (b, 0)),
        scratch_shapes=[pltpu.VMEM((D_MODEL, 2 * D_FF), BF16), pltpu.VMEM((D_FF, D_MODEL), BF16)],
    )
    return pl.pallas_call(
        _expert_kernel,
        grid_spec=grid_spec,
        out_shape=jax.ShapeDtypeStruct((N_ROWS, D_MODEL), BF16),
        compiler_params=pltpu.CompilerParams(
            dimension_semantics=("arbitrary",), vmem_limit_bytes=VMEM_LIMIT),
        name="moe_experts",
    )(blke, nblk, xs, rg, wgu, bgu, wdn, bdn)


def _combine_kernel(*refs):
    astart_ref, nwin_ref, over_ref = refs[0:3]
    win_refs = refs[3:3 + N_EXPERTS]
    (destm_ref, h1_ref, plep_ref, ples_ref, gple_ref, wpg_ref, wpp_ref, gfin_ref, ys_any,
     yp_ref, ysm_ref, moe_s, tmp_s, sem) = refs[3 + N_EXPERTS:]
    i = pl.program_id(0)
    dest = destm_ref[...]
    lane = _lane_iota((TOK_TILE, LANES))
    lane_f = lane.astype(F32)
    lo = lane < WIN
    moe = jnp.zeros((TOK_TILE, D_MODEL), F32)
    group = 4
    for g0 in range(0, N_EXPERTS, group):
        gcols = []
        for p in range(group // 2):
            e0 = g0 + 2 * p
            a0 = astart_ref[i * N_EXPERTS + e0].astype(F32)
            a1 = astart_ref[i * N_EXPERTS + e0 + 1].astype(F32)
            rowid = jnp.where(lo, a0 + lane_f, a1 + lane_f - WIN)
            dcol = jnp.where(lo, dest[:, e0:e0 + 1], dest[:, e0 + 1:e0 + 2])
            gcols.append(jnp.where(dcol == rowid, 1.0, 0.0).astype(BF16))
        onehot = jnp.concatenate(gcols, axis=-1)
        ywin = jnp.concatenate([win_refs[g0 + q][...] for q in range(group)], axis=0)
        moe = moe + jnp.dot(onehot, ywin, preferred_element_type=F32)
    moe_s[...] = moe

    @pl.when(over_ref[i] > 0)
    def _():
        tmp_s[...] = jnp.zeros_like(tmp_s)

        def per_expert(e, carry):
            a = astart_ref[i * N_EXPERTS + e]
            dcol = jnp.sum(jnp.where(lane == e, dest, 0.0), axis=-1, keepdims=True)

            def per_window(w, carry2):
                first = a + w * WIN
                start = pl.multiple_of(jnp.minimum(first, N_ROWS - WIN), WIN_ALIGN)
                cp = pltpu.make_async_copy(ys_any.at[pl.ds(start, WIN)], tmp_s.at[pl.ds(0, WIN)], sem)
                cp.start()
                cp.wait()
                hit = lo & (dcol == start.astype(F32) + lane_f) & (dcol >= first.astype(F32))
                onehot = jnp.where(hit, 1.0, 0.0).astype(BF16)
                moe_s[...] += jnp.dot(onehot, tmp_s[...], preferred_element_type=F32)
                return carry2

            return lax.fori_loop(1, nwin_ref[i * N_EXPERTS + e], per_window, carry)

        lax.fori_loop(0, N_EXPERTS, per_expert, 0)

    is_sample = i == N_TILES - 1
    h2 = h1_ref[...] + moe_s[...]
    ple = jnp.where(is_sample, ples_ref[...], plep_ref[...])
    hn = _rms(h2, gple_ref[...]).astype(BF16)
    gate = jax.nn.sigmoid(jnp.dot(hn, wpg_ref[...], preferred_element_type=F32))
    proj = jnp.dot(ple.astype(BF16), wpp_ref[...], preferred_element_type=F32)
    y = _rms(h2 + gate * proj, gfin_ref[...])

    @pl.when(jnp.logical_not(is_sample))
    def _():
        yp_ref[...] = y

    @pl.when(is_sample)
    def _():
        ysm_ref[...] = y


def _combine_call(astart, nwin, over, ys, destm, h1, plep, ples, gple, wpg, wpp, gfin):
    last = N_TILES - 2

    def win_spec(e):
        return pl.BlockSpec((pl.Element(WIN), pl.Element(D_MODEL)),
                            lambda i, a, nw, ov, e=e: (pl.multiple_of(a[i * N_EXPERTS + e], WIN_ALIGN), 0))

    grid_spec = pltpu.PrefetchScalarGridSpec(
        num_scalar_prefetch=3,
        grid=(N_TILES,),
        in_specs=[win_spec(e) for e in range(N_EXPERTS)] + [
            pl.BlockSpec((TOK_TILE, LANES), lambda i, *_: (i, 0)),
            pl.BlockSpec((TOK_TILE, D_MODEL), lambda i, *_: (i, 0)),
            pl.BlockSpec((TOK_TILE, PLE_DIM), lambda i, *_: (jnp.minimum(i, last), 0)),
            pl.BlockSpec((TOK_TILE, PLE_DIM), lambda i, *_: (0, 0)),
            pl.BlockSpec((1, D_MODEL), lambda i, *_: (0, 0)),
            pl.BlockSpec((D_MODEL, D_MODEL), lambda i, *_: (0, 0)),
            pl.BlockSpec((PLE_DIM, D_MODEL), lambda i, *_: (0, 0)),
            pl.BlockSpec((1, D_MODEL), lambda i, *_: (0, 0)),
            pl.BlockSpec(memory_space=pl.ANY),
        ],
        out_specs=[
            pl.BlockSpec((TOK_TILE, D_MODEL), lambda i, *_: (jnp.minimum(i, last), 0)),
            pl.BlockSpec((TOK_TILE, D_MODEL), lambda i, *_: (0, 0)),
        ],
        scratch_shapes=[pltpu.VMEM((TOK_TILE, D_MODEL), F32), pltpu.VMEM((2 * WIN, D_MODEL), BF16),
                        pltpu.SemaphoreType.DMA],
    )
    return pl.pallas_call(
        _combine_kernel,
        grid_spec=grid_spec,
        out_shape=[jax.ShapeDtypeStruct((N_PROMPT, D_MODEL), F32),
                   jax.ShapeDtypeStruct((DEC_BATCH, D_MODEL), F32)],
        compiler_params=pltpu.CompilerParams(
            dimension_semantics=("arbitrary",), vmem_limit_bytes=VMEM_LIMIT),
        name="moe_combine_tail",
    )(astart, nwin, over, *([ys] * N_EXPERTS), destm, h1, plep, ples, gple, wpg, wpp, gfin, ys)


def kernel(x_prompt, x_sample, cache_swa_k, cache_swa_v, p_prompt, p_sample, g_mix, w_in, ln_v_g, ln_v_b,
           w_sp, b_sp, sinks, g_out_a, g_out_b, w_o, g_moe, w_router, b_router, w_gu, b_gu, w_dn, b_dn,
           g_ple, w_ple_gate, w_ple_proj, g_final):
    l = 0
    row = lambda v: v.reshape(1, -1)
    win = w_in[l].astype(BF16)
    wo = w_o[l].astype(BF16)
    tril = jnp.tril(jnp.ones((CHUNK, CHUNK), dtype=bool))
    wsp = jnp.where(tril, w_sp[l], 0.0).astype(BF16)
    bsp = jnp.repeat(b_sp[l].T, HEAD_DIM, axis=1)
    w00 = row(jnp.repeat(w_sp[l][:, 0, 0], HEAD_DIM))
    b0 = row(jnp.repeat(b_sp[l][:, 0], HEAD_DIM))
    wr_hi = w_router[l].astype(BF16)
    wr_lo = (w_router[l] - wr_hi.astype(F32)).astype(BF16)
    wr = jnp.concatenate([wr_hi, wr_lo, jnp.zeros((D_MODEL, LANES - 2 * N_EXPERTS), BF16)], axis=1)
    br = row(jnp.concatenate([b_router[l], jnp.zeros((LANES - N_EXPERTS,), F32)]))
    common = (row(g_mix[l]), win, row(ln_v_g[l]), row(ln_v_b[l]))
    tail = (row(g_out_a[l]), row(g_out_b[l]), wo, row(g_moe[l]), wr, br)

    h1, xn, gm, sm, k_p, v_p = _prompt_call(
        x_prompt.reshape(N_PROMPT, D_MODEL), sinks[l], *common, wsp, bsp, *tail)
    ck = cache_swa_k[l].reshape(DEC_BATCH * CHUNK, KV_WIDTH)
    cv = cache_swa_v[l].reshape(DEC_BATCH * CHUNK, KV_WIDTH)
    h1, xn, gm, sm, k_s, v_s, va_s = _sample_call(
        x_sample.reshape(DEC_BATCH, D_MODEL), ck, cv, sinks[l], *common, w00, b0, *tail, h1, xn, gm, sm)

    destm, destt, gatet, blke, tlo, thi, nblk, astart, nwin = _plan_call(sm, gm)
    blke1, tlo1, thi1, nblk1 = blke[:, 0], tlo[:, 0], thi[:, 0], nblk[0, :1]
    astart1 = astart[:N_TILES, :N_EXPERTS].reshape(-1)
    nwin2 = nwin[:N_TILES, :N_EXPERTS]
    over1 = (jnp.max(nwin2, axis=1) > 1).astype(jnp.int32)
    nwin1 = nwin2.reshape(-1)

    xs, rg = _dispatch_call(blke1, tlo1, thi1, nblk1, xn, destt[:N_EXPERTS], gatet[:N_EXPERTS])
    ys = _expert_call(blke1, nblk1, xs, rg, w_gu[l], b_gu[l].reshape(N_EXPERTS, 1, 2 * D_FF),
                      w_dn[l], b_dn[l].reshape(N_EXPERTS, 1, D_MODEL))
    y_p, y_s = _combine_call(
        astart1, nwin1, over1, ys, destm, h1,
        p_prompt[l].reshape(N_PROMPT, PLE_DIM), p_sample[l].reshape(DEC_BATCH, PLE_DIM),
        row(g_ple[l]), w_ple_gate[l].astype(BF16), w_ple_proj[l].astype(BF16), row(g_final))

    kv5 = lambda a, n: a.reshape(1, n, CHUNK, 2, HEAD_DIM)
    return (y_p.reshape(BATCH, SEQ, D_MODEL), y_s.reshape(DEC_BATCH, 1, D_MODEL),
            kv5(k_p, BATCH), kv5(v_p, BATCH), kv5(k_s, DEC_BATCH), kv5(v_s, DEC_BATCH),
            va_s.reshape(1, DEC_BATCH, 1, A_WIDTH))
```

```python
import math

import jax
import jax.numpy as jnp
from jax import lax
from jax.experimental import pallas as pl
from jax.experimental.pallas import tpu as pltpu

F32 = jnp.float32
BF16 = jnp.bfloat16

D_MODEL = 1024
BATCH = 4
SEQ = 4096
DEC_BATCH = 128
HEAD_DIM = 64
A_WIDTH = 512
B_WIDTH = 512
B_HEADS = 8
KV_WIDTH = 128
IN_WIDTH = 2 * A_WIDTH + B_WIDTH + 2 * KV_WIDTH
CHUNK = 128
N_EXPERTS = 32
TOP_K = 4
D_FF = 1024
SWIGLU_ALPHA = 1.702
SWIGLU_LIMIT = 7.0
PLE_DIM = 256
EPS = 1e-5

LANES = 128
ROW_BLOCK = 128
EXP_BLOCK = 512
N_PROMPT = BATCH * SEQ
N_TOK = N_PROMPT + DEC_BATCH
TOK_TILE = 128
N_TILES = N_TOK // TOK_TILE
DISP_TILE = 256
TM = 512
N_PAD = ((N_TOK + TM - 1) // TM) * TM
N_PAD_TILES = N_PAD // TOK_TILE
N_XBLOCKS = (N_TOK * TOP_K + N_EXPERTS * (EXP_BLOCK - 1) + EXP_BLOCK - 1) // EXP_BLOCK
N_ROWS = N_XBLOCKS * EXP_BLOCK
N_BLOCKS = N_ROWS // ROW_BLOCK
WIN = 64
WIN_ALIGN = 16
SAMPLE_TILE = 32
NEG = -1e30
VMEM_LIMIT = 56 * 1024 * 1024


def _rms(x, g):
    return x * lax.rsqrt(jnp.mean(x * x, axis=-1, keepdims=True) + EPS) * g


def _gelu(x):
    c = math.sqrt(2.0 / math.pi)
    return x * (0.5 * (1.0 + jnp.tanh(c * (x + 0.044715 * (x * x * x)))))


def _layernorm(x, g, b):
    mu = jnp.mean(x, axis=-1, keepdims=True)
    xc = x - mu
    return xc * lax.rsqrt(jnp.mean(xc * xc, axis=-1, keepdims=True) + EPS) * g + b


def _lane_iota(shape):
    return lax.broadcasted_iota(jnp.int32, shape, len(shape) - 1)


def _route(xn2, wr_ref, br_ref):
    m = xn2.shape[0]
    xh = xn2.astype(BF16)
    xl = (xn2 - xh.astype(F32)).astype(BF16)
    r = jnp.dot(jnp.concatenate([xh, xl], axis=0), wr_ref[...], preferred_element_type=F32)
    r = r[:m] + r[m:]
    lane = _lane_iota((m, LANES))
    lane_f = lane.astype(F32)
    logits = jnp.where(lane < N_EXPERTS, r + pltpu.roll(r, LANES - N_EXPERTS, 1) + br_ref[...], NEG)
    work = logits
    sel = jnp.zeros((m, LANES), F32)
    top = None
    z = None
    for _ in range(TOP_K):
        mx = jnp.max(work, axis=-1, keepdims=True)
        first = jnp.min(jnp.where(work == mx, lane_f, float(LANES)), axis=-1, keepdims=True)
        hit = lane_f == first
        sel = jnp.where(hit, 1.0, sel)
        work = jnp.where(hit, NEG, work)
        if top is None:
            top = mx
            z = jnp.ones_like(mx)
        else:
            z = z + jnp.exp(mx - top)
    gates = jnp.where(sel > 0.0, jnp.exp(logits - top) / z, 0.0)
    return gates, sel


def _prompt_kernel(sinks_ref, x_ref, gmix_ref, win_ref, lng_ref, lnb_ref, wsp_ref, bsp_ref,
                   goa_ref, gob_ref, wo_ref, gmoe_ref, wr_ref, br_ref,
                   h1_ref, xn_ref, gm_ref, sm_ref, k_ref, v_ref,
                   z_s, kv_s, cat_s):
    g = pl.program_id(0)
    j = g % (SEQ // TM)

    @pl.when(g >= N_PROMPT // TM)
    def _():
        h1_ref[...] = jnp.zeros_like(h1_ref)
        xn_ref[...] = jnp.zeros_like(xn_ref)
        gm_ref[...] = jnp.zeros_like(gm_ref)
        sm_ref[...] = jnp.zeros_like(sm_ref)

    @pl.when(g < N_PROMPT // TM)
    def _():
        _prompt_tile(j, sinks_ref, x_ref, gmix_ref, win_ref, lng_ref, lnb_ref, wsp_ref, bsp_ref,
                     goa_ref, gob_ref, wo_ref, gmoe_ref, wr_ref, br_ref,
                     h1_ref, xn_ref, gm_ref, sm_ref, k_ref, v_ref, z_s, kv_s, cat_s)


def _prompt_tile(j, sinks_ref, x_ref, gmix_ref, win_ref, lng_ref, lnb_ref, wsp_ref, bsp_ref,
                 goa_ref, gob_ref, wo_ref, gmoe_ref, wr_ref, br_ref,
                 h1_ref, xn_ref, gm_ref, sm_ref, k_ref, v_ref, z_s, kv_s, cat_s):
    @pl.when(j == 0)
    def _():
        kv_s[0:CHUNK, :] = jnp.zeros((CHUNK, 2 * KV_WIDTH), F32)

    xn = _rms(x_ref[...], gmix_ref[...]).astype(BF16)
    z_s[...] = jnp.dot(xn, win_ref[...], preferred_element_type=F32)
    kv_s[CHUNK:, :] = z_s[:, 2 * A_WIDTH + B_WIDTH:]

    lane = _lane_iota((CHUNK, LANES))
    lo = lane < HEAD_DIM
    lane2 = _lane_iota((2 * CHUNK, LANES))
    lo2 = lane2 < HEAD_DIM
    qi = lax.broadcasted_iota(jnp.int32, (CHUNK, 2 * CHUNK), 0)
    kj = lax.broadcasted_iota(jnp.int32, (CHUNK, 2 * CHUNK), 1)
    dist_i = qi + CHUNK - kj
    dist = dist_i.astype(F32)
    band = (dist_i >= 0) & (dist_i < CHUNK)

    def chunk_body(c, carry):
        r0 = pl.multiple_of(c * CHUNK, CHUNK)
        rows = pl.ds(r0, CHUNK)
        u = _gelu(z_s[rows, 0:A_WIDTH])
        va = _layernorm(_gelu(z_s[rows, A_WIDTH:2 * A_WIDTH]), lng_ref[...], lnb_ref[...])
        vab = va.astype(BF16)
        slabs = []
        for p in range(A_WIDTH // LANES):
            slab = vab[:, p * LANES:(p + 1) * LANES]
            m0 = jnp.dot(wsp_ref[2 * p], slab, preferred_element_type=F32)
            m1 = jnp.dot(wsp_ref[2 * p + 1], slab, preferred_element_type=F32)
            slabs.append(jnp.where(lo, m0, m1))
        ya = u * (jnp.concatenate(slabs, axis=-1) + bsp_ref[...])
        ya_n = _rms(ya, goa_ref[...])
        k2 = kv_s[pl.ds(r0, 2 * CHUNK), 0:KV_WIDTH]
        v2 = kv_s[pl.ds(r0, 2 * CHUNK), KV_WIDTH:2 * KV_WIDTH]
        k2r = pltpu.roll(k2, HEAD_DIM, 1)
        v2r = pltpu.roll(v2, HEAD_DIM, 1)
        kd = (jnp.where(lo2, k2, k2r).astype(BF16), jnp.where(lo2, k2r, k2).astype(BF16))
        vd = (jnp.where(lo2, v2, v2r).astype(BF16), jnp.where(lo2, v2r, v2).astype(BF16))
        prev_ok = (j > 0) | (c > 0)
        valid = band & ((kj >= CHUNK) | prev_ok)
        yb_slabs = []
        for kv in range(2):
            q0 = z_s[rows, 2 * A_WIDTH + (2 * kv) * LANES:2 * A_WIDTH + (2 * kv + 1) * LANES]
            q1 = z_s[rows, 2 * A_WIDTH + (2 * kv + 1) * LANES:2 * A_WIDTH + (2 * kv + 2) * LANES]
            lhs = jnp.concatenate([jnp.where(lo, q0, 0.0), jnp.where(lo, 0.0, q0),
                                   jnp.where(lo, q1, 0.0), jnp.where(lo, 0.0, q1)], axis=0).astype(BF16)
            s_all = lax.dot_general(lhs, kd[kv], (((1,), (1,)), ((), ())), preferred_element_type=F32)
            probs = []
            for i in range(4):
                h = 4 * kv + i
                slope = 2.0 ** (-(h + 1))
                sink = sinks_ref[h]
                s = s_all[i * CHUNK:(i + 1) * CHUNK] * (HEAD_DIM ** -0.5) - slope * dist
                s = jnp.where(valid, s, NEG)
                mx = jnp.maximum(jnp.max(s, axis=-1, keepdims=True), sink)
                e = jnp.exp(s - mx)
                den = jnp.sum(e, axis=-1, keepdims=True) + jnp.exp(sink - mx)
                probs.append(e * (1.0 / den))
            pm = jnp.concatenate(probs, axis=0).astype(BF16)
            o = jnp.dot(pm, vd[kv], preferred_element_type=F32)
            yb_slabs.append(jnp.where(lo, o[0:CHUNK], o[CHUNK:2 * CHUNK]))
            yb_slabs.append(jnp.where(lo, o[2 * CHUNK:3 * CHUNK], o[3 * CHUNK:4 * CHUNK]))
        yb_n = _rms(jnp.concatenate(yb_slabs, axis=-1), gob_ref[...])
        cat_s[rows, 0:A_WIDTH] = ya_n.astype(BF16)
        cat_s[rows, A_WIDTH:] = yb_n.astype(BF16)
        return carry

    lax.fori_loop(0, TM // CHUNK, chunk_body, 0)

    kv_s[0:CHUNK, :] = kv_s[TM:TM + CHUNK, :]
    k_ref[...] = kv_s[TM:TM + CHUNK, 0:KV_WIDTH]
    v_ref[...] = kv_s[TM:TM + CHUNK, KV_WIDTH:]

    h1 = x_ref[...] + jnp.dot(cat_s[...], wo_ref[...], preferred_element_type=F32)
    h1_ref[...] = h1
    xn2 = _rms(h1, gmoe_ref[...])
    xn_ref[...] = xn2.astype(BF16)
    gates, sel = _route(xn2, wr_ref, br_ref)
    gm_ref[...] = gates
    sm_ref[...] = sel


def _full(shape):
    n = len(shape)
    return pl.BlockSpec(shape, lambda *_: (0,) * n)


def _prompt_call(x, sinks, gmix, win, lng, lnb, wsp, bsp, goa, gob, wo, gmoe, wr, br):
    real = N_PROMPT // TM
    row = lambda g: (g, 0)
    seq = lambda g: (jnp.minimum(g, real - 1) // (SEQ // TM), 0, 0)
    return pl.pallas_call(
        _prompt_kernel,
        grid=(N_PAD // TM,),
        in_specs=[
            pl.BlockSpec(memory_space=pltpu.SMEM),
            pl.BlockSpec((TM, D_MODEL), lambda g: (jnp.minimum(g, real - 1), 0)),
            _full((1, D_MODEL)), _full((D_MODEL, IN_WIDTH)), _full((1, A_WIDTH)), _full((1, A_WIDTH)),
            _full((8, CHUNK, CHUNK)), _full((CHUNK, A_WIDTH)), _full((1, A_WIDTH)), _full((1, B_WIDTH)),
            _full((D_MODEL, D_MODEL)), _full((1, D_MODEL)), _full((D_MODEL, LANES)), _full((1, LANES)),
        ],
        out_specs=[
            pl.BlockSpec((TM, D_MODEL), row),
            pl.BlockSpec((TM, D_MODEL), row),
            pl.BlockSpec((TM, LANES), row),
            pl.BlockSpec((TM, LANES), row),
            pl.BlockSpec((None, CHUNK, KV_WIDTH), seq),
            pl.BlockSpec((None, CHUNK, KV_WIDTH), seq),
        ],
        out_shape=[
            jax.ShapeDtypeStruct((N_PAD, D_MODEL), F32),
            jax.ShapeDtypeStruct((N_PAD, D_MODEL), BF16),
            jax.ShapeDtypeStruct((N_PAD, LANES), F32),
            jax.ShapeDtypeStruct((N_PAD, LANES), F32),
            jax.ShapeDtypeStruct((BATCH, CHUNK, KV_WIDTH), F32),
            jax.ShapeDtypeStruct((BATCH, CHUNK, KV_WIDTH), F32),
        ],
        scratch_shapes=[
            pltpu.VMEM((TM, IN_WIDTH), F32),
            pltpu.VMEM((TM + CHUNK, 2 * KV_WIDTH), F32),
            pltpu.VMEM((TM, D_MODEL), BF16),
        ],
        compiler_params=pltpu.CompilerParams(
            dimension_semantics=("arbitrary",), vmem_limit_bytes=VMEM_LIMIT),
        name="prompt_premoe",
    )(sinks, x, gmix, win, lng, lnb, wsp, bsp, goa, gob, wo, gmoe, wr, br)


def _sample_kernel(sinks_ref, x_ref, ck_ref, cv_ref, gmix_ref, win_ref, lng_ref, lnb_ref, w00_ref, b0_ref,
                   goa_ref, gob_ref, wo_ref, gmoe_ref, wr_ref, br_ref,
                   h1_in, xn_in, gm_in, sm_in,
                   h1_ref, xn_ref, gm_ref, sm_ref, nk_ref, nv_ref, va_ref):
    del h1_in, xn_in, gm_in, sm_in
    t = SAMPLE_TILE
    nkeys = t * CHUNK

    if True:
        x = x_ref[...]
        xn = _rms(x, gmix_ref[...]).astype(BF16)
        z = jnp.dot(xn, win_ref[...], preferred_element_type=F32)
        u = _gelu(z[:, 0:A_WIDTH])
        va = _layernorm(_gelu(z[:, A_WIDTH:2 * A_WIDTH]), lng_ref[...], lnb_ref[...])
        va_ref[...] = va
        ya_n = _rms(u * (w00_ref[...] * va + b0_ref[...]), goa_ref[...])

        knew = z[:, 2 * A_WIDTH + B_WIDTH:2 * A_WIDTH + B_WIDTH + KV_WIDTH]
        vnew = z[:, 2 * A_WIDTH + B_WIDTH + KV_WIDTH:]
        lane = _lane_iota((t, LANES))
        lo = lane < HEAD_DIM
        stacked = []
        for h in range(B_HEADS):
            q = z[:, 2 * A_WIDTH + (h // 2) * LANES:2 * A_WIDTH + (h // 2 + 1) * LANES]
            qh = jnp.where(lo if h % 2 == 0 else jnp.logical_not(lo), q, 0.0)
            if h % 2 != h // 4:
                qh = pltpu.roll(qh, HEAD_DIM, 1)
            stacked.append(qh)
        qs = jnp.concatenate(stacked, axis=0)
        rows = B_HEADS * t
        ridx = lax.broadcasted_iota(jnp.int32, (rows, 1), 0)
        slope = jnp.zeros((rows, 1), F32)
        sink = jnp.zeros((rows, 1), F32)
        for h in range(B_HEADS):
            in_h = (ridx >= h * t) & (ridx < (h + 1) * t)
            slope = jnp.where(in_h, 2.0 ** (-(h + 1)), slope)
            sink = jnp.where(in_h, sinks_ref[h], sink)
        s_c = lax.dot_general(qs.astype(BF16), ck_ref[...].astype(BF16), (((1,), (1,)), ((), ())),
                              preferred_element_type=F32)
        rsamp = lax.broadcasted_iota(jnp.int32, (rows, nkeys), 0) % t
        col = lax.broadcasted_iota(jnp.int32, (rows, nkeys), 1)
        pos = col % CHUNK
        own = ((col // CHUNK) == rsamp) & (pos >= 1)
        s_c = s_c * (HEAD_DIM ** -0.5) - slope * (CHUNK - pos).astype(F32)
        s_c = jnp.where(own, s_c, NEG)
        kn8 = jnp.concatenate([knew] * B_HEADS, axis=0)
        vn8 = jnp.concatenate([vnew] * B_HEADS, axis=0)
        s_n = jnp.sum(qs * kn8, axis=-1, keepdims=True) * (HEAD_DIM ** -0.5)
        mx = jnp.maximum(jnp.maximum(jnp.max(s_c, axis=-1, keepdims=True), s_n), sink)
        e_c = jnp.exp(s_c - mx)
        e_n = jnp.exp(s_n - mx)
        inv = 1.0 / (jnp.sum(e_c, axis=-1, keepdims=True) + e_n + jnp.exp(sink - mx))
        o = jnp.dot((e_c * inv).astype(BF16), cv_ref[...].astype(BF16), preferred_element_type=F32)
        o = o + (e_n * inv) * vn8
        yb_slabs = []
        for p in range(B_WIDTH // LANES):
            outs = []
            for half in range(2):
                h = 2 * p + half
                oh = o[h * t:(h + 1) * t]
                oh = jnp.where(lo if h // 4 == 0 else jnp.logical_not(lo), oh, 0.0)
                if half != h // 4:
                    oh = pltpu.roll(oh, HEAD_DIM, 1)
                outs.append(oh)
            yb_slabs.append(outs[0] + outs[1])
        yb_n = _rms(jnp.concatenate(yb_slabs, axis=-1), gob_ref[...])

        cat = jnp.concatenate([ya_n, yb_n], axis=-1).astype(BF16)
        h1 = x + jnp.dot(cat, wo_ref[...], preferred_element_type=F32)
        xn2 = _rms(h1, gmoe_ref[...])
        gates, sel = _route(xn2, wr_ref, br_ref)
        h1_ref[...] = h1
        xn_ref[...] = xn2.astype(BF16)
        gm_ref[...] = gates
        sm_ref[...] = sel

        nk_ref[...] = pltpu.roll(ck_ref[...], nkeys - 1, 0)
        nv_ref[...] = pltpu.roll(cv_ref[...], nkeys - 1, 0)
        for b in range(t):
            nk_ref[b * CHUNK + CHUNK - 1:b * CHUNK + CHUNK, :] = knew[b:b + 1, :]
            nv_ref[b * CHUNK + CHUNK - 1:b * CHUNK + CHUNK, :] = vnew[b:b + 1, :]


def _sample_call(x, ck, cv, sinks, gmix, win, lng, lnb, w00, b0, goa, gob, wo, gmoe, wr, br, h1, xn, gm, sm):
    t = SAMPLE_TILE
    steps = DEC_BATCH // t
    base = N_PROMPT // t
    inrow = lambda i: (i, 0)
    outrow = lambda i: (base + i, 0)
    anyspec = pl.BlockSpec(memory_space=pl.ANY)
    return pl.pallas_call(
        _sample_kernel,
        grid=(steps,),
        in_specs=[
            pl.BlockSpec(memory_space=pltpu.SMEM),
            pl.BlockSpec((t, D_MODEL), inrow),
            pl.BlockSpec((t * CHUNK, KV_WIDTH), inrow),
            pl.BlockSpec((t * CHUNK, KV_WIDTH), inrow),
            _full((1, D_MODEL)), _full((D_MODEL, IN_WIDTH)), _full((1, A_WIDTH)), _full((1, A_WIDTH)),
            _full((1, A_WIDTH)), _full((1, A_WIDTH)), _full((1, A_WIDTH)), _full((1, B_WIDTH)),
            _full((D_MODEL, D_MODEL)), _full((1, D_MODEL)), _full((D_MODEL, LANES)), _full((1, LANES)),
            anyspec, anyspec, anyspec, anyspec,
        ],
        out_specs=[
            pl.BlockSpec((t, D_MODEL), outrow),
            pl.BlockSpec((t, D_MODEL), outrow),
            pl.BlockSpec((t, LANES), outrow),
            pl.BlockSpec((t, LANES), outrow),
            pl.BlockSpec((t * CHUNK, KV_WIDTH), inrow),
            pl.BlockSpec((t * CHUNK, KV_WIDTH), inrow),
            pl.BlockSpec((t, A_WIDTH), inrow),
        ],
        out_shape=[
            jax.ShapeDtypeStruct((N_PAD, D_MODEL), F32),
            jax.ShapeDtypeStruct((N_PAD, D_MODEL), BF16),
            jax.ShapeDtypeStruct((N_PAD, LANES), F32),
            jax.ShapeDtypeStruct((N_PAD, LANES), F32),
            jax.ShapeDtypeStruct((DEC_BATCH * CHUNK, KV_WIDTH), F32),
            jax.ShapeDtypeStruct((DEC_BATCH * CHUNK, KV_WIDTH), F32),
            jax.ShapeDtypeStruct((DEC_BATCH, A_WIDTH), F32),
        ],
        input_output_aliases={16: 0, 17: 1, 18: 2, 19: 3},
        compiler_params=pltpu.CompilerParams(
            dimension_semantics=("arbitrary",), vmem_limit_bytes=VMEM_LIMIT),
        name="sample_premoe",
    )(sinks, x, ck, cv, gmix, win, lng, lnb, w00, b0, goa, gob, wo, gmoe, wr, br, h1, xn, gm, sm)


def _plan_kernel(sm_ref, gm_ref,
                 destm_ref, destt_ref, gatet_ref, blke_ref, tlo_ref, thi_ref, nblk_ref, astart_ref, nwin_ref,
                 base_s, pstart_s):
    ph = pl.program_id(0)
    i = pl.program_id(1)
    sel = sm_ref[...]
    cnt = jnp.sum(sel, axis=0, keepdims=True)
    lane = _lane_iota((1, LANES))

    @pl.when((ph == 0) & (i == 0))
    def _():
        base_s[...] = jnp.zeros_like(base_s)

    @pl.when(ph == 0)
    def _():
        base_s[...] += cnt

    @pl.when((ph == 1) & (i == 0))
    def _():
        counts = base_s[...]
        padded = jnp.floor((counts + (EXP_BLOCK - 1)) * (1.0 / EXP_BLOCK)) * EXP_BLOCK
        pend = padded
        for s in (1, 2, 4, 8, 16):
            pend = pend + jnp.where(lane >= s, pltpu.roll(pend, s, 1), 0.0)
        pstart_s[...] = pend - padded
        base_s[...] = jnp.zeros_like(base_s)
        brow = lax.broadcasted_iota(jnp.int32, (N_BLOCKS, LANES), 0).astype(F32) * ROW_BLOCK
        done = jnp.where((lane < N_EXPERTS) & (pend <= brow), 1.0, 0.0)
        be = jnp.minimum(jnp.sum(done, axis=-1, keepdims=True), N_EXPERTS - 1.0)
        blke_ref[...] = jnp.broadcast_to(be, (N_BLOCKS, LANES)).astype(jnp.int32)
        total = jnp.sum(jnp.where(lane == N_EXPERTS - 1, pend, 0.0), axis=-1, keepdims=True)
        nblk_ref[...] = jnp.broadcast_to(total * (1.0 / ROW_BLOCK), (8, LANES)).astype(jnp.int32)
        tlo_ref[...] = jnp.full((N_BLOCKS, LANES), 1 << 20, jnp.int32)
        thi_ref[...] = jnp.full((N_BLOCKS, LANES), -1, jnp.int32)
        astart_ref[...] = jnp.zeros_like(astart_ref)
        nwin_ref[...] = jnp.zeros_like(nwin_ref)

    @pl.when(ph == 1)
    def _():
        r = lax.broadcasted_iota(jnp.int32, (TOK_TILE, TOK_TILE), 0)
        c = lax.broadcasted_iota(jnp.int32, (TOK_TILE, TOK_TILE), 1)
        lower = jnp.where(c < r, 1.0, 0.0).astype(BF16)
        prefix = jnp.dot(lower, sel.astype(BF16), preferred_element_type=F32)
        start = pstart_s[...] + base_s[...]
        dest = jnp.where(sel > 0.0, prefix + start, -1.0)
        destm_ref[...] = dest
        destt_ref[...] = dest.T
        gatet_ref[...] = gm_ref[...].T
        has = (cnt > 0.0) & (lane < N_EXPERTS)
        blo = jnp.floor(start * (1.0 / ROW_BLOCK))
        bhi = jnp.floor((start + cnt - 1.0) * (1.0 / ROW_BLOCK))
        brow = lax.broadcasted_iota(jnp.int32, (N_BLOCKS, LANES), 0).astype(F32)
        hit = jnp.where(has & (brow >= blo) & (brow <= bhi), 1.0, 0.0)
        anyhit = jnp.max(hit, axis=-1, keepdims=True) > 0.0
        it = i // (DISP_TILE // TOK_TILE)
        tlo_ref[...] = jnp.where(anyhit, jnp.minimum(tlo_ref[...], it), tlo_ref[...])
        thi_ref[...] = jnp.where(anyhit, jnp.maximum(thi_ref[...], it), thi_ref[...])
        a = jnp.minimum(jnp.floor(start * (1.0 / WIN_ALIGN)) * WIN_ALIGN, float(N_ROWS - WIN))
        nw = jnp.where(has, jnp.floor((start + cnt - a + (WIN - 1)) * (1.0 / WIN)), 0.0)
        astart_ref[pl.ds(i, 1), :] = a.astype(jnp.int32)
        nwin_ref[pl.ds(i, 1), :] = nw.astype(jnp.int32)
        base_s[...] += cnt


def _plan_call(sm, gm):
    tile = lambda ph, i: (i * ph, 0)
    tile_t = lambda ph, i: (0, i * ph)
    nt8 = ((N_PAD_TILES + 7) // 8) * 8
    return pl.pallas_call(
        _plan_kernel,
        grid=(2, N_PAD_TILES),
        in_specs=[pl.BlockSpec((TOK_TILE, LANES), lambda ph, i: (i, 0)),
                  pl.BlockSpec((TOK_TILE, LANES), lambda ph, i: (i, 0))],
        out_specs=[
            pl.BlockSpec((TOK_TILE, LANES), tile),
            pl.BlockSpec((LANES, TOK_TILE), tile_t),
            pl.BlockSpec((LANES, TOK_TILE), tile_t),
            _full((N_BLOCKS, LANES)), _full((N_BLOCKS, LANES)), _full((N_BLOCKS, LANES)),
            _full((8, LANES)), _full((nt8, LANES)), _full((nt8, LANES)),
        ],
        out_shape=[
            jax.ShapeDtypeStruct((N_PAD, LANES), F32),
            jax.ShapeDtypeStruct((LANES, N_PAD), F32),
            jax.ShapeDtypeStruct((LANES, N_PAD), F32),
            jax.ShapeDtypeStruct((N_BLOCKS, LANES), jnp.int32),
            jax.ShapeDtypeStruct((N_BLOCKS, LANES), jnp.int32),
            jax.ShapeDtypeStruct((N_BLOCKS, LANES), jnp.int32),
            jax.ShapeDtypeStruct((8, LANES), jnp.int32),
            jax.ShapeDtypeStruct((nt8, LANES), jnp.int32),
            jax.ShapeDtypeStruct((nt8, LANES), jnp.int32),
        ],
        scratch_shapes=[pltpu.VMEM((1, LANES), F32), pltpu.VMEM((1, LANES), F32)],
        compiler_params=pltpu.CompilerParams(
            dimension_semantics=("arbitrary", "arbitrary"), vmem_limit_bytes=VMEM_LIMIT),
        name="moe_plan",
    )(sm, gm)


def _dispatch_kernel(blke_ref, tlo_ref, thi_ref, nblk_ref, xn_ref, destt_ref, gatet_ref,
                     xs_ref, rg_ref, acc_s, rg_s):
    b = pl.program_id(0)
    acc_s[...] = jnp.zeros_like(acc_s)
    rg_s[...] = jnp.zeros_like(rg_s)

    @pl.when(b < nblk_ref[0])
    def _():
        e = blke_ref[b]
        rowid = (b * ROW_BLOCK + lax.broadcasted_iota(jnp.int32, (ROW_BLOCK, 1), 0)).astype(F32)

        def body(t, carry):
            c0 = pl.multiple_of(t * DISP_TILE, DISP_TILE)
            drow = destt_ref[pl.ds(e, 1), pl.ds(c0, DISP_TILE)]
            grow = gatet_ref[pl.ds(e, 1), pl.ds(c0, DISP_TILE)]
            hit = drow == rowid
            onehot = jnp.where(hit, 1.0, 0.0).astype(BF16)
            acc_s[...] += jnp.dot(onehot, xn_ref[pl.ds(c0, DISP_TILE), :], preferred_element_type=F32)
            rg_s[...] += jnp.sum(jnp.where(hit, grow, 0.0), axis=-1, keepdims=True)
            return carry

        lax.fori_loop(tlo_ref[b], thi_ref[b] + 1, body, 0)

    xs_ref[...] = acc_s[...].astype(BF16)
    rg_ref[...] = rg_s[...]


def _dispatch_call(blke, tlo, thi, nblk, xn, destt, gatet):
    grid_spec = pltpu.PrefetchScalarGridSpec(
        num_scalar_prefetch=4,
        grid=(N_BLOCKS,),
        in_specs=[
            pl.BlockSpec((N_PAD, D_MODEL), lambda b, *_: (0, 0), pipeline_mode=pl.Buffered(1)),
            pl.BlockSpec((N_EXPERTS, N_PAD), lambda b, *_: (0, 0), pipeline_mode=pl.Buffered(1)),
            pl.BlockSpec((N_EXPERTS, N_PAD), lambda b, *_: (0, 0), pipeline_mode=pl.Buffered(1)),
        ],
        out_specs=[
            pl.BlockSpec((ROW_BLOCK, D_MODEL), lambda b, *_: (b, 0)),
            pl.BlockSpec((ROW_BLOCK, 1), lambda b, *_: (b, 0)),
        ],
        scratch_shapes=[pltpu.VMEM((ROW_BLOCK, D_MODEL), F32), pltpu.VMEM((ROW_BLOCK, 1), F32)],
    )
    return pl.pallas_call(
        _dispatch_kernel,
        grid_spec=grid_spec,
        out_shape=[jax.ShapeDtypeStruct((N_ROWS, D_MODEL), BF16),
                   jax.ShapeDtypeStruct((N_ROWS, 1), F32)],
        compiler_params=pltpu.CompilerParams(
            dimension_semantics=("arbitrary",), vmem_limit_bytes=VMEM_LIMIT),
        name="moe_dispatch",
    )(blke, tlo, thi, nblk, xn, destt, gatet)


def _expert_kernel(blke_ref, nblk_ref, xs_ref, rg_ref, wgu_ref, bgu_ref, wdn_ref, bdn_ref,
                   ys_ref, wgu_s, wdn_s):
    b = pl.program_id(0)
    prev = blke_ref[jnp.maximum(b - 1, 0)]
    fresh = (b == 0) | (blke_ref[b] != prev)

    @pl.when(fresh)
    def _():
        wgu_s[...] = wgu_ref[...].astype(BF16)
        wdn_s[...] = wdn_ref[...].astype(BF16)

    @pl.when(b < nblk_ref[0])
    def _():
        hid = jnp.dot(xs_ref[...], wgu_s[...], preferred_element_type=F32) + bgu_ref[...]
        gate = jnp.minimum(hid[:, :D_FF], SWIGLU_LIMIT)
        up = jnp.clip(hid[:, D_FF:], -SWIGLU_LIMIT, SWIGLU_LIMIT)
        act = (up + 1.0) * gate * jax.nn.sigmoid(SWIGLU_ALPHA * gate)
        y = jnp.dot(act.astype(BF16), wdn_s[...], preferred_element_type=F32) + bdn_ref[...]
        ys_ref[...] = (y * rg_ref[...]).astype(BF16)

    @pl.when(b >= nblk_ref[0])
    def _():
        ys_ref[...] = jnp.zeros_like(ys_ref)


def _expert_call(blke, nblk, xs, rg, wgu, bgu, wdn, bdn):
    grid_spec = pltpu.PrefetchScalarGridSpec(
        num_scalar_prefetch=2,
        grid=(N_XBLOCKS,),
        in_specs=[
            pl.BlockSpec((EXP_BLOCK, D_MODEL), lambda b, be, nb: (b, 0)),
            pl.BlockSpec((EXP_BLOCK, 1), lambda b, be, nb: (b, 0)),
            pl.BlockSpec((None, D_MODEL, 2 * D_FF), lambda b, be, nb: (be[b], 0, 0)),
            pl.BlockSpec((None, 1, 2 * D_FF), lambda b, be, nb: (be[b], 0, 0)),
            pl.BlockSpec((None, D_FF, D_MODEL), lambda b, be, nb: (be[b], 0, 0)),
            pl.BlockSpec((None, 1, D_MODEL), lambda b, be, nb: (be[b], 0, 0)),
        ],
        out_specs=pl.BlockSpec((EXP_BLOCK, D_MODEL), lambda b, be, nb: (b, 0)),
        scratch_shapes=[pltpu.VMEM((D_MODEL, 2 * D_FF), BF16), pltpu.VMEM((D_FF, D_MODEL), BF16)],
    )
    return pl.pallas_call(
        _expert_kernel,
        grid_spec=grid_spec,
        out_shape=jax.ShapeDtypeStruct((N_ROWS, D_MODEL), BF16),
        compiler_params=pltpu.CompilerParams(
            dimension_semantics=("arbitrary",), vmem_limit_bytes=VMEM_LIMIT),
        name="moe_experts",
    )(blke, nblk, xs, rg, wgu, bgu, wdn, bdn)


def _combine_kernel(*refs):
    astart_ref, nwin_ref, over_ref = refs[0:3]
    win_refs = refs[3:3 + N_EXPERTS]
    (destm_ref, h1_ref, plep_ref, ples_ref, gple_ref, wpg_ref, wpp_ref, gfin_ref, ys_any,
     yp_ref, ysm_ref, moe_s, tmp_s, sem) = refs[3 + N_EXPERTS:]
    i = pl.program_id(0)
    dest = destm_ref[...]
    lane = _lane_iota((TOK_TILE, LANES))
    lane_f = lane.astype(F32)
    lo = lane < WIN
    moe = jnp.zeros((TOK_TILE, D_MODEL), F32)
    group = 4
    for g0 in range(0, N_EXPERTS, group):
        gcols = []
        for p in range(group // 2):
            e0 = g0 + 2 * p
            a0 = astart_ref[i * N_EXPERTS + e0].astype(F32)
            a1 = astart_ref[i * N_EXPERTS + e0 + 1].astype(F32)
            rowid = jnp.where(lo, a0 + lane_f, a1 + lane_f - WIN)
            dcol = jnp.where(lo, dest[:, e0:e0 + 1], dest[:, e0 + 1:e0 + 2])
            gcols.append(jnp.where(dcol == rowid, 1.0, 0.0).astype(BF16))
        onehot = jnp.concatenate(gcols, axis=-1)
        ywin = jnp.concatenate([win_refs[g0 + q][...] for q in range(group)], axis=0)
        moe = moe + jnp.dot(onehot, ywin, preferred_element_type=F32)
    moe_s[...] = moe

    @pl.when(over_ref[i] > 0)
    def _():
        tmp_s[...] = jnp.zeros_like(tmp_s)

        def per_expert(e, carry):
            a = astart_ref[i * N_EXPERTS + e]
            dcol = jnp.sum(jnp.where(lane == e, dest, 0.0), axis=-1, keepdims=True)

            def per_window(w, carry2):
                first = a + w * WIN
                start = pl.multiple_of(jnp.minimum(first, N_ROWS - WIN), WIN_ALIGN)
                cp = pltpu.make_async_copy(ys_any.at[pl.ds(start, WIN)], tmp_s.at[pl.ds(0, WIN)], sem)
                cp.start()
                cp.wait()
                hit = lo & (dcol == start.astype(F32) + lane_f) & (dcol >= first.astype(F32))
                onehot = jnp.where(hit, 1.0, 0.0).astype(BF16)
                moe_s[...] += jnp.dot(onehot, tmp_s[...], preferred_element_type=F32)
                return carry2

            return lax.fori_loop(1, nwin_ref[i * N_EXPERTS + e], per_window, carry)

        lax.fori_loop(0, N_EXPERTS, per_expert, 0)

    is_sample = i == N_TILES - 1
    h2 = h1_ref[...] + moe_s[...]
    ple = jnp.where(is_sample, ples_ref[...], plep_ref[...])
    hn = _rms(h2, gple_ref[...]).astype(BF16)
    gate = jax.nn.sigmoid(jnp.dot(hn, wpg_ref[...], preferred_element_type=F32))
    proj = jnp.dot(ple.astype(BF16), wpp_ref[...], preferred_element_type=F32)
    y = _rms(h2 + gate * proj, gfin_ref[...])

    @pl.when(jnp.logical_not(is_sample))
    def _():
        yp_ref[...] = y

    @pl.when(is_sample)
    def _():
        ysm_ref[...] = y


def _combine_call(astart, nwin, over, ys, destm, h1, plep, ples, gple, wpg, wpp, gfin):
    last = N_TILES - 2

    def win_spec(e):
        return pl.BlockSpec((pl.Element(WIN), pl.Element(D_MODEL)),
                            lambda i, a, nw, ov, e=e: (pl.multiple_of(a[i * N_EXPERTS + e], WIN_ALIGN), 0))

    grid_spec = pltpu.PrefetchScalarGridSpec(
        num_scalar_prefetch=3,
        grid=(N_TILES,),
        in_specs=[win_spec(e) for e in range(N_EXPERTS)] + [
            pl.BlockSpec((TOK_TILE, LANES), lambda i, *_: (i, 0)),
            pl.BlockSpec((TOK_TILE, D_MODEL), lambda i, *_: (i, 0)),
            pl.BlockSpec((TOK_TILE, PLE_DIM), lambda i, *_: (jnp.minimum(i, last), 0)),
            pl.BlockSpec((TOK_TILE, PLE_DIM), lambda i, *_: (0, 0)),
            pl.BlockSpec((1, D_MODEL), lambda i, *_: (0, 0)),
            pl.BlockSpec((D_MODEL, D_MODEL), lambda i, *_: (0, 0)),
            pl.BlockSpec((PLE_DIM, D_MODEL), lambda i, *_: (0, 0)),
            pl.BlockSpec((1, D_MODEL), lambda i, *_: (0, 0)),
            pl.BlockSpec(memory_space=pl.ANY),
        ],
        out_specs=[
            pl.BlockSpec((TOK_TILE, D_MODEL), lambda i, *_: (jnp.minimum(i, last), 0)),
            pl.BlockSpec((TOK_TILE, D_MODEL), lambda i, *_: (0, 0)),
        ],
        scratch_shapes=[pltpu.VMEM((TOK_TILE, D_MODEL), F32), pltpu.VMEM((2 * WIN, D_MODEL), BF16),
                        pltpu.SemaphoreType.DMA],
    )
    return pl.pallas_call(
        _combine_kernel,
        grid_spec=grid_spec,
        out_shape=[jax.ShapeDtypeStruct((N_PROMPT, D_MODEL), F32),
                   jax.ShapeDtypeStruct((DEC_BATCH, D_MODEL), F32)],
        compiler_params=pltpu.CompilerParams(
            dimension_semantics=("arbitrary",), vmem_limit_bytes=VMEM_LIMIT),
        name="moe_combine_tail",
    )(astart, nwin, over, *([ys] * N_EXPERTS), destm, h1, plep, ples, gple, wpg, wpp, gfin, ys)


def kernel(x_prompt, x_sample, cache_swa_k, cache_swa_v, p_prompt, p_sample, g_mix, w_in, ln_v_g, ln_v_b,
           w_sp, b_sp, sinks, g_out_a, g_out_b, w_o, g_moe, w_router, b_router, w_gu, b_gu, w_dn, b_dn,
           g_ple, w_ple_gate, w_ple_proj, g_final):
    l = 0
    row = lambda v: v.reshape(1, -1)
    win = w_in[l].astype(BF16)
    wo = w_o[l].astype(BF16)
    tril = jnp.tril(jnp.ones((CHUNK, CHUNK), dtype=bool))
    wsp = jnp.where(tril, w_sp[l], 0.0).astype(BF16)
    bsp = jnp.repeat(b_sp[l].T, HEAD_DIM, axis=1)
    w00 = row(jnp.repeat(w_sp[l][:, 0, 0], HEAD_DIM))
    b0 = row(jnp.repeat(b_sp[l][:, 0], HEAD_DIM))
    wr_hi = w_router[l].astype(BF16)
    wr_lo = (w_router[l] - wr_hi.astype(F32)).astype(BF16)
    wr = jnp.concatenate([wr_hi, wr_lo, jnp.zeros((D_MODEL, LANES - 2 * N_EXPERTS), BF16)], axis=1)
    br = row(jnp.concatenate([b_router[l], jnp.zeros((LANES - N_EXPERTS,), F32)]))
    common = (row(g_mix[l]), win, row(ln_v_g[l]), row(ln_v_b[l]))
    tail = (row(g_out_a[l]), row(g_out_b[l]), wo, row(g_moe[l]), wr, br)

    h1, xn, gm, sm, k_p, v_p = _prompt_call(
        x_prompt.reshape(N_PROMPT, D_MODEL), sinks[l], *common, wsp, bsp, *tail)
    ck = cache_swa_k[l].reshape(DEC_BATCH * CHUNK, KV_WIDTH)
    cv = cache_swa_v[l].reshape(DEC_BATCH * CHUNK, KV_WIDTH)
    h1, xn, gm, sm, k_s, v_s, va_s = _sample_call(
        x_sample.reshape(DEC_BATCH, D_MODEL), ck, cv, sinks[l], *common, w00, b0, *tail, h1, xn, gm, sm)

    destm, destt, gatet, blke, tlo, thi, nblk, astart, nwin = _plan_call(sm, gm)
    blke1, tlo1, thi1, nblk1 = blke[:, 0], tlo[:, 0], thi[:, 0], nblk[0, :1]
    astart1 = astart[:N_TILES, :N_EXPERTS].reshape(-1)
    nwin2 = nwin[:N_TILES, :N_EXPERTS]
    over1 = (jnp.max(nwin2, axis=1) > 1).astype(jnp.int32)
    nwin1 = nwin2.reshape(-1)

    xs, rg = _dispatch_call(blke1, tlo1, thi1, nblk1, xn, destt[:N_EXPERTS], gatet[:N_EXPERTS])
    per_x = EXP_BLOCK // ROW_BLOCK
    ys = _expert_call(blke1[::per_x], nblk1 // per_x, xs, rg, w_gu[l], b_gu[l].reshape(N_EXPERTS, 1, 2 * D_FF),
                      w_dn[l], b_dn[l].reshape(N_EXPERTS, 1, D_MODEL))
    y_p, y_s = _combine_call(
        astart1, nwin1, over1, ys, destm, h1,
        p_prompt[l].reshape(N_PROMPT, PLE_DIM), p_sample[l].reshape(DEC_BATCH, PLE_DIM),
        row(g_ple[l]), w_ple_gate[l].astype(BF16), w_ple_proj[l].astype(BF16), row(g_final))

    kv5 = lambda a, n: a.reshape(1, n, CHUNK, 2, HEAD_DIM)
    return (y_p.reshape(BATCH, SEQ, D_MODEL), y_s.reshape(DEC_BATCH, 1, D_MODEL),
            kv5(k_p, BATCH), kv5(v_p, BATCH), kv5(k_s, DEC_BATCH), kv5(v_s, DEC_BATCH),
            va_s.reshape(1, DEC_BATCH, 1, A_WIDTH))
```

```python
import math

import jax
import jax.numpy as jnp
from jax import lax
from jax.experimental import pallas as pl
from jax.experimental.pallas import tpu as pltpu

F32 = jnp.float32
BF16 = jnp.bfloat16

D_MODEL = 1024
BATCH = 4
SEQ = 4096
DEC_BATCH = 128
HEAD_DIM = 64
A_WIDTH = 512
B_WIDTH = 512
B_HEADS = 8
KV_WIDTH = 128
IN_WIDTH = 2 * A_WIDTH + B_WIDTH + 2 * KV_WIDTH
CHUNK = 128
N_EXPERTS = 32
TOP_K = 4
D_FF = 1024
SWIGLU_ALPHA = 1.702
SWIGLU_LIMIT = 7.0
PLE_DIM = 256
EPS = 1e-5

LANES = 128
ROW_BLOCK = 128
EXP_BLOCK = 512
N_PROMPT = BATCH * SEQ
N_TOK = N_PROMPT + DEC_BATCH
TOK_TILE = 128
N_TILES = N_TOK // TOK_TILE
DISP_TILE = 256
TM = 512
N_PAD = ((N_TOK + TM - 1) // TM) * TM
N_PAD_TILES = N_PAD // TOK_TILE
DISP_CHUNK = 32
N_XBLOCKS = (N_TOK * TOP_K + N_EXPERTS * (DISP_CHUNK + EXP_BLOCK - 1) + EXP_BLOCK - 1) // EXP_BLOCK
N_ROWS = N_XBLOCKS * EXP_BLOCK
XBE_ROWS = ((N_XBLOCKS + 7) // 8) * 8
TAB_ROWS = ((N_PAD_TILES + 7) // 8) * 8
PACK = D_MODEL // 2 // LANES
WIN = 64
WIN_ALIGN = 16
SAMPLE_TILE = 32
NEG = -1e30
VMEM_LIMIT = 56 * 1024 * 1024


def _rms(x, g):
    return x * lax.rsqrt(jnp.mean(x * x, axis=-1, keepdims=True) + EPS) * g


def _gelu(x):
    c = math.sqrt(2.0 / math.pi)
    return x * (0.5 * (1.0 + jnp.tanh(c * (x + 0.044715 * (x * x * x)))))


def _layernorm(x, g, b):
    mu = jnp.mean(x, axis=-1, keepdims=True)
    xc = x - mu
    return xc * lax.rsqrt(jnp.mean(xc * xc, axis=-1, keepdims=True) + EPS) * g + b


def _lane_iota(shape):
    return lax.broadcasted_iota(jnp.int32, shape, len(shape) - 1)


def _route(xn2, wr_ref, br_ref):
    m = xn2.shape[0]
    xh = xn2.astype(BF16)
    xl = (xn2 - xh.astype(F32)).astype(BF16)
    r = jnp.dot(jnp.concatenate([xh, xl], axis=0), wr_ref[...], preferred_element_type=F32)
    r = r[:m] + r[m:]
    lane = _lane_iota((m, LANES))
    lane_f = lane.astype(F32)
    logits = jnp.where(lane < N_EXPERTS, r + pltpu.roll(r, LANES - N_EXPERTS, 1) + br_ref[...], NEG)
    work = logits
    sel = jnp.zeros((m, LANES), F32)
    top = None
    z = None
    for _ in range(TOP_K):
        mx = jnp.max(work, axis=-1, keepdims=True)
        first = jnp.min(jnp.where(work == mx, lane_f, float(LANES)), axis=-1, keepdims=True)
        hit = lane_f == first
        sel = jnp.where(hit, 1.0, sel)
        work = jnp.where(hit, NEG, work)
        if top is None:
            top = mx
            z = jnp.ones_like(mx)
        else:
            z = z + jnp.exp(mx - top)
    gates = jnp.where(sel > 0.0, jnp.exp(logits - top) / z, 0.0)
    return gates, sel


def _prompt_kernel(sinks_ref, x_ref, gmix_ref, win_ref, lng_ref, lnb_ref, wsp_ref, bsp_ref,
                   goa_ref, gob_ref, wo_ref, gmoe_ref, wr_ref, br_ref,
                   h1_ref, xn_ref, gm_ref, sm_ref, k_ref, v_ref,
                   z_s, kv_s, cat_s):
    g = pl.program_id(0)
    j = g % (SEQ // TM)

    @pl.when(g >= N_PROMPT // TM)
    def _():
        h1_ref[...] = jnp.zeros_like(h1_ref)
        xn_ref[...] = jnp.zeros_like(xn_ref)
        gm_ref[...] = jnp.zeros_like(gm_ref)
        sm_ref[...] = jnp.zeros_like(sm_ref)

    @pl.when(g < N_PROMPT // TM)
    def _():
        _prompt_tile(j, sinks_ref, x_ref, gmix_ref, win_ref, lng_ref, lnb_ref, wsp_ref, bsp_ref,
                     goa_ref, gob_ref, wo_ref, gmoe_ref, wr_ref, br_ref,
                     h1_ref, xn_ref, gm_ref, sm_ref, k_ref, v_ref, z_s, kv_s, cat_s)


def _prompt_tile(j, sinks_ref, x_ref, gmix_ref, win_ref, lng_ref, lnb_ref, wsp_ref, bsp_ref,
                 goa_ref, gob_ref, wo_ref, gmoe_ref, wr_ref, br_ref,
                 h1_ref, xn_ref, gm_ref, sm_ref, k_ref, v_ref, z_s, kv_s, cat_s):
    @pl.when(j == 0)
    def _():
        kv_s[0:CHUNK, :] = jnp.zeros((CHUNK, 2 * KV_WIDTH), F32)

    xn = _rms(x_ref[...], gmix_ref[...]).astype(BF16)
    z_s[...] = jnp.dot(xn, win_ref[...], preferred_element_type=F32)
    kv_s[CHUNK:, :] = z_s[:, 2 * A_WIDTH + B_WIDTH:]

    lane = _lane_iota((CHUNK, LANES))
    lo = lane < HEAD_DIM
    lane2 = _lane_iota((2 * CHUNK, LANES))
    lo2 = lane2 < HEAD_DIM
    qi = lax.broadcasted_iota(jnp.int32, (CHUNK, 2 * CHUNK), 0)
    kj = lax.broadcasted_iota(jnp.int32, (CHUNK, 2 * CHUNK), 1)
    dist_i = qi + CHUNK - kj
    dist = dist_i.astype(F32)
    band = (dist_i >= 0) & (dist_i < CHUNK)

    def chunk_body(c, carry):
        r0 = pl.multiple_of(c * CHUNK, CHUNK)
        rows = pl.ds(r0, CHUNK)
        u = _gelu(z_s[rows, 0:A_WIDTH])
        va = _layernorm(_gelu(z_s[rows, A_WIDTH:2 * A_WIDTH]), lng_ref[...], lnb_ref[...])
        vab = va.astype(BF16)
        slabs = []
        for p in range(A_WIDTH // LANES):
            slab = vab[:, p * LANES:(p + 1) * LANES]
            m0 = jnp.dot(wsp_ref[2 * p], slab, preferred_element_type=F32)
            m1 = jnp.dot(wsp_ref[2 * p + 1], slab, preferred_element_type=F32)
            slabs.append(jnp.where(lo, m0, m1))
        ya = u * (jnp.concatenate(slabs, axis=-1) + bsp_ref[...])
        ya_n = _rms(ya, goa_ref[...])
        k2 = kv_s[pl.ds(r0, 2 * CHUNK), 0:KV_WIDTH]
        v2 = kv_s[pl.ds(r0, 2 * CHUNK), KV_WIDTH:2 * KV_WIDTH]
        k2r = pltpu.roll(k2, HEAD_DIM, 1)
        v2r = pltpu.roll(v2, HEAD_DIM, 1)
        kd = (jnp.where(lo2, k2, k2r).astype(BF16), jnp.where(lo2, k2r, k2).astype(BF16))
        vd = (jnp.where(lo2, v2, v2r).astype(BF16), jnp.where(lo2, v2r, v2).astype(BF16))
        prev_ok = (j > 0) | (c > 0)
        valid = band & ((kj >= CHUNK) | prev_ok)
        yb_slabs = []
        for kv in range(2):
            q0 = z_s[rows, 2 * A_WIDTH + (2 * kv) * LANES:2 * A_WIDTH + (2 * kv + 1) * LANES]
            q1 = z_s[rows, 2 * A_WIDTH + (2 * kv + 1) * LANES:2 * A_WIDTH + (2 * kv + 2) * LANES]
            lhs = jnp.concatenate([jnp.where(lo, q0, 0.0), jnp.where(lo, 0.0, q0),
                                   jnp.where(lo, q1, 0.0), jnp.where(lo, 0.0, q1)], axis=0).astype(BF16)
            s_all = lax.dot_general(lhs, kd[kv], (((1,), (1,)), ((), ())), preferred_element_type=F32)
            probs = []
            for i in range(4):
                h = 4 * kv + i
                slope = 2.0 ** (-(h + 1))
                sink = sinks_ref[h]
                s = s_all[i * CHUNK:(i + 1) * CHUNK] * (HEAD_DIM ** -0.5) - slope * dist
                s = jnp.where(valid, s, NEG)
                mx = jnp.maximum(jnp.max(s, axis=-1, keepdims=True), sink)
                e = jnp.exp(s - mx)
                den = jnp.sum(e, axis=-1, keepdims=True) + jnp.exp(sink - mx)
                probs.append(e * (1.0 / den))
            pm = jnp.concatenate(probs, axis=0).astype(BF16)
            o = jnp.dot(pm, vd[kv], preferred_element_type=F32)
            yb_slabs.append(jnp.where(lo, o[0:CHUNK], o[CHUNK:2 * CHUNK]))
            yb_slabs.append(jnp.where(lo, o[2 * CHUNK:3 * CHUNK], o[3 * CHUNK:4 * CHUNK]))
        yb_n = _rms(jnp.concatenate(yb_slabs, axis=-1), gob_ref[...])
        cat_s[rows, 0:A_WIDTH] = ya_n.astype(BF16)
        cat_s[rows, A_WIDTH:] = yb_n.astype(BF16)
        return carry

    lax.fori_loop(0, TM // CHUNK, chunk_body, 0)

    kv_s[0:CHUNK, :] = kv_s[TM:TM + CHUNK, :]
    k_ref[...] = kv_s[TM:TM + CHUNK, 0:KV_WIDTH]
    v_ref[...] = kv_s[TM:TM + CHUNK, KV_WIDTH:]

    h1 = x_ref[...] + jnp.dot(cat_s[...], wo_ref[...], preferred_element_type=F32)
    h1_ref[...] = h1
    xn2 = _rms(h1, gmoe_ref[...])
    xn_ref[...] = xn2.astype(BF16)
    gates, sel = _route(xn2, wr_ref, br_ref)
    gm_ref[...] = gates
    sm_ref[...] = sel


def _full(shape):
    n = len(shape)
    return pl.BlockSpec(shape, lambda *_: (0,) * n)


def _prompt_call(x, sinks, gmix, win, lng, lnb, wsp, bsp, goa, gob, wo, gmoe, wr, br):
    real = N_PROMPT // TM
    row = lambda g: (g, 0)
    seq = lambda g: (jnp.minimum(g, real - 1) // (SEQ // TM), 0, 0)
    return pl.pallas_call(
        _prompt_kernel,
        grid=(N_PAD // TM,),
        in_specs=[
            pl.BlockSpec(memory_space=pltpu.SMEM),
            pl.BlockSpec((TM, D_MODEL), lambda g: (jnp.minimum(g, real - 1), 0)),
            _full((1, D_MODEL)), _full((D_MODEL, IN_WIDTH)), _full((1, A_WIDTH)), _full((1, A_WIDTH)),
            _full((8, CHUNK, CHUNK)), _full((CHUNK, A_WIDTH)), _full((1, A_WIDTH)), _full((1, B_WIDTH)),
            _full((D_MODEL, D_MODEL)), _full((1, D_MODEL)), _full((D_MODEL, LANES)), _full((1, LANES)),
        ],
        out_specs=[
            pl.BlockSpec((TM, D_MODEL), row),
            pl.BlockSpec((TM, D_MODEL), row),
            pl.BlockSpec((TM, LANES), row),
            pl.BlockSpec((TM, LANES), row),
            pl.BlockSpec((None, CHUNK, KV_WIDTH), seq),
            pl.BlockSpec((None, CHUNK, KV_WIDTH), seq),
        ],
        out_shape=[
            jax.ShapeDtypeStruct((N_PAD, D_MODEL), F32),
            jax.ShapeDtypeStruct((N_PAD, D_MODEL), BF16),
            jax.ShapeDtypeStruct((N_PAD, LANES), F32),
            jax.ShapeDtypeStruct((N_PAD, LANES), F32),
            jax.ShapeDtypeStruct((BATCH, CHUNK, KV_WIDTH), F32),
            jax.ShapeDtypeStruct((BATCH, CHUNK, KV_WIDTH), F32),
        ],
        scratch_shapes=[
            pltpu.VMEM((TM, IN_WIDTH), F32),
            pltpu.VMEM((TM + CHUNK, 2 * KV_WIDTH), F32),
            pltpu.VMEM((TM, D_MODEL), BF16),
        ],
        compiler_params=pltpu.CompilerParams(
            dimension_semantics=("arbitrary",), vmem_limit_bytes=VMEM_LIMIT),
        name="prompt_premoe",
    )(sinks, x, gmix, win, lng, lnb, wsp, bsp, goa, gob, wo, gmoe, wr, br)


def _sample_kernel(sinks_ref, x_ref, ck_ref, cv_ref, gmix_ref, win_ref, lng_ref, lnb_ref, w00_ref, b0_ref,
                   goa_ref, gob_ref, wo_ref, gmoe_ref, wr_ref, br_ref,
                   h1_in, xn_in, gm_in, sm_in,
                   h1_ref, xn_ref, gm_ref, sm_ref, nk_ref, nv_ref, va_ref):
    del h1_in, xn_in, gm_in, sm_in
    t = SAMPLE_TILE
    nkeys = t * CHUNK

    if True:
        x = x_ref[...]
        xn = _rms(x, gmix_ref[...]).astype(BF16)
        z = jnp.dot(xn, win_ref[...], preferred_element_type=F32)
        u = _gelu(z[:, 0:A_WIDTH])
        va = _layernorm(_gelu(z[:, A_WIDTH:2 * A_WIDTH]), lng_ref[...], lnb_ref[...])
        va_ref[...] = va
        ya_n = _rms(u * (w00_ref[...] * va + b0_ref[...]), goa_ref[...])

        knew = z[:, 2 * A_WIDTH + B_WIDTH:2 * A_WIDTH + B_WIDTH + KV_WIDTH]
        vnew = z[:, 2 * A_WIDTH + B_WIDTH + KV_WIDTH:]
        lane = _lane_iota((t, LANES))
        lo = lane < HEAD_DIM
        stacked = []
        for h in range(B_HEADS):
            q = z[:, 2 * A_WIDTH + (h // 2) * LANES:2 * A_WIDTH + (h // 2 + 1) * LANES]
            qh = jnp.where(lo if h % 2 == 0 else jnp.logical_not(lo), q, 0.0)
            if h % 2 != h // 4:
                qh = pltpu.roll(qh, HEAD_DIM, 1)
            stacked.append(qh)
        qs = jnp.concatenate(stacked, axis=0)
        rows = B_HEADS * t
        ridx = lax.broadcasted_iota(jnp.int32, (rows, 1), 0)
        slope = jnp.zeros((rows, 1), F32)
        sink = jnp.zeros((rows, 1), F32)
        for h in range(B_HEADS):
            in_h = (ridx >= h * t) & (ridx < (h + 1) * t)
            slope = jnp.where(in_h, 2.0 ** (-(h + 1)), slope)
            sink = jnp.where(in_h, sinks_ref[h], sink)
        s_c = lax.dot_general(qs.astype(BF16), ck_ref[...].astype(BF16), (((1,), (1,)), ((), ())),
                              preferred_element_type=F32)
        rsamp = lax.broadcasted_iota(jnp.int32, (rows, nkeys), 0) % t
        col = lax.broadcasted_iota(jnp.int32, (rows, nkeys), 1)
        pos = col % CHUNK
        own = ((col // CHUNK) == rsamp) & (pos >= 1)
        s_c = s_c * (HEAD_DIM ** -0.5) - slope * (CHUNK - pos).astype(F32)
        s_c = jnp.where(own, s_c, NEG)
        kn8 = jnp.concatenate([knew] * B_HEADS, axis=0)
        vn8 = jnp.concatenate([vnew] * B_HEADS, axis=0)
        s_n = jnp.sum(qs * kn8, axis=-1, keepdims=True) * (HEAD_DIM ** -0.5)
        mx = jnp.maximum(jnp.maximum(jnp.max(s_c, axis=-1, keepdims=True), s_n), sink)
        e_c = jnp.exp(s_c - mx)
        e_n = jnp.exp(s_n - mx)
        inv = 1.0 / (jnp.sum(e_c, axis=-1, keepdims=True) + e_n + jnp.exp(sink - mx))
        o = jnp.dot((e_c * inv).astype(BF16), cv_ref[...].astype(BF16), preferred_element_type=F32)
        o = o + (e_n * inv) * vn8
        yb_slabs = []
        for p in range(B_WIDTH // LANES):
            outs = []
            for half in range(2):
                h = 2 * p + half
                oh = o[h * t:(h + 1) * t]
                oh = jnp.where(lo if h // 4 == 0 else jnp.logical_not(lo), oh, 0.0)
                if half != h // 4:
                    oh = pltpu.roll(oh, HEAD_DIM, 1)
                outs.append(oh)
            yb_slabs.append(outs[0] + outs[1])
        yb_n = _rms(jnp.concatenate(yb_slabs, axis=-1), gob_ref[...])

        cat = jnp.concatenate([ya_n, yb_n], axis=-1).astype(BF16)
        h1 = x + jnp.dot(cat, wo_ref[...], preferred_element_type=F32)
        xn2 = _rms(h1, gmoe_ref[...])
        gates, sel = _route(xn2, wr_ref, br_ref)
        h1_ref[...] = h1
        xn_ref[...] = xn2.astype(BF16)
        gm_ref[...] = gates
        sm_ref[...] = sel

        nk_ref[...] = pltpu.roll(ck_ref[...], nkeys - 1, 0)
        nv_ref[...] = pltpu.roll(cv_ref[...], nkeys - 1, 0)
        for b in range(t):
            nk_ref[b * CHUNK + CHUNK - 1:b * CHUNK + CHUNK, :] = knew[b:b + 1, :]
            nv_ref[b * CHUNK + CHUNK - 1:b * CHUNK + CHUNK, :] = vnew[b:b + 1, :]


def _sample_call(x, ck, cv, sinks, gmix, win, lng, lnb, w00, b0, goa, gob, wo, gmoe, wr, br, h1, xn, gm, sm):
    t = SAMPLE_TILE
    steps = DEC_BATCH // t
    base = N_PROMPT // t
    inrow = lambda i: (i, 0)
    outrow = lambda i: (base + i, 0)
    anyspec = pl.BlockSpec(memory_space=pl.ANY)
    return pl.pallas_call(
        _sample_kernel,
        grid=(steps,),
        in_specs=[
            pl.BlockSpec(memory_space=pltpu.SMEM),
            pl.BlockSpec((t, D_MODEL), inrow),
            pl.BlockSpec((t * CHUNK, KV_WIDTH), inrow),
            pl.BlockSpec((t * CHUNK, KV_WIDTH), inrow),
            _full((1, D_MODEL)), _full((D_MODEL, IN_WIDTH)), _full((1, A_WIDTH)), _full((1, A_WIDTH)),
            _full((1, A_WIDTH)), _full((1, A_WIDTH)), _full((1, A_WIDTH)), _full((1, B_WIDTH)),
            _full((D_MODEL, D_MODEL)), _full((1, D_MODEL)), _full((D_MODEL, LANES)), _full((1, LANES)),
            anyspec, anyspec, anyspec, anyspec,
        ],
        out_specs=[
            pl.BlockSpec((t, D_MODEL), outrow),
            pl.BlockSpec((t, D_MODEL), outrow),
            pl.BlockSpec((t, LANES), outrow),
            pl.BlockSpec((t, LANES), outrow),
            pl.BlockSpec((t * CHUNK, KV_WIDTH), inrow),
            pl.BlockSpec((t * CHUNK, KV_WIDTH), inrow),
            pl.BlockSpec((t, A_WIDTH), inrow),
        ],
        out_shape=[
            jax.ShapeDtypeStruct((N_PAD, D_MODEL), F32),
            jax.ShapeDtypeStruct((N_PAD, D_MODEL), BF16),
            jax.ShapeDtypeStruct((N_PAD, LANES), F32),
            jax.ShapeDtypeStruct((N_PAD, LANES), F32),
            jax.ShapeDtypeStruct((DEC_BATCH * CHUNK, KV_WIDTH), F32),
            jax.ShapeDtypeStruct((DEC_BATCH * CHUNK, KV_WIDTH), F32),
            jax.ShapeDtypeStruct((DEC_BATCH, A_WIDTH), F32),
        ],
        input_output_aliases={16: 0, 17: 1, 18: 2, 19: 3},
        compiler_params=pltpu.CompilerParams(
            dimension_semantics=("arbitrary",), vmem_limit_bytes=VMEM_LIMIT),
        name="sample_premoe",
    )(sinks, x, ck, cv, gmix, win, lng, lnb, w00, b0, goa, gob, wo, gmoe, wr, br, h1, xn, gm, sm)


def _plan_kernel(sm_ref,
                 destm_ref, destt_ref, xbe_ref, nxblk_ref, stab_ref, ctab_ref, astart_ref, nwin_ref,
                 base_s, pstart_s):
    ph = pl.program_id(0)
    i = pl.program_id(1)
    sel = sm_ref[...]
    cnt = jnp.sum(sel, axis=0, keepdims=True)
    lane = _lane_iota((1, LANES))

    @pl.when((ph == 0) & (i == 0))
    def _():
        base_s[...] = jnp.zeros_like(base_s)

    @pl.when(ph == 0)
    def _():
        base_s[...] += cnt

    @pl.when((ph == 1) & (i == 0))
    def _():
        counts = base_s[...]
        padded = jnp.floor((counts + (DISP_CHUNK + EXP_BLOCK - 1)) * (1.0 / EXP_BLOCK)) * EXP_BLOCK
        padded = jnp.where(counts > 0.0, padded, 0.0)
        pend = padded
        for s in (1, 2, 4, 8, 16):
            pend = pend + jnp.where(lane >= s, pltpu.roll(pend, s, 1), 0.0)
        pstart_s[...] = pend - padded
        base_s[...] = jnp.zeros_like(base_s)
        brow = lax.broadcasted_iota(jnp.int32, (XBE_ROWS, LANES), 0).astype(F32) * EXP_BLOCK
        done = jnp.where((lane < N_EXPERTS) & (pend <= brow), 1.0, 0.0)
        be = jnp.minimum(jnp.sum(done, axis=-1, keepdims=True), N_EXPERTS - 1.0)
        xbe_ref[...] = jnp.broadcast_to(be, (XBE_ROWS, LANES)).astype(jnp.int32)
        total = jnp.sum(jnp.where(lane == N_EXPERTS - 1, pend, 0.0), axis=-1, keepdims=True)
        nxblk_ref[...] = jnp.broadcast_to(total * (1.0 / EXP_BLOCK), (8, LANES)).astype(jnp.int32)
        stab_ref[...] = jnp.zeros_like(stab_ref)
        ctab_ref[...] = jnp.zeros_like(ctab_ref)
        astart_ref[...] = jnp.zeros_like(astart_ref)
        nwin_ref[...] = jnp.zeros_like(nwin_ref)

    @pl.when(ph == 1)
    def _():
        r = lax.broadcasted_iota(jnp.int32, (TOK_TILE, TOK_TILE), 0)
        c = lax.broadcasted_iota(jnp.int32, (TOK_TILE, TOK_TILE), 1)
        lower = jnp.where(c < r, 1.0, 0.0).astype(BF16)
        prefix = jnp.dot(lower, sel.astype(BF16), preferred_element_type=F32)
        start = pstart_s[...] + base_s[...]
        dest = jnp.where(sel > 0.0, prefix + start, -1.0)
        destm_ref[...] = dest
        destt_ref[...] = dest.T
        has = (cnt > 0.0) & (lane < N_EXPERTS)
        stab_ref[pl.ds(i, 1), :] = start.astype(jnp.int32)
        ctab_ref[pl.ds(i, 1), :] = jnp.where(has, cnt, 0.0).astype(jnp.int32)
        a = jnp.minimum(jnp.floor(start * (1.0 / WIN_ALIGN)) * WIN_ALIGN, float(N_ROWS - WIN))
        nw = jnp.where(has, jnp.floor((start + cnt - a + (WIN - 1)) * (1.0 / WIN)), 0.0)
        astart_ref[pl.ds(i, 1), :] = a.astype(jnp.int32)
        nwin_ref[pl.ds(i, 1), :] = nw.astype(jnp.int32)
        base_s[...] += cnt


def _plan_call(sm):
    tile = lambda ph, i: (i * ph, 0)
    tile_t = lambda ph, i: (0, i * ph)
    tab = jax.ShapeDtypeStruct((TAB_ROWS, LANES), jnp.int32)
    return pl.pallas_call(
        _plan_kernel,
        grid=(2, N_PAD_TILES),
        in_specs=[pl.BlockSpec((TOK_TILE, LANES), lambda ph, i: (i, 0))],
        out_specs=[
            pl.BlockSpec((TOK_TILE, LANES), tile),
            pl.BlockSpec((LANES, TOK_TILE), tile_t),
            _full((XBE_ROWS, LANES)), _full((8, LANES)),
            _full((TAB_ROWS, LANES)), _full((TAB_ROWS, LANES)), _full((TAB_ROWS, LANES)), _full((TAB_ROWS, LANES)),
        ],
        out_shape=[
            jax.ShapeDtypeStruct((N_PAD, LANES), F32),
            jax.ShapeDtypeStruct((LANES, N_PAD), F32),
            jax.ShapeDtypeStruct((XBE_ROWS, LANES), jnp.int32),
            jax.ShapeDtypeStruct((8, LANES), jnp.int32),
            tab, tab, tab, tab,
        ],
        scratch_shapes=[pltpu.VMEM((1, LANES), F32), pltpu.VMEM((1, LANES), F32)],
        compiler_params=pltpu.CompilerParams(
            dimension_semantics=("arbitrary", "arbitrary"), vmem_limit_bytes=VMEM_LIMIT),
        name="moe_plan",
    )(sm)


def _pack_rows(z):
    half = D_MODEL // 2
    lo = lax.bitcast_convert_type(z[:, :half], jnp.uint32) >> 16
    hi = lax.bitcast_convert_type(z[:, half:], jnp.uint32) & jnp.uint32(0xFFFF0000)
    return lax.bitcast_convert_type(hi | lo, jnp.int32)


def _unpack_rows(ref):
    lo, hi = [], []
    for s in range(PACK):
        w = lax.bitcast_convert_type(ref[:, s, :], jnp.uint32)
        lo.append(lax.bitcast_convert_type(w << 16, F32))
        hi.append(lax.bitcast_convert_type(w & jnp.uint32(0xFFFF0000), F32))
    return jnp.concatenate(lo + hi, axis=-1).astype(BF16)


def _dispatch_kernel(stab_ref, ctab_ref, cmax_ref, xn_ref, destt_ref, xs_in, xs_ref,
                     stage0, stage1, stage2, sems, sem2):
    del xs_in
    i = pl.program_id(0)
    last = pl.num_programs(0) - 1
    x = xn_ref[...]
    dt = destt_ref[...]
    rio = lax.broadcasted_iota(jnp.int32, (DISP_CHUNK, 1), 0).astype(F32)

    def chunk_rows(j, stage):
        parts = []
        for e in range(N_EXPERTS):
            first = (stab_ref[i * N_EXPERTS + e] + j * DISP_CHUNK).astype(F32)
            parts.append(jnp.where(dt[e:e + 1, :] == first + rio, 1.0, 0.0).astype(BF16))
        onehot = jnp.concatenate(parts, axis=0)
        words = _pack_rows(jnp.dot(onehot, x, preferred_element_type=F32))
        for s in range(PACK):
            stage[:, s, :] = words[:, s * LANES:(s + 1) * LANES]

    def copy(stage, step, e, j, sem):
        first = stab_ref[step * N_EXPERTS + e] + j * DISP_CHUNK
        return pltpu.make_async_copy(stage.at[pl.ds(e * DISP_CHUNK, DISP_CHUNK)],
                                     xs_ref.at[pl.ds(first, DISP_CHUNK)], sem)

    def step_body(stage, prev_stage, par):
        chunk_rows(0, stage)
        for e in range(N_EXPERTS):

            @pl.when((i > 0) & (ctab_ref[jnp.maximum(i - 1, 0) * N_EXPERTS + e] > 0))
            def _(e=e):
                copy(prev_stage, i - 1, e, 0, sems.at[1 - par, e]).wait()

        for e in range(N_EXPERTS):

            @pl.when(ctab_ref[i * N_EXPERTS + e] > 0)
            def _(e=e):
                copy(stage, i, e, 0, sems.at[par, e]).start()

        for e in range(N_EXPERTS):

            @pl.when((i == last) & (ctab_ref[i * N_EXPERTS + e] > 0))
            def _(e=e):
                copy(stage, i, e, 0, sems.at[par, e]).wait()

    @pl.when(i % 2 == 0)
    def _():
        step_body(stage0, stage1, 0)

    @pl.when(i % 2 == 1)
    def _():
        step_body(stage1, stage0, 1)

    for j in range(1, TOK_TILE // DISP_CHUNK):

        @pl.when(cmax_ref[i] > j * DISP_CHUNK)
        def _(j=j):
            chunk_rows(j, stage2)
            for e in range(N_EXPERTS):

                @pl.when(ctab_ref[i * N_EXPERTS + e] > j * DISP_CHUNK)
                def _(e=e):
                    cp = copy(stage2, i, e, j, sem2)
                    cp.start()
                    cp.wait()


def _dispatch_call(stab, ctab, cmax, xn, destt, xs_zero):
    stage = pltpu.VMEM((N_EXPERTS * DISP_CHUNK, PACK, LANES), jnp.int32)
    grid_spec = pltpu.PrefetchScalarGridSpec(
        num_scalar_prefetch=3,
        grid=(N_PAD_TILES,),
        in_specs=[
            pl.BlockSpec((TOK_TILE, D_MODEL), lambda i, *_: (i, 0)),
            pl.BlockSpec((N_EXPERTS, TOK_TILE), lambda i, *_: (0, i)),
            pl.BlockSpec(memory_space=pl.ANY),
        ],
        out_specs=pl.BlockSpec(memory_space=pl.ANY),
        scratch_shapes=[stage, stage, stage, pltpu.SemaphoreType.DMA((2, N_EXPERTS)),
                        pltpu.SemaphoreType.DMA],
    )
    return pl.pallas_call(
        _dispatch_kernel,
        grid_spec=grid_spec,
        out_shape=jax.ShapeDtypeStruct((N_ROWS, PACK, LANES), jnp.int32),
        input_output_aliases={5: 0},
        compiler_params=pltpu.CompilerParams(
            dimension_semantics=("arbitrary",), vmem_limit_bytes=VMEM_LIMIT),
        name="moe_dispatch",
    )(stab, ctab, cmax, xn, destt, xs_zero)


def _expert_kernel(blke_ref, nblk_ref, xs_ref, wgu_ref, bgu_ref, wdn_ref, bdn_ref,
                   ys_ref, wgu_s, wdn_s):
    b = pl.program_id(0)
    prev = blke_ref[jnp.maximum(b - 1, 0)]
    fresh = (b == 0) | (blke_ref[b] != prev)

    @pl.when(fresh)
    def _():
        wgu_s[...] = wgu_ref[...].astype(BF16)
        wdn_s[...] = wdn_ref[...].astype(BF16)

    @pl.when(b < nblk_ref[0])
    def _():
        hid = jnp.dot(_unpack_rows(xs_ref), wgu_s[...], preferred_element_type=F32) + bgu_ref[...]
        gate = jnp.minimum(hid[:, :D_FF], SWIGLU_LIMIT)
        up = jnp.clip(hid[:, D_FF:], -SWIGLU_LIMIT, SWIGLU_LIMIT)
        act = (up + 1.0) * gate * jax.nn.sigmoid(SWIGLU_ALPHA * gate)
        y = jnp.dot(act.astype(BF16), wdn_s[...], preferred_element_type=F32) + bdn_ref[...]
        ys_ref[...] = y.astype(BF16)

    @pl.when(b >= nblk_ref[0])
    def _():
        ys_ref[...] = jnp.zeros_like(ys_ref)


def _expert_call(blke, nblk, xs, wgu, bgu, wdn, bdn):
    grid_spec = pltpu.PrefetchScalarGridSpec(
        num_scalar_prefetch=2,
        grid=(N_XBLOCKS,),
        in_specs=[
            pl.BlockSpec((EXP_BLOCK, PACK, LANES), lambda b, be, nb: (b, 0, 0)),
            pl.BlockSpec((None, D_MODEL, 2 * D_FF), lambda b, be, nb: (be[b], 0, 0)),
            pl.BlockSpec((None, 1, 2 * D_FF), lambda b, be, nb: (be[b], 0, 0)),
            pl.BlockSpec((None, D_FF, D_MODEL), lambda b, be, nb: (be[b], 0, 0)),
            pl.BlockSpec((None, 1, D_MODEL), lambda b, be, nb: (be[b], 0, 0)),
        ],
        out_specs=pl.BlockSpec((EXP_BLOCK, D_MODEL), lambda b, be, nb: (b, 0)),
        scratch_shapes=[pltpu.VMEM((D_MODEL, 2 * D_FF), BF16), pltpu.VMEM((D_FF, D_MODEL), BF16)],
    )
    return pl.pallas_call(
        _expert_kernel,
        grid_spec=grid_spec,
        out_shape=jax.ShapeDtypeStruct((N_ROWS, D_MODEL), BF16),
        compiler_params=pltpu.CompilerParams(
            dimension_semantics=("arbitrary",), vmem_limit_bytes=VMEM_LIMIT),
        name="moe_experts",
    )(blke, nblk, xs, wgu, bgu, wdn, bdn)


def _combine_kernel(*refs):
    astart_ref, nwin_ref, over_ref = refs[0:3]
    win_refs = refs[3:3 + N_EXPERTS]
    (destm_ref, gm_ref, h1_ref, plep_ref, ples_ref, gple_ref, wpg_ref, wpp_ref, gfin_ref, ys_any,
     yp_ref, ysm_ref, moe_s, tmp_s, sem) = refs[3 + N_EXPERTS:]
    i = pl.program_id(0)
    dest = destm_ref[...]
    gates = gm_ref[...]
    lane = _lane_iota((TOK_TILE, LANES))
    lane_f = lane.astype(F32)
    lo = lane < WIN
    moe = jnp.zeros((TOK_TILE, D_MODEL), F32)
    group = 4
    for g0 in range(0, N_EXPERTS, group):
        g_hi, g_lo = [], []
        for p in range(group // 2):
            e0 = g0 + 2 * p
            a0 = astart_ref[i * N_EXPERTS + e0].astype(F32)
            a1 = astart_ref[i * N_EXPERTS + e0 + 1].astype(F32)
            rowid = jnp.where(lo, a0 + lane_f, a1 + lane_f - WIN)
            dcol = jnp.where(lo, dest[:, e0:e0 + 1], dest[:, e0 + 1:e0 + 2])
            gcol = jnp.where(lo, gates[:, e0:e0 + 1], gates[:, e0 + 1:e0 + 2])
            gsel = jnp.where(dcol == rowid, gcol, 0.0)
            hi = gsel.astype(BF16)
            g_hi.append(hi)
            g_lo.append((gsel - hi.astype(F32)).astype(BF16))
        ywin = jnp.concatenate([win_refs[g0 + q][...] for q in range(group)], axis=0)
        moe = moe + jnp.dot(jnp.concatenate(g_hi, axis=-1), ywin, preferred_element_type=F32)
        moe = moe + jnp.dot(jnp.concatenate(g_lo, axis=-1), ywin, preferred_element_type=F32)
    moe_s[...] = moe

    @pl.when(over_ref[i] > 0)
    def _():
        tmp_s[...] = jnp.zeros_like(tmp_s)

        def per_expert(e, carry):
            a = astart_ref[i * N_EXPERTS + e]
            dcol = jnp.sum(jnp.where(lane == e, dest, 0.0), axis=-1, keepdims=True)
            gcol = jnp.sum(jnp.where(lane == e, gates, 0.0), axis=-1, keepdims=True)

            def per_window(w, carry2):
                first = a + w * WIN
                start = pl.multiple_of(jnp.minimum(first, N_ROWS - WIN), WIN_ALIGN)
                cp = pltpu.make_async_copy(ys_any.at[pl.ds(start, WIN)], tmp_s.at[pl.ds(0, WIN)], sem)
                cp.start()
                cp.wait()
                hit = lo & (dcol == start.astype(F32) + lane_f) & (dcol >= first.astype(F32))
                gsel = jnp.where(hit, gcol, 0.0)
                hi = gsel.astype(BF16)
                rest = (gsel - hi.astype(F32)).astype(BF16)
                moe_s[...] += (jnp.dot(hi, tmp_s[...], preferred_element_type=F32)
                               + jnp.dot(rest, tmp_s[...], preferred_element_type=F32))
                return carry2

            return lax.fori_loop(1, nwin_ref[i * N_EXPERTS + e], per_window, carry)

        lax.fori_loop(0, N_EXPERTS, per_expert, 0)

    is_sample = i == N_TILES - 1
    h2 = h1_ref[...] + moe_s[...]
    ple = jnp.where(is_sample, ples_ref[...], plep_ref[...])
    hn = _rms(h2, gple_ref[...]).astype(BF16)
    gate = jax.nn.sigmoid(jnp.dot(hn, wpg_ref[...], preferred_element_type=F32))
    proj = jnp.dot(ple.astype(BF16), wpp_ref[...], preferred_element_type=F32)
    y = _rms(h2 + gate * proj, gfin_ref[...])

    @pl.when(jnp.logical_not(is_sample))
    def _():
        yp_ref[...] = y

    @pl.when(is_sample)
    def _():
        ysm_ref[...] = y


def _combine_call(astart, nwin, over, ys, destm, gm, h1, plep, ples, gple, wpg, wpp, gfin):
    last = N_TILES - 2

    def win_spec(e):
        return pl.BlockSpec((pl.Element(WIN), pl.Element(D_MODEL)),
                            lambda i, a, nw, ov, e=e: (pl.multiple_of(a[i * N_EXPERTS + e], WIN_ALIGN), 0))

    grid_spec = pltpu.PrefetchScalarGridSpec(
        num_scalar_prefetch=3,
        grid=(N_TILES,),
        in_specs=[win_spec(e) for e in range(N_EXPERTS)] + [
            pl.BlockSpec((TOK_TILE, LANES), lambda i, *_: (i, 0)),
            pl.BlockSpec((TOK_TILE, LANES), lambda i, *_: (i, 0)),
            pl.BlockSpec((TOK_TILE, D_MODEL), lambda i, *_: (i, 0)),
            pl.BlockSpec((TOK_TILE, PLE_DIM), lambda i, *_: (jnp.minimum(i, last), 0)),
            pl.BlockSpec((TOK_TILE, PLE_DIM), lambda i, *_: (0, 0)),
            pl.BlockSpec((1, D_MODEL), lambda i, *_: (0, 0)),
            pl.BlockSpec((D_MODEL, D_MODEL), lambda i, *_: (0, 0)),
            pl.BlockSpec((PLE_DIM, D_MODEL), lambda i, *_: (0, 0)),
            pl.BlockSpec((1, D_MODEL), lambda i, *_: (0, 0)),
            pl.BlockSpec(memory_space=pl.ANY),
        ],
        out_specs=[
            pl.BlockSpec((TOK_TILE, D_MODEL), lambda i, *_: (jnp.minimum(i, last), 0)),
            pl.BlockSpec((TOK_TILE, D_MODEL), lambda i, *_: (0, 0)),
        ],
        scratch_shapes=[pltpu.VMEM((TOK_TILE, D_MODEL), F32), pltpu.VMEM((2 * WIN, D_MODEL), BF16),
                        pltpu.SemaphoreType.DMA],
    )
    return pl.pallas_call(
        _combine_kernel,
        grid_spec=grid_spec,
        out_shape=[jax.ShapeDtypeStruct((N_PROMPT, D_MODEL), F32),
                   jax.ShapeDtypeStruct((DEC_BATCH, D_MODEL), F32)],
        compiler_params=pltpu.CompilerParams(
            dimension_semantics=("arbitrary",), vmem_limit_bytes=VMEM_LIMIT),
        name="moe_combine_tail",
    )(astart, nwin, over, *([ys] * N_EXPERTS), destm, gm, h1, plep, ples, gple, wpg, wpp, gfin, ys)


def kernel(x_prompt, x_sample, cache_swa_k, cache_swa_v, p_prompt, p_sample, g_mix, w_in, ln_v_g, ln_v_b,
           w_sp, b_sp, sinks, g_out_a, g_out_b, w_o, g_moe, w_router, b_router, w_gu, b_gu, w_dn, b_dn,
           g_ple, w_ple_gate, w_ple_proj, g_final):
    l = 0
    row = lambda v: v.reshape(1, -1)
    win = w_in[l].astype(BF16)
    wo = w_o[l].astype(BF16)
    tril = jnp.tril(jnp.ones((CHUNK, CHUNK), dtype=bool))
    wsp = jnp.where(tril, w_sp[l], 0.0).astype(BF16)
    bsp = jnp.repeat(b_sp[l].T, HEAD_DIM, axis=1)
    w00 = row(jnp.repeat(w_sp[l][:, 0, 0], HEAD_DIM))
    b0 = row(jnp.repeat(b_sp[l][:, 0], HEAD_DIM))
    wr_hi = w_router[l].astype(BF16)
    wr_lo = (w_router[l] - wr_hi.astype(F32)).astype(BF16)
    wr = jnp.concatenate([wr_hi, wr_lo, jnp.zeros((D_MODEL, LANES - 2 * N_EXPERTS), BF16)], axis=1)
    br = row(jnp.concatenate([b_router[l], jnp.zeros((LANES - N_EXPERTS,), F32)]))
    common = (row(g_mix[l]), win, row(ln_v_g[l]), row(ln_v_b[l]))
    tail = (row(g_out_a[l]), row(g_out_b[l]), wo, row(g_moe[l]), wr, br)

    h1, xn, gm, sm, k_p, v_p = _prompt_call(
        x_prompt.reshape(N_PROMPT, D_MODEL), sinks[l], *common, wsp, bsp, *tail)
    ck = cache_swa_k[l].reshape(DEC_BATCH * CHUNK, KV_WIDTH)
    cv = cache_swa_v[l].reshape(DEC_BATCH * CHUNK, KV_WIDTH)
    h1, xn, gm, sm, k_s, v_s, va_s = _sample_call(
        x_sample.reshape(DEC_BATCH, D_MODEL), ck, cv, sinks[l], *common, w00, b0, *tail, h1, xn, gm, sm)

    destm, destt, xbe, nxblk, stab, ctab, astart, nwin = _plan_call(sm)
    flat = lambda tab, n: tab[:n, :N_EXPERTS].reshape(-1)
    astart1 = flat(astart, N_TILES)
    nwin2 = nwin[:N_TILES, :N_EXPERTS]
    over1 = (jnp.max(nwin2, axis=1) > 1).astype(jnp.int32)
    nwin1 = nwin2.reshape(-1)
    cmax1 = jnp.max(ctab[:N_PAD_TILES, :N_EXPERTS], axis=1)

    xs = _dispatch_call(flat(stab, N_PAD_TILES), flat(ctab, N_PAD_TILES), cmax1, xn, destt[:N_EXPERTS],
                        jnp.zeros((N_ROWS, PACK, LANES), jnp.int32))
    ys = _expert_call(xbe[:N_XBLOCKS, 0], nxblk[0, :1], xs, w_gu[l], b_gu[l].reshape(N_EXPERTS, 1, 2 * D_FF),
                      w_dn[l], b_dn[l].reshape(N_EXPERTS, 1, D_MODEL))
    y_p, y_s = _combine_call(
        astart1, nwin1, over1, ys, destm, gm, h1,
        p_prompt[l].reshape(N_PROMPT, PLE_DIM), p_sample[l].reshape(DEC_BATCH, PLE_DIM),
        row(g_ple[l]), w_ple_gate[l].astype(BF16), w_ple_proj[l].astype(BF16), row(g_final))

    kv5 = lambda a, n: a.reshape(1, n, CHUNK, 2, HEAD_DIM)
    return (y_p.reshape(BATCH, SEQ, D_MODEL), y_s.reshape(DEC_BATCH, 1, D_MODEL),
            kv5(k_p, BATCH), kv5(v_p, BATCH), kv5(k_s, DEC_BATCH), kv5(v_s, DEC_BATCH),
            va_s.reshape(1, DEC_BATCH, 1, A_WIDTH))
```

```python
import math

import jax
import jax.numpy as jnp
from jax import lax
from jax.experimental import pallas as pl
from jax.experimental.pallas import tpu as pltpu

F32 = jnp.float32
BF16 = jnp.bfloat16

D_MODEL = 1024
BATCH = 4
SEQ = 4096
DEC_BATCH = 128
HEAD_DIM = 64
A_WIDTH = 512
B_WIDTH = 512
B_HEADS = 8
KV_WIDTH = 128
IN_WIDTH = 2 * A_WIDTH + B_WIDTH + 2 * KV_WIDTH
CHUNK = 128
N_EXPERTS = 32
TOP_K = 4
D_FF = 1024
SWIGLU_ALPHA = 1.702
SWIGLU_LIMIT = 7.0
PLE_DIM = 256
EPS = 1e-5

LANES = 128
ROW_BLOCK = 128
EXP_BLOCK = 512
N_PROMPT = BATCH * SEQ
N_TOK = N_PROMPT + DEC_BATCH
TOK_TILE = 128
N_TILES = N_TOK // TOK_TILE
DISP_TILE = 256
TM = 512
N_PAD = ((N_TOK + TM - 1) // TM) * TM
N_PAD_TILES = N_PAD // TOK_TILE
DISP_CHUNK = 32
N_XBLOCKS = (N_TOK * TOP_K + N_EXPERTS * (DISP_CHUNK + EXP_BLOCK - 1) + EXP_BLOCK - 1) // EXP_BLOCK
N_ROWS = N_XBLOCKS * EXP_BLOCK
XBE_ROWS = ((N_XBLOCKS + 7) // 8) * 8
TAB_ROWS = ((N_PAD_TILES + 7) // 8) * 8
PACK = D_MODEL // 2 // LANES
WIN = 64
WIN_ALIGN = 16
SAMPLE_TILE = 32
NEG = -1e30
VMEM_LIMIT = 56 * 1024 * 1024


def _rms(x, g):
    return x * lax.rsqrt(jnp.mean(x * x, axis=-1, keepdims=True) + EPS) * g


def _gelu(x):
    c = math.sqrt(2.0 / math.pi)
    return x * (0.5 * (1.0 + jnp.tanh(c * (x + 0.044715 * (x * x * x)))))


def _layernorm(x, g, b):
    mu = jnp.mean(x, axis=-1, keepdims=True)
    xc = x - mu
    return xc * lax.rsqrt(jnp.mean(xc * xc, axis=-1, keepdims=True) + EPS) * g + b


def _lane_iota(shape):
    return lax.broadcasted_iota(jnp.int32, shape, len(shape) - 1)


def _route(xn2, wr_ref, br_ref):
    m = xn2.shape[0]
    xh = xn2.astype(BF16)
    xl = (xn2 - xh.astype(F32)).astype(BF16)
    r = jnp.dot(jnp.concatenate([xh, xl], axis=0), wr_ref[...], preferred_element_type=F32)
    r = r[:m] + r[m:]
    lane = _lane_iota((m, LANES))
    lane_f = lane.astype(F32)
    logits = jnp.where(lane < N_EXPERTS, r + pltpu.roll(r, LANES - N_EXPERTS, 1) + br_ref[...], NEG)
    work = logits
    sel = jnp.zeros((m, LANES), F32)
    top = None
    z = None
    for _ in range(TOP_K):
        mx = jnp.max(work, axis=-1, keepdims=True)
        first = jnp.min(jnp.where(work == mx, lane_f, float(LANES)), axis=-1, keepdims=True)
        hit = lane_f == first
        sel = jnp.where(hit, 1.0, sel)
        work = jnp.where(hit, NEG, work)
        if top is None:
            top = mx
            z = jnp.ones_like(mx)
        else:
            z = z + jnp.exp(mx - top)
    gates = jnp.where(sel > 0.0, jnp.exp(logits - top) / z, 0.0)
    return gates, sel


def _prompt_kernel(sinks_ref, x_ref, gmix_ref, win_ref, lng_ref, lnb_ref, wsp_ref, bsp_ref,
                   goa_ref, gob_ref, wo_ref, gmoe_ref, wr_ref, br_ref,
                   h1_ref, xn_ref, gm_ref, sm_ref, k_ref, v_ref,
                   z_s, kv_s, cat_s):
    g = pl.program_id(0)
    j = g % (SEQ // TM)

    @pl.when(g >= N_PROMPT // TM)
    def _():
        h1_ref[...] = jnp.zeros_like(h1_ref)
        xn_ref[...] = jnp.zeros_like(xn_ref)
        gm_ref[...] = jnp.zeros_like(gm_ref)
        sm_ref[...] = jnp.zeros_like(sm_ref)

    @pl.when(g < N_PROMPT // TM)
    def _():
        _prompt_tile(j, sinks_ref, x_ref, gmix_ref, win_ref, lng_ref, lnb_ref, wsp_ref, bsp_ref,
                     goa_ref, gob_ref, wo_ref, gmoe_ref, wr_ref, br_ref,
                     h1_ref, xn_ref, gm_ref, sm_ref, k_ref, v_ref, z_s, kv_s, cat_s)


def _prompt_tile(j, sinks_ref, x_ref, gmix_ref, win_ref, lng_ref, lnb_ref, wsp_ref, bsp_ref,
                 goa_ref, gob_ref, wo_ref, gmoe_ref, wr_ref, br_ref,
                 h1_ref, xn_ref, gm_ref, sm_ref, k_ref, v_ref, z_s, kv_s, cat_s):
    @pl.when(j == 0)
    def _():
        kv_s[0:CHUNK, :] = jnp.zeros((CHUNK, 2 * KV_WIDTH), F32)

    xn = _rms(x_ref[...], gmix_ref[...]).astype(BF16)
    z_s[...] = jnp.dot(xn, win_ref[...], preferred_element_type=F32)
    kv_s[CHUNK:, :] = z_s[:, 2 * A_WIDTH + B_WIDTH:]

    lane = _lane_iota((CHUNK, LANES))
    lo = lane < HEAD_DIM
    lane2 = _lane_iota((2 * CHUNK, LANES))
    lo2 = lane2 < HEAD_DIM
    qi = lax.broadcasted_iota(jnp.int32, (CHUNK, 2 * CHUNK), 0)
    kj = lax.broadcasted_iota(jnp.int32, (CHUNK, 2 * CHUNK), 1)
    dist_i = qi + CHUNK - kj
    dist = dist_i.astype(F32)
    band = (dist_i >= 0) & (dist_i < CHUNK)

    def chunk_body(c, carry):
        r0 = pl.multiple_of(c * CHUNK, CHUNK)
        rows = pl.ds(r0, CHUNK)
        u = _gelu(z_s[rows, 0:A_WIDTH])
        va = _layernorm(_gelu(z_s[rows, A_WIDTH:2 * A_WIDTH]), lng_ref[...], lnb_ref[...])
        vab = va.astype(BF16)
        slabs = []
        for p in range(A_WIDTH // LANES):
            slab = vab[:, p * LANES:(p + 1) * LANES]
            m0 = jnp.dot(wsp_ref[2 * p], slab, preferred_element_type=F32)
            m1 = jnp.dot(wsp_ref[2 * p + 1], slab, preferred_element_type=F32)
            slabs.append(jnp.where(lo, m0, m1))
        ya = u * (jnp.concatenate(slabs, axis=-1) + bsp_ref[...])
        ya_n = _rms(ya, goa_ref[...])
        k2 = kv_s[pl.ds(r0, 2 * CHUNK), 0:KV_WIDTH]
        v2 = kv_s[pl.ds(r0, 2 * CHUNK), KV_WIDTH:2 * KV_WIDTH]
        k2r = pltpu.roll(k2, HEAD_DIM, 1)
        v2r = pltpu.roll(v2, HEAD_DIM, 1)
        kd = (jnp.where(lo2, k2, k2r).astype(BF16), jnp.where(lo2, k2r, k2).astype(BF16))
        vd = (jnp.where(lo2, v2, v2r).astype(BF16), jnp.where(lo2, v2r, v2).astype(BF16))
        prev_ok = (j > 0) | (c > 0)
        valid = band & ((kj >= CHUNK) | prev_ok)
        yb_slabs = []
        for kv in range(2):
            q0 = z_s[rows, 2 * A_WIDTH + (2 * kv) * LANES:2 * A_WIDTH + (2 * kv + 1) * LANES]
            q1 = z_s[rows, 2 * A_WIDTH + (2 * kv + 1) * LANES:2 * A_WIDTH + (2 * kv + 2) * LANES]
            lhs = jnp.concatenate([jnp.where(lo, q0, 0.0), jnp.where(lo, 0.0, q0),
                                   jnp.where(lo, q1, 0.0), jnp.where(lo, 0.0, q1)], axis=0).astype(BF16)
            s_all = lax.dot_general(lhs, kd[kv], (((1,), (1,)), ((), ())), preferred_element_type=F32)
            probs = []
            for i in range(4):
                h = 4 * kv + i
                slope = 2.0 ** (-(h + 1))
                sink = sinks_ref[h]
                s = s_all[i * CHUNK:(i + 1) * CHUNK] * (HEAD_DIM ** -0.5) - slope * dist
                s = jnp.where(valid, s, NEG)
                mx = jnp.maximum(jnp.max(s, axis=-1, keepdims=True), sink)
                e = jnp.exp(s - mx)
                den = jnp.sum(e, axis=-1, keepdims=True) + jnp.exp(sink - mx)
                probs.append(e * (1.0 / den))
            pm = jnp.concatenate(probs, axis=0).astype(BF16)
            o = jnp.dot(pm, vd[kv], preferred_element_type=F32)
            yb_slabs.append(jnp.where(lo, o[0:CHUNK], o[CHUNK:2 * CHUNK]))
            yb_slabs.append(jnp.where(lo, o[2 * CHUNK:3 * CHUNK], o[3 * CHUNK:4 * CHUNK]))
        yb_n = _rms(jnp.concatenate(yb_slabs, axis=-1), gob_ref[...])
        cat_s[rows, 0:A_WIDTH] = ya_n.astype(BF16)
        cat_s[rows, A_WIDTH:] = yb_n.astype(BF16)
        return carry

    lax.fori_loop(0, TM // CHUNK, chunk_body, 0)

    kv_s[0:CHUNK, :] = kv_s[TM:TM + CHUNK, :]
    k_ref[...] = kv_s[TM:TM + CHUNK, 0:KV_WIDTH]
    v_ref[...] = kv_s[TM:TM + CHUNK, KV_WIDTH:]

    h1 = x_ref[...] + jnp.dot(cat_s[...], wo_ref[...], preferred_element_type=F32)
    h1_ref[...] = h1
    xn2 = _rms(h1, gmoe_ref[...])
    xn_ref[...] = xn2.astype(BF16)
    gates, sel = _route(xn2, wr_ref, br_ref)
    gm_ref[...] = gates
    sm_ref[...] = sel


def _full(shape):
    n = len(shape)
    return pl.BlockSpec(shape, lambda *_: (0,) * n)


def _prompt_call(x, sinks, gmix, win, lng, lnb, wsp, bsp, goa, gob, wo, gmoe, wr, br):
    real = N_PROMPT // TM
    row = lambda g: (g, 0)
    seq = lambda g: (jnp.minimum(g, real - 1) // (SEQ // TM), 0, 0)
    return pl.pallas_call(
        _prompt_kernel,
        grid=(N_PAD // TM,),
        in_specs=[
            pl.BlockSpec(memory_space=pltpu.SMEM),
            pl.BlockSpec((TM, D_MODEL), lambda g: (jnp.minimum(g, real - 1), 0)),
            _full((1, D_MODEL)), _full((D_MODEL, IN_WIDTH)), _full((1, A_WIDTH)), _full((1, A_WIDTH)),
            _full((8, CHUNK, CHUNK)), _full((CHUNK, A_WIDTH)), _full((1, A_WIDTH)), _full((1, B_WIDTH)),
            _full((D_MODEL, D_MODEL)), _full((1, D_MODEL)), _full((D_MODEL, LANES)), _full((1, LANES)),
        ],
        out_specs=[
            pl.BlockSpec((TM, D_MODEL), row),
            pl.BlockSpec((TM, D_MODEL), row),
            pl.BlockSpec((TM, LANES), row),
            pl.BlockSpec((TM, LANES), row),
            pl.BlockSpec((None, CHUNK, KV_WIDTH), seq),
            pl.BlockSpec((None, CHUNK, KV_WIDTH), seq),
        ],
        out_shape=[
            jax.ShapeDtypeStruct((N_PAD, D_MODEL), F32),
            jax.ShapeDtypeStruct((N_PAD, D_MODEL), BF16),
            jax.ShapeDtypeStruct((N_PAD, LANES), F32),
            jax.ShapeDtypeStruct((N_PAD, LANES), F32),
            jax.ShapeDtypeStruct((BATCH, CHUNK, KV_WIDTH), F32),
            jax.ShapeDtypeStruct((BATCH, CHUNK, KV_WIDTH), F32),
        ],
        scratch_shapes=[
            pltpu.VMEM((TM, IN_WIDTH), F32),
            pltpu.VMEM((TM + CHUNK, 2 * KV_WIDTH), F32),
            pltpu.VMEM((TM, D_MODEL), BF16),
        ],
        compiler_params=pltpu.CompilerParams(
            dimension_semantics=("arbitrary",), vmem_limit_bytes=VMEM_LIMIT),
        name="prompt_premoe",
    )(sinks, x, gmix, win, lng, lnb, wsp, bsp, goa, gob, wo, gmoe, wr, br)


def _sample_kernel(sinks_ref, x_ref, ck_ref, cv_ref, gmix_ref, win_ref, lng_ref, lnb_ref, w00_ref, b0_ref,
                   goa_ref, gob_ref, wo_ref, gmoe_ref, wr_ref, br_ref,
                   h1_in, xn_in, gm_in, sm_in,
                   h1_ref, xn_ref, gm_ref, sm_ref, nk_ref, nv_ref, va_ref):
    del h1_in, xn_in, gm_in, sm_in
    t = SAMPLE_TILE
    nkeys = t * CHUNK

    if True:
        x = x_ref[...]
        xn = _rms(x, gmix_ref[...]).astype(BF16)
        z = jnp.dot(xn, win_ref[...], preferred_element_type=F32)
        u = _gelu(z[:, 0:A_WIDTH])
        va = _layernorm(_gelu(z[:, A_WIDTH:2 * A_WIDTH]), lng_ref[...], lnb_ref[...])
        va_ref[...] = va
        ya_n = _rms(u * (w00_ref[...] * va + b0_ref[...]), goa_ref[...])

        knew = z[:, 2 * A_WIDTH + B_WIDTH:2 * A_WIDTH + B_WIDTH + KV_WIDTH]
        vnew = z[:, 2 * A_WIDTH + B_WIDTH + KV_WIDTH:]
        lane = _lane_iota((t, LANES))
        lo = lane < HEAD_DIM
        stacked = []
        for h in range(B_HEADS):
            q = z[:, 2 * A_WIDTH + (h // 2) * LANES:2 * A_WIDTH + (h // 2 + 1) * LANES]
            qh = jnp.where(lo if h % 2 == 0 else jnp.logical_not(lo), q, 0.0)
            if h % 2 != h // 4:
                qh = pltpu.roll(qh, HEAD_DIM, 1)
            stacked.append(qh)
        qs = jnp.concatenate(stacked, axis=0)
        rows = B_HEADS * t
        ridx = lax.broadcasted_iota(jnp.int32, (rows, 1), 0)
        slope = jnp.zeros((rows, 1), F32)
        sink = jnp.zeros((rows, 1), F32)
        for h in range(B_HEADS):
            in_h = (ridx >= h * t) & (ridx < (h + 1) * t)
            slope = jnp.where(in_h, 2.0 ** (-(h + 1)), slope)
            sink = jnp.where(in_h, sinks_ref[h], sink)
        s_c = lax.dot_general(qs.astype(BF16), ck_ref[...].astype(BF16), (((1,), (1,)), ((), ())),
                              preferred_element_type=F32)
        rsamp = lax.broadcasted_iota(jnp.int32, (rows, nkeys), 0) % t
        col = lax.broadcasted_iota(jnp.int32, (rows, nkeys), 1)
        pos = col % CHUNK
        own = ((col // CHUNK) == rsamp) & (pos >= 1)
        s_c = s_c * (HEAD_DIM ** -0.5) - slope * (CHUNK - pos).astype(F32)
        s_c = jnp.where(own, s_c, NEG)
        kn8 = jnp.concatenate([knew] * B_HEADS, axis=0)
        vn8 = jnp.concatenate([vnew] * B_HEADS, axis=0)
        s_n = jnp.sum(qs * kn8, axis=-1, keepdims=True) * (HEAD_DIM ** -0.5)
        mx = jnp.maximum(jnp.maximum(jnp.max(s_c, axis=-1, keepdims=True), s_n), sink)
        e_c = jnp.exp(s_c - mx)
        e_n = jnp.exp(s_n - mx)
        inv = 1.0 / (jnp.sum(e_c, axis=-1, keepdims=True) + e_n + jnp.exp(sink - mx))
        o = jnp.dot((e_c * inv).astype(BF16), cv_ref[...].astype(BF16), preferred_element_type=F32)
        o = o + (e_n * inv) * vn8
        yb_slabs = []
        for p in range(B_WIDTH // LANES):
            outs = []
            for half in range(2):
                h = 2 * p + half
                oh = o[h * t:(h + 1) * t]
                oh = jnp.where(lo if h // 4 == 0 else jnp.logical_not(lo), oh, 0.0)
                if half != h // 4:
                    oh = pltpu.roll(oh, HEAD_DIM, 1)
                outs.append(oh)
            yb_slabs.append(outs[0] + outs[1])
        yb_n = _rms(jnp.concatenate(yb_slabs, axis=-1), gob_ref[...])

        cat = jnp.concatenate([ya_n, yb_n], axis=-1).astype(BF16)
        h1 = x + jnp.dot(cat, wo_ref[...], preferred_element_type=F32)
        xn2 = _rms(h1, gmoe_ref[...])
        gates, sel = _route(xn2, wr_ref, br_ref)
        h1_ref[...] = h1
        xn_ref[...] = xn2.astype(BF16)
        gm_ref[...] = gates
        sm_ref[...] = sel

        nk_ref[...] = pltpu.roll(ck_ref[...], nkeys - 1, 0)
        nv_ref[...] = pltpu.roll(cv_ref[...], nkeys - 1, 0)
        for b in range(t):
            nk_ref[b * CHUNK + CHUNK - 1:b * CHUNK + CHUNK, :] = knew[b:b + 1, :]
            nv_ref[b * CHUNK + CHUNK - 1:b * CHUNK + CHUNK, :] = vnew[b:b + 1, :]


def _sample_call(x, ck, cv, sinks, gmix, win, lng, lnb, w00, b0, goa, gob, wo, gmoe, wr, br, h1, xn, gm, sm):
    t = SAMPLE_TILE
    steps = DEC_BATCH // t
    base = N_PROMPT // t
    inrow = lambda i: (i, 0)
    outrow = lambda i: (base + i, 0)
    anyspec = pl.BlockSpec(memory_space=pl.ANY)
    return pl.pallas_call(
        _sample_kernel,
        grid=(steps,),
        in_specs=[
            pl.BlockSpec(memory_space=pltpu.SMEM),
            pl.BlockSpec((t, D_MODEL), inrow),
            pl.BlockSpec((t * CHUNK, KV_WIDTH), inrow),
            pl.BlockSpec((t * CHUNK, KV_WIDTH), inrow),
            _full((1, D_MODEL)), _full((D_MODEL, IN_WIDTH)), _full((1, A_WIDTH)), _full((1, A_WIDTH)),
            _full((1, A_WIDTH)), _full((1, A_WIDTH)), _full((1, A_WIDTH)), _full((1, B_WIDTH)),
            _full((D_MODEL, D_MODEL)), _full((1, D_MODEL)), _full((D_MODEL, LANES)), _full((1, LANES)),
            anyspec, anyspec, anyspec, anyspec,
        ],
        out_specs=[
            pl.BlockSpec((t, D_MODEL), outrow),
            pl.BlockSpec((t, D_MODEL), outrow),
            pl.BlockSpec((t, LANES), outrow),
            pl.BlockSpec((t, LANES), outrow),
            pl.BlockSpec((t * CHUNK, KV_WIDTH), inrow),
            pl.BlockSpec((t * CHUNK, KV_WIDTH), inrow),
            pl.BlockSpec((t, A_WIDTH), inrow),
        ],
        out_shape=[
            jax.ShapeDtypeStruct((N_PAD, D_MODEL), F32),
            jax.ShapeDtypeStruct((N_PAD, D_MODEL), BF16),
            jax.ShapeDtypeStruct((N_PAD, LANES), F32),
            jax.ShapeDtypeStruct((N_PAD, LANES), F32),
            jax.ShapeDtypeStruct((DEC_BATCH * CHUNK, KV_WIDTH), F32),
            jax.ShapeDtypeStruct((DEC_BATCH * CHUNK, KV_WIDTH), F32),
            jax.ShapeDtypeStruct((DEC_BATCH, A_WIDTH), F32),
        ],
        input_output_aliases={16: 0, 17: 1, 18: 2, 19: 3},
        compiler_params=pltpu.CompilerParams(
            dimension_semantics=("arbitrary",), vmem_limit_bytes=VMEM_LIMIT),
        name="sample_premoe",
    )(sinks, x, ck, cv, gmix, win, lng, lnb, w00, b0, goa, gob, wo, gmoe, wr, br, h1, xn, gm, sm)


def _plan_kernel(sm_ref,
                 destm_ref, destt_ref, xbe_ref, nxblk_ref, stab_ref, ctab_ref, astart_ref, nwin_ref,
                 base_s, pstart_s):
    ph = pl.program_id(0)
    i = pl.program_id(1)
    sel = sm_ref[...]
    cnt = jnp.sum(sel, axis=0, keepdims=True)
    lane = _lane_iota((1, LANES))

    @pl.when((ph == 0) & (i == 0))
    def _():
        base_s[...] = jnp.zeros_like(base_s)

    @pl.when(ph == 0)
    def _():
        base_s[...] += cnt

    @pl.when((ph == 1) & (i == 0))
    def _():
        counts = base_s[...]
        padded = jnp.floor((counts + (DISP_CHUNK + EXP_BLOCK - 1)) * (1.0 / EXP_BLOCK)) * EXP_BLOCK
        padded = jnp.where(counts > 0.0, padded, 0.0)
        pend = padded
        for s in (1, 2, 4, 8, 16):
            pend = pend + jnp.where(lane >= s, pltpu.roll(pend, s, 1), 0.0)
        pstart_s[...] = pend - padded
        base_s[...] = jnp.zeros_like(base_s)
        brow = lax.broadcasted_iota(jnp.int32, (XBE_ROWS, LANES), 0).astype(F32) * EXP_BLOCK
        done = jnp.where((lane < N_EXPERTS) & (pend <= brow), 1.0, 0.0)
        be = jnp.minimum(jnp.sum(done, axis=-1, keepdims=True), N_EXPERTS - 1.0)
        xbe_ref[...] = jnp.broadcast_to(be, (XBE_ROWS, LANES)).astype(jnp.int32)
        total = jnp.sum(jnp.where(lane == N_EXPERTS - 1, pend, 0.0), axis=-1, keepdims=True)
        nxblk_ref[...] = jnp.broadcast_to(total * (1.0 / EXP_BLOCK), (8, LANES)).astype(jnp.int32)
        stab_ref[...] = jnp.zeros_like(stab_ref)
        ctab_ref[...] = jnp.zeros_like(ctab_ref)
        astart_ref[...] = jnp.zeros_like(astart_ref)
        nwin_ref[...] = jnp.zeros_like(nwin_ref)

    @pl.when(ph == 1)
    def _():
        r = lax.broadcasted_iota(jnp.int32, (TOK_TILE, TOK_TILE), 0)
        c = lax.broadcasted_iota(jnp.int32, (TOK_TILE, TOK_TILE), 1)
        lower = jnp.where(c < r, 1.0, 0.0).astype(BF16)
        prefix = jnp.dot(lower, sel.astype(BF16), preferred_element_type=F32)
        start = pstart_s[...] + base_s[...]
        dest = jnp.where(sel > 0.0, prefix + start, -1.0)
        destm_ref[...] = dest
        destt_ref[...] = dest.T
        has = (cnt > 0.0) & (lane < N_EXPERTS)
        stab_ref[pl.ds(i, 1), :] = start.astype(jnp.int32)
        ctab_ref[pl.ds(i, 1), :] = jnp.where(has, cnt, 0.0).astype(jnp.int32)
        a = jnp.minimum(jnp.floor(start * (1.0 / WIN_ALIGN)) * WIN_ALIGN, float(N_ROWS - WIN))
        nw = jnp.where(has, jnp.floor((start + cnt - a + (WIN - 1)) * (1.0 / WIN)), 0.0)
        astart_ref[pl.ds(i, 1), :] = a.astype(jnp.int32)
        nwin_ref[pl.ds(i, 1), :] = nw.astype(jnp.int32)
        base_s[...] += cnt


def _plan_call(sm):
    tile = lambda ph, i: (i * ph, 0)
    tile_t = lambda ph, i: (0, i * ph)
    tab = jax.ShapeDtypeStruct((TAB_ROWS, LANES), jnp.int32)
    return pl.pallas_call(
        _plan_kernel,
        grid=(2, N_PAD_TILES),
        in_specs=[pl.BlockSpec((TOK_TILE, LANES), lambda ph, i: (i, 0))],
        out_specs=[
            pl.BlockSpec((TOK_TILE, LANES), tile),
            pl.BlockSpec((LANES, TOK_TILE), tile_t),
            _full((XBE_ROWS, LANES)), _full((8, LANES)),
            _full((TAB_ROWS, LANES)), _full((TAB_ROWS, LANES)), _full((TAB_ROWS, LANES)), _full((TAB_ROWS, LANES)),
        ],
        out_shape=[
            jax.ShapeDtypeStruct((N_PAD, LANES), F32),
            jax.ShapeDtypeStruct((LANES, N_PAD), F32),
            jax.ShapeDtypeStruct((XBE_ROWS, LANES), jnp.int32),
            jax.ShapeDtypeStruct((8, LANES), jnp.int32),
            tab, tab, tab, tab,
        ],
        scratch_shapes=[pltpu.VMEM((1, LANES), F32), pltpu.VMEM((1, LANES), F32)],
        compiler_params=pltpu.CompilerParams(
            dimension_semantics=("arbitrary", "arbitrary"), vmem_limit_bytes=VMEM_LIMIT),
        name="moe_plan",
    )(sm)


def _pack_rows(z):
    half = D_MODEL // 2
    lo = lax.bitcast_convert_type(z[:, :half], jnp.uint32) >> 16
    hi = lax.bitcast_convert_type(z[:, half:], jnp.uint32) & jnp.uint32(0xFFFF0000)
    return lax.bitcast_convert_type(hi | lo, jnp.int32)


def _unpack_rows(ref):
    rows = ref.shape[0]
    flat = ref.reshape(rows * PACK, LANES)
    lo, hi = [], []
    for s in range(PACK):
        w = lax.bitcast_convert_type(flat[pl.ds(s, rows, stride=PACK), :], jnp.uint32)
        lo.append(lax.bitcast_convert_type(w << 16, F32))
        hi.append(lax.bitcast_convert_type(w & jnp.uint32(0xFFFF0000), F32))
    return jnp.concatenate(lo + hi, axis=-1).astype(BF16)


def _dispatch_kernel(stab_ref, ctab_ref, cmax_ref, xn_ref, destt_ref, xs_in, xs_ref,
                     stage0, stage1, stage2, sems, sem2):
    del xs_in
    i = pl.program_id(0)
    last = pl.num_programs(0) - 1
    x = xn_ref[...]
    dt = destt_ref[...]
    rio = lax.broadcasted_iota(jnp.int32, (DISP_CHUNK, 1), 0).astype(F32)

    def chunk_rows(j, stage):
        parts = []
        for e in range(N_EXPERTS):
            first = (stab_ref[i * N_EXPERTS + e] + j * DISP_CHUNK).astype(F32)
            parts.append(jnp.where(dt[e:e + 1, :] == first + rio, 1.0, 0.0).astype(BF16))
        onehot = jnp.concatenate(parts, axis=0)
        words = _pack_rows(jnp.dot(onehot, x, preferred_element_type=F32))
        for s in range(PACK):
            stage[pl.ds(s, N_EXPERTS * DISP_CHUNK, stride=PACK), :] = words[:, s * LANES:(s + 1) * LANES]

    def copy(stage, step, e, j, sem):
        first = stab_ref[step * N_EXPERTS + e] + j * DISP_CHUNK
        rows = stage.reshape(N_EXPERTS * DISP_CHUNK, PACK, LANES)
        return pltpu.make_async_copy(rows.at[pl.ds(e * DISP_CHUNK, DISP_CHUNK)],
                                     xs_ref.at[pl.ds(first, DISP_CHUNK)], sem)

    def step_body(stage, prev_stage, par):
        chunk_rows(0, stage)
        for e in range(N_EXPERTS):

            @pl.when((i > 0) & (ctab_ref[jnp.maximum(i - 1, 0) * N_EXPERTS + e] > 0))
            def _(e=e):
                copy(prev_stage, i - 1, e, 0, sems.at[1 - par, e]).wait()

        for e in range(N_EXPERTS):

            @pl.when(ctab_ref[i * N_EXPERTS + e] > 0)
            def _(e=e):
                copy(stage, i, e, 0, sems.at[par, e]).start()

        for e in range(N_EXPERTS):

            @pl.when((i == last) & (ctab_ref[i * N_EXPERTS + e] > 0))
            def _(e=e):
                copy(stage, i, e, 0, sems.at[par, e]).wait()

    @pl.when(i % 2 == 0)
    def _():
        step_body(stage0, stage1, 0)

    @pl.when(i % 2 == 1)
    def _():
        step_body(stage1, stage0, 1)

    for j in range(1, TOK_TILE // DISP_CHUNK):

        @pl.when(cmax_ref[i] > j * DISP_CHUNK)
        def _(j=j):
            chunk_rows(j, stage2)
            for e in range(N_EXPERTS):

                @pl.when(ctab_ref[i * N_EXPERTS + e] > j * DISP_CHUNK)
                def _(e=e):
                    cp = copy(stage2, i, e, j, sem2)
                    cp.start()
                    cp.wait()


def _dispatch_call(stab, ctab, cmax, xn, destt, xs_zero):
    stage = pltpu.VMEM((N_EXPERTS * DISP_CHUNK * PACK, LANES), jnp.int32)
    grid_spec = pltpu.PrefetchScalarGridSpec(
        num_scalar_prefetch=3,
        grid=(N_PAD_TILES,),
        in_specs=[
            pl.BlockSpec((TOK_TILE, D_MODEL), lambda i, *_: (i, 0)),
            pl.BlockSpec((N_EXPERTS, TOK_TILE), lambda i, *_: (0, i)),
            pl.BlockSpec(memory_space=pl.ANY),
        ],
        out_specs=pl.BlockSpec(memory_space=pl.ANY),
        scratch_shapes=[stage, stage, stage, pltpu.SemaphoreType.DMA((2, N_EXPERTS)),
                        pltpu.SemaphoreType.DMA],
    )
    return pl.pallas_call(
        _dispatch_kernel,
        grid_spec=grid_spec,
        out_shape=jax.ShapeDtypeStruct((N_ROWS, PACK, LANES), jnp.int32),
        input_output_aliases={5: 0},
        compiler_params=pltpu.CompilerParams(
            dimension_semantics=("arbitrary",), vmem_limit_bytes=VMEM_LIMIT),
        name="moe_dispatch",
    )(stab, ctab, cmax, xn, destt, xs_zero)


def _expert_kernel(blke_ref, nblk_ref, xs_ref, wgu_ref, bgu_ref, wdn_ref, bdn_ref,
                   ys_ref, wgu_s, wdn_s):
    b = pl.program_id(0)
    prev = blke_ref[jnp.maximum(b - 1, 0)]
    fresh = (b == 0) | (blke_ref[b] != prev)

    @pl.when(fresh)
    def _():
        wgu_s[...] = wgu_ref[...].astype(BF16)
        wdn_s[...] = wdn_ref[...].astype(BF16)

    @pl.when(b < nblk_ref[0])
    def _():
        hid = jnp.dot(_unpack_rows(xs_ref), wgu_s[...], preferred_element_type=F32) + bgu_ref[...]
        gate = jnp.minimum(hid[:, :D_FF], SWIGLU_LIMIT)
        up = jnp.clip(hid[:, D_FF:], -SWIGLU_LIMIT, SWIGLU_LIMIT)
        act = (up + 1.0) * gate * jax.nn.sigmoid(SWIGLU_ALPHA * gate)
        y = jnp.dot(act.astype(BF16), wdn_s[...], preferred_element_type=F32) + bdn_ref[...]
        ys_ref[...] = y.astype(BF16)

    @pl.when(b >= nblk_ref[0])
    def _():
        ys_ref[...] = jnp.zeros_like(ys_ref)


def _expert_call(blke, nblk, xs, wgu, bgu, wdn, bdn):
    grid_spec = pltpu.PrefetchScalarGridSpec(
        num_scalar_prefetch=2,
        grid=(N_XBLOCKS,),
        in_specs=[
            pl.BlockSpec((EXP_BLOCK, PACK, LANES), lambda b, be, nb: (b, 0, 0)),
            pl.BlockSpec((None, D_MODEL, 2 * D_FF), lambda b, be, nb: (be[b], 0, 0)),
            pl.BlockSpec((None, 1, 2 * D_FF), lambda b, be, nb: (be[b], 0, 0)),
            pl.BlockSpec((None, D_FF, D_MODEL), lambda b, be, nb: (be[b], 0, 0)),
            pl.BlockSpec((None, 1, D_MODEL), lambda b, be, nb: (be[b], 0, 0)),
        ],
        out_specs=pl.BlockSpec((EXP_BLOCK, D_MODEL), lambda b, be, nb: (b, 0)),
        scratch_shapes=[pltpu.VMEM((D_MODEL, 2 * D_FF), BF16), pltpu.VMEM((D_FF, D_MODEL), BF16)],
    )
    return pl.pallas_call(
        _expert_kernel,
        grid_spec=grid_spec,
        out_shape=jax.ShapeDtypeStruct((N_ROWS, D_MODEL), BF16),
        compiler_params=pltpu.CompilerParams(
            dimension_semantics=("arbitrary",), vmem_limit_bytes=VMEM_LIMIT),
        name="moe_experts",
    )(blke, nblk, xs, wgu, bgu, wdn, bdn)


def _combine_kernel(*refs):
    astart_ref, nwin_ref, over_ref = refs[0:3]
    win_refs = refs[3:3 + N_EXPERTS]
    (destm_ref, gm_ref, h1_ref, plep_ref, ples_ref, gple_ref, wpg_ref, wpp_ref, gfin_ref, ys_any,
     yp_ref, ysm_ref, moe_s, tmp_s, sem) = refs[3 + N_EXPERTS:]
    i = pl.program_id(0)
    dest = destm_ref[...]
    gates = gm_ref[...]
    lane = _lane_iota((TOK_TILE, LANES))
    lane_f = lane.astype(F32)
    lo = lane < WIN
    moe = jnp.zeros((TOK_TILE, D_MODEL), F32)
    group = 4
    for g0 in range(0, N_EXPERTS, group):
        g_hi, g_lo = [], []
        for p in range(group // 2):
            e0 = g0 + 2 * p
            a0 = astart_ref[i * N_EXPERTS + e0].astype(F32)
            a1 = astart_ref[i * N_EXPERTS + e0 + 1].astype(F32)
            rowid = jnp.where(lo, a0 + lane_f, a1 + lane_f - WIN)
            dcol = jnp.where(lo, dest[:, e0:e0 + 1], dest[:, e0 + 1:e0 + 2])
            gcol = jnp.where(lo, gates[:, e0:e0 + 1], gates[:, e0 + 1:e0 + 2])
            gsel = jnp.where(dcol == rowid, gcol, 0.0)
            hi = gsel.astype(BF16)
            g_hi.append(hi)
            g_lo.append((gsel - hi.astype(F32)).astype(BF16))
        ywin = jnp.concatenate([win_refs[g0 + q][...] for q in range(group)], axis=0)
        moe = moe + jnp.dot(jnp.concatenate(g_hi, axis=-1), ywin, preferred_element_type=F32)
        moe = moe + jnp.dot(jnp.concatenate(g_lo, axis=-1), ywin, preferred_element_type=F32)
    moe_s[...] = moe

    @pl.when(over_ref[i] > 0)
    def _():
        tmp_s[...] = jnp.zeros_like(tmp_s)

        def per_expert(e, carry):
            a = astart_ref[i * N_EXPERTS + e]
            dcol = jnp.sum(jnp.where(lane == e, dest, 0.0), axis=-1, keepdims=True)
            gcol = jnp.sum(jnp.where(lane == e, gates, 0.0), axis=-1, keepdims=True)

            def per_window(w, carry2):
                first = a + w * WIN
                start = pl.multiple_of(jnp.minimum(first, N_ROWS - WIN), WIN_ALIGN)
                cp = pltpu.make_async_copy(ys_any.at[pl.ds(start, WIN)], tmp_s.at[pl.ds(0, WIN)], sem)
                cp.start()
                cp.wait()
                hit = lo & (dcol == start.astype(F32) + lane_f) & (dcol >= first.astype(F32))
                gsel = jnp.where(hit, gcol, 0.0)
                hi = gsel.astype(BF16)
                rest = (gsel - hi.astype(F32)).astype(BF16)
                moe_s[...] += (jnp.dot(hi, tmp_s[...], preferred_element_type=F32)
                               + jnp.dot(rest, tmp_s[...], preferred_element_type=F32))
                return carry2

            return lax.fori_loop(1, nwin_ref[i * N_EXPERTS + e], per_window, carry)

        lax.fori_loop(0, N_EXPERTS, per_expert, 0)

    is_sample = i == N_TILES - 1
    h2 = h1_ref[...] + moe_s[...]
    ple = jnp.where(is_sample, ples_ref[...], plep_ref[...])
    hn = _rms(h2, gple_ref[...]).astype(BF16)
    gate = jax.nn.sigmoid(jnp.dot(hn, wpg_ref[...], preferred_element_type=F32))
    proj = jnp.dot(ple.astype(BF16), wpp_ref[...], preferred_element_type=F32)
    y = _rms(h2 + gate * proj, gfin_ref[...])

    @pl.when(jnp.logical_not(is_sample))
    def _():
        yp_ref[...] = y

    @pl.when(is_sample)
    def _():
        ysm_ref[...] = y


def _combine_call(astart, nwin, over, ys, destm, gm, h1, plep, ples, gple, wpg, wpp, gfin):
    last = N_TILES - 2

    def win_spec(e):
        return pl.BlockSpec((pl.Element(WIN), pl.Element(D_MODEL)),
                            lambda i, a, nw, ov, e=e: (pl.multiple_of(a[i * N_EXPERTS + e], WIN_ALIGN), 0))

    grid_spec = pltpu.PrefetchScalarGridSpec(
        num_scalar_prefetch=3,
        grid=(N_TILES,),
        in_specs=[win_spec(e) for e in range(N_EXPERTS)] + [
            pl.BlockSpec((TOK_TILE, LANES), lambda i, *_: (i, 0)),
            pl.BlockSpec((TOK_TILE, LANES), lambda i, *_: (i, 0)),
            pl.BlockSpec((TOK_TILE, D_MODEL), lambda i, *_: (i, 0)),
            pl.BlockSpec((TOK_TILE, PLE_DIM), lambda i, *_: (jnp.minimum(i, last), 0)),
            pl.BlockSpec((TOK_TILE, PLE_DIM), lambda i, *_: (0, 0)),
            pl.BlockSpec((1, D_MODEL), lambda i, *_: (0, 0)),
            pl.BlockSpec((D_MODEL, D_MODEL), lambda i, *_: (0, 0)),
            pl.BlockSpec((PLE_DIM, D_MODEL), lambda i, *_: (0, 0)),
            pl.BlockSpec((1, D_MODEL), lambda i, *_: (0, 0)),
            pl.BlockSpec(memory_space=pl.ANY),
        ],
        out_specs=[
            pl.BlockSpec((TOK_TILE, D_MODEL), lambda i, *_: (jnp.minimum(i, last), 0)),
            pl.BlockSpec((TOK_TILE, D_MODEL), lambda i, *_: (0, 0)),
        ],
        scratch_shapes=[pltpu.VMEM((TOK_TILE, D_MODEL), F32), pltpu.VMEM((2 * WIN, D_MODEL), BF16),
                        pltpu.SemaphoreType.DMA],
    )
    return pl.pallas_call(
        _combine_kernel,
        grid_spec=grid_spec,
        out_shape=[jax.ShapeDtypeStruct((N_PROMPT, D_MODEL), F32),
                   jax.ShapeDtypeStruct((DEC_BATCH, D_MODEL), F32)],
        compiler_params=pltpu.CompilerParams(
            dimension_semantics=("arbitrary",), vmem_limit_bytes=VMEM_LIMIT),
        name="moe_combine_tail",
    )(astart, nwin, over, *([ys] * N_EXPERTS), destm, gm, h1, plep, ples, gple, wpg, wpp, gfin, ys)


def kernel(x_prompt, x_sample, cache_swa_k, cache_swa_v, p_prompt, p_sample, g_mix, w_in, ln_v_g, ln_v_b,
           w_sp, b_sp, sinks, g_out_a, g_out_b, w_o, g_moe, w_router, b_router, w_gu, b_gu, w_dn, b_dn,
           g_ple, w_ple_gate, w_ple_proj, g_final):
    l = 0
    row = lambda v: v.reshape(1, -1)
    win = w_in[l].astype(BF16)
    wo = w_o[l].astype(BF16)
    tril = jnp.tril(jnp.ones((CHUNK, CHUNK), dtype=bool))
    wsp = jnp.where(tril, w_sp[l], 0.0).astype(BF16)
    bsp = jnp.repeat(b_sp[l].T, HEAD_DIM, axis=1)
    w00 = row(jnp.repeat(w_sp[l][:, 0, 0], HEAD_DIM))
    b0 = row(jnp.repeat(b_sp[l][:, 0], HEAD_DIM))
    wr_hi = w_router[l].astype(BF16)
    wr_lo = (w_router[l] - wr_hi.astype(F32)).astype(BF16)
    wr = jnp.concatenate([wr_hi, wr_lo, jnp.zeros((D_MODEL, LANES - 2 * N_EXPERTS), BF16)], axis=1)
    br = row(jnp.concatenate([b_router[l], jnp.zeros((LANES - N_EXPERTS,), F32)]))
    common = (row(g_mix[l]), win, row(ln_v_g[l]), row(ln_v_b[l]))
    tail = (row(g_out_a[l]), row(g_out_b[l]), wo, row(g_moe[l]), wr, br)

    h1, xn, gm, sm, k_p, v_p = _prompt_call(
        x_prompt.reshape(N_PROMPT, D_MODEL), sinks[l], *common, wsp, bsp, *tail)
    ck = cache_swa_k[l].reshape(DEC_BATCH * CHUNK, KV_WIDTH)
    cv = cache_swa_v[l].reshape(DEC_BATCH * CHUNK, KV_WIDTH)
    h1, xn, gm, sm, k_s, v_s, va_s = _sample_call(
        x_sample.reshape(DEC_BATCH, D_MODEL), ck, cv, sinks[l], *common, w00, b0, *tail, h1, xn, gm, sm)

    destm, destt, xbe, nxblk, stab, ctab, astart, nwin = _plan_call(sm)
    flat = lambda tab, n: tab[:n, :N_EXPERTS].reshape(-1)
    astart1 = flat(astart, N_TILES)
    nwin2 = nwin[:N_TILES, :N_EXPERTS]
    over1 = (jnp.max(nwin2, axis=1) > 1).astype(jnp.int32)
    nwin1 = nwin2.reshape(-1)
    cmax1 = jnp.max(ctab[:N_PAD_TILES, :N_EXPERTS], axis=1)

    xs = _dispatch_call(flat(stab, N_PAD_TILES), flat(ctab, N_PAD_TILES), cmax1, xn, destt[:N_EXPERTS],
                        jnp.zeros((N_ROWS, PACK, LANES), jnp.int32))
    ys = _expert_call(xbe[:N_XBLOCKS, 0], nxblk[0, :1], xs, w_gu[l], b_gu[l].reshape(N_EXPERTS, 1, 2 * D_FF),
                      w_dn[l], b_dn[l].reshape(N_EXPERTS, 1, D_MODEL))
    y_p, y_s = _combine_call(
        astart1, nwin1, over1, ys, destm, gm, h1,
        p_prompt[l].reshape(N_PROMPT, PLE_DIM), p_sample[l].reshape(DEC_BATCH, PLE_DIM),
        row(g_ple[l]), w_ple_gate[l].astype(BF16), w_ple_proj[l].astype(BF16), row(g_final))

    kv5 = lambda a, n: a.reshape(1, n, CHUNK, 2, HEAD_DIM)
    return (y_p.reshape(BATCH, SEQ, D_MODEL), y_s.reshape(DEC_BATCH, 1, D_MODEL),
            kv5(k_p, BATCH), kv5(v_p, BATCH), kv5(k_s, DEC_BATCH), kv5(v_s, DEC_BATCH),
            va_s.reshape(1, DEC_BATCH, 1, A_WIDTH))
```

```python
import math

import jax
import jax.numpy as jnp
from jax import lax
from jax.experimental import pallas as pl
from jax.experimental.pallas import tpu as pltpu

F32 = jnp.float32
BF16 = jnp.bfloat16

D_MODEL = 1024
BATCH = 4
SEQ = 4096
DEC_BATCH = 128
HEAD_DIM = 64
A_WIDTH = 512
B_WIDTH = 512
B_HEADS = 8
KV_WIDTH = 128
IN_WIDTH = 2 * A_WIDTH + B_WIDTH + 2 * KV_WIDTH
CHUNK = 128
N_EXPERTS = 32
TOP_K = 4
D_FF = 1024
SWIGLU_ALPHA = 1.702
SWIGLU_LIMIT = 7.0
PLE_DIM = 256
EPS = 1e-5

LANES = 128
ROW_BLOCK = 128
EXP_BLOCK = 512
N_PROMPT = BATCH * SEQ
N_TOK = N_PROMPT + DEC_BATCH
TOK_TILE = 128
N_TILES = N_TOK // TOK_TILE
DISP_TILE = 256
TM = 512
N_PAD = ((N_TOK + TM - 1) // TM) * TM
N_PAD_TILES = N_PAD // TOK_TILE
DISP_CHUNK = 32
N_XBLOCKS = (N_TOK * TOP_K + N_EXPERTS * (DISP_CHUNK + EXP_BLOCK - 1) + EXP_BLOCK - 1) // EXP_BLOCK
N_ROWS = N_XBLOCKS * EXP_BLOCK
XBE_ROWS = ((N_XBLOCKS + 7) // 8) * 8
TAB_ROWS = ((N_PAD_TILES + 7) // 8) * 8
PACK = D_MODEL // 2 // LANES
WIN = 64
WIN_ALIGN = 16
SAMPLE_TILE = 32
NEG = -1e30
VMEM_LIMIT = 56 * 1024 * 1024


def _rms(x, g):
    return x * lax.rsqrt(jnp.mean(x * x, axis=-1, keepdims=True) + EPS) * g


def _gelu(x):
    c = math.sqrt(2.0 / math.pi)
    return x * (0.5 * (1.0 + jnp.tanh(c * (x + 0.044715 * (x * x * x)))))


def _layernorm(x, g, b):
    mu = jnp.mean(x, axis=-1, keepdims=True)
    xc = x - mu
    return xc * lax.rsqrt(jnp.mean(xc * xc, axis=-1, keepdims=True) + EPS) * g + b


def _lane_iota(shape):
    return lax.broadcasted_iota(jnp.int32, shape, len(shape) - 1)


def _route(xn2, wr_ref, br_ref):
    m = xn2.shape[0]
    xh = xn2.astype(BF16)
    xl = (xn2 - xh.astype(F32)).astype(BF16)
    r = jnp.dot(jnp.concatenate([xh, xl], axis=0), wr_ref[...], preferred_element_type=F32)
    r = r[:m] + r[m:]
    lane = _lane_iota((m, LANES))
    lane_f = lane.astype(F32)
    logits = jnp.where(lane < N_EXPERTS, r + pltpu.roll(r, LANES - N_EXPERTS, 1) + br_ref[...], NEG)
    work = logits
    sel = jnp.zeros((m, LANES), F32)
    top = None
    z = None
    for _ in range(TOP_K):
        mx = jnp.max(work, axis=-1, keepdims=True)
        first = jnp.min(jnp.where(work == mx, lane_f, float(LANES)), axis=-1, keepdims=True)
        hit = lane_f == first
        sel = jnp.where(hit, 1.0, sel)
        work = jnp.where(hit, NEG, work)
        if top is None:
            top = mx
            z = jnp.ones_like(mx)
        else:
            z = z + jnp.exp(mx - top)
    gates = jnp.where(sel > 0.0, jnp.exp(logits - top) / z, 0.0)
    return gates, sel


def _prompt_kernel(sinks_ref, x_ref, gmix_ref, win_ref, lng_ref, lnb_ref, wsp_ref, bsp_ref,
                   goa_ref, gob_ref, wo_ref, gmoe_ref, wr_ref, br_ref,
                   h1_ref, xn_ref, gm_ref, sm_ref, k_ref, v_ref,
                   z_s, kv_s, cat_s):
    g = pl.program_id(0)
    j = g % (SEQ // TM)

    @pl.when(g >= N_PROMPT // TM)
    def _():
        h1_ref[...] = jnp.zeros_like(h1_ref)
        xn_ref[...] = jnp.zeros_like(xn_ref)
        gm_ref[...] = jnp.zeros_like(gm_ref)
        sm_ref[...] = jnp.zeros_like(sm_ref)

    @pl.when(g < N_PROMPT // TM)
    def _():
        _prompt_tile(j, sinks_ref, x_ref, gmix_ref, win_ref, lng_ref, lnb_ref, wsp_ref, bsp_ref,
                     goa_ref, gob_ref, wo_ref, gmoe_ref, wr_ref, br_ref,
                     h1_ref, xn_ref, gm_ref, sm_ref, k_ref, v_ref, z_s, kv_s, cat_s)


def _prompt_tile(j, sinks_ref, x_ref, gmix_ref, win_ref, lng_ref, lnb_ref, wsp_ref, bsp_ref,
                 goa_ref, gob_ref, wo_ref, gmoe_ref, wr_ref, br_ref,
                 h1_ref, xn_ref, gm_ref, sm_ref, k_ref, v_ref, z_s, kv_s, cat_s):
    @pl.when(j == 0)
    def _():
        kv_s[0:CHUNK, :] = jnp.zeros((CHUNK, 2 * KV_WIDTH), F32)

    xn = _rms(x_ref[...], gmix_ref[...]).astype(BF16)
    z_s[...] = jnp.dot(xn, win_ref[...], preferred_element_type=F32)
    kv_s[CHUNK:, :] = z_s[:, 2 * A_WIDTH + B_WIDTH:]

    lane = _lane_iota((CHUNK, LANES))
    lo = lane < HEAD_DIM
    lane2 = _lane_iota((2 * CHUNK, LANES))
    lo2 = lane2 < HEAD_DIM
    qi = lax.broadcasted_iota(jnp.int32, (CHUNK, 2 * CHUNK), 0)
    kj = lax.broadcasted_iota(jnp.int32, (CHUNK, 2 * CHUNK), 1)
    dist_i = qi + CHUNK - kj
    dist = dist_i.astype(F32)
    band = (dist_i >= 0) & (dist_i < CHUNK)

    def chunk_body(c, carry):
        r0 = pl.multiple_of(c * CHUNK, CHUNK)
        rows = pl.ds(r0, CHUNK)
        u = _gelu(z_s[rows, 0:A_WIDTH])
        va = _layernorm(_gelu(z_s[rows, A_WIDTH:2 * A_WIDTH]), lng_ref[...], lnb_ref[...])
        vab = va.astype(BF16)
        slabs = []
        for p in range(A_WIDTH // LANES):
            slab = vab[:, p * LANES:(p + 1) * LANES]
            m0 = jnp.dot(wsp_ref[2 * p], slab, preferred_element_type=F32)
            m1 = jnp.dot(wsp_ref[2 * p + 1], slab, preferred_element_type=F32)
            slabs.append(jnp.where(lo, m0, m1))
        ya = u * (jnp.concatenate(slabs, axis=-1) + bsp_ref[...])
        ya_n = _rms(ya, goa_ref[...])
        k2 = kv_s[pl.ds(r0, 2 * CHUNK), 0:KV_WIDTH]
        v2 = kv_s[pl.ds(r0, 2 * CHUNK), KV_WIDTH:2 * KV_WIDTH]
        k2r = pltpu.roll(k2, HEAD_DIM, 1)
        v2r = pltpu.roll(v2, HEAD_DIM, 1)
        kd = (jnp.where(lo2, k2, k2r).astype(BF16), jnp.where(lo2, k2r, k2).astype(BF16))
        vd = (jnp.where(lo2, v2, v2r).astype(BF16), jnp.where(lo2, v2r, v2).astype(BF16))
        prev_ok = (j > 0) | (c > 0)
        valid = band & ((kj >= CHUNK) | prev_ok)
        yb_slabs = []
        for kv in range(2):
            q0 = z_s[rows, 2 * A_WIDTH + (2 * kv) * LANES:2 * A_WIDTH + (2 * kv + 1) * LANES]
            q1 = z_s[rows, 2 * A_WIDTH + (2 * kv + 1) * LANES:2 * A_WIDTH + (2 * kv + 2) * LANES]
            lhs = jnp.concatenate([jnp.where(lo, q0, 0.0), jnp.where(lo, 0.0, q0),
                                   jnp.where(lo, q1, 0.0), jnp.where(lo, 0.0, q1)], axis=0).astype(BF16)
            s_all = lax.dot_general(lhs, kd[kv], (((1,), (1,)), ((), ())), preferred_element_type=F32)
            probs = []
            for i in range(4):
                h = 4 * kv + i
                slope = 2.0 ** (-(h + 1))
                sink = sinks_ref[h]
                s = s_all[i * CHUNK:(i + 1) * CHUNK] * (HEAD_DIM ** -0.5) - slope * dist
                s = jnp.where(valid, s, NEG)
                mx = jnp.maximum(jnp.max(s, axis=-1, keepdims=True), sink)
                e = jnp.exp(s - mx)
                den = jnp.sum(e, axis=-1, keepdims=True) + jnp.exp(sink - mx)
                probs.append(e * (1.0 / den))
            pm = jnp.concatenate(probs, axis=0).astype(BF16)
            o = jnp.dot(pm, vd[kv], preferred_element_type=F32)
            yb_slabs.append(jnp.where(lo, o[0:CHUNK], o[CHUNK:2 * CHUNK]))
            yb_slabs.append(jnp.where(lo, o[2 * CHUNK:3 * CHUNK], o[3 * CHUNK:4 * CHUNK]))
        yb_n = _rms(jnp.concatenate(yb_slabs, axis=-1), gob_ref[...])
        cat_s[rows, 0:A_WIDTH] = ya_n.astype(BF16)
        cat_s[rows, A_WIDTH:] = yb_n.astype(BF16)
        return carry

    lax.fori_loop(0, TM // CHUNK, chunk_body, 0)

    kv_s[0:CHUNK, :] = kv_s[TM:TM + CHUNK, :]
    k_ref[...] = kv_s[TM:TM + CHUNK, 0:KV_WIDTH]
    v_ref[...] = kv_s[TM:TM + CHUNK, KV_WIDTH:]

    h1 = x_ref[...] + jnp.dot(cat_s[...], wo_ref[...], preferred_element_type=F32)
    h1_ref[...] = h1
    xn2 = _rms(h1, gmoe_ref[...])
    xn_ref[...] = xn2.astype(BF16)
    gates, sel = _route(xn2, wr_ref, br_ref)
    gm_ref[...] = gates
    sm_ref[...] = sel


def _full(shape):
    n = len(shape)
    return pl.BlockSpec(shape, lambda *_: (0,) * n)


def _prompt_call(x, sinks, gmix, win, lng, lnb, wsp, bsp, goa, gob, wo, gmoe, wr, br):
    real = N_PROMPT // TM
    row = lambda g: (g, 0)
    seq = lambda g: (jnp.minimum(g, real - 1) // (SEQ // TM), 0, 0)
    return pl.pallas_call(
        _prompt_kernel,
        grid=(N_PAD // TM,),
        in_specs=[
            pl.BlockSpec(memory_space=pltpu.SMEM),
            pl.BlockSpec((TM, D_MODEL), lambda g: (jnp.minimum(g, real - 1), 0)),
            _full((1, D_MODEL)), _full((D_MODEL, IN_WIDTH)), _full((1, A_WIDTH)), _full((1, A_WIDTH)),
            _full((8, CHUNK, CHUNK)), _full((CHUNK, A_WIDTH)), _full((1, A_WIDTH)), _full((1, B_WIDTH)),
            _full((D_MODEL, D_MODEL)), _full((1, D_MODEL)), _full((D_MODEL, LANES)), _full((1, LANES)),
        ],
        out_specs=[
            pl.BlockSpec((TM, D_MODEL), row),
            pl.BlockSpec((TM, D_MODEL), row),
            pl.BlockSpec((TM, LANES), row),
            pl.BlockSpec((TM, LANES), row),
            pl.BlockSpec((None, CHUNK, KV_WIDTH), seq),
            pl.BlockSpec((None, CHUNK, KV_WIDTH), seq),
        ],
        out_shape=[
            jax.ShapeDtypeStruct((N_PAD, D_MODEL), F32),
            jax.ShapeDtypeStruct((N_PAD, D_MODEL), BF16),
            jax.ShapeDtypeStruct((N_PAD, LANES), F32),
            jax.ShapeDtypeStruct((N_PAD, LANES), F32),
            jax.ShapeDtypeStruct((BATCH, CHUNK, KV_WIDTH), F32),
            jax.ShapeDtypeStruct((BATCH, CHUNK, KV_WIDTH), F32),
        ],
        scratch_shapes=[
            pltpu.VMEM((TM, IN_WIDTH), F32),
            pltpu.VMEM((TM + CHUNK, 2 * KV_WIDTH), F32),
            pltpu.VMEM((TM, D_MODEL), BF16),
        ],
        compiler_params=pltpu.CompilerParams(
            dimension_semantics=("arbitrary",), vmem_limit_bytes=VMEM_LIMIT),
        name="prompt_premoe",
    )(sinks, x, gmix, win, lng, lnb, wsp, bsp, goa, gob, wo, gmoe, wr, br)


def _sample_kernel(sinks_ref, x_ref, ck_ref, cv_ref, gmix_ref, win_ref, lng_ref, lnb_ref, w00_ref, b0_ref,
                   goa_ref, gob_ref, wo_ref, gmoe_ref, wr_ref, br_ref,
                   h1_in, xn_in, gm_in, sm_in,
                   h1_ref, xn_ref, gm_ref, sm_ref, nk_ref, nv_ref, va_ref):
    del h1_in, xn_in, gm_in, sm_in
    t = SAMPLE_TILE
    nkeys = t * CHUNK

    if True:
        x = x_ref[...]
        xn = _rms(x, gmix_ref[...]).astype(BF16)
        z = jnp.dot(xn, win_ref[...], preferred_element_type=F32)
        u = _gelu(z[:, 0:A_WIDTH])
        va = _layernorm(_gelu(z[:, A_WIDTH:2 * A_WIDTH]), lng_ref[...], lnb_ref[...])
        va_ref[...] = va
        ya_n = _rms(u * (w00_ref[...] * va + b0_ref[...]), goa_ref[...])

        knew = z[:, 2 * A_WIDTH + B_WIDTH:2 * A_WIDTH + B_WIDTH + KV_WIDTH]
        vnew = z[:, 2 * A_WIDTH + B_WIDTH + KV_WIDTH:]
        lane = _lane_iota((t, LANES))
        lo = lane < HEAD_DIM
        stacked = []
        for h in range(B_HEADS):
            q = z[:, 2 * A_WIDTH + (h // 2) * LANES:2 * A_WIDTH + (h // 2 + 1) * LANES]
            qh = jnp.where(lo if h % 2 == 0 else jnp.logical_not(lo), q, 0.0)
            if h % 2 != h // 4:
                qh = pltpu.roll(qh, HEAD_DIM, 1)
            stacked.append(qh)
        qs = jnp.concatenate(stacked, axis=0)
        rows = B_HEADS * t
        ridx = lax.broadcasted_iota(jnp.int32, (rows, 1), 0)
        slope = jnp.zeros((rows, 1), F32)
        sink = jnp.zeros((rows, 1), F32)
        for h in range(B_HEADS):
            in_h = (ridx >= h * t) & (ridx < (h + 1) * t)
            slope = jnp.where(in_h, 2.0 ** (-(h + 1)), slope)
            sink = jnp.where(in_h, sinks_ref[h], sink)
        s_c = lax.dot_general(qs.astype(BF16), ck_ref[...].astype(BF16), (((1,), (1,)), ((), ())),
                              preferred_element_type=F32)
        rsamp = lax.broadcasted_iota(jnp.int32, (rows, nkeys), 0) % t
        col = lax.broadcasted_iota(jnp.int32, (rows, nkeys), 1)
        pos = col % CHUNK
        own = ((col // CHUNK) == rsamp) & (pos >= 1)
        s_c = s_c * (HEAD_DIM ** -0.5) - slope * (CHUNK - pos).astype(F32)
        s_c = jnp.where(own, s_c, NEG)
        kn8 = jnp.concatenate([knew] * B_HEADS, axis=0)
        vn8 = jnp.concatenate([vnew] * B_HEADS, axis=0)
        s_n = jnp.sum(qs * kn8, axis=-1, keepdims=True) * (HEAD_DIM ** -0.5)
        mx = jnp.maximum(jnp.maximum(jnp.max(s_c, axis=-1, keepdims=True), s_n), sink)
        e_c = jnp.exp(s_c - mx)
        e_n = jnp.exp(s_n - mx)
        inv = 1.0 / (jnp.sum(e_c, axis=-1, keepdims=True) + e_n + jnp.exp(sink - mx))
        o = jnp.dot((e_c * inv).astype(BF16), cv_ref[...].astype(BF16), preferred_element_type=F32)
        o = o + (e_n * inv) * vn8
        yb_slabs = []
        for p in range(B_WIDTH // LANES):
            outs = []
            for half in range(2):
                h = 2 * p + half
                oh = o[h * t:(h + 1) * t]
                oh = jnp.where(lo if h // 4 == 0 else jnp.logical_not(lo), oh, 0.0)
                if half != h // 4:
                    oh = pltpu.roll(oh, HEAD_DIM, 1)
                outs.append(oh)
            yb_slabs.append(outs[0] + outs[1])
        yb_n = _rms(jnp.concatenate(yb_slabs, axis=-1), gob_ref[...])

        cat = jnp.concatenate([ya_n, yb_n], axis=-1).astype(BF16)
        h1 = x + jnp.dot(cat, wo_ref[...], preferred_element_type=F32)
        xn2 = _rms(h1, gmoe_ref[...])
        gates, sel = _route(xn2, wr_ref, br_ref)
        h1_ref[...] = h1
        xn_ref[...] = xn2.astype(BF16)
        gm_ref[...] = gates
        sm_ref[...] = sel

        nk_ref[...] = pltpu.roll(ck_ref[...], nkeys - 1, 0)
        nv_ref[...] = pltpu.roll(cv_ref[...], nkeys - 1, 0)
        for b in range(t):
            nk_ref[b * CHUNK + CHUNK - 1:b * CHUNK + CHUNK, :] = knew[b:b + 1, :]
            nv_ref[b * CHUNK + CHUNK - 1:b * CHUNK + CHUNK, :] = vnew[b:b + 1, :]


def _sample_call(x, ck, cv, sinks, gmix, win, lng, lnb, w00, b0, goa, gob, wo, gmoe, wr, br, h1, xn, gm, sm):
    t = SAMPLE_TILE
    steps = DEC_BATCH // t
    base = N_PROMPT // t
    inrow = lambda i: (i, 0)
    outrow = lambda i: (base + i, 0)
    anyspec = pl.BlockSpec(memory_space=pl.ANY)
    return pl.pallas_call(
        _sample_kernel,
        grid=(steps,),
        in_specs=[
            pl.BlockSpec(memory_space=pltpu.SMEM),
            pl.BlockSpec((t, D_MODEL), inrow),
            pl.BlockSpec((t * CHUNK, KV_WIDTH), inrow),
            pl.BlockSpec((t * CHUNK, KV_WIDTH), inrow),
            _full((1, D_MODEL)), _full((D_MODEL, IN_WIDTH)), _full((1, A_WIDTH)), _full((1, A_WIDTH)),
            _full((1, A_WIDTH)), _full((1, A_WIDTH)), _full((1, A_WIDTH)), _full((1, B_WIDTH)),
            _full((D_MODEL, D_MODEL)), _full((1, D_MODEL)), _full((D_MODEL, LANES)), _full((1, LANES)),
            anyspec, anyspec, anyspec, anyspec,
        ],
        out_specs=[
            pl.BlockSpec((t, D_MODEL), outrow),
            pl.BlockSpec((t, D_MODEL), outrow),
            pl.BlockSpec((t, LANES), outrow),
            pl.BlockSpec((t, LANES), outrow),
            pl.BlockSpec((t * CHUNK, KV_WIDTH), inrow),
            pl.BlockSpec((t * CHUNK, KV_WIDTH), inrow),
            pl.BlockSpec((t, A_WIDTH), inrow),
        ],
        out_shape=[
            jax.ShapeDtypeStruct((N_PAD, D_MODEL), F32),
            jax.ShapeDtypeStruct((N_PAD, D_MODEL), BF16),
            jax.ShapeDtypeStruct((N_PAD, LANES), F32),
            jax.ShapeDtypeStruct((N_PAD, LANES), F32),
            jax.ShapeDtypeStruct((DEC_BATCH * CHUNK, KV_WIDTH), F32),
            jax.ShapeDtypeStruct((DEC_BATCH * CHUNK, KV_WIDTH), F32),
            jax.ShapeDtypeStruct((DEC_BATCH, A_WIDTH), F32),
        ],
        input_output_aliases={16: 0, 17: 1, 18: 2, 19: 3},
        compiler_params=pltpu.CompilerParams(
            dimension_semantics=("arbitrary",), vmem_limit_bytes=VMEM_LIMIT),
        name="sample_premoe",
    )(sinks, x, ck, cv, gmix, win, lng, lnb, w00, b0, goa, gob, wo, gmoe, wr, br, h1, xn, gm, sm)


def _plan_kernel(sm_ref,
                 destm_ref, destt_ref, xbe_ref, nxblk_ref, stab_ref, ctab_ref, astart_ref, nwin_ref,
                 base_s, pstart_s):
    ph = pl.program_id(0)
    step = pl.program_id(1)
    lane = _lane_iota((1, LANES))

    @pl.when((ph == 0) & (step == 0))
    def _():
        base_s[...] = jnp.zeros_like(base_s)

    @pl.when(ph == 0)
    def _():
        base_s[...] += jnp.sum(sm_ref[...], axis=0, keepdims=True)

    @pl.when((ph == 1) & (step == 0))
    def _():
        counts = base_s[...]
        padded = jnp.floor((counts + (DISP_CHUNK + EXP_BLOCK - 1)) * (1.0 / EXP_BLOCK)) * EXP_BLOCK
        padded = jnp.where(counts > 0.0, padded, 0.0)
        pend = padded
        for s in (1, 2, 4, 8, 16):
            pend = pend + jnp.where(lane >= s, pltpu.roll(pend, s, 1), 0.0)
        pstart_s[...] = pend - padded
        base_s[...] = jnp.zeros_like(base_s)
        brow = lax.broadcasted_iota(jnp.int32, (XBE_ROWS, LANES), 0).astype(F32) * EXP_BLOCK
        done = jnp.where((lane < N_EXPERTS) & (pend <= brow), 1.0, 0.0)
        be = jnp.minimum(jnp.sum(done, axis=-1, keepdims=True), N_EXPERTS - 1.0)
        xbe_ref[...] = jnp.broadcast_to(be, (XBE_ROWS, LANES)).astype(jnp.int32)
        total = jnp.sum(jnp.where(lane == N_EXPERTS - 1, pend, 0.0), axis=-1, keepdims=True)
        nxblk_ref[...] = jnp.broadcast_to(total * (1.0 / EXP_BLOCK), (8, LANES)).astype(jnp.int32)
        stab_ref[...] = jnp.zeros_like(stab_ref)
        ctab_ref[...] = jnp.zeros_like(ctab_ref)
        astart_ref[...] = jnp.zeros_like(astart_ref)
        nwin_ref[...] = jnp.zeros_like(nwin_ref)

    @pl.when(ph == 1)
    def _():
        r = lax.broadcasted_iota(jnp.int32, (TOK_TILE, TOK_TILE), 0)
        c = lax.broadcasted_iota(jnp.int32, (TOK_TILE, TOK_TILE), 1)
        lower = jnp.where(c < r, 1.0, 0.0).astype(BF16)
        for q in range(TM // TOK_TILE):
            i = step * (TM // TOK_TILE) + q
            sel = sm_ref[q * TOK_TILE:(q + 1) * TOK_TILE, :]
            cnt = jnp.sum(sel, axis=0, keepdims=True)
            prefix = jnp.dot(lower, sel.astype(BF16), preferred_element_type=F32)
            start = pstart_s[...] + base_s[...]
            dest = jnp.where(sel > 0.0, prefix + start, -1.0)
            destm_ref[q * TOK_TILE:(q + 1) * TOK_TILE, :] = dest
            destt_ref[:, q * TOK_TILE:(q + 1) * TOK_TILE] = dest.T
            has = (cnt > 0.0) & (lane < N_EXPERTS)
            stab_ref[pl.ds(i, 1), :] = start.astype(jnp.int32)
            ctab_ref[pl.ds(i, 1), :] = jnp.where(has, cnt, 0.0).astype(jnp.int32)
            a = jnp.minimum(jnp.floor(start * (1.0 / WIN_ALIGN)) * WIN_ALIGN, float(N_ROWS - WIN))
            nw = jnp.where(has, jnp.floor((start + cnt - a + (WIN - 1)) * (1.0 / WIN)), 0.0)
            astart_ref[pl.ds(i, 1), :] = a.astype(jnp.int32)
            nwin_ref[pl.ds(i, 1), :] = nw.astype(jnp.int32)
            base_s[...] += cnt


def _plan_call(sm):
    tile = lambda ph, i: (i * ph, 0)
    tile_t = lambda ph, i: (0, i * ph)
    tab = jax.ShapeDtypeStruct((TAB_ROWS, LANES), jnp.int32)
    return pl.pallas_call(
        _plan_kernel,
        grid=(2, N_PAD // TM),
        in_specs=[pl.BlockSpec((TM, LANES), lambda ph, i: (i, 0))],
        out_specs=[
            pl.BlockSpec((TM, LANES), tile),
            pl.BlockSpec((LANES, TM), tile_t),
            _full((XBE_ROWS, LANES)), _full((8, LANES)),
            _full((TAB_ROWS, LANES)), _full((TAB_ROWS, LANES)), _full((TAB_ROWS, LANES)), _full((TAB_ROWS, LANES)),
        ],
        out_shape=[
            jax.ShapeDtypeStruct((N_PAD, LANES), F32),
            jax.ShapeDtypeStruct((LANES, N_PAD), F32),
            jax.ShapeDtypeStruct((XBE_ROWS, LANES), jnp.int32),
            jax.ShapeDtypeStruct((8, LANES), jnp.int32),
            tab, tab, tab, tab,
        ],
        scratch_shapes=[pltpu.VMEM((1, LANES), F32), pltpu.VMEM((1, LANES), F32)],
        compiler_params=pltpu.CompilerParams(
            dimension_semantics=("arbitrary", "arbitrary"), vmem_limit_bytes=VMEM_LIMIT),
        name="moe_plan",
    )(sm)


def _pack_rows(z):
    half = D_MODEL // 2
    lo = lax.bitcast_convert_type(z[:, :half], jnp.uint32) >> 16
    hi = lax.bitcast_convert_type(z[:, half:], jnp.uint32) & jnp.uint32(0xFFFF0000)
    return lax.bitcast_convert_type(hi | lo, jnp.int32)


def _unpack_rows(ref):
    rows = ref.shape[0]
    flat = ref.reshape(rows * PACK, LANES)
    lo, hi = [], []
    for s in range(PACK):
        w = lax.bitcast_convert_type(flat[pl.ds(s, rows, stride=PACK), :], jnp.uint32)
        lo.append(lax.bitcast_convert_type(w << 16, F32))
        hi.append(lax.bitcast_convert_type(w & jnp.uint32(0xFFFF0000), F32))
    return jnp.concatenate(lo + hi, axis=-1).astype(BF16)


def _dispatch_kernel(stab_ref, ctab_ref, cmax_ref, xn_ref, destt_ref, xs_in, xs_ref,
                     stage0, stage1, stage2, sems, sem2):
    del xs_in
    i = pl.program_id(0)
    last = pl.num_programs(0) - 1
    x = xn_ref[...]
    dt = destt_ref[...]
    rio = lax.broadcasted_iota(jnp.int32, (DISP_CHUNK, 1), 0).astype(F32)

    def chunk_rows(j, stage):
        parts = []
        for e in range(N_EXPERTS):
            first = (stab_ref[i * N_EXPERTS + e] + j * DISP_CHUNK).astype(F32)
            parts.append(jnp.where(dt[e:e + 1, :] == first + rio, 1.0, 0.0).astype(BF16))
        onehot = jnp.concatenate(parts, axis=0)
        words = _pack_rows(jnp.dot(onehot, x, preferred_element_type=F32))
        for s in range(PACK):
            stage[pl.ds(s, N_EXPERTS * DISP_CHUNK, stride=PACK), :] = words[:, s * LANES:(s + 1) * LANES]

    def copy(stage, step, e, j, sem):
        first = stab_ref[step * N_EXPERTS + e] + j * DISP_CHUNK
        rows = stage.reshape(N_EXPERTS * DISP_CHUNK, PACK, LANES)
        return pltpu.make_async_copy(rows.at[pl.ds(e * DISP_CHUNK, DISP_CHUNK)],
                                     xs_ref.at[pl.ds(first, DISP_CHUNK)], sem)

    def step_body(stage, prev_stage, par):
        chunk_rows(0, stage)
        for e in range(N_EXPERTS):

            @pl.when((i > 0) & (ctab_ref[jnp.maximum(i - 1, 0) * N_EXPERTS + e] > 0))
            def _(e=e):
                copy(prev_stage, i - 1, e, 0, sems.at[1 - par, e]).wait()

        for e in range(N_EXPERTS):

            @pl.when(ctab_ref[i * N_EXPERTS + e] > 0)
            def _(e=e):
                copy(stage, i, e, 0, sems.at[par, e]).start()

        for e in range(N_EXPERTS):

            @pl.when((i == last) & (ctab_ref[i * N_EXPERTS + e] > 0))
            def _(e=e):
                copy(stage, i, e, 0, sems.at[par, e]).wait()

    @pl.when(i % 2 == 0)
    def _():
        step_body(stage0, stage1, 0)

    @pl.when(i % 2 == 1)
    def _():
        step_body(stage1, stage0, 1)

    for j in range(1, TOK_TILE // DISP_CHUNK):

        @pl.when(cmax_ref[i] > j * DISP_CHUNK)
        def _(j=j):
            chunk_rows(j, stage2)
            for e in range(N_EXPERTS):

                @pl.when(ctab_ref[i * N_EXPERTS + e] > j * DISP_CHUNK)
                def _(e=e):
                    cp = copy(stage2, i, e, j, sem2)
                    cp.start()
                    cp.wait()


def _dispatch_call(stab, ctab, cmax, xn, destt, xs_zero):
    stage = pltpu.VMEM((N_EXPERTS * DISP_CHUNK * PACK, LANES), jnp.int32)
    grid_spec = pltpu.PrefetchScalarGridSpec(
        num_scalar_prefetch=3,
        grid=(N_PAD_TILES,),
        in_specs=[
            pl.BlockSpec((TOK_TILE, D_MODEL), lambda i, *_: (i, 0)),
            pl.BlockSpec((N_EXPERTS, TOK_TILE), lambda i, *_: (0, i)),
            pl.BlockSpec(memory_space=pl.ANY),
        ],
        out_specs=pl.BlockSpec(memory_space=pl.ANY),
        scratch_shapes=[stage, stage, stage, pltpu.SemaphoreType.DMA((2, N_EXPERTS)),
                        pltpu.SemaphoreType.DMA],
    )
    return pl.pallas_call(
        _dispatch_kernel,
        grid_spec=grid_spec,
        out_shape=jax.ShapeDtypeStruct((N_ROWS, PACK, LANES), jnp.int32),
        input_output_aliases={5: 0},
        compiler_params=pltpu.CompilerParams(
            dimension_semantics=("arbitrary",), vmem_limit_bytes=VMEM_LIMIT),
        name="moe_dispatch",
    )(stab, ctab, cmax, xn, destt, xs_zero)


def _expert_kernel(blke_ref, nblk_ref, next_ref, xs_ref, wgu_hbm, bgu_ref, wdn_hbm, bdn_ref,
                   ys_ref, wgu_f, wdn_f, wgu_s, wdn_s, sems):
    b = pl.program_id(0)
    used = b < nblk_ref[0]
    prev = blke_ref[jnp.maximum(b - 1, 0)]
    fresh = used & ((b == 0) | (blke_ref[b] != prev))

    def fetch(e):
        return (pltpu.make_async_copy(wgu_hbm.at[e], wgu_f, sems.at[0]),
                pltpu.make_async_copy(wdn_hbm.at[e], wdn_f, sems.at[1]))

    @pl.when(b == 0)
    def _():
        for cp in fetch(blke_ref[0]):
            cp.start()

    @pl.when(fresh)
    def _():
        for cp in fetch(blke_ref[b]):
            cp.wait()
        wgu_s[...] = wgu_f[...].astype(BF16)
        wdn_s[...] = wdn_f[...].astype(BF16)

        @pl.when(next_ref[b] >= 0)
        def _():
            for cp in fetch(next_ref[b]):
                cp.start()

    @pl.when(used)
    def _():
        hid = jnp.dot(_unpack_rows(xs_ref), wgu_s[...], preferred_element_type=F32) + bgu_ref[...]
        gate = jnp.minimum(hid[:, :D_FF], SWIGLU_LIMIT)
        up = jnp.clip(hid[:, D_FF:], -SWIGLU_LIMIT, SWIGLU_LIMIT)
        act = (up + 1.0) * gate * jax.nn.sigmoid(SWIGLU_ALPHA * gate)
        y = jnp.dot(act.astype(BF16), wdn_s[...], preferred_element_type=F32) + bdn_ref[...]
        ys_ref[...] = y.astype(BF16)

    @pl.when(b >= nblk_ref[0])
    def _():
        ys_ref[...] = jnp.zeros_like(ys_ref)


def _expert_call(blke, nblk, nxt, xs, wgu, bgu, wdn, bdn):
    grid_spec = pltpu.PrefetchScalarGridSpec(
        num_scalar_prefetch=3,
        grid=(N_XBLOCKS,),
        in_specs=[
            pl.BlockSpec((EXP_BLOCK, PACK, LANES), lambda b, be, nb, nx: (b, 0, 0)),
            pl.BlockSpec(memory_space=pl.ANY),
            pl.BlockSpec((None, 1, 2 * D_FF), lambda b, be, nb, nx: (be[b], 0, 0)),
            pl.BlockSpec(memory_space=pl.ANY),
            pl.BlockSpec((None, 1, D_MODEL), lambda b, be, nb, nx: (be[b], 0, 0)),
        ],
        out_specs=pl.BlockSpec((EXP_BLOCK, D_MODEL), lambda b, be, nb, nx: (b, 0)),
        scratch_shapes=[pltpu.VMEM((D_MODEL, 2 * D_FF), F32), pltpu.VMEM((D_FF, D_MODEL), F32),
                        pltpu.VMEM((D_MODEL, 2 * D_FF), BF16), pltpu.VMEM((D_FF, D_MODEL), BF16),
                        pltpu.SemaphoreType.DMA((2,))],
    )
    return pl.pallas_call(
        _expert_kernel,
        grid_spec=grid_spec,
        out_shape=jax.ShapeDtypeStruct((N_ROWS, D_MODEL), BF16),
        compiler_params=pltpu.CompilerParams(
            dimension_semantics=("arbitrary",), vmem_limit_bytes=VMEM_LIMIT),
        name="moe_experts",
    )(blke, nblk, nxt, xs, wgu, bgu, wdn, bdn)


def _combine_kernel(*refs):
    astart_ref, nwin_ref, over_ref = refs[0:3]
    win_refs = refs[3:3 + N_EXPERTS]
    (destm_ref, gm_ref, h1_ref, plep_ref, ples_ref, gple_ref, wpg_ref, wpp_ref, gfin_ref, ys_any,
     yp_ref, ysm_ref, moe_s, tmp_s, sem) = refs[3 + N_EXPERTS:]
    i = pl.program_id(0)
    dest = destm_ref[...]
    gates = gm_ref[...]
    lane = _lane_iota((TOK_TILE, LANES))
    lane_f = lane.astype(F32)
    lo = lane < WIN
    moe = jnp.zeros((TOK_TILE, D_MODEL), F32)
    group = 4
    for g0 in range(0, N_EXPERTS, group):
        g_hi, g_lo = [], []
        for p in range(group // 2):
            e0 = g0 + 2 * p
            a0 = astart_ref[i * N_EXPERTS + e0].astype(F32)
            a1 = astart_ref[i * N_EXPERTS + e0 + 1].astype(F32)
            rowid = jnp.where(lo, a0 + lane_f, a1 + lane_f - WIN)
            dcol = jnp.where(lo, dest[:, e0:e0 + 1], dest[:, e0 + 1:e0 + 2])
            gcol = jnp.where(lo, gates[:, e0:e0 + 1], gates[:, e0 + 1:e0 + 2])
            gsel = jnp.where(dcol == rowid, gcol, 0.0)
            hi = gsel.astype(BF16)
            g_hi.append(hi)
            g_lo.append((gsel - hi.astype(F32)).astype(BF16))
        ywin = jnp.concatenate([win_refs[g0 + q][...] for q in range(group)], axis=0)
        both = jnp.concatenate([jnp.concatenate(g_hi, axis=-1), jnp.concatenate(g_lo, axis=-1)], axis=0)
        r = jnp.dot(both, ywin, preferred_element_type=F32)
        moe = moe + r[:TOK_TILE] + r[TOK_TILE:]
    moe_s[...] = moe

    @pl.when(over_ref[i] > 0)
    def _():
        tmp_s[...] = jnp.zeros_like(tmp_s)

        def per_expert(e, carry):
            a = astart_ref[i * N_EXPERTS + e]
            dcol = jnp.sum(jnp.where(lane == e, dest, 0.0), axis=-1, keepdims=True)
            gcol = jnp.sum(jnp.where(lane == e, gates, 0.0), axis=-1, keepdims=True)

            def per_window(w, carry2):
                first = a + w * WIN
                start = pl.multiple_of(jnp.minimum(first, N_ROWS - WIN), WIN_ALIGN)
                cp = pltpu.make_async_copy(ys_any.at[pl.ds(start, WIN)], tmp_s.at[pl.ds(0, WIN)], sem)
                cp.start()
                cp.wait()
                hit = lo & (dcol == start.astype(F32) + lane_f) & (dcol >= first.astype(F32))
                gsel = jnp.where(hit, gcol, 0.0)
                hi = gsel.astype(BF16)
                rest = (gsel - hi.astype(F32)).astype(BF16)
                moe_s[...] += (jnp.dot(hi, tmp_s[...], preferred_element_type=F32)
                               + jnp.dot(rest, tmp_s[...], preferred_element_type=F32))
                return carry2

            return lax.fori_loop(1, nwin_ref[i * N_EXPERTS + e], per_window, carry)

        lax.fori_loop(0, N_EXPERTS, per_expert, 0)

    is_sample = i == N_TILES - 1
    h2 = h1_ref[...] + moe_s[...]
    ple = jnp.where(is_sample, ples_ref[...], plep_ref[...])
    hn = _rms(h2, gple_ref[...]).astype(BF16)
    gate = jax.nn.sigmoid(jnp.dot(hn, wpg_ref[...], preferred_element_type=F32))
    proj = jnp.dot(ple.astype(BF16), wpp_ref[...], preferred_element_type=F32)
    y = _rms(h2 + gate * proj, gfin_ref[...])

    @pl.when(jnp.logical_not(is_sample))
    def _():
        yp_ref[...] = y

    @pl.when(is_sample)
    def _():
        ysm_ref[...] = y


def _combine_call(astart, nwin, over, ys, destm, gm, h1, plep, ples, gple, wpg, wpp, gfin):
    last = N_TILES - 2

    def win_spec(e):
        return pl.BlockSpec((pl.Element(WIN), pl.Element(D_MODEL)),
                            lambda i, a, nw, ov, e=e: (pl.multiple_of(a[i * N_EXPERTS + e], WIN_ALIGN), 0))

    grid_spec = pltpu.PrefetchScalarGridSpec(
        num_scalar_prefetch=3,
        grid=(N_TILES,),
        in_specs=[win_spec(e) for e in range(N_EXPERTS)] + [
            pl.BlockSpec((TOK_TILE, LANES), lambda i, *_: (i, 0)),
            pl.BlockSpec((TOK_TILE, LANES), lambda i, *_: (i, 0)),
            pl.BlockSpec((TOK_TILE, D_MODEL), lambda i, *_: (i, 0)),
            pl.BlockSpec((TOK_TILE, PLE_DIM), lambda i, *_: (jnp.minimum(i, last), 0)),
            pl.BlockSpec((TOK_TILE, PLE_DIM), lambda i, *_: (0, 0)),
            pl.BlockSpec((1, D_MODEL), lambda i, *_: (0, 0)),
            pl.BlockSpec((D_MODEL, D_MODEL), lambda i, *_: (0, 0)),
            pl.BlockSpec((PLE_DIM, D_MODEL), lambda i, *_: (0, 0)),
            pl.BlockSpec((1, D_MODEL), lambda i, *_: (0, 0)),
            pl.BlockSpec(memory_space=pl.ANY),
        ],
        out_specs=[
            pl.BlockSpec((TOK_TILE, D_MODEL), lambda i, *_: (jnp.minimum(i, last), 0)),
            pl.BlockSpec((TOK_TILE, D_MODEL), lambda i, *_: (0, 0)),
        ],
        scratch_shapes=[pltpu.VMEM((TOK_TILE, D_MODEL), F32), pltpu.VMEM((2 * WIN, D_MODEL), BF16),
                        pltpu.SemaphoreType.DMA],
    )
    return pl.pallas_call(
        _combine_kernel,
        grid_spec=grid_spec,
        out_shape=[jax.ShapeDtypeStruct((N_PROMPT, D_MODEL), F32),
                   jax.ShapeDtypeStruct((DEC_BATCH, D_MODEL), F32)],
        compiler_params=pltpu.CompilerParams(
            dimension_semantics=("arbitrary",), vmem_limit_bytes=VMEM_LIMIT),
        name="moe_combine_tail",
    )(astart, nwin, over, *([ys] * N_EXPERTS), destm, gm, h1, plep, ples, gple, wpg, wpp, gfin, ys)


def kernel(x_prompt, x_sample, cache_swa_k, cache_swa_v, p_prompt, p_sample, g_mix, w_in, ln_v_g, ln_v_b,
           w_sp, b_sp, sinks, g_out_a, g_out_b, w_o, g_moe, w_router, b_router, w_gu, b_gu, w_dn, b_dn,
           g_ple, w_ple_gate, w_ple_proj, g_final):
    l = 0
    row = lambda v: v.reshape(1, -1)
    win = w_in[l].astype(BF16)
    wo = w_o[l].astype(BF16)
    tril = jnp.tril(jnp.ones((CHUNK, CHUNK), dtype=bool))
    wsp = jnp.where(tril, w_sp[l], 0.0).astype(BF16)
    bsp = jnp.repeat(b_sp[l].T, HEAD_DIM, axis=1)
    w00 = row(jnp.repeat(w_sp[l][:, 0, 0], HEAD_DIM))
    b0 = row(jnp.repeat(b_sp[l][:, 0], HEAD_DIM))
    wr_hi = w_router[l].astype(BF16)
    wr_lo = (w_router[l] - wr_hi.astype(F32)).astype(BF16)
    wr = jnp.concatenate([wr_hi, wr_lo, jnp.zeros((D_MODEL, LANES - 2 * N_EXPERTS), BF16)], axis=1)
    br = row(jnp.concatenate([b_router[l], jnp.zeros((LANES - N_EXPERTS,), F32)]))
    common = (row(g_mix[l]), win, row(ln_v_g[l]), row(ln_v_b[l]))
    tail = (row(g_out_a[l]), row(g_out_b[l]), wo, row(g_moe[l]), wr, br)

    h1, xn, gm, sm, k_p, v_p = _prompt_call(
        x_prompt.reshape(N_PROMPT, D_MODEL), sinks[l], *common, wsp, bsp, *tail)
    ck = cache_swa_k[l].reshape(DEC_BATCH * CHUNK, KV_WIDTH)
    cv = cache_swa_v[l].reshape(DEC_BATCH * CHUNK, KV_WIDTH)
    h1, xn, gm, sm, k_s, v_s, va_s = _sample_call(
        x_sample.reshape(DEC_BATCH, D_MODEL), ck, cv, sinks[l], *common, w00, b0, *tail, h1, xn, gm, sm)

    destm, destt, xbe, nxblk, stab, ctab, astart, nwin = _plan_call(sm)
    flat = lambda tab, n: tab[:n, :N_EXPERTS].reshape(-1)
    astart1 = flat(astart, N_TILES)
    nwin2 = nwin[:N_TILES, :N_EXPERTS]
    over1 = (jnp.max(nwin2, axis=1) > 1).astype(jnp.int32)
    nwin1 = nwin2.reshape(-1)
    cmax1 = jnp.max(ctab[:N_PAD_TILES, :N_EXPERTS], axis=1)

    xs = _dispatch_call(flat(stab, N_PAD_TILES), flat(ctab, N_PAD_TILES), cmax1, xn, destt[:N_EXPERTS],
                        jnp.zeros((N_ROWS, PACK, LANES), jnp.int32))
    xbe1, nxblk1 = xbe[:N_XBLOCKS, 0], nxblk[0, :1]
    blk = jnp.arange(N_XBLOCKS, dtype=jnp.int32)
    starts = (blk < nxblk1[0]) & ((blk == 0) | (xbe1 != jnp.roll(xbe1, 1)))
    pos = jnp.where(starts, blk, N_XBLOCKS)
    nxt_pos = jnp.roll(lax.cummin(pos, reverse=True), -1).at[N_XBLOCKS - 1].set(N_XBLOCKS)
    nxt1 = jnp.where(nxt_pos < N_XBLOCKS, xbe1[jnp.minimum(nxt_pos, N_XBLOCKS - 1)], -1).astype(jnp.int32)
    ys = _expert_call(xbe1, nxblk1, nxt1, xs, w_gu[l], b_gu[l].reshape(N_EXPERTS, 1, 2 * D_FF),
                      w_dn[l], b_dn[l].reshape(N_EXPERTS, 1, D_MODEL))
    y_p, y_s = _combine_call(
        astart1, nwin1, over1, ys, destm, gm, h1,
        p_prompt[l].reshape(N_PROMPT, PLE_DIM), p_sample[l].reshape(DEC_BATCH, PLE_DIM),
        row(g_ple[l]), w_ple_gate[l].astype(BF16), w_ple_proj[l].astype(BF16), row(g_final))

    kv5 = lambda a, n: a.reshape(1, n, CHUNK, 2, HEAD_DIM)
    return (y_p.reshape(BATCH, SEQ, D_MODEL), y_s.reshape(DEC_BATCH, 1, D_MODEL),
            kv5(k_p, BATCH), kv5(v_p, BATCH), kv5(k_s, DEC_BATCH), kv5(v_s, DEC_BATCH),
            va_s.reshape(1, DEC_BATCH, 1, A_WIDTH))
```

```python
import math

import jax
import jax.numpy as jnp
from jax import lax
from jax.experimental import pallas as pl
from jax.experimental.pallas import tpu as pltpu

F32 = jnp.float32
BF16 = jnp.bfloat16

D_MODEL = 1024
BATCH = 4
SEQ = 4096
DEC_BATCH = 128
HEAD_DIM = 64
A_WIDTH = 512
B_WIDTH = 512
B_HEADS = 8
KV_WIDTH = 128
IN_WIDTH = 2 * A_WIDTH + B_WIDTH + 2 * KV_WIDTH
CHUNK = 128
N_EXPERTS = 32
TOP_K = 4
D_FF = 1024
SWIGLU_ALPHA = 1.702
SWIGLU_LIMIT = 7.0
PLE_DIM = 256
EPS = 1e-5

LANES = 128
ROW_BLOCK = 128
EXP_BLOCK = 256
N_PROMPT = BATCH * SEQ
N_TOK = N_PROMPT + DEC_BATCH
TOK_TILE = 128
N_TILES = N_TOK // TOK_TILE
DISP_TILE = 256
TM = 512
N_PAD = ((N_TOK + TM - 1) // TM) * TM
N_PAD_TILES = N_PAD // TOK_TILE
DISP_CHUNK = 32
N_XBLOCKS = (N_TOK * TOP_K + N_EXPERTS * (DISP_CHUNK + EXP_BLOCK - 1) + EXP_BLOCK - 1) // EXP_BLOCK
N_ROWS = N_XBLOCKS * EXP_BLOCK
XBE_ROWS = ((N_XBLOCKS + 7) // 8) * 8
TAB_ROWS = ((N_PAD_TILES + 7) // 8) * 8
PACK = D_MODEL // 2 // LANES
WIN = 64
WIN_ALIGN = 16
SAMPLE_TILE = 32
NEG = -1e30
VMEM_LIMIT = 56 * 1024 * 1024


def _rms(x, g):
    return x * lax.rsqrt(jnp.mean(x * x, axis=-1, keepdims=True) + EPS) * g


def _gelu(x):
    c = math.sqrt(2.0 / math.pi)
    return x * (0.5 * (1.0 + jnp.tanh(c * (x + 0.044715 * (x * x * x)))))


def _layernorm(x, g, b):
    mu = jnp.mean(x, axis=-1, keepdims=True)
    xc = x - mu
    return xc * lax.rsqrt(jnp.mean(xc * xc, axis=-1, keepdims=True) + EPS) * g + b


def _lane_iota(shape):
    return lax.broadcasted_iota(jnp.int32, shape, len(shape) - 1)


def _route(xn2, wr_ref, br_ref):
    m = xn2.shape[0]
    xh = xn2.astype(BF16)
    xl = (xn2 - xh.astype(F32)).astype(BF16)
    r = jnp.dot(jnp.concatenate([xh, xl], axis=0), wr_ref[...], preferred_element_type=F32)
    r = r[:m] + r[m:]
    lane = _lane_iota((m, LANES))
    lane_f = lane.astype(F32)
    logits = jnp.where(lane < N_EXPERTS, r + pltpu.roll(r, LANES - N_EXPERTS, 1) + br_ref[...], NEG)
    work = logits
    sel = jnp.zeros((m, LANES), F32)
    top = None
    z = None
    for _ in range(TOP_K):
        mx = jnp.max(work, axis=-1, keepdims=True)
        first = jnp.min(jnp.where(work == mx, lane_f, float(LANES)), axis=-1, keepdims=True)
        hit = lane_f == first
        sel = jnp.where(hit, 1.0, sel)
        work = jnp.where(hit, NEG, work)
        if top is None:
            top = mx
            z = jnp.ones_like(mx)
        else:
            z = z + jnp.exp(mx - top)
    gates = jnp.where(sel > 0.0, jnp.exp(logits - top) / z, 0.0)
    return gates, sel


def _prompt_kernel(sinks_ref, x_ref, gmix_ref, win_ref, lng_ref, lnb_ref, wsp_ref, bsp_ref,
                   goa_ref, gob_ref, wo_ref, gmoe_ref, wr_ref, br_ref,
                   h1_ref, xn_ref, gm_ref, sm_ref, k_ref, v_ref,
                   z_s, kv_s, cat_s):
    g = pl.program_id(0)
    j = g % (SEQ // TM)

    @pl.when(g >= N_PROMPT // TM)
    def _():
        h1_ref[...] = jnp.zeros_like(h1_ref)
        xn_ref[...] = jnp.zeros_like(xn_ref)
        gm_ref[...] = jnp.zeros_like(gm_ref)
        sm_ref[...] = jnp.zeros_like(sm_ref)

    @pl.when(g < N_PROMPT // TM)
    def _():
        _prompt_tile(j, sinks_ref, x_ref, gmix_ref, win_ref, lng_ref, lnb_ref, wsp_ref, bsp_ref,
                     goa_ref, gob_ref, wo_ref, gmoe_ref, wr_ref, br_ref,
                     h1_ref, xn_ref, gm_ref, sm_ref, k_ref, v_ref, z_s, kv_s, cat_s)


def _prompt_tile(j, sinks_ref, x_ref, gmix_ref, win_ref, lng_ref, lnb_ref, wsp_ref, bsp_ref,
                 goa_ref, gob_ref, wo_ref, gmoe_ref, wr_ref, br_ref,
                 h1_ref, xn_ref, gm_ref, sm_ref, k_ref, v_ref, z_s, kv_s, cat_s):
    @pl.when(j == 0)
    def _():
        kv_s[0:CHUNK, :] = jnp.zeros((CHUNK, 2 * KV_WIDTH), F32)

    xn = _rms(x_ref[...], gmix_ref[...]).astype(BF16)
    z_s[...] = jnp.dot(xn, win_ref[...], preferred_element_type=F32)
    kv_s[CHUNK:, :] = z_s[:, 2 * A_WIDTH + B_WIDTH:]

    lane = _lane_iota((CHUNK, LANES))
    lo = lane < HEAD_DIM
    lane2 = _lane_iota((2 * CHUNK, LANES))
    lo2 = lane2 < HEAD_DIM
    qi = lax.broadcasted_iota(jnp.int32, (CHUNK, 2 * CHUNK), 0)
    kj = lax.broadcasted_iota(jnp.int32, (CHUNK, 2 * CHUNK), 1)
    dist_i = qi + CHUNK - kj
    dist = dist_i.astype(F32)
    band = (dist_i >= 0) & (dist_i < CHUNK)

    def chunk_body(c, carry):
        r0 = pl.multiple_of(c * CHUNK, CHUNK)
        rows = pl.ds(r0, CHUNK)
        u = _gelu(z_s[rows, 0:A_WIDTH])
        va = _layernorm(_gelu(z_s[rows, A_WIDTH:2 * A_WIDTH]), lng_ref[...], lnb_ref[...])
        vab = va.astype(BF16)
        slabs = []
        for p in range(A_WIDTH // LANES):
            slab = vab[:, p * LANES:(p + 1) * LANES]
            m0 = jnp.dot(wsp_ref[2 * p], slab, preferred_element_type=F32)
            m1 = jnp.dot(wsp_ref[2 * p + 1], slab, preferred_element_type=F32)
            slabs.append(jnp.where(lo, m0, m1))
        ya = u * (jnp.concatenate(slabs, axis=-1) + bsp_ref[...])
        ya_n = _rms(ya, goa_ref[...])
        k2 = kv_s[pl.ds(r0, 2 * CHUNK), 0:KV_WIDTH]
        v2 = kv_s[pl.ds(r0, 2 * CHUNK), KV_WIDTH:2 * KV_WIDTH]
        k2r = pltpu.roll(k2, HEAD_DIM, 1)
        v2r = pltpu.roll(v2, HEAD_DIM, 1)
        kd = (jnp.where(lo2, k2, k2r).astype(BF16), jnp.where(lo2, k2r, k2).astype(BF16))
        vd = (jnp.where(lo2, v2, v2r).astype(BF16), jnp.where(lo2, v2r, v2).astype(BF16))
        prev_ok = (j > 0) | (c > 0)
        valid = band & ((kj >= CHUNK) | prev_ok)
        yb_slabs = []
        for kv in range(2):
            q0 = z_s[rows, 2 * A_WIDTH + (2 * kv) * LANES:2 * A_WIDTH + (2 * kv + 1) * LANES]
            q1 = z_s[rows, 2 * A_WIDTH + (2 * kv + 1) * LANES:2 * A_WIDTH + (2 * kv + 2) * LANES]
            lhs = jnp.concatenate([jnp.where(lo, q0, 0.0), jnp.where(lo, 0.0, q0),
                                   jnp.where(lo, q1, 0.0), jnp.where(lo, 0.0, q1)], axis=0).astype(BF16)
            s_all = lax.dot_general(lhs, kd[kv], (((1,), (1,)), ((), ())), preferred_element_type=F32)
            probs = []
            for i in range(4):
                h = 4 * kv + i
                slope = 2.0 ** (-(h + 1))
                sink = sinks_ref[h]
                s = s_all[i * CHUNK:(i + 1) * CHUNK] * (HEAD_DIM ** -0.5) - slope * dist
                s = jnp.where(valid, s, NEG)
                mx = jnp.maximum(jnp.max(s, axis=-1, keepdims=True), sink)
                e = jnp.exp(s - mx)
                den = jnp.sum(e, axis=-1, keepdims=True) + jnp.exp(sink - mx)
                probs.append(e * (1.0 / den))
            pm = jnp.concatenate(probs, axis=0).astype(BF16)
            o = jnp.dot(pm, vd[kv], preferred_element_type=F32)
            yb_slabs.append(jnp.where(lo, o[0:CHUNK], o[CHUNK:2 * CHUNK]))
            yb_slabs.append(jnp.where(lo, o[2 * CHUNK:3 * CHUNK], o[3 * CHUNK:4 * CHUNK]))
        yb_n = _rms(jnp.concatenate(yb_slabs, axis=-1), gob_ref[...])
        cat_s[rows, 0:A_WIDTH] = ya_n.astype(BF16)
        cat_s[rows, A_WIDTH:] = yb_n.astype(BF16)
        return carry

    lax.fori_loop(0, TM // CHUNK, chunk_body, 0)

    kv_s[0:CHUNK, :] = kv_s[TM:TM + CHUNK, :]
    k_ref[...] = kv_s[TM:TM + CHUNK, 0:KV_WIDTH]
    v_ref[...] = kv_s[TM:TM + CHUNK, KV_WIDTH:]

    h1 = x_ref[...] + jnp.dot(cat_s[...], wo_ref[...], preferred_element_type=F32)
    h1_ref[...] = h1
    xn2 = _rms(h1, gmoe_ref[...])
    xn_ref[...] = xn2.astype(BF16)
    gates, sel = _route(xn2, wr_ref, br_ref)
    gm_ref[...] = gates
    sm_ref[...] = sel


def _full(shape):
    n = len(shape)
    return pl.BlockSpec(shape, lambda *_: (0,) * n)


def _prompt_call(x, sinks, gmix, win, lng, lnb, wsp, bsp, goa, gob, wo, gmoe, wr, br):
    real = N_PROMPT // TM
    row = lambda g: (g, 0)
    seq = lambda g: (jnp.minimum(g, real - 1) // (SEQ // TM), 0, 0)
    return pl.pallas_call(
        _prompt_kernel,
        grid=(N_PAD // TM,),
        in_specs=[
            pl.BlockSpec(memory_space=pltpu.SMEM),
            pl.BlockSpec((TM, D_MODEL), lambda g: (jnp.minimum(g, real - 1), 0)),
            _full((1, D_MODEL)), _full((D_MODEL, IN_WIDTH)), _full((1, A_WIDTH)), _full((1, A_WIDTH)),
            _full((8, CHUNK, CHUNK)), _full((CHUNK, A_WIDTH)), _full((1, A_WIDTH)), _full((1, B_WIDTH)),
            _full((D_MODEL, D_MODEL)), _full((1, D_MODEL)), _full((D_MODEL, LANES)), _full((1, LANES)),
        ],
        out_specs=[
            pl.BlockSpec((TM, D_MODEL), row),
            pl.BlockSpec((TM, D_MODEL), row),
            pl.BlockSpec((TM, LANES), row),
            pl.BlockSpec((TM, LANES), row),
            pl.BlockSpec((None, CHUNK, KV_WIDTH), seq),
            pl.BlockSpec((None, CHUNK, KV_WIDTH), seq),
        ],
        out_shape=[
            jax.ShapeDtypeStruct((N_PAD, D_MODEL), F32),
            jax.ShapeDtypeStruct((N_PAD, D_MODEL), BF16),
            jax.ShapeDtypeStruct((N_PAD, LANES), F32),
            jax.ShapeDtypeStruct((N_PAD, LANES), F32),
            jax.ShapeDtypeStruct((BATCH, CHUNK, KV_WIDTH), F32),
            jax.ShapeDtypeStruct((BATCH, CHUNK, KV_WIDTH), F32),
        ],
        scratch_shapes=[
            pltpu.VMEM((TM, IN_WIDTH), F32),
            pltpu.VMEM((TM + CHUNK, 2 * KV_WIDTH), F32),
            pltpu.VMEM((TM, D_MODEL), BF16),
        ],
        compiler_params=pltpu.CompilerParams(
            dimension_semantics=("arbitrary",), vmem_limit_bytes=VMEM_LIMIT),
        name="prompt_premoe",
    )(sinks, x, gmix, win, lng, lnb, wsp, bsp, goa, gob, wo, gmoe, wr, br)


def _sample_kernel(sinks_ref, x_ref, ck_ref, cv_ref, gmix_ref, win_ref, lng_ref, lnb_ref, w00_ref, b0_ref,
                   goa_ref, gob_ref, wo_ref, gmoe_ref, wr_ref, br_ref,
                   h1_in, xn_in, gm_in, sm_in,
                   h1_ref, xn_ref, gm_ref, sm_ref, nk_ref, nv_ref, va_ref):
    del h1_in, xn_in, gm_in, sm_in
    t = SAMPLE_TILE
    nkeys = t * CHUNK

    if True:
        x = x_ref[...]
        xn = _rms(x, gmix_ref[...]).astype(BF16)
        z = jnp.dot(xn, win_ref[...], preferred_element_type=F32)
        u = _gelu(z[:, 0:A_WIDTH])
        va = _layernorm(_gelu(z[:, A_WIDTH:2 * A_WIDTH]), lng_ref[...], lnb_ref[...])
        va_ref[...] = va
        ya_n = _rms(u * (w00_ref[...] * va + b0_ref[...]), goa_ref[...])

        knew = z[:, 2 * A_WIDTH + B_WIDTH:2 * A_WIDTH + B_WIDTH + KV_WIDTH]
        vnew = z[:, 2 * A_WIDTH + B_WIDTH + KV_WIDTH:]
        lane = _lane_iota((t, LANES))
        lo = lane < HEAD_DIM
        stacked = []
        for h in range(B_HEADS):
            q = z[:, 2 * A_WIDTH + (h // 2) * LANES:2 * A_WIDTH + (h // 2 + 1) * LANES]
            qh = jnp.where(lo if h % 2 == 0 else jnp.logical_not(lo), q, 0.0)
            if h % 2 != h // 4:
                qh = pltpu.roll(qh, HEAD_DIM, 1)
            stacked.append(qh)
        qs = jnp.concatenate(stacked, axis=0)
        rows = B_HEADS * t
        ridx = lax.broadcasted_iota(jnp.int32, (rows, 1), 0)
        slope = jnp.zeros((rows, 1), F32)
        sink = jnp.zeros((rows, 1), F32)
        for h in range(B_HEADS):
            in_h = (ridx >= h * t) & (ridx < (h + 1) * t)
            slope = jnp.where(in_h, 2.0 ** (-(h + 1)), slope)
            sink = jnp.where(in_h, sinks_ref[h], sink)
        s_c = lax.dot_general(qs.astype(BF16), ck_ref[...].astype(BF16), (((1,), (1,)), ((), ())),
                              preferred_element_type=F32)
        rsamp = lax.broadcasted_iota(jnp.int32, (rows, nkeys), 0) % t
        col = lax.broadcasted_iota(jnp.int32, (rows, nkeys), 1)
        pos = col % CHUNK
        own = ((col // CHUNK) == rsamp) & (pos >= 1)
        s_c = s_c * (HEAD_DIM ** -0.5) - slope * (CHUNK - pos).astype(F32)
        s_c = jnp.where(own, s_c, NEG)
        kn8 = jnp.concatenate([knew] * B_HEADS, axis=0)
        vn8 = jnp.concatenate([vnew] * B_HEADS, axis=0)
        s_n = jnp.sum(qs * kn8, axis=-1, keepdims=True) * (HEAD_DIM ** -0.5)
        mx = jnp.maximum(jnp.maximum(jnp.max(s_c, axis=-1, keepdims=True), s_n), sink)
        e_c = jnp.exp(s_c - mx)
        e_n = jnp.exp(s_n - mx)
        inv = 1.0 / (jnp.sum(e_c, axis=-1, keepdims=True) + e_n + jnp.exp(sink - mx))
        o = jnp.dot((e_c * inv).astype(BF16), cv_ref[...].astype(BF16), preferred_element_type=F32)
        o = o + (e_n * inv) * vn8
        yb_slabs = []
        for p in range(B_WIDTH // LANES):
            outs = []
            for half in range(2):
                h = 2 * p + half
                oh = o[h * t:(h + 1) * t]
                oh = jnp.where(lo if h // 4 == 0 else jnp.logical_not(lo), oh, 0.0)
                if half != h // 4:
                    oh = pltpu.roll(oh, HEAD_DIM, 1)
                outs.append(oh)
            yb_slabs.append(outs[0] + outs[1])
        yb_n = _rms(jnp.concatenate(yb_slabs, axis=-1), gob_ref[...])

        cat = jnp.concatenate([ya_n, yb_n], axis=-1).astype(BF16)
        h1 = x + jnp.dot(cat, wo_ref[...], preferred_element_type=F32)
        xn2 = _rms(h1, gmoe_ref[...])
        gates, sel = _route(xn2, wr_ref, br_ref)
        h1_ref[...] = h1
        xn_ref[...] = xn2.astype(BF16)
        gm_ref[...] = gates
        sm_ref[...] = sel

        nk_ref[...] = pltpu.roll(ck_ref[...], nkeys - 1, 0)
        nv_ref[...] = pltpu.roll(cv_ref[...], nkeys - 1, 0)
        for b in range(t):
            nk_ref[b * CHUNK + CHUNK - 1:b * CHUNK + CHUNK, :] = knew[b:b + 1, :]
            nv_ref[b * CHUNK + CHUNK - 1:b * CHUNK + CHUNK, :] = vnew[b:b + 1, :]


def _sample_call(x, ck, cv, sinks, gmix, win, lng, lnb, w00, b0, goa, gob, wo, gmoe, wr, br, h1, xn, gm, sm):
    t = SAMPLE_TILE
    steps = DEC_BATCH // t
    base = N_PROMPT // t
    inrow = lambda i: (i, 0)
    outrow = lambda i: (base + i, 0)
    anyspec = pl.BlockSpec(memory_space=pl.ANY)
    return pl.pallas_call(
        _sample_kernel,
        grid=(steps,),
        in_specs=[
            pl.BlockSpec(memory_space=pltpu.SMEM),
            pl.BlockSpec((t, D_MODEL), inrow),
            pl.BlockSpec((t * CHUNK, KV_WIDTH), inrow),
            pl.BlockSpec((t * CHUNK, KV_WIDTH), inrow),
            _full((1, D_MODEL)), _full((D_MODEL, IN_WIDTH)), _full((1, A_WIDTH)), _full((1, A_WIDTH)),
            _full((1, A_WIDTH)), _full((1, A_WIDTH)), _full((1, A_WIDTH)), _full((1, B_WIDTH)),
            _full((D_MODEL, D_MODEL)), _full((1, D_MODEL)), _full((D_MODEL, LANES)), _full((1, LANES)),
            anyspec, anyspec, anyspec, anyspec,
        ],
        out_specs=[
            pl.BlockSpec((t, D_MODEL), outrow),
            pl.BlockSpec((t, D_MODEL), outrow),
            pl.BlockSpec((t, LANES), outrow),
            pl.BlockSpec((t, LANES), outrow),
            pl.BlockSpec((t * CHUNK, KV_WIDTH), inrow),
            pl.BlockSpec((t * CHUNK, KV_WIDTH), inrow),
            pl.BlockSpec((t, A_WIDTH), inrow),
        ],
        out_shape=[
            jax.ShapeDtypeStruct((N_PAD, D_MODEL), F32),
            jax.ShapeDtypeStruct((N_PAD, D_MODEL), BF16),
            jax.ShapeDtypeStruct((N_PAD, LANES), F32),
            jax.ShapeDtypeStruct((N_PAD, LANES), F32),
            jax.ShapeDtypeStruct((DEC_BATCH * CHUNK, KV_WIDTH), F32),
            jax.ShapeDtypeStruct((DEC_BATCH * CHUNK, KV_WIDTH), F32),
            jax.ShapeDtypeStruct((DEC_BATCH, A_WIDTH), F32),
        ],
        input_output_aliases={16: 0, 17: 1, 18: 2, 19: 3},
        compiler_params=pltpu.CompilerParams(
            dimension_semantics=("arbitrary",), vmem_limit_bytes=VMEM_LIMIT),
        name="sample_premoe",
    )(sinks, x, ck, cv, gmix, win, lng, lnb, w00, b0, goa, gob, wo, gmoe, wr, br, h1, xn, gm, sm)


def _plan_kernel(sm_ref,
                 destm_ref, destt_ref, xbe_ref, nxblk_ref, stab_ref, ctab_ref, astart_ref, nwin_ref,
                 base_s, pstart_s):
    ph = pl.program_id(0)
    step = pl.program_id(1)
    lane = _lane_iota((1, LANES))

    @pl.when((ph == 0) & (step == 0))
    def _():
        base_s[...] = jnp.zeros_like(base_s)

    @pl.when(ph == 0)
    def _():
        base_s[...] += jnp.sum(sm_ref[...], axis=0, keepdims=True)

    @pl.when((ph == 1) & (step == 0))
    def _():
        counts = base_s[...]
        padded = jnp.floor((counts + (DISP_CHUNK + EXP_BLOCK - 1)) * (1.0 / EXP_BLOCK)) * EXP_BLOCK
        padded = jnp.where(counts > 0.0, padded, 0.0)
        pend = padded
        for s in (1, 2, 4, 8, 16):
            pend = pend + jnp.where(lane >= s, pltpu.roll(pend, s, 1), 0.0)
        pstart_s[...] = pend - padded
        base_s[...] = jnp.zeros_like(base_s)
        brow = lax.broadcasted_iota(jnp.int32, (XBE_ROWS, LANES), 0).astype(F32) * EXP_BLOCK
        done = jnp.where((lane < N_EXPERTS) & (pend <= brow), 1.0, 0.0)
        be = jnp.minimum(jnp.sum(done, axis=-1, keepdims=True), N_EXPERTS - 1.0)
        xbe_ref[...] = jnp.broadcast_to(be, (XBE_ROWS, LANES)).astype(jnp.int32)
        total = jnp.sum(jnp.where(lane == N_EXPERTS - 1, pend, 0.0), axis=-1, keepdims=True)
        nxblk_ref[...] = jnp.broadcast_to(total * (1.0 / EXP_BLOCK), (8, LANES)).astype(jnp.int32)
        stab_ref[...] = jnp.zeros_like(stab_ref)
        ctab_ref[...] = jnp.zeros_like(ctab_ref)
        astart_ref[...] = jnp.zeros_like(astart_ref)
        nwin_ref[...] = jnp.zeros_like(nwin_ref)

    @pl.when(ph == 1)
    def _():
        r = lax.broadcasted_iota(jnp.int32, (TOK_TILE, TOK_TILE), 0)
        c = lax.broadcasted_iota(jnp.int32, (TOK_TILE, TOK_TILE), 1)
        lower = jnp.where(c < r, 1.0, 0.0).astype(BF16)
        for q in range(TM // TOK_TILE):
            i = step * (TM // TOK_TILE) + q
            sel = sm_ref[q * TOK_TILE:(q + 1) * TOK_TILE, :]
            cnt = jnp.sum(sel, axis=0, keepdims=True)
            prefix = jnp.dot(lower, sel.astype(BF16), preferred_element_type=F32)
            start = pstart_s[...] + base_s[...]
            dest = jnp.where(sel > 0.0, prefix + start, -1.0)
            destm_ref[q * TOK_TILE:(q + 1) * TOK_TILE, :] = dest
            destt_ref[:, q * TOK_TILE:(q + 1) * TOK_TILE] = dest.T
            has = (cnt > 0.0) & (lane < N_EXPERTS)
            stab_ref[pl.ds(i, 1), :] = start.astype(jnp.int32)
            ctab_ref[pl.ds(i, 1), :] = jnp.where(has, cnt, 0.0).astype(jnp.int32)
            a = jnp.minimum(jnp.floor(start * (1.0 / WIN_ALIGN)) * WIN_ALIGN, float(N_ROWS - WIN))
            nw = jnp.where(has, jnp.floor((start + cnt - a + (WIN - 1)) * (1.0 / WIN)), 0.0)
            astart_ref[pl.ds(i, 1), :] = a.astype(jnp.int32)
            nwin_ref[pl.ds(i, 1), :] = nw.astype(jnp.int32)
            base_s[...] += cnt


def _plan_call(sm):
    tile = lambda ph, i: (i * ph, 0)
    tile_t = lambda ph, i: (0, i * ph)
    tab = jax.ShapeDtypeStruct((TAB_ROWS, LANES), jnp.int32)
    return pl.pallas_call(
        _plan_kernel,
        grid=(2, N_PAD // TM),
        in_specs=[pl.BlockSpec((TM, LANES), lambda ph, i: (i, 0))],
        out_specs=[
            pl.BlockSpec((TM, LANES), tile),
            pl.BlockSpec((LANES, TM), tile_t),
            _full((XBE_ROWS, LANES)), _full((8, LANES)),
            _full((TAB_ROWS, LANES)), _full((TAB_ROWS, LANES)), _full((TAB_ROWS, LANES)), _full((TAB_ROWS, LANES)),
        ],
        out_shape=[
            jax.ShapeDtypeStruct((N_PAD, LANES), F32),
            jax.ShapeDtypeStruct((LANES, N_PAD), F32),
            jax.ShapeDtypeStruct((XBE_ROWS, LANES), jnp.int32),
            jax.ShapeDtypeStruct((8, LANES), jnp.int32),
            tab, tab, tab, tab,
        ],
        scratch_shapes=[pltpu.VMEM((1, LANES), F32), pltpu.VMEM((1, LANES), F32)],
        compiler_params=pltpu.CompilerParams(
            dimension_semantics=("arbitrary", "arbitrary"), vmem_limit_bytes=VMEM_LIMIT),
        name="moe_plan",
    )(sm)


def _pack_rows(z):
    half = D_MODEL // 2
    lo = lax.bitcast_convert_type(z[:, :half], jnp.uint32) >> 16
    hi = lax.bitcast_convert_type(z[:, half:], jnp.uint32) & jnp.uint32(0xFFFF0000)
    return lax.bitcast_convert_type(hi | lo, jnp.int32)


def _unpack_rows(ref):
    rows = ref.shape[0]
    flat = ref.reshape(rows * PACK, LANES)
    lo, hi = [], []
    for s in range(PACK):
        w = lax.bitcast_convert_type(flat[pl.ds(s, rows, stride=PACK), :], jnp.uint32)
        lo.append(lax.bitcast_convert_type(w << 16, F32))
        hi.append(lax.bitcast_convert_type(w & jnp.uint32(0xFFFF0000), F32))
    return jnp.concatenate(lo + hi, axis=-1).astype(BF16)


def _dispatch_kernel(stab_ref, ctab_ref, cmax_ref, xn_ref, destt_ref, xs_in, xs_ref,
                     stage0, stage1, stage2, sems, sem2):
    del xs_in
    i = pl.program_id(0)
    last = pl.num_programs(0) - 1
    x = xn_ref[...]
    dt = destt_ref[...]
    rio = lax.broadcasted_iota(jnp.int32, (DISP_CHUNK, 1), 0).astype(F32)

    def chunk_rows(j, stage):
        parts = []
        for e in range(N_EXPERTS):
            first = (stab_ref[i * N_EXPERTS + e] + j * DISP_CHUNK).astype(F32)
            parts.append(jnp.where(dt[e:e + 1, :] == first + rio, 1.0, 0.0).astype(BF16))
        onehot = jnp.concatenate(parts, axis=0)
        words = _pack_rows(jnp.dot(onehot, x, preferred_element_type=F32))
        for s in range(PACK):
            stage[pl.ds(s, N_EXPERTS * DISP_CHUNK, stride=PACK), :] = words[:, s * LANES:(s + 1) * LANES]

    def copy(stage, step, e, j, sem):
        first = stab_ref[step * N_EXPERTS + e] + j * DISP_CHUNK
        rows = stage.reshape(N_EXPERTS * DISP_CHUNK, PACK, LANES)
        return pltpu.make_async_copy(rows.at[pl.ds(e * DISP_CHUNK, DISP_CHUNK)],
                                     xs_ref.at[pl.ds(first, DISP_CHUNK)], sem)

    def step_body(stage, prev_stage, par):
        chunk_rows(0, stage)
        for e in range(N_EXPERTS):

            @pl.when((i > 0) & (ctab_ref[jnp.maximum(i - 1, 0) * N_EXPERTS + e] > 0))
            def _(e=e):
                copy(prev_stage, i - 1, e, 0, sems.at[1 - par, e]).wait()

        for e in range(N_EXPERTS):

            @pl.when(ctab_ref[i * N_EXPERTS + e] > 0)
            def _(e=e):
                copy(stage, i, e, 0, sems.at[par, e]).start()

        for e in range(N_EXPERTS):

            @pl.when((i == last) & (ctab_ref[i * N_EXPERTS + e] > 0))
            def _(e=e):
                copy(stage, i, e, 0, sems.at[par, e]).wait()

    @pl.when(i % 2 == 0)
    def _():
        step_body(stage0, stage1, 0)

    @pl.when(i % 2 == 1)
    def _():
        step_body(stage1, stage0, 1)

    for j in range(1, TOK_TILE // DISP_CHUNK):

        @pl.when(cmax_ref[i] > j * DISP_CHUNK)
        def _(j=j):
            chunk_rows(j, stage2)
            for e in range(N_EXPERTS):

                @pl.when(ctab_ref[i * N_EXPERTS + e] > j * DISP_CHUNK)
                def _(e=e):
                    cp = copy(stage2, i, e, j, sem2)
                    cp.start()
                    cp.wait()


def _dispatch_call(stab, ctab, cmax, xn, destt, xs_zero):
    stage = pltpu.VMEM((N_EXPERTS * DISP_CHUNK * PACK, LANES), jnp.int32)
    grid_spec = pltpu.PrefetchScalarGridSpec(
        num_scalar_prefetch=3,
        grid=(N_PAD_TILES,),
        in_specs=[
            pl.BlockSpec((TOK_TILE, D_MODEL), lambda i, *_: (i, 0)),
            pl.BlockSpec((N_EXPERTS, TOK_TILE), lambda i, *_: (0, i)),
            pl.BlockSpec(memory_space=pl.ANY),
        ],
        out_specs=pl.BlockSpec(memory_space=pl.ANY),
        scratch_shapes=[stage, stage, stage, pltpu.SemaphoreType.DMA((2, N_EXPERTS)),
                        pltpu.SemaphoreType.DMA],
    )
    return pl.pallas_call(
        _dispatch_kernel,
        grid_spec=grid_spec,
        out_shape=jax.ShapeDtypeStruct((N_ROWS, PACK, LANES), jnp.int32),
        input_output_aliases={5: 0},
        compiler_params=pltpu.CompilerParams(
            dimension_semantics=("arbitrary",), vmem_limit_bytes=VMEM_LIMIT),
        name="moe_dispatch",
    )(stab, ctab, cmax, xn, destt, xs_zero)


def _expert_kernel(blke_ref, nblk_ref, next_ref, xs_ref, wgu_hbm, bgu_ref, wdn_hbm, bdn_ref,
                   ys_ref, wgu_f, wdn_f, wgu_s, wdn_s, sems):
    b = pl.program_id(0)
    used = b < nblk_ref[0]
    prev = blke_ref[jnp.maximum(b - 1, 0)]
    fresh = used & ((b == 0) | (blke_ref[b] != prev))

    def fetch(e):
        return (pltpu.make_async_copy(wgu_hbm.at[e], wgu_f, sems.at[0]),
                pltpu.make_async_copy(wdn_hbm.at[e], wdn_f, sems.at[1]))

    @pl.when(b == 0)
    def _():
        for cp in fetch(blke_ref[0]):
            cp.start()

    @pl.when(fresh)
    def _():
        for cp in fetch(blke_ref[b]):
            cp.wait()
        wgu_s[...] = wgu_f[...].astype(BF16)
        wdn_s[...] = wdn_f[...].astype(BF16)

        @pl.when(next_ref[b] >= 0)
        def _():
            for cp in fetch(next_ref[b]):
                cp.start()

    @pl.when(used)
    def _():
        hid = jnp.dot(_unpack_rows(xs_ref), wgu_s[...], preferred_element_type=F32) + bgu_ref[...]
        gate = jnp.minimum(hid[:, :D_FF], SWIGLU_LIMIT)
        up = jnp.clip(hid[:, D_FF:], -SWIGLU_LIMIT, SWIGLU_LIMIT)
        act = (up + 1.0) * gate * jax.nn.sigmoid(SWIGLU_ALPHA * gate)
        y = jnp.dot(act.astype(BF16), wdn_s[...], preferred_element_type=F32) + bdn_ref[...]
        ys_ref[...] = y.astype(BF16)

    @pl.when(b >= nblk_ref[0])
    def _():
        ys_ref[...] = jnp.zeros_like(ys_ref)


def _expert_call(blke, nblk, nxt, xs, wgu, bgu, wdn, bdn):
    grid_spec = pltpu.PrefetchScalarGridSpec(
        num_scalar_prefetch=3,
        grid=(N_XBLOCKS,),
        in_specs=[
            pl.BlockSpec((EXP_BLOCK, PACK, LANES), lambda b, be, nb, nx: (b, 0, 0)),
            pl.BlockSpec(memory_space=pl.ANY),
            pl.BlockSpec((None, 1, 2 * D_FF), lambda b, be, nb, nx: (be[b], 0, 0)),
            pl.BlockSpec(memory_space=pl.ANY),
            pl.BlockSpec((None, 1, D_MODEL), lambda b, be, nb, nx: (be[b], 0, 0)),
        ],
        out_specs=pl.BlockSpec((EXP_BLOCK, D_MODEL), lambda b, be, nb, nx: (b, 0)),
        scratch_shapes=[pltpu.VMEM((D_MODEL, 2 * D_FF), F32), pltpu.VMEM((D_FF, D_MODEL), F32),
                        pltpu.VMEM((D_MODEL, 2 * D_FF), BF16), pltpu.VMEM((D_FF, D_MODEL), BF16),
                        pltpu.SemaphoreType.DMA((2,))],
    )
    return pl.pallas_call(
        _expert_kernel,
        grid_spec=grid_spec,
        out_shape=jax.ShapeDtypeStruct((N_ROWS, D_MODEL), BF16),
        compiler_params=pltpu.CompilerParams(
            dimension_semantics=("arbitrary",), vmem_limit_bytes=VMEM_LIMIT),
        name="moe_experts",
    )(blke, nblk, nxt, xs, wgu, bgu, wdn, bdn)


def _combine_kernel(*refs):
    astart_ref, nwin_ref, over_ref = refs[0:3]
    win_refs = refs[3:3 + N_EXPERTS]
    (destm_ref, gm_ref, h1_ref, plep_ref, ples_ref, gple_ref, wpg_ref, wpp_ref, gfin_ref, ys_any,
     yp_ref, ysm_ref, moe_s, tmp_s, sem) = refs[3 + N_EXPERTS:]
    i = pl.program_id(0)
    dest = destm_ref[...]
    gates = gm_ref[...]
    lane = _lane_iota((TOK_TILE, LANES))
    lane_f = lane.astype(F32)
    lo = lane < WIN
    moe = jnp.zeros((TOK_TILE, D_MODEL), F32)
    group = 4
    for g0 in range(0, N_EXPERTS, group):
        g_hi, g_lo = [], []
        for p in range(group // 2):
            e0 = g0 + 2 * p
            a0 = astart_ref[i * N_EXPERTS + e0].astype(F32)
            a1 = astart_ref[i * N_EXPERTS + e0 + 1].astype(F32)
            rowid = jnp.where(lo, a0 + lane_f, a1 + lane_f - WIN)
            dcol = jnp.where(lo, dest[:, e0:e0 + 1], dest[:, e0 + 1:e0 + 2])
            gcol = jnp.where(lo, gates[:, e0:e0 + 1], gates[:, e0 + 1:e0 + 2])
            gsel = jnp.where(dcol == rowid, gcol, 0.0)
            hi = gsel.astype(BF16)
            g_hi.append(hi)
            g_lo.append((gsel - hi.astype(F32)).astype(BF16))
        ywin = jnp.concatenate([win_refs[g0 + q][...] for q in range(group)], axis=0)
        both = jnp.concatenate([jnp.concatenate(g_hi, axis=-1), jnp.concatenate(g_lo, axis=-1)], axis=0)
        r = jnp.dot(both, ywin, preferred_element_type=F32)
        moe = moe + r[:TOK_TILE] + r[TOK_TILE:]
    moe_s[...] = moe

    @pl.when(over_ref[i] > 0)
    def _():
        tmp_s[...] = jnp.zeros_like(tmp_s)

        def per_expert(e, carry):
            a = astart_ref[i * N_EXPERTS + e]
            dcol = jnp.sum(jnp.where(lane == e, dest, 0.0), axis=-1, keepdims=True)
            gcol = jnp.sum(jnp.where(lane == e, gates, 0.0), axis=-1, keepdims=True)

            def per_window(w, carry2):
                first = a + w * WIN
                start = pl.multiple_of(jnp.minimum(first, N_ROWS - WIN), WIN_ALIGN)
                cp = pltpu.make_async_copy(ys_any.at[pl.ds(start, WIN)], tmp_s.at[pl.ds(0, WIN)], sem)
                cp.start()
                cp.wait()
                hit = lo & (dcol == start.astype(F32) + lane_f) & (dcol >= first.astype(F32))
                gsel = jnp.where(hit, gcol, 0.0)
                hi = gsel.astype(BF16)
                rest = (gsel - hi.astype(F32)).astype(BF16)
                moe_s[...] += (jnp.dot(hi, tmp_s[...], preferred_element_type=F32)
                               + jnp.dot(rest, tmp_s[...], preferred_element_type=F32))
                return carry2

            return lax.fori_loop(1, nwin_ref[i * N_EXPERTS + e], per_window, carry)

        lax.fori_loop(0, N_EXPERTS, per_expert, 0)

    is_sample = i == N_TILES - 1
    h2 = h1_ref[...] + moe_s[...]
    ple = jnp.where(is_sample, ples_ref[...], plep_ref[...])
    hn = _rms(h2, gple_ref[...]).astype(BF16)
    gate = jax.nn.sigmoid(jnp.dot(hn, wpg_ref[...], preferred_element_type=F32))
    proj = jnp.dot(ple.astype(BF16), wpp_ref[...], preferred_element_type=F32)
    y = _rms(h2 + gate * proj, gfin_ref[...])

    @pl.when(jnp.logical_not(is_sample))
    def _():
        yp_ref[...] = y

    @pl.when(is_sample)
    def _():
        ysm_ref[...] = y


def _combine_call(astart, nwin, over, ys, destm, gm, h1, plep, ples, gple, wpg, wpp, gfin):
    last = N_TILES - 2

    def win_spec(e):
        return pl.BlockSpec((pl.Element(WIN), pl.Element(D_MODEL)),
                            lambda i, a, nw, ov, e=e: (pl.multiple_of(a[i * N_EXPERTS + e], WIN_ALIGN), 0))

    grid_spec = pltpu.PrefetchScalarGridSpec(
        num_scalar_prefetch=3,
        grid=(N_TILES,),
        in_specs=[win_spec(e) for e in range(N_EXPERTS)] + [
            pl.BlockSpec((TOK_TILE, LANES), lambda i, *_: (i, 0)),
            pl.BlockSpec((TOK_TILE, LANES), lambda i, *_: (i, 0)),
            pl.BlockSpec((TOK_TILE, D_MODEL), lambda i, *_: (i, 0)),
            pl.BlockSpec((TOK_TILE, PLE_DIM), lambda i, *_: (jnp.minimum(i, last), 0)),
            pl.BlockSpec((TOK_TILE, PLE_DIM), lambda i, *_: (0, 0)),
            pl.BlockSpec((1, D_MODEL), lambda i, *_: (0, 0)),
            pl.BlockSpec((D_MODEL, D_MODEL), lambda i, *_: (0, 0)),
            pl.BlockSpec((PLE_DIM, D_MODEL), lambda i, *_: (0, 0)),
            pl.BlockSpec((1, D_MODEL), lambda i, *_: (0, 0)),
            pl.BlockSpec(memory_space=pl.ANY),
        ],
        out_specs=[
            pl.BlockSpec((TOK_TILE, D_MODEL), lambda i, *_: (jnp.minimum(i, last), 0)),
            pl.BlockSpec((TOK_TILE, D_MODEL), lambda i, *_: (0, 0)),
        ],
        scratch_shapes=[pltpu.VMEM((TOK_TILE, D_MODEL), F32), pltpu.VMEM((2 * WIN, D_MODEL), BF16),
                        pltpu.SemaphoreType.DMA],
    )
    return pl.pallas_call(
        _combine_kernel,
        grid_spec=grid_spec,
        out_shape=[jax.ShapeDtypeStruct((N_PROMPT, D_MODEL), F32),
                   jax.ShapeDtypeStruct((DEC_BATCH, D_MODEL), F32)],
        compiler_params=pltpu.CompilerParams(
            dimension_semantics=("arbitrary",), vmem_limit_bytes=VMEM_LIMIT),
        name="moe_combine_tail",
    )(astart, nwin, over, *([ys] * N_EXPERTS), destm, gm, h1, plep, ples, gple, wpg, wpp, gfin, ys)


def kernel(x_prompt, x_sample, cache_swa_k, cache_swa_v, p_prompt, p_sample, g_mix, w_in, ln_v_g, ln_v_b,
           w_sp, b_sp, sinks, g_out_a, g_out_b, w_o, g_moe, w_router, b_router, w_gu, b_gu, w_dn, b_dn,
           g_ple, w_ple_gate, w_ple_proj, g_final):
    l = 0
    row = lambda v: v.reshape(1, -1)
    win = w_in[l].astype(BF16)
    wo = w_o[l].astype(BF16)
    tril = jnp.tril(jnp.ones((CHUNK, CHUNK), dtype=bool))
    wsp = jnp.where(tril, w_sp[l], 0.0).astype(BF16)
    bsp = jnp.repeat(b_sp[l].T, HEAD_DIM, axis=1)
    w00 = row(jnp.repeat(w_sp[l][:, 0, 0], HEAD_DIM))
    b0 = row(jnp.repeat(b_sp[l][:, 0], HEAD_DIM))
    wr_hi = w_router[l].astype(BF16)
    wr_lo = (w_router[l] - wr_hi.astype(F32)).astype(BF16)
    wr = jnp.concatenate([wr_hi, wr_lo, jnp.zeros((D_MODEL, LANES - 2 * N_EXPERTS), BF16)], axis=1)
    br = row(jnp.concatenate([b_router[l], jnp.zeros((LANES - N_EXPERTS,), F32)]))
    common = (row(g_mix[l]), win, row(ln_v_g[l]), row(ln_v_b[l]))
    tail = (row(g_out_a[l]), row(g_out_b[l]), wo, row(g_moe[l]), wr, br)

    h1, xn, gm, sm, k_p, v_p = _prompt_call(
        x_prompt.reshape(N_PROMPT, D_MODEL), sinks[l], *common, wsp, bsp, *tail)
    ck = cache_swa_k[l].reshape(DEC_BATCH * CHUNK, KV_WIDTH)
    cv = cache_swa_v[l].reshape(DEC_BATCH * CHUNK, KV_WIDTH)
    h1, xn, gm, sm, k_s, v_s, va_s = _sample_call(
        x_sample.reshape(DEC_BATCH, D_MODEL), ck, cv, sinks[l], *common, w00, b0, *tail, h1, xn, gm, sm)

    destm, destt, xbe, nxblk, stab, ctab, astart, nwin = _plan_call(sm)
    flat = lambda tab, n: tab[:n, :N_EXPERTS].reshape(-1)
    astart1 = flat(astart, N_TILES)
    nwin2 = nwin[:N_TILES, :N_EXPERTS]
    over1 = (jnp.max(nwin2, axis=1) > 1).astype(jnp.int32)
    nwin1 = nwin2.reshape(-1)
    cmax1 = jnp.max(ctab[:N_PAD_TILES, :N_EXPERTS], axis=1)

    xs = _dispatch_call(flat(stab, N_PAD_TILES), flat(ctab, N_PAD_TILES), cmax1, xn, destt[:N_EXPERTS],
                        jnp.zeros((N_ROWS, PACK, LANES), jnp.int32))
    xbe1, nxblk1 = xbe[:N_XBLOCKS, 0], nxblk[0, :1]
    blk = jnp.arange(N_XBLOCKS, dtype=jnp.int32)
    starts = (blk < nxblk1[0]) & ((blk == 0) | (xbe1 != jnp.roll(xbe1, 1)))
    pos = jnp.where(starts, blk, N_XBLOCKS)
    nxt_pos = jnp.roll(lax.cummin(pos, reverse=True), -1).at[N_XBLOCKS - 1].set(N_XBLOCKS)
    nxt1 = jnp.where(nxt_pos < N_XBLOCKS, xbe1[jnp.minimum(nxt_pos, N_XBLOCKS - 1)], -1).astype(jnp.int32)
    ys = _expert_call(xbe1, nxblk1, nxt1, xs, w_gu[l], b_gu[l].reshape(N_EXPERTS, 1, 2 * D_FF),
                      w_dn[l], b_dn[l].reshape(N_EXPERTS, 1, D_MODEL))
    y_p, y_s = _combine_call(
        astart1, nwin1, over1, ys, destm, gm, h1,
        p_prompt[l].reshape(N_PROMPT, PLE_DIM), p_sample[l].reshape(DEC_BATCH, PLE_DIM),
        row(g_ple[l]), w_ple_gate[l].astype(BF16), w_ple_proj[l].astype(BF16), row(g_final))

    kv5 = lambda a, n: a.reshape(1, n, CHUNK, 2, HEAD_DIM)
    return (y_p.reshape(BATCH, SEQ, D_MODEL), y_s.reshape(DEC_BATCH, 1, D_MODEL),
            kv5(k_p, BATCH), kv5(v_p, BATCH), kv5(k_s, DEC_BATCH), kv5(v_s, DEC_BATCH),
            va_s.reshape(1, DEC_BATCH, 1, A_WIDTH))
```

```python
import math

import jax
import jax.numpy as jnp
from jax import lax
from jax.experimental import pallas as pl
from jax.experimental.pallas import tpu as pltpu

F32 = jnp.float32
BF16 = jnp.bfloat16

D_MODEL = 1024
BATCH = 4
SEQ = 4096
DEC_BATCH = 128
HEAD_DIM = 64
A_WIDTH = 512
B_WIDTH = 512
B_HEADS = 8
KV_WIDTH = 128
IN_WIDTH = 2 * A_WIDTH + B_WIDTH + 2 * KV_WIDTH
CHUNK = 128
N_EXPERTS = 32
TOP_K = 4
D_FF = 1024
SWIGLU_ALPHA = 1.702
SWIGLU_LIMIT = 7.0
PLE_DIM = 256
EPS = 1e-5

LANES = 128
ROW_BLOCK = 128
EXP_BLOCK = 512
N_PROMPT = BATCH * SEQ
N_TOK = N_PROMPT + DEC_BATCH
TOK_TILE = 128
N_TILES = N_TOK // TOK_TILE
DISP_TILE = 256
TM = 512
N_PAD = ((N_TOK + TM - 1) // TM) * TM
N_PAD_TILES = N_PAD // TOK_TILE
DISP_CHUNK = 32
N_XBLOCKS = (N_TOK * TOP_K + N_EXPERTS * (DISP_CHUNK + EXP_BLOCK - 1) + EXP_BLOCK - 1) // EXP_BLOCK
N_ROWS = N_XBLOCKS * EXP_BLOCK
XBE_ROWS = ((N_XBLOCKS + 7) // 8) * 8
TAB_ROWS = ((N_PAD_TILES + 7) // 8) * 8
PACK = D_MODEL // 2 // LANES
WIN = 64
WIN_ALIGN = 16
SAMPLE_TILE = 32
DEC_TILE = 16
NEG = -1e30
VMEM_LIMIT = 56 * 1024 * 1024


def _rms(x, g):
    return x * lax.rsqrt(jnp.mean(x * x, axis=-1, keepdims=True) + EPS) * g


def _gelu(x):
    c = math.sqrt(2.0 / math.pi)
    return x * (0.5 * (1.0 + jnp.tanh(c * (x + 0.044715 * (x * x * x)))))


def _layernorm(x, g, b):
    mu = jnp.mean(x, axis=-1, keepdims=True)
    xc = x - mu
    return xc * lax.rsqrt(jnp.mean(xc * xc, axis=-1, keepdims=True) + EPS) * g + b


def _lane_iota(shape):
    return lax.broadcasted_iota(jnp.int32, shape, len(shape) - 1)


def _route(xn2, wr_ref, br_ref):
    m = xn2.shape[0]
    xh = xn2.astype(BF16)
    xl = (xn2 - xh.astype(F32)).astype(BF16)
    r = jnp.dot(jnp.concatenate([xh, xl], axis=0), wr_ref[...], preferred_element_type=F32)
    r = r[:m] + r[m:]
    lane = _lane_iota((m, LANES))
    lane_f = lane.astype(F32)
    logits = jnp.where(lane < N_EXPERTS, r + pltpu.roll(r, LANES - N_EXPERTS, 1) + br_ref[...], NEG)
    work = logits
    sel = jnp.zeros((m, LANES), F32)
    top = None
    z = None
    for _ in range(TOP_K):
        mx = jnp.max(work, axis=-1, keepdims=True)
        first = jnp.min(jnp.where(work == mx, lane_f, float(LANES)), axis=-1, keepdims=True)
        hit = lane_f == first
        sel = jnp.where(hit, 1.0, sel)
        work = jnp.where(hit, NEG, work)
        if top is None:
            top = mx
            z = jnp.ones_like(mx)
        else:
            z = z + jnp.exp(mx - top)
    gates = jnp.where(sel > 0.0, jnp.exp(logits - top) / z, 0.0)
    return gates, sel


def _prompt_kernel(sinks_ref, x_ref, gmix_ref, win_ref, lng_ref, lnb_ref, wsp_ref, bsp_ref,
                   goa_ref, gob_ref, wo_ref, gmoe_ref, wr_ref, br_ref,
                   h1_ref, xn_ref, gm_ref, sm_ref, k_ref, v_ref,
                   z_s, kv_s, cat_s):
    g = pl.program_id(0)
    j = g % (SEQ // TM)

    @pl.when(g >= N_PROMPT // TM)
    def _():
        h1_ref[...] = jnp.zeros_like(h1_ref)
        xn_ref[...] = jnp.zeros_like(xn_ref)
        gm_ref[...] = jnp.zeros_like(gm_ref)
        sm_ref[...] = jnp.zeros_like(sm_ref)

    @pl.when(g < N_PROMPT // TM)
    def _():
        _prompt_tile(j, sinks_ref, x_ref, gmix_ref, win_ref, lng_ref, lnb_ref, wsp_ref, bsp_ref,
                     goa_ref, gob_ref, wo_ref, gmoe_ref, wr_ref, br_ref,
                     h1_ref, xn_ref, gm_ref, sm_ref, k_ref, v_ref, z_s, kv_s, cat_s)


def _prompt_tile(j, sinks_ref, x_ref, gmix_ref, win_ref, lng_ref, lnb_ref, wsp_ref, bsp_ref,
                 goa_ref, gob_ref, wo_ref, gmoe_ref, wr_ref, br_ref,
                 h1_ref, xn_ref, gm_ref, sm_ref, k_ref, v_ref, z_s, kv_s, cat_s):
    @pl.when(j == 0)
    def _():
        kv_s[0:CHUNK, :] = jnp.zeros((CHUNK, 2 * KV_WIDTH), F32)

    xn = _rms(x_ref[...], gmix_ref[...]).astype(BF16)
    z_s[...] = jnp.dot(xn, win_ref[...], preferred_element_type=F32)
    kv_s[CHUNK:, :] = z_s[:, 2 * A_WIDTH + B_WIDTH:]

    lane = _lane_iota((CHUNK, LANES))
    lo = lane < HEAD_DIM
    lane2 = _lane_iota((2 * CHUNK, LANES))
    lo2 = lane2 < HEAD_DIM
    qi = lax.broadcasted_iota(jnp.int32, (CHUNK, 2 * CHUNK), 0)
    kj = lax.broadcasted_iota(jnp.int32, (CHUNK, 2 * CHUNK), 1)
    dist_i = qi + CHUNK - kj
    dist = dist_i.astype(F32)
    band = (dist_i >= 0) & (dist_i < CHUNK)

    def chunk_body(c, carry):
        r0 = pl.multiple_of(c * CHUNK, CHUNK)
        rows = pl.ds(r0, CHUNK)
        u = _gelu(z_s[rows, 0:A_WIDTH])
        va = _layernorm(_gelu(z_s[rows, A_WIDTH:2 * A_WIDTH]), lng_ref[...], lnb_ref[...])
        vab = va.astype(BF16)
        slabs = []
        for p in range(A_WIDTH // LANES):
            slab = vab[:, p * LANES:(p + 1) * LANES]
            m0 = jnp.dot(wsp_ref[2 * p], slab, preferred_element_type=F32)
            m1 = jnp.dot(wsp_ref[2 * p + 1], slab, preferred_element_type=F32)
            slabs.append(jnp.where(lo, m0, m1))
        ya = u * (jnp.concatenate(slabs, axis=-1) + bsp_ref[...])
        ya_n = _rms(ya, goa_ref[...])
        k2 = kv_s[pl.ds(r0, 2 * CHUNK), 0:KV_WIDTH]
        v2 = kv_s[pl.ds(r0, 2 * CHUNK), KV_WIDTH:2 * KV_WIDTH]
        k2r = pltpu.roll(k2, HEAD_DIM, 1)
        v2r = pltpu.roll(v2, HEAD_DIM, 1)
        kd = (jnp.where(lo2, k2, k2r).astype(BF16), jnp.where(lo2, k2r, k2).astype(BF16))
        vd = (jnp.where(lo2, v2, v2r).astype(BF16), jnp.where(lo2, v2r, v2).astype(BF16))
        prev_ok = (j > 0) | (c > 0)
        valid = band & ((kj >= CHUNK) | prev_ok)
        yb_slabs = []
        for kv in range(2):
            q0 = z_s[rows, 2 * A_WIDTH + (2 * kv) * LANES:2 * A_WIDTH + (2 * kv + 1) * LANES]
            q1 = z_s[rows, 2 * A_WIDTH + (2 * kv + 1) * LANES:2 * A_WIDTH + (2 * kv + 2) * LANES]
            lhs = jnp.concatenate([jnp.where(lo, q0, 0.0), jnp.where(lo, 0.0, q0),
                                   jnp.where(lo, q1, 0.0), jnp.where(lo, 0.0, q1)], axis=0).astype(BF16)
            s_all = lax.dot_general(lhs, kd[kv], (((1,), (1,)), ((), ())), preferred_element_type=F32)
            probs = []
            for i in range(4):
                h = 4 * kv + i
                slope = 2.0 ** (-(h + 1))
                sink = sinks_ref[h]
                s = s_all[i * CHUNK:(i + 1) * CHUNK] * (HEAD_DIM ** -0.5) - slope * dist
                s = jnp.where(valid, s, NEG)
                mx = jnp.maximum(jnp.max(s, axis=-1, keepdims=True), sink)
                e = jnp.exp(s - mx)
                den = jnp.sum(e, axis=-1, keepdims=True) + jnp.exp(sink - mx)
                probs.append(e * (1.0 / den))
            pm = jnp.concatenate(probs, axis=0).astype(BF16)
            o = jnp.dot(pm, vd[kv], preferred_element_type=F32)
            yb_slabs.append(jnp.where(lo, o[0:CHUNK], o[CHUNK:2 * CHUNK]))
            yb_slabs.append(jnp.where(lo, o[2 * CHUNK:3 * CHUNK], o[3 * CHUNK:4 * CHUNK]))
        yb_n = _rms(jnp.concatenate(yb_slabs, axis=-1), gob_ref[...])
        cat_s[rows, 0:A_WIDTH] = ya_n.astype(BF16)
        cat_s[rows, A_WIDTH:] = yb_n.astype(BF16)
        return carry

    lax.fori_loop(0, TM // CHUNK, chunk_body, 0)

    kv_s[0:CHUNK, :] = kv_s[TM:TM + CHUNK, :]
    k_ref[...] = kv_s[TM:TM + CHUNK, 0:KV_WIDTH]
    v_ref[...] = kv_s[TM:TM + CHUNK, KV_WIDTH:]

    h1 = x_ref[...] + jnp.dot(cat_s[...], wo_ref[...], preferred_element_type=F32)
    h1_ref[...] = h1
    xn2 = _rms(h1, gmoe_ref[...])
    xn_ref[...] = xn2.astype(BF16)
    gates, sel = _route(xn2, wr_ref, br_ref)
    gm_ref[...] = gates
    sm_ref[...] = sel


def _full(shape):
    n = len(shape)
    return pl.BlockSpec(shape, lambda *_: (0,) * n)


def _prompt_call(x, sinks, gmix, win, lng, lnb, wsp, bsp, goa, gob, wo, gmoe, wr, br):
    real = N_PROMPT // TM
    row = lambda g: (g, 0)
    seq = lambda g: (jnp.minimum(g, real - 1) // (SEQ // TM), 0, 0)
    return pl.pallas_call(
        _prompt_kernel,
        grid=(N_PAD // TM,),
        in_specs=[
            pl.BlockSpec(memory_space=pltpu.SMEM),
            pl.BlockSpec((TM, D_MODEL), lambda g: (jnp.minimum(g, real - 1), 0)),
            _full((1, D_MODEL)), _full((D_MODEL, IN_WIDTH)), _full((1, A_WIDTH)), _full((1, A_WIDTH)),
            _full((8, CHUNK, CHUNK)), _full((CHUNK, A_WIDTH)), _full((1, A_WIDTH)), _full((1, B_WIDTH)),
            _full((D_MODEL, D_MODEL)), _full((1, D_MODEL)), _full((D_MODEL, LANES)), _full((1, LANES)),
        ],
        out_specs=[
            pl.BlockSpec((TM, D_MODEL), row),
            pl.BlockSpec((TM, D_MODEL), row),
            pl.BlockSpec((TM, LANES), row),
            pl.BlockSpec((TM, LANES), row),
            pl.BlockSpec((None, CHUNK, KV_WIDTH), seq),
            pl.BlockSpec((None, CHUNK, KV_WIDTH), seq),
        ],
        out_shape=[
            jax.ShapeDtypeStruct((N_PAD, D_MODEL), F32),
            jax.ShapeDtypeStruct((N_PAD, D_MODEL), BF16),
            jax.ShapeDtypeStruct((N_PAD, LANES), F32),
            jax.ShapeDtypeStruct((N_PAD, LANES), F32),
            jax.ShapeDtypeStruct((BATCH, CHUNK, KV_WIDTH), F32),
            jax.ShapeDtypeStruct((BATCH, CHUNK, KV_WIDTH), F32),
        ],
        scratch_shapes=[
            pltpu.VMEM((TM, IN_WIDTH), F32),
            pltpu.VMEM((TM + CHUNK, 2 * KV_WIDTH), F32),
            pltpu.VMEM((TM, D_MODEL), BF16),
        ],
        compiler_params=pltpu.CompilerParams(
            dimension_semantics=("arbitrary",), vmem_limit_bytes=VMEM_LIMIT),
        name="prompt_premoe",
    )(sinks, x, gmix, win, lng, lnb, wsp, bsp, goa, gob, wo, gmoe, wr, br)


def _sample_kernel(sinks_ref, x_ref, ck_ref, cv_ref, gmix_ref, win_ref, lng_ref, lnb_ref, w00_ref, b0_ref,
                   goa_ref, gob_ref, wo_ref, gmoe_ref, wr_ref, br_ref,
                   h1_in, xn_in, gm_in, sm_in,
                   h1_ref, xn_ref, gm_ref, sm_ref, nk_ref, nv_ref, va_ref):
    del h1_in, xn_in, gm_in, sm_in
    t = SAMPLE_TILE
    nkeys = t * CHUNK

    if True:
        x = x_ref[...]
        xn = _rms(x, gmix_ref[...]).astype(BF16)
        z = jnp.dot(xn, win_ref[...], preferred_element_type=F32)
        u = _gelu(z[:, 0:A_WIDTH])
        va = _layernorm(_gelu(z[:, A_WIDTH:2 * A_WIDTH]), lng_ref[...], lnb_ref[...])
        va_ref[...] = va
        ya_n = _rms(u * (w00_ref[...] * va + b0_ref[...]), goa_ref[...])

        knew = z[:, 2 * A_WIDTH + B_WIDTH:2 * A_WIDTH + B_WIDTH + KV_WIDTH]
        vnew = z[:, 2 * A_WIDTH + B_WIDTH + KV_WIDTH:]
        lane = _lane_iota((t, LANES))
        lo = lane < HEAD_DIM
        stacked = []
        for h in range(B_HEADS):
            q = z[:, 2 * A_WIDTH + (h // 2) * LANES:2 * A_WIDTH + (h // 2 + 1) * LANES]
            qh = jnp.where(lo if h % 2 == 0 else jnp.logical_not(lo), q, 0.0)
            if h % 2 != h // 4:
                qh = pltpu.roll(qh, HEAD_DIM, 1)
            stacked.append(qh)
        qs = jnp.concatenate(stacked, axis=0)
        rows = B_HEADS * t
        ridx = lax.broadcasted_iota(jnp.int32, (rows, 1), 0)
        slope = jnp.zeros((rows, 1), F32)
        sink = jnp.zeros((rows, 1), F32)
        for h in range(B_HEADS):
            in_h = (ridx >= h * t) & (ridx < (h + 1) * t)
            slope = jnp.where(in_h, 2.0 ** (-(h + 1)), slope)
            sink = jnp.where(in_h, sinks_ref[h], sink)
        s_c = lax.dot_general(qs.astype(BF16), ck_ref[...].astype(BF16), (((1,), (1,)), ((), ())),
                              preferred_element_type=F32)
        rsamp = lax.broadcasted_iota(jnp.int32, (rows, nkeys), 0) % t
        col = lax.broadcasted_iota(jnp.int32, (rows, nkeys), 1)
        pos = col % CHUNK
        own = ((col // CHUNK) == rsamp) & (pos >= 1)
        s_c = s_c * (HEAD_DIM ** -0.5) - slope * (CHUNK - pos).astype(F32)
        s_c = jnp.where(own, s_c, NEG)
        kn8 = jnp.concatenate([knew] * B_HEADS, axis=0)
        vn8 = jnp.concatenate([vnew] * B_HEADS, axis=0)
        s_n = jnp.sum(qs * kn8, axis=-1, keepdims=True) * (HEAD_DIM ** -0.5)
        mx = jnp.maximum(jnp.maximum(jnp.max(s_c, axis=-1, keepdims=True), s_n), sink)
        e_c = jnp.exp(s_c - mx)
        e_n = jnp.exp(s_n - mx)
        inv = 1.0 / (jnp.sum(e_c, axis=-1, keepdims=True) + e_n + jnp.exp(sink - mx))
        o = jnp.dot((e_c * inv).astype(BF16), cv_ref[...].astype(BF16), preferred_element_type=F32)
        o = o + (e_n * inv) * vn8
        yb_slabs = []
        for p in range(B_WIDTH // LANES):
            outs = []
            for half in range(2):
                h = 2 * p + half
                oh = o[h * t:(h + 1) * t]
                oh = jnp.where(lo if h // 4 == 0 else jnp.logical_not(lo), oh, 0.0)
                if half != h // 4:
                    oh = pltpu.roll(oh, HEAD_DIM, 1)
                outs.append(oh)
            yb_slabs.append(outs[0] + outs[1])
        yb_n = _rms(jnp.concatenate(yb_slabs, axis=-1), gob_ref[...])

        cat = jnp.concatenate([ya_n, yb_n], axis=-1).astype(BF16)
        h1 = x + jnp.dot(cat, wo_ref[...], preferred_element_type=F32)
        xn2 = _rms(h1, gmoe_ref[...])
        gates, sel = _route(xn2, wr_ref, br_ref)
        h1_ref[...] = h1
        xn_ref[...] = xn2.astype(BF16)
        gm_ref[...] = gates
        sm_ref[...] = sel

        nk_ref[...] = pltpu.roll(ck_ref[...], nkeys - 1, 0)
        nv_ref[...] = pltpu.roll(cv_ref[...], nkeys - 1, 0)
        for b in range(t):
            nk_ref[b * CHUNK + CHUNK - 1:b * CHUNK + CHUNK, :] = knew[b:b + 1, :]
            nv_ref[b * CHUNK + CHUNK - 1:b * CHUNK + CHUNK, :] = vnew[b:b + 1, :]


def _sample_call(x, ck, cv, sinks, gmix, win, lng, lnb, w00, b0, goa, gob, wo, gmoe, wr, br, h1, xn, gm, sm):
    t = SAMPLE_TILE
    steps = DEC_BATCH // t
    base = N_PROMPT // t
    inrow = lambda i: (i, 0)
    outrow = lambda i: (base + i, 0)
    anyspec = pl.BlockSpec(memory_space=pl.ANY)
    return pl.pallas_call(
        _sample_kernel,
        grid=(steps,),
        in_specs=[
            pl.BlockSpec(memory_space=pltpu.SMEM),
            pl.BlockSpec((t, D_MODEL), inrow),
            pl.BlockSpec((t * CHUNK, KV_WIDTH), inrow),
            pl.BlockSpec((t * CHUNK, KV_WIDTH), inrow),
            _full((1, D_MODEL)), _full((D_MODEL, IN_WIDTH)), _full((1, A_WIDTH)), _full((1, A_WIDTH)),
            _full((1, A_WIDTH)), _full((1, A_WIDTH)), _full((1, A_WIDTH)), _full((1, B_WIDTH)),
            _full((D_MODEL, D_MODEL)), _full((1, D_MODEL)), _full((D_MODEL, LANES)), _full((1, LANES)),
            anyspec, anyspec, anyspec, anyspec,
        ],
        out_specs=[
            pl.BlockSpec((t, D_MODEL), outrow),
            pl.BlockSpec((t, D_MODEL), outrow),
            pl.BlockSpec((t, LANES), outrow),
            pl.BlockSpec((t, LANES), outrow),
            pl.BlockSpec((t * CHUNK, KV_WIDTH), inrow),
            pl.BlockSpec((t * CHUNK, KV_WIDTH), inrow),
            pl.BlockSpec((t, A_WIDTH), inrow),
        ],
        out_shape=[
            jax.ShapeDtypeStruct((N_PAD, D_MODEL), F32),
            jax.ShapeDtypeStruct((N_PAD, D_MODEL), BF16),
            jax.ShapeDtypeStruct((N_PAD, LANES), F32),
            jax.ShapeDtypeStruct((N_PAD, LANES), F32),
            jax.ShapeDtypeStruct((DEC_BATCH * CHUNK, KV_WIDTH), F32),
            jax.ShapeDtypeStruct((DEC_BATCH * CHUNK, KV_WIDTH), F32),
            jax.ShapeDtypeStruct((DEC_BATCH, A_WIDTH), F32),
        ],
        input_output_aliases={16: 0, 17: 1, 18: 2, 19: 3},
        compiler_params=pltpu.CompilerParams(
            dimension_semantics=("arbitrary",), vmem_limit_bytes=VMEM_LIMIT),
        name="sample_premoe",
    )(sinks, x, ck, cv, gmix, win, lng, lnb, w00, b0, goa, gob, wo, gmoe, wr, br, h1, xn, gm, sm)


def _decode_kernel(sinks_ref, x_ref, ck_ref, cv_ref, gmix_ref, win_ref, lng_ref, lnb_ref, w00_ref, b0_ref,
                   goa_ref, gob_ref, wo_ref, gmoe_ref, wr_ref, br_ref,
                   h1_in, xn_in, gm_in, sm_in,
                   h1_ref, xn_ref, gm_ref, sm_ref, nk_ref, nv_ref, va_ref,
                   q_s, kn_s, vn_s, ya_s, yb_s):
    del h1_in, xn_in, gm_in, sm_in
    i = pl.program_id(0)
    t = DEC_TILE
    per_seq = CHUNK * 2
    ncols = t * per_seq

    @pl.when(i == 0)
    def _():
        xn = _rms(x_ref[...], gmix_ref[...]).astype(BF16)
        z = jnp.dot(xn, win_ref[...], preferred_element_type=F32)
        u = _gelu(z[:, 0:A_WIDTH])
        va = _layernorm(_gelu(z[:, A_WIDTH:2 * A_WIDTH]), lng_ref[...], lnb_ref[...])
        va_ref[...] = va
        ya_s[...] = _rms(u * (w00_ref[...] * va + b0_ref[...]), goa_ref[...])
        q_s[...] = z[:, 2 * A_WIDTH:2 * A_WIDTH + B_WIDTH]
        kn_s[...] = z[:, 2 * A_WIDTH + B_WIDTH:2 * A_WIDTH + B_WIDTH + KV_WIDTH]
        vn_s[...] = z[:, 2 * A_WIDTH + B_WIDTH + KV_WIDTH:]

    rows = pl.ds(pl.multiple_of(i * t, t), t)
    q = q_s[rows, :]
    kn = kn_s[rows, :]
    vn = vn_s[rows, :]
    qs = jnp.concatenate([q[:, h * HEAD_DIM:(h + 1) * HEAD_DIM] for h in range(B_HEADS)], axis=0)
    kn8 = jnp.concatenate([kn[:, (h // 4) * HEAD_DIM:(h // 4 + 1) * HEAD_DIM] for h in range(B_HEADS)], axis=0)
    vn8 = jnp.concatenate([vn[:, (h // 4) * HEAD_DIM:(h // 4 + 1) * HEAD_DIM] for h in range(B_HEADS)], axis=0)
    nrows = B_HEADS * t
    ridx = lax.broadcasted_iota(jnp.int32, (nrows, 1), 0)
    slope = jnp.zeros((nrows, 1), F32)
    sink = jnp.zeros((nrows, 1), F32)
    for h in range(B_HEADS):
        in_h = (ridx >= h * t) & (ridx < (h + 1) * t)
        slope = jnp.where(in_h, 2.0 ** (-(h + 1)), slope)
        sink = jnp.where(in_h, sinks_ref[h], sink)
    s_c = lax.dot_general(qs.astype(BF16), ck_ref[...].astype(BF16), (((1,), (1,)), ((), ())),
                          preferred_element_type=F32)
    rr = lax.broadcasted_iota(jnp.int32, (nrows, ncols), 0)
    col = lax.broadcasted_iota(jnp.int32, (nrows, ncols), 1)
    pos = (col // 2) % CHUNK
    own = ((col // per_seq) == (rr % t)) & ((col % 2) == (rr // (4 * t))) & (pos >= 1)
    s_c = s_c * (HEAD_DIM ** -0.5) - slope * (CHUNK - pos).astype(F32)
    s_c = jnp.where(own, s_c, NEG)
    s_n = jnp.sum(qs * kn8, axis=-1, keepdims=True) * (HEAD_DIM ** -0.5)
    mx = jnp.maximum(jnp.maximum(jnp.max(s_c, axis=-1, keepdims=True), s_n), sink)
    e_c = jnp.exp(s_c - mx)
    e_n = jnp.exp(s_n - mx)
    inv = 1.0 / (jnp.sum(e_c, axis=-1, keepdims=True) + e_n + jnp.exp(sink - mx))
    o = jnp.dot((e_c * inv).astype(BF16), cv_ref[...].astype(BF16), preferred_element_type=F32)
    o = o + (e_n * inv) * vn8
    yb_s[rows, :] = jnp.concatenate([o[h * t:(h + 1) * t] for h in range(B_HEADS)], axis=-1)

    nk_ref[...] = pltpu.roll(ck_ref[...], ncols - 2, 0)
    nv_ref[...] = pltpu.roll(cv_ref[...], ncols - 2, 0)
    for b in range(t):
        for kv in range(2):
            r = b * per_seq + per_seq - 2 + kv
            nk_ref[r:r + 1, :] = kn[b:b + 1, kv * HEAD_DIM:(kv + 1) * HEAD_DIM]
            nv_ref[r:r + 1, :] = vn[b:b + 1, kv * HEAD_DIM:(kv + 1) * HEAD_DIM]

    @pl.when(i == pl.num_programs(0) - 1)
    def _():
        yb_n = _rms(yb_s[...], gob_ref[...])
        cat = jnp.concatenate([ya_s[...], yb_n], axis=-1).astype(BF16)
        h1 = x_ref[...] + jnp.dot(cat, wo_ref[...], preferred_element_type=F32)
        xn2 = _rms(h1, gmoe_ref[...])
        gates, sel = _route(xn2, wr_ref, br_ref)
        h1_ref[...] = h1
        xn_ref[...] = xn2.astype(BF16)
        gm_ref[...] = gates
        sm_ref[...] = sel


def _decode_call(x, ck, cv, sinks, gmix, win, lng, lnb, w00, b0, goa, gob, wo, gmoe, wr, br, h1, xn, gm, sm):
    t = DEC_TILE
    per_seq = CHUNK * 2
    cache = pl.BlockSpec((t * per_seq, HEAD_DIM), lambda i: (i, 0))
    tok = lambda width: pl.BlockSpec((DEC_BATCH, width), lambda i: (N_PROMPT // DEC_BATCH, 0))
    anyspec = pl.BlockSpec(memory_space=pl.ANY)
    return pl.pallas_call(
        _decode_kernel,
        grid=(DEC_BATCH // t,),
        in_specs=[
            pl.BlockSpec(memory_space=pltpu.SMEM),
            _full((DEC_BATCH, D_MODEL)), cache, cache,
            _full((1, D_MODEL)), _full((D_MODEL, IN_WIDTH)), _full((1, A_WIDTH)), _full((1, A_WIDTH)),
            _full((1, A_WIDTH)), _full((1, A_WIDTH)), _full((1, A_WIDTH)), _full((1, B_WIDTH)),
            _full((D_MODEL, D_MODEL)), _full((1, D_MODEL)), _full((D_MODEL, LANES)), _full((1, LANES)),
            anyspec, anyspec, anyspec, anyspec,
        ],
        out_specs=[tok(D_MODEL), tok(D_MODEL), tok(LANES), tok(LANES), cache, cache, _full((DEC_BATCH, A_WIDTH))],
        out_shape=[
            jax.ShapeDtypeStruct((N_PAD, D_MODEL), F32),
            jax.ShapeDtypeStruct((N_PAD, D_MODEL), BF16),
            jax.ShapeDtypeStruct((N_PAD, LANES), F32),
            jax.ShapeDtypeStruct((N_PAD, LANES), F32),
            jax.ShapeDtypeStruct((DEC_BATCH * per_seq, HEAD_DIM), F32),
            jax.ShapeDtypeStruct((DEC_BATCH * per_seq, HEAD_DIM), F32),
            jax.ShapeDtypeStruct((DEC_BATCH, A_WIDTH), F32),
        ],
        scratch_shapes=[pltpu.VMEM((DEC_BATCH, B_WIDTH), F32), pltpu.VMEM((DEC_BATCH, KV_WIDTH), F32),
                        pltpu.VMEM((DEC_BATCH, KV_WIDTH), F32), pltpu.VMEM((DEC_BATCH, A_WIDTH), F32),
                        pltpu.VMEM((DEC_BATCH, B_WIDTH), F32)],
        input_output_aliases={16: 0, 17: 1, 18: 2, 19: 3},
        compiler_params=pltpu.CompilerParams(
            dimension_semantics=("arbitrary",), vmem_limit_bytes=VMEM_LIMIT),
        name="sample_premoe",
    )(sinks, x, ck, cv, gmix, win, lng, lnb, w00, b0, goa, gob, wo, gmoe, wr, br, h1, xn, gm, sm)


def _plan_kernel(sm_ref,
                 destm_ref, destt_ref, xbe_ref, nxblk_ref, stab_ref, ctab_ref, astart_ref, nwin_ref,
                 base_s, pstart_s):
    ph = pl.program_id(0)
    step = pl.program_id(1)
    lane = _lane_iota((1, LANES))

    @pl.when((ph == 0) & (step == 0))
    def _():
        base_s[...] = jnp.zeros_like(base_s)

    @pl.when(ph == 0)
    def _():
        base_s[...] += jnp.sum(sm_ref[...], axis=0, keepdims=True)

    @pl.when((ph == 1) & (step == 0))
    def _():
        counts = base_s[...]
        padded = jnp.floor((counts + (DISP_CHUNK + EXP_BLOCK - 1)) * (1.0 / EXP_BLOCK)) * EXP_BLOCK
        padded = jnp.where(counts > 0.0, padded, 0.0)
        pend = padded
        for s in (1, 2, 4, 8, 16):
            pend = pend + jnp.where(lane >= s, pltpu.roll(pend, s, 1), 0.0)
        pstart_s[...] = pend - padded
        base_s[...] = jnp.zeros_like(base_s)
        brow = lax.broadcasted_iota(jnp.int32, (XBE_ROWS, LANES), 0).astype(F32) * EXP_BLOCK
        done = jnp.where((lane < N_EXPERTS) & (pend <= brow), 1.0, 0.0)
        be = jnp.minimum(jnp.sum(done, axis=-1, keepdims=True), N_EXPERTS - 1.0)
        xbe_ref[...] = jnp.broadcast_to(be, (XBE_ROWS, LANES)).astype(jnp.int32)
        total = jnp.sum(jnp.where(lane == N_EXPERTS - 1, pend, 0.0), axis=-1, keepdims=True)
        nxblk_ref[...] = jnp.broadcast_to(total * (1.0 / EXP_BLOCK), (8, LANES)).astype(jnp.int32)
        stab_ref[...] = jnp.zeros_like(stab_ref)
        ctab_ref[...] = jnp.zeros_like(ctab_ref)
        astart_ref[...] = jnp.zeros_like(astart_ref)
        nwin_ref[...] = jnp.zeros_like(nwin_ref)

    @pl.when(ph == 1)
    def _():
        r = lax.broadcasted_iota(jnp.int32, (TOK_TILE, TOK_TILE), 0)
        c = lax.broadcasted_iota(jnp.int32, (TOK_TILE, TOK_TILE), 1)
        lower = jnp.where(c < r, 1.0, 0.0).astype(BF16)
        for q in range(TM // TOK_TILE):
            i = step * (TM // TOK_TILE) + q
            sel = sm_ref[q * TOK_TILE:(q + 1) * TOK_TILE, :]
            cnt = jnp.sum(sel, axis=0, keepdims=True)
            prefix = jnp.dot(lower, sel.astype(BF16), preferred_element_type=F32)
            start = pstart_s[...] + base_s[...]
            dest = jnp.where(sel > 0.0, prefix + start, -1.0)
            destm_ref[q * TOK_TILE:(q + 1) * TOK_TILE, :] = dest
            destt_ref[:, q * TOK_TILE:(q + 1) * TOK_TILE] = dest.T
            has = (cnt > 0.0) & (lane < N_EXPERTS)
            stab_ref[pl.ds(i, 1), :] = start.astype(jnp.int32)
            ctab_ref[pl.ds(i, 1), :] = jnp.where(has, cnt, 0.0).astype(jnp.int32)
            a = jnp.minimum(jnp.floor(start * (1.0 / WIN_ALIGN)) * WIN_ALIGN, float(N_ROWS - WIN))
            nw = jnp.where(has, jnp.floor((start + cnt - a + (WIN - 1)) * (1.0 / WIN)), 0.0)
            astart_ref[pl.ds(i, 1), :] = a.astype(jnp.int32)
            nwin_ref[pl.ds(i, 1), :] = nw.astype(jnp.int32)
            base_s[...] += cnt


def _plan_call(sm):
    tile = lambda ph, i: (i * ph, 0)
    tile_t = lambda ph, i: (0, i * ph)
    tab = jax.ShapeDtypeStruct((TAB_ROWS, LANES), jnp.int32)
    return pl.pallas_call(
        _plan_kernel,
        grid=(2, N_PAD // TM),
        in_specs=[pl.BlockSpec((TM, LANES), lambda ph, i: (i, 0))],
        out_specs=[
            pl.BlockSpec((TM, LANES), tile),
            pl.BlockSpec((LANES, TM), tile_t),
            _full((XBE_ROWS, LANES)), _full((8, LANES)),
            _full((TAB_ROWS, LANES)), _full((TAB_ROWS, LANES)), _full((TAB_ROWS, LANES)), _full((TAB_ROWS, LANES)),
        ],
        out_shape=[
            jax.ShapeDtypeStruct((N_PAD, LANES), F32),
            jax.ShapeDtypeStruct((LANES, N_PAD), F32),
            jax.ShapeDtypeStruct((XBE_ROWS, LANES), jnp.int32),
            jax.ShapeDtypeStruct((8, LANES), jnp.int32),
            tab, tab, tab, tab,
        ],
        scratch_shapes=[pltpu.VMEM((1, LANES), F32), pltpu.VMEM((1, LANES), F32)],
        compiler_params=pltpu.CompilerParams(
            dimension_semantics=("arbitrary", "arbitrary"), vmem_limit_bytes=VMEM_LIMIT),
        name="moe_plan",
    )(sm)


def _pack_rows(z):
    half = D_MODEL // 2
    lo = lax.bitcast_convert_type(z[:, :half], jnp.uint32) >> 16
    hi = lax.bitcast_convert_type(z[:, half:], jnp.uint32) & jnp.uint32(0xFFFF0000)
    return lax.bitcast_convert_type(hi | lo, jnp.int32)


def _unpack_rows(ref):
    rows = ref.shape[0]
    flat = ref.reshape(rows * PACK, LANES)
    lo, hi = [], []
    for s in range(PACK):
        w = lax.bitcast_convert_type(flat[pl.ds(s, rows, stride=PACK), :], jnp.uint32)
        lo.append(lax.bitcast_convert_type(w << 16, F32))
        hi.append(lax.bitcast_convert_type(w & jnp.uint32(0xFFFF0000), F32))
    return jnp.concatenate(lo + hi, axis=-1).astype(BF16)


def _dispatch_kernel(stab_ref, ctab_ref, cmax_ref, xn_ref, destt_ref, xs_in, xs_ref,
                     stage0, stage1, stage2, sems, sem2):
    del xs_in
    i = pl.program_id(0)
    last = pl.num_programs(0) - 1
    x = xn_ref[...]
    dt = destt_ref[...]
    rio = lax.broadcasted_iota(jnp.int32, (DISP_CHUNK, 1), 0).astype(F32)

    def chunk_rows(j, stage):
        parts = []
        for e in range(N_EXPERTS):
            first = (stab_ref[i * N_EXPERTS + e] + j * DISP_CHUNK).astype(F32)
            parts.append(jnp.where(dt[e:e + 1, :] == first + rio, 1.0, 0.0).astype(BF16))
        onehot = jnp.concatenate(parts, axis=0)
        words = _pack_rows(jnp.dot(onehot, x, preferred_element_type=F32))
        for s in range(PACK):
            stage[pl.ds(s, N_EXPERTS * DISP_CHUNK, stride=PACK), :] = words[:, s * LANES:(s + 1) * LANES]

    def copy(stage, step, e, j, sem):
        first = stab_ref[step * N_EXPERTS + e] + j * DISP_CHUNK
        rows = stage.reshape(N_EXPERTS * DISP_CHUNK, PACK, LANES)
        return pltpu.make_async_copy(rows.at[pl.ds(e * DISP_CHUNK, DISP_CHUNK)],
                                     xs_ref.at[pl.ds(first, DISP_CHUNK)], sem)

    def step_body(stage, prev_stage, par):
        chunk_rows(0, stage)
        for e in range(N_EXPERTS):

            @pl.when((i > 0) & (ctab_ref[jnp.maximum(i - 1, 0) * N_EXPERTS + e] > 0))
            def _(e=e):
                copy(prev_stage, i - 1, e, 0, sems.at[1 - par, e]).wait()

        for e in range(N_EXPERTS):

            @pl.when(ctab_ref[i * N_EXPERTS + e] > 0)
            def _(e=e):
                copy(stage, i, e, 0, sems.at[par, e]).start()

        for e in range(N_EXPERTS):

            @pl.when((i == last) & (ctab_ref[i * N_EXPERTS + e] > 0))
            def _(e=e):
                copy(stage, i, e, 0, sems.at[par, e]).wait()

    @pl.when(i % 2 == 0)
    def _():
        step_body(stage0, stage1, 0)

    @pl.when(i % 2 == 1)
    def _():
        step_body(stage1, stage0, 1)

    for j in range(1, TOK_TILE // DISP_CHUNK):

        @pl.when(cmax_ref[i] > j * DISP_CHUNK)
        def _(j=j):
            chunk_rows(j, stage2)
            for e in range(N_EXPERTS):

                @pl.when(ctab_ref[i * N_EXPERTS + e] > j * DISP_CHUNK)
                def _(e=e):
                    cp = copy(stage2, i, e, j, sem2)
                    cp.start()
                    cp.wait()


def _dispatch_call(stab, ctab, cmax, xn, destt, xs_zero):
    stage = pltpu.VMEM((N_EXPERTS * DISP_CHUNK * PACK, LANES), jnp.int32)
    grid_spec = pltpu.PrefetchScalarGridSpec(
        num_scalar_prefetch=3,
        grid=(N_PAD_TILES,),
        in_specs=[
            pl.BlockSpec((TOK_TILE, D_MODEL), lambda i, *_: (i, 0)),
            pl.BlockSpec((N_EXPERTS, TOK_TILE), lambda i, *_: (0, i)),
            pl.BlockSpec(memory_space=pl.ANY),
        ],
        out_specs=pl.BlockSpec(memory_space=pl.ANY),
        scratch_shapes=[stage, stage, stage, pltpu.SemaphoreType.DMA((2, N_EXPERTS)),
                        pltpu.SemaphoreType.DMA],
    )
    return pl.pallas_call(
        _dispatch_kernel,
        grid_spec=grid_spec,
        out_shape=jax.ShapeDtypeStruct((N_ROWS, PACK, LANES), jnp.int32),
        input_output_aliases={5: 0},
        compiler_params=pltpu.CompilerParams(
            dimension_semantics=("arbitrary",), vmem_limit_bytes=VMEM_LIMIT),
        name="moe_dispatch",
    )(stab, ctab, cmax, xn, destt, xs_zero)


def _expert_kernel(blke_ref, nblk_ref, next_ref, xs_ref, wgu_hbm, bgu_ref, wdn_hbm, bdn_ref,
                   ys_ref, wgu_f, wdn_f, wgu_s, wdn_s, sems):
    b = pl.program_id(0)
    used = b < nblk_ref[0]
    prev = blke_ref[jnp.maximum(b - 1, 0)]
    fresh = used & ((b == 0) | (blke_ref[b] != prev))

    def fetch(e):
        return (pltpu.make_async_copy(wgu_hbm.at[e], wgu_f, sems.at[0]),
                pltpu.make_async_copy(wdn_hbm.at[e], wdn_f, sems.at[1]))

    @pl.when(b == 0)
    def _():
        for cp in fetch(blke_ref[0]):
            cp.start()

    @pl.when(fresh)
    def _():
        for cp in fetch(blke_ref[b]):
            cp.wait()
        wgu_s[...] = wgu_f[...].astype(BF16)
        wdn_s[...] = wdn_f[...].astype(BF16)

        @pl.when(next_ref[b] >= 0)
        def _():
            for cp in fetch(next_ref[b]):
                cp.start()

    @pl.when(used)
    def _():
        hid = jnp.dot(_unpack_rows(xs_ref), wgu_s[...], preferred_element_type=F32) + bgu_ref[...]
        gate = jnp.minimum(hid[:, :D_FF], SWIGLU_LIMIT)
        up = jnp.clip(hid[:, D_FF:], -SWIGLU_LIMIT, SWIGLU_LIMIT)
        act = (up + 1.0) * gate * jax.nn.sigmoid(SWIGLU_ALPHA * gate)
        y = jnp.dot(act.astype(BF16), wdn_s[...], preferred_element_type=F32) + bdn_ref[...]
        ys_ref[...] = y.astype(BF16)

    @pl.when(b >= nblk_ref[0])
    def _():
        ys_ref[...] = jnp.zeros_like(ys_ref)


def _expert_call(blke, nblk, nxt, xs, wgu, bgu, wdn, bdn):
    grid_spec = pltpu.PrefetchScalarGridSpec(
        num_scalar_prefetch=3,
        grid=(N_XBLOCKS,),
        in_specs=[
            pl.BlockSpec((EXP_BLOCK, PACK, LANES), lambda b, be, nb, nx: (b, 0, 0)),
            pl.BlockSpec(memory_space=pl.ANY),
            pl.BlockSpec((None, 1, 2 * D_FF), lambda b, be, nb, nx: (be[b], 0, 0)),
            pl.BlockSpec(memory_space=pl.ANY),
            pl.BlockSpec((None, 1, D_MODEL), lambda b, be, nb, nx: (be[b], 0, 0)),
        ],
        out_specs=pl.BlockSpec((EXP_BLOCK, D_MODEL), lambda b, be, nb, nx: (b, 0)),
        scratch_shapes=[pltpu.VMEM((D_MODEL, 2 * D_FF), F32), pltpu.VMEM((D_FF, D_MODEL), F32),
                        pltpu.VMEM((D_MODEL, 2 * D_FF), BF16), pltpu.VMEM((D_FF, D_MODEL), BF16),
                        pltpu.SemaphoreType.DMA((2,))],
    )
    return pl.pallas_call(
        _expert_kernel,
        grid_spec=grid_spec,
        out_shape=jax.ShapeDtypeStruct((N_ROWS, D_MODEL), BF16),
        compiler_params=pltpu.CompilerParams(
            dimension_semantics=("arbitrary",), vmem_limit_bytes=VMEM_LIMIT),
        name="moe_experts",
    )(blke, nblk, nxt, xs, wgu, bgu, wdn, bdn)


def _combine_kernel(*refs):
    astart_ref, nwin_ref, over_ref = refs[0:3]
    win_refs = refs[3:3 + N_EXPERTS]
    (destm_ref, gm_ref, h1_ref, plep_ref, ples_ref, gple_ref, wpg_ref, wpp_ref, gfin_ref, ys_any,
     yp_ref, ysm_ref, moe_s, tmp_s, sem) = refs[3 + N_EXPERTS:]
    i = pl.program_id(0)
    dest = destm_ref[...]
    gates = gm_ref[...]
    lane = _lane_iota((TOK_TILE, LANES))
    lane_f = lane.astype(F32)
    lo = lane < WIN
    moe = jnp.zeros((TOK_TILE, D_MODEL), F32)
    group = 4
    for g0 in range(0, N_EXPERTS, group):
        g_hi, g_lo = [], []
        for p in range(group // 2):
            e0 = g0 + 2 * p
            a0 = astart_ref[i * N_EXPERTS + e0].astype(F32)
            a1 = astart_ref[i * N_EXPERTS + e0 + 1].astype(F32)
            rowid = jnp.where(lo, a0 + lane_f, a1 + lane_f - WIN)
            dcol = jnp.where(lo, dest[:, e0:e0 + 1], dest[:, e0 + 1:e0 + 2])
            gcol = jnp.where(lo, gates[:, e0:e0 + 1], gates[:, e0 + 1:e0 + 2])
            gsel = jnp.where(dcol == rowid, gcol, 0.0)
            hi = gsel.astype(BF16)
            g_hi.append(hi)
            g_lo.append((gsel - hi.astype(F32)).astype(BF16))
        ywin = jnp.concatenate([win_refs[g0 + q][...] for q in range(group)], axis=0)
        both = jnp.concatenate([jnp.concatenate(g_hi, axis=-1), jnp.concatenate(g_lo, axis=-1)], axis=0)
        r = jnp.dot(both, ywin, preferred_element_type=F32)
        moe = moe + r[:TOK_TILE] + r[TOK_TILE:]
    moe_s[...] = moe

    @pl.when(over_ref[i] > 0)
    def _():
        tmp_s[...] = jnp.zeros_like(tmp_s)

        def per_expert(e, carry):
            a = astart_ref[i * N_EXPERTS + e]
            dcol = jnp.sum(jnp.where(lane == e, dest, 0.0), axis=-1, keepdims=True)
            gcol = jnp.sum(jnp.where(lane == e, gates, 0.0), axis=-1, keepdims=True)

            def per_window(w, carry2):
                first = a + w * WIN
                start = pl.multiple_of(jnp.minimum(first, N_ROWS - WIN), WIN_ALIGN)
                cp = pltpu.make_async_copy(ys_any.at[pl.ds(start, WIN)], tmp_s.at[pl.ds(0, WIN)], sem)
                cp.start()
                cp.wait()
                hit = lo & (dcol == start.astype(F32) + lane_f) & (dcol >= first.astype(F32))
                gsel = jnp.where(hit, gcol, 0.0)
                hi = gsel.astype(BF16)
                rest = (gsel - hi.astype(F32)).astype(BF16)
                moe_s[...] += (jnp.dot(hi, tmp_s[...], preferred_element_type=F32)
                               + jnp.dot(rest, tmp_s[...], preferred_element_type=F32))
                return carry2

            return lax.fori_loop(1, nwin_ref[i * N_EXPERTS + e], per_window, carry)

        lax.fori_loop(0, N_EXPERTS, per_expert, 0)

    is_sample = i == N_TILES - 1
    h2 = h1_ref[...] + moe_s[...]
    ple = jnp.where(is_sample, ples_ref[...], plep_ref[...])
    hn = _rms(h2, gple_ref[...]).astype(BF16)
    gate = jax.nn.sigmoid(jnp.dot(hn, wpg_ref[...], preferred_element_type=F32))
    proj = jnp.dot(ple.astype(BF16), wpp_ref[...], preferred_element_type=F32)
    y = _rms(h2 + gate * proj, gfin_ref[...])

    @pl.when(jnp.logical_not(is_sample))
    def _():
        yp_ref[...] = y

    @pl.when(is_sample)
    def _():
        ysm_ref[...] = y


def _combine_call(astart, nwin, over, ys, destm, gm, h1, plep, ples, gple, wpg, wpp, gfin):
    last = N_TILES - 2

    def win_spec(e):
        return pl.BlockSpec((pl.Element(WIN), pl.Element(D_MODEL)),
                            lambda i, a, nw, ov, e=e: (pl.multiple_of(a[i * N_EXPERTS + e], WIN_ALIGN), 0))

    grid_spec = pltpu.PrefetchScalarGridSpec(
        num_scalar_prefetch=3,
        grid=(N_TILES,),
        in_specs=[win_spec(e) for e in range(N_EXPERTS)] + [
            pl.BlockSpec((TOK_TILE, LANES), lambda i, *_: (i, 0)),
            pl.BlockSpec((TOK_TILE, LANES), lambda i, *_: (i, 0)),
            pl.BlockSpec((TOK_TILE, D_MODEL), lambda i, *_: (i, 0)),
            pl.BlockSpec((TOK_TILE, PLE_DIM), lambda i, *_: (jnp.minimum(i, last), 0)),
            pl.BlockSpec((TOK_TILE, PLE_DIM), lambda i, *_: (0, 0)),
            pl.BlockSpec((1, D_MODEL), lambda i, *_: (0, 0)),
            pl.BlockSpec((D_MODEL, D_MODEL), lambda i, *_: (0, 0)),
            pl.BlockSpec((PLE_DIM, D_MODEL), lambda i, *_: (0, 0)),
            pl.BlockSpec((1, D_MODEL), lambda i, *_: (0, 0)),
            pl.BlockSpec(memory_space=pl.ANY),
        ],
        out_specs=[
            pl.BlockSpec((TOK_TILE, D_MODEL), lambda i, *_: (jnp.minimum(i, last), 0)),
            pl.BlockSpec((TOK_TILE, D_MODEL), lambda i, *_: (0, 0)),
        ],
        scratch_shapes=[pltpu.VMEM((TOK_TILE, D_MODEL), F32), pltpu.VMEM((2 * WIN, D_MODEL), BF16),
                        pltpu.SemaphoreType.DMA],
    )
    return pl.pallas_call(
        _combine_kernel,
        grid_spec=grid_spec,
        out_shape=[jax.ShapeDtypeStruct((N_PROMPT, D_MODEL), F32),
                   jax.ShapeDtypeStruct((DEC_BATCH, D_MODEL), F32)],
        compiler_params=pltpu.CompilerParams(
            dimension_semantics=("arbitrary",), vmem_limit_bytes=VMEM_LIMIT),
        name="moe_combine_tail",
    )(astart, nwin, over, *([ys] * N_EXPERTS), destm, gm, h1, plep, ples, gple, wpg, wpp, gfin, ys)


def kernel(x_prompt, x_sample, cache_swa_k, cache_swa_v, p_prompt, p_sample, g_mix, w_in, ln_v_g, ln_v_b,
           w_sp, b_sp, sinks, g_out_a, g_out_b, w_o, g_moe, w_router, b_router, w_gu, b_gu, w_dn, b_dn,
           g_ple, w_ple_gate, w_ple_proj, g_final):
    l = 0
    row = lambda v: v.reshape(1, -1)
    win = w_in[l].astype(BF16)
    wo = w_o[l].astype(BF16)
    tril = jnp.tril(jnp.ones((CHUNK, CHUNK), dtype=bool))
    wsp = jnp.where(tril, w_sp[l], 0.0).astype(BF16)
    bsp = jnp.repeat(b_sp[l].T, HEAD_DIM, axis=1)
    w00 = row(jnp.repeat(w_sp[l][:, 0, 0], HEAD_DIM))
    b0 = row(jnp.repeat(b_sp[l][:, 0], HEAD_DIM))
    wr_hi = w_router[l].astype(BF16)
    wr_lo = (w_router[l] - wr_hi.astype(F32)).astype(BF16)
    wr = jnp.concatenate([wr_hi, wr_lo, jnp.zeros((D_MODEL, LANES - 2 * N_EXPERTS), BF16)], axis=1)
    br = row(jnp.concatenate([b_router[l], jnp.zeros((LANES - N_EXPERTS,), F32)]))
    common = (row(g_mix[l]), win, row(ln_v_g[l]), row(ln_v_b[l]))
    tail = (row(g_out_a[l]), row(g_out_b[l]), wo, row(g_moe[l]), wr, br)

    h1, xn, gm, sm, k_p, v_p = _prompt_call(
        x_prompt.reshape(N_PROMPT, D_MODEL), sinks[l], *common, wsp, bsp, *tail)
    ck = cache_swa_k[l].reshape(DEC_BATCH * CHUNK * 2, HEAD_DIM)
    cv = cache_swa_v[l].reshape(DEC_BATCH * CHUNK * 2, HEAD_DIM)
    h1, xn, gm, sm, k_s, v_s, va_s = _decode_call(
        x_sample.reshape(DEC_BATCH, D_MODEL), ck, cv, sinks[l], *common, w00, b0, *tail, h1, xn, gm, sm)

    destm, destt, xbe, nxblk, stab, ctab, astart, nwin = _plan_call(sm)
    flat = lambda tab, n: tab[:n, :N_EXPERTS].reshape(-1)
    astart1 = flat(astart, N_TILES)
    nwin2 = nwin[:N_TILES, :N_EXPERTS]
    over1 = (jnp.max(nwin2, axis=1) > 1).astype(jnp.int32)
    nwin1 = nwin2.reshape(-1)
    cmax1 = jnp.max(ctab[:N_PAD_TILES, :N_EXPERTS], axis=1)

    xs = _dispatch_call(flat(stab, N_PAD_TILES), flat(ctab, N_PAD_TILES), cmax1, xn, destt[:N_EXPERTS],
                        jnp.zeros((N_ROWS, PACK, LANES), jnp.int32))
    xbe1, nxblk1 = xbe[:N_XBLOCKS, 0], nxblk[0, :1]
    blk = jnp.arange(N_XBLOCKS, dtype=jnp.int32)
    starts = (blk < nxblk1[0]) & ((blk == 0) | (xbe1 != jnp.roll(xbe1, 1)))
    pos = jnp.where(starts, blk, N_XBLOCKS)
    nxt_pos = jnp.roll(lax.cummin(pos, reverse=True), -1).at[N_XBLOCKS - 1].set(N_XBLOCKS)
    nxt1 = jnp.where(nxt_pos < N_XBLOCKS, xbe1[jnp.minimum(nxt_pos, N_XBLOCKS - 1)], -1).astype(jnp.int32)
    ys = _expert_call(xbe1, nxblk1, nxt1, xs, w_gu[l], b_gu[l].reshape(N_EXPERTS, 1, 2 * D_FF),
                      w_dn[l], b_dn[l].reshape(N_EXPERTS, 1, D_MODEL))
    y_p, y_s = _combine_call(
        astart1, nwin1, over1, ys, destm, gm, h1,
        p_prompt[l].reshape(N_PROMPT, PLE_DIM), p_sample[l].reshape(DEC_BATCH, PLE_DIM),
        row(g_ple[l]), w_ple_gate[l].astype(BF16), w_ple_proj[l].astype(BF16), row(g_final))

    kv5 = lambda a, n: a.reshape(1, n, CHUNK, 2, HEAD_DIM)
    return (y_p.reshape(BATCH, SEQ, D_MODEL), y_s.reshape(DEC_BATCH, 1, D_MODEL),
            kv5(k_p, BATCH), kv5(v_p, BATCH), kv5(k_s, DEC_BATCH), kv5(v_s, DEC_BATCH),
            va_s.reshape(1, DEC_BATCH, 1, A_WIDTH))
```

```python
import math

import jax
import jax.numpy as jnp
from jax import lax
from jax.experimental import pallas as pl
from jax.experimental.pallas import tpu as pltpu

F32 = jnp.float32
BF16 = jnp.bfloat16

D_MODEL = 1024
BATCH = 4
SEQ = 4096
DEC_BATCH = 128
HEAD_DIM = 64
A_WIDTH = 512
B_WIDTH = 512
B_HEADS = 8
KV_WIDTH = 128
IN_WIDTH = 2 * A_WIDTH + B_WIDTH + 2 * KV_WIDTH
CHUNK = 128
N_EXPERTS = 32
TOP_K = 4
D_FF = 1024
SWIGLU_ALPHA = 1.702
SWIGLU_LIMIT = 7.0
PLE_DIM = 256
EPS = 1e-5

LANES = 128
ROW_BLOCK = 128
EXP_BLOCK = 512
N_PROMPT = BATCH * SEQ
N_TOK = N_PROMPT + DEC_BATCH
TOK_TILE = 128
N_TILES = N_TOK // TOK_TILE
DISP_TILE = 256
TM = 512
N_PAD = ((N_TOK + TM - 1) // TM) * TM
N_PAD_TILES = N_PAD // TOK_TILE
DISP_CHUNK = 32
N_XBLOCKS = (N_TOK * TOP_K + N_EXPERTS * (DISP_CHUNK + EXP_BLOCK - 1) + EXP_BLOCK - 1) // EXP_BLOCK
N_ROWS = N_XBLOCKS * EXP_BLOCK
XS_ROWS = N_ROWS + N_EXPERTS * DISP_CHUNK
XBE_ROWS = ((N_XBLOCKS + 7) // 8) * 8
TAB_ROWS = ((N_PAD_TILES + 7) // 8) * 8
PACK = D_MODEL // 2 // LANES
WIN = 64
WIN_ALIGN = 16
SAMPLE_TILE = 32
DEC_TILE = 16
NEG = -1e30
VMEM_LIMIT = 56 * 1024 * 1024


def _rms(x, g):
    return x * lax.rsqrt(jnp.mean(x * x, axis=-1, keepdims=True) + EPS) * g


def _gelu(x):
    c = math.sqrt(2.0 / math.pi)
    return x * (0.5 * (1.0 + jnp.tanh(c * (x + 0.044715 * (x * x * x)))))


def _layernorm(x, g, b):
    mu = jnp.mean(x, axis=-1, keepdims=True)
    xc = x - mu
    return xc * lax.rsqrt(jnp.mean(xc * xc, axis=-1, keepdims=True) + EPS) * g + b


def _lane_iota(shape):
    return lax.broadcasted_iota(jnp.int32, shape, len(shape) - 1)


def _route(xn2, wr_ref, br_ref):
    m = xn2.shape[0]
    xh = xn2.astype(BF16)
    xl = (xn2 - xh.astype(F32)).astype(BF16)
    r = jnp.dot(jnp.concatenate([xh, xl], axis=0), wr_ref[...], preferred_element_type=F32)
    r = r[:m] + r[m:]
    lane = _lane_iota((m, LANES))
    lane_f = lane.astype(F32)
    logits = jnp.where(lane < N_EXPERTS, r + pltpu.roll(r, LANES - N_EXPERTS, 1) + br_ref[...], NEG)
    work = logits
    sel = jnp.zeros((m, LANES), F32)
    top = None
    z = None
    for _ in range(TOP_K):
        mx = jnp.max(work, axis=-1, keepdims=True)
        first = jnp.min(jnp.where(work == mx, lane_f, float(LANES)), axis=-1, keepdims=True)
        hit = lane_f == first
        sel = jnp.where(hit, 1.0, sel)
        work = jnp.where(hit, NEG, work)
        if top is None:
            top = mx
            z = jnp.ones_like(mx)
        else:
            z = z + jnp.exp(mx - top)
    gates = jnp.where(sel > 0.0, jnp.exp(logits - top) / z, 0.0)
    return gates, sel


def _prompt_kernel(sinks_ref, x_ref, gmix_ref, win_ref, lng_ref, lnb_ref, wsp_ref, bsp_ref,
                   goa_ref, gob_ref, wo_ref, gmoe_ref, wr_ref, br_ref,
                   h1_ref, xn_ref, gm_ref, sm_ref, k_ref, v_ref,
                   z_s, kv_s, cat_s):
    g = pl.program_id(0)
    j = g % (SEQ // TM)

    @pl.when(g >= N_PROMPT // TM)
    def _():
        h1_ref[...] = jnp.zeros_like(h1_ref)
        xn_ref[...] = jnp.zeros_like(xn_ref)
        gm_ref[...] = jnp.zeros_like(gm_ref)
        sm_ref[...] = jnp.zeros_like(sm_ref)

    @pl.when(g < N_PROMPT // TM)
    def _():
        _prompt_tile(j, sinks_ref, x_ref, gmix_ref, win_ref, lng_ref, lnb_ref, wsp_ref, bsp_ref,
                     goa_ref, gob_ref, wo_ref, gmoe_ref, wr_ref, br_ref,
                     h1_ref, xn_ref, gm_ref, sm_ref, k_ref, v_ref, z_s, kv_s, cat_s)


def _prompt_tile(j, sinks_ref, x_ref, gmix_ref, win_ref, lng_ref, lnb_ref, wsp_ref, bsp_ref,
                 goa_ref, gob_ref, wo_ref, gmoe_ref, wr_ref, br_ref,
                 h1_ref, xn_ref, gm_ref, sm_ref, k_ref, v_ref, z_s, kv_s, cat_s):
    @pl.when(j == 0)
    def _():
        kv_s[0:CHUNK, :] = jnp.zeros((CHUNK, 2 * KV_WIDTH), F32)

    xn = _rms(x_ref[...], gmix_ref[...]).astype(BF16)
    z_s[...] = jnp.dot(xn, win_ref[...], preferred_element_type=F32)
    kv_s[CHUNK:, :] = z_s[:, 2 * A_WIDTH + B_WIDTH:]

    lane = _lane_iota((CHUNK, LANES))
    lo = lane < HEAD_DIM
    lane2 = _lane_iota((2 * CHUNK, LANES))
    lo2 = lane2 < HEAD_DIM
    qi = lax.broadcasted_iota(jnp.int32, (CHUNK, CHUNK), 0)
    kc = lax.broadcasted_iota(jnp.int32, (CHUNK, CHUNK), 1)
    from_prev = kc > qi
    dist = jnp.where(from_prev, qi + CHUNK - kc, qi - kc).astype(F32)

    def chunk_body(c, carry):
        r0 = pl.multiple_of(c * CHUNK, CHUNK)
        rows = pl.ds(r0, CHUNK)
        u = _gelu(z_s[rows, 0:A_WIDTH])
        va = _layernorm(_gelu(z_s[rows, A_WIDTH:2 * A_WIDTH]), lng_ref[...], lnb_ref[...])
        vab = va.astype(BF16)
        slabs = []
        for p in range(A_WIDTH // LANES):
            slab = vab[:, p * LANES:(p + 1) * LANES]
            m0 = jnp.dot(wsp_ref[2 * p], slab, preferred_element_type=F32)
            m1 = jnp.dot(wsp_ref[2 * p + 1], slab, preferred_element_type=F32)
            slabs.append(jnp.where(lo, m0, m1))
        ya = u * (jnp.concatenate(slabs, axis=-1) + bsp_ref[...])
        ya_n = _rms(ya, goa_ref[...])
        k2 = kv_s[pl.ds(r0, 2 * CHUNK), 0:KV_WIDTH]
        v2 = kv_s[pl.ds(r0, 2 * CHUNK), KV_WIDTH:2 * KV_WIDTH]
        k2r = pltpu.roll(k2, HEAD_DIM, 1)
        v2r = pltpu.roll(v2, HEAD_DIM, 1)
        kd = (jnp.where(lo2, k2, k2r).astype(BF16), jnp.where(lo2, k2r, k2).astype(BF16))
        vd = (jnp.where(lo2, v2, v2r).astype(BF16), jnp.where(lo2, v2r, v2).astype(BF16))
        prev_ok = (j > 0) | (c > 0)
        masked = from_prev & jnp.logical_not(prev_ok)
        yb_slabs = []
        for kv in range(2):
            q0 = z_s[rows, 2 * A_WIDTH + (2 * kv) * LANES:2 * A_WIDTH + (2 * kv + 1) * LANES]
            q1 = z_s[rows, 2 * A_WIDTH + (2 * kv + 1) * LANES:2 * A_WIDTH + (2 * kv + 2) * LANES]
            lhs = jnp.concatenate([jnp.where(lo, q0, 0.0), jnp.where(lo, 0.0, q0),
                                   jnp.where(lo, q1, 0.0), jnp.where(lo, 0.0, q1)], axis=0).astype(BF16)
            s_all = lax.dot_general(lhs, kd[kv], (((1,), (1,)), ((), ())), preferred_element_type=F32)
            probs = []
            for i in range(4):
                h = 4 * kv + i
                slope = 2.0 ** (-(h + 1))
                sink = sinks_ref[h]
                sh = s_all[i * CHUNK:(i + 1) * CHUNK]
                s = jnp.where(from_prev, sh[:, :CHUNK], sh[:, CHUNK:]) * (HEAD_DIM ** -0.5) - slope * dist
                s = jnp.where(masked, NEG, s)
                mx = jnp.maximum(jnp.max(s, axis=-1, keepdims=True), sink)
                e = jnp.exp(s - mx)
                den = jnp.sum(e, axis=-1, keepdims=True) + jnp.exp(sink - mx)
                p = e * (1.0 / den)
                probs.append(jnp.concatenate([jnp.where(from_prev, p, 0.0), jnp.where(from_prev, 0.0, p)], axis=-1))
            pm = jnp.concatenate(probs, axis=0).astype(BF16)
            o = jnp.dot(pm, vd[kv], preferred_element_type=F32)
            yb_slabs.append(jnp.where(lo, o[0:CHUNK], o[CHUNK:2 * CHUNK]))
            yb_slabs.append(jnp.where(lo, o[2 * CHUNK:3 * CHUNK], o[3 * CHUNK:4 * CHUNK]))
        yb_n = _rms(jnp.concatenate(yb_slabs, axis=-1), gob_ref[...])
        cat_s[rows, 0:A_WIDTH] = ya_n.astype(BF16)
        cat_s[rows, A_WIDTH:] = yb_n.astype(BF16)
        return carry

    lax.fori_loop(0, TM // CHUNK, chunk_body, 0)

    kv_s[0:CHUNK, :] = kv_s[TM:TM + CHUNK, :]
    k_ref[...] = kv_s[TM:TM + CHUNK, 0:KV_WIDTH]
    v_ref[...] = kv_s[TM:TM + CHUNK, KV_WIDTH:]

    h1 = x_ref[...] + jnp.dot(cat_s[...], wo_ref[...], preferred_element_type=F32)
    h1_ref[...] = h1
    xn2 = _rms(h1, gmoe_ref[...])
    xn_ref[...] = xn2.astype(BF16)
    gates, sel = _route(xn2, wr_ref, br_ref)
    gm_ref[...] = gates
    sm_ref[...] = sel


def _full(shape):
    n = len(shape)
    return pl.BlockSpec(shape, lambda *_: (0,) * n)


def _prompt_call(x, sinks, gmix, win, lng, lnb, wsp, bsp, goa, gob, wo, gmoe, wr, br):
    real = N_PROMPT // TM
    row = lambda g: (g, 0)
    seq = lambda g: (jnp.minimum(g, real - 1) // (SEQ // TM), 0, 0)
    return pl.pallas_call(
        _prompt_kernel,
        grid=(N_PAD // TM,),
        in_specs=[
            pl.BlockSpec(memory_space=pltpu.SMEM),
            pl.BlockSpec((TM, D_MODEL), lambda g: (jnp.minimum(g, real - 1), 0)),
            _full((1, D_MODEL)), _full((D_MODEL, IN_WIDTH)), _full((1, A_WIDTH)), _full((1, A_WIDTH)),
            _full((8, CHUNK, CHUNK)), _full((CHUNK, A_WIDTH)), _full((1, A_WIDTH)), _full((1, B_WIDTH)),
            _full((D_MODEL, D_MODEL)), _full((1, D_MODEL)), _full((D_MODEL, LANES)), _full((1, LANES)),
        ],
        out_specs=[
            pl.BlockSpec((TM, D_MODEL), row),
            pl.BlockSpec((TM, D_MODEL), row),
            pl.BlockSpec((TM, LANES), row),
            pl.BlockSpec((TM, LANES), row),
            pl.BlockSpec((None, CHUNK, KV_WIDTH), seq),
            pl.BlockSpec((None, CHUNK, KV_WIDTH), seq),
        ],
        out_shape=[
            jax.ShapeDtypeStruct((N_PAD, D_MODEL), F32),
            jax.ShapeDtypeStruct((N_PAD, D_MODEL), BF16),
            jax.ShapeDtypeStruct((N_PAD, LANES), F32),
            jax.ShapeDtypeStruct((N_PAD, LANES), F32),
            jax.ShapeDtypeStruct((BATCH, CHUNK, KV_WIDTH), F32),
            jax.ShapeDtypeStruct((BATCH, CHUNK, KV_WIDTH), F32),
        ],
        scratch_shapes=[
            pltpu.VMEM((TM, IN_WIDTH), F32),
            pltpu.VMEM((TM + CHUNK, 2 * KV_WIDTH), F32),
            pltpu.VMEM((TM, D_MODEL), BF16),
        ],
        compiler_params=pltpu.CompilerParams(
            dimension_semantics=("arbitrary",), vmem_limit_bytes=VMEM_LIMIT),
        name="prompt_premoe",
    )(sinks, x, gmix, win, lng, lnb, wsp, bsp, goa, gob, wo, gmoe, wr, br)


def _sample_kernel(sinks_ref, x_ref, ck_ref, cv_ref, gmix_ref, win_ref, lng_ref, lnb_ref, w00_ref, b0_ref,
                   goa_ref, gob_ref, wo_ref, gmoe_ref, wr_ref, br_ref,
                   h1_in, xn_in, gm_in, sm_in,
                   h1_ref, xn_ref, gm_ref, sm_ref, nk_ref, nv_ref, va_ref):
    del h1_in, xn_in, gm_in, sm_in
    t = SAMPLE_TILE
    nkeys = t * CHUNK

    if True:
        x = x_ref[...]
        xn = _rms(x, gmix_ref[...]).astype(BF16)
        z = jnp.dot(xn, win_ref[...], preferred_element_type=F32)
        u = _gelu(z[:, 0:A_WIDTH])
        va = _layernorm(_gelu(z[:, A_WIDTH:2 * A_WIDTH]), lng_ref[...], lnb_ref[...])
        va_ref[...] = va
        ya_n = _rms(u * (w00_ref[...] * va + b0_ref[...]), goa_ref[...])

        knew = z[:, 2 * A_WIDTH + B_WIDTH:2 * A_WIDTH + B_WIDTH + KV_WIDTH]
        vnew = z[:, 2 * A_WIDTH + B_WIDTH + KV_WIDTH:]
        lane = _lane_iota((t, LANES))
        lo = lane < HEAD_DIM
        stacked = []
        for h in range(B_HEADS):
            q = z[:, 2 * A_WIDTH + (h // 2) * LANES:2 * A_WIDTH + (h // 2 + 1) * LANES]
            qh = jnp.where(lo if h % 2 == 0 else jnp.logical_not(lo), q, 0.0)
            if h % 2 != h // 4:
                qh = pltpu.roll(qh, HEAD_DIM, 1)
            stacked.append(qh)
        qs = jnp.concatenate(stacked, axis=0)
        rows = B_HEADS * t
        ridx = lax.broadcasted_iota(jnp.int32, (rows, 1), 0)
        slope = jnp.zeros((rows, 1), F32)
        sink = jnp.zeros((rows, 1), F32)
        for h in range(B_HEADS):
            in_h = (ridx >= h * t) & (ridx < (h + 1) * t)
            slope = jnp.where(in_h, 2.0 ** (-(h + 1)), slope)
            sink = jnp.where(in_h, sinks_ref[h], sink)
        s_c = lax.dot_general(qs.astype(BF16), ck_ref[...].astype(BF16), (((1,), (1,)), ((), ())),
                              preferred_element_type=F32)
        rsamp = lax.broadcasted_iota(jnp.int32, (rows, nkeys), 0) % t
        col = lax.broadcasted_iota(jnp.int32, (rows, nkeys), 1)
        pos = col % CHUNK
        own = ((col // CHUNK) == rsamp) & (pos >= 1)
        s_c = s_c * (HEAD_DIM ** -0.5) - slope * (CHUNK - pos).astype(F32)
        s_c = jnp.where(own, s_c, NEG)
        kn8 = jnp.concatenate([knew] * B_HEADS, axis=0)
        vn8 = jnp.concatenate([vnew] * B_HEADS, axis=0)
        s_n = jnp.sum(qs * kn8, axis=-1, keepdims=True) * (HEAD_DIM ** -0.5)
        mx = jnp.maximum(jnp.maximum(jnp.max(s_c, axis=-1, keepdims=True), s_n), sink)
        e_c = jnp.exp(s_c - mx)
        e_n = jnp.exp(s_n - mx)
        inv = 1.0 / (jnp.sum(e_c, axis=-1, keepdims=True) + e_n + jnp.exp(sink - mx))
        o = jnp.dot((e_c * inv).astype(BF16), cv_ref[...].astype(BF16), preferred_element_type=F32)
        o = o + (e_n * inv) * vn8
        yb_slabs = []
        for p in range(B_WIDTH // LANES):
            outs = []
            for half in range(2):
                h = 2 * p + half
                oh = o[h * t:(h + 1) * t]
                oh = jnp.where(lo if h // 4 == 0 else jnp.logical_not(lo), oh, 0.0)
                if half != h // 4:
                    oh = pltpu.roll(oh, HEAD_DIM, 1)
                outs.append(oh)
            yb_slabs.append(outs[0] + outs[1])
        yb_n = _rms(jnp.concatenate(yb_slabs, axis=-1), gob_ref[...])

        cat = jnp.concatenate([ya_n, yb_n], axis=-1).astype(BF16)
        h1 = x + jnp.dot(cat, wo_ref[...], preferred_element_type=F32)
        xn2 = _rms(h1, gmoe_ref[...])
        gates, sel = _route(xn2, wr_ref, br_ref)
        h1_ref[...] = h1
        xn_ref[...] = xn2.astype(BF16)
        gm_ref[...] = gates
        sm_ref[...] = sel

        nk_ref[...] = pltpu.roll(ck_ref[...], nkeys - 1, 0)
        nv_ref[...] = pltpu.roll(cv_ref[...], nkeys - 1, 0)
        for b in range(t):
            nk_ref[b * CHUNK + CHUNK - 1:b * CHUNK + CHUNK, :] = knew[b:b + 1, :]
            nv_ref[b * CHUNK + CHUNK - 1:b * CHUNK + CHUNK, :] = vnew[b:b + 1, :]


def _sample_call(x, ck, cv, sinks, gmix, win, lng, lnb, w00, b0, goa, gob, wo, gmoe, wr, br, h1, xn, gm, sm):
    t = SAMPLE_TILE
    steps = DEC_BATCH // t
    base = N_PROMPT // t
    inrow = lambda i: (i, 0)
    outrow = lambda i: (base + i, 0)
    anyspec = pl.BlockSpec(memory_space=pl.ANY)
    return pl.pallas_call(
        _sample_kernel,
        grid=(steps,),
        in_specs=[
            pl.BlockSpec(memory_space=pltpu.SMEM),
            pl.BlockSpec((t, D_MODEL), inrow),
            pl.BlockSpec((t * CHUNK, KV_WIDTH), inrow),
            pl.BlockSpec((t * CHUNK, KV_WIDTH), inrow),
            _full((1, D_MODEL)), _full((D_MODEL, IN_WIDTH)), _full((1, A_WIDTH)), _full((1, A_WIDTH)),
            _full((1, A_WIDTH)), _full((1, A_WIDTH)), _full((1, A_WIDTH)), _full((1, B_WIDTH)),
            _full((D_MODEL, D_MODEL)), _full((1, D_MODEL)), _full((D_MODEL, LANES)), _full((1, LANES)),
            anyspec, anyspec, anyspec, anyspec,
        ],
        out_specs=[
            pl.BlockSpec((t, D_MODEL), outrow),
            pl.BlockSpec((t, D_MODEL), outrow),
            pl.BlockSpec((t, LANES), outrow),
            pl.BlockSpec((t, LANES), outrow),
            pl.BlockSpec((t * CHUNK, KV_WIDTH), inrow),
            pl.BlockSpec((t * CHUNK, KV_WIDTH), inrow),
            pl.BlockSpec((t, A_WIDTH), inrow),
        ],
        out_shape=[
            jax.ShapeDtypeStruct((N_PAD, D_MODEL), F32),
            jax.ShapeDtypeStruct((N_PAD, D_MODEL), BF16),
            jax.ShapeDtypeStruct((N_PAD, LANES), F32),
            jax.ShapeDtypeStruct((N_PAD, LANES), F32),
            jax.ShapeDtypeStruct((DEC_BATCH * CHUNK, KV_WIDTH), F32),
            jax.ShapeDtypeStruct((DEC_BATCH * CHUNK, KV_WIDTH), F32),
            jax.ShapeDtypeStruct((DEC_BATCH, A_WIDTH), F32),
        ],
        input_output_aliases={16: 0, 17: 1, 18: 2, 19: 3},
        compiler_params=pltpu.CompilerParams(
            dimension_semantics=("arbitrary",), vmem_limit_bytes=VMEM_LIMIT),
        name="sample_premoe",
    )(sinks, x, ck, cv, gmix, win, lng, lnb, w00, b0, goa, gob, wo, gmoe, wr, br, h1, xn, gm, sm)


def _decode_kernel(sinks_ref, x_ref, ck_ref, cv_ref, gmix_ref, win_ref, lng_ref, lnb_ref, w00_ref, b0_ref,
                   goa_ref, gob_ref, wo_ref, gmoe_ref, wr_ref, br_ref,
                   h1_in, xn_in, gm_in, sm_in,
                   h1_ref, xn_ref, gm_ref, sm_ref, nk_ref, nv_ref, va_ref,
                   q_s, kn_s, vn_s, ya_s, yb_s):
    del h1_in, xn_in, gm_in, sm_in
    i = pl.program_id(0)
    t = DEC_TILE
    per_seq = CHUNK * 2
    ncols = t * per_seq

    @pl.when(i == 0)
    def _():
        xn = _rms(x_ref[...], gmix_ref[...]).astype(BF16)
        z = jnp.dot(xn, win_ref[...], preferred_element_type=F32)
        u = _gelu(z[:, 0:A_WIDTH])
        va = _layernorm(_gelu(z[:, A_WIDTH:2 * A_WIDTH]), lng_ref[...], lnb_ref[...])
        va_ref[...] = va
        ya_s[...] = _rms(u * (w00_ref[...] * va + b0_ref[...]), goa_ref[...])
        q_s[...] = z[:, 2 * A_WIDTH:2 * A_WIDTH + B_WIDTH]
        kn_s[...] = z[:, 2 * A_WIDTH + B_WIDTH:2 * A_WIDTH + B_WIDTH + KV_WIDTH]
        vn_s[...] = z[:, 2 * A_WIDTH + B_WIDTH + KV_WIDTH:]

    rows = pl.ds(pl.multiple_of(i * t, t), t)
    q = q_s[rows, :]
    kn = kn_s[rows, :]
    vn = vn_s[rows, :]
    qs = jnp.concatenate([q[:, h * HEAD_DIM:(h + 1) * HEAD_DIM] for h in range(B_HEADS)], axis=0)
    kn8 = jnp.concatenate([kn[:, (h // 4) * HEAD_DIM:(h // 4 + 1) * HEAD_DIM] for h in range(B_HEADS)], axis=0)
    vn8 = jnp.concatenate([vn[:, (h // 4) * HEAD_DIM:(h // 4 + 1) * HEAD_DIM] for h in range(B_HEADS)], axis=0)
    nrows = B_HEADS * t
    ridx = lax.broadcasted_iota(jnp.int32, (nrows, 1), 0)
    slope = jnp.zeros((nrows, 1), F32)
    sink = jnp.zeros((nrows, 1), F32)
    for h in range(B_HEADS):
        in_h = (ridx >= h * t) & (ridx < (h + 1) * t)
        slope = jnp.where(in_h, 2.0 ** (-(h + 1)), slope)
        sink = jnp.where(in_h, sinks_ref[h], sink)
    s_c = lax.dot_general(qs.astype(BF16), ck_ref[...].astype(BF16), (((1,), (1,)), ((), ())),
                          preferred_element_type=F32)
    rr = lax.broadcasted_iota(jnp.int32, (nrows, ncols), 0)
    col = lax.broadcasted_iota(jnp.int32, (nrows, ncols), 1)
    pos = (col // 2) % CHUNK
    own = ((col // per_seq) == (rr % t)) & ((col % 2) == (rr // (4 * t))) & (pos >= 1)
    s_c = s_c * (HEAD_DIM ** -0.5) - slope * (CHUNK - pos).astype(F32)
    s_c = jnp.where(own, s_c, NEG)
    s_n = jnp.sum(qs * kn8, axis=-1, keepdims=True) * (HEAD_DIM ** -0.5)
    mx = jnp.maximum(jnp.maximum(jnp.max(s_c, axis=-1, keepdims=True), s_n), sink)
    e_c = jnp.exp(s_c - mx)
    e_n = jnp.exp(s_n - mx)
    inv = 1.0 / (jnp.sum(e_c, axis=-1, keepdims=True) + e_n + jnp.exp(sink - mx))
    o = jnp.dot((e_c * inv).astype(BF16), cv_ref[...].astype(BF16), preferred_element_type=F32)
    o = o + (e_n * inv) * vn8
    yb_s[rows, :] = jnp.concatenate([o[h * t:(h + 1) * t] for h in range(B_HEADS)], axis=-1)

    nk_ref[...] = pltpu.roll(ck_ref[...], ncols - 2, 0)
    nv_ref[...] = pltpu.roll(cv_ref[...], ncols - 2, 0)
    for b in range(t):
        for kv in range(2):
            r = b * per_seq + per_seq - 2 + kv
            nk_ref[r:r + 1, :] = kn[b:b + 1, kv * HEAD_DIM:(kv + 1) * HEAD_DIM]
            nv_ref[r:r + 1, :] = vn[b:b + 1, kv * HEAD_DIM:(kv + 1) * HEAD_DIM]

    @pl.when(i == pl.num_programs(0) - 1)
    def _():
        yb_n = _rms(yb_s[...], gob_ref[...])
        cat = jnp.concatenate([ya_s[...], yb_n], axis=-1).astype(BF16)
        h1 = x_ref[...] + jnp.dot(cat, wo_ref[...], preferred_element_type=F32)
        xn2 = _rms(h1, gmoe_ref[...])
        gates, sel = _route(xn2, wr_ref, br_ref)
        h1_ref[...] = h1
        xn_ref[...] = xn2.astype(BF16)
        gm_ref[...] = gates
        sm_ref[...] = sel


def _decode_call(x, ck, cv, sinks, gmix, win, lng, lnb, w00, b0, goa, gob, wo, gmoe, wr, br, h1, xn, gm, sm):
    t = DEC_TILE
    per_seq = CHUNK * 2
    cache = pl.BlockSpec((t * per_seq, HEAD_DIM), lambda i: (i, 0))
    tok = lambda width: pl.BlockSpec((DEC_BATCH, width), lambda i: (N_PROMPT // DEC_BATCH, 0))
    anyspec = pl.BlockSpec(memory_space=pl.ANY)
    return pl.pallas_call(
        _decode_kernel,
        grid=(DEC_BATCH // t,),
        in_specs=[
            pl.BlockSpec(memory_space=pltpu.SMEM),
            _full((DEC_BATCH, D_MODEL)), cache, cache,
            _full((1, D_MODEL)), _full((D_MODEL, IN_WIDTH)), _full((1, A_WIDTH)), _full((1, A_WIDTH)),
            _full((1, A_WIDTH)), _full((1, A_WIDTH)), _full((1, A_WIDTH)), _full((1, B_WIDTH)),
            _full((D_MODEL, D_MODEL)), _full((1, D_MODEL)), _full((D_MODEL, LANES)), _full((1, LANES)),
            anyspec, anyspec, anyspec, anyspec,
        ],
        out_specs=[tok(D_MODEL), tok(D_MODEL), tok(LANES), tok(LANES), cache, cache, _full((DEC_BATCH, A_WIDTH))],
        out_shape=[
            jax.ShapeDtypeStruct((N_PAD, D_MODEL), F32),
            jax.ShapeDtypeStruct((N_PAD, D_MODEL), BF16),
            jax.ShapeDtypeStruct((N_PAD, LANES), F32),
            jax.ShapeDtypeStruct((N_PAD, LANES), F32),
            jax.ShapeDtypeStruct((DEC_BATCH * per_seq, HEAD_DIM), F32),
            jax.ShapeDtypeStruct((DEC_BATCH * per_seq, HEAD_DIM), F32),
            jax.ShapeDtypeStruct((DEC_BATCH, A_WIDTH), F32),
        ],
        scratch_shapes=[pltpu.VMEM((DEC_BATCH, B_WIDTH), F32), pltpu.VMEM((DEC_BATCH, KV_WIDTH), F32),
                        pltpu.VMEM((DEC_BATCH, KV_WIDTH), F32), pltpu.VMEM((DEC_BATCH, A_WIDTH), F32),
                        pltpu.VMEM((DEC_BATCH, B_WIDTH), F32)],
        input_output_aliases={16: 0, 17: 1, 18: 2, 19: 3},
        compiler_params=pltpu.CompilerParams(
            dimension_semantics=("arbitrary",), vmem_limit_bytes=VMEM_LIMIT),
        name="sample_premoe",
    )(sinks, x, ck, cv, gmix, win, lng, lnb, w00, b0, goa, gob, wo, gmoe, wr, br, h1, xn, gm, sm)


def _plan_kernel(sm_ref,
                 destm_ref, destt_ref, xbe_ref, nxblk_ref, stab_ref, ctab_ref, astart_ref, nwin_ref,
                 base_s, pstart_s):
    ph = pl.program_id(0)
    step = pl.program_id(1)
    lane = _lane_iota((1, LANES))

    @pl.when((ph == 0) & (step == 0))
    def _():
        base_s[...] = jnp.zeros_like(base_s)

    @pl.when(ph == 0)
    def _():
        base_s[...] += jnp.sum(sm_ref[...], axis=0, keepdims=True)

    @pl.when((ph == 1) & (step == 0))
    def _():
        counts = base_s[...]
        padded = jnp.floor((counts + (DISP_CHUNK + EXP_BLOCK - 1)) * (1.0 / EXP_BLOCK)) * EXP_BLOCK
        padded = jnp.where(counts > 0.0, padded, 0.0)
        pend = padded
        for s in (1, 2, 4, 8, 16):
            pend = pend + jnp.where(lane >= s, pltpu.roll(pend, s, 1), 0.0)
        spare = (N_ROWS + lane * DISP_CHUNK).astype(F32)
        pstart_s[...] = jnp.where(counts > 0.0, pend - padded, spare)
        base_s[...] = jnp.zeros_like(base_s)
        brow = lax.broadcasted_iota(jnp.int32, (XBE_ROWS, LANES), 0).astype(F32) * EXP_BLOCK
        done = jnp.where((lane < N_EXPERTS) & (pend <= brow), 1.0, 0.0)
        be = jnp.minimum(jnp.sum(done, axis=-1, keepdims=True), N_EXPERTS - 1.0)
        xbe_ref[...] = jnp.broadcast_to(be, (XBE_ROWS, LANES)).astype(jnp.int32)
        total = jnp.sum(jnp.where(lane == N_EXPERTS - 1, pend, 0.0), axis=-1, keepdims=True)
        nxblk_ref[...] = jnp.broadcast_to(total * (1.0 / EXP_BLOCK), (8, LANES)).astype(jnp.int32)
        stab_ref[...] = jnp.zeros_like(stab_ref)
        ctab_ref[...] = jnp.zeros_like(ctab_ref)
        astart_ref[...] = jnp.zeros_like(astart_ref)
        nwin_ref[...] = jnp.zeros_like(nwin_ref)

    @pl.when(ph == 1)
    def _():
        r = lax.broadcasted_iota(jnp.int32, (TOK_TILE, TOK_TILE), 0)
        c = lax.broadcasted_iota(jnp.int32, (TOK_TILE, TOK_TILE), 1)
        lower = jnp.where(c < r, 1.0, 0.0).astype(BF16)
        for q in range(TM // TOK_TILE):
            i = step * (TM // TOK_TILE) + q
            sel = sm_ref[q * TOK_TILE:(q + 1) * TOK_TILE, :]
            cnt = jnp.sum(sel, axis=0, keepdims=True)
            prefix = jnp.dot(lower, sel.astype(BF16), preferred_element_type=F32)
            start = pstart_s[...] + base_s[...]
            dest = jnp.where(sel > 0.0, prefix + start, -1.0)
            destm_ref[q * TOK_TILE:(q + 1) * TOK_TILE, :] = dest
            destt_ref[:, q * TOK_TILE:(q + 1) * TOK_TILE] = dest.T
            has = (cnt > 0.0) & (lane < N_EXPERTS)
            stab_ref[pl.ds(i, 1), :] = start.astype(jnp.int32)
            ctab_ref[pl.ds(i, 1), :] = jnp.where(has, cnt, 0.0).astype(jnp.int32)
            a = jnp.minimum(jnp.floor(start * (1.0 / WIN_ALIGN)) * WIN_ALIGN, float(N_ROWS - WIN))
            nw = jnp.where(has, jnp.floor((start + cnt - a + (WIN - 1)) * (1.0 / WIN)), 0.0)
            astart_ref[pl.ds(i, 1), :] = a.astype(jnp.int32)
            nwin_ref[pl.ds(i, 1), :] = nw.astype(jnp.int32)
            base_s[...] += cnt


def _plan_call(sm):
    tile = lambda ph, i: (i * ph, 0)
    tile_t = lambda ph, i: (0, i * ph)
    tab = jax.ShapeDtypeStruct((TAB_ROWS, LANES), jnp.int32)
    return pl.pallas_call(
        _plan_kernel,
        grid=(2, N_PAD // TM),
        in_specs=[pl.BlockSpec((TM, LANES), lambda ph, i: (i, 0))],
        out_specs=[
            pl.BlockSpec((TM, LANES), tile),
            pl.BlockSpec((LANES, TM), tile_t),
            _full((XBE_ROWS, LANES)), _full((8, LANES)),
            _full((TAB_ROWS, LANES)), _full((TAB_ROWS, LANES)), _full((TAB_ROWS, LANES)), _full((TAB_ROWS, LANES)),
        ],
        out_shape=[
            jax.ShapeDtypeStruct((N_PAD, LANES), F32),
            jax.ShapeDtypeStruct((LANES, N_PAD), F32),
            jax.ShapeDtypeStruct((XBE_ROWS, LANES), jnp.int32),
            jax.ShapeDtypeStruct((8, LANES), jnp.int32),
            tab, tab, tab, tab,
        ],
        scratch_shapes=[pltpu.VMEM((1, LANES), F32), pltpu.VMEM((1, LANES), F32)],
        compiler_params=pltpu.CompilerParams(
            dimension_semantics=("arbitrary", "arbitrary"), vmem_limit_bytes=VMEM_LIMIT),
        name="moe_plan",
    )(sm)


def _pack_rows(z):
    half = D_MODEL // 2
    lo = lax.bitcast_convert_type(z[:, :half], jnp.uint32) >> 16
    hi = lax.bitcast_convert_type(z[:, half:], jnp.uint32) & jnp.uint32(0xFFFF0000)
    return lax.bitcast_convert_type(hi | lo, jnp.int32)


def _unpack_rows(ref):
    rows = ref.shape[0]
    flat = ref.reshape(rows * PACK, LANES)
    lo, hi = [], []
    for s in range(PACK):
        w = lax.bitcast_convert_type(flat[pl.ds(s, rows, stride=PACK), :], jnp.uint32)
        lo.append(lax.bitcast_convert_type(w << 16, F32))
        hi.append(lax.bitcast_convert_type(w & jnp.uint32(0xFFFF0000), F32))
    return jnp.concatenate(lo + hi, axis=-1).astype(BF16)


def _dispatch_kernel(stab_ref, ctab_ref, cmax_ref, xn_ref, destt_ref, xs_in, xs_ref,
                     stage0, stage1, stage2, sems, sem2):
    del xs_in
    i = pl.program_id(0)
    last = pl.num_programs(0) - 1
    x = xn_ref[...]
    dt = destt_ref[...]
    rio = lax.broadcasted_iota(jnp.int32, (DISP_CHUNK, 1), 0).astype(F32)

    def chunk_rows(j, stage):
        parts = []
        for e in range(N_EXPERTS):
            first = (stab_ref[i * N_EXPERTS + e] + j * DISP_CHUNK).astype(F32)
            parts.append(jnp.where(dt[e:e + 1, :] == first + rio, 1.0, 0.0).astype(BF16))
        onehot = jnp.concatenate(parts, axis=0)
        words = _pack_rows(jnp.dot(onehot, x, preferred_element_type=F32))
        for s in range(PACK):
            stage[pl.ds(s, N_EXPERTS * DISP_CHUNK, stride=PACK), :] = words[:, s * LANES:(s + 1) * LANES]

    def copy(stage, step, e, j, sem):
        first = stab_ref[step * N_EXPERTS + e] + j * DISP_CHUNK
        rows = stage.reshape(N_EXPERTS * DISP_CHUNK, PACK, LANES)
        return pltpu.make_async_copy(rows.at[pl.ds(e * DISP_CHUNK, DISP_CHUNK)],
                                     xs_ref.at[pl.ds(first, DISP_CHUNK)], sem)

    def step_body(stage, prev_stage, par):
        chunk_rows(0, stage)

        @pl.when(i > 0)
        def _():
            for e in range(N_EXPERTS):
                copy(prev_stage, i - 1, e, 0, sems.at[1 - par, e]).wait()

        for e in range(N_EXPERTS):
            copy(stage, i, e, 0, sems.at[par, e]).start()

        @pl.when(i == last)
        def _():
            for e in range(N_EXPERTS):
                copy(stage, i, e, 0, sems.at[par, e]).wait()

    @pl.when(i % 2 == 0)
    def _():
        step_body(stage0, stage1, 0)

    @pl.when(i % 2 == 1)
    def _():
        step_body(stage1, stage0, 1)

    for j in range(1, TOK_TILE // DISP_CHUNK):

        @pl.when(cmax_ref[i] > j * DISP_CHUNK)
        def _(j=j):
            chunk_rows(j, stage2)
            for e in range(N_EXPERTS):

                @pl.when(ctab_ref[i * N_EXPERTS + e] > j * DISP_CHUNK)
                def _(e=e):
                    cp = copy(stage2, i, e, j, sem2)
                    cp.start()
                    cp.wait()


def _dispatch_call(stab, ctab, cmax, xn, destt, xs_zero):
    stage = pltpu.VMEM((N_EXPERTS * DISP_CHUNK * PACK, LANES), jnp.int32)
    grid_spec = pltpu.PrefetchScalarGridSpec(
        num_scalar_prefetch=3,
        grid=(N_PAD_TILES,),
        in_specs=[
            pl.BlockSpec((TOK_TILE, D_MODEL), lambda i, *_: (i, 0)),
            pl.BlockSpec((N_EXPERTS, TOK_TILE), lambda i, *_: (0, i)),
            pl.BlockSpec(memory_space=pl.ANY),
        ],
        out_specs=pl.BlockSpec(memory_space=pl.ANY),
        scratch_shapes=[stage, stage, stage, pltpu.SemaphoreType.DMA((2, N_EXPERTS)),
                        pltpu.SemaphoreType.DMA],
    )
    return pl.pallas_call(
        _dispatch_kernel,
        grid_spec=grid_spec,
        out_shape=jax.ShapeDtypeStruct((XS_ROWS, PACK, LANES), jnp.int32),
        input_output_aliases={5: 0},
        compiler_params=pltpu.CompilerParams(
            dimension_semantics=("arbitrary",), vmem_limit_bytes=VMEM_LIMIT),
        name="moe_dispatch",
    )(stab, ctab, cmax, xn, destt, xs_zero)


def _expert_kernel(blke_ref, nblk_ref, next_ref, xs_ref, wgu_hbm, bgu_ref, wdn_hbm, bdn_ref,
                   ys_ref, wgu_f, wdn_f, wgu_s, wdn_s, sems):
    b = pl.program_id(0)
    used = b < nblk_ref[0]
    prev = blke_ref[jnp.maximum(b - 1, 0)]
    fresh = used & ((b == 0) | (blke_ref[b] != prev))

    def fetch(e):
        return (pltpu.make_async_copy(wgu_hbm.at[e], wgu_f, sems.at[0]),
                pltpu.make_async_copy(wdn_hbm.at[e], wdn_f, sems.at[1]))

    @pl.when(b == 0)
    def _():
        for cp in fetch(blke_ref[0]):
            cp.start()

    @pl.when(fresh)
    def _():
        for cp in fetch(blke_ref[b]):
            cp.wait()
        wgu_s[...] = wgu_f[...].astype(BF16)
        wdn_s[...] = wdn_f[...].astype(BF16)

        @pl.when(next_ref[b] >= 0)
        def _():
            for cp in fetch(next_ref[b]):
                cp.start()

    @pl.when(used)
    def _():
        hid = jnp.dot(_unpack_rows(xs_ref), wgu_s[...], preferred_element_type=F32) + bgu_ref[...]
        gate = jnp.minimum(hid[:, :D_FF], SWIGLU_LIMIT)
        up = jnp.clip(hid[:, D_FF:], -SWIGLU_LIMIT, SWIGLU_LIMIT)
        act = (up + 1.0) * gate * jax.nn.sigmoid(SWIGLU_ALPHA * gate)
        y = jnp.dot(act.astype(BF16), wdn_s[...], preferred_element_type=F32) + bdn_ref[...]
        ys_ref[...] = y.astype(BF16)

    @pl.when(b >= nblk_ref[0])
    def _():
        ys_ref[...] = jnp.zeros_like(ys_ref)


def _expert_call(blke, nblk, nxt, xs, wgu, bgu, wdn, bdn):
    grid_spec = pltpu.PrefetchScalarGridSpec(
        num_scalar_prefetch=3,
        grid=(N_XBLOCKS,),
        in_specs=[
            pl.BlockSpec((EXP_BLOCK, PACK, LANES), lambda b, be, nb, nx: (b, 0, 0)),
            pl.BlockSpec(memory_space=pl.ANY),
            pl.BlockSpec((None, 1, 2 * D_FF), lambda b, be, nb, nx: (be[b], 0, 0)),
            pl.BlockSpec(memory_space=pl.ANY),
            pl.BlockSpec((None, 1, D_MODEL), lambda b, be, nb, nx: (be[b], 0, 0)),
        ],
        out_specs=pl.BlockSpec((EXP_BLOCK, D_MODEL), lambda b, be, nb, nx: (b, 0)),
        scratch_shapes=[pltpu.VMEM((D_MODEL, 2 * D_FF), F32), pltpu.VMEM((D_FF, D_MODEL), F32),
                        pltpu.VMEM((D_MODEL, 2 * D_FF), BF16), pltpu.VMEM((D_FF, D_MODEL), BF16),
                        pltpu.SemaphoreType.DMA((2,))],
    )
    return pl.pallas_call(
        _expert_kernel,
        grid_spec=grid_spec,
        out_shape=jax.ShapeDtypeStruct((N_ROWS, D_MODEL), BF16),
        compiler_params=pltpu.CompilerParams(
            dimension_semantics=("arbitrary",), vmem_limit_bytes=VMEM_LIMIT),
        name="moe_experts",
    )(blke, nblk, nxt, xs, wgu, bgu, wdn, bdn)


def _combine_kernel(*refs):
    astart_ref, nwin_ref, over_ref = refs[0:3]
    win_refs = refs[3:3 + N_EXPERTS]
    (destm_ref, gm_ref, h1_ref, plep_ref, ples_ref, gple_ref, wpg_ref, wpp_ref, gfin_ref, ys_any,
     yp_ref, ysm_ref, moe_s, tmp_s, sem) = refs[3 + N_EXPERTS:]
    i = pl.program_id(0)
    dest = destm_ref[...]
    gates = gm_ref[...]
    lane = _lane_iota((TOK_TILE, LANES))
    lane_f = lane.astype(F32)
    lo = lane < WIN
    moe = jnp.zeros((TOK_TILE, D_MODEL), F32)
    group = 4
    for g0 in range(0, N_EXPERTS, group):
        g_hi, g_lo = [], []
        for p in range(group // 2):
            e0 = g0 + 2 * p
            a0 = astart_ref[i * N_EXPERTS + e0].astype(F32)
            a1 = astart_ref[i * N_EXPERTS + e0 + 1].astype(F32)
            rowid = jnp.where(lo, a0 + lane_f, a1 + lane_f - WIN)
            dcol = jnp.where(lo, dest[:, e0:e0 + 1], dest[:, e0 + 1:e0 + 2])
            gcol = jnp.where(lo, gates[:, e0:e0 + 1], gates[:, e0 + 1:e0 + 2])
            gsel = jnp.where(dcol == rowid, gcol, 0.0)
            hi = gsel.astype(BF16)
            g_hi.append(hi)
            g_lo.append((gsel - hi.astype(F32)).astype(BF16))
        ywin = jnp.concatenate([win_refs[g0 + q][...] for q in range(group)], axis=0)
        both = jnp.concatenate([jnp.concatenate(g_hi, axis=-1), jnp.concatenate(g_lo, axis=-1)], axis=0)
        r = jnp.dot(both, ywin, preferred_element_type=F32)
        moe = moe + r[:TOK_TILE] + r[TOK_TILE:]
    moe_s[...] = moe

    @pl.when(over_ref[i] > 0)
    def _():
        tmp_s[...] = jnp.zeros_like(tmp_s)

        def per_expert(e, carry):
            a = astart_ref[i * N_EXPERTS + e]
            dcol = jnp.sum(jnp.where(lane == e, dest, 0.0), axis=-1, keepdims=True)
            gcol = jnp.sum(jnp.where(lane == e, gates, 0.0), axis=-1, keepdims=True)

            def per_window(w, carry2):
                first = a + w * WIN
                start = pl.multiple_of(jnp.minimum(first, N_ROWS - WIN), WIN_ALIGN)
                cp = pltpu.make_async_copy(ys_any.at[pl.ds(start, WIN)], tmp_s.at[pl.ds(0, WIN)], sem)
                cp.start()
                cp.wait()
                hit = lo & (dcol == start.astype(F32) + lane_f) & (dcol >= first.astype(F32))
                gsel = jnp.where(hit, gcol, 0.0)
                hi = gsel.astype(BF16)
                rest = (gsel - hi.astype(F32)).astype(BF16)
                moe_s[...] += (jnp.dot(hi, tmp_s[...], preferred_element_type=F32)
                               + jnp.dot(rest, tmp_s[...], preferred_element_type=F32))
                return carry2

            return lax.fori_loop(1, nwin_ref[i * N_EXPERTS + e], per_window, carry)

        lax.fori_loop(0, N_EXPERTS, per_expert, 0)

    is_sample = i == N_TILES - 1
    h2 = h1_ref[...] + moe_s[...]
    ple = jnp.where(is_sample, ples_ref[...], plep_ref[...])
    hn = _rms(h2, gple_ref[...]).astype(BF16)
    gate = jax.nn.sigmoid(jnp.dot(hn, wpg_ref[...], preferred_element_type=F32))
    proj = jnp.dot(ple.astype(BF16), wpp_ref[...], preferred_element_type=F32)
    y = _rms(h2 + gate * proj, gfin_ref[...])

    @pl.when(jnp.logical_not(is_sample))
    def _():
        yp_ref[...] = y

    @pl.when(is_sample)
    def _():
        ysm_ref[...] = y


def _combine_call(astart, nwin, over, ys, destm, gm, h1, plep, ples, gple, wpg, wpp, gfin):
    last = N_TILES - 2

    def win_spec(e):
        return pl.BlockSpec((pl.Element(WIN), pl.Element(D_MODEL)),
                            lambda i, a, nw, ov, e=e: (pl.multiple_of(a[i * N_EXPERTS + e], WIN_ALIGN), 0))

    grid_spec = pltpu.PrefetchScalarGridSpec(
        num_scalar_prefetch=3,
        grid=(N_TILES,),
        in_specs=[win_spec(e) for e in range(N_EXPERTS)] + [
            pl.BlockSpec((TOK_TILE, LANES), lambda i, *_: (i, 0)),
            pl.BlockSpec((TOK_TILE, LANES), lambda i, *_: (i, 0)),
            pl.BlockSpec((TOK_TILE, D_MODEL), lambda i, *_: (i, 0)),
            pl.BlockSpec((TOK_TILE, PLE_DIM), lambda i, *_: (jnp.minimum(i, last), 0)),
            pl.BlockSpec((TOK_TILE, PLE_DIM), lambda i, *_: (0, 0)),
            pl.BlockSpec((1, D_MODEL), lambda i, *_: (0, 0)),
            pl.BlockSpec((D_MODEL, D_MODEL), lambda i, *_: (0, 0)),
            pl.BlockSpec((PLE_DIM, D_MODEL), lambda i, *_: (0, 0)),
            pl.BlockSpec((1, D_MODEL), lambda i, *_: (0, 0)),
            pl.BlockSpec(memory_space=pl.ANY),
        ],
        out_specs=[
            pl.BlockSpec((TOK_TILE, D_MODEL), lambda i, *_: (jnp.minimum(i, last), 0)),
            pl.BlockSpec((TOK_TILE, D_MODEL), lambda i, *_: (0, 0)),
        ],
        scratch_shapes=[pltpu.VMEM((TOK_TILE, D_MODEL), F32), pltpu.VMEM((2 * WIN, D_MODEL), BF16),
                        pltpu.SemaphoreType.DMA],
    )
    return pl.pallas_call(
        _combine_kernel,
        grid_spec=grid_spec,
        out_shape=[jax.ShapeDtypeStruct((N_PROMPT, D_MODEL), F32),
                   jax.ShapeDtypeStruct((DEC_BATCH, D_MODEL), F32)],
        compiler_params=pltpu.CompilerParams(
            dimension_semantics=("arbitrary",), vmem_limit_bytes=VMEM_LIMIT),
        name="moe_combine_tail",
    )(astart, nwin, over, *([ys] * N_EXPERTS), destm, gm, h1, plep, ples, gple, wpg, wpp, gfin, ys)


def kernel(x_prompt, x_sample, cache_swa_k, cache_swa_v, p_prompt, p_sample, g_mix, w_in, ln_v_g, ln_v_b,
           w_sp, b_sp, sinks, g_out_a, g_out_b, w_o, g_moe, w_router, b_router, w_gu, b_gu, w_dn, b_dn,
           g_ple, w_ple_gate, w_ple_proj, g_final):
    l = 0
    row = lambda v: v.reshape(1, -1)
    win = w_in[l].astype(BF16)
    wo = w_o[l].astype(BF16)
    tril = jnp.tril(jnp.ones((CHUNK, CHUNK), dtype=bool))
    wsp = jnp.where(tril, w_sp[l], 0.0).astype(BF16)
    bsp = jnp.repeat(b_sp[l].T, HEAD_DIM, axis=1)
    w00 = row(jnp.repeat(w_sp[l][:, 0, 0], HEAD_DIM))
    b0 = row(jnp.repeat(b_sp[l][:, 0], HEAD_DIM))
    wr_hi = w_router[l].astype(BF16)
    wr_lo = (w_router[l] - wr_hi.astype(F32)).astype(BF16)
    wr = jnp.concatenate([wr_hi, wr_lo, jnp.zeros((D_MODEL, LANES - 2 * N_EXPERTS), BF16)], axis=1)
    br = row(jnp.concatenate([b_router[l], jnp.zeros((LANES - N_EXPERTS,), F32)]))
    common = (row(g_mix[l]), win, row(ln_v_g[l]), row(ln_v_b[l]))
    tail = (row(g_out_a[l]), row(g_out_b[l]), wo, row(g_moe[l]), wr, br)

    h1, xn, gm, sm, k_p, v_p = _prompt_call(
        x_prompt.reshape(N_PROMPT, D_MODEL), sinks[l], *common, wsp, bsp, *tail)
    ck = cache_swa_k[l].reshape(DEC_BATCH * CHUNK * 2, HEAD_DIM)
    cv = cache_swa_v[l].reshape(DEC_BATCH * CHUNK * 2, HEAD_DIM)
    h1, xn, gm, sm, k_s, v_s, va_s = _decode_call(
        x_sample.reshape(DEC_BATCH, D_MODEL), ck, cv, sinks[l], *common, w00, b0, *tail, h1, xn, gm, sm)

    destm, destt, xbe, nxblk, stab, ctab, astart, nwin = _plan_call(sm)
    flat = lambda tab, n: tab[:n, :N_EXPERTS].reshape(-1)
    astart1 = flat(astart, N_TILES)
    nwin2 = nwin[:N_TILES, :N_EXPERTS]
    over1 = (jnp.max(nwin2, axis=1) > 1).astype(jnp.int32)
    nwin1 = nwin2.reshape(-1)
    cmax1 = jnp.max(ctab[:N_PAD_TILES, :N_EXPERTS], axis=1)

    xs = _dispatch_call(flat(stab, N_PAD_TILES), flat(ctab, N_PAD_TILES), cmax1, xn, destt[:N_EXPERTS],
                        jnp.zeros((XS_ROWS, PACK, LANES), jnp.int32))
    xbe1, nxblk1 = xbe[:N_XBLOCKS, 0], nxblk[0, :1]
    blk = jnp.arange(N_XBLOCKS, dtype=jnp.int32)
    starts = (blk < nxblk1[0]) & ((blk == 0) | (xbe1 != jnp.roll(xbe1, 1)))
    pos = jnp.where(starts, blk, N_XBLOCKS)
    nxt_pos = jnp.roll(lax.cummin(pos, reverse=True), -1).at[N_XBLOCKS - 1].set(N_XBLOCKS)
    nxt1 = jnp.where(nxt_pos < N_XBLOCKS, xbe1[jnp.minimum(nxt_pos, N_XBLOCKS - 1)], -1).astype(jnp.int32)
    ys = _expert_call(xbe1, nxblk1, nxt1, xs, w_gu[l], b_gu[l].reshape(N_EXPERTS, 1, 2 * D_FF),
                      w_dn[l], b_dn[l].reshape(N_EXPERTS, 1, D_MODEL))
    y_p, y_s = _combine_call(
        astart1, nwin1, over1, ys, destm, gm, h1,
        p_prompt[l].reshape(N_PROMPT, PLE_DIM), p_sample[l].reshape(DEC_BATCH, PLE_DIM),
        row(g_ple[l]), w_ple_gate[l].astype(BF16), w_ple_proj[l].astype(BF16), row(g_final))

    kv5 = lambda a, n: a.reshape(1, n, CHUNK, 2, HEAD_DIM)
    return (y_p.reshape(BATCH, SEQ, D_MODEL), y_s.reshape(DEC_BATCH, 1, D_MODEL),
            kv5(k_p, BATCH), kv5(v_p, BATCH), kv5(k_s, DEC_BATCH), kv5(v_s, DEC_BATCH),
            va_s.reshape(1, DEC_BATCH, 1, A_WIDTH))
```

```python
import math

import jax
import jax.numpy as jnp
from jax import lax
from jax.experimental import pallas as pl
from jax.experimental.pallas import tpu as pltpu

F32 = jnp.float32
BF16 = jnp.bfloat16

D_MODEL = 1024
BATCH = 4
SEQ = 4096
DEC_BATCH = 128
HEAD_DIM = 64
A_WIDTH = 512
B_WIDTH = 512
B_HEADS = 8
KV_WIDTH = 128
IN_WIDTH = 2 * A_WIDTH + B_WIDTH + 2 * KV_WIDTH
CHUNK = 128
N_EXPERTS = 32
TOP_K = 4
D_FF = 1024
SWIGLU_ALPHA = 1.702
SWIGLU_LIMIT = 7.0
PLE_DIM = 256
EPS = 1e-5

LANES = 128
ROW_BLOCK = 128
EXP_BLOCK = 512
N_PROMPT = BATCH * SEQ
N_TOK = N_PROMPT + DEC_BATCH
TOK_TILE = 128
N_TILES = N_TOK // TOK_TILE
DISP_TILE = 256
TM = 512
N_PAD = ((N_TOK + TM - 1) // TM) * TM
N_PAD_TILES = N_PAD // TOK_TILE
DISP_CHUNK = 32
N_XBLOCKS = (N_TOK * TOP_K + N_EXPERTS * (DISP_CHUNK + EXP_BLOCK - 1) + EXP_BLOCK - 1) // EXP_BLOCK
N_ROWS = N_XBLOCKS * EXP_BLOCK
XS_ROWS = N_ROWS + N_EXPERTS * DISP_CHUNK
XBE_ROWS = ((N_XBLOCKS + 7) // 8) * 8
TAB_ROWS = ((N_PAD_TILES + 7) // 8) * 8
PACK = D_MODEL // 2 // LANES
WIN = 64
WIN_ALIGN = 16
CWIN = 32
TAIL_TILES = 4
SAMPLE_TILE = 32
DEC_TILE = 16
NEG = -1e30
VMEM_LIMIT = 56 * 1024 * 1024


def _rms(x, g):
    return x * lax.rsqrt(jnp.mean(x * x, axis=-1, keepdims=True) + EPS) * g


def _gelu(x):
    c = math.sqrt(2.0 / math.pi)
    return x * (0.5 * (1.0 + jnp.tanh(c * (x + 0.044715 * (x * x * x)))))


def _layernorm(x, g, b):
    mu = jnp.mean(x, axis=-1, keepdims=True)
    xc = x - mu
    return xc * lax.rsqrt(jnp.mean(xc * xc, axis=-1, keepdims=True) + EPS) * g + b


def _lane_iota(shape):
    return lax.broadcasted_iota(jnp.int32, shape, len(shape) - 1)


def _route(xn2, wr_ref, br_ref):
    m = xn2.shape[0]
    xh = xn2.astype(BF16)
    xl = (xn2 - xh.astype(F32)).astype(BF16)
    r = jnp.dot(jnp.concatenate([xh, xl], axis=0), wr_ref[...], preferred_element_type=F32)
    r = r[:m] + r[m:]
    lane = _lane_iota((m, LANES))
    lane_f = lane.astype(F32)
    logits = jnp.where(lane < N_EXPERTS, r + pltpu.roll(r, LANES - N_EXPERTS, 1) + br_ref[...], NEG)
    work = logits
    sel = jnp.zeros((m, LANES), F32)
    top = None
    z = None
    for _ in range(TOP_K):
        mx = jnp.max(work, axis=-1, keepdims=True)
        first = jnp.min(jnp.where(work == mx, lane_f, float(LANES)), axis=-1, keepdims=True)
        hit = lane_f == first
        sel = jnp.where(hit, 1.0, sel)
        work = jnp.where(hit, NEG, work)
        if top is None:
            top = mx
            z = jnp.ones_like(mx)
        else:
            z = z + jnp.exp(mx - top)
    gates = jnp.where(sel > 0.0, jnp.exp(logits - top) / z, 0.0)
    return gates, sel


def _prompt_kernel(sinks_ref, x_ref, gmix_ref, win_ref, lng_ref, lnb_ref, wsp_ref, bsp_ref,
                   goa_ref, gob_ref, wo_ref, gmoe_ref, wr_ref, br_ref,
                   h1_ref, xn_ref, gm_ref, sm_ref, k_ref, v_ref,
                   z_s, kv_s, cat_s):
    g = pl.program_id(0)
    j = g % (SEQ // TM)

    @pl.when(g >= N_PROMPT // TM)
    def _():
        h1_ref[...] = jnp.zeros_like(h1_ref)
        xn_ref[...] = jnp.zeros_like(xn_ref)
        gm_ref[...] = jnp.zeros_like(gm_ref)
        sm_ref[...] = jnp.zeros_like(sm_ref)

    @pl.when(g < N_PROMPT // TM)
    def _():
        _prompt_tile(j, sinks_ref, x_ref, gmix_ref, win_ref, lng_ref, lnb_ref, wsp_ref, bsp_ref,
                     goa_ref, gob_ref, wo_ref, gmoe_ref, wr_ref, br_ref,
                     h1_ref, xn_ref, gm_ref, sm_ref, k_ref, v_ref, z_s, kv_s, cat_s)


def _prompt_tile(j, sinks_ref, x_ref, gmix_ref, win_ref, lng_ref, lnb_ref, wsp_ref, bsp_ref,
                 goa_ref, gob_ref, wo_ref, gmoe_ref, wr_ref, br_ref,
                 h1_ref, xn_ref, gm_ref, sm_ref, k_ref, v_ref, z_s, kv_s, cat_s):
    @pl.when(j == 0)
    def _():
        kv_s[0:CHUNK, :] = jnp.zeros((CHUNK, 2 * KV_WIDTH), F32)

    xn = _rms(x_ref[...], gmix_ref[...]).astype(BF16)
    z_s[...] = jnp.dot(xn, win_ref[...], preferred_element_type=F32)
    kv_s[CHUNK:, :] = z_s[:, 2 * A_WIDTH + B_WIDTH:]

    lane = _lane_iota((CHUNK, LANES))
    lo = lane < HEAD_DIM
    lane2 = _lane_iota((2 * CHUNK, LANES))
    lo2 = lane2 < HEAD_DIM
    qi = lax.broadcasted_iota(jnp.int32, (CHUNK, CHUNK), 0)
    kc = lax.broadcasted_iota(jnp.int32, (CHUNK, CHUNK), 1)
    from_prev = kc > qi
    dist = jnp.where(from_prev, qi + CHUNK - kc, qi - kc).astype(F32)

    def chunk_body(c, carry):
        r0 = pl.multiple_of(c * CHUNK, CHUNK)
        rows = pl.ds(r0, CHUNK)
        u = _gelu(z_s[rows, 0:A_WIDTH])
        va = _layernorm(_gelu(z_s[rows, A_WIDTH:2 * A_WIDTH]), lng_ref[...], lnb_ref[...])
        vab = va.astype(BF16)
        slabs = []
        for p in range(A_WIDTH // LANES):
            slab = vab[:, p * LANES:(p + 1) * LANES]
            m0 = jnp.dot(wsp_ref[2 * p], slab, preferred_element_type=F32)
            m1 = jnp.dot(wsp_ref[2 * p + 1], slab, preferred_element_type=F32)
            slabs.append(jnp.where(lo, m0, m1))
        ya = u * (jnp.concatenate(slabs, axis=-1) + bsp_ref[...])
        ya_n = _rms(ya, goa_ref[...])
        k2 = kv_s[pl.ds(r0, 2 * CHUNK), 0:KV_WIDTH]
        v2 = kv_s[pl.ds(r0, 2 * CHUNK), KV_WIDTH:2 * KV_WIDTH]
        k2r = pltpu.roll(k2, HEAD_DIM, 1)
        v2r = pltpu.roll(v2, HEAD_DIM, 1)
        kd = (jnp.where(lo2, k2, k2r).astype(BF16), jnp.where(lo2, k2r, k2).astype(BF16))
        vd = (jnp.where(lo2, v2, v2r).astype(BF16), jnp.where(lo2, v2r, v2).astype(BF16))
        prev_ok = (j > 0) | (c > 0)
        masked = from_prev & jnp.logical_not(prev_ok)
        yb_slabs = []
        for kv in range(2):
            q0 = z_s[rows, 2 * A_WIDTH + (2 * kv) * LANES:2 * A_WIDTH + (2 * kv + 1) * LANES]
            q1 = z_s[rows, 2 * A_WIDTH + (2 * kv + 1) * LANES:2 * A_WIDTH + (2 * kv + 2) * LANES]
            lhs = jnp.concatenate([jnp.where(lo, q0, 0.0), jnp.where(lo, 0.0, q0),
                                   jnp.where(lo, q1, 0.0), jnp.where(lo, 0.0, q1)], axis=0).astype(BF16)
            s_all = lax.dot_general(lhs, kd[kv], (((1,), (1,)), ((), ())), preferred_element_type=F32)
            probs = []
            for i in range(4):
                h = 4 * kv + i
                slope = 2.0 ** (-(h + 1))
                sink = sinks_ref[h]
                sh = s_all[i * CHUNK:(i + 1) * CHUNK]
                s = jnp.where(from_prev, sh[:, :CHUNK], sh[:, CHUNK:]) * (HEAD_DIM ** -0.5) - slope * dist
                s = jnp.where(masked, NEG, s)
                mx = jnp.maximum(jnp.max(s, axis=-1, keepdims=True), sink)
                e = jnp.exp(s - mx)
                den = jnp.sum(e, axis=-1, keepdims=True) + jnp.exp(sink - mx)
                p = e * (1.0 / den)
                probs.append(jnp.concatenate([jnp.where(from_prev, p, 0.0), jnp.where(from_prev, 0.0, p)], axis=-1))
            pm = jnp.concatenate(probs, axis=0).astype(BF16)
            o = jnp.dot(pm, vd[kv], preferred_element_type=F32)
            yb_slabs.append(jnp.where(lo, o[0:CHUNK], o[CHUNK:2 * CHUNK]))
            yb_slabs.append(jnp.where(lo, o[2 * CHUNK:3 * CHUNK], o[3 * CHUNK:4 * CHUNK]))
        yb_n = _rms(jnp.concatenate(yb_slabs, axis=-1), gob_ref[...])
        cat_s[rows, 0:A_WIDTH] = ya_n.astype(BF16)
        cat_s[rows, A_WIDTH:] = yb_n.astype(BF16)
        return carry

    lax.fori_loop(0, TM // CHUNK, chunk_body, 0)

    kv_s[0:CHUNK, :] = kv_s[TM:TM + CHUNK, :]
    k_ref[...] = kv_s[TM:TM + CHUNK, 0:KV_WIDTH]
    v_ref[...] = kv_s[TM:TM + CHUNK, KV_WIDTH:]

    h1 = x_ref[...] + jnp.dot(cat_s[...], wo_ref[...], preferred_element_type=F32)
    h1_ref[...] = h1
    xn2 = _rms(h1, gmoe_ref[...])
    xn_ref[...] = xn2.astype(BF16)
    gates, sel = _route(xn2, wr_ref, br_ref)
    gm_ref[...] = gates
    sm_ref[...] = sel


def _full(shape):
    n = len(shape)
    return pl.BlockSpec(shape, lambda *_: (0,) * n)


def _prompt_call(x, sinks, gmix, win, lng, lnb, wsp, bsp, goa, gob, wo, gmoe, wr, br):
    real = N_PROMPT // TM
    row = lambda g: (g, 0)
    seq = lambda g: (jnp.minimum(g, real - 1) // (SEQ // TM), 0, 0)
    return pl.pallas_call(
        _prompt_kernel,
        grid=(N_PAD // TM,),
        in_specs=[
            pl.BlockSpec(memory_space=pltpu.SMEM),
            pl.BlockSpec((TM, D_MODEL), lambda g: (jnp.minimum(g, real - 1), 0)),
            _full((1, D_MODEL)), _full((D_MODEL, IN_WIDTH)), _full((1, A_WIDTH)), _full((1, A_WIDTH)),
            _full((8, CHUNK, CHUNK)), _full((CHUNK, A_WIDTH)), _full((1, A_WIDTH)), _full((1, B_WIDTH)),
            _full((D_MODEL, D_MODEL)), _full((1, D_MODEL)), _full((D_MODEL, LANES)), _full((1, LANES)),
        ],
        out_specs=[
            pl.BlockSpec((TM, D_MODEL), row),
            pl.BlockSpec((TM, D_MODEL), row),
            pl.BlockSpec((TM, LANES), row),
            pl.BlockSpec((TM, LANES), row),
            pl.BlockSpec((None, CHUNK, KV_WIDTH), seq),
            pl.BlockSpec((None, CHUNK, KV_WIDTH), seq),
        ],
        out_shape=[
            jax.ShapeDtypeStruct((N_PAD, D_MODEL), F32),
            jax.ShapeDtypeStruct((N_PAD, D_MODEL), BF16),
            jax.ShapeDtypeStruct((N_PAD, LANES), F32),
            jax.ShapeDtypeStruct((N_PAD, LANES), F32),
            jax.ShapeDtypeStruct((BATCH, CHUNK, KV_WIDTH), F32),
            jax.ShapeDtypeStruct((BATCH, CHUNK, KV_WIDTH), F32),
        ],
        scratch_shapes=[
            pltpu.VMEM((TM, IN_WIDTH), F32),
            pltpu.VMEM((TM + CHUNK, 2 * KV_WIDTH), F32),
            pltpu.VMEM((TM, D_MODEL), BF16),
        ],
        compiler_params=pltpu.CompilerParams(
            dimension_semantics=("arbitrary",), vmem_limit_bytes=VMEM_LIMIT),
        name="prompt_premoe",
    )(sinks, x, gmix, win, lng, lnb, wsp, bsp, goa, gob, wo, gmoe, wr, br)


def _sample_kernel(sinks_ref, x_ref, ck_ref, cv_ref, gmix_ref, win_ref, lng_ref, lnb_ref, w00_ref, b0_ref,
                   goa_ref, gob_ref, wo_ref, gmoe_ref, wr_ref, br_ref,
                   h1_in, xn_in, gm_in, sm_in,
                   h1_ref, xn_ref, gm_ref, sm_ref, nk_ref, nv_ref, va_ref):
    del h1_in, xn_in, gm_in, sm_in
    t = SAMPLE_TILE
    nkeys = t * CHUNK

    if True:
        x = x_ref[...]
        xn = _rms(x, gmix_ref[...]).astype(BF16)
        z = jnp.dot(xn, win_ref[...], preferred_element_type=F32)
        u = _gelu(z[:, 0:A_WIDTH])
        va = _layernorm(_gelu(z[:, A_WIDTH:2 * A_WIDTH]), lng_ref[...], lnb_ref[...])
        va_ref[...] = va
        ya_n = _rms(u * (w00_ref[...] * va + b0_ref[...]), goa_ref[...])

        knew = z[:, 2 * A_WIDTH + B_WIDTH:2 * A_WIDTH + B_WIDTH + KV_WIDTH]
        vnew = z[:, 2 * A_WIDTH + B_WIDTH + KV_WIDTH:]
        lane = _lane_iota((t, LANES))
        lo = lane < HEAD_DIM
        stacked = []
        for h in range(B_HEADS):
            q = z[:, 2 * A_WIDTH + (h // 2) * LANES:2 * A_WIDTH + (h // 2 + 1) * LANES]
            qh = jnp.where(lo if h % 2 == 0 else jnp.logical_not(lo), q, 0.0)
            if h % 2 != h // 4:
                qh = pltpu.roll(qh, HEAD_DIM, 1)
            stacked.append(qh)
        qs = jnp.concatenate(stacked, axis=0)
        rows = B_HEADS * t
        ridx = lax.broadcasted_iota(jnp.int32, (rows, 1), 0)
        slope = jnp.zeros((rows, 1), F32)
        sink = jnp.zeros((rows, 1), F32)
        for h in range(B_HEADS):
            in_h = (ridx >= h * t) & (ridx < (h + 1) * t)
            slope = jnp.where(in_h, 2.0 ** (-(h + 1)), slope)
            sink = jnp.where(in_h, sinks_ref[h], sink)
        s_c = lax.dot_general(qs.astype(BF16), ck_ref[...].astype(BF16), (((1,), (1,)), ((), ())),
                              preferred_element_type=F32)
        rsamp = lax.broadcasted_iota(jnp.int32, (rows, nkeys), 0) % t
        col = lax.broadcasted_iota(jnp.int32, (rows, nkeys), 1)
        pos = col % CHUNK
        own = ((col // CHUNK) == rsamp) & (pos >= 1)
        s_c = s_c * (HEAD_DIM ** -0.5) - slope * (CHUNK - pos).astype(F32)
        s_c = jnp.where(own, s_c, NEG)
        kn8 = jnp.concatenate([knew] * B_HEADS, axis=0)
        vn8 = jnp.concatenate([vnew] * B_HEADS, axis=0)
        s_n = jnp.sum(qs * kn8, axis=-1, keepdims=True) * (HEAD_DIM ** -0.5)
        mx = jnp.maximum(jnp.maximum(jnp.max(s_c, axis=-1, keepdims=True), s_n), sink)
        e_c = jnp.exp(s_c - mx)
        e_n = jnp.exp(s_n - mx)
        inv = 1.0 / (jnp.sum(e_c, axis=-1, keepdims=True) + e_n + jnp.exp(sink - mx))
        o = jnp.dot((e_c * inv).astype(BF16), cv_ref[...].astype(BF16), preferred_element_type=F32)
        o = o + (e_n * inv) * vn8
        yb_slabs = []
        for p in range(B_WIDTH // LANES):
            outs = []
            for half in range(2):
                h = 2 * p + half
                oh = o[h * t:(h + 1) * t]
                oh = jnp.where(lo if h // 4 == 0 else jnp.logical_not(lo), oh, 0.0)
                if half != h // 4:
                    oh = pltpu.roll(oh, HEAD_DIM, 1)
                outs.append(oh)
            yb_slabs.append(outs[0] + outs[1])
        yb_n = _rms(jnp.concatenate(yb_slabs, axis=-1), gob_ref[...])

        cat = jnp.concatenate([ya_n, yb_n], axis=-1).astype(BF16)
        h1 = x + jnp.dot(cat, wo_ref[...], preferred_element_type=F32)
        xn2 = _rms(h1, gmoe_ref[...])
        gates, sel = _route(xn2, wr_ref, br_ref)
        h1_ref[...] = h1
        xn_ref[...] = xn2.astype(BF16)
        gm_ref[...] = gates
        sm_ref[...] = sel

        nk_ref[...] = pltpu.roll(ck_ref[...], nkeys - 1, 0)
        nv_ref[...] = pltpu.roll(cv_ref[...], nkeys - 1, 0)
        for b in range(t):
            nk_ref[b * CHUNK + CHUNK - 1:b * CHUNK + CHUNK, :] = knew[b:b + 1, :]
            nv_ref[b * CHUNK + CHUNK - 1:b * CHUNK + CHUNK, :] = vnew[b:b + 1, :]


def _sample_call(x, ck, cv, sinks, gmix, win, lng, lnb, w00, b0, goa, gob, wo, gmoe, wr, br, h1, xn, gm, sm):
    t = SAMPLE_TILE
    steps = DEC_BATCH // t
    base = N_PROMPT // t
    inrow = lambda i: (i, 0)
    outrow = lambda i: (base + i, 0)
    anyspec = pl.BlockSpec(memory_space=pl.ANY)
    return pl.pallas_call(
        _sample_kernel,
        grid=(steps,),
        in_specs=[
            pl.BlockSpec(memory_space=pltpu.SMEM),
            pl.BlockSpec((t, D_MODEL), inrow),
            pl.BlockSpec((t * CHUNK, KV_WIDTH), inrow),
            pl.BlockSpec((t * CHUNK, KV_WIDTH), inrow),
            _full((1, D_MODEL)), _full((D_MODEL, IN_WIDTH)), _full((1, A_WIDTH)), _full((1, A_WIDTH)),
            _full((1, A_WIDTH)), _full((1, A_WIDTH)), _full((1, A_WIDTH)), _full((1, B_WIDTH)),
            _full((D_MODEL, D_MODEL)), _full((1, D_MODEL)), _full((D_MODEL, LANES)), _full((1, LANES)),
            anyspec, anyspec, anyspec, anyspec,
        ],
        out_specs=[
            pl.BlockSpec((t, D_MODEL), outrow),
            pl.BlockSpec((t, D_MODEL), outrow),
            pl.BlockSpec((t, LANES), outrow),
            pl.BlockSpec((t, LANES), outrow),
            pl.BlockSpec((t * CHUNK, KV_WIDTH), inrow),
            pl.BlockSpec((t * CHUNK, KV_WIDTH), inrow),
            pl.BlockSpec((t, A_WIDTH), inrow),
        ],
        out_shape=[
            jax.ShapeDtypeStruct((N_PAD, D_MODEL), F32),
            jax.ShapeDtypeStruct((N_PAD, D_MODEL), BF16),
            jax.ShapeDtypeStruct((N_PAD, LANES), F32),
            jax.ShapeDtypeStruct((N_PAD, LANES), F32),
            jax.ShapeDtypeStruct((DEC_BATCH * CHUNK, KV_WIDTH), F32),
            jax.ShapeDtypeStruct((DEC_BATCH * CHUNK, KV_WIDTH), F32),
            jax.ShapeDtypeStruct((DEC_BATCH, A_WIDTH), F32),
        ],
        input_output_aliases={16: 0, 17: 1, 18: 2, 19: 3},
        compiler_params=pltpu.CompilerParams(
            dimension_semantics=("arbitrary",), vmem_limit_bytes=VMEM_LIMIT),
        name="sample_premoe",
    )(sinks, x, ck, cv, gmix, win, lng, lnb, w00, b0, goa, gob, wo, gmoe, wr, br, h1, xn, gm, sm)


def _decode_kernel(sinks_ref, x_ref, ck_ref, cv_ref, gmix_ref, win_ref, lng_ref, lnb_ref, w00_ref, b0_ref,
                   goa_ref, gob_ref, wo_ref, gmoe_ref, wr_ref, br_ref,
                   h1_in, xn_in, gm_in, sm_in,
                   h1_ref, xn_ref, gm_ref, sm_ref, nk_ref, nv_ref, va_ref,
                   q_s, kn_s, vn_s, ya_s, yb_s):
    del h1_in, xn_in, gm_in, sm_in
    i = pl.program_id(0)
    t = DEC_TILE
    per_seq = CHUNK * 2
    ncols = t * per_seq

    @pl.when(i == 0)
    def _():
        xn = _rms(x_ref[...], gmix_ref[...]).astype(BF16)
        z = jnp.dot(xn, win_ref[...], preferred_element_type=F32)
        u = _gelu(z[:, 0:A_WIDTH])
        va = _layernorm(_gelu(z[:, A_WIDTH:2 * A_WIDTH]), lng_ref[...], lnb_ref[...])
        va_ref[...] = va
        ya_s[...] = _rms(u * (w00_ref[...] * va + b0_ref[...]), goa_ref[...])
        q_s[...] = z[:, 2 * A_WIDTH:2 * A_WIDTH + B_WIDTH]
        kn_s[...] = z[:, 2 * A_WIDTH + B_WIDTH:2 * A_WIDTH + B_WIDTH + KV_WIDTH]
        vn_s[...] = z[:, 2 * A_WIDTH + B_WIDTH + KV_WIDTH:]

    rows = pl.ds(pl.multiple_of(i * t, t), t)
    q = q_s[rows, :]
    kn = kn_s[rows, :]
    vn = vn_s[rows, :]
    qs = jnp.concatenate([q[:, h * HEAD_DIM:(h + 1) * HEAD_DIM] for h in range(B_HEADS)], axis=0)
    kn8 = jnp.concatenate([kn[:, (h // 4) * HEAD_DIM:(h // 4 + 1) * HEAD_DIM] for h in range(B_HEADS)], axis=0)
    vn8 = jnp.concatenate([vn[:, (h // 4) * HEAD_DIM:(h // 4 + 1) * HEAD_DIM] for h in range(B_HEADS)], axis=0)
    nrows = B_HEADS * t
    ridx = lax.broadcasted_iota(jnp.int32, (nrows, 1), 0)
    slope = jnp.zeros((nrows, 1), F32)
    sink = jnp.zeros((nrows, 1), F32)
    for h in range(B_HEADS):
        in_h = (ridx >= h * t) & (ridx < (h + 1) * t)
        slope = jnp.where(in_h, 2.0 ** (-(h + 1)), slope)
        sink = jnp.where(in_h, sinks_ref[h], sink)
    s_c = lax.dot_general(qs.astype(BF16), ck_ref[...].astype(BF16), (((1,), (1,)), ((), ())),
                          preferred_element_type=F32)
    rr = lax.broadcasted_iota(jnp.int32, (nrows, ncols), 0)
    col = lax.broadcasted_iota(jnp.int32, (nrows, ncols), 1)
    pos = (col // 2) % CHUNK
    own = ((col // per_seq) == (rr % t)) & ((col % 2) == (rr // (4 * t))) & (pos >= 1)
    s_c = s_c * (HEAD_DIM ** -0.5) - slope * (CHUNK - pos).astype(F32)
    s_c = jnp.where(own, s_c, NEG)
    s_n = jnp.sum(qs * kn8, axis=-1, keepdims=True) * (HEAD_DIM ** -0.5)
    mx = jnp.maximum(jnp.maximum(jnp.max(s_c, axis=-1, keepdims=True), s_n), sink)
    e_c = jnp.exp(s_c - mx)
    e_n = jnp.exp(s_n - mx)
    inv = 1.0 / (jnp.sum(e_c, axis=-1, keepdims=True) + e_n + jnp.exp(sink - mx))
    o = jnp.dot((e_c * inv).astype(BF16), cv_ref[...].astype(BF16), preferred_element_type=F32)
    o = o + (e_n * inv) * vn8
    yb_s[rows, :] = jnp.concatenate([o[h * t:(h + 1) * t] for h in range(B_HEADS)], axis=-1)

    nk_ref[...] = pltpu.roll(ck_ref[...], ncols - 2, 0)
    nv_ref[...] = pltpu.roll(cv_ref[...], ncols - 2, 0)
    for b in range(t):
        for kv in range(2):
            r = b * per_seq + per_seq - 2 + kv
            nk_ref[r:r + 1, :] = kn[b:b + 1, kv * HEAD_DIM:(kv + 1) * HEAD_DIM]
            nv_ref[r:r + 1, :] = vn[b:b + 1, kv * HEAD_DIM:(kv + 1) * HEAD_DIM]

    @pl.when(i == pl.num_programs(0) - 1)
    def _():
        yb_n = _rms(yb_s[...], gob_ref[...])
        cat = jnp.concatenate([ya_s[...], yb_n], axis=-1).astype(BF16)
        h1 = x_ref[...] + jnp.dot(cat, wo_ref[...], preferred_element_type=F32)
        xn2 = _rms(h1, gmoe_ref[...])
        gates, sel = _route(xn2, wr_ref, br_ref)
        h1_ref[...] = h1
        xn_ref[...] = xn2.astype(BF16)
        gm_ref[...] = gates
        sm_ref[...] = sel


def _decode_call(x, ck, cv, sinks, gmix, win, lng, lnb, w00, b0, goa, gob, wo, gmoe, wr, br, h1, xn, gm, sm):
    t = DEC_TILE
    per_seq = CHUNK * 2
    cache = pl.BlockSpec((t * per_seq, HEAD_DIM), lambda i: (i, 0))
    tok = lambda width: pl.BlockSpec((DEC_BATCH, width), lambda i: (N_PROMPT // DEC_BATCH, 0))
    anyspec = pl.BlockSpec(memory_space=pl.ANY)
    return pl.pallas_call(
        _decode_kernel,
        grid=(DEC_BATCH // t,),
        in_specs=[
            pl.BlockSpec(memory_space=pltpu.SMEM),
            _full((DEC_BATCH, D_MODEL)), cache, cache,
            _full((1, D_MODEL)), _full((D_MODEL, IN_WIDTH)), _full((1, A_WIDTH)), _full((1, A_WIDTH)),
            _full((1, A_WIDTH)), _full((1, A_WIDTH)), _full((1, A_WIDTH)), _full((1, B_WIDTH)),
            _full((D_MODEL, D_MODEL)), _full((1, D_MODEL)), _full((D_MODEL, LANES)), _full((1, LANES)),
            anyspec, anyspec, anyspec, anyspec,
        ],
        out_specs=[tok(D_MODEL), tok(D_MODEL), tok(LANES), tok(LANES), cache, cache, _full((DEC_BATCH, A_WIDTH))],
        out_shape=[
            jax.ShapeDtypeStruct((N_PAD, D_MODEL), F32),
            jax.ShapeDtypeStruct((N_PAD, D_MODEL), BF16),
            jax.ShapeDtypeStruct((N_PAD, LANES), F32),
            jax.ShapeDtypeStruct((N_PAD, LANES), F32),
            jax.ShapeDtypeStruct((DEC_BATCH * per_seq, HEAD_DIM), F32),
            jax.ShapeDtypeStruct((DEC_BATCH * per_seq, HEAD_DIM), F32),
            jax.ShapeDtypeStruct((DEC_BATCH, A_WIDTH), F32),
        ],
        scratch_shapes=[pltpu.VMEM((DEC_BATCH, B_WIDTH), F32), pltpu.VMEM((DEC_BATCH, KV_WIDTH), F32),
                        pltpu.VMEM((DEC_BATCH, KV_WIDTH), F32), pltpu.VMEM((DEC_BATCH, A_WIDTH), F32),
                        pltpu.VMEM((DEC_BATCH, B_WIDTH), F32)],
        input_output_aliases={16: 0, 17: 1, 18: 2, 19: 3},
        compiler_params=pltpu.CompilerParams(
            dimension_semantics=("arbitrary",), vmem_limit_bytes=VMEM_LIMIT),
        name="sample_premoe",
    )(sinks, x, ck, cv, gmix, win, lng, lnb, w00, b0, goa, gob, wo, gmoe, wr, br, h1, xn, gm, sm)


def _plan_kernel(sm_ref,
                 destm_ref, destt_ref, xbe_ref, nxblk_ref, stab_ref, ctab_ref, astart_ref, nwin_ref,
                 base_s, pstart_s):
    ph = pl.program_id(0)
    step = pl.program_id(1)
    lane = _lane_iota((1, LANES))

    @pl.when((ph == 0) & (step == 0))
    def _():
        base_s[...] = jnp.zeros_like(base_s)

    @pl.when(ph == 0)
    def _():
        base_s[...] += jnp.sum(sm_ref[...], axis=0, keepdims=True)

    @pl.when((ph == 1) & (step == 0))
    def _():
        counts = base_s[...]
        padded = jnp.floor((counts + (DISP_CHUNK + EXP_BLOCK - 1)) * (1.0 / EXP_BLOCK)) * EXP_BLOCK
        padded = jnp.where(counts > 0.0, padded, 0.0)
        pend = padded
        for s in (1, 2, 4, 8, 16):
            pend = pend + jnp.where(lane >= s, pltpu.roll(pend, s, 1), 0.0)
        spare = (N_ROWS + lane * DISP_CHUNK).astype(F32)
        pstart_s[...] = jnp.where(counts > 0.0, pend - padded, spare)
        base_s[...] = jnp.zeros_like(base_s)
        brow = lax.broadcasted_iota(jnp.int32, (XBE_ROWS, LANES), 0).astype(F32) * EXP_BLOCK
        done = jnp.where((lane < N_EXPERTS) & (pend <= brow), 1.0, 0.0)
        be = jnp.minimum(jnp.sum(done, axis=-1, keepdims=True), N_EXPERTS - 1.0)
        xbe_ref[...] = jnp.broadcast_to(be, (XBE_ROWS, LANES)).astype(jnp.int32)
        total = jnp.sum(jnp.where(lane == N_EXPERTS - 1, pend, 0.0), axis=-1, keepdims=True)
        nxblk_ref[...] = jnp.broadcast_to(total * (1.0 / EXP_BLOCK), (8, LANES)).astype(jnp.int32)
        stab_ref[...] = jnp.zeros_like(stab_ref)
        ctab_ref[...] = jnp.zeros_like(ctab_ref)
        astart_ref[...] = jnp.zeros_like(astart_ref)
        nwin_ref[...] = jnp.zeros_like(nwin_ref)

    @pl.when(ph == 1)
    def _():
        r = lax.broadcasted_iota(jnp.int32, (TOK_TILE, TOK_TILE), 0)
        c = lax.broadcasted_iota(jnp.int32, (TOK_TILE, TOK_TILE), 1)
        lower = jnp.where(c < r, 1.0, 0.0).astype(BF16)
        for q in range(TM // TOK_TILE):
            i = step * (TM // TOK_TILE) + q
            sel = sm_ref[q * TOK_TILE:(q + 1) * TOK_TILE, :]
            cnt = jnp.sum(sel, axis=0, keepdims=True)
            prefix = jnp.dot(lower, sel.astype(BF16), preferred_element_type=F32)
            start = pstart_s[...] + base_s[...]
            dest = jnp.where(sel > 0.0, prefix + start, -1.0)
            destm_ref[q * TOK_TILE:(q + 1) * TOK_TILE, :] = dest
            destt_ref[:, q * TOK_TILE:(q + 1) * TOK_TILE] = dest.T
            has = (cnt > 0.0) & (lane < N_EXPERTS)
            stab_ref[pl.ds(i, 1), :] = start.astype(jnp.int32)
            ctab_ref[pl.ds(i, 1), :] = jnp.where(has, cnt, 0.0).astype(jnp.int32)
            a = jnp.minimum(start, float(N_ROWS - CWIN))
            nw = jnp.where(has, jnp.floor((start + cnt - a + (CWIN - 1)) * (1.0 / CWIN)), 0.0)
            astart_ref[pl.ds(i, 1), :] = a.astype(jnp.int32)
            nwin_ref[pl.ds(i, 1), :] = nw.astype(jnp.int32)
            base_s[...] += cnt


def _plan_call(sm):
    tile = lambda ph, i: (i * ph, 0)
    tile_t = lambda ph, i: (0, i * ph)
    tab = jax.ShapeDtypeStruct((TAB_ROWS, LANES), jnp.int32)
    return pl.pallas_call(
        _plan_kernel,
        grid=(2, N_PAD // TM),
        in_specs=[pl.BlockSpec((TM, LANES), lambda ph, i: (i, 0))],
        out_specs=[
            pl.BlockSpec((TM, LANES), tile),
            pl.BlockSpec((LANES, TM), tile_t),
            _full((XBE_ROWS, LANES)), _full((8, LANES)),
            _full((TAB_ROWS, LANES)), _full((TAB_ROWS, LANES)), _full((TAB_ROWS, LANES)), _full((TAB_ROWS, LANES)),
        ],
        out_shape=[
            jax.ShapeDtypeStruct((N_PAD, LANES), F32),
            jax.ShapeDtypeStruct((LANES, N_PAD), F32),
            jax.ShapeDtypeStruct((XBE_ROWS, LANES), jnp.int32),
            jax.ShapeDtypeStruct((8, LANES), jnp.int32),
            tab, tab, tab, tab,
        ],
        scratch_shapes=[pltpu.VMEM((1, LANES), F32), pltpu.VMEM((1, LANES), F32)],
        compiler_params=pltpu.CompilerParams(
            dimension_semantics=("arbitrary", "arbitrary"), vmem_limit_bytes=VMEM_LIMIT),
        name="moe_plan",
    )(sm)


def _pack_rows(z):
    half = D_MODEL // 2
    lo = lax.bitcast_convert_type(z[:, :half], jnp.uint32) >> 16
    hi = lax.bitcast_convert_type(z[:, half:], jnp.uint32) & jnp.uint32(0xFFFF0000)
    return lax.bitcast_convert_type(hi | lo, jnp.int32)


def _unpack_rows(ref):
    rows = ref.shape[0]
    flat = ref.reshape(rows * PACK, LANES)
    lo, hi = [], []
    for s in range(PACK):
        w = lax.bitcast_convert_type(flat[pl.ds(s, rows, stride=PACK), :], jnp.uint32)
        lo.append(lax.bitcast_convert_type(w << 16, F32))
        hi.append(lax.bitcast_convert_type(w & jnp.uint32(0xFFFF0000), F32))
    return jnp.concatenate(lo + hi, axis=-1).astype(BF16)


def _dispatch_kernel(stab_ref, ctab_ref, cmax_ref, xn_ref, destt_ref, xs_in, xs_ref,
                     stage0, stage1, stage2, sems, sem2):
    del xs_in
    i = pl.program_id(0)
    last = pl.num_programs(0) - 1
    x = xn_ref[...]
    dt = destt_ref[...]
    rio = lax.broadcasted_iota(jnp.int32, (DISP_CHUNK, 1), 0).astype(F32)

    def chunk_rows(j, stage):
        parts = []
        for e in range(N_EXPERTS):
            first = (stab_ref[i * N_EXPERTS + e] + j * DISP_CHUNK).astype(F32)
            parts.append(jnp.where(dt[e:e + 1, :] == first + rio, 1.0, 0.0).astype(BF16))
        onehot = jnp.concatenate(parts, axis=0)
        words = _pack_rows(jnp.dot(onehot, x, preferred_element_type=F32))
        for s in range(PACK):
            stage[pl.ds(s, N_EXPERTS * DISP_CHUNK, stride=PACK), :] = words[:, s * LANES:(s + 1) * LANES]

    def copy(stage, step, e, j, sem):
        first = stab_ref[step * N_EXPERTS + e] + j * DISP_CHUNK
        rows = stage.reshape(N_EXPERTS * DISP_CHUNK, PACK, LANES)
        return pltpu.make_async_copy(rows.at[pl.ds(e * DISP_CHUNK, DISP_CHUNK)],
                                     xs_ref.at[pl.ds(first, DISP_CHUNK)], sem)

    def step_body(stage, prev_stage, par):
        chunk_rows(0, stage)

        @pl.when(i > 0)
        def _():
            for e in range(N_EXPERTS):
                copy(prev_stage, i - 1, e, 0, sems.at[1 - par, e]).wait()

        for e in range(N_EXPERTS):
            copy(stage, i, e, 0, sems.at[par, e]).start()

        @pl.when(i == last)
        def _():
            for e in range(N_EXPERTS):
                copy(stage, i, e, 0, sems.at[par, e]).wait()

    @pl.when(i % 2 == 0)
    def _():
        step_body(stage0, stage1, 0)

    @pl.when(i % 2 == 1)
    def _():
        step_body(stage1, stage0, 1)

    for j in range(1, TOK_TILE // DISP_CHUNK):

        @pl.when(cmax_ref[i] > j * DISP_CHUNK)
        def _(j=j):
            chunk_rows(j, stage2)
            for e in range(N_EXPERTS):

                @pl.when(ctab_ref[i * N_EXPERTS + e] > j * DISP_CHUNK)
                def _(e=e):
                    cp = copy(stage2, i, e, j, sem2)
                    cp.start()
                    cp.wait()


def _dispatch_call(stab, ctab, cmax, xn, destt, xs_zero):
    stage = pltpu.VMEM((N_EXPERTS * DISP_CHUNK * PACK, LANES), jnp.int32)
    grid_spec = pltpu.PrefetchScalarGridSpec(
        num_scalar_prefetch=3,
        grid=(N_PAD_TILES,),
        in_specs=[
            pl.BlockSpec((TOK_TILE, D_MODEL), lambda i, *_: (i, 0)),
            pl.BlockSpec((N_EXPERTS, TOK_TILE), lambda i, *_: (0, i)),
            pl.BlockSpec(memory_space=pl.ANY),
        ],
        out_specs=pl.BlockSpec(memory_space=pl.ANY),
        scratch_shapes=[stage, stage, stage, pltpu.SemaphoreType.DMA((2, N_EXPERTS)),
                        pltpu.SemaphoreType.DMA],
    )
    return pl.pallas_call(
        _dispatch_kernel,
        grid_spec=grid_spec,
        out_shape=jax.ShapeDtypeStruct((XS_ROWS, PACK, LANES), jnp.int32),
        input_output_aliases={5: 0},
        compiler_params=pltpu.CompilerParams(
            dimension_semantics=("arbitrary",), vmem_limit_bytes=VMEM_LIMIT),
        name="moe_dispatch",
    )(stab, ctab, cmax, xn, destt, xs_zero)


def _expert_kernel(blke_ref, nblk_ref, next_ref, xs_ref, wgu_hbm, bgu_ref, wdn_hbm, bdn_ref,
                   ys_ref, wgu_f, wdn_f, wgu_s, wdn_s, sems):
    b = pl.program_id(0)
    used = b < nblk_ref[0]
    prev = blke_ref[jnp.maximum(b - 1, 0)]
    fresh = used & ((b == 0) | (blke_ref[b] != prev))

    def fetch(e):
        return (pltpu.make_async_copy(wgu_hbm.at[e], wgu_f, sems.at[0]),
                pltpu.make_async_copy(wdn_hbm.at[e], wdn_f, sems.at[1]))

    @pl.when(b == 0)
    def _():
        for cp in fetch(blke_ref[0]):
            cp.start()

    @pl.when(fresh)
    def _():
        for cp in fetch(blke_ref[b]):
            cp.wait()
        wgu_s[...] = wgu_f[...].astype(BF16)
        wdn_s[...] = wdn_f[...].astype(BF16)

        @pl.when(next_ref[b] >= 0)
        def _():
            for cp in fetch(next_ref[b]):
                cp.start()

    @pl.when(used)
    def _():
        hid = jnp.dot(_unpack_rows(xs_ref), wgu_s[...], preferred_element_type=F32) + bgu_ref[...]
        gate = jnp.minimum(hid[:, :D_FF], SWIGLU_LIMIT)
        up = jnp.clip(hid[:, D_FF:], -SWIGLU_LIMIT, SWIGLU_LIMIT)
        act = (up + 1.0) * gate * jax.nn.sigmoid(SWIGLU_ALPHA * gate)
        y = jnp.dot(act.astype(BF16), wdn_s[...], preferred_element_type=F32) + bdn_ref[...]
        words = _pack_rows(y.astype(BF16).astype(F32))
        flat = ys_ref.reshape(EXP_BLOCK * PACK, LANES)
        for s in range(PACK):
            flat[pl.ds(s, EXP_BLOCK, stride=PACK), :] = words[:, s * LANES:(s + 1) * LANES]

    @pl.when(b >= nblk_ref[0])
    def _():
        ys_ref[...] = jnp.zeros_like(ys_ref)


def _expert_call(blke, nblk, nxt, xs, wgu, bgu, wdn, bdn):
    grid_spec = pltpu.PrefetchScalarGridSpec(
        num_scalar_prefetch=3,
        grid=(N_XBLOCKS,),
        in_specs=[
            pl.BlockSpec((EXP_BLOCK, PACK, LANES), lambda b, be, nb, nx: (b, 0, 0)),
            pl.BlockSpec(memory_space=pl.ANY),
            pl.BlockSpec((None, 1, 2 * D_FF), lambda b, be, nb, nx: (be[b], 0, 0)),
            pl.BlockSpec(memory_space=pl.ANY),
            pl.BlockSpec((None, 1, D_MODEL), lambda b, be, nb, nx: (be[b], 0, 0)),
        ],
        out_specs=pl.BlockSpec((EXP_BLOCK, PACK, LANES), lambda b, be, nb, nx: (b, 0, 0)),
        scratch_shapes=[pltpu.VMEM((D_MODEL, 2 * D_FF), F32), pltpu.VMEM((D_FF, D_MODEL), F32),
                        pltpu.VMEM((D_MODEL, 2 * D_FF), BF16), pltpu.VMEM((D_FF, D_MODEL), BF16),
                        pltpu.SemaphoreType.DMA((2,))],
    )
    return pl.pallas_call(
        _expert_kernel,
        grid_spec=grid_spec,
        out_shape=jax.ShapeDtypeStruct((N_ROWS, PACK, LANES), jnp.int32),
        compiler_params=pltpu.CompilerParams(
            dimension_semantics=("arbitrary",), vmem_limit_bytes=VMEM_LIMIT),
        name="moe_experts",
    )(blke, nblk, nxt, xs, wgu, bgu, wdn, bdn)


def _combine_kernel(*refs):
    astart_ref, nwin_ref, over_ref = refs[0:3]
    win_refs = refs[3:3 + N_EXPERTS]
    (destm_ref, gm_ref, h1_ref, plep_ref, ples_ref, gple_ref, wpg_ref, wpp_ref, gfin_ref, ys_any,
     yp_ref, ysm_ref, moe_s, tmp_s, sem) = refs[3 + N_EXPERTS:]
    i = pl.program_id(0)
    dest = destm_ref[...]
    gates = gm_ref[...]
    lane = _lane_iota((TOK_TILE, LANES))
    lane_f = lane.astype(F32)
    lo = lane < WIN
    moe = jnp.zeros((TOK_TILE, D_MODEL), F32)
    group = 4
    for g0 in range(0, N_EXPERTS, group):
        g_hi, g_lo = [], []
        for p in range(group // 2):
            e0 = g0 + 2 * p
            a0 = astart_ref[i * N_EXPERTS + e0].astype(F32)
            a1 = astart_ref[i * N_EXPERTS + e0 + 1].astype(F32)
            rowid = jnp.where(lo, a0 + lane_f, a1 + lane_f - WIN)
            dcol = jnp.where(lo, dest[:, e0:e0 + 1], dest[:, e0 + 1:e0 + 2])
            gcol = jnp.where(lo, gates[:, e0:e0 + 1], gates[:, e0 + 1:e0 + 2])
            gsel = jnp.where(dcol == rowid, gcol, 0.0)
            hi = gsel.astype(BF16)
            g_hi.append(hi)
            g_lo.append((gsel - hi.astype(F32)).astype(BF16))
        ywin = jnp.concatenate([win_refs[g0 + q][...] for q in range(group)], axis=0)
        both = jnp.concatenate([jnp.concatenate(g_hi, axis=-1), jnp.concatenate(g_lo, axis=-1)], axis=0)
        r = jnp.dot(both, ywin, preferred_element_type=F32)
        moe = moe + r[:TOK_TILE] + r[TOK_TILE:]
    moe_s[...] = moe

    @pl.when(over_ref[i] > 0)
    def _():
        tmp_s[...] = jnp.zeros_like(tmp_s)

        def per_expert(e, carry):
            a = astart_ref[i * N_EXPERTS + e]
            dcol = jnp.sum(jnp.where(lane == e, dest, 0.0), axis=-1, keepdims=True)
            gcol = jnp.sum(jnp.where(lane == e, gates, 0.0), axis=-1, keepdims=True)

            def per_window(w, carry2):
                first = a + w * WIN
                start = pl.multiple_of(jnp.minimum(first, N_ROWS - WIN), WIN_ALIGN)
                cp = pltpu.make_async_copy(ys_any.at[pl.ds(start, WIN)], tmp_s.at[pl.ds(0, WIN)], sem)
                cp.start()
                cp.wait()
                hit = lo & (dcol == start.astype(F32) + lane_f) & (dcol >= first.astype(F32))
                gsel = jnp.where(hit, gcol, 0.0)
                hi = gsel.astype(BF16)
                rest = (gsel - hi.astype(F32)).astype(BF16)
                moe_s[...] += (jnp.dot(hi, tmp_s[...], preferred_element_type=F32)
                               + jnp.dot(rest, tmp_s[...], preferred_element_type=F32))
                return carry2

            return lax.fori_loop(1, nwin_ref[i * N_EXPERTS + e], per_window, carry)

        lax.fori_loop(0, N_EXPERTS, per_expert, 0)

    is_sample = i == N_TILES - 1
    h2 = h1_ref[...] + moe_s[...]
    ple = jnp.where(is_sample, ples_ref[...], plep_ref[...])
    hn = _rms(h2, gple_ref[...]).astype(BF16)
    gate = jax.nn.sigmoid(jnp.dot(hn, wpg_ref[...], preferred_element_type=F32))
    proj = jnp.dot(ple.astype(BF16), wpp_ref[...], preferred_element_type=F32)
    y = _rms(h2 + gate * proj, gfin_ref[...])

    @pl.when(jnp.logical_not(is_sample))
    def _():
        yp_ref[...] = y

    @pl.when(is_sample)
    def _():
        ysm_ref[...] = y


def _combine_call(astart, nwin, over, ys, destm, gm, h1, plep, ples, gple, wpg, wpp, gfin):
    last = N_TILES - 2

    def win_spec(e):
        return pl.BlockSpec((pl.Element(WIN), pl.Element(D_MODEL)),
                            lambda i, a, nw, ov, e=e: (pl.multiple_of(a[i * N_EXPERTS + e], WIN_ALIGN), 0))

    grid_spec = pltpu.PrefetchScalarGridSpec(
        num_scalar_prefetch=3,
        grid=(N_TILES,),
        in_specs=[win_spec(e) for e in range(N_EXPERTS)] + [
            pl.BlockSpec((TOK_TILE, LANES), lambda i, *_: (i, 0)),
            pl.BlockSpec((TOK_TILE, LANES), lambda i, *_: (i, 0)),
            pl.BlockSpec((TOK_TILE, D_MODEL), lambda i, *_: (i, 0)),
            pl.BlockSpec((TOK_TILE, PLE_DIM), lambda i, *_: (jnp.minimum(i, last), 0)),
            pl.BlockSpec((TOK_TILE, PLE_DIM), lambda i, *_: (0, 0)),
            pl.BlockSpec((1, D_MODEL), lambda i, *_: (0, 0)),
            pl.BlockSpec((D_MODEL, D_MODEL), lambda i, *_: (0, 0)),
            pl.BlockSpec((PLE_DIM, D_MODEL), lambda i, *_: (0, 0)),
            pl.BlockSpec((1, D_MODEL), lambda i, *_: (0, 0)),
            pl.BlockSpec(memory_space=pl.ANY),
        ],
        out_specs=[
            pl.BlockSpec((TOK_TILE, D_MODEL), lambda i, *_: (jnp.minimum(i, last), 0)),
            pl.BlockSpec((TOK_TILE, D_MODEL), lambda i, *_: (0, 0)),
        ],
        scratch_shapes=[pltpu.VMEM((TOK_TILE, D_MODEL), F32), pltpu.VMEM((2 * WIN, D_MODEL), BF16),
                        pltpu.SemaphoreType.DMA],
    )
    return pl.pallas_call(
        _combine_kernel,
        grid_spec=grid_spec,
        out_shape=[jax.ShapeDtypeStruct((N_PROMPT, D_MODEL), F32),
                   jax.ShapeDtypeStruct((DEC_BATCH, D_MODEL), F32)],
        compiler_params=pltpu.CompilerParams(
            dimension_semantics=("arbitrary",), vmem_limit_bytes=VMEM_LIMIT),
        name="moe_combine_tail",
    )(astart, nwin, over, *([ys] * N_EXPERTS), destm, gm, h1, plep, ples, gple, wpg, wpp, gfin, ys)


def _gather_kernel(*refs):
    astart_ref, nwin_ref, over_ref = refs[0:3]
    win_refs = refs[3:3 + N_EXPERTS]
    (destm_ref, gm_ref, h1_ref, plep_ref, ples_ref, gple_ref, wpg_ref, wpp_ref, gfin_ref, ys_any,
     yp_ref, ysm_ref, moe_s, tmp_s, sem) = refs[3 + N_EXPERTS:]
    i = pl.program_id(0)
    slot = i % TAIL_TILES
    dest = destm_ref[...]
    gates = gm_ref[...]
    lane = _lane_iota((TOK_TILE, LANES))
    lane_f = lane.astype(F32)
    per_slab = LANES // CWIN
    within = (lane % CWIN).astype(F32)
    group = 2 * per_slab

    def split(gsel):
        hi = gsel.astype(BF16)
        return hi, (gsel - hi.astype(F32)).astype(BF16)

    moe = jnp.zeros((TOK_TILE, D_MODEL), F32)
    for g0 in range(0, N_EXPERTS, group):
        his, los = [], []
        for sl in range(2):
            rowid = jnp.zeros((TOK_TILE, LANES), F32)
            dcol = jnp.zeros((TOK_TILE, LANES), F32)
            gcol = jnp.zeros((TOK_TILE, LANES), F32)
            for q in range(per_slab):
                e = g0 + sl * per_slab + q
                mine = (lane >= q * CWIN) & (lane < (q + 1) * CWIN)
                rowid = jnp.where(mine, astart_ref[i * N_EXPERTS + e].astype(F32) + within, rowid)
                dcol = jnp.where(mine, dest[:, e:e + 1], dcol)
                gcol = jnp.where(mine, gates[:, e:e + 1], gcol)
            hi, lo = split(jnp.where(dcol == rowid, gcol, 0.0))
            his.append(hi)
            los.append(lo)
        ywin = jnp.concatenate([_unpack_rows(win_refs[g0 + q]) for q in range(group)], axis=0)
        both = jnp.concatenate([jnp.concatenate(his, axis=-1), jnp.concatenate(los, axis=-1)], axis=0)
        r = jnp.dot(both, ywin, preferred_element_type=F32)
        moe = moe + r[:TOK_TILE] + r[TOK_TILE:]
    mrows = pl.ds(pl.multiple_of(slot * TOK_TILE, TOK_TILE), TOK_TILE)
    moe_s[mrows, :] = moe

    @pl.when(over_ref[i] > 0)
    def _():
        def per_expert(e, carry):
            a = astart_ref[i * N_EXPERTS + e]
            dcol = jnp.sum(jnp.where(lane == e, dest, 0.0), axis=-1, keepdims=True)
            gcol = jnp.sum(jnp.where(lane == e, gates, 0.0), axis=-1, keepdims=True)

            def per_window(w, carry2):
                first = a + w * CWIN
                start = jnp.minimum(first, N_ROWS - CWIN)
                cp = pltpu.make_async_copy(ys_any.at[pl.ds(start, CWIN)], tmp_s, sem)
                cp.start()
                cp.wait()
                hit = (lane < CWIN) & (dcol == start.astype(F32) + lane_f) & (dcol >= first.astype(F32))
                hi, lo = split(jnp.where(hit, gcol, 0.0))
                rows = jnp.concatenate([_unpack_rows(tmp_s), jnp.zeros((LANES - CWIN, D_MODEL), BF16)], axis=0)
                moe_s[mrows, :] += (jnp.dot(hi, rows, preferred_element_type=F32)
                                    + jnp.dot(lo, rows, preferred_element_type=F32))
                return carry2

            return lax.fori_loop(1, nwin_ref[i * N_EXPERTS + e], per_window, carry)

        lax.fori_loop(0, N_EXPERTS, per_expert, 0)

    def tail(h1, moe_rows, ple):
        h2 = h1 + moe_rows
        hn = _rms(h2, gple_ref[...]).astype(BF16)
        gate = jax.nn.sigmoid(jnp.dot(hn, wpg_ref[...], preferred_element_type=F32))
        proj = jnp.dot(ple.astype(BF16), wpp_ref[...], preferred_element_type=F32)
        return _rms(h2 + gate * proj, gfin_ref[...])

    is_sample = i == N_TILES - 1

    @pl.when((slot == TAIL_TILES - 1) & jnp.logical_not(is_sample))
    def _():
        yp_ref[...] = tail(h1_ref[...], moe_s[...], plep_ref[...])

    @pl.when(is_sample)
    def _():
        ysm_ref[...] = tail(h1_ref[0:TOK_TILE, :], moe_s[0:TOK_TILE, :], ples_ref[...])


def _gather_call(astart, nwin, over, ys, destm, gm, h1, plep, ples, gple, wpg, wpp, gfin):
    rows = TAIL_TILES * TOK_TILE
    last = N_PROMPT // rows - 1

    def win_spec(e):
        return pl.BlockSpec((pl.Element(CWIN), pl.Element(PACK), pl.Element(LANES)),
                            lambda i, a, nw, ov, e=e: (a[i * N_EXPERTS + e], 0, 0))

    grid_spec = pltpu.PrefetchScalarGridSpec(
        num_scalar_prefetch=3,
        grid=(N_TILES,),
        in_specs=[win_spec(e) for e in range(N_EXPERTS)] + [
            pl.BlockSpec((TOK_TILE, LANES), lambda i, *_: (i, 0)),
            pl.BlockSpec((TOK_TILE, LANES), lambda i, *_: (i, 0)),
            pl.BlockSpec((rows, D_MODEL), lambda i, *_: (i // TAIL_TILES, 0)),
            pl.BlockSpec((rows, PLE_DIM), lambda i, *_: (jnp.minimum(i // TAIL_TILES, last), 0)),
            pl.BlockSpec((TOK_TILE, PLE_DIM), lambda i, *_: (0, 0)),
            pl.BlockSpec((1, D_MODEL), lambda i, *_: (0, 0)),
            pl.BlockSpec((D_MODEL, D_MODEL), lambda i, *_: (0, 0)),
            pl.BlockSpec((PLE_DIM, D_MODEL), lambda i, *_: (0, 0)),
            pl.BlockSpec((1, D_MODEL), lambda i, *_: (0, 0)),
            pl.BlockSpec(memory_space=pl.ANY),
        ],
        out_specs=[
            pl.BlockSpec((rows, D_MODEL), lambda i, *_: (jnp.minimum(i // TAIL_TILES, last), 0)),
            pl.BlockSpec((TOK_TILE, D_MODEL), lambda i, *_: (0, 0)),
        ],
        scratch_shapes=[pltpu.VMEM((rows, D_MODEL), F32), pltpu.VMEM((CWIN, PACK, LANES), jnp.int32),
                        pltpu.SemaphoreType.DMA],
    )
    return pl.pallas_call(
        _gather_kernel,
        grid_spec=grid_spec,
        out_shape=[jax.ShapeDtypeStruct((N_PROMPT, D_MODEL), F32),
                   jax.ShapeDtypeStruct((DEC_BATCH, D_MODEL), F32)],
        compiler_params=pltpu.CompilerParams(
            dimension_semantics=("arbitrary",), vmem_limit_bytes=VMEM_LIMIT),
        name="moe_combine_tail",
    )(astart, nwin, over, *([ys] * N_EXPERTS), destm, gm, h1, plep, ples, gple, wpg, wpp, gfin, ys)


def kernel(x_prompt, x_sample, cache_swa_k, cache_swa_v, p_prompt, p_sample, g_mix, w_in, ln_v_g, ln_v_b,
           w_sp, b_sp, sinks, g_out_a, g_out_b, w_o, g_moe, w_router, b_router, w_gu, b_gu, w_dn, b_dn,
           g_ple, w_ple_gate, w_ple_proj, g_final):
    l = 0
    row = lambda v: v.reshape(1, -1)
    win = w_in[l].astype(BF16)
    wo = w_o[l].astype(BF16)
    tril = jnp.tril(jnp.ones((CHUNK, CHUNK), dtype=bool))
    wsp = jnp.where(tril, w_sp[l], 0.0).astype(BF16)
    bsp = jnp.repeat(b_sp[l].T, HEAD_DIM, axis=1)
    w00 = row(jnp.repeat(w_sp[l][:, 0, 0], HEAD_DIM))
    b0 = row(jnp.repeat(b_sp[l][:, 0], HEAD_DIM))
    wr_hi = w_router[l].astype(BF16)
    wr_lo = (w_router[l] - wr_hi.astype(F32)).astype(BF16)
    wr = jnp.concatenate([wr_hi, wr_lo, jnp.zeros((D_MODEL, LANES - 2 * N_EXPERTS), BF16)], axis=1)
    br = row(jnp.concatenate([b_router[l], jnp.zeros((LANES - N_EXPERTS,), F32)]))
    common = (row(g_mix[l]), win, row(ln_v_g[l]), row(ln_v_b[l]))
    tail = (row(g_out_a[l]), row(g_out_b[l]), wo, row(g_moe[l]), wr, br)

    h1, xn, gm, sm, k_p, v_p = _prompt_call(
        x_prompt.reshape(N_PROMPT, D_MODEL), sinks[l], *common, wsp, bsp, *tail)
    ck = cache_swa_k[l].reshape(DEC_BATCH * CHUNK * 2, HEAD_DIM)
    cv = cache_swa_v[l].reshape(DEC_BATCH * CHUNK * 2, HEAD_DIM)
    h1, xn, gm, sm, k_s, v_s, va_s = _decode_call(
        x_sample.reshape(DEC_BATCH, D_MODEL), ck, cv, sinks[l], *common, w00, b0, *tail, h1, xn, gm, sm)

    destm, destt, xbe, nxblk, stab, ctab, astart, nwin = _plan_call(sm)
    flat = lambda tab, n: tab[:n, :N_EXPERTS].reshape(-1)
    astart1 = flat(astart, N_TILES)
    nwin2 = nwin[:N_TILES, :N_EXPERTS]
    over1 = (jnp.max(nwin2, axis=1) > 1).astype(jnp.int32)
    nwin1 = nwin2.reshape(-1)
    cmax1 = jnp.max(ctab[:N_PAD_TILES, :N_EXPERTS], axis=1)

    xs = _dispatch_call(flat(stab, N_PAD_TILES), flat(ctab, N_PAD_TILES), cmax1, xn, destt[:N_EXPERTS],
                        jnp.zeros((XS_ROWS, PACK, LANES), jnp.int32))
    xbe1, nxblk1 = xbe[:N_XBLOCKS, 0], nxblk[0, :1]
    blk = jnp.arange(N_XBLOCKS, dtype=jnp.int32)
    starts = (blk < nxblk1[0]) & ((blk == 0) | (xbe1 != jnp.roll(xbe1, 1)))
    pos = jnp.where(starts, blk, N_XBLOCKS)
    nxt_pos = jnp.roll(lax.cummin(pos, reverse=True), -1).at[N_XBLOCKS - 1].set(N_XBLOCKS)
    nxt1 = jnp.where(nxt_pos < N_XBLOCKS, xbe1[jnp.minimum(nxt_pos, N_XBLOCKS - 1)], -1).astype(jnp.int32)
    ys = _expert_call(xbe1, nxblk1, nxt1, xs, w_gu[l], b_gu[l].reshape(N_EXPERTS, 1, 2 * D_FF),
                      w_dn[l], b_dn[l].reshape(N_EXPERTS, 1, D_MODEL))
    y_p, y_s = _gather_call(
        astart1, nwin1, over1, ys, destm, gm, h1,
        p_prompt[l].reshape(N_PROMPT, PLE_DIM), p_sample[l].reshape(DEC_BATCH, PLE_DIM),
        row(g_ple[l]), w_ple_gate[l].astype(BF16), w_ple_proj[l].astype(BF16), row(g_final))

    kv5 = lambda a, n: a.reshape(1, n, CHUNK, 2, HEAD_DIM)
    return (y_p.reshape(BATCH, SEQ, D_MODEL), y_s.reshape(DEC_BATCH, 1, D_MODEL),
            kv5(k_p, BATCH), kv5(v_p, BATCH), kv5(k_s, DEC_BATCH), kv5(v_s, DEC_BATCH),
            va_s.reshape(1, DEC_BATCH, 1, A_WIDTH))
```

```python
import math

import jax
import jax.numpy as jnp
from jax import lax
from jax.experimental import pallas as pl
from jax.experimental.pallas import tpu as pltpu

F32 = jnp.float32
BF16 = jnp.bfloat16

D_MODEL = 1024
BATCH = 4
SEQ = 4096
DEC_BATCH = 128
HEAD_DIM = 64
A_WIDTH = 512
B_WIDTH = 512
B_HEADS = 8
KV_WIDTH = 128
IN_WIDTH = 2 * A_WIDTH + B_WIDTH + 2 * KV_WIDTH
CHUNK = 128
N_EXPERTS = 32
TOP_K = 4
D_FF = 1024
SWIGLU_ALPHA = 1.702
SWIGLU_LIMIT = 7.0
PLE_DIM = 256
EPS = 1e-5

LANES = 128
ROW_BLOCK = 128
EXP_BLOCK = 512
N_PROMPT = BATCH * SEQ
N_TOK = N_PROMPT + DEC_BATCH
TOK_TILE = 128
N_TILES = N_TOK // TOK_TILE
DISP_TILE = 256
TM = 512
N_PAD = ((N_TOK + TM - 1) // TM) * TM
N_PAD_TILES = N_PAD // TOK_TILE
DISP_CHUNK = 32
N_XBLOCKS = (N_TOK * TOP_K + N_EXPERTS * (DISP_CHUNK + EXP_BLOCK - 1) + EXP_BLOCK - 1) // EXP_BLOCK
N_ROWS = N_XBLOCKS * EXP_BLOCK
XS_ROWS = N_ROWS + N_EXPERTS * DISP_CHUNK
XBE_ROWS = ((N_XBLOCKS + 7) // 8) * 8
TAB_ROWS = ((N_PAD_TILES + 7) // 8) * 8
PACK = D_MODEL // 2 // LANES
WIN = 64
WIN_ALIGN = 16
CWIN = 32
TAIL_TILES = 4
SAMPLE_TILE = 32
DEC_TILE = 16
NEG = -1e30
VMEM_LIMIT = 56 * 1024 * 1024


def _rms(x, g):
    return x * lax.rsqrt(jnp.mean(x * x, axis=-1, keepdims=True) + EPS) * g


def _gelu(x):
    c = math.sqrt(2.0 / math.pi)
    return x * (0.5 * (1.0 + jnp.tanh(c * (x + 0.044715 * (x * x * x)))))


def _layernorm(x, g, b):
    mu = jnp.mean(x, axis=-1, keepdims=True)
    xc = x - mu
    return xc * lax.rsqrt(jnp.mean(xc * xc, axis=-1, keepdims=True) + EPS) * g + b


def _lane_iota(shape):
    return lax.broadcasted_iota(jnp.int32, shape, len(shape) - 1)


def _route(xn2, wr_ref, br_ref):
    m = xn2.shape[0]
    xh = xn2.astype(BF16)
    xl = (xn2 - xh.astype(F32)).astype(BF16)
    r = jnp.dot(jnp.concatenate([xh, xl], axis=0), wr_ref[...], preferred_element_type=F32)
    r = r[:m] + r[m:]
    lane = _lane_iota((m, LANES))
    lane_f = lane.astype(F32)
    logits = jnp.where(lane < N_EXPERTS, r + pltpu.roll(r, LANES - N_EXPERTS, 1) + br_ref[...], NEG)
    work = logits
    sel = jnp.zeros((m, LANES), F32)
    top = None
    z = None
    for _ in range(TOP_K):
        mx = jnp.max(work, axis=-1, keepdims=True)
        first = jnp.min(jnp.where(work == mx, lane_f, float(LANES)), axis=-1, keepdims=True)
        hit = lane_f == first
        sel = jnp.where(hit, 1.0, sel)
        work = jnp.where(hit, NEG, work)
        if top is None:
            top = mx
            z = jnp.ones_like(mx)
        else:
            z = z + jnp.exp(mx - top)
    gates = jnp.where(sel > 0.0, jnp.exp(logits - top) / z, 0.0)
    return gates, sel


def _prompt_kernel(sinks_ref, x_ref, gmix_ref, win_ref, lng_ref, lnb_ref, wsp_ref, bsp_ref,
                   goa_ref, gob_ref, wo_ref, gmoe_ref, wr_ref, br_ref,
                   h1_ref, xn_ref, gm_ref, sm_ref, k_ref, v_ref,
                   z_s, kv_s, cat_s):
    g = pl.program_id(0)
    j = g % (SEQ // TM)

    @pl.when(g >= N_PROMPT // TM)
    def _():
        h1_ref[...] = jnp.zeros_like(h1_ref)
        xn_ref[...] = jnp.zeros_like(xn_ref)
        gm_ref[...] = jnp.zeros_like(gm_ref)
        sm_ref[...] = jnp.zeros_like(sm_ref)

    @pl.when(g < N_PROMPT // TM)
    def _():
        _prompt_tile(j, sinks_ref, x_ref, gmix_ref, win_ref, lng_ref, lnb_ref, wsp_ref, bsp_ref,
                     goa_ref, gob_ref, wo_ref, gmoe_ref, wr_ref, br_ref,
                     h1_ref, xn_ref, gm_ref, sm_ref, k_ref, v_ref, z_s, kv_s, cat_s)


def _prompt_tile(j, sinks_ref, x_ref, gmix_ref, win_ref, lng_ref, lnb_ref, wsp_ref, bsp_ref,
                 goa_ref, gob_ref, wo_ref, gmoe_ref, wr_ref, br_ref,
                 h1_ref, xn_ref, gm_ref, sm_ref, k_ref, v_ref, z_s, kv_s, cat_s):
    @pl.when(j == 0)
    def _():
        kv_s[0:CHUNK, :] = jnp.zeros((CHUNK, 2 * KV_WIDTH), F32)

    xn = _rms(x_ref[...], gmix_ref[...]).astype(BF16)
    z_s[...] = jnp.dot(xn, win_ref[...], preferred_element_type=F32)
    kv_s[CHUNK:, :] = z_s[:, 2 * A_WIDTH + B_WIDTH:]

    lane = _lane_iota((CHUNK, LANES))
    lo = lane < HEAD_DIM
    lane2 = _lane_iota((2 * CHUNK, LANES))
    lo2 = lane2 < HEAD_DIM
    qi = lax.broadcasted_iota(jnp.int32, (CHUNK, CHUNK), 0)
    kc = lax.broadcasted_iota(jnp.int32, (CHUNK, CHUNK), 1)
    from_prev = kc > qi
    dist = jnp.where(from_prev, qi + CHUNK - kc, qi - kc).astype(F32)

    def chunk_body(c, carry):
        r0 = pl.multiple_of(c * CHUNK, CHUNK)
        rows = pl.ds(r0, CHUNK)
        u = _gelu(z_s[rows, 0:A_WIDTH])
        va = _layernorm(_gelu(z_s[rows, A_WIDTH:2 * A_WIDTH]), lng_ref[...], lnb_ref[...])
        vab = va.astype(BF16)
        slabs = []
        for p in range(A_WIDTH // LANES):
            slab = vab[:, p * LANES:(p + 1) * LANES]
            m0 = jnp.dot(wsp_ref[2 * p], slab, preferred_element_type=F32)
            m1 = jnp.dot(wsp_ref[2 * p + 1], slab, preferred_element_type=F32)
            slabs.append(jnp.where(lo, m0, m1))
        ya = u * (jnp.concatenate(slabs, axis=-1) + bsp_ref[...])
        ya_n = _rms(ya, goa_ref[...])
        k2 = kv_s[pl.ds(r0, 2 * CHUNK), 0:KV_WIDTH]
        v2 = kv_s[pl.ds(r0, 2 * CHUNK), KV_WIDTH:2 * KV_WIDTH]
        k2r = pltpu.roll(k2, HEAD_DIM, 1)
        v2r = pltpu.roll(v2, HEAD_DIM, 1)
        kd = (jnp.where(lo2, k2, k2r).astype(BF16), jnp.where(lo2, k2r, k2).astype(BF16))
        vd = (jnp.where(lo2, v2, v2r).astype(BF16), jnp.where(lo2, v2r, v2).astype(BF16))
        prev_ok = (j > 0) | (c > 0)
        masked = from_prev & jnp.logical_not(prev_ok)
        yb_slabs = []
        for kv in range(2):
            q0 = z_s[rows, 2 * A_WIDTH + (2 * kv) * LANES:2 * A_WIDTH + (2 * kv + 1) * LANES]
            q1 = z_s[rows, 2 * A_WIDTH + (2 * kv + 1) * LANES:2 * A_WIDTH + (2 * kv + 2) * LANES]
            lhs = jnp.concatenate([jnp.where(lo, q0, 0.0), jnp.where(lo, 0.0, q0),
                                   jnp.where(lo, q1, 0.0), jnp.where(lo, 0.0, q1)], axis=0).astype(BF16)
            s_all = lax.dot_general(lhs, kd[kv], (((1,), (1,)), ((), ())), preferred_element_type=F32)
            probs = []
            for i in range(4):
                h = 4 * kv + i
                slope = 2.0 ** (-(h + 1))
                sink = sinks_ref[h]
                sh = s_all[i * CHUNK:(i + 1) * CHUNK]
                s = jnp.where(from_prev, sh[:, :CHUNK], sh[:, CHUNK:]) * (HEAD_DIM ** -0.5) - slope * dist
                s = jnp.where(masked, NEG, s)
                mx = jnp.maximum(jnp.max(s, axis=-1, keepdims=True), sink)
                e = jnp.exp(s - mx)
                den = jnp.sum(e, axis=-1, keepdims=True) + jnp.exp(sink - mx)
                p = e * (1.0 / den)
                probs.append(jnp.concatenate([jnp.where(from_prev, p, 0.0), jnp.where(from_prev, 0.0, p)], axis=-1))
            pm = jnp.concatenate(probs, axis=0).astype(BF16)
            o = jnp.dot(pm, vd[kv], preferred_element_type=F32)
            yb_slabs.append(jnp.where(lo, o[0:CHUNK], o[CHUNK:2 * CHUNK]))
            yb_slabs.append(jnp.where(lo, o[2 * CHUNK:3 * CHUNK], o[3 * CHUNK:4 * CHUNK]))
        yb_n = _rms(jnp.concatenate(yb_slabs, axis=-1), gob_ref[...])
        cat_s[rows, 0:A_WIDTH] = ya_n.astype(BF16)
        cat_s[rows, A_WIDTH:] = yb_n.astype(BF16)
        return carry

    lax.fori_loop(0, TM // CHUNK, chunk_body, 0)

    kv_s[0:CHUNK, :] = kv_s[TM:TM + CHUNK, :]
    k_ref[...] = kv_s[TM:TM + CHUNK, 0:KV_WIDTH]
    v_ref[...] = kv_s[TM:TM + CHUNK, KV_WIDTH:]

    h1 = x_ref[...] + jnp.dot(cat_s[...], wo_ref[...], preferred_element_type=F32)
    h1_ref[...] = h1
    xn2 = _rms(h1, gmoe_ref[...])
    xn_ref[...] = xn2.astype(BF16)
    gates, sel = _route(xn2, wr_ref, br_ref)
    gm_ref[...] = gates
    sm_ref[...] = sel


def _full(shape):
    n = len(shape)
    return pl.BlockSpec(shape, lambda *_: (0,) * n)


def _prompt_call(x, sinks, gmix, win, lng, lnb, wsp, bsp, goa, gob, wo, gmoe, wr, br):
    real = N_PROMPT // TM
    row = lambda g: (g, 0)
    seq = lambda g: (jnp.minimum(g, real - 1) // (SEQ // TM), 0, 0)
    return pl.pallas_call(
        _prompt_kernel,
        grid=(N_PAD // TM,),
        in_specs=[
            pl.BlockSpec(memory_space=pltpu.SMEM),
            pl.BlockSpec((TM, D_MODEL), lambda g: (jnp.minimum(g, real - 1), 0)),
            _full((1, D_MODEL)), _full((D_MODEL, IN_WIDTH)), _full((1, A_WIDTH)), _full((1, A_WIDTH)),
            _full((8, CHUNK, CHUNK)), _full((CHUNK, A_WIDTH)), _full((1, A_WIDTH)), _full((1, B_WIDTH)),
            _full((D_MODEL, D_MODEL)), _full((1, D_MODEL)), _full((D_MODEL, LANES)), _full((1, LANES)),
        ],
        out_specs=[
            pl.BlockSpec((TM, D_MODEL), row),
            pl.BlockSpec((TM, D_MODEL), row),
            pl.BlockSpec((TM, LANES), row),
            pl.BlockSpec((TM, LANES), row),
            pl.BlockSpec((None, CHUNK, KV_WIDTH), seq),
            pl.BlockSpec((None, CHUNK, KV_WIDTH), seq),
        ],
        out_shape=[
            jax.ShapeDtypeStruct((N_PAD, D_MODEL), F32),
            jax.ShapeDtypeStruct((N_PAD, D_MODEL), BF16),
            jax.ShapeDtypeStruct((N_PAD, LANES), F32),
            jax.ShapeDtypeStruct((N_PAD, LANES), F32),
            jax.ShapeDtypeStruct((BATCH, CHUNK, KV_WIDTH), F32),
            jax.ShapeDtypeStruct((BATCH, CHUNK, KV_WIDTH), F32),
        ],
        scratch_shapes=[
            pltpu.VMEM((TM, IN_WIDTH), F32),
            pltpu.VMEM((TM + CHUNK, 2 * KV_WIDTH), F32),
            pltpu.VMEM((TM, D_MODEL), BF16),
        ],
        compiler_params=pltpu.CompilerParams(
            dimension_semantics=("arbitrary",), vmem_limit_bytes=VMEM_LIMIT),
        name="prompt_premoe",
    )(sinks, x, gmix, win, lng, lnb, wsp, bsp, goa, gob, wo, gmoe, wr, br)


def _sample_kernel(sinks_ref, x_ref, ck_ref, cv_ref, gmix_ref, win_ref, lng_ref, lnb_ref, w00_ref, b0_ref,
                   goa_ref, gob_ref, wo_ref, gmoe_ref, wr_ref, br_ref,
                   h1_in, xn_in, gm_in, sm_in,
                   h1_ref, xn_ref, gm_ref, sm_ref, nk_ref, nv_ref, va_ref):
    del h1_in, xn_in, gm_in, sm_in
    t = SAMPLE_TILE
    nkeys = t * CHUNK

    if True:
        x = x_ref[...]
        xn = _rms(x, gmix_ref[...]).astype(BF16)
        z = jnp.dot(xn, win_ref[...], preferred_element_type=F32)
        u = _gelu(z[:, 0:A_WIDTH])
        va = _layernorm(_gelu(z[:, A_WIDTH:2 * A_WIDTH]), lng_ref[...], lnb_ref[...])
        va_ref[...] = va
        ya_n = _rms(u * (w00_ref[...] * va + b0_ref[...]), goa_ref[...])

        knew = z[:, 2 * A_WIDTH + B_WIDTH:2 * A_WIDTH + B_WIDTH + KV_WIDTH]
        vnew = z[:, 2 * A_WIDTH + B_WIDTH + KV_WIDTH:]
        lane = _lane_iota((t, LANES))
        lo = lane < HEAD_DIM
        stacked = []
        for h in range(B_HEADS):
            q = z[:, 2 * A_WIDTH + (h // 2) * LANES:2 * A_WIDTH + (h // 2 + 1) * LANES]
            qh = jnp.where(lo if h % 2 == 0 else jnp.logical_not(lo), q, 0.0)
            if h % 2 != h // 4:
                qh = pltpu.roll(qh, HEAD_DIM, 1)
            stacked.append(qh)
        qs = jnp.concatenate(stacked, axis=0)
        rows = B_HEADS * t
        ridx = lax.broadcasted_iota(jnp.int32, (rows, 1), 0)
        slope = jnp.zeros((rows, 1), F32)
        sink = jnp.zeros((rows, 1), F32)
        for h in range(B_HEADS):
            in_h = (ridx >= h * t) & (ridx < (h + 1) * t)
            slope = jnp.where(in_h, 2.0 ** (-(h + 1)), slope)
            sink = jnp.where(in_h, sinks_ref[h], sink)
        s_c = lax.dot_general(qs.astype(BF16), ck_ref[...].astype(BF16), (((1,), (1,)), ((), ())),
                              preferred_element_type=F32)
        rsamp = lax.broadcasted_iota(jnp.int32, (rows, nkeys), 0) % t
        col = lax.broadcasted_iota(jnp.int32, (rows, nkeys), 1)
        pos = col % CHUNK
        own = ((col // CHUNK) == rsamp) & (pos >= 1)
        s_c = s_c * (HEAD_DIM ** -0.5) - slope * (CHUNK - pos).astype(F32)
        s_c = jnp.where(own, s_c, NEG)
        kn8 = jnp.concatenate([knew] * B_HEADS, axis=0)
        vn8 = jnp.concatenate([vnew] * B_HEADS, axis=0)
        s_n = jnp.sum(qs * kn8, axis=-1, keepdims=True) * (HEAD_DIM ** -0.5)
        mx = jnp.maximum(jnp.maximum(jnp.max(s_c, axis=-1, keepdims=True), s_n), sink)
        e_c = jnp.exp(s_c - mx)
        e_n = jnp.exp(s_n - mx)
        inv = 1.0 / (jnp.sum(e_c, axis=-1, keepdims=True) + e_n + jnp.exp(sink - mx))
        o = jnp.dot((e_c * inv).astype(BF16), cv_ref[...].astype(BF16), preferred_element_type=F32)
        o = o + (e_n * inv) * vn8
        yb_slabs = []
        for p in range(B_WIDTH // LANES):
            outs = []
            for half in range(2):
                h = 2 * p + half
                oh = o[h * t:(h + 1) * t]
                oh = jnp.where(lo if h // 4 == 0 else jnp.logical_not(lo), oh, 0.0)
                if half != h // 4:
                    oh = pltpu.roll(oh, HEAD_DIM, 1)
                outs.append(oh)
            yb_slabs.append(outs[0] + outs[1])
        yb_n = _rms(jnp.concatenate(yb_slabs, axis=-1), gob_ref[...])

        cat = jnp.concatenate([ya_n, yb_n], axis=-1).astype(BF16)
        h1 = x + jnp.dot(cat, wo_ref[...], preferred_element_type=F32)
        xn2 = _rms(h1, gmoe_ref[...])
        gates, sel = _route(xn2, wr_ref, br_ref)
        h1_ref[...] = h1
        xn_ref[...] = xn2.astype(BF16)
        gm_ref[...] = gates
        sm_ref[...] = sel

        nk_ref[...] = pltpu.roll(ck_ref[...], nkeys - 1, 0)
        nv_ref[...] = pltpu.roll(cv_ref[...], nkeys - 1, 0)
        for b in range(t):
            nk_ref[b * CHUNK + CHUNK - 1:b * CHUNK + CHUNK, :] = knew[b:b + 1, :]
            nv_ref[b * CHUNK + CHUNK - 1:b * CHUNK + CHUNK, :] = vnew[b:b + 1, :]


def _sample_call(x, ck, cv, sinks, gmix, win, lng, lnb, w00, b0, goa, gob, wo, gmoe, wr, br, h1, xn, gm, sm):
    t = SAMPLE_TILE
    steps = DEC_BATCH // t
    base = N_PROMPT // t
    inrow = lambda i: (i, 0)
    outrow = lambda i: (base + i, 0)
    anyspec = pl.BlockSpec(memory_space=pl.ANY)
    return pl.pallas_call(
        _sample_kernel,
        grid=(steps,),
        in_specs=[
            pl.BlockSpec(memory_space=pltpu.SMEM),
            pl.BlockSpec((t, D_MODEL), inrow),
            pl.BlockSpec((t * CHUNK, KV_WIDTH), inrow),
            pl.BlockSpec((t * CHUNK, KV_WIDTH), inrow),
            _full((1, D_MODEL)), _full((D_MODEL, IN_WIDTH)), _full((1, A_WIDTH)), _full((1, A_WIDTH)),
            _full((1, A_WIDTH)), _full((1, A_WIDTH)), _full((1, A_WIDTH)), _full((1, B_WIDTH)),
            _full((D_MODEL, D_MODEL)), _full((1, D_MODEL)), _full((D_MODEL, LANES)), _full((1, LANES)),
            anyspec, anyspec, anyspec, anyspec,
        ],
        out_specs=[
            pl.BlockSpec((t, D_MODEL), outrow),
            pl.BlockSpec((t, D_MODEL), outrow),
            pl.BlockSpec((t, LANES), outrow),
            pl.BlockSpec((t, LANES), outrow),
            pl.BlockSpec((t * CHUNK, KV_WIDTH), inrow),
            pl.BlockSpec((t * CHUNK, KV_WIDTH), inrow),
            pl.BlockSpec((t, A_WIDTH), inrow),
        ],
        out_shape=[
            jax.ShapeDtypeStruct((N_PAD, D_MODEL), F32),
            jax.ShapeDtypeStruct((N_PAD, D_MODEL), BF16),
            jax.ShapeDtypeStruct((N_PAD, LANES), F32),
            jax.ShapeDtypeStruct((N_PAD, LANES), F32),
            jax.ShapeDtypeStruct((DEC_BATCH * CHUNK, KV_WIDTH), F32),
            jax.ShapeDtypeStruct((DEC_BATCH * CHUNK, KV_WIDTH), F32),
            jax.ShapeDtypeStruct((DEC_BATCH, A_WIDTH), F32),
        ],
        input_output_aliases={16: 0, 17: 1, 18: 2, 19: 3},
        compiler_params=pltpu.CompilerParams(
            dimension_semantics=("arbitrary",), vmem_limit_bytes=VMEM_LIMIT),
        name="sample_premoe",
    )(sinks, x, ck, cv, gmix, win, lng, lnb, w00, b0, goa, gob, wo, gmoe, wr, br, h1, xn, gm, sm)


def _decode_kernel(sinks_ref, x_ref, ck_ref, cv_ref, gmix_ref, win_ref, lng_ref, lnb_ref, w00_ref, b0_ref,
                   goa_ref, gob_ref, wo_ref, gmoe_ref, wr_ref, br_ref,
                   h1_in, xn_in, gm_in, sm_in,
                   h1_ref, xn_ref, gm_ref, sm_ref, nk_ref, nv_ref, va_ref,
                   q_s, kn_s, vn_s, ya_s, yb_s):
    del h1_in, xn_in, gm_in, sm_in
    i = pl.program_id(0)
    t = DEC_TILE
    per_seq = CHUNK * 2
    ncols = t * per_seq

    @pl.when(i == 0)
    def _():
        xn = _rms(x_ref[...], gmix_ref[...]).astype(BF16)
        z = jnp.dot(xn, win_ref[...], preferred_element_type=F32)
        u = _gelu(z[:, 0:A_WIDTH])
        va = _layernorm(_gelu(z[:, A_WIDTH:2 * A_WIDTH]), lng_ref[...], lnb_ref[...])
        va_ref[...] = va
        ya_s[...] = _rms(u * (w00_ref[...] * va + b0_ref[...]), goa_ref[...])
        q_s[...] = z[:, 2 * A_WIDTH:2 * A_WIDTH + B_WIDTH]
        kn_s[...] = z[:, 2 * A_WIDTH + B_WIDTH:2 * A_WIDTH + B_WIDTH + KV_WIDTH]
        vn_s[...] = z[:, 2 * A_WIDTH + B_WIDTH + KV_WIDTH:]

    rows = pl.ds(pl.multiple_of(i * t, t), t)
    q = q_s[rows, :]
    kn = kn_s[rows, :]
    vn = vn_s[rows, :]
    qs = jnp.concatenate([q[:, h * HEAD_DIM:(h + 1) * HEAD_DIM] for h in range(B_HEADS)], axis=0)
    kn8 = jnp.concatenate([kn[:, (h // 4) * HEAD_DIM:(h // 4 + 1) * HEAD_DIM] for h in range(B_HEADS)], axis=0)
    vn8 = jnp.concatenate([vn[:, (h // 4) * HEAD_DIM:(h // 4 + 1) * HEAD_DIM] for h in range(B_HEADS)], axis=0)
    nrows = B_HEADS * t
    ridx = lax.broadcasted_iota(jnp.int32, (nrows, 1), 0)
    slope = jnp.zeros((nrows, 1), F32)
    sink = jnp.zeros((nrows, 1), F32)
    for h in range(B_HEADS):
        in_h = (ridx >= h * t) & (ridx < (h + 1) * t)
        slope = jnp.where(in_h, 2.0 ** (-(h + 1)), slope)
        sink = jnp.where(in_h, sinks_ref[h], sink)
    s_c = lax.dot_general(qs.astype(BF16), ck_ref[...].astype(BF16), (((1,), (1,)), ((), ())),
                          preferred_element_type=F32)
    rr = lax.broadcasted_iota(jnp.int32, (nrows, ncols), 0)
    col = lax.broadcasted_iota(jnp.int32, (nrows, ncols), 1)
    pos = (col // 2) % CHUNK
    own = ((col // per_seq) == (rr % t)) & ((col % 2) == (rr // (4 * t))) & (pos >= 1)
    s_c = s_c * (HEAD_DIM ** -0.5) - slope * (CHUNK - pos).astype(F32)
    s_c = jnp.where(own, s_c, NEG)
    s_n = jnp.sum(qs * kn8, axis=-1, keepdims=True) * (HEAD_DIM ** -0.5)
    mx = jnp.maximum(jnp.maximum(jnp.max(s_c, axis=-1, keepdims=True), s_n), sink)
    e_c = jnp.exp(s_c - mx)
    e_n = jnp.exp(s_n - mx)
    inv = 1.0 / (jnp.sum(e_c, axis=-1, keepdims=True) + e_n + jnp.exp(sink - mx))
    o = jnp.dot((e_c * inv).astype(BF16), cv_ref[...].astype(BF16), preferred_element_type=F32)
    o = o + (e_n * inv) * vn8
    yb_s[rows, :] = jnp.concatenate([o[h * t:(h + 1) * t] for h in range(B_HEADS)], axis=-1)

    nk_ref[...] = pltpu.roll(ck_ref[...], ncols - 2, 0)
    nv_ref[...] = pltpu.roll(cv_ref[...], ncols - 2, 0)
    for b in range(t):
        for kv in range(2):
            r = b * per_seq + per_seq - 2 + kv
            nk_ref[r:r + 1, :] = kn[b:b + 1, kv * HEAD_DIM:(kv + 1) * HEAD_DIM]
            nv_ref[r:r + 1, :] = vn[b:b + 1, kv * HEAD_DIM:(kv + 1) * HEAD_DIM]

    @pl.when(i == pl.num_programs(0) - 1)
    def _():
        yb_n = _rms(yb_s[...], gob_ref[...])
        cat = jnp.concatenate([ya_s[...], yb_n], axis=-1).astype(BF16)
        h1 = x_ref[...] + jnp.dot(cat, wo_ref[...], preferred_element_type=F32)
        xn2 = _rms(h1, gmoe_ref[...])
        gates, sel = _route(xn2, wr_ref, br_ref)
        h1_ref[...] = h1
        xn_ref[...] = xn2.astype(BF16)
        gm_ref[...] = gates
        sm_ref[...] = sel


def _decode_call(x, ck, cv, sinks, gmix, win, lng, lnb, w00, b0, goa, gob, wo, gmoe, wr, br, h1, xn, gm, sm):
    t = DEC_TILE
    per_seq = CHUNK * 2
    cache = pl.BlockSpec((t * per_seq, HEAD_DIM), lambda i: (i, 0))
    tok = lambda width: pl.BlockSpec((DEC_BATCH, width), lambda i: (N_PROMPT // DEC_BATCH, 0))
    anyspec = pl.BlockSpec(memory_space=pl.ANY)
    return pl.pallas_call(
        _decode_kernel,
        grid=(DEC_BATCH // t,),
        in_specs=[
            pl.BlockSpec(memory_space=pltpu.SMEM),
            _full((DEC_BATCH, D_MODEL)), cache, cache,
            _full((1, D_MODEL)), _full((D_MODEL, IN_WIDTH)), _full((1, A_WIDTH)), _full((1, A_WIDTH)),
            _full((1, A_WIDTH)), _full((1, A_WIDTH)), _full((1, A_WIDTH)), _full((1, B_WIDTH)),
            _full((D_MODEL, D_MODEL)), _full((1, D_MODEL)), _full((D_MODEL, LANES)), _full((1, LANES)),
            anyspec, anyspec, anyspec, anyspec,
        ],
        out_specs=[tok(D_MODEL), tok(D_MODEL), tok(LANES), tok(LANES), cache, cache, _full((DEC_BATCH, A_WIDTH))],
        out_shape=[
            jax.ShapeDtypeStruct((N_PAD, D_MODEL), F32),
            jax.ShapeDtypeStruct((N_PAD, D_MODEL), BF16),
            jax.ShapeDtypeStruct((N_PAD, LANES), F32),
            jax.ShapeDtypeStruct((N_PAD, LANES), F32),
            jax.ShapeDtypeStruct((DEC_BATCH * per_seq, HEAD_DIM), F32),
            jax.ShapeDtypeStruct((DEC_BATCH * per_seq, HEAD_DIM), F32),
            jax.ShapeDtypeStruct((DEC_BATCH, A_WIDTH), F32),
        ],
        scratch_shapes=[pltpu.VMEM((DEC_BATCH, B_WIDTH), F32), pltpu.VMEM((DEC_BATCH, KV_WIDTH), F32),
                        pltpu.VMEM((DEC_BATCH, KV_WIDTH), F32), pltpu.VMEM((DEC_BATCH, A_WIDTH), F32),
                        pltpu.VMEM((DEC_BATCH, B_WIDTH), F32)],
        input_output_aliases={16: 0, 17: 1, 18: 2, 19: 3},
        compiler_params=pltpu.CompilerParams(
            dimension_semantics=("arbitrary",), vmem_limit_bytes=VMEM_LIMIT),
        name="sample_premoe",
    )(sinks, x, ck, cv, gmix, win, lng, lnb, w00, b0, goa, gob, wo, gmoe, wr, br, h1, xn, gm, sm)


def _plan_kernel(sm_ref,
                 destm_ref, destt_ref, xbe_ref, nxblk_ref, stab_ref, ctab_ref, astart_ref, nwin_ref,
                 base_s, pstart_s):
    ph = pl.program_id(0)
    step = pl.program_id(1)
    lane = _lane_iota((1, LANES))

    @pl.when((ph == 0) & (step == 0))
    def _():
        base_s[...] = jnp.zeros_like(base_s)

    @pl.when(ph == 0)
    def _():
        base_s[...] += jnp.sum(sm_ref[...], axis=0, keepdims=True)

    @pl.when((ph == 1) & (step == 0))
    def _():
        counts = base_s[...]
        padded = jnp.floor((counts + (DISP_CHUNK + EXP_BLOCK - 1)) * (1.0 / EXP_BLOCK)) * EXP_BLOCK
        padded = jnp.where(counts > 0.0, padded, 0.0)
        pend = padded
        for s in (1, 2, 4, 8, 16):
            pend = pend + jnp.where(lane >= s, pltpu.roll(pend, s, 1), 0.0)
        spare = (N_ROWS + lane * DISP_CHUNK).astype(F32)
        pstart_s[...] = jnp.where(counts > 0.0, pend - padded, spare)
        base_s[...] = jnp.zeros_like(base_s)
        brow = lax.broadcasted_iota(jnp.int32, (XBE_ROWS, LANES), 0).astype(F32) * EXP_BLOCK
        done = jnp.where((lane < N_EXPERTS) & (pend <= brow), 1.0, 0.0)
        be = jnp.minimum(jnp.sum(done, axis=-1, keepdims=True), N_EXPERTS - 1.0)
        xbe_ref[...] = jnp.broadcast_to(be, (XBE_ROWS, LANES)).astype(jnp.int32)
        total = jnp.sum(jnp.where(lane == N_EXPERTS - 1, pend, 0.0), axis=-1, keepdims=True)
        nxblk_ref[...] = jnp.broadcast_to(total * (1.0 / EXP_BLOCK), (8, LANES)).astype(jnp.int32)
        stab_ref[...] = jnp.zeros_like(stab_ref)
        ctab_ref[...] = jnp.zeros_like(ctab_ref)
        astart_ref[...] = jnp.zeros_like(astart_ref)
        nwin_ref[...] = jnp.zeros_like(nwin_ref)

    @pl.when(ph == 1)
    def _():
        r = lax.broadcasted_iota(jnp.int32, (TOK_TILE, TOK_TILE), 0)
        c = lax.broadcasted_iota(jnp.int32, (TOK_TILE, TOK_TILE), 1)
        lower = jnp.where(c < r, 1.0, 0.0).astype(BF16)
        for q in range(TM // TOK_TILE):
            i = step * (TM // TOK_TILE) + q
            sel = sm_ref[q * TOK_TILE:(q + 1) * TOK_TILE, :]
            cnt = jnp.sum(sel, axis=0, keepdims=True)
            prefix = jnp.dot(lower, sel.astype(BF16), preferred_element_type=F32)
            start = pstart_s[...] + base_s[...]
            dest = jnp.where(sel > 0.0, prefix + start, -1.0)
            destm_ref[q * TOK_TILE:(q + 1) * TOK_TILE, :] = dest
            destt_ref[:, q * TOK_TILE:(q + 1) * TOK_TILE] = dest.T
            has = (cnt > 0.0) & (lane < N_EXPERTS)
            stab_ref[pl.ds(i, 1), :] = start.astype(jnp.int32)
            ctab_ref[pl.ds(i, 1), :] = jnp.where(has, cnt, 0.0).astype(jnp.int32)
            a = jnp.minimum(jnp.floor(start * (1.0 / WIN_ALIGN)) * WIN_ALIGN, float(N_ROWS - WIN))
            nw = jnp.where(has, jnp.floor((start + cnt - a + (WIN - 1)) * (1.0 / WIN)), 0.0)
            astart_ref[pl.ds(i, 1), :] = a.astype(jnp.int32)
            nwin_ref[pl.ds(i, 1), :] = nw.astype(jnp.int32)
            base_s[...] += cnt


def _plan_call(sm):
    tile = lambda ph, i: (i * ph, 0)
    tile_t = lambda ph, i: (0, i * ph)
    tab = jax.ShapeDtypeStruct((TAB_ROWS, LANES), jnp.int32)
    return pl.pallas_call(
        _plan_kernel,
        grid=(2, N_PAD // TM),
        in_specs=[pl.BlockSpec((TM, LANES), lambda ph, i: (i, 0))],
        out_specs=[
            pl.BlockSpec((TM, LANES), tile),
            pl.BlockSpec((LANES, TM), tile_t),
            _full((XBE_ROWS, LANES)), _full((8, LANES)),
            _full((TAB_ROWS, LANES)), _full((TAB_ROWS, LANES)), _full((TAB_ROWS, LANES)), _full((TAB_ROWS, LANES)),
        ],
        out_shape=[
            jax.ShapeDtypeStruct((N_PAD, LANES), F32),
            jax.ShapeDtypeStruct((LANES, N_PAD), F32),
            jax.ShapeDtypeStruct((XBE_ROWS, LANES), jnp.int32),
            jax.ShapeDtypeStruct((8, LANES), jnp.int32),
            tab, tab, tab, tab,
        ],
        scratch_shapes=[pltpu.VMEM((1, LANES), F32), pltpu.VMEM((1, LANES), F32)],
        compiler_params=pltpu.CompilerParams(
            dimension_semantics=("arbitrary", "arbitrary"), vmem_limit_bytes=VMEM_LIMIT),
        name="moe_plan",
    )(sm)


def _pack_rows(z):
    half = D_MODEL // 2
    lo = lax.bitcast_convert_type(z[:, :half], jnp.uint32) >> 16
    hi = lax.bitcast_convert_type(z[:, half:], jnp.uint32) & jnp.uint32(0xFFFF0000)
    return lax.bitcast_convert_type(hi | lo, jnp.int32)


def _unpack_rows(ref):
    rows = ref.shape[0]
    flat = ref.reshape(rows * PACK, LANES)
    lo, hi = [], []
    for s in range(PACK):
        w = lax.bitcast_convert_type(flat[pl.ds(s, rows, stride=PACK), :], jnp.uint32)
        lo.append(lax.bitcast_convert_type(w << 16, F32))
        hi.append(lax.bitcast_convert_type(w & jnp.uint32(0xFFFF0000), F32))
    return jnp.concatenate(lo + hi, axis=-1).astype(BF16)


def _dispatch_kernel(stab_ref, ctab_ref, cmax_ref, xn_ref, destt_ref, xs_in, xs_ref,
                     stage0, stage1, stage2, sems, sem2):
    del xs_in
    i = pl.program_id(0)
    last = pl.num_programs(0) - 1
    x = xn_ref[...]
    dt = destt_ref[...]
    rio = lax.broadcasted_iota(jnp.int32, (DISP_CHUNK, 1), 0).astype(F32)

    def chunk_rows(j, stage):
        parts = []
        for e in range(N_EXPERTS):
            first = (stab_ref[i * N_EXPERTS + e] + j * DISP_CHUNK).astype(F32)
            parts.append(jnp.where(dt[e:e + 1, :] == first + rio, 1.0, 0.0).astype(BF16))
        onehot = jnp.concatenate(parts, axis=0)
        words = _pack_rows(jnp.dot(onehot, x, preferred_element_type=F32))
        for s in range(PACK):
            stage[pl.ds(s, N_EXPERTS * DISP_CHUNK, stride=PACK), :] = words[:, s * LANES:(s + 1) * LANES]

    def copy(stage, step, e, j, sem):
        first = stab_ref[step * N_EXPERTS + e] + j * DISP_CHUNK
        rows = stage.reshape(N_EXPERTS * DISP_CHUNK, PACK, LANES)
        return pltpu.make_async_copy(rows.at[pl.ds(e * DISP_CHUNK, DISP_CHUNK)],
                                     xs_ref.at[pl.ds(first, DISP_CHUNK)], sem)

    def step_body(stage, prev_stage, par):
        chunk_rows(0, stage)

        @pl.when(i > 0)
        def _():
            for e in range(N_EXPERTS):
                copy(prev_stage, i - 1, e, 0, sems.at[1 - par, e]).wait()

        for e in range(N_EXPERTS):
            copy(stage, i, e, 0, sems.at[par, e]).start()

        @pl.when(i == last)
        def _():
            for e in range(N_EXPERTS):
                copy(stage, i, e, 0, sems.at[par, e]).wait()

    @pl.when(i % 2 == 0)
    def _():
        step_body(stage0, stage1, 0)

    @pl.when(i % 2 == 1)
    def _():
        step_body(stage1, stage0, 1)

    for j in range(1, TOK_TILE // DISP_CHUNK):

        @pl.when(cmax_ref[i] > j * DISP_CHUNK)
        def _(j=j):
            chunk_rows(j, stage2)
            for e in range(N_EXPERTS):

                @pl.when(ctab_ref[i * N_EXPERTS + e] > j * DISP_CHUNK)
                def _(e=e):
                    cp = copy(stage2, i, e, j, sem2)
                    cp.start()
                    cp.wait()


def _dispatch_call(stab, ctab, cmax, xn, destt, xs_zero):
    stage = pltpu.VMEM((N_EXPERTS * DISP_CHUNK * PACK, LANES), jnp.int32)
    grid_spec = pltpu.PrefetchScalarGridSpec(
        num_scalar_prefetch=3,
        grid=(N_PAD_TILES,),
        in_specs=[
            pl.BlockSpec((TOK_TILE, D_MODEL), lambda i, *_: (i, 0)),
            pl.BlockSpec((N_EXPERTS, TOK_TILE), lambda i, *_: (0, i)),
            pl.BlockSpec(memory_space=pl.ANY),
        ],
        out_specs=pl.BlockSpec(memory_space=pl.ANY),
        scratch_shapes=[stage, stage, stage, pltpu.SemaphoreType.DMA((2, N_EXPERTS)),
                        pltpu.SemaphoreType.DMA],
    )
    return pl.pallas_call(
        _dispatch_kernel,
        grid_spec=grid_spec,
        out_shape=jax.ShapeDtypeStruct((XS_ROWS, PACK, LANES), jnp.int32),
        input_output_aliases={5: 0},
        compiler_params=pltpu.CompilerParams(
            dimension_semantics=("arbitrary",), vmem_limit_bytes=VMEM_LIMIT),
        name="moe_dispatch",
    )(stab, ctab, cmax, xn, destt, xs_zero)


def _expert_kernel(blke_ref, nblk_ref, next_ref, xs_ref, wgu_hbm, bgu_ref, wdn_hbm, bdn_ref,
                   ys_ref, wgu_f, wdn_f, wgu_s, wdn_s, sems):
    b = pl.program_id(0)
    used = b < nblk_ref[0]
    prev = blke_ref[jnp.maximum(b - 1, 0)]
    fresh = used & ((b == 0) | (blke_ref[b] != prev))

    def fetch(e):
        return (pltpu.make_async_copy(wgu_hbm.at[e], wgu_f, sems.at[0]),
                pltpu.make_async_copy(wdn_hbm.at[e], wdn_f, sems.at[1]))

    @pl.when(b == 0)
    def _():
        for cp in fetch(blke_ref[0]):
            cp.start()

    @pl.when(fresh)
    def _():
        for cp in fetch(blke_ref[b]):
            cp.wait()
        wgu_s[...] = wgu_f[...].astype(BF16)
        wdn_s[...] = wdn_f[...].astype(BF16)

        @pl.when(next_ref[b] >= 0)
        def _():
            for cp in fetch(next_ref[b]):
                cp.start()

    @pl.when(used)
    def _():
        hid = jnp.dot(_unpack_rows(xs_ref), wgu_s[...], preferred_element_type=F32) + bgu_ref[...]
        gate = jnp.minimum(hid[:, :D_FF], SWIGLU_LIMIT)
        up = jnp.clip(hid[:, D_FF:], -SWIGLU_LIMIT, SWIGLU_LIMIT)
        act = (up + 1.0) * gate * jax.nn.sigmoid(SWIGLU_ALPHA * gate)
        y = jnp.dot(act.astype(BF16), wdn_s[...], preferred_element_type=F32) + bdn_ref[...]
        ys_ref[...] = y.astype(BF16)

    @pl.when(b >= nblk_ref[0])
    def _():
        ys_ref[...] = jnp.zeros_like(ys_ref)


def _expert_call(blke, nblk, nxt, xs, wgu, bgu, wdn, bdn):
    grid_spec = pltpu.PrefetchScalarGridSpec(
        num_scalar_prefetch=3,
        grid=(N_XBLOCKS,),
        in_specs=[
            pl.BlockSpec((EXP_BLOCK, PACK, LANES), lambda b, be, nb, nx: (b, 0, 0)),
            pl.BlockSpec(memory_space=pl.ANY),
            pl.BlockSpec((None, 1, 2 * D_FF), lambda b, be, nb, nx: (be[b], 0, 0)),
            pl.BlockSpec(memory_space=pl.ANY),
            pl.BlockSpec((None, 1, D_MODEL), lambda b, be, nb, nx: (be[b], 0, 0)),
        ],
        out_specs=pl.BlockSpec((EXP_BLOCK, D_MODEL), lambda b, be, nb, nx: (b, 0)),
        scratch_shapes=[pltpu.VMEM((D_MODEL, 2 * D_FF), F32), pltpu.VMEM((D_FF, D_MODEL), F32),
                        pltpu.VMEM((D_MODEL, 2 * D_FF), BF16), pltpu.VMEM((D_FF, D_MODEL), BF16),
                        pltpu.SemaphoreType.DMA((2,))],
    )
    return pl.pallas_call(
        _expert_kernel,
        grid_spec=grid_spec,
        out_shape=jax.ShapeDtypeStruct((N_ROWS, D_MODEL), BF16),
        compiler_params=pltpu.CompilerParams(
            dimension_semantics=("arbitrary",), vmem_limit_bytes=VMEM_LIMIT),
        name="moe_experts",
    )(blke, nblk, nxt, xs, wgu, bgu, wdn, bdn)


def _combine_kernel(*refs):
    astart_ref, nwin_ref, over_ref = refs[0:3]
    win_refs = refs[3:3 + N_EXPERTS]
    (destm_ref, gm_ref, h1_ref, plep_ref, ples_ref, gple_ref, wpg_ref, wpp_ref, gfin_ref, ys_any,
     yp_ref, ysm_ref, moe_s, tmp_s, sem) = refs[3 + N_EXPERTS:]
    i = pl.program_id(0)
    dest = destm_ref[...]
    gates = gm_ref[...]
    lane = _lane_iota((TOK_TILE, LANES))
    lane_f = lane.astype(F32)
    lo = lane < WIN
    moe = jnp.zeros((TOK_TILE, D_MODEL), F32)
    group = 4
    for g0 in range(0, N_EXPERTS, group):
        g_hi, g_lo = [], []
        for p in range(group // 2):
            e0 = g0 + 2 * p
            a0 = astart_ref[i * N_EXPERTS + e0].astype(F32)
            a1 = astart_ref[i * N_EXPERTS + e0 + 1].astype(F32)
            rowid = jnp.where(lo, a0 + lane_f, a1 + lane_f - WIN)
            dcol = jnp.where(lo, dest[:, e0:e0 + 1], dest[:, e0 + 1:e0 + 2])
            gcol = jnp.where(lo, gates[:, e0:e0 + 1], gates[:, e0 + 1:e0 + 2])
            gsel = jnp.where(dcol == rowid, gcol, 0.0)
            hi = gsel.astype(BF16)
            g_hi.append(hi)
            g_lo.append((gsel - hi.astype(F32)).astype(BF16))
        ywin = jnp.concatenate([win_refs[g0 + q][...] for q in range(group)], axis=0)
        both = jnp.concatenate([jnp.concatenate(g_hi, axis=-1), jnp.concatenate(g_lo, axis=-1)], axis=0)
        r = jnp.dot(both, ywin, preferred_element_type=F32)
        moe = moe + r[:TOK_TILE] + r[TOK_TILE:]
    mrows = pl.ds(pl.multiple_of((i % TAIL_TILES) * TOK_TILE, TOK_TILE), TOK_TILE)
    moe_s[mrows, :] = moe

    @pl.when(over_ref[i] > 0)
    def _():
        tmp_s[...] = jnp.zeros_like(tmp_s)

        def per_expert(e, carry):
            a = astart_ref[i * N_EXPERTS + e]
            dcol = jnp.sum(jnp.where(lane == e, dest, 0.0), axis=-1, keepdims=True)
            gcol = jnp.sum(jnp.where(lane == e, gates, 0.0), axis=-1, keepdims=True)

            def per_window(w, carry2):
                first = a + w * WIN
                start = pl.multiple_of(jnp.minimum(first, N_ROWS - WIN), WIN_ALIGN)
                cp = pltpu.make_async_copy(ys_any.at[pl.ds(start, WIN)], tmp_s.at[pl.ds(0, WIN)], sem)
                cp.start()
                cp.wait()
                hit = lo & (dcol == start.astype(F32) + lane_f) & (dcol >= first.astype(F32))
                gsel = jnp.where(hit, gcol, 0.0)
                hi = gsel.astype(BF16)
                rest = (gsel - hi.astype(F32)).astype(BF16)
                moe_s[mrows, :] += (jnp.dot(hi, tmp_s[...], preferred_element_type=F32)
                                    + jnp.dot(rest, tmp_s[...], preferred_element_type=F32))
                return carry2

            return lax.fori_loop(1, nwin_ref[i * N_EXPERTS + e], per_window, carry)

        lax.fori_loop(0, N_EXPERTS, per_expert, 0)

    def tail(h1, moe_rows, ple):
        h2 = h1 + moe_rows
        hn = _rms(h2, gple_ref[...]).astype(BF16)
        gate = jax.nn.sigmoid(jnp.dot(hn, wpg_ref[...], preferred_element_type=F32))
        proj = jnp.dot(ple.astype(BF16), wpp_ref[...], preferred_element_type=F32)
        return _rms(h2 + gate * proj, gfin_ref[...])

    is_sample = i == N_TILES - 1

    @pl.when((i % TAIL_TILES == TAIL_TILES - 1) & jnp.logical_not(is_sample))
    def _():
        yp_ref[...] = tail(h1_ref[...], moe_s[...], plep_ref[...])

    @pl.when(is_sample)
    def _():
        ysm_ref[...] = tail(h1_ref[0:TOK_TILE, :], moe_s[0:TOK_TILE, :], ples_ref[...])


def _combine_call(astart, nwin, over, ys, destm, gm, h1, plep, ples, gple, wpg, wpp, gfin):
    rows = TAIL_TILES * TOK_TILE
    last = N_PROMPT // rows - 1

    def win_spec(e):
        return pl.BlockSpec((pl.Element(WIN), pl.Element(D_MODEL)),
                            lambda i, a, nw, ov, e=e: (pl.multiple_of(a[i * N_EXPERTS + e], WIN_ALIGN), 0))

    grid_spec = pltpu.PrefetchScalarGridSpec(
        num_scalar_prefetch=3,
        grid=(N_TILES,),
        in_specs=[win_spec(e) for e in range(N_EXPERTS)] + [
            pl.BlockSpec((TOK_TILE, LANES), lambda i, *_: (i, 0)),
            pl.BlockSpec((TOK_TILE, LANES), lambda i, *_: (i, 0)),
            pl.BlockSpec((rows, D_MODEL), lambda i, *_: (i // TAIL_TILES, 0)),
            pl.BlockSpec((rows, PLE_DIM), lambda i, *_: (jnp.minimum(i // TAIL_TILES, last), 0)),
            pl.BlockSpec((TOK_TILE, PLE_DIM), lambda i, *_: (0, 0)),
            pl.BlockSpec((1, D_MODEL), lambda i, *_: (0, 0)),
            pl.BlockSpec((D_MODEL, D_MODEL), lambda i, *_: (0, 0)),
            pl.BlockSpec((PLE_DIM, D_MODEL), lambda i, *_: (0, 0)),
            pl.BlockSpec((1, D_MODEL), lambda i, *_: (0, 0)),
            pl.BlockSpec(memory_space=pl.ANY),
        ],
        out_specs=[
            pl.BlockSpec((rows, D_MODEL), lambda i, *_: (jnp.minimum(i // TAIL_TILES, last), 0)),
            pl.BlockSpec((TOK_TILE, D_MODEL), lambda i, *_: (0, 0)),
        ],
        scratch_shapes=[pltpu.VMEM((rows, D_MODEL), F32), pltpu.VMEM((2 * WIN, D_MODEL), BF16),
                        pltpu.SemaphoreType.DMA],
    )
    return pl.pallas_call(
        _combine_kernel,
        grid_spec=grid_spec,
        out_shape=[jax.ShapeDtypeStruct((N_PROMPT, D_MODEL), F32),
                   jax.ShapeDtypeStruct((DEC_BATCH, D_MODEL), F32)],
        compiler_params=pltpu.CompilerParams(
            dimension_semantics=("arbitrary",), vmem_limit_bytes=VMEM_LIMIT),
        name="moe_combine_tail",
    )(astart, nwin, over, *([ys] * N_EXPERTS), destm, gm, h1, plep, ples, gple, wpg, wpp, gfin, ys)


def _gather_kernel(*refs):
    astart_ref, nwin_ref, over_ref = refs[0:3]
    win_refs = refs[3:3 + N_EXPERTS]
    (destm_ref, gm_ref, h1_ref, plep_ref, ples_ref, gple_ref, wpg_ref, wpp_ref, gfin_ref, ys_any,
     yp_ref, ysm_ref, moe_s, tmp_s, sem) = refs[3 + N_EXPERTS:]
    i = pl.program_id(0)
    slot = i % TAIL_TILES
    dest = destm_ref[...]
    gates = gm_ref[...]
    lane = _lane_iota((TOK_TILE, LANES))
    lane_f = lane.astype(F32)
    per_slab = LANES // CWIN
    within = (lane % CWIN).astype(F32)
    group = 2 * per_slab

    def split(gsel):
        hi = gsel.astype(BF16)
        return hi, (gsel - hi.astype(F32)).astype(BF16)

    moe = jnp.zeros((TOK_TILE, D_MODEL), F32)
    for g0 in range(0, N_EXPERTS, group):
        his, los = [], []
        for sl in range(2):
            rowid = jnp.zeros((TOK_TILE, LANES), F32)
            dcol = jnp.zeros((TOK_TILE, LANES), F32)
            gcol = jnp.zeros((TOK_TILE, LANES), F32)
            for q in range(per_slab):
                e = g0 + sl * per_slab + q
                mine = (lane >= q * CWIN) & (lane < (q + 1) * CWIN)
                rowid = jnp.where(mine, astart_ref[i * N_EXPERTS + e].astype(F32) + within, rowid)
                dcol = jnp.where(mine, dest[:, e:e + 1], dcol)
                gcol = jnp.where(mine, gates[:, e:e + 1], gcol)
            hi, lo = split(jnp.where(dcol == rowid, gcol, 0.0))
            his.append(hi)
            los.append(lo)
        ywin = jnp.concatenate([_unpack_rows(win_refs[g0 + q]) for q in range(group)], axis=0)
        both = jnp.concatenate([jnp.concatenate(his, axis=-1), jnp.concatenate(los, axis=-1)], axis=0)
        r = jnp.dot(both, ywin, preferred_element_type=F32)
        moe = moe + r[:TOK_TILE] + r[TOK_TILE:]
    mrows = pl.ds(pl.multiple_of(slot * TOK_TILE, TOK_TILE), TOK_TILE)
    moe_s[mrows, :] = moe

    @pl.when(over_ref[i] > 0)
    def _():
        def per_expert(e, carry):
            a = astart_ref[i * N_EXPERTS + e]
            dcol = jnp.sum(jnp.where(lane == e, dest, 0.0), axis=-1, keepdims=True)
            gcol = jnp.sum(jnp.where(lane == e, gates, 0.0), axis=-1, keepdims=True)

            def per_window(w, carry2):
                first = a + w * CWIN
                start = jnp.minimum(first, N_ROWS - CWIN)
                cp = pltpu.make_async_copy(ys_any.at[pl.ds(start, CWIN)], tmp_s, sem)
                cp.start()
                cp.wait()
                hit = (lane < CWIN) & (dcol == start.astype(F32) + lane_f) & (dcol >= first.astype(F32))
                hi, lo = split(jnp.where(hit, gcol, 0.0))
                rows = jnp.concatenate([_unpack_rows(tmp_s), jnp.zeros((LANES - CWIN, D_MODEL), BF16)], axis=0)
                moe_s[mrows, :] += (jnp.dot(hi, rows, preferred_element_type=F32)
                                    + jnp.dot(lo, rows, preferred_element_type=F32))
                return carry2

            return lax.fori_loop(1, nwin_ref[i * N_EXPERTS + e], per_window, carry)

        lax.fori_loop(0, N_EXPERTS, per_expert, 0)

    def tail(h1, moe_rows, ple):
        h2 = h1 + moe_rows
        hn = _rms(h2, gple_ref[...]).astype(BF16)
        gate = jax.nn.sigmoid(jnp.dot(hn, wpg_ref[...], preferred_element_type=F32))
        proj = jnp.dot(ple.astype(BF16), wpp_ref[...], preferred_element_type=F32)
        return _rms(h2 + gate * proj, gfin_ref[...])

    is_sample = i == N_TILES - 1

    @pl.when((slot == TAIL_TILES - 1) & jnp.logical_not(is_sample))
    def _():
        yp_ref[...] = tail(h1_ref[...], moe_s[...], plep_ref[...])

    @pl.when(is_sample)
    def _():
        ysm_ref[...] = tail(h1_ref[0:TOK_TILE, :], moe_s[0:TOK_TILE, :], ples_ref[...])


def _gather_call(astart, nwin, over, ys, destm, gm, h1, plep, ples, gple, wpg, wpp, gfin):
    rows = TAIL_TILES * TOK_TILE
    last = N_PROMPT // rows - 1

    def win_spec(e):
        return pl.BlockSpec((pl.Element(CWIN), pl.Element(PACK), pl.Element(LANES)),
                            lambda i, a, nw, ov, e=e: (a[i * N_EXPERTS + e], 0, 0))

    grid_spec = pltpu.PrefetchScalarGridSpec(
        num_scalar_prefetch=3,
        grid=(N_TILES,),
        in_specs=[win_spec(e) for e in range(N_EXPERTS)] + [
            pl.BlockSpec((TOK_TILE, LANES), lambda i, *_: (i, 0)),
            pl.BlockSpec((TOK_TILE, LANES), lambda i, *_: (i, 0)),
            pl.BlockSpec((rows, D_MODEL), lambda i, *_: (i // TAIL_TILES, 0)),
            pl.BlockSpec((rows, PLE_DIM), lambda i, *_: (jnp.minimum(i // TAIL_TILES, last), 0)),
            pl.BlockSpec((TOK_TILE, PLE_DIM), lambda i, *_: (0, 0)),
            pl.BlockSpec((1, D_MODEL), lambda i, *_: (0, 0)),
            pl.BlockSpec((D_MODEL, D_MODEL), lambda i, *_: (0, 0)),
            pl.BlockSpec((PLE_DIM, D_MODEL), lambda i, *_: (0, 0)),
            pl.BlockSpec((1, D_MODEL), lambda i, *_: (0, 0)),
            pl.BlockSpec(memory_space=pl.ANY),
        ],
        out_specs=[
            pl.BlockSpec((rows, D_MODEL), lambda i, *_: (jnp.minimum(i // TAIL_TILES, last), 0)),
            pl.BlockSpec((TOK_TILE, D_MODEL), lambda i, *_: (0, 0)),
        ],
        scratch_shapes=[pltpu.VMEM((rows, D_MODEL), F32), pltpu.VMEM((CWIN, PACK, LANES), jnp.int32),
                        pltpu.SemaphoreType.DMA],
    )
    return pl.pallas_call(
        _gather_kernel,
        grid_spec=grid_spec,
        out_shape=[jax.ShapeDtypeStruct((N_PROMPT, D_MODEL), F32),
                   jax.ShapeDtypeStruct((DEC_BATCH, D_MODEL), F32)],
        compiler_params=pltpu.CompilerParams(
            dimension_semantics=("arbitrary",), vmem_limit_bytes=VMEM_LIMIT),
        name="moe_combine_tail",
    )(astart, nwin, over, *([ys] * N_EXPERTS), destm, gm, h1, plep, ples, gple, wpg, wpp, gfin, ys)


def kernel(x_prompt, x_sample, cache_swa_k, cache_swa_v, p_prompt, p_sample, g_mix, w_in, ln_v_g, ln_v_b,
           w_sp, b_sp, sinks, g_out_a, g_out_b, w_o, g_moe, w_router, b_router, w_gu, b_gu, w_dn, b_dn,
           g_ple, w_ple_gate, w_ple_proj, g_final):
    l = 0
    row = lambda v: v.reshape(1, -1)
    win = w_in[l].astype(BF16)
    wo = w_o[l].astype(BF16)
    tril = jnp.tril(jnp.ones((CHUNK, CHUNK), dtype=bool))
    wsp = jnp.where(tril, w_sp[l], 0.0).astype(BF16)
    bsp = jnp.repeat(b_sp[l].T, HEAD_DIM, axis=1)
    w00 = row(jnp.repeat(w_sp[l][:, 0, 0], HEAD_DIM))
    b0 = row(jnp.repeat(b_sp[l][:, 0], HEAD_DIM))
    wr_hi = w_router[l].astype(BF16)
    wr_lo = (w_router[l] - wr_hi.astype(F32)).astype(BF16)
    wr = jnp.concatenate([wr_hi, wr_lo, jnp.zeros((D_MODEL, LANES - 2 * N_EXPERTS), BF16)], axis=1)
    br = row(jnp.concatenate([b_router[l], jnp.zeros((LANES - N_EXPERTS,), F32)]))
    common = (row(g_mix[l]), win, row(ln_v_g[l]), row(ln_v_b[l]))
    tail = (row(g_out_a[l]), row(g_out_b[l]), wo, row(g_moe[l]), wr, br)

    h1, xn, gm, sm, k_p, v_p = _prompt_call(
        x_prompt.reshape(N_PROMPT, D_MODEL), sinks[l], *common, wsp, bsp, *tail)
    ck = cache_swa_k[l].reshape(DEC_BATCH * CHUNK * 2, HEAD_DIM)
    cv = cache_swa_v[l].reshape(DEC_BATCH * CHUNK * 2, HEAD_DIM)
    h1, xn, gm, sm, k_s, v_s, va_s = _decode_call(
        x_sample.reshape(DEC_BATCH, D_MODEL), ck, cv, sinks[l], *common, w00, b0, *tail, h1, xn, gm, sm)

    destm, destt, xbe, nxblk, stab, ctab, astart, nwin = _plan_call(sm)
    flat = lambda tab, n: tab[:n, :N_EXPERTS].reshape(-1)
    astart1 = flat(astart, N_TILES)
    nwin2 = nwin[:N_TILES, :N_EXPERTS]
    over1 = (jnp.max(nwin2, axis=1) > 1).astype(jnp.int32)
    nwin1 = nwin2.reshape(-1)
    cmax1 = jnp.max(ctab[:N_PAD_TILES, :N_EXPERTS], axis=1)

    xs = _dispatch_call(flat(stab, N_PAD_TILES), flat(ctab, N_PAD_TILES), cmax1, xn, destt[:N_EXPERTS],
                        jnp.zeros((XS_ROWS, PACK, LANES), jnp.int32))
    xbe1, nxblk1 = xbe[:N_XBLOCKS, 0], nxblk[0, :1]
    blk = jnp.arange(N_XBLOCKS, dtype=jnp.int32)
    starts = (blk < nxblk1[0]) & ((blk == 0) | (xbe1 != jnp.roll(xbe1, 1)))
    pos = jnp.where(starts, blk, N_XBLOCKS)
    nxt_pos = jnp.roll(lax.cummin(pos, reverse=True), -1).at[N_XBLOCKS - 1].set(N_XBLOCKS)
    nxt1 = jnp.where(nxt_pos < N_XBLOCKS, xbe1[jnp.minimum(nxt_pos, N_XBLOCKS - 1)], -1).astype(jnp.int32)
    ys = _expert_call(xbe1, nxblk1, nxt1, xs, w_gu[l], b_gu[l].reshape(N_EXPERTS, 1, 2 * D_FF),
                      w_dn[l], b_dn[l].reshape(N_EXPERTS, 1, D_MODEL))
    y_p, y_s = _combine_call(
        astart1, nwin1, over1, ys, destm, gm, h1,
        p_prompt[l].reshape(N_PROMPT, PLE_DIM), p_sample[l].reshape(DEC_BATCH, PLE_DIM),
        row(g_ple[l]), w_ple_gate[l].astype(BF16), w_ple_proj[l].astype(BF16), row(g_final))

    kv5 = lambda a, n: a.reshape(1, n, CHUNK, 2, HEAD_DIM)
    return (y_p.reshape(BATCH, SEQ, D_MODEL), y_s.reshape(DEC_BATCH, 1, D_MODEL),
            kv5(k_p, BATCH), kv5(v_p, BATCH), kv5(k_s, DEC_BATCH), kv5(v_s, DEC_BATCH),
            va_s.reshape(1, DEC_BATCH, 1, A_WIDTH))
```

```python
import math

import jax
import jax.numpy as jnp
from jax import lax
from jax.experimental import pallas as pl
from jax.experimental.pallas import tpu as pltpu

F32 = jnp.float32
BF16 = jnp.bfloat16

D_MODEL = 1024
BATCH = 4
SEQ = 4096
DEC_BATCH = 128
HEAD_DIM = 64
A_WIDTH = 512
B_WIDTH = 512
B_HEADS = 8
KV_WIDTH = 128
IN_WIDTH = 2 * A_WIDTH + B_WIDTH + 2 * KV_WIDTH
CHUNK = 128
N_EXPERTS = 32
TOP_K = 4
D_FF = 1024
SWIGLU_ALPHA = 1.702
SWIGLU_LIMIT = 7.0
PLE_DIM = 256
EPS = 1e-5

LANES = 128
ROW_GROUP = 128
EXP_BLOCK = 512
N_PROMPT = BATCH * SEQ
N_TOK = N_PROMPT + DEC_BATCH
TOK_TILE = 128
N_TILES = N_TOK // TOK_TILE
DISP_TILE = 256
TM = 512
N_PAD = ((N_TOK + TM - 1) // TM) * TM
N_PAD_TILES = N_PAD // TOK_TILE
DISP_CHUNK = 32
N_XBLOCKS = (N_TOK * TOP_K + N_EXPERTS * (DISP_CHUNK + EXP_BLOCK - 1) + EXP_BLOCK - 1) // EXP_BLOCK
N_ROWS = N_XBLOCKS * EXP_BLOCK
XS_ROWS = N_ROWS + N_EXPERTS * DISP_CHUNK
XBE_ROWS = ((N_XBLOCKS + 7) // 8) * 8
TAB_ROWS = ((N_PAD_TILES + 7) // 8) * 8
PACK = D_MODEL // 2 // LANES
WIN = 64
WIN_ALIGN = 16
CWIN = 32
TAIL_TILES = 4
SAMPLE_TILE = 32
DEC_TILE = 16
NEG = -1e30
VMEM_LIMIT = 56 * 1024 * 1024


def _rms(x, g):
    return x * lax.rsqrt(jnp.mean(x * x, axis=-1, keepdims=True) + EPS) * g


def _gelu(x):
    c = math.sqrt(2.0 / math.pi)
    return x * (0.5 * (1.0 + jnp.tanh(c * (x + 0.044715 * (x * x * x)))))


def _layernorm(x, g, b):
    mu = jnp.mean(x, axis=-1, keepdims=True)
    xc = x - mu
    return xc * lax.rsqrt(jnp.mean(xc * xc, axis=-1, keepdims=True) + EPS) * g + b


def _lane_iota(shape):
    return lax.broadcasted_iota(jnp.int32, shape, len(shape) - 1)


def _route(xn2, wr_ref, br_ref):
    m = xn2.shape[0]
    xh = xn2.astype(BF16)
    xl = (xn2 - xh.astype(F32)).astype(BF16)
    r = jnp.dot(jnp.concatenate([xh, xl], axis=0), wr_ref[...], preferred_element_type=F32)
    r = r[:m] + r[m:]
    lane = _lane_iota((m, LANES))
    lane_f = lane.astype(F32)
    logits = jnp.where(lane < N_EXPERTS, r + pltpu.roll(r, LANES - N_EXPERTS, 1) + br_ref[...], NEG)
    work = logits
    sel = jnp.zeros((m, LANES), F32)
    top = None
    z = None
    for _ in range(TOP_K):
        mx = jnp.max(work, axis=-1, keepdims=True)
        first = jnp.min(jnp.where(work == mx, lane_f, float(LANES)), axis=-1, keepdims=True)
        hit = lane_f == first
        sel = jnp.where(hit, 1.0, sel)
        work = jnp.where(hit, NEG, work)
        if top is None:
            top = mx
            z = jnp.ones_like(mx)
        else:
            z = z + jnp.exp(mx - top)
    gates = jnp.where(sel > 0.0, jnp.exp(logits - top) / z, 0.0)
    return gates, sel


def _prompt_kernel(sinks_ref, x_ref, gmix_ref, win_ref, lng_ref, lnb_ref, wsp_ref, bsp_ref,
                   goa_ref, gob_ref, wo_ref, gmoe_ref, wr_ref, br_ref,
                   h1_ref, xn_ref, gm_ref, sm_ref, k_ref, v_ref,
                   z_s, kv_s, cat_s):
    g = pl.program_id(0)
    j = g % (SEQ // TM)

    @pl.when(g >= N_PROMPT // TM)
    def _():
        h1_ref[...] = jnp.zeros_like(h1_ref)
        xn_ref[...] = jnp.zeros_like(xn_ref)
        gm_ref[...] = jnp.zeros_like(gm_ref)
        sm_ref[...] = jnp.zeros_like(sm_ref)

    @pl.when(g < N_PROMPT // TM)
    def _():
        _prompt_tile(j, sinks_ref, x_ref, gmix_ref, win_ref, lng_ref, lnb_ref, wsp_ref, bsp_ref,
                     goa_ref, gob_ref, wo_ref, gmoe_ref, wr_ref, br_ref,
                     h1_ref, xn_ref, gm_ref, sm_ref, k_ref, v_ref, z_s, kv_s, cat_s)


def _prompt_tile(j, sinks_ref, x_ref, gmix_ref, win_ref, lng_ref, lnb_ref, wsp_ref, bsp_ref,
                 goa_ref, gob_ref, wo_ref, gmoe_ref, wr_ref, br_ref,
                 h1_ref, xn_ref, gm_ref, sm_ref, k_ref, v_ref, z_s, kv_s, cat_s):
    @pl.when(j == 0)
    def _():
        kv_s[0:CHUNK, :] = jnp.zeros((CHUNK, 2 * KV_WIDTH), F32)

    xn = _rms(x_ref[...], gmix_ref[...]).astype(BF16)
    z_s[...] = jnp.dot(xn, win_ref[...], preferred_element_type=F32)
    kv_s[CHUNK:, :] = z_s[:, 2 * A_WIDTH + B_WIDTH:]

    lane = _lane_iota((CHUNK, LANES))
    lo = lane < HEAD_DIM
    lane2 = _lane_iota((2 * CHUNK, LANES))
    lo2 = lane2 < HEAD_DIM
    qi = lax.broadcasted_iota(jnp.int32, (CHUNK, CHUNK), 0)
    kc = lax.broadcasted_iota(jnp.int32, (CHUNK, CHUNK), 1)
    from_prev = kc > qi
    dist = jnp.where(from_prev, qi + CHUNK - kc, qi - kc).astype(F32)

    def chunk_body(c, carry):
        r0 = pl.multiple_of(c * CHUNK, CHUNK)
        rows = pl.ds(r0, CHUNK)
        u = _gelu(z_s[rows, 0:A_WIDTH])
        va = _layernorm(_gelu(z_s[rows, A_WIDTH:2 * A_WIDTH]), lng_ref[...], lnb_ref[...])
        vab = va.astype(BF16)
        slabs = []
        for p in range(A_WIDTH // LANES):
            slab = vab[:, p * LANES:(p + 1) * LANES]
            m0 = jnp.dot(wsp_ref[2 * p], slab, preferred_element_type=F32)
            m1 = jnp.dot(wsp_ref[2 * p + 1], slab, preferred_element_type=F32)
            slabs.append(jnp.where(lo, m0, m1))
        ya = u * (jnp.concatenate(slabs, axis=-1) + bsp_ref[...])
        ya_n = _rms(ya, goa_ref[...])
        k2 = kv_s[pl.ds(r0, 2 * CHUNK), 0:KV_WIDTH]
        v2 = kv_s[pl.ds(r0, 2 * CHUNK), KV_WIDTH:2 * KV_WIDTH]
        k2r = pltpu.roll(k2, HEAD_DIM, 1)
        v2r = pltpu.roll(v2, HEAD_DIM, 1)
        kd = (jnp.where(lo2, k2, k2r).astype(BF16), jnp.where(lo2, k2r, k2).astype(BF16))
        vd = (jnp.where(lo2, v2, v2r).astype(BF16), jnp.where(lo2, v2r, v2).astype(BF16))
        prev_ok = (j > 0) | (c > 0)
        masked = from_prev & jnp.logical_not(prev_ok)
        yb_slabs = []
        for kv in range(2):
            q0 = z_s[rows, 2 * A_WIDTH + (2 * kv) * LANES:2 * A_WIDTH + (2 * kv + 1) * LANES]
            q1 = z_s[rows, 2 * A_WIDTH + (2 * kv + 1) * LANES:2 * A_WIDTH + (2 * kv + 2) * LANES]
            lhs = jnp.concatenate([jnp.where(lo, q0, 0.0), jnp.where(lo, 0.0, q0),
                                   jnp.where(lo, q1, 0.0), jnp.where(lo, 0.0, q1)], axis=0).astype(BF16)
            s_all = lax.dot_general(lhs, kd[kv], (((1,), (1,)), ((), ())), preferred_element_type=F32)
            probs = []
            for i in range(4):
                h = 4 * kv + i
                slope = 2.0 ** (-(h + 1))
                sink = sinks_ref[h]
                sh = s_all[i * CHUNK:(i + 1) * CHUNK]
                s = jnp.where(from_prev, sh[:, :CHUNK], sh[:, CHUNK:]) * (HEAD_DIM ** -0.5) - slope * dist
                s = jnp.where(masked, NEG, s)
                mx = jnp.maximum(jnp.max(s, axis=-1, keepdims=True), sink)
                e = jnp.exp(s - mx)
                den = jnp.sum(e, axis=-1, keepdims=True) + jnp.exp(sink - mx)
                p = e * (1.0 / den)
                probs.append(jnp.concatenate([jnp.where(from_prev, p, 0.0), jnp.where(from_prev, 0.0, p)], axis=-1))
            pm = jnp.concatenate(probs, axis=0).astype(BF16)
            o = jnp.dot(pm, vd[kv], preferred_element_type=F32)
            yb_slabs.append(jnp.where(lo, o[0:CHUNK], o[CHUNK:2 * CHUNK]))
            yb_slabs.append(jnp.where(lo, o[2 * CHUNK:3 * CHUNK], o[3 * CHUNK:4 * CHUNK]))
        yb_n = _rms(jnp.concatenate(yb_slabs, axis=-1), gob_ref[...])
        cat_s[rows, 0:A_WIDTH] = ya_n.astype(BF16)
        cat_s[rows, A_WIDTH:] = yb_n.astype(BF16)
        return carry

    lax.fori_loop(0, TM // CHUNK, chunk_body, 0)

    kv_s[0:CHUNK, :] = kv_s[TM:TM + CHUNK, :]
    k_ref[...] = kv_s[TM:TM + CHUNK, 0:KV_WIDTH]
    v_ref[...] = kv_s[TM:TM + CHUNK, KV_WIDTH:]

    h1 = x_ref[...] + jnp.dot(cat_s[...], wo_ref[...], preferred_element_type=F32)
    h1_ref[...] = h1
    xn2 = _rms(h1, gmoe_ref[...])
    xn_ref[...] = xn2.astype(BF16)
    gates, sel = _route(xn2, wr_ref, br_ref)
    gm_ref[...] = gates
    sm_ref[...] = sel


def _full(shape):
    n = len(shape)
    return pl.BlockSpec(shape, lambda *_: (0,) * n)


def _prompt_call(x, sinks, gmix, win, lng, lnb, wsp, bsp, goa, gob, wo, gmoe, wr, br):
    real = N_PROMPT // TM
    row = lambda g: (g, 0)
    seq = lambda g: (jnp.minimum(g, real - 1) // (SEQ // TM), 0, 0)
    return pl.pallas_call(
        _prompt_kernel,
        grid=(N_PAD // TM,),
        in_specs=[
            pl.BlockSpec(memory_space=pltpu.SMEM),
            pl.BlockSpec((TM, D_MODEL), lambda g: (jnp.minimum(g, real - 1), 0)),
            _full((1, D_MODEL)), _full((D_MODEL, IN_WIDTH)), _full((1, A_WIDTH)), _full((1, A_WIDTH)),
            _full((8, CHUNK, CHUNK)), _full((CHUNK, A_WIDTH)), _full((1, A_WIDTH)), _full((1, B_WIDTH)),
            _full((D_MODEL, D_MODEL)), _full((1, D_MODEL)), _full((D_MODEL, LANES)), _full((1, LANES)),
        ],
        out_specs=[
            pl.BlockSpec((TM, D_MODEL), row),
            pl.BlockSpec((TM, D_MODEL), row),
            pl.BlockSpec((TM, LANES), row),
            pl.BlockSpec((TM, LANES), row),
            pl.BlockSpec((None, CHUNK, KV_WIDTH), seq),
            pl.BlockSpec((None, CHUNK, KV_WIDTH), seq),
        ],
        out_shape=[
            jax.ShapeDtypeStruct((N_PAD, D_MODEL), F32),
            jax.ShapeDtypeStruct((N_PAD, D_MODEL), BF16),
            jax.ShapeDtypeStruct((N_PAD, LANES), F32),
            jax.ShapeDtypeStruct((N_PAD, LANES), F32),
            jax.ShapeDtypeStruct((BATCH, CHUNK, KV_WIDTH), F32),
            jax.ShapeDtypeStruct((BATCH, CHUNK, KV_WIDTH), F32),
        ],
        scratch_shapes=[
            pltpu.VMEM((TM, IN_WIDTH), F32),
            pltpu.VMEM((TM + CHUNK, 2 * KV_WIDTH), F32),
            pltpu.VMEM((TM, D_MODEL), BF16),
        ],
        compiler_params=pltpu.CompilerParams(
            dimension_semantics=("arbitrary",), vmem_limit_bytes=VMEM_LIMIT),
        name="prompt_premoe",
    )(sinks, x, gmix, win, lng, lnb, wsp, bsp, goa, gob, wo, gmoe, wr, br)


def _sample_kernel(sinks_ref, x_ref, ck_ref, cv_ref, gmix_ref, win_ref, lng_ref, lnb_ref, w00_ref, b0_ref,
                   goa_ref, gob_ref, wo_ref, gmoe_ref, wr_ref, br_ref,
                   h1_in, xn_in, gm_in, sm_in,
                   h1_ref, xn_ref, gm_ref, sm_ref, nk_ref, nv_ref, va_ref):
    del h1_in, xn_in, gm_in, sm_in
    t = SAMPLE_TILE
    nkeys = t * CHUNK

    if True:
        x = x_ref[...]
        xn = _rms(x, gmix_ref[...]).astype(BF16)
        z = jnp.dot(xn, win_ref[...], preferred_element_type=F32)
        u = _gelu(z[:, 0:A_WIDTH])
        va = _layernorm(_gelu(z[:, A_WIDTH:2 * A_WIDTH]), lng_ref[...], lnb_ref[...])
        va_ref[...] = va
        ya_n = _rms(u * (w00_ref[...] * va + b0_ref[...]), goa_ref[...])

        knew = z[:, 2 * A_WIDTH + B_WIDTH:2 * A_WIDTH + B_WIDTH + KV_WIDTH]
        vnew = z[:, 2 * A_WIDTH + B_WIDTH + KV_WIDTH:]
        lane = _lane_iota((t, LANES))
        lo = lane < HEAD_DIM
        stacked = []
        for h in range(B_HEADS):
            q = z[:, 2 * A_WIDTH + (h // 2) * LANES:2 * A_WIDTH + (h // 2 + 1) * LANES]
            qh = jnp.where(lo if h % 2 == 0 else jnp.logical_not(lo), q, 0.0)
            if h % 2 != h // 4:
                qh = pltpu.roll(qh, HEAD_DIM, 1)
            stacked.append(qh)
        qs = jnp.concatenate(stacked, axis=0)
        rows = B_HEADS * t
        ridx = lax.broadcasted_iota(jnp.int32, (rows, 1), 0)
        slope = jnp.zeros((rows, 1), F32)
        sink = jnp.zeros((rows, 1), F32)
        for h in range(B_HEADS):
            in_h = (ridx >= h * t) & (ridx < (h + 1) * t)
            slope = jnp.where(in_h, 2.0 ** (-(h + 1)), slope)
            sink = jnp.where(in_h, sinks_ref[h], sink)
        s_c = lax.dot_general(qs.astype(BF16), ck_ref[...].astype(BF16), (((1,), (1,)), ((), ())),
                              preferred_element_type=F32)
        rsamp = lax.broadcasted_iota(jnp.int32, (rows, nkeys), 0) % t
        col = lax.broadcasted_iota(jnp.int32, (rows, nkeys), 1)
        pos = col % CHUNK
        own = ((col // CHUNK) == rsamp) & (pos >= 1)
        s_c = s_c * (HEAD_DIM ** -0.5) - slope * (CHUNK - pos).astype(F32)
        s_c = jnp.where(own, s_c, NEG)
        kn8 = jnp.concatenate([knew] * B_HEADS, axis=0)
        vn8 = jnp.concatenate([vnew] * B_HEADS, axis=0)
        s_n = jnp.sum(qs * kn8, axis=-1, keepdims=True) * (HEAD_DIM ** -0.5)
        mx = jnp.maximum(jnp.maximum(jnp.max(s_c, axis=-1, keepdims=True), s_n), sink)
        e_c = jnp.exp(s_c - mx)
        e_n = jnp.exp(s_n - mx)
        inv = 1.0 / (jnp.sum(e_c, axis=-1, keepdims=True) + e_n + jnp.exp(sink - mx))
        o = jnp.dot((e_c * inv).astype(BF16), cv_ref[...].astype(BF16), preferred_element_type=F32)
        o = o + (e_n * inv) * vn8
        yb_slabs = []
        for p in range(B_WIDTH // LANES):
            outs = []
            for half in range(2):
                h = 2 * p + half
                oh = o[h * t:(h + 1) * t]
                oh = jnp.where(lo if h // 4 == 0 else jnp.logical_not(lo), oh, 0.0)
                if half != h // 4:
                    oh = pltpu.roll(oh, HEAD_DIM, 1)
                outs.append(oh)
            yb_slabs.append(outs[0] + outs[1])
        yb_n = _rms(jnp.concatenate(yb_slabs, axis=-1), gob_ref[...])

        cat = jnp.concatenate([ya_n, yb_n], axis=-1).astype(BF16)
        h1 = x + jnp.dot(cat, wo_ref[...], preferred_element_type=F32)
        xn2 = _rms(h1, gmoe_ref[...])
        gates, sel = _route(xn2, wr_ref, br_ref)
        h1_ref[...] = h1
        xn_ref[...] = xn2.astype(BF16)
        gm_ref[...] = gates
        sm_ref[...] = sel

        nk_ref[...] = pltpu.roll(ck_ref[...], nkeys - 1, 0)
        nv_ref[...] = pltpu.roll(cv_ref[...], nkeys - 1, 0)
        for b in range(t):
            nk_ref[b * CHUNK + CHUNK - 1:b * CHUNK + CHUNK, :] = knew[b:b + 1, :]
            nv_ref[b * CHUNK + CHUNK - 1:b * CHUNK + CHUNK, :] = vnew[b:b + 1, :]


def _sample_call(x, ck, cv, sinks, gmix, win, lng, lnb, w00, b0, goa, gob, wo, gmoe, wr, br, h1, xn, gm, sm):
    t = SAMPLE_TILE
    steps = DEC_BATCH // t
    base = N_PROMPT // t
    inrow = lambda i: (i, 0)
    outrow = lambda i: (base + i, 0)
    anyspec = pl.BlockSpec(memory_space=pl.ANY)
    return pl.pallas_call(
        _sample_kernel,
        grid=(steps,),
        in_specs=[
            pl.BlockSpec(memory_space=pltpu.SMEM),
            pl.BlockSpec((t, D_MODEL), inrow),
            pl.BlockSpec((t * CHUNK, KV_WIDTH), inrow),
            pl.BlockSpec((t * CHUNK, KV_WIDTH), inrow),
            _full((1, D_MODEL)), _full((D_MODEL, IN_WIDTH)), _full((1, A_WIDTH)), _full((1, A_WIDTH)),
            _full((1, A_WIDTH)), _full((1, A_WIDTH)), _full((1, A_WIDTH)), _full((1, B_WIDTH)),
            _full((D_MODEL, D_MODEL)), _full((1, D_MODEL)), _full((D_MODEL, LANES)), _full((1, LANES)),
            anyspec, anyspec, anyspec, anyspec,
        ],
        out_specs=[
            pl.BlockSpec((t, D_MODEL), outrow),
            pl.BlockSpec((t, D_MODEL), outrow),
            pl.BlockSpec((t, LANES), outrow),
            pl.BlockSpec((t, LANES), outrow),
            pl.BlockSpec((t * CHUNK, KV_WIDTH), inrow),
            pl.BlockSpec((t * CHUNK, KV_WIDTH), inrow),
            pl.BlockSpec((t, A_WIDTH), inrow),
        ],
        out_shape=[
            jax.ShapeDtypeStruct((N_PAD, D_MODEL), F32),
            jax.ShapeDtypeStruct((N_PAD, D_MODEL), BF16),
            jax.ShapeDtypeStruct((N_PAD, LANES), F32),
            jax.ShapeDtypeStruct((N_PAD, LANES), F32),
            jax.ShapeDtypeStruct((DEC_BATCH * CHUNK, KV_WIDTH), F32),
            jax.ShapeDtypeStruct((DEC_BATCH * CHUNK, KV_WIDTH), F32),
            jax.ShapeDtypeStruct((DEC_BATCH, A_WIDTH), F32),
        ],
        input_output_aliases={16: 0, 17: 1, 18: 2, 19: 3},
        compiler_params=pltpu.CompilerParams(
            dimension_semantics=("arbitrary",), vmem_limit_bytes=VMEM_LIMIT),
        name="sample_premoe",
    )(sinks, x, ck, cv, gmix, win, lng, lnb, w00, b0, goa, gob, wo, gmoe, wr, br, h1, xn, gm, sm)


def _decode_kernel(sinks_ref, x_ref, ck_ref, cv_ref, gmix_ref, win_ref, lng_ref, lnb_ref, w00_ref, b0_ref,
                   goa_ref, gob_ref, wo_ref, gmoe_ref, wr_ref, br_ref,
                   h1_in, xn_in, gm_in, sm_in,
                   h1_ref, xn_ref, gm_ref, sm_ref, nk_ref, nv_ref, va_ref,
                   q_s, kn_s, vn_s, ya_s, yb_s):
    del h1_in, xn_in, gm_in, sm_in
    i = pl.program_id(0)
    t = DEC_TILE
    per_seq = CHUNK * 2
    ncols = t * per_seq

    @pl.when(i == 0)
    def _():
        xn = _rms(x_ref[...], gmix_ref[...]).astype(BF16)
        z = jnp.dot(xn, win_ref[...], preferred_element_type=F32)
        u = _gelu(z[:, 0:A_WIDTH])
        va = _layernorm(_gelu(z[:, A_WIDTH:2 * A_WIDTH]), lng_ref[...], lnb_ref[...])
        va_ref[...] = va
        ya_s[...] = _rms(u * (w00_ref[...] * va + b0_ref[...]), goa_ref[...])
        q_s[...] = z[:, 2 * A_WIDTH:2 * A_WIDTH + B_WIDTH]
        kn_s[...] = z[:, 2 * A_WIDTH + B_WIDTH:2 * A_WIDTH + B_WIDTH + KV_WIDTH]
        vn_s[...] = z[:, 2 * A_WIDTH + B_WIDTH + KV_WIDTH:]

    rows = pl.ds(pl.multiple_of(i * t, t), t)
    q = q_s[rows, :]
    kn = kn_s[rows, :]
    vn = vn_s[rows, :]
    qs = jnp.concatenate([q[:, h * HEAD_DIM:(h + 1) * HEAD_DIM] for h in range(B_HEADS)], axis=0)
    kn8 = jnp.concatenate([kn[:, (h // 4) * HEAD_DIM:(h // 4 + 1) * HEAD_DIM] for h in range(B_HEADS)], axis=0)
    vn8 = jnp.concatenate([vn[:, (h // 4) * HEAD_DIM:(h // 4 + 1) * HEAD_DIM] for h in range(B_HEADS)], axis=0)
    nrows = B_HEADS * t
    ridx = lax.broadcasted_iota(jnp.int32, (nrows, 1), 0)
    slope = jnp.zeros((nrows, 1), F32)
    sink = jnp.zeros((nrows, 1), F32)
    for h in range(B_HEADS):
        in_h = (ridx >= h * t) & (ridx < (h + 1) * t)
        slope = jnp.where(in_h, 2.0 ** (-(h + 1)), slope)
        sink = jnp.where(in_h, sinks_ref[h], sink)
    s_c = lax.dot_general(qs.astype(BF16), ck_ref[...].astype(BF16), (((1,), (1,)), ((), ())),
                          preferred_element_type=F32)
    rr = lax.broadcasted_iota(jnp.int32, (nrows, ncols), 0)
    col = lax.broadcasted_iota(jnp.int32, (nrows, ncols), 1)
    pos = (col // 2) % CHUNK
    own = ((col // per_seq) == (rr % t)) & ((col % 2) == (rr // (4 * t))) & (pos >= 1)
    s_c = s_c * (HEAD_DIM ** -0.5) - slope * (CHUNK - pos).astype(F32)
    s_c = jnp.where(own, s_c, NEG)
    s_n = jnp.sum(qs * kn8, axis=-1, keepdims=True) * (HEAD_DIM ** -0.5)
    mx = jnp.maximum(jnp.maximum(jnp.max(s_c, axis=-1, keepdims=True), s_n), sink)
    e_c = jnp.exp(s_c - mx)
    e_n = jnp.exp(s_n - mx)
    inv = 1.0 / (jnp.sum(e_c, axis=-1, keepdims=True) + e_n + jnp.exp(sink - mx))
    o = jnp.dot((e_c * inv).astype(BF16), cv_ref[...].astype(BF16), preferred_element_type=F32)
    o = o + (e_n * inv) * vn8
    yb_s[rows, :] = jnp.concatenate([o[h * t:(h + 1) * t] for h in range(B_HEADS)], axis=-1)

    nk_ref[...] = pltpu.roll(ck_ref[...], ncols - 2, 0)
    nv_ref[...] = pltpu.roll(cv_ref[...], ncols - 2, 0)
    for b in range(t):
        for kv in range(2):
            r = b * per_seq + per_seq - 2 + kv
            nk_ref[r:r + 1, :] = kn[b:b + 1, kv * HEAD_DIM:(kv + 1) * HEAD_DIM]
            nv_ref[r:r + 1, :] = vn[b:b + 1, kv * HEAD_DIM:(kv + 1) * HEAD_DIM]

    @pl.when(i == pl.num_programs(0) - 1)
    def _():
        yb_n = _rms(yb_s[...], gob_ref[...])
        cat = jnp.concatenate([ya_s[...], yb_n], axis=-1).astype(BF16)
        h1 = x_ref[...] + jnp.dot(cat, wo_ref[...], preferred_element_type=F32)
        xn2 = _rms(h1, gmoe_ref[...])
        gates, sel = _route(xn2, wr_ref, br_ref)
        h1_ref[...] = h1
        xn_ref[...] = xn2.astype(BF16)
        gm_ref[...] = gates
        sm_ref[...] = sel


def _decode_call(x, ck, cv, sinks, gmix, win, lng, lnb, w00, b0, goa, gob, wo, gmoe, wr, br, h1, xn, gm, sm):
    t = DEC_TILE
    per_seq = CHUNK * 2
    cache = pl.BlockSpec((t * per_seq, HEAD_DIM), lambda i: (i, 0))
    tok = lambda width: pl.BlockSpec((DEC_BATCH, width), lambda i: (N_PROMPT // DEC_BATCH, 0))
    anyspec = pl.BlockSpec(memory_space=pl.ANY)
    return pl.pallas_call(
        _decode_kernel,
        grid=(DEC_BATCH // t,),
        in_specs=[
            pl.BlockSpec(memory_space=pltpu.SMEM),
            _full((DEC_BATCH, D_MODEL)), cache, cache,
            _full((1, D_MODEL)), _full((D_MODEL, IN_WIDTH)), _full((1, A_WIDTH)), _full((1, A_WIDTH)),
            _full((1, A_WIDTH)), _full((1, A_WIDTH)), _full((1, A_WIDTH)), _full((1, B_WIDTH)),
            _full((D_MODEL, D_MODEL)), _full((1, D_MODEL)), _full((D_MODEL, LANES)), _full((1, LANES)),
            anyspec, anyspec, anyspec, anyspec,
        ],
        out_specs=[tok(D_MODEL), tok(D_MODEL), tok(LANES), tok(LANES), cache, cache, _full((DEC_BATCH, A_WIDTH))],
        out_shape=[
            jax.ShapeDtypeStruct((N_PAD, D_MODEL), F32),
            jax.ShapeDtypeStruct((N_PAD, D_MODEL), BF16),
            jax.ShapeDtypeStruct((N_PAD, LANES), F32),
            jax.ShapeDtypeStruct((N_PAD, LANES), F32),
            jax.ShapeDtypeStruct((DEC_BATCH * per_seq, HEAD_DIM), F32),
            jax.ShapeDtypeStruct((DEC_BATCH * per_seq, HEAD_DIM), F32),
            jax.ShapeDtypeStruct((DEC_BATCH, A_WIDTH), F32),
        ],
        scratch_shapes=[pltpu.VMEM((DEC_BATCH, B_WIDTH), F32), pltpu.VMEM((DEC_BATCH, KV_WIDTH), F32),
                        pltpu.VMEM((DEC_BATCH, KV_WIDTH), F32), pltpu.VMEM((DEC_BATCH, A_WIDTH), F32),
                        pltpu.VMEM((DEC_BATCH, B_WIDTH), F32)],
        input_output_aliases={16: 0, 17: 1, 18: 2, 19: 3},
        compiler_params=pltpu.CompilerParams(
            dimension_semantics=("arbitrary",), vmem_limit_bytes=VMEM_LIMIT),
        name="sample_premoe",
    )(sinks, x, ck, cv, gmix, win, lng, lnb, w00, b0, goa, gob, wo, gmoe, wr, br, h1, xn, gm, sm)


def _plan_kernel(sm_ref,
                 destm_ref, destt_ref, xbe_ref, nxblk_ref, stab_ref, ctab_ref, astart_ref, nwin_ref,
                 base_s, pstart_s):
    ph = pl.program_id(0)
    step = pl.program_id(1)
    lane = _lane_iota((1, LANES))

    @pl.when((ph == 0) & (step == 0))
    def _():
        base_s[...] = jnp.zeros_like(base_s)

    @pl.when(ph == 0)
    def _():
        base_s[...] += jnp.sum(sm_ref[...], axis=0, keepdims=True)

    @pl.when((ph == 1) & (step == 0))
    def _():
        counts = base_s[...]
        padded = jnp.floor((counts + (DISP_CHUNK + EXP_BLOCK - 1)) * (1.0 / EXP_BLOCK)) * EXP_BLOCK
        padded = jnp.where(counts > 0.0, padded, 0.0)
        pend = padded
        for s in (1, 2, 4, 8, 16):
            pend = pend + jnp.where(lane >= s, pltpu.roll(pend, s, 1), 0.0)
        spare = (N_ROWS + lane * DISP_CHUNK).astype(F32)
        pstart_s[...] = jnp.where(counts > 0.0, pend - padded, spare)
        base_s[...] = jnp.zeros_like(base_s)
        brow = lax.broadcasted_iota(jnp.int32, (XBE_ROWS, LANES), 0).astype(F32) * EXP_BLOCK
        done = jnp.where((lane < N_EXPERTS) & (pend <= brow), 1.0, 0.0)
        be = jnp.minimum(jnp.sum(done, axis=-1, keepdims=True), N_EXPERTS - 1.0)
        xbe_ref[...] = jnp.broadcast_to(be, (XBE_ROWS, LANES)).astype(jnp.int32)
        total = jnp.sum(jnp.where(lane == N_EXPERTS - 1, pend, 0.0), axis=-1, keepdims=True)
        nxblk_ref[...] = jnp.broadcast_to(total * (1.0 / EXP_BLOCK), (8, LANES)).astype(jnp.int32)
        stab_ref[...] = jnp.zeros_like(stab_ref)
        ctab_ref[...] = jnp.zeros_like(ctab_ref)
        astart_ref[...] = jnp.zeros_like(astart_ref)
        nwin_ref[...] = jnp.zeros_like(nwin_ref)

    @pl.when(ph == 1)
    def _():
        r = lax.broadcasted_iota(jnp.int32, (TOK_TILE, TOK_TILE), 0)
        c = lax.broadcasted_iota(jnp.int32, (TOK_TILE, TOK_TILE), 1)
        lower = jnp.where(c < r, 1.0, 0.0).astype(BF16)
        for q in range(TM // TOK_TILE):
            i = step * (TM // TOK_TILE) + q
            sel = sm_ref[q * TOK_TILE:(q + 1) * TOK_TILE, :]
            cnt = jnp.sum(sel, axis=0, keepdims=True)
            prefix = jnp.dot(lower, sel.astype(BF16), preferred_element_type=F32)
            start = pstart_s[...] + base_s[...]
            dest = jnp.where(sel > 0.0, prefix + start, -1.0)
            destm_ref[q * TOK_TILE:(q + 1) * TOK_TILE, :] = dest
            destt_ref[:, q * TOK_TILE:(q + 1) * TOK_TILE] = dest.T
            has = (cnt > 0.0) & (lane < N_EXPERTS)
            stab_ref[pl.ds(i, 1), :] = start.astype(jnp.int32)
            ctab_ref[pl.ds(i, 1), :] = jnp.where(has, cnt, 0.0).astype(jnp.int32)
            a = jnp.minimum(jnp.floor(start * (1.0 / WIN_ALIGN)) * WIN_ALIGN, float(N_ROWS - WIN))
            nw = jnp.where(has, jnp.floor((start + cnt - a + (WIN - 1)) * (1.0 / WIN)), 0.0)
            astart_ref[pl.ds(i, 1), :] = a.astype(jnp.int32)
            nwin_ref[pl.ds(i, 1), :] = nw.astype(jnp.int32)
            base_s[...] += cnt


def _plan_call(sm):
    tile = lambda ph, i: (i * ph, 0)
    tile_t = lambda ph, i: (0, i * ph)
    tab = jax.ShapeDtypeStruct((TAB_ROWS, LANES), jnp.int32)
    return pl.pallas_call(
        _plan_kernel,
        grid=(2, N_PAD // TM),
        in_specs=[pl.BlockSpec((TM, LANES), lambda ph, i: (i, 0))],
        out_specs=[
            pl.BlockSpec((TM, LANES), tile),
            pl.BlockSpec((LANES, TM), tile_t),
            _full((XBE_ROWS, LANES)), _full((8, LANES)),
            _full((TAB_ROWS, LANES)), _full((TAB_ROWS, LANES)), _full((TAB_ROWS, LANES)), _full((TAB_ROWS, LANES)),
        ],
        out_shape=[
            jax.ShapeDtypeStruct((N_PAD, LANES), F32),
            jax.ShapeDtypeStruct((LANES, N_PAD), F32),
            jax.ShapeDtypeStruct((XBE_ROWS, LANES), jnp.int32),
            jax.ShapeDtypeStruct((8, LANES), jnp.int32),
            tab, tab, tab, tab,
        ],
        scratch_shapes=[pltpu.VMEM((1, LANES), F32), pltpu.VMEM((1, LANES), F32)],
        compiler_params=pltpu.CompilerParams(
            dimension_semantics=("arbitrary", "arbitrary"), vmem_limit_bytes=VMEM_LIMIT),
        name="moe_plan",
    )(sm)


def _pack_rows(z):
    half = D_MODEL // 2
    lo = lax.bitcast_convert_type(z[:, :half], jnp.uint32) >> 16
    hi = lax.bitcast_convert_type(z[:, half:], jnp.uint32) & jnp.uint32(0xFFFF0000)
    return lax.bitcast_convert_type(hi | lo, jnp.int32)


def _unpack_rows(ref, rows=None):
    rows = ref.shape[0] if rows is None else rows
    flat = ref.reshape(ref.shape[0] * PACK, LANES)
    lo, hi = [], []
    for s in range(PACK):
        w = lax.bitcast_convert_type(flat[pl.ds(s, rows, stride=PACK), :], jnp.uint32)
        lo.append(lax.bitcast_convert_type(w << 16, F32))
        hi.append(lax.bitcast_convert_type(w & jnp.uint32(0xFFFF0000), F32))
    return jnp.concatenate(lo + hi, axis=-1).astype(BF16)


def _dispatch_kernel(stab_ref, ctab_ref, cmax_ref, xn_ref, destt_ref, xs_in, xs_ref,
                     stage0, stage1, stage2, sems, sem2):
    del xs_in
    i = pl.program_id(0)
    last = pl.num_programs(0) - 1
    x = xn_ref[...]
    dt = destt_ref[...]
    rio = lax.broadcasted_iota(jnp.int32, (DISP_CHUNK, 1), 0).astype(F32)

    def chunk_rows(j, stage):
        parts = []
        for e in range(N_EXPERTS):
            first = (stab_ref[i * N_EXPERTS + e] + j * DISP_CHUNK).astype(F32)
            parts.append(jnp.where(dt[e:e + 1, :] == first + rio, 1.0, 0.0).astype(BF16))
        onehot = jnp.concatenate(parts, axis=0)
        words = _pack_rows(jnp.dot(onehot, x, preferred_element_type=F32))
        for s in range(PACK):
            stage[pl.ds(s, N_EXPERTS * DISP_CHUNK, stride=PACK), :] = words[:, s * LANES:(s + 1) * LANES]

    def copy(stage, step, e, j, sem):
        first = stab_ref[step * N_EXPERTS + e] + j * DISP_CHUNK
        rows = stage.reshape(N_EXPERTS * DISP_CHUNK, PACK, LANES)
        return pltpu.make_async_copy(rows.at[pl.ds(e * DISP_CHUNK, DISP_CHUNK)],
                                     xs_ref.at[pl.ds(first, DISP_CHUNK)], sem)

    def step_body(stage, prev_stage, par):
        chunk_rows(0, stage)

        @pl.when(i > 0)
        def _():
            for e in range(N_EXPERTS):
                copy(prev_stage, i - 1, e, 0, sems.at[1 - par, e]).wait()

        for e in range(N_EXPERTS):
            copy(stage, i, e, 0, sems.at[par, e]).start()

        @pl.when(i == last)
        def _():
            for e in range(N_EXPERTS):
                copy(stage, i, e, 0, sems.at[par, e]).wait()

    @pl.when(i % 2 == 0)
    def _():
        step_body(stage0, stage1, 0)

    @pl.when(i % 2 == 1)
    def _():
        step_body(stage1, stage0, 1)

    for j in range(1, TOK_TILE // DISP_CHUNK):

        @pl.when(cmax_ref[i] > j * DISP_CHUNK)
        def _(j=j):
            chunk_rows(j, stage2)
            for e in range(N_EXPERTS):

                @pl.when(ctab_ref[i * N_EXPERTS + e] > j * DISP_CHUNK)
                def _(e=e):
                    cp = copy(stage2, i, e, j, sem2)
                    cp.start()
                    cp.wait()


def _dispatch_call(stab, ctab, cmax, xn, destt, xs_zero):
    stage = pltpu.VMEM((N_EXPERTS * DISP_CHUNK * PACK, LANES), jnp.int32)
    grid_spec = pltpu.PrefetchScalarGridSpec(
        num_scalar_prefetch=3,
        grid=(N_PAD_TILES,),
        in_specs=[
            pl.BlockSpec((TOK_TILE, D_MODEL), lambda i, *_: (i, 0)),
            pl.BlockSpec((N_EXPERTS, TOK_TILE), lambda i, *_: (0, i)),
            pl.BlockSpec(memory_space=pl.ANY),
        ],
        out_specs=pl.BlockSpec(memory_space=pl.ANY),
        scratch_shapes=[stage, stage, stage, pltpu.SemaphoreType.DMA((2, N_EXPERTS)),
                        pltpu.SemaphoreType.DMA],
    )
    return pl.pallas_call(
        _dispatch_kernel,
        grid_spec=grid_spec,
        out_shape=jax.ShapeDtypeStruct((XS_ROWS, PACK, LANES), jnp.int32),
        input_output_aliases={5: 0},
        compiler_params=pltpu.CompilerParams(
            dimension_semantics=("arbitrary",), vmem_limit_bytes=VMEM_LIMIT),
        name="moe_dispatch",
    )(stab, ctab, cmax, xn, destt, xs_zero)


def _expert_kernel(blke_ref, nblk_ref, next_ref, live_ref, xs_ref, wgu_hbm, bgu_ref, wdn_hbm, bdn_ref,
                   ys_ref, wgu_f, wdn_f, wgu_s, wdn_s, sems):
    b = pl.program_id(0)
    used = b < nblk_ref[0]
    prev = blke_ref[jnp.maximum(b - 1, 0)]
    fresh = used & ((b == 0) | (blke_ref[b] != prev))

    def fetch(e):
        return (pltpu.make_async_copy(wgu_hbm.at[e], wgu_f, sems.at[0]),
                pltpu.make_async_copy(wdn_hbm.at[e], wdn_f, sems.at[1]))

    @pl.when(b == 0)
    def _():
        for cp in fetch(blke_ref[0]):
            cp.start()

    @pl.when(fresh)
    def _():
        for cp in fetch(blke_ref[b]):
            cp.wait()
        wgu_s[...] = wgu_f[...].astype(BF16)
        wdn_s[...] = wdn_f[...].astype(BF16)

        @pl.when(next_ref[b] >= 0)
        def _():
            for cp in fetch(next_ref[b]):
                cp.start()

    for groups in range(1, EXP_BLOCK // ROW_GROUP + 1):
        rows = groups * ROW_GROUP

        @pl.when(used & (live_ref[b] == groups))
        def _(rows=rows):
            hid = jnp.dot(_unpack_rows(xs_ref, rows), wgu_s[...], preferred_element_type=F32) + bgu_ref[...]
            gate = jnp.minimum(hid[:, :D_FF], SWIGLU_LIMIT)
            up = jnp.clip(hid[:, D_FF:], -SWIGLU_LIMIT, SWIGLU_LIMIT)
            act = (up + 1.0) * gate * jax.nn.sigmoid(SWIGLU_ALPHA * gate)
            y = jnp.dot(act.astype(BF16), wdn_s[...], preferred_element_type=F32) + bdn_ref[...]
            ys_ref[0:rows, :] = y.astype(BF16)
            if rows < EXP_BLOCK:
                ys_ref[rows:, :] = jnp.zeros((EXP_BLOCK - rows, D_MODEL), BF16)

    @pl.when(jnp.logical_not(used) | (live_ref[b] == 0))
    def _():
        ys_ref[...] = jnp.zeros_like(ys_ref)


def _expert_call(blke, nblk, nxt, live, xs, wgu, bgu, wdn, bdn):
    grid_spec = pltpu.PrefetchScalarGridSpec(
        num_scalar_prefetch=4,
        grid=(N_XBLOCKS,),
        in_specs=[
            pl.BlockSpec((EXP_BLOCK, PACK, LANES), lambda b, be, *_: (b, 0, 0)),
            pl.BlockSpec(memory_space=pl.ANY),
            pl.BlockSpec((None, 1, 2 * D_FF), lambda b, be, *_: (be[b], 0, 0)),
            pl.BlockSpec(memory_space=pl.ANY),
            pl.BlockSpec((None, 1, D_MODEL), lambda b, be, *_: (be[b], 0, 0)),
        ],
        out_specs=pl.BlockSpec((EXP_BLOCK, D_MODEL), lambda b, be, *_: (b, 0)),
        scratch_shapes=[pltpu.VMEM((D_MODEL, 2 * D_FF), F32), pltpu.VMEM((D_FF, D_MODEL), F32),
                        pltpu.VMEM((D_MODEL, 2 * D_FF), BF16), pltpu.VMEM((D_FF, D_MODEL), BF16),
                        pltpu.SemaphoreType.DMA((2,))],
    )
    return pl.pallas_call(
        _expert_kernel,
        grid_spec=grid_spec,
        out_shape=jax.ShapeDtypeStruct((N_ROWS, D_MODEL), BF16),
        compiler_params=pltpu.CompilerParams(
            dimension_semantics=("arbitrary",), vmem_limit_bytes=VMEM_LIMIT),
        name="moe_experts",
    )(blke, nblk, nxt, live, xs, wgu, bgu, wdn, bdn)


def _combine_kernel(*refs):
    astart_ref, nwin_ref, over_ref = refs[0:3]
    win_refs = refs[3:3 + N_EXPERTS]
    (destm_ref, gm_ref, h1_ref, plep_ref, ples_ref, gple_ref, wpg_ref, wpp_ref, gfin_ref, ys_any,
     yp_ref, ysm_ref, moe_s, tmp_s, sem) = refs[3 + N_EXPERTS:]
    i = pl.program_id(0)
    dest = destm_ref[...]
    gates = gm_ref[...]
    lane = _lane_iota((TOK_TILE, LANES))
    lane_f = lane.astype(F32)
    lo = lane < WIN
    moe = jnp.zeros((TOK_TILE, D_MODEL), F32)
    group = 4
    for g0 in range(0, N_EXPERTS, group):
        g_hi, g_lo = [], []
        for p in range(group // 2):
            e0 = g0 + 2 * p
            a0 = astart_ref[i * N_EXPERTS + e0].astype(F32)
            a1 = astart_ref[i * N_EXPERTS + e0 + 1].astype(F32)
            rowid = jnp.where(lo, a0 + lane_f, a1 + lane_f - WIN)
            dcol = jnp.where(lo, dest[:, e0:e0 + 1], dest[:, e0 + 1:e0 + 2])
            gcol = jnp.where(lo, gates[:, e0:e0 + 1], gates[:, e0 + 1:e0 + 2])
            gsel = jnp.where(dcol == rowid, gcol, 0.0)
            hi = gsel.astype(BF16)
            g_hi.append(hi)
            g_lo.append((gsel - hi.astype(F32)).astype(BF16))
        ywin = jnp.concatenate([win_refs[g0 + q][...] for q in range(group)], axis=0)
        both = jnp.concatenate([jnp.concatenate(g_hi, axis=-1), jnp.concatenate(g_lo, axis=-1)], axis=0)
        r = jnp.dot(both, ywin, preferred_element_type=F32)
        moe = moe + r[:TOK_TILE] + r[TOK_TILE:]
    mrows = pl.ds(pl.multiple_of((i % TAIL_TILES) * TOK_TILE, TOK_TILE), TOK_TILE)
    moe_s[mrows, :] = moe

    @pl.when(over_ref[i] > 0)
    def _():
        tmp_s[...] = jnp.zeros_like(tmp_s)

        def per_expert(e, carry):
            a = astart_ref[i * N_EXPERTS + e]
            dcol = jnp.sum(jnp.where(lane == e, dest, 0.0), axis=-1, keepdims=True)
            gcol = jnp.sum(jnp.where(lane == e, gates, 0.0), axis=-1, keepdims=True)

            def per_window(w, carry2):
                first = a + w * WIN
                start = pl.multiple_of(jnp.minimum(first, N_ROWS - WIN), WIN_ALIGN)
                cp = pltpu.make_async_copy(ys_any.at[pl.ds(start, WIN)], tmp_s.at[pl.ds(0, WIN)], sem)
                cp.start()
                cp.wait()
                hit = lo & (dcol == start.astype(F32) + lane_f) & (dcol >= first.astype(F32))
                gsel = jnp.where(hit, gcol, 0.0)
                hi = gsel.astype(BF16)
                rest = (gsel - hi.astype(F32)).astype(BF16)
                moe_s[mrows, :] += (jnp.dot(hi, tmp_s[...], preferred_element_type=F32)
                                    + jnp.dot(rest, tmp_s[...], preferred_element_type=F32))
                return carry2

            return lax.fori_loop(1, nwin_ref[i * N_EXPERTS + e], per_window, carry)

        lax.fori_loop(0, N_EXPERTS, per_expert, 0)

    def tail(h1, moe_rows, ple):
        h2 = h1 + moe_rows
        hn = _rms(h2, gple_ref[...]).astype(BF16)
        gate = jax.nn.sigmoid(jnp.dot(hn, wpg_ref[...], preferred_element_type=F32))
        proj = jnp.dot(ple.astype(BF16), wpp_ref[...], preferred_element_type=F32)
        return _rms(h2 + gate * proj, gfin_ref[...])

    is_sample = i == N_TILES - 1

    @pl.when((i % TAIL_TILES == TAIL_TILES - 1) & jnp.logical_not(is_sample))
    def _():
        yp_ref[...] = tail(h1_ref[...], moe_s[...], plep_ref[...])

    @pl.when(is_sample)
    def _():
        ysm_ref[...] = tail(h1_ref[0:TOK_TILE, :], moe_s[0:TOK_TILE, :], ples_ref[...])


def _combine_call(astart, nwin, over, ys, destm, gm, h1, plep, ples, gple, wpg, wpp, gfin):
    rows = TAIL_TILES * TOK_TILE
    last = N_PROMPT // rows - 1

    def win_spec(e):
        return pl.BlockSpec((pl.Element(WIN), pl.Element(D_MODEL)),
                            lambda i, a, nw, ov, e=e: (pl.multiple_of(a[i * N_EXPERTS + e], WIN_ALIGN), 0))

    grid_spec = pltpu.PrefetchScalarGridSpec(
        num_scalar_prefetch=3,
        grid=(N_TILES,),
        in_specs=[win_spec(e) for e in range(N_EXPERTS)] + [
            pl.BlockSpec((TOK_TILE, LANES), lambda i, *_: (i, 0)),
            pl.BlockSpec((TOK_TILE, LANES), lambda i, *_: (i, 0)),
            pl.BlockSpec((rows, D_MODEL), lambda i, *_: (i // TAIL_TILES, 0)),
            pl.BlockSpec((rows, PLE_DIM), lambda i, *_: (jnp.minimum(i // TAIL_TILES, last), 0)),
            pl.BlockSpec((TOK_TILE, PLE_DIM), lambda i, *_: (0, 0)),
            pl.BlockSpec((1, D_MODEL), lambda i, *_: (0, 0)),
            pl.BlockSpec((D_MODEL, D_MODEL), lambda i, *_: (0, 0)),
            pl.BlockSpec((PLE_DIM, D_MODEL), lambda i, *_: (0, 0)),
            pl.BlockSpec((1, D_MODEL), lambda i, *_: (0, 0)),
            pl.BlockSpec(memory_space=pl.ANY),
        ],
        out_specs=[
            pl.BlockSpec((rows, D_MODEL), lambda i, *_: (jnp.minimum(i // TAIL_TILES, last), 0)),
            pl.BlockSpec((TOK_TILE, D_MODEL), lambda i, *_: (0, 0)),
        ],
        scratch_shapes=[pltpu.VMEM((rows, D_MODEL), F32), pltpu.VMEM((2 * WIN, D_MODEL), BF16),
                        pltpu.SemaphoreType.DMA],
    )
    return pl.pallas_call(
        _combine_kernel,
        grid_spec=grid_spec,
        out_shape=[jax.ShapeDtypeStruct((N_PROMPT, D_MODEL), F32),
                   jax.ShapeDtypeStruct((DEC_BATCH, D_MODEL), F32)],
        compiler_params=pltpu.CompilerParams(
            dimension_semantics=("arbitrary",), vmem_limit_bytes=VMEM_LIMIT),
        name="moe_combine_tail",
    )(astart, nwin, over, *([ys] * N_EXPERTS), destm, gm, h1, plep, ples, gple, wpg, wpp, gfin, ys)


def _gather_kernel(*refs):
    astart_ref, nwin_ref, over_ref = refs[0:3]
    win_refs = refs[3:3 + N_EXPERTS]
    (destm_ref, gm_ref, h1_ref, plep_ref, ples_ref, gple_ref, wpg_ref, wpp_ref, gfin_ref, ys_any,
     yp_ref, ysm_ref, moe_s, tmp_s, sem) = refs[3 + N_EXPERTS:]
    i = pl.program_id(0)
    slot = i % TAIL_TILES
    dest = destm_ref[...]
    gates = gm_ref[...]
    lane = _lane_iota((TOK_TILE, LANES))
    lane_f = lane.astype(F32)
    per_slab = LANES // CWIN
    within = (lane % CWIN).astype(F32)
    group = 2 * per_slab

    def split(gsel):
        hi = gsel.astype(BF16)
        return hi, (gsel - hi.astype(F32)).astype(BF16)

    moe = jnp.zeros((TOK_TILE, D_MODEL), F32)
    for g0 in range(0, N_EXPERTS, group):
        his, los = [], []
        for sl in range(2):
            rowid = jnp.zeros((TOK_TILE, LANES), F32)
            dcol = jnp.zeros((TOK_TILE, LANES), F32)
            gcol = jnp.zeros((TOK_TILE, LANES), F32)
            for q in range(per_slab):
                e = g0 + sl * per_slab + q
                mine = (lane >= q * CWIN) & (lane < (q + 1) * CWIN)
                rowid = jnp.where(mine, astart_ref[i * N_EXPERTS + e].astype(F32) + within, rowid)
                dcol = jnp.where(mine, dest[:, e:e + 1], dcol)
                gcol = jnp.where(mine, gates[:, e:e + 1], gcol)
            hi, lo = split(jnp.where(dcol == rowid, gcol, 0.0))
            his.append(hi)
            los.append(lo)
        ywin = jnp.concatenate([_unpack_rows(win_refs[g0 + q]) for q in range(group)], axis=0)
        both = jnp.concatenate([jnp.concatenate(his, axis=-1), jnp.concatenate(los, axis=-1)], axis=0)
        r = jnp.dot(both, ywin, preferred_element_type=F32)
        moe = moe + r[:TOK_TILE] + r[TOK_TILE:]
    mrows = pl.ds(pl.multiple_of(slot * TOK_TILE, TOK_TILE), TOK_TILE)
    moe_s[mrows, :] = moe

    @pl.when(over_ref[i] > 0)
    def _():
        def per_expert(e, carry):
            a = astart_ref[i * N_EXPERTS + e]
            dcol = jnp.sum(jnp.where(lane == e, dest, 0.0), axis=-1, keepdims=True)
            gcol = jnp.sum(jnp.where(lane == e, gates, 0.0), axis=-1, keepdims=True)

            def per_window(w, carry2):
                first = a + w * CWIN
                start = jnp.minimum(first, N_ROWS - CWIN)
                cp = pltpu.make_async_copy(ys_any.at[pl.ds(start, CWIN)], tmp_s, sem)
                cp.start()
                cp.wait()
                hit = (lane < CWIN) & (dcol == start.astype(F32) + lane_f) & (dcol >= first.astype(F32))
                hi, lo = split(jnp.where(hit, gcol, 0.0))
                rows = jnp.concatenate([_unpack_rows(tmp_s), jnp.zeros((LANES - CWIN, D_MODEL), BF16)], axis=0)
                moe_s[mrows, :] += (jnp.dot(hi, rows, preferred_element_type=F32)
                                    + jnp.dot(lo, rows, preferred_element_type=F32))
                return carry2

            return lax.fori_loop(1, nwin_ref[i * N_EXPERTS + e], per_window, carry)

        lax.fori_loop(0, N_EXPERTS, per_expert, 0)

    def tail(h1, moe_rows, ple):
        h2 = h1 + moe_rows
        hn = _rms(h2, gple_ref[...]).astype(BF16)
        gate = jax.nn.sigmoid(jnp.dot(hn, wpg_ref[...], preferred_element_type=F32))
        proj = jnp.dot(ple.astype(BF16), wpp_ref[...], preferred_element_type=F32)
        return _rms(h2 + gate * proj, gfin_ref[...])

    is_sample = i == N_TILES - 1

    @pl.when((slot == TAIL_TILES - 1) & jnp.logical_not(is_sample))
    def _():
        yp_ref[...] = tail(h1_ref[...], moe_s[...], plep_ref[...])

    @pl.when(is_sample)
    def _():
        ysm_ref[...] = tail(h1_ref[0:TOK_TILE, :], moe_s[0:TOK_TILE, :], ples_ref[...])


def _gather_call(astart, nwin, over, ys, destm, gm, h1, plep, ples, gple, wpg, wpp, gfin):
    rows = TAIL_TILES * TOK_TILE
    last = N_PROMPT // rows - 1

    def win_spec(e):
        return pl.BlockSpec((pl.Element(CWIN), pl.Element(PACK), pl.Element(LANES)),
                            lambda i, a, nw, ov, e=e: (a[i * N_EXPERTS + e], 0, 0))

    grid_spec = pltpu.PrefetchScalarGridSpec(
        num_scalar_prefetch=3,
        grid=(N_TILES,),
        in_specs=[win_spec(e) for e in range(N_EXPERTS)] + [
            pl.BlockSpec((TOK_TILE, LANES), lambda i, *_: (i, 0)),
            pl.BlockSpec((TOK_TILE, LANES), lambda i, *_: (i, 0)),
            pl.BlockSpec((rows, D_MODEL), lambda i, *_: (i // TAIL_TILES, 0)),
            pl.BlockSpec((rows, PLE_DIM), lambda i, *_: (jnp.minimum(i // TAIL_TILES, last), 0)),
            pl.BlockSpec((TOK_TILE, PLE_DIM), lambda i, *_: (0, 0)),
            pl.BlockSpec((1, D_MODEL), lambda i, *_: (0, 0)),
            pl.BlockSpec((D_MODEL, D_MODEL), lambda i, *_: (0, 0)),
            pl.BlockSpec((PLE_DIM, D_MODEL), lambda i, *_: (0, 0)),
            pl.BlockSpec((1, D_MODEL), lambda i, *_: (0, 0)),
            pl.BlockSpec(memory_space=pl.ANY),
        ],
        out_specs=[
            pl.BlockSpec((rows, D_MODEL), lambda i, *_: (jnp.minimum(i // TAIL_TILES, last), 0)),
            pl.BlockSpec((TOK_TILE, D_MODEL), lambda i, *_: (0, 0)),
        ],
        scratch_shapes=[pltpu.VMEM((rows, D_MODEL), F32), pltpu.VMEM((CWIN, PACK, LANES), jnp.int32),
                        pltpu.SemaphoreType.DMA],
    )
    return pl.pallas_call(
        _gather_kernel,
        grid_spec=grid_spec,
        out_shape=[jax.ShapeDtypeStruct((N_PROMPT, D_MODEL), F32),
                   jax.ShapeDtypeStruct((DEC_BATCH, D_MODEL), F32)],
        compiler_params=pltpu.CompilerParams(
            dimension_semantics=("arbitrary",), vmem_limit_bytes=VMEM_LIMIT),
        name="moe_combine_tail",
    )(astart, nwin, over, *([ys] * N_EXPERTS), destm, gm, h1, plep, ples, gple, wpg, wpp, gfin, ys)


def kernel(x_prompt, x_sample, cache_swa_k, cache_swa_v, p_prompt, p_sample, g_mix, w_in, ln_v_g, ln_v_b,
           w_sp, b_sp, sinks, g_out_a, g_out_b, w_o, g_moe, w_router, b_router, w_gu, b_gu, w_dn, b_dn,
           g_ple, w_ple_gate, w_ple_proj, g_final):
    l = 0
    row = lambda v: v.reshape(1, -1)
    win = w_in[l].astype(BF16)
    wo = w_o[l].astype(BF16)
    tril = jnp.tril(jnp.ones((CHUNK, CHUNK), dtype=bool))
    wsp = jnp.where(tril, w_sp[l], 0.0).astype(BF16)
    bsp = jnp.repeat(b_sp[l].T, HEAD_DIM, axis=1)
    w00 = row(jnp.repeat(w_sp[l][:, 0, 0], HEAD_DIM))
    b0 = row(jnp.repeat(b_sp[l][:, 0], HEAD_DIM))
    wr_hi = w_router[l].astype(BF16)
    wr_lo = (w_router[l] - wr_hi.astype(F32)).astype(BF16)
    wr = jnp.concatenate([wr_hi, wr_lo, jnp.zeros((D_MODEL, LANES - 2 * N_EXPERTS), BF16)], axis=1)
    br = row(jnp.concatenate([b_router[l], jnp.zeros((LANES - N_EXPERTS,), F32)]))
    common = (row(g_mix[l]), win, row(ln_v_g[l]), row(ln_v_b[l]))
    tail = (row(g_out_a[l]), row(g_out_b[l]), wo, row(g_moe[l]), wr, br)

    h1, xn, gm, sm, k_p, v_p = _prompt_call(
        x_prompt.reshape(N_PROMPT, D_MODEL), sinks[l], *common, wsp, bsp, *tail)
    ck = cache_swa_k[l].reshape(DEC_BATCH * CHUNK * 2, HEAD_DIM)
    cv = cache_swa_v[l].reshape(DEC_BATCH * CHUNK * 2, HEAD_DIM)
    h1, xn, gm, sm, k_s, v_s, va_s = _decode_call(
        x_sample.reshape(DEC_BATCH, D_MODEL), ck, cv, sinks[l], *common, w00, b0, *tail, h1, xn, gm, sm)

    destm, destt, xbe, nxblk, stab, ctab, astart, nwin = _plan_call(sm)
    flat = lambda tab, n: tab[:n, :N_EXPERTS].reshape(-1)
    astart1 = flat(astart, N_TILES)
    nwin2 = nwin[:N_TILES, :N_EXPERTS]
    over1 = (jnp.max(nwin2, axis=1) > 1).astype(jnp.int32)
    nwin1 = nwin2.reshape(-1)
    cmax1 = jnp.max(ctab[:N_PAD_TILES, :N_EXPERTS], axis=1)

    xs = _dispatch_call(flat(stab, N_PAD_TILES), flat(ctab, N_PAD_TILES), cmax1, xn, destt[:N_EXPERTS],
                        jnp.zeros((XS_ROWS, PACK, LANES), jnp.int32))
    xbe1, nxblk1 = xbe[:N_XBLOCKS, 0], nxblk[0, :1]
    blk = jnp.arange(N_XBLOCKS, dtype=jnp.int32)
    starts = (blk < nxblk1[0]) & ((blk == 0) | (xbe1 != jnp.roll(xbe1, 1)))
    pos = jnp.where(starts, blk, N_XBLOCKS)
    nxt_pos = jnp.roll(lax.cummin(pos, reverse=True), -1).at[N_XBLOCKS - 1].set(N_XBLOCKS)
    nxt1 = jnp.where(nxt_pos < N_XBLOCKS, xbe1[jnp.minimum(nxt_pos, N_XBLOCKS - 1)], -1).astype(jnp.int32)
    total = jnp.sum(ctab[:N_PAD_TILES, :N_EXPERTS], axis=0)
    real = jnp.clip(total[xbe1] - (blk * EXP_BLOCK - stab[0, :N_EXPERTS][xbe1]), 0, EXP_BLOCK)
    live1 = ((real + ROW_GROUP - 1) // ROW_GROUP).astype(jnp.int32)
    ys = _expert_call(xbe1, nxblk1, nxt1, live1, xs, w_gu[l], b_gu[l].reshape(N_EXPERTS, 1, 2 * D_FF),
                      w_dn[l], b_dn[l].reshape(N_EXPERTS, 1, D_MODEL))
    y_p, y_s = _combine_call(
        astart1, nwin1, over1, ys, destm, gm, h1,
        p_prompt[l].reshape(N_PROMPT, PLE_DIM), p_sample[l].reshape(DEC_BATCH, PLE_DIM),
        row(g_ple[l]), w_ple_gate[l].astype(BF16), w_ple_proj[l].astype(BF16), row(g_final))

    kv5 = lambda a, n: a.reshape(1, n, CHUNK, 2, HEAD_DIM)
    return (y_p.reshape(BATCH, SEQ, D_MODEL), y_s.reshape(DEC_BATCH, 1, D_MODEL),
            kv5(k_p, BATCH), kv5(v_p, BATCH), kv5(k_s, DEC_BATCH), kv5(v_s, DEC_BATCH),
            va_s.reshape(1, DEC_BATCH, 1, A_WIDTH))
```

```python
import math

import jax
import jax.numpy as jnp
from jax import lax
from jax.experimental import pallas as pl
from jax.experimental.pallas import tpu as pltpu

F32 = jnp.float32
BF16 = jnp.bfloat16

D_MODEL = 1024
BATCH = 4
SEQ = 4096
DEC_BATCH = 128
HEAD_DIM = 64
A_WIDTH = 512
B_WIDTH = 512
B_HEADS = 8
KV_WIDTH = 128
IN_WIDTH = 2 * A_WIDTH + B_WIDTH + 2 * KV_WIDTH
CHUNK = 128
N_EXPERTS = 32
TOP_K = 4
D_FF = 1024
SWIGLU_ALPHA = 1.702
SWIGLU_LIMIT = 7.0
PLE_DIM = 256
EPS = 1e-5

LANES = 128
ROW_GROUP = 128
EXP_BLOCK = 512
N_PROMPT = BATCH * SEQ
N_TOK = N_PROMPT + DEC_BATCH
TOK_TILE = 128
N_TILES = N_TOK // TOK_TILE
DISP_TILE = 256
TM = 512
N_PAD = ((N_TOK + TM - 1) // TM) * TM
N_PAD_TILES = N_PAD // TOK_TILE
DISP_CHUNK = 32
N_XBLOCKS = (N_TOK * TOP_K + N_EXPERTS * (DISP_CHUNK + EXP_BLOCK - 1) + EXP_BLOCK - 1) // EXP_BLOCK
N_ROWS = N_XBLOCKS * EXP_BLOCK
XS_ROWS = N_ROWS + N_EXPERTS * DISP_CHUNK
XBE_ROWS = ((N_XBLOCKS + 7) // 8) * 8
TAB_ROWS = ((N_PAD_TILES + 7) // 8) * 8
PACK = D_MODEL // 2 // LANES
WIN = 64
WIN_ALIGN = 16
CWIN = 32
TAIL_TILES = 4
SAMPLE_TILE = 32
DEC_TILE = 16
NEG = -1e30
VMEM_LIMIT = 56 * 1024 * 1024


def _rms(x, g):
    return x * lax.rsqrt(jnp.mean(x * x, axis=-1, keepdims=True) + EPS) * g


def _gelu(x):
    c = math.sqrt(2.0 / math.pi)
    return x * (0.5 * (1.0 + jnp.tanh(c * (x + 0.044715 * (x * x * x)))))


def _layernorm(x, g, b):
    mu = jnp.mean(x, axis=-1, keepdims=True)
    xc = x - mu
    return xc * lax.rsqrt(jnp.mean(xc * xc, axis=-1, keepdims=True) + EPS) * g + b


def _lane_iota(shape):
    return lax.broadcasted_iota(jnp.int32, shape, len(shape) - 1)


def _route(xn2, wr_ref, br_ref):
    m = xn2.shape[0]
    xh = xn2.astype(BF16)
    xl = (xn2 - xh.astype(F32)).astype(BF16)
    r = jnp.dot(jnp.concatenate([xh, xl], axis=0), wr_ref[...], preferred_element_type=F32)
    r = r[:m] + r[m:]
    lane = _lane_iota((m, LANES))
    lane_f = lane.astype(F32)
    logits = jnp.where(lane < N_EXPERTS, r + pltpu.roll(r, LANES - N_EXPERTS, 1) + br_ref[...], NEG)
    work = logits
    sel = jnp.zeros((m, LANES), F32)
    top = None
    z = None
    for _ in range(TOP_K):
        mx = jnp.max(work, axis=-1, keepdims=True)
        first = jnp.min(jnp.where(work == mx, lane_f, float(LANES)), axis=-1, keepdims=True)
        hit = lane_f == first
        sel = jnp.where(hit, 1.0, sel)
        work = jnp.where(hit, NEG, work)
        if top is None:
            top = mx
            z = jnp.ones_like(mx)
        else:
            z = z + jnp.exp(mx - top)
    gates = jnp.where(sel > 0.0, jnp.exp(logits - top) / z, 0.0)
    return gates, sel


def _prompt_kernel(sinks_ref, x_ref, gmix_ref, win_ref, lng_ref, lnb_ref, wsp_ref, bsp_ref,
                   goa_ref, gob_ref, wo_ref, gmoe_ref, wr_ref, br_ref,
                   h1_ref, xn_ref, gm_ref, sm_ref, k_ref, v_ref,
                   z_s, kv_s, cat_s):
    g = pl.program_id(0)
    j = g % (SEQ // TM)

    @pl.when(g >= N_PROMPT // TM)
    def _():
        h1_ref[...] = jnp.zeros_like(h1_ref)
        xn_ref[...] = jnp.zeros_like(xn_ref)
        gm_ref[...] = jnp.zeros_like(gm_ref)
        sm_ref[...] = jnp.zeros_like(sm_ref)

    @pl.when(g < N_PROMPT // TM)
    def _():
        _prompt_tile(j, sinks_ref, x_ref, gmix_ref, win_ref, lng_ref, lnb_ref, wsp_ref, bsp_ref,
                     goa_ref, gob_ref, wo_ref, gmoe_ref, wr_ref, br_ref,
                     h1_ref, xn_ref, gm_ref, sm_ref, k_ref, v_ref, z_s, kv_s, cat_s)


def _prompt_tile(j, sinks_ref, x_ref, gmix_ref, win_ref, lng_ref, lnb_ref, wsp_ref, bsp_ref,
                 goa_ref, gob_ref, wo_ref, gmoe_ref, wr_ref, br_ref,
                 h1_ref, xn_ref, gm_ref, sm_ref, k_ref, v_ref, z_s, kv_s, cat_s):
    @pl.when(j == 0)
    def _():
        kv_s[0:CHUNK, :] = jnp.zeros((CHUNK, 2 * KV_WIDTH), F32)

    xn = _rms(x_ref[...], gmix_ref[...]).astype(BF16)
    z_s[...] = jnp.dot(xn, win_ref[...], preferred_element_type=F32)
    kv_s[CHUNK:, :] = z_s[:, 2 * A_WIDTH + B_WIDTH:]

    lane = _lane_iota((CHUNK, LANES))
    lo = lane < HEAD_DIM
    lane2 = _lane_iota((2 * CHUNK, LANES))
    lo2 = lane2 < HEAD_DIM
    qi = lax.broadcasted_iota(jnp.int32, (CHUNK, CHUNK), 0)
    kc = lax.broadcasted_iota(jnp.int32, (CHUNK, CHUNK), 1)
    from_prev = kc > qi
    dist = jnp.where(from_prev, qi + CHUNK - kc, qi - kc).astype(F32)

    def chunk_body(c, carry):
        r0 = pl.multiple_of(c * CHUNK, CHUNK)
        rows = pl.ds(r0, CHUNK)
        u = _gelu(z_s[rows, 0:A_WIDTH])
        va = _layernorm(_gelu(z_s[rows, A_WIDTH:2 * A_WIDTH]), lng_ref[...], lnb_ref[...])
        vab = va.astype(BF16)
        slabs = []
        for p in range(A_WIDTH // LANES):
            slab = vab[:, p * LANES:(p + 1) * LANES]
            m0 = jnp.dot(wsp_ref[2 * p], slab, preferred_element_type=F32)
            m1 = jnp.dot(wsp_ref[2 * p + 1], slab, preferred_element_type=F32)
            slabs.append(jnp.where(lo, m0, m1))
        ya = u * (jnp.concatenate(slabs, axis=-1) + bsp_ref[...])
        ya_n = _rms(ya, goa_ref[...])
        k2 = kv_s[pl.ds(r0, 2 * CHUNK), 0:KV_WIDTH]
        v2 = kv_s[pl.ds(r0, 2 * CHUNK), KV_WIDTH:2 * KV_WIDTH]
        k2r = pltpu.roll(k2, HEAD_DIM, 1)
        v2r = pltpu.roll(v2, HEAD_DIM, 1)
        kd = (jnp.where(lo2, k2, k2r).astype(BF16), jnp.where(lo2, k2r, k2).astype(BF16))
        vd = (jnp.where(lo2, v2, v2r).astype(BF16), jnp.where(lo2, v2r, v2).astype(BF16))
        prev_ok = (j > 0) | (c > 0)
        masked = from_prev & jnp.logical_not(prev_ok)
        yb_slabs = []
        for kv in range(2):
            q0 = z_s[rows, 2 * A_WIDTH + (2 * kv) * LANES:2 * A_WIDTH + (2 * kv + 1) * LANES]
            q1 = z_s[rows, 2 * A_WIDTH + (2 * kv + 1) * LANES:2 * A_WIDTH + (2 * kv + 2) * LANES]
            lhs = jnp.concatenate([jnp.where(lo, q0, 0.0), jnp.where(lo, 0.0, q0),
                                   jnp.where(lo, q1, 0.0), jnp.where(lo, 0.0, q1)], axis=0).astype(BF16)
            s_all = lax.dot_general(lhs, kd[kv], (((1,), (1,)), ((), ())), preferred_element_type=F32)
            probs = []
            for i in range(4):
                h = 4 * kv + i
                slope = 2.0 ** (-(h + 1))
                sink = sinks_ref[h]
                sh = s_all[i * CHUNK:(i + 1) * CHUNK]
                s = jnp.where(from_prev, sh[:, :CHUNK], sh[:, CHUNK:]) * (HEAD_DIM ** -0.5) - slope * dist
                s = jnp.where(masked, NEG, s)
                mx = jnp.maximum(jnp.max(s, axis=-1, keepdims=True), sink)
                e = jnp.exp(s - mx)
                den = jnp.sum(e, axis=-1, keepdims=True) + jnp.exp(sink - mx)
                p = e * (1.0 / den)
                probs.append(jnp.concatenate([jnp.where(from_prev, p, 0.0), jnp.where(from_prev, 0.0, p)], axis=-1))
            pm = jnp.concatenate(probs, axis=0).astype(BF16)
            o = jnp.dot(pm, vd[kv], preferred_element_type=F32)
            yb_slabs.append(jnp.where(lo, o[0:CHUNK], o[CHUNK:2 * CHUNK]))
            yb_slabs.append(jnp.where(lo, o[2 * CHUNK:3 * CHUNK], o[3 * CHUNK:4 * CHUNK]))
        yb_n = _rms(jnp.concatenate(yb_slabs, axis=-1), gob_ref[...])
        cat_s[rows, 0:A_WIDTH] = ya_n.astype(BF16)
        cat_s[rows, A_WIDTH:] = yb_n.astype(BF16)
        return carry

    lax.fori_loop(0, TM // CHUNK, chunk_body, 0)

    kv_s[0:CHUNK, :] = kv_s[TM:TM + CHUNK, :]
    k_ref[...] = kv_s[TM:TM + CHUNK, 0:KV_WIDTH]
    v_ref[...] = kv_s[TM:TM + CHUNK, KV_WIDTH:]

    h1 = x_ref[...] + jnp.dot(cat_s[...], wo_ref[...], preferred_element_type=F32)
    h1_ref[...] = h1
    xn2 = _rms(h1, gmoe_ref[...])
    xn_ref[...] = xn2.astype(BF16)
    gates, sel = _route(xn2, wr_ref, br_ref)
    gm_ref[...] = gates
    sm_ref[...] = sel


def _full(shape):
    n = len(shape)
    return pl.BlockSpec(shape, lambda *_: (0,) * n)


def _prompt_call(x, sinks, gmix, win, lng, lnb, wsp, bsp, goa, gob, wo, gmoe, wr, br):
    real = N_PROMPT // TM
    row = lambda g: (g, 0)
    seq = lambda g: (jnp.minimum(g, real - 1) // (SEQ // TM), 0, 0)
    return pl.pallas_call(
        _prompt_kernel,
        grid=(N_PAD // TM,),
        in_specs=[
            pl.BlockSpec(memory_space=pltpu.SMEM),
            pl.BlockSpec((TM, D_MODEL), lambda g: (jnp.minimum(g, real - 1), 0)),
            _full((1, D_MODEL)), _full((D_MODEL, IN_WIDTH)), _full((1, A_WIDTH)), _full((1, A_WIDTH)),
            _full((8, CHUNK, CHUNK)), _full((CHUNK, A_WIDTH)), _full((1, A_WIDTH)), _full((1, B_WIDTH)),
            _full((D_MODEL, D_MODEL)), _full((1, D_MODEL)), _full((D_MODEL, LANES)), _full((1, LANES)),
        ],
        out_specs=[
            pl.BlockSpec((TM, D_MODEL), row),
            pl.BlockSpec((TM, D_MODEL), row),
            pl.BlockSpec((TM, LANES), row),
            pl.BlockSpec((TM, LANES), row),
            pl.BlockSpec((None, CHUNK, KV_WIDTH), seq),
            pl.BlockSpec((None, CHUNK, KV_WIDTH), seq),
        ],
        out_shape=[
            jax.ShapeDtypeStruct((N_PAD, D_MODEL), F32),
            jax.ShapeDtypeStruct((N_PAD, D_MODEL), BF16),
            jax.ShapeDtypeStruct((N_PAD, LANES), F32),
            jax.ShapeDtypeStruct((N_PAD, LANES), F32),
            jax.ShapeDtypeStruct((BATCH, CHUNK, KV_WIDTH), F32),
            jax.ShapeDtypeStruct((BATCH, CHUNK, KV_WIDTH), F32),
        ],
        scratch_shapes=[
            pltpu.VMEM((TM, IN_WIDTH), F32),
            pltpu.VMEM((TM + CHUNK, 2 * KV_WIDTH), F32),
            pltpu.VMEM((TM, D_MODEL), BF16),
        ],
        compiler_params=pltpu.CompilerParams(
            dimension_semantics=("arbitrary",), vmem_limit_bytes=VMEM_LIMIT),
        name="prompt_premoe",
    )(sinks, x, gmix, win, lng, lnb, wsp, bsp, goa, gob, wo, gmoe, wr, br)


def _sample_kernel(sinks_ref, x_ref, ck_ref, cv_ref, gmix_ref, win_ref, lng_ref, lnb_ref, w00_ref, b0_ref,
                   goa_ref, gob_ref, wo_ref, gmoe_ref, wr_ref, br_ref,
                   h1_in, xn_in, gm_in, sm_in,
                   h1_ref, xn_ref, gm_ref, sm_ref, nk_ref, nv_ref, va_ref):
    del h1_in, xn_in, gm_in, sm_in
    t = SAMPLE_TILE
    nkeys = t * CHUNK

    if True:
        x = x_ref[...]
        xn = _rms(x, gmix_ref[...]).astype(BF16)
        z = jnp.dot(xn, win_ref[...], preferred_element_type=F32)
        u = _gelu(z[:, 0:A_WIDTH])
        va = _layernorm(_gelu(z[:, A_WIDTH:2 * A_WIDTH]), lng_ref[...], lnb_ref[...])
        va_ref[...] = va
        ya_n = _rms(u * (w00_ref[...] * va + b0_ref[...]), goa_ref[...])

        knew = z[:, 2 * A_WIDTH + B_WIDTH:2 * A_WIDTH + B_WIDTH + KV_WIDTH]
        vnew = z[:, 2 * A_WIDTH + B_WIDTH + KV_WIDTH:]
        lane = _lane_iota((t, LANES))
        lo = lane < HEAD_DIM
        stacked = []
        for h in range(B_HEADS):
            q = z[:, 2 * A_WIDTH + (h // 2) * LANES:2 * A_WIDTH + (h // 2 + 1) * LANES]
            qh = jnp.where(lo if h % 2 == 0 else jnp.logical_not(lo), q, 0.0)
            if h % 2 != h // 4:
                qh = pltpu.roll(qh, HEAD_DIM, 1)
            stacked.append(qh)
        qs = jnp.concatenate(stacked, axis=0)
        rows = B_HEADS * t
        ridx = lax.broadcasted_iota(jnp.int32, (rows, 1), 0)
        slope = jnp.zeros((rows, 1), F32)
        sink = jnp.zeros((rows, 1), F32)
        for h in range(B_HEADS):
            in_h = (ridx >= h * t) & (ridx < (h + 1) * t)
            slope = jnp.where(in_h, 2.0 ** (-(h + 1)), slope)
            sink = jnp.where(in_h, sinks_ref[h], sink)
        s_c = lax.dot_general(qs.astype(BF16), ck_ref[...].astype(BF16), (((1,), (1,)), ((), ())),
                              preferred_element_type=F32)
        rsamp = lax.broadcasted_iota(jnp.int32, (rows, nkeys), 0) % t
        col = lax.broadcasted_iota(jnp.int32, (rows, nkeys), 1)
        pos = col % CHUNK
        own = ((col // CHUNK) == rsamp) & (pos >= 1)
        s_c = s_c * (HEAD_DIM ** -0.5) - slope * (CHUNK - pos).astype(F32)
        s_c = jnp.where(own, s_c, NEG)
        kn8 = jnp.concatenate([knew] * B_HEADS, axis=0)
        vn8 = jnp.concatenate([vnew] * B_HEADS, axis=0)
        s_n = jnp.sum(qs * kn8, axis=-1, keepdims=True) * (HEAD_DIM ** -0.5)
        mx = jnp.maximum(jnp.maximum(jnp.max(s_c, axis=-1, keepdims=True), s_n), sink)
        e_c = jnp.exp(s_c - mx)
        e_n = jnp.exp(s_n - mx)
        inv = 1.0 / (jnp.sum(e_c, axis=-1, keepdims=True) + e_n + jnp.exp(sink - mx))
        o = jnp.dot((e_c * inv).astype(BF16), cv_ref[...].astype(BF16), preferred_element_type=F32)
        o = o + (e_n * inv) * vn8
        yb_slabs = []
        for p in range(B_WIDTH // LANES):
            outs = []
            for half in range(2):
                h = 2 * p + half
                oh = o[h * t:(h + 1) * t]
                oh = jnp.where(lo if h // 4 == 0 else jnp.logical_not(lo), oh, 0.0)
                if half != h // 4:
                    oh = pltpu.roll(oh, HEAD_DIM, 1)
                outs.append(oh)
            yb_slabs.append(outs[0] + outs[1])
        yb_n = _rms(jnp.concatenate(yb_slabs, axis=-1), gob_ref[...])

        cat = jnp.concatenate([ya_n, yb_n], axis=-1).astype(BF16)
        h1 = x + jnp.dot(cat, wo_ref[...], preferred_element_type=F32)
        xn2 = _rms(h1, gmoe_ref[...])
        gates, sel = _route(xn2, wr_ref, br_ref)
        h1_ref[...] = h1
        xn_ref[...] = xn2.astype(BF16)
        gm_ref[...] = gates
        sm_ref[...] = sel

        nk_ref[...] = pltpu.roll(ck_ref[...], nkeys - 1, 0)
        nv_ref[...] = pltpu.roll(cv_ref[...], nkeys - 1, 0)
        for b in range(t):
            nk_ref[b * CHUNK + CHUNK - 1:b * CHUNK + CHUNK, :] = knew[b:b + 1, :]
            nv_ref[b * CHUNK + CHUNK - 1:b * CHUNK + CHUNK, :] = vnew[b:b + 1, :]


def _sample_call(x, ck, cv, sinks, gmix, win, lng, lnb, w00, b0, goa, gob, wo, gmoe, wr, br, h1, xn, gm, sm):
    t = SAMPLE_TILE
    steps = DEC_BATCH // t
    base = N_PROMPT // t
    inrow = lambda i: (i, 0)
    outrow = lambda i: (base + i, 0)
    anyspec = pl.BlockSpec(memory_space=pl.ANY)
    return pl.pallas_call(
        _sample_kernel,
        grid=(steps,),
        in_specs=[
            pl.BlockSpec(memory_space=pltpu.SMEM),
            pl.BlockSpec((t, D_MODEL), inrow),
            pl.BlockSpec((t * CHUNK, KV_WIDTH), inrow),
            pl.BlockSpec((t * CHUNK, KV_WIDTH), inrow),
            _full((1, D_MODEL)), _full((D_MODEL, IN_WIDTH)), _full((1, A_WIDTH)), _full((1, A_WIDTH)),
            _full((1, A_WIDTH)), _full((1, A_WIDTH)), _full((1, A_WIDTH)), _full((1, B_WIDTH)),
            _full((D_MODEL, D_MODEL)), _full((1, D_MODEL)), _full((D_MODEL, LANES)), _full((1, LANES)),
            anyspec, anyspec, anyspec, anyspec,
        ],
        out_specs=[
            pl.BlockSpec((t, D_MODEL), outrow),
            pl.BlockSpec((t, D_MODEL), outrow),
            pl.BlockSpec((t, LANES), outrow),
            pl.BlockSpec((t, LANES), outrow),
            pl.BlockSpec((t * CHUNK, KV_WIDTH), inrow),
            pl.BlockSpec((t * CHUNK, KV_WIDTH), inrow),
            pl.BlockSpec((t, A_WIDTH), inrow),
        ],
        out_shape=[
            jax.ShapeDtypeStruct((N_PAD, D_MODEL), F32),
            jax.ShapeDtypeStruct((N_PAD, D_MODEL), BF16),
            jax.ShapeDtypeStruct((N_PAD, LANES), F32),
            jax.ShapeDtypeStruct((N_PAD, LANES), F32),
            jax.ShapeDtypeStruct((DEC_BATCH * CHUNK, KV_WIDTH), F32),
            jax.ShapeDtypeStruct((DEC_BATCH * CHUNK, KV_WIDTH), F32),
            jax.ShapeDtypeStruct((DEC_BATCH, A_WIDTH), F32),
        ],
        input_output_aliases={16: 0, 17: 1, 18: 2, 19: 3},
        compiler_params=pltpu.CompilerParams(
            dimension_semantics=("arbitrary",), vmem_limit_bytes=VMEM_LIMIT),
        name="sample_premoe",
    )(sinks, x, ck, cv, gmix, win, lng, lnb, w00, b0, goa, gob, wo, gmoe, wr, br, h1, xn, gm, sm)


def _decode_kernel(sinks_ref, x_ref, ck_ref, cv_ref, gmix_ref, win_ref, lng_ref, lnb_ref, w00_ref, b0_ref,
                   goa_ref, gob_ref, wo_ref, gmoe_ref, wr_ref, br_ref,
                   h1_in, xn_in, gm_in, sm_in,
                   h1_ref, xn_ref, gm_ref, sm_ref, nk_ref, nv_ref, va_ref,
                   q_s, kn_s, vn_s, ya_s, yb_s):
    del h1_in, xn_in, gm_in, sm_in
    i = pl.program_id(0)
    t = DEC_TILE
    per_seq = CHUNK * 2
    ncols = t * per_seq

    @pl.when(i == 0)
    def _():
        xn = _rms(x_ref[...], gmix_ref[...]).astype(BF16)
        z = jnp.dot(xn, win_ref[...], preferred_element_type=F32)
        u = _gelu(z[:, 0:A_WIDTH])
        va = _layernorm(_gelu(z[:, A_WIDTH:2 * A_WIDTH]), lng_ref[...], lnb_ref[...])
        va_ref[...] = va
        ya_s[...] = _rms(u * (w00_ref[...] * va + b0_ref[...]), goa_ref[...])
        q_s[...] = z[:, 2 * A_WIDTH:2 * A_WIDTH + B_WIDTH]
        kn_s[...] = z[:, 2 * A_WIDTH + B_WIDTH:2 * A_WIDTH + B_WIDTH + KV_WIDTH]
        vn_s[...] = z[:, 2 * A_WIDTH + B_WIDTH + KV_WIDTH:]

    rows = pl.ds(pl.multiple_of(i * t, t), t)
    q = q_s[rows, :]
    kn = kn_s[rows, :]
    vn = vn_s[rows, :]
    qs = jnp.concatenate([q[:, h * HEAD_DIM:(h + 1) * HEAD_DIM] for h in range(B_HEADS)], axis=0)
    kn8 = jnp.concatenate([kn[:, (h // 4) * HEAD_DIM:(h // 4 + 1) * HEAD_DIM] for h in range(B_HEADS)], axis=0)
    vn8 = jnp.concatenate([vn[:, (h // 4) * HEAD_DIM:(h // 4 + 1) * HEAD_DIM] for h in range(B_HEADS)], axis=0)
    nrows = B_HEADS * t
    ridx = lax.broadcasted_iota(jnp.int32, (nrows, 1), 0)
    slope = jnp.zeros((nrows, 1), F32)
    sink = jnp.zeros((nrows, 1), F32)
    for h in range(B_HEADS):
        in_h = (ridx >= h * t) & (ridx < (h + 1) * t)
        slope = jnp.where(in_h, 2.0 ** (-(h + 1)), slope)
        sink = jnp.where(in_h, sinks_ref[h], sink)
    s_c = lax.dot_general(qs.astype(BF16), ck_ref[...].astype(BF16), (((1,), (1,)), ((), ())),
                          preferred_element_type=F32)
    rr = lax.broadcasted_iota(jnp.int32, (nrows, ncols), 0)
    col = lax.broadcasted_iota(jnp.int32, (nrows, ncols), 1)
    pos = (col // 2) % CHUNK
    own = ((col // per_seq) == (rr % t)) & ((col % 2) == (rr // (4 * t))) & (pos >= 1)
    s_c = s_c * (HEAD_DIM ** -0.5) - slope * (CHUNK - pos).astype(F32)
    s_c = jnp.where(own, s_c, NEG)
    s_n = jnp.sum(qs * kn8, axis=-1, keepdims=True) * (HEAD_DIM ** -0.5)
    mx = jnp.maximum(jnp.maximum(jnp.max(s_c, axis=-1, keepdims=True), s_n), sink)
    e_c = jnp.exp(s_c - mx)
    e_n = jnp.exp(s_n - mx)
    inv = 1.0 / (jnp.sum(e_c, axis=-1, keepdims=True) + e_n + jnp.exp(sink - mx))
    o = jnp.dot((e_c * inv).astype(BF16), cv_ref[...].astype(BF16), preferred_element_type=F32)
    o = o + (e_n * inv) * vn8
    yb_s[rows, :] = jnp.concatenate([o[h * t:(h + 1) * t] for h in range(B_HEADS)], axis=-1)

    nk_ref[...] = pltpu.roll(ck_ref[...], ncols - 2, 0)
    nv_ref[...] = pltpu.roll(cv_ref[...], ncols - 2, 0)
    for b in range(t):
        for kv in range(2):
            r = b * per_seq + per_seq - 2 + kv
            nk_ref[r:r + 1, :] = kn[b:b + 1, kv * HEAD_DIM:(kv + 1) * HEAD_DIM]
            nv_ref[r:r + 1, :] = vn[b:b + 1, kv * HEAD_DIM:(kv + 1) * HEAD_DIM]

    @pl.when(i == pl.num_programs(0) - 1)
    def _():
        yb_n = _rms(yb_s[...], gob_ref[...])
        cat = jnp.concatenate([ya_s[...], yb_n], axis=-1).astype(BF16)
        h1 = x_ref[...] + jnp.dot(cat, wo_ref[...], preferred_element_type=F32)
        xn2 = _rms(h1, gmoe_ref[...])
        gates, sel = _route(xn2, wr_ref, br_ref)
        h1_ref[...] = h1
        xn_ref[...] = xn2.astype(BF16)
        gm_ref[...] = gates
        sm_ref[...] = sel


def _decode_call(x, ck, cv, sinks, gmix, win, lng, lnb, w00, b0, goa, gob, wo, gmoe, wr, br, h1, xn, gm, sm):
    t = DEC_TILE
    per_seq = CHUNK * 2
    cache = pl.BlockSpec((t * per_seq, HEAD_DIM), lambda i: (i, 0))
    tok = lambda width: pl.BlockSpec((DEC_BATCH, width), lambda i: (N_PROMPT // DEC_BATCH, 0))
    anyspec = pl.BlockSpec(memory_space=pl.ANY)
    return pl.pallas_call(
        _decode_kernel,
        grid=(DEC_BATCH // t,),
        in_specs=[
            pl.BlockSpec(memory_space=pltpu.SMEM),
            _full((DEC_BATCH, D_MODEL)), cache, cache,
            _full((1, D_MODEL)), _full((D_MODEL, IN_WIDTH)), _full((1, A_WIDTH)), _full((1, A_WIDTH)),
            _full((1, A_WIDTH)), _full((1, A_WIDTH)), _full((1, A_WIDTH)), _full((1, B_WIDTH)),
            _full((D_MODEL, D_MODEL)), _full((1, D_MODEL)), _full((D_MODEL, LANES)), _full((1, LANES)),
            anyspec, anyspec, anyspec, anyspec,
        ],
        out_specs=[tok(D_MODEL), tok(D_MODEL), tok(LANES), tok(LANES), cache, cache, _full((DEC_BATCH, A_WIDTH))],
        out_shape=[
            jax.ShapeDtypeStruct((N_PAD, D_MODEL), F32),
            jax.ShapeDtypeStruct((N_PAD, D_MODEL), BF16),
            jax.ShapeDtypeStruct((N_PAD, LANES), F32),
            jax.ShapeDtypeStruct((N_PAD, LANES), F32),
            jax.ShapeDtypeStruct((DEC_BATCH * per_seq, HEAD_DIM), F32),
            jax.ShapeDtypeStruct((DEC_BATCH * per_seq, HEAD_DIM), F32),
            jax.ShapeDtypeStruct((DEC_BATCH, A_WIDTH), F32),
        ],
        scratch_shapes=[pltpu.VMEM((DEC_BATCH, B_WIDTH), F32), pltpu.VMEM((DEC_BATCH, KV_WIDTH), F32),
                        pltpu.VMEM((DEC_BATCH, KV_WIDTH), F32), pltpu.VMEM((DEC_BATCH, A_WIDTH), F32),
                        pltpu.VMEM((DEC_BATCH, B_WIDTH), F32)],
        input_output_aliases={16: 0, 17: 1, 18: 2, 19: 3},
        compiler_params=pltpu.CompilerParams(
            dimension_semantics=("arbitrary",), vmem_limit_bytes=VMEM_LIMIT),
        name="sample_premoe",
    )(sinks, x, ck, cv, gmix, win, lng, lnb, w00, b0, goa, gob, wo, gmoe, wr, br, h1, xn, gm, sm)


def _plan_kernel(sm_ref,
                 destm_ref, destt_ref, xbe_ref, live_ref, nxblk_ref, stab_ref, ctab_ref, astart_ref, nwin_ref,
                 base_s, pstart_s):
    ph = pl.program_id(0)
    step = pl.program_id(1)
    lane = _lane_iota((1, LANES))

    @pl.when((ph == 0) & (step == 0))
    def _():
        base_s[...] = jnp.zeros_like(base_s)

    @pl.when(ph == 0)
    def _():
        base_s[...] += jnp.sum(sm_ref[...], axis=0, keepdims=True)

    @pl.when((ph == 1) & (step == 0))
    def _():
        counts = base_s[...]
        padded = jnp.floor((counts + (DISP_CHUNK + EXP_BLOCK - 1)) * (1.0 / EXP_BLOCK)) * EXP_BLOCK
        padded = jnp.where(counts > 0.0, padded, 0.0)
        pend = padded
        for s in (1, 2, 4, 8, 16):
            pend = pend + jnp.where(lane >= s, pltpu.roll(pend, s, 1), 0.0)
        spare = (N_ROWS + lane * DISP_CHUNK).astype(F32)
        pstart_s[...] = jnp.where(counts > 0.0, pend - padded, spare)
        base_s[...] = jnp.zeros_like(base_s)
        brow = lax.broadcasted_iota(jnp.int32, (XBE_ROWS, LANES), 0).astype(F32) * EXP_BLOCK
        done = jnp.where((lane < N_EXPERTS) & (pend <= brow), 1.0, 0.0)
        be = jnp.minimum(jnp.sum(done, axis=-1, keepdims=True), N_EXPERTS - 1.0)
        xbe_ref[...] = jnp.broadcast_to(be, (XBE_ROWS, LANES)).astype(jnp.int32)
        real = jnp.clip(counts - (brow - (pend - padded)), 0.0, float(EXP_BLOCK))
        real = jnp.sum(jnp.where(lane.astype(F32) == be, real, 0.0), axis=-1, keepdims=True)
        groups = jnp.floor((real + (ROW_GROUP - 1)) * (1.0 / ROW_GROUP))
        live_ref[...] = jnp.broadcast_to(groups, (XBE_ROWS, LANES)).astype(jnp.int32)
        total = jnp.sum(jnp.where(lane == N_EXPERTS - 1, pend, 0.0), axis=-1, keepdims=True)
        nxblk_ref[...] = jnp.broadcast_to(total * (1.0 / EXP_BLOCK), (8, LANES)).astype(jnp.int32)
        stab_ref[...] = jnp.zeros_like(stab_ref)
        ctab_ref[...] = jnp.zeros_like(ctab_ref)
        astart_ref[...] = jnp.zeros_like(astart_ref)
        nwin_ref[...] = jnp.zeros_like(nwin_ref)

    @pl.when(ph == 1)
    def _():
        r = lax.broadcasted_iota(jnp.int32, (TOK_TILE, TOK_TILE), 0)
        c = lax.broadcasted_iota(jnp.int32, (TOK_TILE, TOK_TILE), 1)
        lower = jnp.where(c < r, 1.0, 0.0).astype(BF16)
        for q in range(TM // TOK_TILE):
            i = step * (TM // TOK_TILE) + q
            sel = sm_ref[q * TOK_TILE:(q + 1) * TOK_TILE, :]
            cnt = jnp.sum(sel, axis=0, keepdims=True)
            prefix = jnp.dot(lower, sel.astype(BF16), preferred_element_type=F32)
            start = pstart_s[...] + base_s[...]
            dest = jnp.where(sel > 0.0, prefix + start, -1.0)
            destm_ref[q * TOK_TILE:(q + 1) * TOK_TILE, :] = dest
            destt_ref[:, q * TOK_TILE:(q + 1) * TOK_TILE] = dest.T
            has = (cnt > 0.0) & (lane < N_EXPERTS)
            stab_ref[pl.ds(i, 1), :] = start.astype(jnp.int32)
            ctab_ref[pl.ds(i, 1), :] = jnp.where(has, cnt, 0.0).astype(jnp.int32)
            a = jnp.minimum(jnp.floor(start * (1.0 / WIN_ALIGN)) * WIN_ALIGN, float(N_ROWS - WIN))
            nw = jnp.where(has, jnp.floor((start + cnt - a + (WIN - 1)) * (1.0 / WIN)), 0.0)
            astart_ref[pl.ds(i, 1), :] = a.astype(jnp.int32)
            nwin_ref[pl.ds(i, 1), :] = nw.astype(jnp.int32)
            base_s[...] += cnt


def _plan_call(sm):
    tile = lambda ph, i: (i * ph, 0)
    tile_t = lambda ph, i: (0, i * ph)
    tab = jax.ShapeDtypeStruct((TAB_ROWS, LANES), jnp.int32)
    return pl.pallas_call(
        _plan_kernel,
        grid=(2, N_PAD // TM),
        in_specs=[pl.BlockSpec((TM, LANES), lambda ph, i: (i, 0))],
        out_specs=[
            pl.BlockSpec((TM, LANES), tile),
            pl.BlockSpec((LANES, TM), tile_t),
            _full((XBE_ROWS, LANES)), _full((XBE_ROWS, LANES)), _full((8, LANES)),
            _full((TAB_ROWS, LANES)), _full((TAB_ROWS, LANES)), _full((TAB_ROWS, LANES)), _full((TAB_ROWS, LANES)),
        ],
        out_shape=[
            jax.ShapeDtypeStruct((N_PAD, LANES), F32),
            jax.ShapeDtypeStruct((LANES, N_PAD), F32),
            jax.ShapeDtypeStruct((XBE_ROWS, LANES), jnp.int32),
            jax.ShapeDtypeStruct((XBE_ROWS, LANES), jnp.int32),
            jax.ShapeDtypeStruct((8, LANES), jnp.int32),
            tab, tab, tab, tab,
        ],
        scratch_shapes=[pltpu.VMEM((1, LANES), F32), pltpu.VMEM((1, LANES), F32)],
        compiler_params=pltpu.CompilerParams(
            dimension_semantics=("arbitrary", "arbitrary"), vmem_limit_bytes=VMEM_LIMIT),
        name="moe_plan",
    )(sm)


def _pack_rows(z):
    half = D_MODEL // 2
    lo = lax.bitcast_convert_type(z[:, :half], jnp.uint32) >> 16
    hi = lax.bitcast_convert_type(z[:, half:], jnp.uint32) & jnp.uint32(0xFFFF0000)
    return lax.bitcast_convert_type(hi | lo, jnp.int32)


def _unpack_rows(ref, rows=None):
    rows = ref.shape[0] if rows is None else rows
    flat = ref.reshape(ref.shape[0] * PACK, LANES)
    lo, hi = [], []
    for s in range(PACK):
        w = lax.bitcast_convert_type(flat[pl.ds(s, rows, stride=PACK), :], jnp.uint32)
        lo.append(lax.bitcast_convert_type(w << 16, F32))
        hi.append(lax.bitcast_convert_type(w & jnp.uint32(0xFFFF0000), F32))
    return jnp.concatenate(lo + hi, axis=-1).astype(BF16)


def _dispatch_kernel(stab_ref, ctab_ref, cmax_ref, xn_ref, destt_ref, xs_in, xs_ref,
                     stage0, stage1, stage2, sems, sem2):
    del xs_in
    i = pl.program_id(0)
    last = pl.num_programs(0) - 1
    x = xn_ref[...]
    dt = destt_ref[...]
    rio = lax.broadcasted_iota(jnp.int32, (DISP_CHUNK, 1), 0).astype(F32)

    def chunk_rows(j, stage):
        parts = []
        for e in range(N_EXPERTS):
            first = (stab_ref[i * N_EXPERTS + e] + j * DISP_CHUNK).astype(F32)
            parts.append(jnp.where(dt[e:e + 1, :] == first + rio, 1.0, 0.0).astype(BF16))
        onehot = jnp.concatenate(parts, axis=0)
        words = _pack_rows(jnp.dot(onehot, x, preferred_element_type=F32))
        for s in range(PACK):
            stage[pl.ds(s, N_EXPERTS * DISP_CHUNK, stride=PACK), :] = words[:, s * LANES:(s + 1) * LANES]

    def copy(stage, step, e, j, sem):
        first = stab_ref[step * N_EXPERTS + e] + j * DISP_CHUNK
        rows = stage.reshape(N_EXPERTS * DISP_CHUNK, PACK, LANES)
        return pltpu.make_async_copy(rows.at[pl.ds(e * DISP_CHUNK, DISP_CHUNK)],
                                     xs_ref.at[pl.ds(first, DISP_CHUNK)], sem)

    def step_body(stage, prev_stage, par):
        chunk_rows(0, stage)

        @pl.when(i > 0)
        def _():
            for e in range(N_EXPERTS):
                copy(prev_stage, i - 1, e, 0, sems.at[1 - par, e]).wait()

        for e in range(N_EXPERTS):
            copy(stage, i, e, 0, sems.at[par, e]).start()

        @pl.when(i == last)
        def _():
            for e in range(N_EXPERTS):
                copy(stage, i, e, 0, sems.at[par, e]).wait()

    @pl.when(i % 2 == 0)
    def _():
        step_body(stage0, stage1, 0)

    @pl.when(i % 2 == 1)
    def _():
        step_body(stage1, stage0, 1)

    for j in range(1, TOK_TILE // DISP_CHUNK):

        @pl.when(cmax_ref[i] > j * DISP_CHUNK)
        def _(j=j):
            chunk_rows(j, stage2)
            for e in range(N_EXPERTS):

                @pl.when(ctab_ref[i * N_EXPERTS + e] > j * DISP_CHUNK)
                def _(e=e):
                    cp = copy(stage2, i, e, j, sem2)
                    cp.start()
                    cp.wait()


def _dispatch_call(stab, ctab, cmax, xn, destt, xs_zero):
    stage = pltpu.VMEM((N_EXPERTS * DISP_CHUNK * PACK, LANES), jnp.int32)
    grid_spec = pltpu.PrefetchScalarGridSpec(
        num_scalar_prefetch=3,
        grid=(N_PAD_TILES,),
        in_specs=[
            pl.BlockSpec((TOK_TILE, D_MODEL), lambda i, *_: (i, 0)),
            pl.BlockSpec((N_EXPERTS, TOK_TILE), lambda i, *_: (0, i)),
            pl.BlockSpec(memory_space=pl.ANY),
        ],
        out_specs=pl.BlockSpec(memory_space=pl.ANY),
        scratch_shapes=[stage, stage, stage, pltpu.SemaphoreType.DMA((2, N_EXPERTS)),
                        pltpu.SemaphoreType.DMA],
    )
    return pl.pallas_call(
        _dispatch_kernel,
        grid_spec=grid_spec,
        out_shape=jax.ShapeDtypeStruct((XS_ROWS, PACK, LANES), jnp.int32),
        input_output_aliases={5: 0},
        compiler_params=pltpu.CompilerParams(
            dimension_semantics=("arbitrary",), vmem_limit_bytes=VMEM_LIMIT),
        name="moe_dispatch",
    )(stab, ctab, cmax, xn, destt, xs_zero)


def _expert_kernel(blke_ref, nblk_ref, live_ref, xs_ref, wgu_hbm, bgu_ref, wdn_hbm, bdn_ref,
                   ys_ref, wgu_f, wdn_f, wgu_s, wdn_s, sems):
    b = pl.program_id(0)
    used = b < nblk_ref[0]
    prev = blke_ref[jnp.maximum(b - 1, 0)]
    fresh = used & ((b == 0) | (blke_ref[b] != prev))

    def fetch(e):
        return (pltpu.make_async_copy(wgu_hbm.at[e], wgu_f, sems.at[0]),
                pltpu.make_async_copy(wdn_hbm.at[e], wdn_f, sems.at[1]))

    @pl.when(b == 0)
    def _():
        for cp in fetch(blke_ref[0]):
            cp.start()

    @pl.when(fresh)
    def _():
        for cp in fetch(blke_ref[b]):
            cp.wait()
        wgu_s[...] = wgu_f[...].astype(BF16)
        wdn_s[...] = wdn_f[...].astype(BF16)

        nxt = lax.while_loop(lambda p: (p < nblk_ref[0]) & (blke_ref[jnp.minimum(p, N_XBLOCKS - 1)] == blke_ref[b]),
                             lambda p: p + 1, b + 1)

        @pl.when(nxt < nblk_ref[0])
        def _():
            for cp in fetch(blke_ref[jnp.minimum(nxt, N_XBLOCKS - 1)]):
                cp.start()

    for groups in range(1, EXP_BLOCK // ROW_GROUP + 1):
        rows = groups * ROW_GROUP

        @pl.when(used & (live_ref[b] == groups))
        def _(rows=rows):
            hid = jnp.dot(_unpack_rows(xs_ref, rows), wgu_s[...], preferred_element_type=F32) + bgu_ref[...]
            gate = jnp.minimum(hid[:, :D_FF], SWIGLU_LIMIT)
            up = jnp.clip(hid[:, D_FF:], -SWIGLU_LIMIT, SWIGLU_LIMIT)
            act = (up + 1.0) * gate * jax.nn.sigmoid(SWIGLU_ALPHA * gate)
            y = jnp.dot(act.astype(BF16), wdn_s[...], preferred_element_type=F32) + bdn_ref[...]
            ys_ref[0:rows, :] = y.astype(BF16)
            if rows < EXP_BLOCK:
                ys_ref[rows:, :] = jnp.zeros((EXP_BLOCK - rows, D_MODEL), BF16)

    @pl.when(jnp.logical_not(used) | (live_ref[b] == 0))
    def _():
        ys_ref[...] = jnp.zeros_like(ys_ref)


def _expert_call(blke, nblk, live, xs, wgu, bgu, wdn, bdn):
    grid_spec = pltpu.PrefetchScalarGridSpec(
        num_scalar_prefetch=3,
        grid=(N_XBLOCKS,),
        in_specs=[
            pl.BlockSpec((EXP_BLOCK, PACK, LANES), lambda b, be, *_: (b, 0, 0)),
            pl.BlockSpec(memory_space=pl.ANY),
            pl.BlockSpec((None, 1, 2 * D_FF), lambda b, be, *_: (be[b], 0, 0)),
            pl.BlockSpec(memory_space=pl.ANY),
            pl.BlockSpec((None, 1, D_MODEL), lambda b, be, *_: (be[b], 0, 0)),
        ],
        out_specs=pl.BlockSpec((EXP_BLOCK, D_MODEL), lambda b, be, *_: (b, 0)),
        scratch_shapes=[pltpu.VMEM((D_MODEL, 2 * D_FF), F32), pltpu.VMEM((D_FF, D_MODEL), F32),
                        pltpu.VMEM((D_MODEL, 2 * D_FF), BF16), pltpu.VMEM((D_FF, D_MODEL), BF16),
                        pltpu.SemaphoreType.DMA((2,))],
    )
    return pl.pallas_call(
        _expert_kernel,
        grid_spec=grid_spec,
        out_shape=jax.ShapeDtypeStruct((N_ROWS, D_MODEL), BF16),
        compiler_params=pltpu.CompilerParams(
            dimension_semantics=("arbitrary",), vmem_limit_bytes=VMEM_LIMIT),
        name="moe_experts",
    )(blke, nblk, live, xs, wgu, bgu, wdn, bdn)


def _combine_kernel(*refs):
    astart_ref, nwin_ref, over_ref = refs[0:3]
    win_refs = refs[3:3 + N_EXPERTS]
    (destm_ref, gm_ref, h1_ref, plep_ref, ples_ref, gple_ref, wpg_ref, wpp_ref, gfin_ref, ys_any,
     yp_ref, ysm_ref, moe_s, tmp_s, sem) = refs[3 + N_EXPERTS:]
    i = pl.program_id(0)
    dest = destm_ref[...]
    gates = gm_ref[...]
    lane = _lane_iota((TOK_TILE, LANES))
    lane_f = lane.astype(F32)
    lo = lane < WIN
    moe = jnp.zeros((TOK_TILE, D_MODEL), F32)
    group = 4
    for g0 in range(0, N_EXPERTS, group):
        g_hi, g_lo = [], []
        for p in range(group // 2):
            e0 = g0 + 2 * p
            a0 = astart_ref[i * N_EXPERTS + e0].astype(F32)
            a1 = astart_ref[i * N_EXPERTS + e0 + 1].astype(F32)
            rowid = jnp.where(lo, a0 + lane_f, a1 + lane_f - WIN)
            dcol = jnp.where(lo, dest[:, e0:e0 + 1], dest[:, e0 + 1:e0 + 2])
            gcol = jnp.where(lo, gates[:, e0:e0 + 1], gates[:, e0 + 1:e0 + 2])
            gsel = jnp.where(dcol == rowid, gcol, 0.0)
            hi = gsel.astype(BF16)
            g_hi.append(hi)
            g_lo.append((gsel - hi.astype(F32)).astype(BF16))
        ywin = jnp.concatenate([win_refs[g0 + q][...] for q in range(group)], axis=0)
        both = jnp.concatenate([jnp.concatenate(g_hi, axis=-1), jnp.concatenate(g_lo, axis=-1)], axis=0)
        r = jnp.dot(both, ywin, preferred_element_type=F32)
        moe = moe + r[:TOK_TILE] + r[TOK_TILE:]
    mrows = pl.ds(pl.multiple_of((i % TAIL_TILES) * TOK_TILE, TOK_TILE), TOK_TILE)
    moe_s[mrows, :] = moe

    @pl.when(over_ref[i] > 0)
    def _():
        tmp_s[...] = jnp.zeros_like(tmp_s)

        def per_expert(e, carry):
            a = astart_ref[i * N_EXPERTS + e]
            dcol = jnp.sum(jnp.where(lane == e, dest, 0.0), axis=-1, keepdims=True)
            gcol = jnp.sum(jnp.where(lane == e, gates, 0.0), axis=-1, keepdims=True)

            def per_window(w, carry2):
                first = a + w * WIN
                start = pl.multiple_of(jnp.minimum(first, N_ROWS - WIN), WIN_ALIGN)
                cp = pltpu.make_async_copy(ys_any.at[pl.ds(start, WIN)], tmp_s.at[pl.ds(0, WIN)], sem)
                cp.start()
                cp.wait()
                hit = lo & (dcol == start.astype(F32) + lane_f) & (dcol >= first.astype(F32))
                gsel = jnp.where(hit, gcol, 0.0)
                hi = gsel.astype(BF16)
                rest = (gsel - hi.astype(F32)).astype(BF16)
                moe_s[mrows, :] += (jnp.dot(hi, tmp_s[...], preferred_element_type=F32)
                                    + jnp.dot(rest, tmp_s[...], preferred_element_type=F32))
                return carry2

            return lax.fori_loop(1, nwin_ref[i * N_EXPERTS + e], per_window, carry)

        lax.fori_loop(0, N_EXPERTS, per_expert, 0)

    def tail(h1, moe_rows, ple):
        h2 = h1 + moe_rows
        hn = _rms(h2, gple_ref[...]).astype(BF16)
        gate = jax.nn.sigmoid(jnp.dot(hn, wpg_ref[...], preferred_element_type=F32))
        proj = jnp.dot(ple.astype(BF16), wpp_ref[...], preferred_element_type=F32)
        return _rms(h2 + gate * proj, gfin_ref[...])

    is_sample = i == N_TILES - 1

    @pl.when((i % TAIL_TILES == TAIL_TILES - 1) & jnp.logical_not(is_sample))
    def _():
        yp_ref[...] = tail(h1_ref[...], moe_s[...], plep_ref[...])

    @pl.when(is_sample)
    def _():
        ysm_ref[...] = tail(h1_ref[0:TOK_TILE, :], moe_s[0:TOK_TILE, :], ples_ref[...])


def _combine_call(astart, nwin, over, ys, destm, gm, h1, plep, ples, gple, wpg, wpp, gfin):
    rows = TAIL_TILES * TOK_TILE
    last = N_PROMPT // rows - 1

    def win_spec(e):
        return pl.BlockSpec((pl.Element(WIN), pl.Element(D_MODEL)),
                            lambda i, a, nw, ov, e=e: (pl.multiple_of(a[i * N_EXPERTS + e], WIN_ALIGN), 0))

    grid_spec = pltpu.PrefetchScalarGridSpec(
        num_scalar_prefetch=3,
        grid=(N_TILES,),
        in_specs=[win_spec(e) for e in range(N_EXPERTS)] + [
            pl.BlockSpec((TOK_TILE, LANES), lambda i, *_: (i, 0)),
            pl.BlockSpec((TOK_TILE, LANES), lambda i, *_: (i, 0)),
            pl.BlockSpec((rows, D_MODEL), lambda i, *_: (i // TAIL_TILES, 0)),
            pl.BlockSpec((rows, PLE_DIM), lambda i, *_: (jnp.minimum(i // TAIL_TILES, last), 0)),
            pl.BlockSpec((TOK_TILE, PLE_DIM), lambda i, *_: (0, 0)),
            pl.BlockSpec((1, D_MODEL), lambda i, *_: (0, 0)),
            pl.BlockSpec((D_MODEL, D_MODEL), lambda i, *_: (0, 0)),
            pl.BlockSpec((PLE_DIM, D_MODEL), lambda i, *_: (0, 0)),
            pl.BlockSpec((1, D_MODEL), lambda i, *_: (0, 0)),
            pl.BlockSpec(memory_space=pl.ANY),
        ],
        out_specs=[
            pl.BlockSpec((rows, D_MODEL), lambda i, *_: (jnp.minimum(i // TAIL_TILES, last), 0)),
            pl.BlockSpec((TOK_TILE, D_MODEL), lambda i, *_: (0, 0)),
        ],
        scratch_shapes=[pltpu.VMEM((rows, D_MODEL), F32), pltpu.VMEM((2 * WIN, D_MODEL), BF16),
                        pltpu.SemaphoreType.DMA],
    )
    return pl.pallas_call(
        _combine_kernel,
        grid_spec=grid_spec,
        out_shape=[jax.ShapeDtypeStruct((N_PROMPT, D_MODEL), F32),
                   jax.ShapeDtypeStruct((DEC_BATCH, D_MODEL), F32)],
        compiler_params=pltpu.CompilerParams(
            dimension_semantics=("arbitrary",), vmem_limit_bytes=VMEM_LIMIT),
        name="moe_combine_tail",
    )(astart, nwin, over, *([ys] * N_EXPERTS), destm, gm, h1, plep, ples, gple, wpg, wpp, gfin, ys)


def _gather_kernel(*refs):
    astart_ref, nwin_ref, over_ref = refs[0:3]
    win_refs = refs[3:3 + N_EXPERTS]
    (destm_ref, gm_ref, h1_ref, plep_ref, ples_ref, gple_ref, wpg_ref, wpp_ref, gfin_ref, ys_any,
     yp_ref, ysm_ref, moe_s, tmp_s, sem) = refs[3 + N_EXPERTS:]
    i = pl.program_id(0)
    slot = i % TAIL_TILES
    dest = destm_ref[...]
    gates = gm_ref[...]
    lane = _lane_iota((TOK_TILE, LANES))
    lane_f = lane.astype(F32)
    per_slab = LANES // CWIN
    within = (lane % CWIN).astype(F32)
    group = 2 * per_slab

    def split(gsel):
        hi = gsel.astype(BF16)
        return hi, (gsel - hi.astype(F32)).astype(BF16)

    moe = jnp.zeros((TOK_TILE, D_MODEL), F32)
    for g0 in range(0, N_EXPERTS, group):
        his, los = [], []
        for sl in range(2):
            rowid = jnp.zeros((TOK_TILE, LANES), F32)
            dcol = jnp.zeros((TOK_TILE, LANES), F32)
            gcol = jnp.zeros((TOK_TILE, LANES), F32)
            for q in range(per_slab):
                e = g0 + sl * per_slab + q
                mine = (lane >= q * CWIN) & (lane < (q + 1) * CWIN)
                rowid = jnp.where(mine, astart_ref[i * N_EXPERTS + e].astype(F32) + within, rowid)
                dcol = jnp.where(mine, dest[:, e:e + 1], dcol)
                gcol = jnp.where(mine, gates[:, e:e + 1], gcol)
            hi, lo = split(jnp.where(dcol == rowid, gcol, 0.0))
            his.append(hi)
            los.append(lo)
        ywin = jnp.concatenate([_unpack_rows(win_refs[g0 + q]) for q in range(group)], axis=0)
        both = jnp.concatenate([jnp.concatenate(his, axis=-1), jnp.concatenate(los, axis=-1)], axis=0)
        r = jnp.dot(both, ywin, preferred_element_type=F32)
        moe = moe + r[:TOK_TILE] + r[TOK_TILE:]
    mrows = pl.ds(pl.multiple_of(slot * TOK_TILE, TOK_TILE), TOK_TILE)
    moe_s[mrows, :] = moe

    @pl.when(over_ref[i] > 0)
    def _():
        def per_expert(e, carry):
            a = astart_ref[i * N_EXPERTS + e]
            dcol = jnp.sum(jnp.where(lane == e, dest, 0.0), axis=-1, keepdims=True)
            gcol = jnp.sum(jnp.where(lane == e, gates, 0.0), axis=-1, keepdims=True)

            def per_window(w, carry2):
                first = a + w * CWIN
                start = jnp.minimum(first, N_ROWS - CWIN)
                cp = pltpu.make_async_copy(ys_any.at[pl.ds(start, CWIN)], tmp_s, sem)
                cp.start()
                cp.wait()
                hit = (lane < CWIN) & (dcol == start.astype(F32) + lane_f) & (dcol >= first.astype(F32))
                hi, lo = split(jnp.where(hit, gcol, 0.0))
                rows = jnp.concatenate([_unpack_rows(tmp_s), jnp.zeros((LANES - CWIN, D_MODEL), BF16)], axis=0)
                moe_s[mrows, :] += (jnp.dot(hi, rows, preferred_element_type=F32)
                                    + jnp.dot(lo, rows, preferred_element_type=F32))
                return carry2

            return lax.fori_loop(1, nwin_ref[i * N_EXPERTS + e], per_window, carry)

        lax.fori_loop(0, N_EXPERTS, per_expert, 0)

    def tail(h1, moe_rows, ple):
        h2 = h1 + moe_rows
        hn = _rms(h2, gple_ref[...]).astype(BF16)
        gate = jax.nn.sigmoid(jnp.dot(hn, wpg_ref[...], preferred_element_type=F32))
        proj = jnp.dot(ple.astype(BF16), wpp_ref[...], preferred_element_type=F32)
        return _rms(h2 + gate * proj, gfin_ref[...])

    is_sample = i == N_TILES - 1

    @pl.when((slot == TAIL_TILES - 1) & jnp.logical_not(is_sample))
    def _():
        yp_ref[...] = tail(h1_ref[...], moe_s[...], plep_ref[...])

    @pl.when(is_sample)
    def _():
        ysm_ref[...] = tail(h1_ref[0:TOK_TILE, :], moe_s[0:TOK_TILE, :], ples_ref[...])


def _gather_call(astart, nwin, over, ys, destm, gm, h1, plep, ples, gple, wpg, wpp, gfin):
    rows = TAIL_TILES * TOK_TILE
    last = N_PROMPT // rows - 1

    def win_spec(e):
        return pl.BlockSpec((pl.Element(CWIN), pl.Element(PACK), pl.Element(LANES)),
                            lambda i, a, nw, ov, e=e: (a[i * N_EXPERTS + e], 0, 0))

    grid_spec = pltpu.PrefetchScalarGridSpec(
        num_scalar_prefetch=3,
        grid=(N_TILES,),
        in_specs=[win_spec(e) for e in range(N_EXPERTS)] + [
            pl.BlockSpec((TOK_TILE, LANES), lambda i, *_: (i, 0)),
            pl.BlockSpec((TOK_TILE, LANES), lambda i, *_: (i, 0)),
            pl.BlockSpec((rows, D_MODEL), lambda i, *_: (i // TAIL_TILES, 0)),
            pl.BlockSpec((rows, PLE_DIM), lambda i, *_: (jnp.minimum(i // TAIL_TILES, last), 0)),
            pl.BlockSpec((TOK_TILE, PLE_DIM), lambda i, *_: (0, 0)),
            pl.BlockSpec((1, D_MODEL), lambda i, *_: (0, 0)),
            pl.BlockSpec((D_MODEL, D_MODEL), lambda i, *_: (0, 0)),
            pl.BlockSpec((PLE_DIM, D_MODEL), lambda i, *_: (0, 0)),
            pl.BlockSpec((1, D_MODEL), lambda i, *_: (0, 0)),
            pl.BlockSpec(memory_space=pl.ANY),
        ],
        out_specs=[
            pl.BlockSpec((rows, D_MODEL), lambda i, *_: (jnp.minimum(i // TAIL_TILES, last), 0)),
            pl.BlockSpec((TOK_TILE, D_MODEL), lambda i, *_: (0, 0)),
        ],
        scratch_shapes=[pltpu.VMEM((rows, D_MODEL), F32), pltpu.VMEM((CWIN, PACK, LANES), jnp.int32),
                        pltpu.SemaphoreType.DMA],
    )
    return pl.pallas_call(
        _gather_kernel,
        grid_spec=grid_spec,
        out_shape=[jax.ShapeDtypeStruct((N_PROMPT, D_MODEL), F32),
                   jax.ShapeDtypeStruct((DEC_BATCH, D_MODEL), F32)],
        compiler_params=pltpu.CompilerParams(
            dimension_semantics=("arbitrary",), vmem_limit_bytes=VMEM_LIMIT),
        name="moe_combine_tail",
    )(astart, nwin, over, *([ys] * N_EXPERTS), destm, gm, h1, plep, ples, gple, wpg, wpp, gfin, ys)


def kernel(x_prompt, x_sample, cache_swa_k, cache_swa_v, p_prompt, p_sample, g_mix, w_in, ln_v_g, ln_v_b,
           w_sp, b_sp, sinks, g_out_a, g_out_b, w_o, g_moe, w_router, b_router, w_gu, b_gu, w_dn, b_dn,
           g_ple, w_ple_gate, w_ple_proj, g_final):
    l = 0
    row = lambda v: v.reshape(1, -1)
    win = w_in[l].astype(BF16)
    wo = w_o[l].astype(BF16)
    tril = jnp.tril(jnp.ones((CHUNK, CHUNK), dtype=bool))
    wsp = jnp.where(tril, w_sp[l], 0.0).astype(BF16)
    bsp = jnp.repeat(b_sp[l].T, HEAD_DIM, axis=1)
    w00 = row(jnp.repeat(w_sp[l][:, 0, 0], HEAD_DIM))
    b0 = row(jnp.repeat(b_sp[l][:, 0], HEAD_DIM))
    wr_hi = w_router[l].astype(BF16)
    wr_lo = (w_router[l] - wr_hi.astype(F32)).astype(BF16)
    wr = jnp.concatenate([wr_hi, wr_lo, jnp.zeros((D_MODEL, LANES - 2 * N_EXPERTS), BF16)], axis=1)
    br = row(jnp.concatenate([b_router[l], jnp.zeros((LANES - N_EXPERTS,), F32)]))
    common = (row(g_mix[l]), win, row(ln_v_g[l]), row(ln_v_b[l]))
    tail = (row(g_out_a[l]), row(g_out_b[l]), wo, row(g_moe[l]), wr, br)

    h1, xn, gm, sm, k_p, v_p = _prompt_call(
        x_prompt.reshape(N_PROMPT, D_MODEL), sinks[l], *common, wsp, bsp, *tail)
    ck = cache_swa_k[l].reshape(DEC_BATCH * CHUNK * 2, HEAD_DIM)
    cv = cache_swa_v[l].reshape(DEC_BATCH * CHUNK * 2, HEAD_DIM)
    h1, xn, gm, sm, k_s, v_s, va_s = _decode_call(
        x_sample.reshape(DEC_BATCH, D_MODEL), ck, cv, sinks[l], *common, w00, b0, *tail, h1, xn, gm, sm)

    destm, destt, xbe, live, nxblk, stab, ctab, astart, nwin = _plan_call(sm)
    flat = lambda tab, n: tab[:n, :N_EXPERTS].reshape(-1)
    astart1 = flat(astart, N_TILES)
    nwin2 = nwin[:N_TILES, :N_EXPERTS]
    over1 = (jnp.max(nwin2, axis=1) > 1).astype(jnp.int32)
    nwin1 = nwin2.reshape(-1)
    cmax1 = jnp.max(ctab[:N_PAD_TILES, :N_EXPERTS], axis=1)

    xs = _dispatch_call(flat(stab, N_PAD_TILES), flat(ctab, N_PAD_TILES), cmax1, xn, destt[:N_EXPERTS],
                        jnp.zeros((XS_ROWS, PACK, LANES), jnp.int32))
    ys = _expert_call(xbe[:N_XBLOCKS, 0], nxblk[0, :1], live[:N_XBLOCKS, 0], xs, w_gu[l], b_gu[l].reshape(N_EXPERTS, 1, 2 * D_FF),
                      w_dn[l], b_dn[l].reshape(N_EXPERTS, 1, D_MODEL))
    y_p, y_s = _combine_call(
        astart1, nwin1, over1, ys, destm, gm, h1,
        p_prompt[l].reshape(N_PROMPT, PLE_DIM), p_sample[l].reshape(DEC_BATCH, PLE_DIM),
        row(g_ple[l]), w_ple_gate[l].astype(BF16), w_ple_proj[l].astype(BF16), row(g_final))

    kv5 = lambda a, n: a.reshape(1, n, CHUNK, 2, HEAD_DIM)
    return (y_p.reshape(BATCH, SEQ, D_MODEL), y_s.reshape(DEC_BATCH, 1, D_MODEL),
            kv5(k_p, BATCH), kv5(v_p, BATCH), kv5(k_s, DEC_BATCH), kv5(v_s, DEC_BATCH),
            va_s.reshape(1, DEC_BATCH, 1, A_WIDTH))
```

```python
import math

import jax
import jax.numpy as jnp
from jax import lax
from jax.experimental import pallas as pl
from jax.experimental.pallas import tpu as pltpu

F32 = jnp.float32
BF16 = jnp.bfloat16

D_MODEL = 1024
BATCH = 4
SEQ = 4096
DEC_BATCH = 128
HEAD_DIM = 64
A_WIDTH = 512
B_WIDTH = 512
B_HEADS = 8
KV_WIDTH = 128
IN_WIDTH = 2 * A_WIDTH + B_WIDTH + 2 * KV_WIDTH
CHUNK = 128
N_EXPERTS = 32
TOP_K = 4
D_FF = 1024
SWIGLU_ALPHA = 1.702
SWIGLU_LIMIT = 7.0
PLE_DIM = 256
EPS = 1e-5

LANES = 128
ROW_GROUP = 128
EXP_BLOCK = 512
N_PROMPT = BATCH * SEQ
N_TOK = N_PROMPT + DEC_BATCH
TOK_TILE = 128
N_TILES = N_TOK // TOK_TILE
DISP_TILE = 256
TM = 512
N_PAD = ((N_TOK + TM - 1) // TM) * TM
N_PAD_TILES = N_PAD // TOK_TILE
DISP_CHUNK = 32
N_XBLOCKS = (N_TOK * TOP_K + N_EXPERTS * (DISP_CHUNK + EXP_BLOCK - 1) + EXP_BLOCK - 1) // EXP_BLOCK
N_ROWS = N_XBLOCKS * EXP_BLOCK
XS_ROWS = N_ROWS + N_EXPERTS * DISP_CHUNK
XBE_ROWS = ((N_XBLOCKS + 7) // 8) * 8
TAB_ROWS = ((N_PAD_TILES + 7) // 8) * 8
PACK = D_MODEL // 2 // LANES
WIN = 64
WIN_ALIGN = 16
CWIN = 32
TAIL_TILES = 4
SAMPLE_TILE = 32
DEC_TILE = 16
NEG = -1e30
VMEM_LIMIT = 56 * 1024 * 1024


def _rms(x, g):
    return x * lax.rsqrt(jnp.mean(x * x, axis=-1, keepdims=True) + EPS) * g


def _gelu(x):
    c = math.sqrt(2.0 / math.pi)
    return x * (0.5 * (1.0 + jnp.tanh(c * (x + 0.044715 * (x * x * x)))))


def _layernorm(x, g, b):
    mu = jnp.mean(x, axis=-1, keepdims=True)
    xc = x - mu
    return xc * lax.rsqrt(jnp.mean(xc * xc, axis=-1, keepdims=True) + EPS) * g + b


def _lane_iota(shape):
    return lax.broadcasted_iota(jnp.int32, shape, len(shape) - 1)


def _route(xn2, wr_ref, br_ref):
    m = xn2.shape[0]
    xh = xn2.astype(BF16)
    xl = (xn2 - xh.astype(F32)).astype(BF16)
    r = jnp.dot(jnp.concatenate([xh, xl], axis=0), wr_ref[...], preferred_element_type=F32)
    r = r[:m] + r[m:]
    lane = _lane_iota((m, LANES))
    lane_f = lane.astype(F32)
    logits = jnp.where(lane < N_EXPERTS, r + pltpu.roll(r, LANES - N_EXPERTS, 1) + br_ref[...], NEG)
    work = logits
    sel = jnp.zeros((m, LANES), F32)
    top = None
    z = None
    for _ in range(TOP_K):
        mx = jnp.max(work, axis=-1, keepdims=True)
        first = jnp.min(jnp.where(work == mx, lane_f, float(LANES)), axis=-1, keepdims=True)
        hit = lane_f == first
        sel = jnp.where(hit, 1.0, sel)
        work = jnp.where(hit, NEG, work)
        if top is None:
            top = mx
            z = jnp.ones_like(mx)
        else:
            z = z + jnp.exp(mx - top)
    gates = jnp.where(sel > 0.0, jnp.exp(logits - top) / z, 0.0)
    return gates, sel


def _prompt_kernel(sinks_ref, x_ref, gmix_ref, win_ref, lng_ref, lnb_ref, wsp_ref, bsp_ref,
                   goa_ref, gob_ref, wo_ref, gmoe_ref, wr_ref, br_ref,
                   h1_ref, xn_ref, gm_ref, sm_ref, k_ref, v_ref,
                   z_s, kv_s, cat_s):
    g = pl.program_id(0)
    j = g % (SEQ // TM)

    @pl.when(g >= N_PROMPT // TM)
    def _():
        h1_ref[...] = jnp.zeros_like(h1_ref)
        xn_ref[...] = jnp.zeros_like(xn_ref)
        gm_ref[...] = jnp.zeros_like(gm_ref)
        sm_ref[...] = jnp.zeros_like(sm_ref)

    @pl.when(g < N_PROMPT // TM)
    def _():
        _prompt_tile(j, sinks_ref, x_ref, gmix_ref, win_ref, lng_ref, lnb_ref, wsp_ref, bsp_ref,
                     goa_ref, gob_ref, wo_ref, gmoe_ref, wr_ref, br_ref,
                     h1_ref, xn_ref, gm_ref, sm_ref, k_ref, v_ref, z_s, kv_s, cat_s)


def _prompt_tile(j, sinks_ref, x_ref, gmix_ref, win_ref, lng_ref, lnb_ref, wsp_ref, bsp_ref,
                 goa_ref, gob_ref, wo_ref, gmoe_ref, wr_ref, br_ref,
                 h1_ref, xn_ref, gm_ref, sm_ref, k_ref, v_ref, z_s, kv_s, cat_s):
    @pl.when(j == 0)
    def _():
        kv_s[0:CHUNK, :] = jnp.zeros((CHUNK, 2 * KV_WIDTH), F32)

    xn = _rms(x_ref[...], gmix_ref[...]).astype(BF16)
    z_s[...] = jnp.dot(xn, win_ref[...], preferred_element_type=F32)
    kv_s[CHUNK:, :] = z_s[:, 2 * A_WIDTH + B_WIDTH:]

    lane = _lane_iota((CHUNK, LANES))
    lo = lane < HEAD_DIM
    lane2 = _lane_iota((2 * CHUNK, LANES))
    lo2 = lane2 < HEAD_DIM
    qi = lax.broadcasted_iota(jnp.int32, (CHUNK, CHUNK), 0)
    kc = lax.broadcasted_iota(jnp.int32, (CHUNK, CHUNK), 1)
    from_prev = kc > qi
    dist = jnp.where(from_prev, qi + CHUNK - kc, qi - kc).astype(F32)

    def chunk_body(c, carry):
        r0 = pl.multiple_of(c * CHUNK, CHUNK)
        rows = pl.ds(r0, CHUNK)
        u = _gelu(z_s[rows, 0:A_WIDTH])
        va = _layernorm(_gelu(z_s[rows, A_WIDTH:2 * A_WIDTH]), lng_ref[...], lnb_ref[...])
        vab = va.astype(BF16)
        slabs = []
        for p in range(A_WIDTH // LANES):
            slab = vab[:, p * LANES:(p + 1) * LANES]
            m0 = jnp.dot(wsp_ref[2 * p], slab, preferred_element_type=F32)
            m1 = jnp.dot(wsp_ref[2 * p + 1], slab, preferred_element_type=F32)
            slabs.append(jnp.where(lo, m0, m1))
        ya = u * (jnp.concatenate(slabs, axis=-1) + bsp_ref[...])
        ya_n = _rms(ya, goa_ref[...])
        k2 = kv_s[pl.ds(r0, 2 * CHUNK), 0:KV_WIDTH]
        v2 = kv_s[pl.ds(r0, 2 * CHUNK), KV_WIDTH:2 * KV_WIDTH]
        k2r = pltpu.roll(k2, HEAD_DIM, 1)
        v2r = pltpu.roll(v2, HEAD_DIM, 1)
        kd = (jnp.where(lo2, k2, k2r).astype(BF16), jnp.where(lo2, k2r, k2).astype(BF16))
        vd = (jnp.where(lo2, v2, v2r).astype(BF16), jnp.where(lo2, v2r, v2).astype(BF16))
        prev_ok = (j > 0) | (c > 0)
        masked = from_prev & jnp.logical_not(prev_ok)
        yb_slabs = []
        for kv in range(2):
            q0 = z_s[rows, 2 * A_WIDTH + (2 * kv) * LANES:2 * A_WIDTH + (2 * kv + 1) * LANES]
            q1 = z_s[rows, 2 * A_WIDTH + (2 * kv + 1) * LANES:2 * A_WIDTH + (2 * kv + 2) * LANES]
            lhs = jnp.concatenate([jnp.where(lo, q0, 0.0), jnp.where(lo, 0.0, q0),
                                   jnp.where(lo, q1, 0.0), jnp.where(lo, 0.0, q1)], axis=0).astype(BF16)
            s_all = lax.dot_general(lhs, kd[kv], (((1,), (1,)), ((), ())), preferred_element_type=F32)
            probs = []
            for i in range(4):
                h = 4 * kv + i
                slope = 2.0 ** (-(h + 1))
                sink = sinks_ref[h]
                sh = s_all[i * CHUNK:(i + 1) * CHUNK]
                s = jnp.where(from_prev, sh[:, :CHUNK], sh[:, CHUNK:]) * (HEAD_DIM ** -0.5) - slope * dist
                s = jnp.where(masked, NEG, s)
                mx = jnp.maximum(jnp.max(s, axis=-1, keepdims=True), sink)
                e = jnp.exp(s - mx)
                den = jnp.sum(e, axis=-1, keepdims=True) + jnp.exp(sink - mx)
                p = e * (1.0 / den)
                probs.append(jnp.concatenate([jnp.where(from_prev, p, 0.0), jnp.where(from_prev, 0.0, p)], axis=-1))
            pm = jnp.concatenate(probs, axis=0).astype(BF16)
            o = jnp.dot(pm, vd[kv], preferred_element_type=F32)
            yb_slabs.append(jnp.where(lo, o[0:CHUNK], o[CHUNK:2 * CHUNK]))
            yb_slabs.append(jnp.where(lo, o[2 * CHUNK:3 * CHUNK], o[3 * CHUNK:4 * CHUNK]))
        yb_n = _rms(jnp.concatenate(yb_slabs, axis=-1), gob_ref[...])
        cat_s[rows, 0:A_WIDTH] = ya_n.astype(BF16)
        cat_s[rows, A_WIDTH:] = yb_n.astype(BF16)
        return carry

    lax.fori_loop(0, TM // CHUNK, chunk_body, 0)

    kv_s[0:CHUNK, :] = kv_s[TM:TM + CHUNK, :]
    k_ref[...] = kv_s[TM:TM + CHUNK, 0:KV_WIDTH]
    v_ref[...] = kv_s[TM:TM + CHUNK, KV_WIDTH:]

    h1 = x_ref[...] + jnp.dot(cat_s[...], wo_ref[...], preferred_element_type=F32)
    h1_ref[...] = h1
    xn2 = _rms(h1, gmoe_ref[...])
    xn_ref[...] = xn2.astype(BF16)
    gates, sel = _route(xn2, wr_ref, br_ref)
    gm_ref[...] = gates
    sm_ref[...] = sel


def _full(shape):
    n = len(shape)
    return pl.BlockSpec(shape, lambda *_: (0,) * n)


def _prompt_call(x, sinks, gmix, win, lng, lnb, wsp, bsp, goa, gob, wo, gmoe, wr, br):
    real = N_PROMPT // TM
    row = lambda g: (g, 0)
    seq = lambda g: (jnp.minimum(g, real - 1) // (SEQ // TM), 0, 0)
    return pl.pallas_call(
        _prompt_kernel,
        grid=(N_PAD // TM,),
        in_specs=[
            pl.BlockSpec(memory_space=pltpu.SMEM),
            pl.BlockSpec((TM, D_MODEL), lambda g: (jnp.minimum(g, real - 1), 0)),
            _full((1, D_MODEL)), _full((D_MODEL, IN_WIDTH)), _full((1, A_WIDTH)), _full((1, A_WIDTH)),
            _full((8, CHUNK, CHUNK)), _full((CHUNK, A_WIDTH)), _full((1, A_WIDTH)), _full((1, B_WIDTH)),
            _full((D_MODEL, D_MODEL)), _full((1, D_MODEL)), _full((D_MODEL, LANES)), _full((1, LANES)),
        ],
        out_specs=[
            pl.BlockSpec((TM, D_MODEL), row),
            pl.BlockSpec((TM, D_MODEL), row),
            pl.BlockSpec((TM, LANES), row),
            pl.BlockSpec((TM, LANES), row),
            pl.BlockSpec((None, CHUNK, KV_WIDTH), seq),
            pl.BlockSpec((None, CHUNK, KV_WIDTH), seq),
        ],
        out_shape=[
            jax.ShapeDtypeStruct((N_PAD, D_MODEL), F32),
            jax.ShapeDtypeStruct((N_PAD, D_MODEL), BF16),
            jax.ShapeDtypeStruct((N_PAD, LANES), F32),
            jax.ShapeDtypeStruct((N_PAD, LANES), F32),
            jax.ShapeDtypeStruct((BATCH, CHUNK, KV_WIDTH), F32),
            jax.ShapeDtypeStruct((BATCH, CHUNK, KV_WIDTH), F32),
        ],
        scratch_shapes=[
            pltpu.VMEM((TM, IN_WIDTH), F32),
            pltpu.VMEM((TM + CHUNK, 2 * KV_WIDTH), F32),
            pltpu.VMEM((TM, D_MODEL), BF16),
        ],
        compiler_params=pltpu.CompilerParams(
            dimension_semantics=("arbitrary",), vmem_limit_bytes=VMEM_LIMIT),
        name="prompt_premoe",
    )(sinks, x, gmix, win, lng, lnb, wsp, bsp, goa, gob, wo, gmoe, wr, br)


def _sample_kernel(sinks_ref, x_ref, ck_ref, cv_ref, gmix_ref, win_ref, lng_ref, lnb_ref, w00_ref, b0_ref,
                   goa_ref, gob_ref, wo_ref, gmoe_ref, wr_ref, br_ref,
                   h1_in, xn_in, gm_in, sm_in,
                   h1_ref, xn_ref, gm_ref, sm_ref, nk_ref, nv_ref, va_ref):
    del h1_in, xn_in, gm_in, sm_in
    t = SAMPLE_TILE
    nkeys = t * CHUNK

    if True:
        x = x_ref[...]
        xn = _rms(x, gmix_ref[...]).astype(BF16)
        z = jnp.dot(xn, win_ref[...], preferred_element_type=F32)
        u = _gelu(z[:, 0:A_WIDTH])
        va = _layernorm(_gelu(z[:, A_WIDTH:2 * A_WIDTH]), lng_ref[...], lnb_ref[...])
        va_ref[...] = va
        ya_n = _rms(u * (w00_ref[...] * va + b0_ref[...]), goa_ref[...])

        knew = z[:, 2 * A_WIDTH + B_WIDTH:2 * A_WIDTH + B_WIDTH + KV_WIDTH]
        vnew = z[:, 2 * A_WIDTH + B_WIDTH + KV_WIDTH:]
        lane = _lane_iota((t, LANES))
        lo = lane < HEAD_DIM
        stacked = []
        for h in range(B_HEADS):
            q = z[:, 2 * A_WIDTH + (h // 2) * LANES:2 * A_WIDTH + (h // 2 + 1) * LANES]
            qh = jnp.where(lo if h % 2 == 0 else jnp.logical_not(lo), q, 0.0)
            if h % 2 != h // 4:
                qh = pltpu.roll(qh, HEAD_DIM, 1)
            stacked.append(qh)
        qs = jnp.concatenate(stacked, axis=0)
        rows = B_HEADS * t
        ridx = lax.broadcasted_iota(jnp.int32, (rows, 1), 0)
        slope = jnp.zeros((rows, 1), F32)
        sink = jnp.zeros((rows, 1), F32)
        for h in range(B_HEADS):
            in_h = (ridx >= h * t) & (ridx < (h + 1) * t)
            slope = jnp.where(in_h, 2.0 ** (-(h + 1)), slope)
            sink = jnp.where(in_h, sinks_ref[h], sink)
        s_c = lax.dot_general(qs.astype(BF16), ck_ref[...].astype(BF16), (((1,), (1,)), ((), ())),
                              preferred_element_type=F32)
        rsamp = lax.broadcasted_iota(jnp.int32, (rows, nkeys), 0) % t
        col = lax.broadcasted_iota(jnp.int32, (rows, nkeys), 1)
        pos = col % CHUNK
        own = ((col // CHUNK) == rsamp) & (pos >= 1)
        s_c = s_c * (HEAD_DIM ** -0.5) - slope * (CHUNK - pos).astype(F32)
        s_c = jnp.where(own, s_c, NEG)
        kn8 = jnp.concatenate([knew] * B_HEADS, axis=0)
        vn8 = jnp.concatenate([vnew] * B_HEADS, axis=0)
        s_n = jnp.sum(qs * kn8, axis=-1, keepdims=True) * (HEAD_DIM ** -0.5)
        mx = jnp.maximum(jnp.maximum(jnp.max(s_c, axis=-1, keepdims=True), s_n), sink)
        e_c = jnp.exp(s_c - mx)
        e_n = jnp.exp(s_n - mx)
        inv = 1.0 / (jnp.sum(e_c, axis=-1, keepdims=True) + e_n + jnp.exp(sink - mx))
        o = jnp.dot((e_c * inv).astype(BF16), cv_ref[...].astype(BF16), preferred_element_type=F32)
        o = o + (e_n * inv) * vn8
        yb_slabs = []
        for p in range(B_WIDTH // LANES):
            outs = []
            for half in range(2):
                h = 2 * p + half
                oh = o[h * t:(h + 1) * t]
                oh = jnp.where(lo if h // 4 == 0 else jnp.logical_not(lo), oh, 0.0)
                if half != h // 4:
                    oh = pltpu.roll(oh, HEAD_DIM, 1)
                outs.append(oh)
            yb_slabs.append(outs[0] + outs[1])
        yb_n = _rms(jnp.concatenate(yb_slabs, axis=-1), gob_ref[...])

        cat = jnp.concatenate([ya_n, yb_n], axis=-1).astype(BF16)
        h1 = x + jnp.dot(cat, wo_ref[...], preferred_element_type=F32)
        xn2 = _rms(h1, gmoe_ref[...])
        gates, sel = _route(xn2, wr_ref, br_ref)
        h1_ref[...] = h1
        xn_ref[...] = xn2.astype(BF16)
        gm_ref[...] = gates
        sm_ref[...] = sel

        nk_ref[...] = pltpu.roll(ck_ref[...], nkeys - 1, 0)
        nv_ref[...] = pltpu.roll(cv_ref[...], nkeys - 1, 0)
        for b in range(t):
            nk_ref[b * CHUNK + CHUNK - 1:b * CHUNK + CHUNK, :] = knew[b:b + 1, :]
            nv_ref[b * CHUNK + CHUNK - 1:b * CHUNK + CHUNK, :] = vnew[b:b + 1, :]


def _sample_call(x, ck, cv, sinks, gmix, win, lng, lnb, w00, b0, goa, gob, wo, gmoe, wr, br, h1, xn, gm, sm):
    t = SAMPLE_TILE
    steps = DEC_BATCH // t
    base = N_PROMPT // t
    inrow = lambda i: (i, 0)
    outrow = lambda i: (base + i, 0)
    anyspec = pl.BlockSpec(memory_space=pl.ANY)
    return pl.pallas_call(
        _sample_kernel,
        grid=(steps,),
        in_specs=[
            pl.BlockSpec(memory_space=pltpu.SMEM),
            pl.BlockSpec((t, D_MODEL), inrow),
            pl.BlockSpec((t * CHUNK, KV_WIDTH), inrow),
            pl.BlockSpec((t * CHUNK, KV_WIDTH), inrow),
            _full((1, D_MODEL)), _full((D_MODEL, IN_WIDTH)), _full((1, A_WIDTH)), _full((1, A_WIDTH)),
            _full((1, A_WIDTH)), _full((1, A_WIDTH)), _full((1, A_WIDTH)), _full((1, B_WIDTH)),
            _full((D_MODEL, D_MODEL)), _full((1, D_MODEL)), _full((D_MODEL, LANES)), _full((1, LANES)),
            anyspec, anyspec, anyspec, anyspec,
        ],
        out_specs=[
            pl.BlockSpec((t, D_MODEL), outrow),
            pl.BlockSpec((t, D_MODEL), outrow),
            pl.BlockSpec((t, LANES), outrow),
            pl.BlockSpec((t, LANES), outrow),
            pl.BlockSpec((t * CHUNK, KV_WIDTH), inrow),
            pl.BlockSpec((t * CHUNK, KV_WIDTH), inrow),
            pl.BlockSpec((t, A_WIDTH), inrow),
        ],
        out_shape=[
            jax.ShapeDtypeStruct((N_PAD, D_MODEL), F32),
            jax.ShapeDtypeStruct((N_PAD, D_MODEL), BF16),
            jax.ShapeDtypeStruct((N_PAD, LANES), F32),
            jax.ShapeDtypeStruct((N_PAD, LANES), F32),
            jax.ShapeDtypeStruct((DEC_BATCH * CHUNK, KV_WIDTH), F32),
            jax.ShapeDtypeStruct((DEC_BATCH * CHUNK, KV_WIDTH), F32),
            jax.ShapeDtypeStruct((DEC_BATCH, A_WIDTH), F32),
        ],
        input_output_aliases={16: 0, 17: 1, 18: 2, 19: 3},
        compiler_params=pltpu.CompilerParams(
            dimension_semantics=("arbitrary",), vmem_limit_bytes=VMEM_LIMIT),
        name="sample_premoe",
    )(sinks, x, ck, cv, gmix, win, lng, lnb, w00, b0, goa, gob, wo, gmoe, wr, br, h1, xn, gm, sm)


def _decode_kernel(sinks_ref, x_ref, ck_ref, cv_ref, gmix_ref, win_ref, lng_ref, lnb_ref, w00_ref, b0_ref,
                   goa_ref, gob_ref, wo_ref, gmoe_ref, wr_ref, br_ref,
                   h1_in, xn_in, gm_in, sm_in,
                   h1_ref, xn_ref, gm_ref, sm_ref, nk_ref, nv_ref, va_ref,
                   q_s, kn_s, vn_s, ya_s, yb_s):
    del h1_in, xn_in, gm_in, sm_in
    i = pl.program_id(0)
    t = DEC_TILE
    per_seq = CHUNK * 2
    ncols = t * per_seq

    @pl.when(i == 0)
    def _():
        xn = _rms(x_ref[...], gmix_ref[...]).astype(BF16)
        z = jnp.dot(xn, win_ref[...], preferred_element_type=F32)
        u = _gelu(z[:, 0:A_WIDTH])
        va = _layernorm(_gelu(z[:, A_WIDTH:2 * A_WIDTH]), lng_ref[...], lnb_ref[...])
        va_ref[...] = va
        ya_s[...] = _rms(u * (w00_ref[...] * va + b0_ref[...]), goa_ref[...])
        q_s[...] = z[:, 2 * A_WIDTH:2 * A_WIDTH + B_WIDTH]
        kn_s[...] = z[:, 2 * A_WIDTH + B_WIDTH:2 * A_WIDTH + B_WIDTH + KV_WIDTH]
        vn_s[...] = z[:, 2 * A_WIDTH + B_WIDTH + KV_WIDTH:]

    rows = pl.ds(pl.multiple_of(i * t, t), t)
    q = q_s[rows, :]
    kn = kn_s[rows, :]
    vn = vn_s[rows, :]
    qs = jnp.concatenate([q[:, h * HEAD_DIM:(h + 1) * HEAD_DIM] for h in range(B_HEADS)], axis=0)
    kn8 = jnp.concatenate([kn[:, (h // 4) * HEAD_DIM:(h // 4 + 1) * HEAD_DIM] for h in range(B_HEADS)], axis=0)
    vn8 = jnp.concatenate([vn[:, (h // 4) * HEAD_DIM:(h // 4 + 1) * HEAD_DIM] for h in range(B_HEADS)], axis=0)
    nrows = B_HEADS * t
    ridx = lax.broadcasted_iota(jnp.int32, (nrows, 1), 0)
    slope = jnp.zeros((nrows, 1), F32)
    sink = jnp.zeros((nrows, 1), F32)
    for h in range(B_HEADS):
        in_h = (ridx >= h * t) & (ridx < (h + 1) * t)
        slope = jnp.where(in_h, 2.0 ** (-(h + 1)), slope)
        sink = jnp.where(in_h, sinks_ref[h], sink)
    s_c = lax.dot_general(qs.astype(BF16), ck_ref[...].astype(BF16), (((1,), (1,)), ((), ())),
                          preferred_element_type=F32)
    rr = lax.broadcasted_iota(jnp.int32, (nrows, ncols), 0)
    col = lax.broadcasted_iota(jnp.int32, (nrows, ncols), 1)
    pos = (col // 2) % CHUNK
    own = ((col // per_seq) == (rr % t)) & ((col % 2) == (rr // (4 * t))) & (pos >= 1)
    s_c = s_c * (HEAD_DIM ** -0.5) - slope * (CHUNK - pos).astype(F32)
    s_c = jnp.where(own, s_c, NEG)
    s_n = jnp.sum(qs * kn8, axis=-1, keepdims=True) * (HEAD_DIM ** -0.5)
    mx = jnp.maximum(jnp.maximum(jnp.max(s_c, axis=-1, keepdims=True), s_n), sink)
    e_c = jnp.exp(s_c - mx)
    e_n = jnp.exp(s_n - mx)
    inv = 1.0 / (jnp.sum(e_c, axis=-1, keepdims=True) + e_n + jnp.exp(sink - mx))
    o = jnp.dot((e_c * inv).astype(BF16), cv_ref[...].astype(BF16), preferred_element_type=F32)
    o = o + (e_n * inv) * vn8
    yb_s[rows, :] = jnp.concatenate([o[h * t:(h + 1) * t] for h in range(B_HEADS)], axis=-1)

    nk_ref[...] = pltpu.roll(ck_ref[...], ncols - 2, 0)
    nv_ref[...] = pltpu.roll(cv_ref[...], ncols - 2, 0)
    for b in range(t):
        for kv in range(2):
            r = b * per_seq + per_seq - 2 + kv
            nk_ref[r:r + 1, :] = kn[b:b + 1, kv * HEAD_DIM:(kv + 1) * HEAD_DIM]
            nv_ref[r:r + 1, :] = vn[b:b + 1, kv * HEAD_DIM:(kv + 1) * HEAD_DIM]

    @pl.when(i == pl.num_programs(0) - 1)
    def _():
        yb_n = _rms(yb_s[...], gob_ref[...])
        cat = jnp.concatenate([ya_s[...], yb_n], axis=-1).astype(BF16)
        h1 = x_ref[...] + jnp.dot(cat, wo_ref[...], preferred_element_type=F32)
        xn2 = _rms(h1, gmoe_ref[...])
        gates, sel = _route(xn2, wr_ref, br_ref)
        h1_ref[...] = h1
        xn_ref[...] = xn2.astype(BF16)
        gm_ref[...] = gates
        sm_ref[...] = sel


def _decode_call(x, ck, cv, sinks, gmix, win, lng, lnb, w00, b0, goa, gob, wo, gmoe, wr, br, h1, xn, gm, sm):
    t = DEC_TILE
    per_seq = CHUNK * 2
    cache = pl.BlockSpec((t * per_seq, HEAD_DIM), lambda i: (i, 0))
    tok = lambda width: pl.BlockSpec((DEC_BATCH, width), lambda i: (N_PROMPT // DEC_BATCH, 0))
    anyspec = pl.BlockSpec(memory_space=pl.ANY)
    return pl.pallas_call(
        _decode_kernel,
        grid=(DEC_BATCH // t,),
        in_specs=[
            pl.BlockSpec(memory_space=pltpu.SMEM),
            _full((DEC_BATCH, D_MODEL)), cache, cache,
            _full((1, D_MODEL)), _full((D_MODEL, IN_WIDTH)), _full((1, A_WIDTH)), _full((1, A_WIDTH)),
            _full((1, A_WIDTH)), _full((1, A_WIDTH)), _full((1, A_WIDTH)), _full((1, B_WIDTH)),
            _full((D_MODEL, D_MODEL)), _full((1, D_MODEL)), _full((D_MODEL, LANES)), _full((1, LANES)),
            anyspec, anyspec, anyspec, anyspec,
        ],
        out_specs=[tok(D_MODEL), tok(D_MODEL), tok(LANES), tok(LANES), cache, cache, _full((DEC_BATCH, A_WIDTH))],
        out_shape=[
            jax.ShapeDtypeStruct((N_PAD, D_MODEL), F32),
            jax.ShapeDtypeStruct((N_PAD, D_MODEL), BF16),
            jax.ShapeDtypeStruct((N_PAD, LANES), F32),
            jax.ShapeDtypeStruct((N_PAD, LANES), F32),
            jax.ShapeDtypeStruct((DEC_BATCH * per_seq, HEAD_DIM), F32),
            jax.ShapeDtypeStruct((DEC_BATCH * per_seq, HEAD_DIM), F32),
            jax.ShapeDtypeStruct((DEC_BATCH, A_WIDTH), F32),
        ],
        scratch_shapes=[pltpu.VMEM((DEC_BATCH, B_WIDTH), F32), pltpu.VMEM((DEC_BATCH, KV_WIDTH), F32),
                        pltpu.VMEM((DEC_BATCH, KV_WIDTH), F32), pltpu.VMEM((DEC_BATCH, A_WIDTH), F32),
                        pltpu.VMEM((DEC_BATCH, B_WIDTH), F32)],
        input_output_aliases={16: 0, 17: 1, 18: 2, 19: 3},
        compiler_params=pltpu.CompilerParams(
            dimension_semantics=("arbitrary",), vmem_limit_bytes=VMEM_LIMIT),
        name="sample_premoe",
    )(sinks, x, ck, cv, gmix, win, lng, lnb, w00, b0, goa, gob, wo, gmoe, wr, br, h1, xn, gm, sm)


def _plan_kernel(sm_ref,
                 destm_ref, destt_ref, xbe_ref, live_ref, nxblk_ref, stab_ref, ctab_ref, astart_ref, nwin_ref,
                 base_s, pstart_s):
    ph = pl.program_id(0)
    step = pl.program_id(1)
    lane = _lane_iota((1, LANES))

    @pl.when((ph == 0) & (step == 0))
    def _():
        base_s[...] = jnp.zeros_like(base_s)

    @pl.when(ph == 0)
    def _():
        base_s[...] += jnp.sum(sm_ref[...], axis=0, keepdims=True)

    @pl.when((ph == 1) & (step == 0))
    def _():
        counts = base_s[...]
        padded = jnp.floor((counts + (DISP_CHUNK + EXP_BLOCK - 1)) * (1.0 / EXP_BLOCK)) * EXP_BLOCK
        padded = jnp.where(counts > 0.0, padded, 0.0)
        pend = padded
        for s in (1, 2, 4, 8, 16):
            pend = pend + jnp.where(lane >= s, pltpu.roll(pend, s, 1), 0.0)
        spare = (N_ROWS + lane * DISP_CHUNK).astype(F32)
        pstart_s[...] = jnp.where(counts > 0.0, pend - padded, spare)
        base_s[...] = jnp.zeros_like(base_s)
        brow = lax.broadcasted_iota(jnp.int32, (XBE_ROWS, LANES), 0).astype(F32) * EXP_BLOCK
        done = jnp.where((lane < N_EXPERTS) & (pend <= brow), 1.0, 0.0)
        be = jnp.minimum(jnp.sum(done, axis=-1, keepdims=True), N_EXPERTS - 1.0)
        xbe_ref[...] = jnp.broadcast_to(be, (XBE_ROWS, LANES)).astype(jnp.int32)
        real = jnp.clip(counts - (brow - (pend - padded)), 0.0, float(EXP_BLOCK))
        real = jnp.sum(jnp.where(lane.astype(F32) == be, real, 0.0), axis=-1, keepdims=True)
        groups = jnp.floor((real + (ROW_GROUP - 1)) * (1.0 / ROW_GROUP))
        live_ref[...] = jnp.broadcast_to(groups, (XBE_ROWS, LANES)).astype(jnp.int32)
        total = jnp.sum(jnp.where(lane == N_EXPERTS - 1, pend, 0.0), axis=-1, keepdims=True)
        nxblk_ref[...] = jnp.broadcast_to(total * (1.0 / EXP_BLOCK), (8, LANES)).astype(jnp.int32)
        stab_ref[...] = jnp.zeros_like(stab_ref)
        ctab_ref[...] = jnp.zeros_like(ctab_ref)
        astart_ref[...] = jnp.zeros_like(astart_ref)
        nwin_ref[...] = jnp.zeros_like(nwin_ref)

    @pl.when(ph == 1)
    def _():
        r = lax.broadcasted_iota(jnp.int32, (TOK_TILE, TOK_TILE), 0)
        c = lax.broadcasted_iota(jnp.int32, (TOK_TILE, TOK_TILE), 1)
        lower = jnp.where(c < r, 1.0, 0.0).astype(BF16)
        for q in range(TM // TOK_TILE):
            i = step * (TM // TOK_TILE) + q
            sel = sm_ref[q * TOK_TILE:(q + 1) * TOK_TILE, :]
            cnt = jnp.sum(sel, axis=0, keepdims=True)
            prefix = jnp.dot(lower, sel.astype(BF16), preferred_element_type=F32)
            start = pstart_s[...] + base_s[...]
            dest = jnp.where(sel > 0.0, prefix + start, -1.0)
            destm_ref[q * TOK_TILE:(q + 1) * TOK_TILE, :] = dest
            destt_ref[:, q * TOK_TILE:(q + 1) * TOK_TILE] = dest.T
            has = (cnt > 0.0) & (lane < N_EXPERTS)
            stab_ref[pl.ds(i, 1), :] = start.astype(jnp.int32)
            ctab_ref[pl.ds(i, 1), :] = jnp.where(has, cnt, 0.0).astype(jnp.int32)
            a = jnp.minimum(jnp.floor(start * (1.0 / WIN_ALIGN)) * WIN_ALIGN, float(N_ROWS - WIN))
            nw = jnp.where(has, jnp.floor((start + cnt - a + (WIN - 1)) * (1.0 / WIN)), 0.0)
            astart_ref[pl.ds(i, 1), :] = a.astype(jnp.int32)
            nwin_ref[pl.ds(i, 1), :] = nw.astype(jnp.int32)
            base_s[...] += cnt


def _plan_call(sm):
    tile = lambda ph, i: (i * ph, 0)
    tile_t = lambda ph, i: (0, i * ph)
    tab = jax.ShapeDtypeStruct((TAB_ROWS, LANES), jnp.int32)
    return pl.pallas_call(
        _plan_kernel,
        grid=(2, N_PAD // TM),
        in_specs=[pl.BlockSpec((TM, LANES), lambda ph, i: (i, 0))],
        out_specs=[
            pl.BlockSpec((TM, LANES), tile),
            pl.BlockSpec((LANES, TM), tile_t),
            _full((XBE_ROWS, LANES)), _full((XBE_ROWS, LANES)), _full((8, LANES)),
            _full((TAB_ROWS, LANES)), _full((TAB_ROWS, LANES)), _full((TAB_ROWS, LANES)), _full((TAB_ROWS, LANES)),
        ],
        out_shape=[
            jax.ShapeDtypeStruct((N_PAD, LANES), F32),
            jax.ShapeDtypeStruct((LANES, N_PAD), F32),
            jax.ShapeDtypeStruct((XBE_ROWS, LANES), jnp.int32),
            jax.ShapeDtypeStruct((XBE_ROWS, LANES), jnp.int32),
            jax.ShapeDtypeStruct((8, LANES), jnp.int32),
            tab, tab, tab, tab,
        ],
        scratch_shapes=[pltpu.VMEM((1, LANES), F32), pltpu.VMEM((1, LANES), F32)],
        compiler_params=pltpu.CompilerParams(
            dimension_semantics=("arbitrary", "arbitrary"), vmem_limit_bytes=VMEM_LIMIT),
        name="moe_plan",
    )(sm)


def _pack_rows(z):
    half = D_MODEL // 2
    lo = lax.bitcast_convert_type(z[:, :half], jnp.uint32) >> 16
    hi = lax.bitcast_convert_type(z[:, half:], jnp.uint32) & jnp.uint32(0xFFFF0000)
    return lax.bitcast_convert_type(hi | lo, jnp.int32)


def _unpack_rows(ref, rows=None):
    rows = ref.shape[0] if rows is None else rows
    flat = ref.reshape(ref.shape[0] * PACK, LANES)
    lo, hi = [], []
    for s in range(PACK):
        w = lax.bitcast_convert_type(flat[pl.ds(s, rows, stride=PACK), :], jnp.uint32)
        lo.append(lax.bitcast_convert_type(w << 16, F32))
        hi.append(lax.bitcast_convert_type(w & jnp.uint32(0xFFFF0000), F32))
    return jnp.concatenate(lo + hi, axis=-1).astype(BF16)


def _dispatch_kernel(stab_ref, ctab_ref, cmax_ref, xn_ref, destt_ref, xs_in, xs_ref,
                     stage0, stage1, stage2, sems, sem2):
    del xs_in
    i = pl.program_id(0)
    last = pl.num_programs(0) - 1
    x = xn_ref[...]
    dt = destt_ref[...]
    rio = lax.broadcasted_iota(jnp.int32, (DISP_CHUNK, 1), 0).astype(F32)

    def chunk_rows(j, stage):
        parts = []
        for e in range(N_EXPERTS):
            first = (stab_ref[i * N_EXPERTS + e] + j * DISP_CHUNK).astype(F32)
            parts.append(jnp.where(dt[e:e + 1, :] == first + rio, 1.0, 0.0).astype(BF16))
        onehot = jnp.concatenate(parts, axis=0)
        words = _pack_rows(jnp.dot(onehot, x, preferred_element_type=F32))
        for s in range(PACK):
            stage[pl.ds(s, N_EXPERTS * DISP_CHUNK, stride=PACK), :] = words[:, s * LANES:(s + 1) * LANES]

    def copy(stage, step, e, j, sem):
        first = stab_ref[step * N_EXPERTS + e] + j * DISP_CHUNK
        rows = stage.reshape(N_EXPERTS * DISP_CHUNK, PACK, LANES)
        return pltpu.make_async_copy(rows.at[pl.ds(e * DISP_CHUNK, DISP_CHUNK)],
                                     xs_ref.at[pl.ds(first, DISP_CHUNK)], sem)

    def step_body(stage, prev_stage, par):
        chunk_rows(0, stage)

        @pl.when(i > 0)
        def _():
            for e in range(N_EXPERTS):
                copy(prev_stage, i - 1, e, 0, sems.at[1 - par, 0]).wait()

        for e in range(N_EXPERTS):
            copy(stage, i, e, 0, sems.at[par, 0]).start()

        @pl.when(i == last)
        def _():
            for e in range(N_EXPERTS):
                copy(stage, i, e, 0, sems.at[par, 0]).wait()

    @pl.when(i % 2 == 0)
    def _():
        step_body(stage0, stage1, 0)

    @pl.when(i % 2 == 1)
    def _():
        step_body(stage1, stage0, 1)

    for j in range(1, TOK_TILE // DISP_CHUNK):

        @pl.when(cmax_ref[i] > j * DISP_CHUNK)
        def _(j=j):
            chunk_rows(j, stage2)
            for e in range(N_EXPERTS):

                @pl.when(ctab_ref[i * N_EXPERTS + e] > j * DISP_CHUNK)
                def _(e=e):
                    cp = copy(stage2, i, e, j, sem2)
                    cp.start()
                    cp.wait()


def _dispatch_call(stab, ctab, cmax, xn, destt, xs_zero):
    stage = pltpu.VMEM((N_EXPERTS * DISP_CHUNK * PACK, LANES), jnp.int32)
    grid_spec = pltpu.PrefetchScalarGridSpec(
        num_scalar_prefetch=3,
        grid=(N_PAD_TILES,),
        in_specs=[
            pl.BlockSpec((TOK_TILE, D_MODEL), lambda i, *_: (i, 0)),
            pl.BlockSpec((N_EXPERTS, TOK_TILE), lambda i, *_: (0, i)),
            pl.BlockSpec(memory_space=pl.ANY),
        ],
        out_specs=pl.BlockSpec(memory_space=pl.ANY),
        scratch_shapes=[stage, stage, stage, pltpu.SemaphoreType.DMA((2, N_EXPERTS)),
                        pltpu.SemaphoreType.DMA],
    )
    return pl.pallas_call(
        _dispatch_kernel,
        grid_spec=grid_spec,
        out_shape=jax.ShapeDtypeStruct((XS_ROWS, PACK, LANES), jnp.int32),
        input_output_aliases={5: 0},
        compiler_params=pltpu.CompilerParams(
            dimension_semantics=("arbitrary",), vmem_limit_bytes=VMEM_LIMIT),
        name="moe_dispatch",
    )(stab, ctab, cmax, xn, destt, xs_zero)


def _expert_kernel(blke_ref, nblk_ref, live_ref, xs_ref, wgu_hbm, bgu_ref, wdn_hbm, bdn_ref,
                   ys_ref, wgu_f, wdn_f, wgu_s, wdn_s, sems):
    b = pl.program_id(0)
    used = b < nblk_ref[0]
    prev = blke_ref[jnp.maximum(b - 1, 0)]
    fresh = used & ((b == 0) | (blke_ref[b] != prev))

    def fetch(e):
        return (pltpu.make_async_copy(wgu_hbm.at[e], wgu_f, sems.at[0]),
                pltpu.make_async_copy(wdn_hbm.at[e], wdn_f, sems.at[1]))

    @pl.when(b == 0)
    def _():
        for cp in fetch(blke_ref[0]):
            cp.start()

    @pl.when(fresh)
    def _():
        for cp in fetch(blke_ref[b]):
            cp.wait()
        wgu_s[...] = wgu_f[...].astype(BF16)
        wdn_s[...] = wdn_f[...].astype(BF16)

        nxt = lax.while_loop(lambda p: (p < nblk_ref[0]) & (blke_ref[jnp.minimum(p, N_XBLOCKS - 1)] == blke_ref[b]),
                             lambda p: p + 1, b + 1)

        @pl.when(nxt < nblk_ref[0])
        def _():
            for cp in fetch(blke_ref[jnp.minimum(nxt, N_XBLOCKS - 1)]):
                cp.start()

    for groups in range(1, EXP_BLOCK // ROW_GROUP + 1):
        rows = groups * ROW_GROUP

        @pl.when(used & (live_ref[b] == groups))
        def _(rows=rows):
            hid = jnp.dot(_unpack_rows(xs_ref, rows), wgu_s[...], preferred_element_type=F32) + bgu_ref[...]
            gate = jnp.minimum(hid[:, :D_FF], SWIGLU_LIMIT)
            up = jnp.clip(hid[:, D_FF:], -SWIGLU_LIMIT, SWIGLU_LIMIT)
            act = (up + 1.0) * gate * jax.nn.sigmoid(SWIGLU_ALPHA * gate)
            y = jnp.dot(act.astype(BF16), wdn_s[...], preferred_element_type=F32) + bdn_ref[...]
            ys_ref[0:rows, :] = y.astype(BF16)
            if rows < EXP_BLOCK:
                ys_ref[rows:, :] = jnp.zeros((EXP_BLOCK - rows, D_MODEL), BF16)

    @pl.when(jnp.logical_not(used) | (live_ref[b] == 0))
    def _():
        ys_ref[...] = jnp.zeros_like(ys_ref)


def _expert_call(blke, nblk, live, xs, wgu, bgu, wdn, bdn):
    grid_spec = pltpu.PrefetchScalarGridSpec(
        num_scalar_prefetch=3,
        grid=(N_XBLOCKS,),
        in_specs=[
            pl.BlockSpec((EXP_BLOCK, PACK, LANES), lambda b, be, *_: (b, 0, 0)),
            pl.BlockSpec(memory_space=pl.ANY),
            pl.BlockSpec((None, 1, 2 * D_FF), lambda b, be, *_: (be[b], 0, 0)),
            pl.BlockSpec(memory_space=pl.ANY),
            pl.BlockSpec((None, 1, D_MODEL), lambda b, be, *_: (be[b], 0, 0)),
        ],
        out_specs=pl.BlockSpec((EXP_BLOCK, D_MODEL), lambda b, be, *_: (b, 0)),
        scratch_shapes=[pltpu.VMEM((D_MODEL, 2 * D_FF), F32), pltpu.VMEM((D_FF, D_MODEL), F32),
                        pltpu.VMEM((D_MODEL, 2 * D_FF), BF16), pltpu.VMEM((D_FF, D_MODEL), BF16),
                        pltpu.SemaphoreType.DMA((2,))],
    )
    return pl.pallas_call(
        _expert_kernel,
        grid_spec=grid_spec,
        out_shape=jax.ShapeDtypeStruct((N_ROWS, D_MODEL), BF16),
        compiler_params=pltpu.CompilerParams(
            dimension_semantics=("arbitrary",), vmem_limit_bytes=VMEM_LIMIT),
        name="moe_experts",
    )(blke, nblk, live, xs, wgu, bgu, wdn, bdn)


def _combine_kernel(*refs):
    astart_ref, nwin_ref, over_ref = refs[0:3]
    win_refs = refs[3:3 + N_EXPERTS]
    (destm_ref, gm_ref, h1_ref, plep_ref, ples_ref, gple_ref, wpg_ref, wpp_ref, gfin_ref, ys_any,
     yp_ref, ysm_ref, moe_s, tmp_s, sem) = refs[3 + N_EXPERTS:]
    i = pl.program_id(0)
    dest = destm_ref[...]
    gates = gm_ref[...]
    lane = _lane_iota((TOK_TILE, LANES))
    lane_f = lane.astype(F32)
    lo = lane < WIN
    moe = jnp.zeros((TOK_TILE, D_MODEL), F32)
    group = 4
    for g0 in range(0, N_EXPERTS, group):
        g_hi, g_lo = [], []
        for p in range(group // 2):
            e0 = g0 + 2 * p
            a0 = astart_ref[i * N_EXPERTS + e0].astype(F32)
            a1 = astart_ref[i * N_EXPERTS + e0 + 1].astype(F32)
            rowid = jnp.where(lo, a0 + lane_f, a1 + lane_f - WIN)
            dcol = jnp.where(lo, dest[:, e0:e0 + 1], dest[:, e0 + 1:e0 + 2])
            gcol = jnp.where(lo, gates[:, e0:e0 + 1], gates[:, e0 + 1:e0 + 2])
            gsel = jnp.where(dcol == rowid, gcol, 0.0)
            hi = gsel.astype(BF16)
            g_hi.append(hi)
            g_lo.append((gsel - hi.astype(F32)).astype(BF16))
        ywin = jnp.concatenate([win_refs[g0 + q][...] for q in range(group)], axis=0)
        both = jnp.concatenate([jnp.concatenate(g_hi, axis=-1), jnp.concatenate(g_lo, axis=-1)], axis=0)
        r = jnp.dot(both, ywin, preferred_element_type=F32)
        moe = moe + r[:TOK_TILE] + r[TOK_TILE:]
    mrows = pl.ds(pl.multiple_of((i % TAIL_TILES) * TOK_TILE, TOK_TILE), TOK_TILE)
    moe_s[mrows, :] = moe

    @pl.when(over_ref[i] > 0)
    def _():
        tmp_s[...] = jnp.zeros_like(tmp_s)

        def per_expert(e, carry):
            a = astart_ref[i * N_EXPERTS + e]
            dcol = jnp.sum(jnp.where(lane == e, dest, 0.0), axis=-1, keepdims=True)
            gcol = jnp.sum(jnp.where(lane == e, gates, 0.0), axis=-1, keepdims=True)

            def per_window(w, carry2):
                first = a + w * WIN
                start = pl.multiple_of(jnp.minimum(first, N_ROWS - WIN), WIN_ALIGN)
                cp = pltpu.make_async_copy(ys_any.at[pl.ds(start, WIN)], tmp_s.at[pl.ds(0, WIN)], sem)
                cp.start()
                cp.wait()
                hit = lo & (dcol == start.astype(F32) + lane_f) & (dcol >= first.astype(F32))
                gsel = jnp.where(hit, gcol, 0.0)
                hi = gsel.astype(BF16)
                rest = (gsel - hi.astype(F32)).astype(BF16)
                moe_s[mrows, :] += (jnp.dot(hi, tmp_s[...], preferred_element_type=F32)
                                    + jnp.dot(rest, tmp_s[...], preferred_element_type=F32))
                return carry2

            return lax.fori_loop(1, nwin_ref[i * N_EXPERTS + e], per_window, carry)

        lax.fori_loop(0, N_EXPERTS, per_expert, 0)

    def tail(h1, moe_rows, ple):
        h2 = h1 + moe_rows
        hn = _rms(h2, gple_ref[...]).astype(BF16)
        gate = jax.nn.sigmoid(jnp.dot(hn, wpg_ref[...], preferred_element_type=F32))
        proj = jnp.dot(ple.astype(BF16), wpp_ref[...], preferred_element_type=F32)
        return _rms(h2 + gate * proj, gfin_ref[...])

    is_sample = i == N_TILES - 1

    @pl.when((i % TAIL_TILES == TAIL_TILES - 1) & jnp.logical_not(is_sample))
    def _():
        yp_ref[...] = tail(h1_ref[...], moe_s[...], plep_ref[...])

    @pl.when(is_sample)
    def _():
        ysm_ref[...] = tail(h1_ref[0:TOK_TILE, :], moe_s[0:TOK_TILE, :], ples_ref[...])


def _combine_call(astart, nwin, over, ys, destm, gm, h1, plep, ples, gple, wpg, wpp, gfin):
    rows = TAIL_TILES * TOK_TILE
    last = N_PROMPT // rows - 1

    def win_spec(e):
        return pl.BlockSpec((pl.Element(WIN), pl.Element(D_MODEL)),
                            lambda i, a, nw, ov, e=e: (pl.multiple_of(a[i * N_EXPERTS + e], WIN_ALIGN), 0))

    grid_spec = pltpu.PrefetchScalarGridSpec(
        num_scalar_prefetch=3,
        grid=(N_TILES,),
        in_specs=[win_spec(e) for e in range(N_EXPERTS)] + [
            pl.BlockSpec((TOK_TILE, LANES), lambda i, *_: (i, 0)),
            pl.BlockSpec((TOK_TILE, LANES), lambda i, *_: (i, 0)),
            pl.BlockSpec((rows, D_MODEL), lambda i, *_: (i // TAIL_TILES, 0)),
            pl.BlockSpec((rows, PLE_DIM), lambda i, *_: (jnp.minimum(i // TAIL_TILES, last), 0)),
            pl.BlockSpec((TOK_TILE, PLE_DIM), lambda i, *_: (0, 0)),
            pl.BlockSpec((1, D_MODEL), lambda i, *_: (0, 0)),
            pl.BlockSpec((D_MODEL, D_MODEL), lambda i, *_: (0, 0)),
            pl.BlockSpec((PLE_DIM, D_MODEL), lambda i, *_: (0, 0)),
            pl.BlockSpec((1, D_MODEL), lambda i, *_: (0, 0)),
            pl.BlockSpec(memory_space=pl.ANY),
        ],
        out_specs=[
            pl.BlockSpec((rows, D_MODEL), lambda i, *_: (jnp.minimum(i // TAIL_TILES, last), 0)),
            pl.BlockSpec((TOK_TILE, D_MODEL), lambda i, *_: (0, 0)),
        ],
        scratch_shapes=[pltpu.VMEM((rows, D_MODEL), F32), pltpu.VMEM((2 * WIN, D_MODEL), BF16),
                        pltpu.SemaphoreType.DMA],
    )
    return pl.pallas_call(
        _combine_kernel,
        grid_spec=grid_spec,
        out_shape=[jax.ShapeDtypeStruct((N_PROMPT, D_MODEL), F32),
                   jax.ShapeDtypeStruct((DEC_BATCH, D_MODEL), F32)],
        compiler_params=pltpu.CompilerParams(
            dimension_semantics=("arbitrary",), vmem_limit_bytes=VMEM_LIMIT),
        name="moe_combine_tail",
    )(astart, nwin, over, *([ys] * N_EXPERTS), destm, gm, h1, plep, ples, gple, wpg, wpp, gfin, ys)


def _gather_kernel(*refs):
    astart_ref, nwin_ref, over_ref = refs[0:3]
    win_refs = refs[3:3 + N_EXPERTS]
    (destm_ref, gm_ref, h1_ref, plep_ref, ples_ref, gple_ref, wpg_ref, wpp_ref, gfin_ref, ys_any,
     yp_ref, ysm_ref, moe_s, tmp_s, sem) = refs[3 + N_EXPERTS:]
    i = pl.program_id(0)
    slot = i % TAIL_TILES
    dest = destm_ref[...]
    gates = gm_ref[...]
    lane = _lane_iota((TOK_TILE, LANES))
    lane_f = lane.astype(F32)
    per_slab = LANES // CWIN
    within = (lane % CWIN).astype(F32)
    group = 2 * per_slab

    def split(gsel):
        hi = gsel.astype(BF16)
        return hi, (gsel - hi.astype(F32)).astype(BF16)

    moe = jnp.zeros((TOK_TILE, D_MODEL), F32)
    for g0 in range(0, N_EXPERTS, group):
        his, los = [], []
        for sl in range(2):
            rowid = jnp.zeros((TOK_TILE, LANES), F32)
            dcol = jnp.zeros((TOK_TILE, LANES), F32)
            gcol = jnp.zeros((TOK_TILE, LANES), F32)
            for q in range(per_slab):
                e = g0 + sl * per_slab + q
                mine = (lane >= q * CWIN) & (lane < (q + 1) * CWIN)
                rowid = jnp.where(mine, astart_ref[i * N_EXPERTS + e].astype(F32) + within, rowid)
                dcol = jnp.where(mine, dest[:, e:e + 1], dcol)
                gcol = jnp.where(mine, gates[:, e:e + 1], gcol)
            hi, lo = split(jnp.where(dcol == rowid, gcol, 0.0))
            his.append(hi)
            los.append(lo)
        ywin = jnp.concatenate([_unpack_rows(win_refs[g0 + q]) for q in range(group)], axis=0)
        both = jnp.concatenate([jnp.concatenate(his, axis=-1), jnp.concatenate(los, axis=-1)], axis=0)
        r = jnp.dot(both, ywin, preferred_element_type=F32)
        moe = moe + r[:TOK_TILE] + r[TOK_TILE:]
    mrows = pl.ds(pl.multiple_of(slot * TOK_TILE, TOK_TILE), TOK_TILE)
    moe_s[mrows, :] = moe

    @pl.when(over_ref[i] > 0)
    def _():
        def per_expert(e, carry):
            a = astart_ref[i * N_EXPERTS + e]
            dcol = jnp.sum(jnp.where(lane == e, dest, 0.0), axis=-1, keepdims=True)
            gcol = jnp.sum(jnp.where(lane == e, gates, 0.0), axis=-1, keepdims=True)

            def per_window(w, carry2):
                first = a + w * CWIN
                start = jnp.minimum(first, N_ROWS - CWIN)
                cp = pltpu.make_async_copy(ys_any.at[pl.ds(start, CWIN)], tmp_s, sem)
                cp.start()
                cp.wait()
                hit = (lane < CWIN) & (dcol == start.astype(F32) + lane_f) & (dcol >= first.astype(F32))
                hi, lo = split(jnp.where(hit, gcol, 0.0))
                rows = jnp.concatenate([_unpack_rows(tmp_s), jnp.zeros((LANES - CWIN, D_MODEL), BF16)], axis=0)
                moe_s[mrows, :] += (jnp.dot(hi, rows, preferred_element_type=F32)
                                    + jnp.dot(lo, rows, preferred_element_type=F32))
                return carry2

            return lax.fori_loop(1, nwin_ref[i * N_EXPERTS + e], per_window, carry)

        lax.fori_loop(0, N_EXPERTS, per_expert, 0)

    def tail(h1, moe_rows, ple):
        h2 = h1 + moe_rows
        hn = _rms(h2, gple_ref[...]).astype(BF16)
        gate = jax.nn.sigmoid(jnp.dot(hn, wpg_ref[...], preferred_element_type=F32))
        proj = jnp.dot(ple.astype(BF16), wpp_ref[...], preferred_element_type=F32)
        return _rms(h2 + gate * proj, gfin_ref[...])

    is_sample = i == N_TILES - 1

    @pl.when((slot == TAIL_TILES - 1) & jnp.logical_not(is_sample))
    def _():
        yp_ref[...] = tail(h1_ref[...], moe_s[...], plep_ref[...])

    @pl.when(is_sample)
    def _():
        ysm_ref[...] = tail(h1_ref[0:TOK_TILE, :], moe_s[0:TOK_TILE, :], ples_ref[...])


def _gather_call(astart, nwin, over, ys, destm, gm, h1, plep, ples, gple, wpg, wpp, gfin):
    rows = TAIL_TILES * TOK_TILE
    last = N_PROMPT // rows - 1

    def win_spec(e):
        return pl.BlockSpec((pl.Element(CWIN), pl.Element(PACK), pl.Element(LANES)),
                            lambda i, a, nw, ov, e=e: (a[i * N_EXPERTS + e], 0, 0))

    grid_spec = pltpu.PrefetchScalarGridSpec(
        num_scalar_prefetch=3,
        grid=(N_TILES,),
        in_specs=[win_spec(e) for e in range(N_EXPERTS)] + [
            pl.BlockSpec((TOK_TILE, LANES), lambda i, *_: (i, 0)),
            pl.BlockSpec((TOK_TILE, LANES), lambda i, *_: (i, 0)),
            pl.BlockSpec((rows, D_MODEL), lambda i, *_: (i // TAIL_TILES, 0)),
            pl.BlockSpec((rows, PLE_DIM), lambda i, *_: (jnp.minimum(i // TAIL_TILES, last), 0)),
            pl.BlockSpec((TOK_TILE, PLE_DIM), lambda i, *_: (0, 0)),
            pl.BlockSpec((1, D_MODEL), lambda i, *_: (0, 0)),
            pl.BlockSpec((D_MODEL, D_MODEL), lambda i, *_: (0, 0)),
            pl.BlockSpec((PLE_DIM, D_MODEL), lambda i, *_: (0, 0)),
            pl.BlockSpec((1, D_MODEL), lambda i, *_: (0, 0)),
            pl.BlockSpec(memory_space=pl.ANY),
        ],
        out_specs=[
            pl.BlockSpec((rows, D_MODEL), lambda i, *_: (jnp.minimum(i // TAIL_TILES, last), 0)),
            pl.BlockSpec((TOK_TILE, D_MODEL), lambda i, *_: (0, 0)),
        ],
        scratch_shapes=[pltpu.VMEM((rows, D_MODEL), F32), pltpu.VMEM((CWIN, PACK, LANES), jnp.int32),
                        pltpu.SemaphoreType.DMA],
    )
    return pl.pallas_call(
        _gather_kernel,
        grid_spec=grid_spec,
        out_shape=[jax.ShapeDtypeStruct((N_PROMPT, D_MODEL), F32),
                   jax.ShapeDtypeStruct((DEC_BATCH, D_MODEL), F32)],
        compiler_params=pltpu.CompilerParams(
            dimension_semantics=("arbitrary",), vmem_limit_bytes=VMEM_LIMIT),
        name="moe_combine_tail",
    )(astart, nwin, over, *([ys] * N_EXPERTS), destm, gm, h1, plep, ples, gple, wpg, wpp, gfin, ys)


def kernel(x_prompt, x_sample, cache_swa_k, cache_swa_v, p_prompt, p_sample, g_mix, w_in, ln_v_g, ln_v_b,
           w_sp, b_sp, sinks, g_out_a, g_out_b, w_o, g_moe, w_router, b_router, w_gu, b_gu, w_dn, b_dn,
           g_ple, w_ple_gate, w_ple_proj, g_final):
    l = 0
    row = lambda v: v.reshape(1, -1)
    win = w_in[l].astype(BF16)
    wo = w_o[l].astype(BF16)
    tril = jnp.tril(jnp.ones((CHUNK, CHUNK), dtype=bool))
    wsp = jnp.where(tril, w_sp[l], 0.0).astype(BF16)
    bsp = jnp.repeat(b_sp[l].T, HEAD_DIM, axis=1)
    w00 = row(jnp.repeat(w_sp[l][:, 0, 0], HEAD_DIM))
    b0 = row(jnp.repeat(b_sp[l][:, 0], HEAD_DIM))
    wr_hi = w_router[l].astype(BF16)
    wr_lo = (w_router[l] - wr_hi.astype(F32)).astype(BF16)
    wr = jnp.concatenate([wr_hi, wr_lo, jnp.zeros((D_MODEL, LANES - 2 * N_EXPERTS), BF16)], axis=1)
    br = row(jnp.concatenate([b_router[l], jnp.zeros((LANES - N_EXPERTS,), F32)]))
    common = (row(g_mix[l]), win, row(ln_v_g[l]), row(ln_v_b[l]))
    tail = (row(g_out_a[l]), row(g_out_b[l]), wo, row(g_moe[l]), wr, br)

    h1, xn, gm, sm, k_p, v_p = _prompt_call(
        x_prompt.reshape(N_PROMPT, D_MODEL), sinks[l], *common, wsp, bsp, *tail)
    ck = cache_swa_k[l].reshape(DEC_BATCH * CHUNK * 2, HEAD_DIM)
    cv = cache_swa_v[l].reshape(DEC_BATCH * CHUNK * 2, HEAD_DIM)
    h1, xn, gm, sm, k_s, v_s, va_s = _decode_call(
        x_sample.reshape(DEC_BATCH, D_MODEL), ck, cv, sinks[l], *common, w00, b0, *tail, h1, xn, gm, sm)

    destm, destt, xbe, live, nxblk, stab, ctab, astart, nwin = _plan_call(sm)
    flat = lambda tab, n: tab[:n, :N_EXPERTS].reshape(-1)
    astart1 = flat(astart, N_TILES)
    nwin2 = nwin[:N_TILES, :N_EXPERTS]
    over1 = (jnp.max(nwin2, axis=1) > 1).astype(jnp.int32)
    nwin1 = nwin2.reshape(-1)
    cmax1 = jnp.max(ctab[:N_PAD_TILES, :N_EXPERTS], axis=1)

    xs = _dispatch_call(flat(stab, N_PAD_TILES), flat(ctab, N_PAD_TILES), cmax1, xn, destt[:N_EXPERTS],
                        jnp.zeros((XS_ROWS, PACK, LANES), jnp.int32))
    ys = _expert_call(xbe[:N_XBLOCKS, 0], nxblk[0, :1], live[:N_XBLOCKS, 0], xs, w_gu[l], b_gu[l].reshape(N_EXPERTS, 1, 2 * D_FF),
                      w_dn[l], b_dn[l].reshape(N_EXPERTS, 1, D_MODEL))
    y_p, y_s = _combine_call(
        astart1, nwin1, over1, ys, destm, gm, h1,
        p_prompt[l].reshape(N_PROMPT, PLE_DIM), p_sample[l].reshape(DEC_BATCH, PLE_DIM),
        row(g_ple[l]), w_ple_gate[l].astype(BF16), w_ple_proj[l].astype(BF16), row(g_final))

    kv5 = lambda a, n: a.reshape(1, n, CHUNK, 2, HEAD_DIM)
    return (y_p.reshape(BATCH, SEQ, D_MODEL), y_s.reshape(DEC_BATCH, 1, D_MODEL),
            kv5(k_p, BATCH), kv5(v_p, BATCH), kv5(k_s, DEC_BATCH), kv5(v_s, DEC_BATCH),
            va_s.reshape(1, DEC_BATCH, 1, A_WIDTH))
```

```python
import math

import jax
import jax.numpy as jnp
from jax import lax
from jax.experimental import pallas as pl
from jax.experimental.pallas import tpu as pltpu

F32 = jnp.float32
BF16 = jnp.bfloat16

D_MODEL = 1024
BATCH = 4
SEQ = 4096
DEC_BATCH = 128
HEAD_DIM = 64
A_WIDTH = 512
B_WIDTH = 512
B_HEADS = 8
KV_WIDTH = 128
IN_WIDTH = 2 * A_WIDTH + B_WIDTH + 2 * KV_WIDTH
CHUNK = 128
N_EXPERTS = 32
TOP_K = 4
D_FF = 1024
SWIGLU_ALPHA = 1.702
SWIGLU_LIMIT = 7.0
PLE_DIM = 256
EPS = 1e-5

LANES = 128
ROW_GROUP = 128
EXP_BLOCK = 512
N_PROMPT = BATCH * SEQ
N_TOK = N_PROMPT + DEC_BATCH
TOK_TILE = 128
N_TILES = N_TOK // TOK_TILE
DISP_TILE = 256
TM = 512
N_PAD = ((N_TOK + TM - 1) // TM) * TM
N_PAD_TILES = N_PAD // TOK_TILE
DISP_CHUNK = 32
N_XBLOCKS = (N_TOK * TOP_K + N_EXPERTS * (DISP_CHUNK + EXP_BLOCK - 1) + EXP_BLOCK - 1) // EXP_BLOCK
N_ROWS = N_XBLOCKS * EXP_BLOCK
XS_ROWS = N_ROWS + N_EXPERTS * DISP_CHUNK
XBE_ROWS = ((N_XBLOCKS + 7) // 8) * 8
TAB_ROWS = ((N_PAD_TILES + 7) // 8) * 8
PACK = D_MODEL // 2 // LANES
WIN = 64
WIN_ALIGN = 16
CWIN = 32
COMB_TILE = 256
N_CTILES = N_PROMPT // COMB_TILE + 1
TAIL_TILES = 2
SAMPLE_TILE = 32
DEC_TILE = 16
NEG = -1e30
VMEM_LIMIT = 56 * 1024 * 1024


def _rms(x, g):
    return x * lax.rsqrt(jnp.mean(x * x, axis=-1, keepdims=True) + EPS) * g


def _gelu(x):
    c = math.sqrt(2.0 / math.pi)
    return x * (0.5 * (1.0 + jnp.tanh(c * (x + 0.044715 * (x * x * x)))))


def _layernorm(x, g, b):
    mu = jnp.mean(x, axis=-1, keepdims=True)
    xc = x - mu
    return xc * lax.rsqrt(jnp.mean(xc * xc, axis=-1, keepdims=True) + EPS) * g + b


def _lane_iota(shape):
    return lax.broadcasted_iota(jnp.int32, shape, len(shape) - 1)


def _route(xn2, wr_ref, br_ref):
    m = xn2.shape[0]
    xh = xn2.astype(BF16)
    xl = (xn2 - xh.astype(F32)).astype(BF16)
    r = jnp.dot(jnp.concatenate([xh, xl], axis=0), wr_ref[...], preferred_element_type=F32)
    r = r[:m] + r[m:]
    lane = _lane_iota((m, LANES))
    lane_f = lane.astype(F32)
    logits = jnp.where(lane < N_EXPERTS, r + pltpu.roll(r, LANES - N_EXPERTS, 1) + br_ref[...], NEG)
    work = logits
    sel = jnp.zeros((m, LANES), F32)
    top = None
    z = None
    for _ in range(TOP_K):
        mx = jnp.max(work, axis=-1, keepdims=True)
        first = jnp.min(jnp.where(work == mx, lane_f, float(LANES)), axis=-1, keepdims=True)
        hit = lane_f == first
        sel = jnp.where(hit, 1.0, sel)
        work = jnp.where(hit, NEG, work)
        if top is None:
            top = mx
            z = jnp.ones_like(mx)
        else:
            z = z + jnp.exp(mx - top)
    gates = jnp.where(sel > 0.0, jnp.exp(logits - top) / z, 0.0)
    return gates, sel


def _prompt_kernel(sinks_ref, x_ref, gmix_ref, win_ref, lng_ref, lnb_ref, wsp_ref, bsp_ref,
                   goa_ref, gob_ref, wo_ref, gmoe_ref, wr_ref, br_ref,
                   h1_ref, xn_ref, gm_ref, sm_ref, k_ref, v_ref,
                   z_s, kv_s, cat_s):
    g = pl.program_id(0)
    j = g % (SEQ // TM)

    @pl.when(g >= N_PROMPT // TM)
    def _():
        h1_ref[...] = jnp.zeros_like(h1_ref)
        xn_ref[...] = jnp.zeros_like(xn_ref)
        gm_ref[...] = jnp.zeros_like(gm_ref)
        sm_ref[...] = jnp.zeros_like(sm_ref)

    @pl.when(g < N_PROMPT // TM)
    def _():
        _prompt_tile(j, sinks_ref, x_ref, gmix_ref, win_ref, lng_ref, lnb_ref, wsp_ref, bsp_ref,
                     goa_ref, gob_ref, wo_ref, gmoe_ref, wr_ref, br_ref,
                     h1_ref, xn_ref, gm_ref, sm_ref, k_ref, v_ref, z_s, kv_s, cat_s)


def _prompt_tile(j, sinks_ref, x_ref, gmix_ref, win_ref, lng_ref, lnb_ref, wsp_ref, bsp_ref,
                 goa_ref, gob_ref, wo_ref, gmoe_ref, wr_ref, br_ref,
                 h1_ref, xn_ref, gm_ref, sm_ref, k_ref, v_ref, z_s, kv_s, cat_s):
    @pl.when(j == 0)
    def _():
        kv_s[0:CHUNK, :] = jnp.zeros((CHUNK, 2 * KV_WIDTH), F32)

    xn = _rms(x_ref[...], gmix_ref[...]).astype(BF16)
    z_s[...] = jnp.dot(xn, win_ref[...], preferred_element_type=F32)
    kv_s[CHUNK:, :] = z_s[:, 2 * A_WIDTH + B_WIDTH:]

    lane = _lane_iota((CHUNK, LANES))
    lo = lane < HEAD_DIM
    lane2 = _lane_iota((2 * CHUNK, LANES))
    lo2 = lane2 < HEAD_DIM
    qi = lax.broadcasted_iota(jnp.int32, (CHUNK, CHUNK), 0)
    kc = lax.broadcasted_iota(jnp.int32, (CHUNK, CHUNK), 1)
    from_prev = kc > qi
    dist = jnp.where(from_prev, qi + CHUNK - kc, qi - kc).astype(F32)

    def chunk_body(c, carry):
        r0 = pl.multiple_of(c * CHUNK, CHUNK)
        rows = pl.ds(r0, CHUNK)
        u = _gelu(z_s[rows, 0:A_WIDTH])
        va = _layernorm(_gelu(z_s[rows, A_WIDTH:2 * A_WIDTH]), lng_ref[...], lnb_ref[...])
        vab = va.astype(BF16)
        slabs = []
        for p in range(A_WIDTH // LANES):
            slab = vab[:, p * LANES:(p + 1) * LANES]
            m0 = jnp.dot(wsp_ref[2 * p], slab, preferred_element_type=F32)
            m1 = jnp.dot(wsp_ref[2 * p + 1], slab, preferred_element_type=F32)
            slabs.append(jnp.where(lo, m0, m1))
        ya = u * (jnp.concatenate(slabs, axis=-1) + bsp_ref[...])
        ya_n = _rms(ya, goa_ref[...])
        k2 = kv_s[pl.ds(r0, 2 * CHUNK), 0:KV_WIDTH]
        v2 = kv_s[pl.ds(r0, 2 * CHUNK), KV_WIDTH:2 * KV_WIDTH]
        k2r = pltpu.roll(k2, HEAD_DIM, 1)
        v2r = pltpu.roll(v2, HEAD_DIM, 1)
        kd = (jnp.where(lo2, k2, k2r).astype(BF16), jnp.where(lo2, k2r, k2).astype(BF16))
        vd = (jnp.where(lo2, v2, v2r).astype(BF16), jnp.where(lo2, v2r, v2).astype(BF16))
        prev_ok = (j > 0) | (c > 0)
        masked = from_prev & jnp.logical_not(prev_ok)
        yb_slabs = []
        for kv in range(2):
            q0 = z_s[rows, 2 * A_WIDTH + (2 * kv) * LANES:2 * A_WIDTH + (2 * kv + 1) * LANES]
            q1 = z_s[rows, 2 * A_WIDTH + (2 * kv + 1) * LANES:2 * A_WIDTH + (2 * kv + 2) * LANES]
            lhs = jnp.concatenate([jnp.where(lo, q0, 0.0), jnp.where(lo, 0.0, q0),
                                   jnp.where(lo, q1, 0.0), jnp.where(lo, 0.0, q1)], axis=0).astype(BF16)
            s_all = lax.dot_general(lhs, kd[kv], (((1,), (1,)), ((), ())), preferred_element_type=F32)
            probs = []
            for i in range(4):
                h = 4 * kv + i
                slope = 2.0 ** (-(h + 1))
                sink = sinks_ref[h]
                sh = s_all[i * CHUNK:(i + 1) * CHUNK]
                s = jnp.where(from_prev, sh[:, :CHUNK], sh[:, CHUNK:]) * (HEAD_DIM ** -0.5) - slope * dist
                s = jnp.where(masked, NEG, s)
                mx = jnp.maximum(jnp.max(s, axis=-1, keepdims=True), sink)
                e = jnp.exp(s - mx)
                den = jnp.sum(e, axis=-1, keepdims=True) + jnp.exp(sink - mx)
                p = e * (1.0 / den)
                probs.append(jnp.concatenate([jnp.where(from_prev, p, 0.0), jnp.where(from_prev, 0.0, p)], axis=-1))
            pm = jnp.concatenate(probs, axis=0).astype(BF16)
            o = jnp.dot(pm, vd[kv], preferred_element_type=F32)
            yb_slabs.append(jnp.where(lo, o[0:CHUNK], o[CHUNK:2 * CHUNK]))
            yb_slabs.append(jnp.where(lo, o[2 * CHUNK:3 * CHUNK], o[3 * CHUNK:4 * CHUNK]))
        yb_n = _rms(jnp.concatenate(yb_slabs, axis=-1), gob_ref[...])
        cat_s[rows, 0:A_WIDTH] = ya_n.astype(BF16)
        cat_s[rows, A_WIDTH:] = yb_n.astype(BF16)
        return carry

    lax.fori_loop(0, TM // CHUNK, chunk_body, 0)

    kv_s[0:CHUNK, :] = kv_s[TM:TM + CHUNK, :]
    k_ref[...] = kv_s[TM:TM + CHUNK, 0:KV_WIDTH]
    v_ref[...] = kv_s[TM:TM + CHUNK, KV_WIDTH:]

    h1 = x_ref[...] + jnp.dot(cat_s[...], wo_ref[...], preferred_element_type=F32)
    h1_ref[...] = h1
    xn2 = _rms(h1, gmoe_ref[...])
    xn_ref[...] = xn2.astype(BF16)
    gates, sel = _route(xn2, wr_ref, br_ref)
    gm_ref[...] = gates
    sm_ref[...] = sel


def _full(shape):
    n = len(shape)
    return pl.BlockSpec(shape, lambda *_: (0,) * n)


def _prompt_call(x, sinks, gmix, win, lng, lnb, wsp, bsp, goa, gob, wo, gmoe, wr, br):
    real = N_PROMPT // TM
    row = lambda g: (g, 0)
    seq = lambda g: (jnp.minimum(g, real - 1) // (SEQ // TM), 0, 0)
    return pl.pallas_call(
        _prompt_kernel,
        grid=(N_PAD // TM,),
        in_specs=[
            pl.BlockSpec(memory_space=pltpu.SMEM),
            pl.BlockSpec((TM, D_MODEL), lambda g: (jnp.minimum(g, real - 1), 0)),
            _full((1, D_MODEL)), _full((D_MODEL, IN_WIDTH)), _full((1, A_WIDTH)), _full((1, A_WIDTH)),
            _full((8, CHUNK, CHUNK)), _full((CHUNK, A_WIDTH)), _full((1, A_WIDTH)), _full((1, B_WIDTH)),
            _full((D_MODEL, D_MODEL)), _full((1, D_MODEL)), _full((D_MODEL, LANES)), _full((1, LANES)),
        ],
        out_specs=[
            pl.BlockSpec((TM, D_MODEL), row),
            pl.BlockSpec((TM, D_MODEL), row),
            pl.BlockSpec((TM, LANES), row),
            pl.BlockSpec((TM, LANES), row),
            pl.BlockSpec((None, CHUNK, KV_WIDTH), seq),
            pl.BlockSpec((None, CHUNK, KV_WIDTH), seq),
        ],
        out_shape=[
            jax.ShapeDtypeStruct((N_PAD, D_MODEL), F32),
            jax.ShapeDtypeStruct((N_PAD, D_MODEL), BF16),
            jax.ShapeDtypeStruct((N_PAD, LANES), F32),
            jax.ShapeDtypeStruct((N_PAD, LANES), F32),
            jax.ShapeDtypeStruct((BATCH, CHUNK, KV_WIDTH), F32),
            jax.ShapeDtypeStruct((BATCH, CHUNK, KV_WIDTH), F32),
        ],
        scratch_shapes=[
            pltpu.VMEM((TM, IN_WIDTH), F32),
            pltpu.VMEM((TM + CHUNK, 2 * KV_WIDTH), F32),
            pltpu.VMEM((TM, D_MODEL), BF16),
        ],
        compiler_params=pltpu.CompilerParams(
            dimension_semantics=("arbitrary",), vmem_limit_bytes=VMEM_LIMIT),
        name="prompt_premoe",
    )(sinks, x, gmix, win, lng, lnb, wsp, bsp, goa, gob, wo, gmoe, wr, br)


def _sample_kernel(sinks_ref, x_ref, ck_ref, cv_ref, gmix_ref, win_ref, lng_ref, lnb_ref, w00_ref, b0_ref,
                   goa_ref, gob_ref, wo_ref, gmoe_ref, wr_ref, br_ref,
                   h1_in, xn_in, gm_in, sm_in,
                   h1_ref, xn_ref, gm_ref, sm_ref, nk_ref, nv_ref, va_ref):
    del h1_in, xn_in, gm_in, sm_in
    t = SAMPLE_TILE
    nkeys = t * CHUNK

    if True:
        x = x_ref[...]
        xn = _rms(x, gmix_ref[...]).astype(BF16)
        z = jnp.dot(xn, win_ref[...], preferred_element_type=F32)
        u = _gelu(z[:, 0:A_WIDTH])
        va = _layernorm(_gelu(z[:, A_WIDTH:2 * A_WIDTH]), lng_ref[...], lnb_ref[...])
        va_ref[...] = va
        ya_n = _rms(u * (w00_ref[...] * va + b0_ref[...]), goa_ref[...])

        knew = z[:, 2 * A_WIDTH + B_WIDTH:2 * A_WIDTH + B_WIDTH + KV_WIDTH]
        vnew = z[:, 2 * A_WIDTH + B_WIDTH + KV_WIDTH:]
        lane = _lane_iota((t, LANES))
        lo = lane < HEAD_DIM
        stacked = []
        for h in range(B_HEADS):
            q = z[:, 2 * A_WIDTH + (h // 2) * LANES:2 * A_WIDTH + (h // 2 + 1) * LANES]
            qh = jnp.where(lo if h % 2 == 0 else jnp.logical_not(lo), q, 0.0)
            if h % 2 != h // 4:
                qh = pltpu.roll(qh, HEAD_DIM, 1)
            stacked.append(qh)
        qs = jnp.concatenate(stacked, axis=0)
        rows = B_HEADS * t
        ridx = lax.broadcasted_iota(jnp.int32, (rows, 1), 0)
        slope = jnp.zeros((rows, 1), F32)
        sink = jnp.zeros((rows, 1), F32)
        for h in range(B_HEADS):
            in_h = (ridx >= h * t) & (ridx < (h + 1) * t)
            slope = jnp.where(in_h, 2.0 ** (-(h + 1)), slope)
            sink = jnp.where(in_h, sinks_ref[h], sink)
        s_c = lax.dot_general(qs.astype(BF16), ck_ref[...].astype(BF16), (((1,), (1,)), ((), ())),
                              preferred_element_type=F32)
        rsamp = lax.broadcasted_iota(jnp.int32, (rows, nkeys), 0) % t
        col = lax.broadcasted_iota(jnp.int32, (rows, nkeys), 1)
        pos = col % CHUNK
        own = ((col // CHUNK) == rsamp) & (pos >= 1)
        s_c = s_c * (HEAD_DIM ** -0.5) - slope * (CHUNK - pos).astype(F32)
        s_c = jnp.where(own, s_c, NEG)
        kn8 = jnp.concatenate([knew] * B_HEADS, axis=0)
        vn8 = jnp.concatenate([vnew] * B_HEADS, axis=0)
        s_n = jnp.sum(qs * kn8, axis=-1, keepdims=True) * (HEAD_DIM ** -0.5)
        mx = jnp.maximum(jnp.maximum(jnp.max(s_c, axis=-1, keepdims=True), s_n), sink)
        e_c = jnp.exp(s_c - mx)
        e_n = jnp.exp(s_n - mx)
        inv = 1.0 / (jnp.sum(e_c, axis=-1, keepdims=True) + e_n + jnp.exp(sink - mx))
        o = jnp.dot((e_c * inv).astype(BF16), cv_ref[...].astype(BF16), preferred_element_type=F32)
        o = o + (e_n * inv) * vn8
        yb_slabs = []
        for p in range(B_WIDTH // LANES):
            outs = []
            for half in range(2):
                h = 2 * p + half
                oh = o[h * t:(h + 1) * t]
                oh = jnp.where(lo if h // 4 == 0 else jnp.logical_not(lo), oh, 0.0)
                if half != h // 4:
                    oh = pltpu.roll(oh, HEAD_DIM, 1)
                outs.append(oh)
            yb_slabs.append(outs[0] + outs[1])
        yb_n = _rms(jnp.concatenate(yb_slabs, axis=-1), gob_ref[...])

        cat = jnp.concatenate([ya_n, yb_n], axis=-1).astype(BF16)
        h1 = x + jnp.dot(cat, wo_ref[...], preferred_element_type=F32)
        xn2 = _rms(h1, gmoe_ref[...])
        gates, sel = _route(xn2, wr_ref, br_ref)
        h1_ref[...] = h1
        xn_ref[...] = xn2.astype(BF16)
        gm_ref[...] = gates
        sm_ref[...] = sel

        nk_ref[...] = pltpu.roll(ck_ref[...], nkeys - 1, 0)
        nv_ref[...] = pltpu.roll(cv_ref[...], nkeys - 1, 0)
        for b in range(t):
            nk_ref[b * CHUNK + CHUNK - 1:b * CHUNK + CHUNK, :] = knew[b:b + 1, :]
            nv_ref[b * CHUNK + CHUNK - 1:b * CHUNK + CHUNK, :] = vnew[b:b + 1, :]


def _sample_call(x, ck, cv, sinks, gmix, win, lng, lnb, w00, b0, goa, gob, wo, gmoe, wr, br, h1, xn, gm, sm):
    t = SAMPLE_TILE
    steps = DEC_BATCH // t
    base = N_PROMPT // t
    inrow = lambda i: (i, 0)
    outrow = lambda i: (base + i, 0)
    anyspec = pl.BlockSpec(memory_space=pl.ANY)
    return pl.pallas_call(
        _sample_kernel,
        grid=(steps,),
        in_specs=[
            pl.BlockSpec(memory_space=pltpu.SMEM),
            pl.BlockSpec((t, D_MODEL), inrow),
            pl.BlockSpec((t * CHUNK, KV_WIDTH), inrow),
            pl.BlockSpec((t * CHUNK, KV_WIDTH), inrow),
            _full((1, D_MODEL)), _full((D_MODEL, IN_WIDTH)), _full((1, A_WIDTH)), _full((1, A_WIDTH)),
            _full((1, A_WIDTH)), _full((1, A_WIDTH)), _full((1, A_WIDTH)), _full((1, B_WIDTH)),
            _full((D_MODEL, D_MODEL)), _full((1, D_MODEL)), _full((D_MODEL, LANES)), _full((1, LANES)),
            anyspec, anyspec, anyspec, anyspec,
        ],
        out_specs=[
            pl.BlockSpec((t, D_MODEL), outrow),
            pl.BlockSpec((t, D_MODEL), outrow),
            pl.BlockSpec((t, LANES), outrow),
            pl.BlockSpec((t, LANES), outrow),
            pl.BlockSpec((t * CHUNK, KV_WIDTH), inrow),
            pl.BlockSpec((t * CHUNK, KV_WIDTH), inrow),
            pl.BlockSpec((t, A_WIDTH), inrow),
        ],
        out_shape=[
            jax.ShapeDtypeStruct((N_PAD, D_MODEL), F32),
            jax.ShapeDtypeStruct((N_PAD, D_MODEL), BF16),
            jax.ShapeDtypeStruct((N_PAD, LANES), F32),
            jax.ShapeDtypeStruct((N_PAD, LANES), F32),
            jax.ShapeDtypeStruct((DEC_BATCH * CHUNK, KV_WIDTH), F32),
            jax.ShapeDtypeStruct((DEC_BATCH * CHUNK, KV_WIDTH), F32),
            jax.ShapeDtypeStruct((DEC_BATCH, A_WIDTH), F32),
        ],
        input_output_aliases={16: 0, 17: 1, 18: 2, 19: 3},
        compiler_params=pltpu.CompilerParams(
            dimension_semantics=("arbitrary",), vmem_limit_bytes=VMEM_LIMIT),
        name="sample_premoe",
    )(sinks, x, ck, cv, gmix, win, lng, lnb, w00, b0, goa, gob, wo, gmoe, wr, br, h1, xn, gm, sm)


def _decode_kernel(sinks_ref, x_ref, ck_ref, cv_ref, gmix_ref, win_ref, lng_ref, lnb_ref, w00_ref, b0_ref,
                   goa_ref, gob_ref, wo_ref, gmoe_ref, wr_ref, br_ref,
                   h1_in, xn_in, gm_in, sm_in,
                   h1_ref, xn_ref, gm_ref, sm_ref, nk_ref, nv_ref, va_ref,
                   q_s, kn_s, vn_s, ya_s, yb_s):
    del h1_in, xn_in, gm_in, sm_in
    i = pl.program_id(0)
    t = DEC_TILE
    per_seq = CHUNK * 2
    ncols = t * per_seq

    @pl.when(i == 0)
    def _():
        xn = _rms(x_ref[...], gmix_ref[...]).astype(BF16)
        z = jnp.dot(xn, win_ref[...], preferred_element_type=F32)
        u = _gelu(z[:, 0:A_WIDTH])
        va = _layernorm(_gelu(z[:, A_WIDTH:2 * A_WIDTH]), lng_ref[...], lnb_ref[...])
        va_ref[...] = va
        ya_s[...] = _rms(u * (w00_ref[...] * va + b0_ref[...]), goa_ref[...])
        q_s[...] = z[:, 2 * A_WIDTH:2 * A_WIDTH + B_WIDTH]
        kn_s[...] = z[:, 2 * A_WIDTH + B_WIDTH:2 * A_WIDTH + B_WIDTH + KV_WIDTH]
        vn_s[...] = z[:, 2 * A_WIDTH + B_WIDTH + KV_WIDTH:]

    rows = pl.ds(pl.multiple_of(i * t, t), t)
    q = q_s[rows, :]
    kn = kn_s[rows, :]
    vn = vn_s[rows, :]
    qs = jnp.concatenate([q[:, h * HEAD_DIM:(h + 1) * HEAD_DIM] for h in range(B_HEADS)], axis=0)
    kn8 = jnp.concatenate([kn[:, (h // 4) * HEAD_DIM:(h // 4 + 1) * HEAD_DIM] for h in range(B_HEADS)], axis=0)
    vn8 = jnp.concatenate([vn[:, (h // 4) * HEAD_DIM:(h // 4 + 1) * HEAD_DIM] for h in range(B_HEADS)], axis=0)
    nrows = B_HEADS * t
    ridx = lax.broadcasted_iota(jnp.int32, (nrows, 1), 0)
    slope = jnp.zeros((nrows, 1), F32)
    sink = jnp.zeros((nrows, 1), F32)
    for h in range(B_HEADS):
        in_h = (ridx >= h * t) & (ridx < (h + 1) * t)
        slope = jnp.where(in_h, 2.0 ** (-(h + 1)), slope)
        sink = jnp.where(in_h, sinks_ref[h], sink)
    s_c = lax.dot_general(qs.astype(BF16), ck_ref[...].astype(BF16), (((1,), (1,)), ((), ())),
                          preferred_element_type=F32)
    rr = lax.broadcasted_iota(jnp.int32, (nrows, ncols), 0)
    col = lax.broadcasted_iota(jnp.int32, (nrows, ncols), 1)
    pos = (col // 2) % CHUNK
    own = ((col // per_seq) == (rr % t)) & ((col % 2) == (rr // (4 * t))) & (pos >= 1)
    s_c = s_c * (HEAD_DIM ** -0.5) - slope * (CHUNK - pos).astype(F32)
    s_c = jnp.where(own, s_c, NEG)
    s_n = jnp.sum(qs * kn8, axis=-1, keepdims=True) * (HEAD_DIM ** -0.5)
    mx = jnp.maximum(jnp.maximum(jnp.max(s_c, axis=-1, keepdims=True), s_n), sink)
    e_c = jnp.exp(s_c - mx)
    e_n = jnp.exp(s_n - mx)
    inv = 1.0 / (jnp.sum(e_c, axis=-1, keepdims=True) + e_n + jnp.exp(sink - mx))
    o = jnp.dot((e_c * inv).astype(BF16), cv_ref[...].astype(BF16), preferred_element_type=F32)
    o = o + (e_n * inv) * vn8
    yb_s[rows, :] = jnp.concatenate([o[h * t:(h + 1) * t] for h in range(B_HEADS)], axis=-1)

    nk_ref[...] = pltpu.roll(ck_ref[...], ncols - 2, 0)
    nv_ref[...] = pltpu.roll(cv_ref[...], ncols - 2, 0)
    for b in range(t):
        for kv in range(2):
            r = b * per_seq + per_seq - 2 + kv
            nk_ref[r:r + 1, :] = kn[b:b + 1, kv * HEAD_DIM:(kv + 1) * HEAD_DIM]
            nv_ref[r:r + 1, :] = vn[b:b + 1, kv * HEAD_DIM:(kv + 1) * HEAD_DIM]

    @pl.when(i == pl.num_programs(0) - 1)
    def _():
        yb_n = _rms(yb_s[...], gob_ref[...])
        cat = jnp.concatenate([ya_s[...], yb_n], axis=-1).astype(BF16)
        h1 = x_ref[...] + jnp.dot(cat, wo_ref[...], preferred_element_type=F32)
        xn2 = _rms(h1, gmoe_ref[...])
        gates, sel = _route(xn2, wr_ref, br_ref)
        h1_ref[...] = h1
        xn_ref[...] = xn2.astype(BF16)
        gm_ref[...] = gates
        sm_ref[...] = sel


def _decode_call(x, ck, cv, sinks, gmix, win, lng, lnb, w00, b0, goa, gob, wo, gmoe, wr, br, h1, xn, gm, sm):
    t = DEC_TILE
    per_seq = CHUNK * 2
    cache = pl.BlockSpec((t * per_seq, HEAD_DIM), lambda i: (i, 0))
    tok = lambda width: pl.BlockSpec((DEC_BATCH, width), lambda i: (N_PROMPT // DEC_BATCH, 0))
    anyspec = pl.BlockSpec(memory_space=pl.ANY)
    return pl.pallas_call(
        _decode_kernel,
        grid=(DEC_BATCH // t,),
        in_specs=[
            pl.BlockSpec(memory_space=pltpu.SMEM),
            _full((DEC_BATCH, D_MODEL)), cache, cache,
            _full((1, D_MODEL)), _full((D_MODEL, IN_WIDTH)), _full((1, A_WIDTH)), _full((1, A_WIDTH)),
            _full((1, A_WIDTH)), _full((1, A_WIDTH)), _full((1, A_WIDTH)), _full((1, B_WIDTH)),
            _full((D_MODEL, D_MODEL)), _full((1, D_MODEL)), _full((D_MODEL, LANES)), _full((1, LANES)),
            anyspec, anyspec, anyspec, anyspec,
        ],
        out_specs=[tok(D_MODEL), tok(D_MODEL), tok(LANES), tok(LANES), cache, cache, _full((DEC_BATCH, A_WIDTH))],
        out_shape=[
            jax.ShapeDtypeStruct((N_PAD, D_MODEL), F32),
            jax.ShapeDtypeStruct((N_PAD, D_MODEL), BF16),
            jax.ShapeDtypeStruct((N_PAD, LANES), F32),
            jax.ShapeDtypeStruct((N_PAD, LANES), F32),
            jax.ShapeDtypeStruct((DEC_BATCH * per_seq, HEAD_DIM), F32),
            jax.ShapeDtypeStruct((DEC_BATCH * per_seq, HEAD_DIM), F32),
            jax.ShapeDtypeStruct((DEC_BATCH, A_WIDTH), F32),
        ],
        scratch_shapes=[pltpu.VMEM((DEC_BATCH, B_WIDTH), F32), pltpu.VMEM((DEC_BATCH, KV_WIDTH), F32),
                        pltpu.VMEM((DEC_BATCH, KV_WIDTH), F32), pltpu.VMEM((DEC_BATCH, A_WIDTH), F32),
                        pltpu.VMEM((DEC_BATCH, B_WIDTH), F32)],
        input_output_aliases={16: 0, 17: 1, 18: 2, 19: 3},
        compiler_params=pltpu.CompilerParams(
            dimension_semantics=("arbitrary",), vmem_limit_bytes=VMEM_LIMIT),
        name="sample_premoe",
    )(sinks, x, ck, cv, gmix, win, lng, lnb, w00, b0, goa, gob, wo, gmoe, wr, br, h1, xn, gm, sm)


def _plan_kernel(sm_ref,
                 destm_ref, destt_ref, xbe_ref, live_ref, nxblk_ref, stab_ref, ctab_ref, astart_ref, nwin_ref,
                 base_s, pstart_s):
    ph = pl.program_id(0)
    step = pl.program_id(1)
    lane = _lane_iota((1, LANES))

    @pl.when((ph == 0) & (step == 0))
    def _():
        base_s[...] = jnp.zeros_like(base_s)

    @pl.when(ph == 0)
    def _():
        base_s[...] += jnp.sum(sm_ref[...], axis=0, keepdims=True)

    @pl.when((ph == 1) & (step == 0))
    def _():
        counts = base_s[...]
        padded = jnp.floor((counts + (DISP_CHUNK + EXP_BLOCK - 1)) * (1.0 / EXP_BLOCK)) * EXP_BLOCK
        padded = jnp.where(counts > 0.0, padded, 0.0)
        pend = padded
        for s in (1, 2, 4, 8, 16):
            pend = pend + jnp.where(lane >= s, pltpu.roll(pend, s, 1), 0.0)
        spare = (N_ROWS + lane * DISP_CHUNK).astype(F32)
        pstart_s[...] = jnp.where(counts > 0.0, pend - padded, spare)
        base_s[...] = jnp.zeros_like(base_s)
        brow = lax.broadcasted_iota(jnp.int32, (XBE_ROWS, LANES), 0).astype(F32) * EXP_BLOCK
        done = jnp.where((lane < N_EXPERTS) & (pend <= brow), 1.0, 0.0)
        be = jnp.minimum(jnp.sum(done, axis=-1, keepdims=True), N_EXPERTS - 1.0)
        xbe_ref[...] = jnp.broadcast_to(be, (XBE_ROWS, LANES)).astype(jnp.int32)
        real = jnp.clip(counts - (brow - (pend - padded)), 0.0, float(EXP_BLOCK))
        real = jnp.sum(jnp.where(lane.astype(F32) == be, real, 0.0), axis=-1, keepdims=True)
        groups = jnp.floor((real + (ROW_GROUP - 1)) * (1.0 / ROW_GROUP))
        live_ref[...] = jnp.broadcast_to(groups, (XBE_ROWS, LANES)).astype(jnp.int32)
        total = jnp.sum(jnp.where(lane == N_EXPERTS - 1, pend, 0.0), axis=-1, keepdims=True)
        nxblk_ref[...] = jnp.broadcast_to(total * (1.0 / EXP_BLOCK), (8, LANES)).astype(jnp.int32)
        stab_ref[...] = jnp.zeros_like(stab_ref)
        ctab_ref[...] = jnp.zeros_like(ctab_ref)
        astart_ref[...] = jnp.zeros_like(astart_ref)
        nwin_ref[...] = jnp.zeros_like(nwin_ref)

    @pl.when(ph == 1)
    def _():
        r = lax.broadcasted_iota(jnp.int32, (TOK_TILE, TOK_TILE), 0)
        c = lax.broadcasted_iota(jnp.int32, (TOK_TILE, TOK_TILE), 1)
        lower = jnp.where(c < r, 1.0, 0.0).astype(BF16)
        for q in range(TM // TOK_TILE):
            i = step * (TM // TOK_TILE) + q
            sel = sm_ref[q * TOK_TILE:(q + 1) * TOK_TILE, :]
            cnt = jnp.sum(sel, axis=0, keepdims=True)
            prefix = jnp.dot(lower, sel.astype(BF16), preferred_element_type=F32)
            start = pstart_s[...] + base_s[...]
            dest = jnp.where(sel > 0.0, prefix + start, -1.0)
            destm_ref[q * TOK_TILE:(q + 1) * TOK_TILE, :] = dest
            destt_ref[:, q * TOK_TILE:(q + 1) * TOK_TILE] = dest.T
            has = (cnt > 0.0) & (lane < N_EXPERTS)
            stab_ref[pl.ds(i, 1), :] = start.astype(jnp.int32)
            ctab_ref[pl.ds(i, 1), :] = jnp.where(has, cnt, 0.0).astype(jnp.int32)
            a = jnp.minimum(jnp.floor(start * (1.0 / WIN_ALIGN)) * WIN_ALIGN, float(N_ROWS - WIN))
            nw = jnp.where(has, jnp.floor((start + cnt - a + (WIN - 1)) * (1.0 / WIN)), 0.0)
            astart_ref[pl.ds(i, 1), :] = a.astype(jnp.int32)
            nwin_ref[pl.ds(i, 1), :] = nw.astype(jnp.int32)
            base_s[...] += cnt


def _plan_call(sm):
    tile = lambda ph, i: (i * ph, 0)
    tile_t = lambda ph, i: (0, i * ph)
    tab = jax.ShapeDtypeStruct((TAB_ROWS, LANES), jnp.int32)
    return pl.pallas_call(
        _plan_kernel,
        grid=(2, N_PAD // TM),
        in_specs=[pl.BlockSpec((TM, LANES), lambda ph, i: (i, 0))],
        out_specs=[
            pl.BlockSpec((TM, LANES), tile),
            pl.BlockSpec((LANES, TM), tile_t),
            _full((XBE_ROWS, LANES)), _full((XBE_ROWS, LANES)), _full((8, LANES)),
            _full((TAB_ROWS, LANES)), _full((TAB_ROWS, LANES)), _full((TAB_ROWS, LANES)), _full((TAB_ROWS, LANES)),
        ],
        out_shape=[
            jax.ShapeDtypeStruct((N_PAD, LANES), F32),
            jax.ShapeDtypeStruct((LANES, N_PAD), F32),
            jax.ShapeDtypeStruct((XBE_ROWS, LANES), jnp.int32),
            jax.ShapeDtypeStruct((XBE_ROWS, LANES), jnp.int32),
            jax.ShapeDtypeStruct((8, LANES), jnp.int32),
            tab, tab, tab, tab,
        ],
        scratch_shapes=[pltpu.VMEM((1, LANES), F32), pltpu.VMEM((1, LANES), F32)],
        compiler_params=pltpu.CompilerParams(
            dimension_semantics=("arbitrary", "arbitrary"), vmem_limit_bytes=VMEM_LIMIT),
        name="moe_plan",
    )(sm)


def _pack_rows(z):
    half = D_MODEL // 2
    lo = lax.bitcast_convert_type(z[:, :half], jnp.uint32) >> 16
    hi = lax.bitcast_convert_type(z[:, half:], jnp.uint32) & jnp.uint32(0xFFFF0000)
    return lax.bitcast_convert_type(hi | lo, jnp.int32)


def _unpack_rows(ref, rows=None):
    rows = ref.shape[0] if rows is None else rows
    flat = ref.reshape(ref.shape[0] * PACK, LANES)
    lo, hi = [], []
    for s in range(PACK):
        w = lax.bitcast_convert_type(flat[pl.ds(s, rows, stride=PACK), :], jnp.uint32)
        lo.append(lax.bitcast_convert_type(w << 16, F32))
        hi.append(lax.bitcast_convert_type(w & jnp.uint32(0xFFFF0000), F32))
    return jnp.concatenate(lo + hi, axis=-1).astype(BF16)


def _dispatch_kernel(stab_ref, ctab_ref, cmax_ref, xn_ref, destt_ref, xs_in, xs_ref,
                     stage0, stage1, stage2, sems, sem2):
    del xs_in
    i = pl.program_id(0)
    last = pl.num_programs(0) - 1
    x = xn_ref[...]
    dt = destt_ref[...]
    rio = lax.broadcasted_iota(jnp.int32, (DISP_CHUNK, 1), 0).astype(F32)

    def chunk_rows(j, stage):
        parts = []
        for e in range(N_EXPERTS):
            first = (stab_ref[i * N_EXPERTS + e] + j * DISP_CHUNK).astype(F32)
            parts.append(jnp.where(dt[e:e + 1, :] == first + rio, 1.0, 0.0).astype(BF16))
        onehot = jnp.concatenate(parts, axis=0)
        words = _pack_rows(jnp.dot(onehot, x, preferred_element_type=F32))
        for s in range(PACK):
            stage[pl.ds(s, N_EXPERTS * DISP_CHUNK, stride=PACK), :] = words[:, s * LANES:(s + 1) * LANES]

    def copy(stage, step, e, j, sem):
        first = stab_ref[step * N_EXPERTS + e] + j * DISP_CHUNK
        rows = stage.reshape(N_EXPERTS * DISP_CHUNK, PACK, LANES)
        return pltpu.make_async_copy(rows.at[pl.ds(e * DISP_CHUNK, DISP_CHUNK)],
                                     xs_ref.at[pl.ds(first, DISP_CHUNK)], sem)

    def step_body(stage, prev_stage, par):
        chunk_rows(0, stage)

        @pl.when(i > 0)
        def _():
            for e in range(N_EXPERTS):
                copy(prev_stage, i - 1, e, 0, sems.at[1 - par, 0]).wait()

        for e in range(N_EXPERTS):
            copy(stage, i, e, 0, sems.at[par, 0]).start()

        @pl.when(i == last)
        def _():
            for e in range(N_EXPERTS):
                copy(stage, i, e, 0, sems.at[par, 0]).wait()

    @pl.when(i % 2 == 0)
    def _():
        step_body(stage0, stage1, 0)

    @pl.when(i % 2 == 1)
    def _():
        step_body(stage1, stage0, 1)

    for j in range(1, TOK_TILE // DISP_CHUNK):

        @pl.when(cmax_ref[i] > j * DISP_CHUNK)
        def _(j=j):
            chunk_rows(j, stage2)
            for e in range(N_EXPERTS):

                @pl.when(ctab_ref[i * N_EXPERTS + e] > j * DISP_CHUNK)
                def _(e=e):
                    cp = copy(stage2, i, e, j, sem2)
                    cp.start()
                    cp.wait()


def _dispatch_call(stab, ctab, cmax, xn, destt, xs_zero):
    stage = pltpu.VMEM((N_EXPERTS * DISP_CHUNK * PACK, LANES), jnp.int32)
    grid_spec = pltpu.PrefetchScalarGridSpec(
        num_scalar_prefetch=3,
        grid=(N_PAD_TILES,),
        in_specs=[
            pl.BlockSpec((TOK_TILE, D_MODEL), lambda i, *_: (i, 0)),
            pl.BlockSpec((N_EXPERTS, TOK_TILE), lambda i, *_: (0, i)),
            pl.BlockSpec(memory_space=pl.ANY),
        ],
        out_specs=pl.BlockSpec(memory_space=pl.ANY),
        scratch_shapes=[stage, stage, stage, pltpu.SemaphoreType.DMA((2, N_EXPERTS)),
                        pltpu.SemaphoreType.DMA],
    )
    return pl.pallas_call(
        _dispatch_kernel,
        grid_spec=grid_spec,
        out_shape=jax.ShapeDtypeStruct((XS_ROWS, PACK, LANES), jnp.int32),
        input_output_aliases={5: 0},
        compiler_params=pltpu.CompilerParams(
            dimension_semantics=("arbitrary",), vmem_limit_bytes=VMEM_LIMIT),
        name="moe_dispatch",
    )(stab, ctab, cmax, xn, destt, xs_zero)


def _expert_kernel(blke_ref, nblk_ref, live_ref, xs_ref, wgu_hbm, bgu_ref, wdn_hbm, bdn_ref,
                   ys_ref, wgu_f, wdn_f, wgu_s, wdn_s, sems):
    b = pl.program_id(0)
    used = b < nblk_ref[0]
    prev = blke_ref[jnp.maximum(b - 1, 0)]
    fresh = used & ((b == 0) | (blke_ref[b] != prev))

    def fetch(e):
        return (pltpu.make_async_copy(wgu_hbm.at[e], wgu_f, sems.at[0]),
                pltpu.make_async_copy(wdn_hbm.at[e], wdn_f, sems.at[1]))

    @pl.when(b == 0)
    def _():
        for cp in fetch(blke_ref[0]):
            cp.start()

    @pl.when(fresh)
    def _():
        for cp in fetch(blke_ref[b]):
            cp.wait()
        wgu_s[...] = wgu_f[...].astype(BF16)
        wdn_s[...] = wdn_f[...].astype(BF16)

        nxt = lax.while_loop(lambda p: (p < nblk_ref[0]) & (blke_ref[jnp.minimum(p, N_XBLOCKS - 1)] == blke_ref[b]),
                             lambda p: p + 1, b + 1)

        @pl.when(nxt < nblk_ref[0])
        def _():
            for cp in fetch(blke_ref[jnp.minimum(nxt, N_XBLOCKS - 1)]):
                cp.start()

    for groups in range(1, EXP_BLOCK // ROW_GROUP + 1):
        rows = groups * ROW_GROUP

        @pl.when(used & (live_ref[b] == groups))
        def _(rows=rows):
            hid = jnp.dot(_unpack_rows(xs_ref, rows), wgu_s[...], preferred_element_type=F32) + bgu_ref[...]
            gate = jnp.minimum(hid[:, :D_FF], SWIGLU_LIMIT)
            up = jnp.clip(hid[:, D_FF:], -SWIGLU_LIMIT, SWIGLU_LIMIT)
            act = (up + 1.0) * gate * jax.nn.sigmoid(SWIGLU_ALPHA * gate)
            y = jnp.dot(act.astype(BF16), wdn_s[...], preferred_element_type=F32) + bdn_ref[...]
            ys_ref[0:rows, :] = y.astype(BF16)
            if rows < EXP_BLOCK:
                ys_ref[rows:, :] = jnp.zeros((EXP_BLOCK - rows, D_MODEL), BF16)

    @pl.when(jnp.logical_not(used) | (live_ref[b] == 0))
    def _():
        ys_ref[...] = jnp.zeros_like(ys_ref)


def _expert_call(blke, nblk, live, xs, wgu, bgu, wdn, bdn):
    grid_spec = pltpu.PrefetchScalarGridSpec(
        num_scalar_prefetch=3,
        grid=(N_XBLOCKS,),
        in_specs=[
            pl.BlockSpec((EXP_BLOCK, PACK, LANES), lambda b, be, *_: (b, 0, 0)),
            pl.BlockSpec(memory_space=pl.ANY),
            pl.BlockSpec((None, 1, 2 * D_FF), lambda b, be, *_: (be[b], 0, 0)),
            pl.BlockSpec(memory_space=pl.ANY),
            pl.BlockSpec((None, 1, D_MODEL), lambda b, be, *_: (be[b], 0, 0)),
        ],
        out_specs=pl.BlockSpec((EXP_BLOCK, D_MODEL), lambda b, be, *_: (b, 0)),
        scratch_shapes=[pltpu.VMEM((D_MODEL, 2 * D_FF), F32), pltpu.VMEM((D_FF, D_MODEL), F32),
                        pltpu.VMEM((D_MODEL, 2 * D_FF), BF16), pltpu.VMEM((D_FF, D_MODEL), BF16),
                        pltpu.SemaphoreType.DMA((2,))],
    )
    return pl.pallas_call(
        _expert_kernel,
        grid_spec=grid_spec,
        out_shape=jax.ShapeDtypeStruct((N_ROWS, D_MODEL), BF16),
        compiler_params=pltpu.CompilerParams(
            dimension_semantics=("arbitrary",), vmem_limit_bytes=VMEM_LIMIT),
        name="moe_experts",
    )(blke, nblk, live, xs, wgu, bgu, wdn, bdn)


def _combine_kernel(*refs):
    astart_ref, nwin_ref, over_ref = refs[0:3]
    win_refs = refs[3:3 + N_EXPERTS]
    (destm_ref, gm_ref, h1_ref, plep_ref, ples_ref, gple_ref, wpg_ref, wpp_ref, gfin_ref, ys_any,
     yp_ref, ysm_ref, moe_s, tmp_s, sem) = refs[3 + N_EXPERTS:]
    i = pl.program_id(0)
    dest = destm_ref[...]
    gates = gm_ref[...]
    lane = _lane_iota((COMB_TILE, LANES))
    lane_f = lane.astype(F32)
    lo = lane < WIN
    moe = jnp.zeros((COMB_TILE, D_MODEL), F32)
    group = 4
    for g0 in range(0, N_EXPERTS, group):
        g_hi, g_lo = [], []
        for p in range(group // 2):
            e0 = g0 + 2 * p
            a0 = astart_ref[i * N_EXPERTS + e0].astype(F32)
            a1 = astart_ref[i * N_EXPERTS + e0 + 1].astype(F32)
            rowid = jnp.where(lo, a0 + lane_f, a1 + lane_f - WIN)
            dcol = jnp.where(lo, dest[:, e0:e0 + 1], dest[:, e0 + 1:e0 + 2])
            gcol = jnp.where(lo, gates[:, e0:e0 + 1], gates[:, e0 + 1:e0 + 2])
            gsel = jnp.where(dcol == rowid, gcol, 0.0)
            hi = gsel.astype(BF16)
            g_hi.append(hi)
            g_lo.append((gsel - hi.astype(F32)).astype(BF16))
        ywin = jnp.concatenate([win_refs[g0 + q][...] for q in range(group)], axis=0)
        both = jnp.concatenate([jnp.concatenate(g_hi, axis=-1), jnp.concatenate(g_lo, axis=-1)], axis=0)
        r = jnp.dot(both, ywin, preferred_element_type=F32)
        moe = moe + r[:COMB_TILE] + r[COMB_TILE:]
    mrows = pl.ds(pl.multiple_of((i % TAIL_TILES) * COMB_TILE, COMB_TILE), COMB_TILE)
    moe_s[mrows, :] = moe

    @pl.when(over_ref[i] > 0)
    def _():
        tmp_s[...] = jnp.zeros_like(tmp_s)

        def per_expert(e, carry):
            a = astart_ref[i * N_EXPERTS + e]
            dcol = jnp.sum(jnp.where(lane == e, dest, 0.0), axis=-1, keepdims=True)
            gcol = jnp.sum(jnp.where(lane == e, gates, 0.0), axis=-1, keepdims=True)

            def per_window(w, carry2):
                first = a + w * WIN
                start = pl.multiple_of(jnp.minimum(first, N_ROWS - WIN), WIN_ALIGN)
                cp = pltpu.make_async_copy(ys_any.at[pl.ds(start, WIN)], tmp_s.at[pl.ds(0, WIN)], sem)
                cp.start()
                cp.wait()
                hit = lo & (dcol == start.astype(F32) + lane_f) & (dcol >= first.astype(F32))
                gsel = jnp.where(hit, gcol, 0.0)
                hi = gsel.astype(BF16)
                rest = (gsel - hi.astype(F32)).astype(BF16)
                moe_s[mrows, :] += (jnp.dot(hi, tmp_s[...], preferred_element_type=F32)
                                    + jnp.dot(rest, tmp_s[...], preferred_element_type=F32))
                return carry2

            return lax.fori_loop(1, nwin_ref[i * N_EXPERTS + e], per_window, carry)

        lax.fori_loop(0, N_EXPERTS, per_expert, 0)

    def tail(h1, moe_rows, ple):
        h2 = h1 + moe_rows
        hn = _rms(h2, gple_ref[...]).astype(BF16)
        gate = jax.nn.sigmoid(jnp.dot(hn, wpg_ref[...], preferred_element_type=F32))
        proj = jnp.dot(ple.astype(BF16), wpp_ref[...], preferred_element_type=F32)
        return _rms(h2 + gate * proj, gfin_ref[...])

    is_sample = i == N_CTILES - 1

    @pl.when((i % TAIL_TILES == TAIL_TILES - 1) & jnp.logical_not(is_sample))
    def _():
        yp_ref[...] = tail(h1_ref[...], moe_s[...], plep_ref[...])

    @pl.when(is_sample)
    def _():
        ysm_ref[...] = tail(h1_ref[0:DEC_BATCH, :], moe_s[0:DEC_BATCH, :], ples_ref[...])


def _combine_call(astart, nwin, over, ys, destm, gm, h1, plep, ples, gple, wpg, wpp, gfin):
    rows = TAIL_TILES * COMB_TILE
    last = N_PROMPT // rows - 1

    def win_spec(e):
        return pl.BlockSpec((pl.Element(WIN), pl.Element(D_MODEL)),
                            lambda i, a, nw, ov, e=e: (pl.multiple_of(a[i * N_EXPERTS + e], WIN_ALIGN), 0))

    grid_spec = pltpu.PrefetchScalarGridSpec(
        num_scalar_prefetch=3,
        grid=(N_CTILES,),
        in_specs=[win_spec(e) for e in range(N_EXPERTS)] + [
            pl.BlockSpec((COMB_TILE, LANES), lambda i, *_: (i, 0)),
            pl.BlockSpec((COMB_TILE, LANES), lambda i, *_: (i, 0)),
            pl.BlockSpec((rows, D_MODEL), lambda i, *_: (i // TAIL_TILES, 0)),
            pl.BlockSpec((rows, PLE_DIM), lambda i, *_: (jnp.minimum(i // TAIL_TILES, last), 0)),
            pl.BlockSpec((DEC_BATCH, PLE_DIM), lambda i, *_: (0, 0)),
            pl.BlockSpec((1, D_MODEL), lambda i, *_: (0, 0)),
            pl.BlockSpec((D_MODEL, D_MODEL), lambda i, *_: (0, 0)),
            pl.BlockSpec((PLE_DIM, D_MODEL), lambda i, *_: (0, 0)),
            pl.BlockSpec((1, D_MODEL), lambda i, *_: (0, 0)),
            pl.BlockSpec(memory_space=pl.ANY),
        ],
        out_specs=[
            pl.BlockSpec((rows, D_MODEL), lambda i, *_: (jnp.minimum(i // TAIL_TILES, last), 0)),
            pl.BlockSpec((DEC_BATCH, D_MODEL), lambda i, *_: (0, 0)),
        ],
        scratch_shapes=[pltpu.VMEM((rows, D_MODEL), F32), pltpu.VMEM((2 * WIN, D_MODEL), BF16),
                        pltpu.SemaphoreType.DMA],
    )
    return pl.pallas_call(
        _combine_kernel,
        grid_spec=grid_spec,
        out_shape=[jax.ShapeDtypeStruct((N_PROMPT, D_MODEL), F32),
                   jax.ShapeDtypeStruct((DEC_BATCH, D_MODEL), F32)],
        compiler_params=pltpu.CompilerParams(
            dimension_semantics=("arbitrary",), vmem_limit_bytes=VMEM_LIMIT),
        name="moe_combine_tail",
    )(astart, nwin, over, *([ys] * N_EXPERTS), destm, gm, h1, plep, ples, gple, wpg, wpp, gfin, ys)


def _gather_kernel(*refs):
    astart_ref, nwin_ref, over_ref = refs[0:3]
    win_refs = refs[3:3 + N_EXPERTS]
    (destm_ref, gm_ref, h1_ref, plep_ref, ples_ref, gple_ref, wpg_ref, wpp_ref, gfin_ref, ys_any,
     yp_ref, ysm_ref, moe_s, tmp_s, sem) = refs[3 + N_EXPERTS:]
    i = pl.program_id(0)
    slot = i % TAIL_TILES
    dest = destm_ref[...]
    gates = gm_ref[...]
    lane = _lane_iota((TOK_TILE, LANES))
    lane_f = lane.astype(F32)
    per_slab = LANES // CWIN
    within = (lane % CWIN).astype(F32)
    group = 2 * per_slab

    def split(gsel):
        hi = gsel.astype(BF16)
        return hi, (gsel - hi.astype(F32)).astype(BF16)

    moe = jnp.zeros((TOK_TILE, D_MODEL), F32)
    for g0 in range(0, N_EXPERTS, group):
        his, los = [], []
        for sl in range(2):
            rowid = jnp.zeros((TOK_TILE, LANES), F32)
            dcol = jnp.zeros((TOK_TILE, LANES), F32)
            gcol = jnp.zeros((TOK_TILE, LANES), F32)
            for q in range(per_slab):
                e = g0 + sl * per_slab + q
                mine = (lane >= q * CWIN) & (lane < (q + 1) * CWIN)
                rowid = jnp.where(mine, astart_ref[i * N_EXPERTS + e].astype(F32) + within, rowid)
                dcol = jnp.where(mine, dest[:, e:e + 1], dcol)
                gcol = jnp.where(mine, gates[:, e:e + 1], gcol)
            hi, lo = split(jnp.where(dcol == rowid, gcol, 0.0))
            his.append(hi)
            los.append(lo)
        ywin = jnp.concatenate([_unpack_rows(win_refs[g0 + q]) for q in range(group)], axis=0)
        both = jnp.concatenate([jnp.concatenate(his, axis=-1), jnp.concatenate(los, axis=-1)], axis=0)
        r = jnp.dot(both, ywin, preferred_element_type=F32)
        moe = moe + r[:TOK_TILE] + r[TOK_TILE:]
    mrows = pl.ds(pl.multiple_of(slot * TOK_TILE, TOK_TILE), TOK_TILE)
    moe_s[mrows, :] = moe

    @pl.when(over_ref[i] > 0)
    def _():
        def per_expert(e, carry):
            a = astart_ref[i * N_EXPERTS + e]
            dcol = jnp.sum(jnp.where(lane == e, dest, 0.0), axis=-1, keepdims=True)
            gcol = jnp.sum(jnp.where(lane == e, gates, 0.0), axis=-1, keepdims=True)

            def per_window(w, carry2):
                first = a + w * CWIN
                start = jnp.minimum(first, N_ROWS - CWIN)
                cp = pltpu.make_async_copy(ys_any.at[pl.ds(start, CWIN)], tmp_s, sem)
                cp.start()
                cp.wait()
                hit = (lane < CWIN) & (dcol == start.astype(F32) + lane_f) & (dcol >= first.astype(F32))
                hi, lo = split(jnp.where(hit, gcol, 0.0))
                rows = jnp.concatenate([_unpack_rows(tmp_s), jnp.zeros((LANES - CWIN, D_MODEL), BF16)], axis=0)
                moe_s[mrows, :] += (jnp.dot(hi, rows, preferred_element_type=F32)
                                    + jnp.dot(lo, rows, preferred_element_type=F32))
                return carry2

            return lax.fori_loop(1, nwin_ref[i * N_EXPERTS + e], per_window, carry)

        lax.fori_loop(0, N_EXPERTS, per_expert, 0)

    def tail(h1, moe_rows, ple):
        h2 = h1 + moe_rows
        hn = _rms(h2, gple_ref[...]).astype(BF16)
        gate = jax.nn.sigmoid(jnp.dot(hn, wpg_ref[...], preferred_element_type=F32))
        proj = jnp.dot(ple.astype(BF16), wpp_ref[...], preferred_element_type=F32)
        return _rms(h2 + gate * proj, gfin_ref[...])

    is_sample = i == N_TILES - 1

    @pl.when((slot == TAIL_TILES - 1) & jnp.logical_not(is_sample))
    def _():
        yp_ref[...] = tail(h1_ref[...], moe_s[...], plep_ref[...])

    @pl.when(is_sample)
    def _():
        ysm_ref[...] = tail(h1_ref[0:TOK_TILE, :], moe_s[0:TOK_TILE, :], ples_ref[...])


def _gather_call(astart, nwin, over, ys, destm, gm, h1, plep, ples, gple, wpg, wpp, gfin):
    rows = TAIL_TILES * TOK_TILE
    last = N_PROMPT // rows - 1

    def win_spec(e):
        return pl.BlockSpec((pl.Element(CWIN), pl.Element(PACK), pl.Element(LANES)),
                            lambda i, a, nw, ov, e=e: (a[i * N_EXPERTS + e], 0, 0))

    grid_spec = pltpu.PrefetchScalarGridSpec(
        num_scalar_prefetch=3,
        grid=(N_CTILES,),
        in_specs=[win_spec(e) for e in range(N_EXPERTS)] + [
            pl.BlockSpec((COMB_TILE, LANES), lambda i, *_: (i, 0)),
            pl.BlockSpec((COMB_TILE, LANES), lambda i, *_: (i, 0)),
            pl.BlockSpec((rows, D_MODEL), lambda i, *_: (i // TAIL_TILES, 0)),
            pl.BlockSpec((rows, PLE_DIM), lambda i, *_: (jnp.minimum(i // TAIL_TILES, last), 0)),
            pl.BlockSpec((DEC_BATCH, PLE_DIM), lambda i, *_: (0, 0)),
            pl.BlockSpec((1, D_MODEL), lambda i, *_: (0, 0)),
            pl.BlockSpec((D_MODEL, D_MODEL), lambda i, *_: (0, 0)),
            pl.BlockSpec((PLE_DIM, D_MODEL), lambda i, *_: (0, 0)),
            pl.BlockSpec((1, D_MODEL), lambda i, *_: (0, 0)),
            pl.BlockSpec(memory_space=pl.ANY),
        ],
        out_specs=[
            pl.BlockSpec((rows, D_MODEL), lambda i, *_: (jnp.minimum(i // TAIL_TILES, last), 0)),
            pl.BlockSpec((TOK_TILE, D_MODEL), lambda i, *_: (0, 0)),
        ],
        scratch_shapes=[pltpu.VMEM((rows, D_MODEL), F32), pltpu.VMEM((CWIN, PACK, LANES), jnp.int32),
                        pltpu.SemaphoreType.DMA],
    )
    return pl.pallas_call(
        _gather_kernel,
        grid_spec=grid_spec,
        out_shape=[jax.ShapeDtypeStruct((N_PROMPT, D_MODEL), F32),
                   jax.ShapeDtypeStruct((DEC_BATCH, D_MODEL), F32)],
        compiler_params=pltpu.CompilerParams(
            dimension_semantics=("arbitrary",), vmem_limit_bytes=VMEM_LIMIT),
        name="moe_combine_tail",
    )(astart, nwin, over, *([ys] * N_EXPERTS), destm, gm, h1, plep, ples, gple, wpg, wpp, gfin, ys)


def kernel(x_prompt, x_sample, cache_swa_k, cache_swa_v, p_prompt, p_sample, g_mix, w_in, ln_v_g, ln_v_b,
           w_sp, b_sp, sinks, g_out_a, g_out_b, w_o, g_moe, w_router, b_router, w_gu, b_gu, w_dn, b_dn,
           g_ple, w_ple_gate, w_ple_proj, g_final):
    l = 0
    row = lambda v: v.reshape(1, -1)
    win = w_in[l].astype(BF16)
    wo = w_o[l].astype(BF16)
    tril = jnp.tril(jnp.ones((CHUNK, CHUNK), dtype=bool))
    wsp = jnp.where(tril, w_sp[l], 0.0).astype(BF16)
    bsp = jnp.repeat(b_sp[l].T, HEAD_DIM, axis=1)
    w00 = row(jnp.repeat(w_sp[l][:, 0, 0], HEAD_DIM))
    b0 = row(jnp.repeat(b_sp[l][:, 0], HEAD_DIM))
    wr_hi = w_router[l].astype(BF16)
    wr_lo = (w_router[l] - wr_hi.astype(F32)).astype(BF16)
    wr = jnp.concatenate([wr_hi, wr_lo, jnp.zeros((D_MODEL, LANES - 2 * N_EXPERTS), BF16)], axis=1)
    br = row(jnp.concatenate([b_router[l], jnp.zeros((LANES - N_EXPERTS,), F32)]))
    common = (row(g_mix[l]), win, row(ln_v_g[l]), row(ln_v_b[l]))
    tail = (row(g_out_a[l]), row(g_out_b[l]), wo, row(g_moe[l]), wr, br)

    h1, xn, gm, sm, k_p, v_p = _prompt_call(
        x_prompt.reshape(N_PROMPT, D_MODEL), sinks[l], *common, wsp, bsp, *tail)
    ck = cache_swa_k[l].reshape(DEC_BATCH * CHUNK * 2, HEAD_DIM)
    cv = cache_swa_v[l].reshape(DEC_BATCH * CHUNK * 2, HEAD_DIM)
    h1, xn, gm, sm, k_s, v_s, va_s = _decode_call(
        x_sample.reshape(DEC_BATCH, D_MODEL), ck, cv, sinks[l], *common, w00, b0, *tail, h1, xn, gm, sm)

    destm, destt, xbe, live, nxblk, stab, ctab, astart, nwin = _plan_call(sm)
    flat = lambda tab, n: tab[:n, :N_EXPERTS].reshape(-1)
    del astart, nwin
    per = COMB_TILE // TOK_TILE
    first = stab[0:per * N_CTILES:per, :N_EXPERTS]
    count = sum(ctab[q:per * N_CTILES:per, :N_EXPERTS] for q in range(per))
    astart2 = jnp.minimum(first // WIN_ALIGN * WIN_ALIGN, N_ROWS - WIN)
    nwin2 = jnp.where(count > 0, (first + count - astart2 + WIN - 1) // WIN, 0)
    astart1 = astart2.reshape(-1)
    over1 = (jnp.max(nwin2, axis=1) > 1).astype(jnp.int32)
    nwin1 = nwin2.reshape(-1)
    cmax1 = jnp.max(ctab[:N_PAD_TILES, :N_EXPERTS], axis=1)

    xs = _dispatch_call(flat(stab, N_PAD_TILES), flat(ctab, N_PAD_TILES), cmax1, xn, destt[:N_EXPERTS],
                        jnp.zeros((XS_ROWS, PACK, LANES), jnp.int32))
    ys = _expert_call(xbe[:N_XBLOCKS, 0], nxblk[0, :1], live[:N_XBLOCKS, 0], xs, w_gu[l], b_gu[l].reshape(N_EXPERTS, 1, 2 * D_FF),
                      w_dn[l], b_dn[l].reshape(N_EXPERTS, 1, D_MODEL))
    y_p, y_s = _combine_call(
        astart1, nwin1, over1, ys, destm, gm, h1,
        p_prompt[l].reshape(N_PROMPT, PLE_DIM), p_sample[l].reshape(DEC_BATCH, PLE_DIM),
        row(g_ple[l]), w_ple_gate[l].astype(BF16), w_ple_proj[l].astype(BF16), row(g_final))

    kv5 = lambda a, n: a.reshape(1, n, CHUNK, 2, HEAD_DIM)
    return (y_p.reshape(BATCH, SEQ, D_MODEL), y_s.reshape(DEC_BATCH, 1, D_MODEL),
            kv5(k_p, BATCH), kv5(v_p, BATCH), kv5(k_s, DEC_BATCH), kv5(v_s, DEC_BATCH),
            va_s.reshape(1, DEC_BATCH, 1, A_WIDTH))
```

```python
import math

import jax
import jax.numpy as jnp
from jax import lax
from jax.experimental import pallas as pl
from jax.experimental.pallas import tpu as pltpu

F32 = jnp.float32
BF16 = jnp.bfloat16

D_MODEL = 1024
BATCH = 4
SEQ = 4096
DEC_BATCH = 128
HEAD_DIM = 64
A_WIDTH = 512
B_WIDTH = 512
B_HEADS = 8
KV_WIDTH = 128
IN_WIDTH = 2 * A_WIDTH + B_WIDTH + 2 * KV_WIDTH
CHUNK = 128
N_EXPERTS = 32
TOP_K = 4
D_FF = 1024
SWIGLU_ALPHA = 1.702
SWIGLU_LIMIT = 7.0
PLE_DIM = 256
EPS = 1e-5

LANES = 128
ROW_GROUP = 128
EXP_BLOCK = 512
N_PROMPT = BATCH * SEQ
N_TOK = N_PROMPT + DEC_BATCH
TOK_TILE = 128
N_TILES = N_TOK // TOK_TILE
DISP_TILE = 256
TM = 512
N_PAD = ((N_TOK + TM - 1) // TM) * TM
N_PAD_TILES = N_PAD // TOK_TILE
DISP_CHUNK = 32
N_XBLOCKS = (N_TOK * TOP_K + N_EXPERTS * (DISP_CHUNK + EXP_BLOCK - 1) + EXP_BLOCK - 1) // EXP_BLOCK
N_ROWS = N_XBLOCKS * EXP_BLOCK
XS_ROWS = N_ROWS + N_EXPERTS * DISP_CHUNK
XBE_ROWS = ((N_XBLOCKS + 7) // 8) * 8
TAB_ROWS = ((N_PAD_TILES + 7) // 8) * 8
PACK = D_MODEL // 2 // LANES
WIN = 64
WIN_ALIGN = 16
CWIN = 32
COMB_TILE = 128
N_CTILES = N_PROMPT // COMB_TILE + 1
TAIL_TILES = 4
SAMPLE_TILE = 32
DEC_TILE = 16
NEG = -1e30
VMEM_LIMIT = 56 * 1024 * 1024


def _rms(x, g):
    return x * lax.rsqrt(jnp.mean(x * x, axis=-1, keepdims=True) + EPS) * g


def _gelu(x):
    c = math.sqrt(2.0 / math.pi)
    return x * (0.5 * (1.0 + jnp.tanh(c * (x + 0.044715 * (x * x * x)))))


def _layernorm(x, g, b):
    mu = jnp.mean(x, axis=-1, keepdims=True)
    xc = x - mu
    return xc * lax.rsqrt(jnp.mean(xc * xc, axis=-1, keepdims=True) + EPS) * g + b


def _lane_iota(shape):
    return lax.broadcasted_iota(jnp.int32, shape, len(shape) - 1)


def _route(xn2, wr_ref, br_ref):
    m = xn2.shape[0]
    xh = xn2.astype(BF16)
    xl = (xn2 - xh.astype(F32)).astype(BF16)
    r = jnp.dot(jnp.concatenate([xh, xl], axis=0), wr_ref[...], preferred_element_type=F32)
    r = r[:m] + r[m:]
    lane = _lane_iota((m, LANES))
    lane_f = lane.astype(F32)
    logits = jnp.where(lane < N_EXPERTS, r + pltpu.roll(r, LANES - N_EXPERTS, 1) + br_ref[...], NEG)
    work = logits
    sel = jnp.zeros((m, LANES), F32)
    top = None
    z = None
    for _ in range(TOP_K):
        mx = jnp.max(work, axis=-1, keepdims=True)
        first = jnp.min(jnp.where(work == mx, lane_f, float(LANES)), axis=-1, keepdims=True)
        hit = lane_f == first
        sel = jnp.where(hit, 1.0, sel)
        work = jnp.where(hit, NEG, work)
        if top is None:
            top = mx
            z = jnp.ones_like(mx)
        else:
            z = z + jnp.exp(mx - top)
    gates = jnp.where(sel > 0.0, jnp.exp(logits - top) / z, 0.0)
    return gates, sel


def _prompt_kernel(sinks_ref, x_ref, gmix_ref, win_ref, lng_ref, lnb_ref, wsp_ref, bsp_ref,
                   goa_ref, gob_ref, wo_ref, gmoe_ref, wr_ref, br_ref,
                   h1_ref, xn_ref, gm_ref, sm_ref, k_ref, v_ref,
                   z_s, kv_s, cat_s):
    g = pl.program_id(0)
    j = g % (SEQ // TM)

    @pl.when(g >= N_PROMPT // TM)
    def _():
        h1_ref[...] = jnp.zeros_like(h1_ref)
        xn_ref[...] = jnp.zeros_like(xn_ref)
        gm_ref[...] = jnp.zeros_like(gm_ref)
        sm_ref[...] = jnp.zeros_like(sm_ref)

    @pl.when(g < N_PROMPT // TM)
    def _():
        _prompt_tile(j, sinks_ref, x_ref, gmix_ref, win_ref, lng_ref, lnb_ref, wsp_ref, bsp_ref,
                     goa_ref, gob_ref, wo_ref, gmoe_ref, wr_ref, br_ref,
                     h1_ref, xn_ref, gm_ref, sm_ref, k_ref, v_ref, z_s, kv_s, cat_s)


def _prompt_tile(j, sinks_ref, x_ref, gmix_ref, win_ref, lng_ref, lnb_ref, wsp_ref, bsp_ref,
                 goa_ref, gob_ref, wo_ref, gmoe_ref, wr_ref, br_ref,
                 h1_ref, xn_ref, gm_ref, sm_ref, k_ref, v_ref, z_s, kv_s, cat_s):
    @pl.when(j == 0)
    def _():
        kv_s[0:CHUNK, :] = jnp.zeros((CHUNK, 2 * KV_WIDTH), F32)

    xn = _rms(x_ref[...], gmix_ref[...]).astype(BF16)
    z_s[...] = jnp.dot(xn, win_ref[...], preferred_element_type=F32)
    kv_s[CHUNK:, :] = z_s[:, 2 * A_WIDTH + B_WIDTH:]

    lane = _lane_iota((CHUNK, LANES))
    lo = lane < HEAD_DIM
    lane2 = _lane_iota((2 * CHUNK, LANES))
    lo2 = lane2 < HEAD_DIM
    qi = lax.broadcasted_iota(jnp.int32, (CHUNK, CHUNK), 0)
    kc = lax.broadcasted_iota(jnp.int32, (CHUNK, CHUNK), 1)
    from_prev = kc > qi
    dist = jnp.where(from_prev, qi + CHUNK - kc, qi - kc).astype(F32)

    def chunk_body(c, carry):
        r0 = pl.multiple_of(c * CHUNK, CHUNK)
        rows = pl.ds(r0, CHUNK)
        u = _gelu(z_s[rows, 0:A_WIDTH])
        va = _layernorm(_gelu(z_s[rows, A_WIDTH:2 * A_WIDTH]), lng_ref[...], lnb_ref[...])
        vab = va.astype(BF16)
        slabs = []
        for p in range(A_WIDTH // LANES):
            slab = vab[:, p * LANES:(p + 1) * LANES]
            m0 = jnp.dot(wsp_ref[2 * p], slab, preferred_element_type=F32)
            m1 = jnp.dot(wsp_ref[2 * p + 1], slab, preferred_element_type=F32)
            slabs.append(jnp.where(lo, m0, m1))
        ya = u * (jnp.concatenate(slabs, axis=-1) + bsp_ref[...])
        ya_n = _rms(ya, goa_ref[...])
        k2 = kv_s[pl.ds(r0, 2 * CHUNK), 0:KV_WIDTH]
        v2 = kv_s[pl.ds(r0, 2 * CHUNK), KV_WIDTH:2 * KV_WIDTH]
        k2r = pltpu.roll(k2, HEAD_DIM, 1)
        v2r = pltpu.roll(v2, HEAD_DIM, 1)
        kd = (jnp.where(lo2, k2, k2r).astype(BF16), jnp.where(lo2, k2r, k2).astype(BF16))
        vd = (jnp.where(lo2, v2, v2r).astype(BF16), jnp.where(lo2, v2r, v2).astype(BF16))
        prev_ok = (j > 0) | (c > 0)
        masked = from_prev & jnp.logical_not(prev_ok)
        yb_slabs = []
        for kv in range(2):
            q0 = z_s[rows, 2 * A_WIDTH + (2 * kv) * LANES:2 * A_WIDTH + (2 * kv + 1) * LANES]
            q1 = z_s[rows, 2 * A_WIDTH + (2 * kv + 1) * LANES:2 * A_WIDTH + (2 * kv + 2) * LANES]
            lhs = jnp.concatenate([jnp.where(lo, q0, 0.0), jnp.where(lo, 0.0, q0),
                                   jnp.where(lo, q1, 0.0), jnp.where(lo, 0.0, q1)], axis=0).astype(BF16)
            s_all = lax.dot_general(lhs, kd[kv], (((1,), (1,)), ((), ())), preferred_element_type=F32)
            probs = []
            for i in range(4):
                h = 4 * kv + i
                slope = 2.0 ** (-(h + 1))
                sink = sinks_ref[h]
                sh = s_all[i * CHUNK:(i + 1) * CHUNK]
                s = jnp.where(from_prev, sh[:, :CHUNK], sh[:, CHUNK:]) * (HEAD_DIM ** -0.5) - slope * dist
                s = jnp.where(masked, NEG, s)
                mx = jnp.maximum(jnp.max(s, axis=-1, keepdims=True), sink)
                e = jnp.exp(s - mx)
                den = jnp.sum(e, axis=-1, keepdims=True) + jnp.exp(sink - mx)
                p = e * (1.0 / den)
                probs.append(jnp.concatenate([jnp.where(from_prev, p, 0.0), jnp.where(from_prev, 0.0, p)], axis=-1))
            pm = jnp.concatenate(probs, axis=0).astype(BF16)
            o = jnp.dot(pm, vd[kv], preferred_element_type=F32)
            yb_slabs.append(jnp.where(lo, o[0:CHUNK], o[CHUNK:2 * CHUNK]))
            yb_slabs.append(jnp.where(lo, o[2 * CHUNK:3 * CHUNK], o[3 * CHUNK:4 * CHUNK]))
        yb_n = _rms(jnp.concatenate(yb_slabs, axis=-1), gob_ref[...])
        cat_s[rows, 0:A_WIDTH] = ya_n.astype(BF16)
        cat_s[rows, A_WIDTH:] = yb_n.astype(BF16)
        return carry

    lax.fori_loop(0, TM // CHUNK, chunk_body, 0)

    kv_s[0:CHUNK, :] = kv_s[TM:TM + CHUNK, :]
    k_ref[...] = kv_s[TM:TM + CHUNK, 0:KV_WIDTH]
    v_ref[...] = kv_s[TM:TM + CHUNK, KV_WIDTH:]

    h1 = x_ref[...] + jnp.dot(cat_s[...], wo_ref[...], preferred_element_type=F32)
    h1_ref[...] = h1
    xn2 = _rms(h1, gmoe_ref[...])
    xn_ref[...] = xn2.astype(BF16)
    gates, sel = _route(xn2, wr_ref, br_ref)
    gm_ref[...] = gates
    sm_ref[...] = sel


def _full(shape):
    n = len(shape)
    return pl.BlockSpec(shape, lambda *_: (0,) * n)


def _prompt_call(x, sinks, gmix, win, lng, lnb, wsp, bsp, goa, gob, wo, gmoe, wr, br):
    real = N_PROMPT // TM
    row = lambda g: (g, 0)
    seq = lambda g: (jnp.minimum(g, real - 1) // (SEQ // TM), 0, 0)
    return pl.pallas_call(
        _prompt_kernel,
        grid=(N_PAD // TM,),
        in_specs=[
            pl.BlockSpec(memory_space=pltpu.SMEM),
            pl.BlockSpec((TM, D_MODEL), lambda g: (jnp.minimum(g, real - 1), 0)),
            _full((1, D_MODEL)), _full((D_MODEL, IN_WIDTH)), _full((1, A_WIDTH)), _full((1, A_WIDTH)),
            _full((8, CHUNK, CHUNK)), _full((CHUNK, A_WIDTH)), _full((1, A_WIDTH)), _full((1, B_WIDTH)),
            _full((D_MODEL, D_MODEL)), _full((1, D_MODEL)), _full((D_MODEL, LANES)), _full((1, LANES)),
        ],
        out_specs=[
            pl.BlockSpec((TM, D_MODEL), row),
            pl.BlockSpec((TM, D_MODEL), row),
            pl.BlockSpec((TM, LANES), row),
            pl.BlockSpec((TM, LANES), row),
            pl.BlockSpec((None, CHUNK, KV_WIDTH), seq),
            pl.BlockSpec((None, CHUNK, KV_WIDTH), seq),
        ],
        out_shape=[
            jax.ShapeDtypeStruct((N_PAD, D_MODEL), F32),
            jax.ShapeDtypeStruct((N_PAD, D_MODEL), BF16),
            jax.ShapeDtypeStruct((N_PAD, LANES), F32),
            jax.ShapeDtypeStruct((N_PAD, LANES), F32),
            jax.ShapeDtypeStruct((BATCH, CHUNK, KV_WIDTH), F32),
            jax.ShapeDtypeStruct((BATCH, CHUNK, KV_WIDTH), F32),
        ],
        scratch_shapes=[
            pltpu.VMEM((TM, IN_WIDTH), F32),
            pltpu.VMEM((TM + CHUNK, 2 * KV_WIDTH), F32),
            pltpu.VMEM((TM, D_MODEL), BF16),
        ],
        compiler_params=pltpu.CompilerParams(
            dimension_semantics=("arbitrary",), vmem_limit_bytes=VMEM_LIMIT),
        name="prompt_premoe",
    )(sinks, x, gmix, win, lng, lnb, wsp, bsp, goa, gob, wo, gmoe, wr, br)


def _sample_kernel(sinks_ref, x_ref, ck_ref, cv_ref, gmix_ref, win_ref, lng_ref, lnb_ref, w00_ref, b0_ref,
                   goa_ref, gob_ref, wo_ref, gmoe_ref, wr_ref, br_ref,
                   h1_in, xn_in, gm_in, sm_in,
                   h1_ref, xn_ref, gm_ref, sm_ref, nk_ref, nv_ref, va_ref):
    del h1_in, xn_in, gm_in, sm_in
    t = SAMPLE_TILE
    nkeys = t * CHUNK

    if True:
        x = x_ref[...]
        xn = _rms(x, gmix_ref[...]).astype(BF16)
        z = jnp.dot(xn, win_ref[...], preferred_element_type=F32)
        u = _gelu(z[:, 0:A_WIDTH])
        va = _layernorm(_gelu(z[:, A_WIDTH:2 * A_WIDTH]), lng_ref[...], lnb_ref[...])
        va_ref[...] = va
        ya_n = _rms(u * (w00_ref[...] * va + b0_ref[...]), goa_ref[...])

        knew = z[:, 2 * A_WIDTH + B_WIDTH:2 * A_WIDTH + B_WIDTH + KV_WIDTH]
        vnew = z[:, 2 * A_WIDTH + B_WIDTH + KV_WIDTH:]
        lane = _lane_iota((t, LANES))
        lo = lane < HEAD_DIM
        stacked = []
        for h in range(B_HEADS):
            q = z[:, 2 * A_WIDTH + (h // 2) * LANES:2 * A_WIDTH + (h // 2 + 1) * LANES]
            qh = jnp.where(lo if h % 2 == 0 else jnp.logical_not(lo), q, 0.0)
            if h % 2 != h // 4:
                qh = pltpu.roll(qh, HEAD_DIM, 1)
            stacked.append(qh)
        qs = jnp.concatenate(stacked, axis=0)
        rows = B_HEADS * t
        ridx = lax.broadcasted_iota(jnp.int32, (rows, 1), 0)
        slope = jnp.zeros((rows, 1), F32)
        sink = jnp.zeros((rows, 1), F32)
        for h in range(B_HEADS):
            in_h = (ridx >= h * t) & (ridx < (h + 1) * t)
            slope = jnp.where(in_h, 2.0 ** (-(h + 1)), slope)
            sink = jnp.where(in_h, sinks_ref[h], sink)
        s_c = lax.dot_general(qs.astype(BF16), ck_ref[...].astype(BF16), (((1,), (1,)), ((), ())),
                              preferred_element_type=F32)
        rsamp = lax.broadcasted_iota(jnp.int32, (rows, nkeys), 0) % t
        col = lax.broadcasted_iota(jnp.int32, (rows, nkeys), 1)
        pos = col % CHUNK
        own = ((col // CHUNK) == rsamp) & (pos >= 1)
        s_c = s_c * (HEAD_DIM ** -0.5) - slope * (CHUNK - pos).astype(F32)
        s_c = jnp.where(own, s_c, NEG)
        kn8 = jnp.concatenate([knew] * B_HEADS, axis=0)
        vn8 = jnp.concatenate([vnew] * B_HEADS, axis=0)
        s_n = jnp.sum(qs * kn8, axis=-1, keepdims=True) * (HEAD_DIM ** -0.5)
        mx = jnp.maximum(jnp.maximum(jnp.max(s_c, axis=-1, keepdims=True), s_n), sink)
        e_c = jnp.exp(s_c - mx)
        e_n = jnp.exp(s_n - mx)
        inv = 1.0 / (jnp.sum(e_c, axis=-1, keepdims=True) + e_n + jnp.exp(sink - mx))
        o = jnp.dot((e_c * inv).astype(BF16), cv_ref[...].astype(BF16), preferred_element_type=F32)
        o = o + (e_n * inv) * vn8
        yb_slabs = []
        for p in range(B_WIDTH // LANES):
            outs = []
            for half in range(2):
                h = 2 * p + half
                oh = o[h * t:(h + 1) * t]
                oh = jnp.where(lo if h // 4 == 0 else jnp.logical_not(lo), oh, 0.0)
                if half != h // 4:
                    oh = pltpu.roll(oh, HEAD_DIM, 1)
                outs.append(oh)
            yb_slabs.append(outs[0] + outs[1])
        yb_n = _rms(jnp.concatenate(yb_slabs, axis=-1), gob_ref[...])

        cat = jnp.concatenate([ya_n, yb_n], axis=-1).astype(BF16)
        h1 = x + jnp.dot(cat, wo_ref[...], preferred_element_type=F32)
        xn2 = _rms(h1, gmoe_ref[...])
        gates, sel = _route(xn2, wr_ref, br_ref)
        h1_ref[...] = h1
        xn_ref[...] = xn2.astype(BF16)
        gm_ref[...] = gates
        sm_ref[...] = sel

        nk_ref[...] = pltpu.roll(ck_ref[...], nkeys - 1, 0)
        nv_ref[...] = pltpu.roll(cv_ref[...], nkeys - 1, 0)
        for b in range(t):
            nk_ref[b * CHUNK + CHUNK - 1:b * CHUNK + CHUNK, :] = knew[b:b + 1, :]
            nv_ref[b * CHUNK + CHUNK - 1:b * CHUNK + CHUNK, :] = vnew[b:b + 1, :]


def _sample_call(x, ck, cv, sinks, gmix, win, lng, lnb, w00, b0, goa, gob, wo, gmoe, wr, br, h1, xn, gm, sm):
    t = SAMPLE_TILE
    steps = DEC_BATCH // t
    base = N_PROMPT // t
    inrow = lambda i: (i, 0)
    outrow = lambda i: (base + i, 0)
    anyspec = pl.BlockSpec(memory_space=pl.ANY)
    return pl.pallas_call(
        _sample_kernel,
        grid=(steps,),
        in_specs=[
            pl.BlockSpec(memory_space=pltpu.SMEM),
            pl.BlockSpec((t, D_MODEL), inrow),
            pl.BlockSpec((t * CHUNK, KV_WIDTH), inrow),
            pl.BlockSpec((t * CHUNK, KV_WIDTH), inrow),
            _full((1, D_MODEL)), _full((D_MODEL, IN_WIDTH)), _full((1, A_WIDTH)), _full((1, A_WIDTH)),
            _full((1, A_WIDTH)), _full((1, A_WIDTH)), _full((1, A_WIDTH)), _full((1, B_WIDTH)),
            _full((D_MODEL, D_MODEL)), _full((1, D_MODEL)), _full((D_MODEL, LANES)), _full((1, LANES)),
            anyspec, anyspec, anyspec, anyspec,
        ],
        out_specs=[
            pl.BlockSpec((t, D_MODEL), outrow),
            pl.BlockSpec((t, D_MODEL), outrow),
            pl.BlockSpec((t, LANES), outrow),
            pl.BlockSpec((t, LANES), outrow),
            pl.BlockSpec((t * CHUNK, KV_WIDTH), inrow),
            pl.BlockSpec((t * CHUNK, KV_WIDTH), inrow),
            pl.BlockSpec((t, A_WIDTH), inrow),
        ],
        out_shape=[
            jax.ShapeDtypeStruct((N_PAD, D_MODEL), F32),
            jax.ShapeDtypeStruct((N_PAD, D_MODEL), BF16),
            jax.ShapeDtypeStruct((N_PAD, LANES), F32),
            jax.ShapeDtypeStruct((N_PAD, LANES), F32),
            jax.ShapeDtypeStruct((DEC_BATCH * CHUNK, KV_WIDTH), F32),
            jax.ShapeDtypeStruct((DEC_BATCH * CHUNK, KV_WIDTH), F32),
            jax.ShapeDtypeStruct((DEC_BATCH, A_WIDTH), F32),
        ],
        input_output_aliases={16: 0, 17: 1, 18: 2, 19: 3},
        compiler_params=pltpu.CompilerParams(
            dimension_semantics=("arbitrary",), vmem_limit_bytes=VMEM_LIMIT),
        name="sample_premoe",
    )(sinks, x, ck, cv, gmix, win, lng, lnb, w00, b0, goa, gob, wo, gmoe, wr, br, h1, xn, gm, sm)


def _decode_kernel(sinks_ref, x_ref, ck_ref, cv_ref, gmix_ref, win_ref, lng_ref, lnb_ref, w00_ref, b0_ref,
                   goa_ref, gob_ref, wo_ref, gmoe_ref, wr_ref, br_ref,
                   h1_in, xn_in, gm_in, sm_in,
                   h1_ref, xn_ref, gm_ref, sm_ref, nk_ref, nv_ref, va_ref,
                   q_s, kn_s, vn_s, ya_s, yb_s):
    del h1_in, xn_in, gm_in, sm_in
    i = pl.program_id(0)
    t = DEC_TILE
    per_seq = CHUNK * 2
    ncols = t * per_seq

    @pl.when(i == 0)
    def _():
        xn = _rms(x_ref[...], gmix_ref[...]).astype(BF16)
        z = jnp.dot(xn, win_ref[...], preferred_element_type=F32)
        u = _gelu(z[:, 0:A_WIDTH])
        va = _layernorm(_gelu(z[:, A_WIDTH:2 * A_WIDTH]), lng_ref[...], lnb_ref[...])
        va_ref[...] = va
        ya_s[...] = _rms(u * (w00_ref[...] * va + b0_ref[...]), goa_ref[...])
        q_s[...] = z[:, 2 * A_WIDTH:2 * A_WIDTH + B_WIDTH]
        kn_s[...] = z[:, 2 * A_WIDTH + B_WIDTH:2 * A_WIDTH + B_WIDTH + KV_WIDTH]
        vn_s[...] = z[:, 2 * A_WIDTH + B_WIDTH + KV_WIDTH:]

    rows = pl.ds(pl.multiple_of(i * t, t), t)
    q = q_s[rows, :]
    kn = kn_s[rows, :]
    vn = vn_s[rows, :]
    qs = jnp.concatenate([q[:, h * HEAD_DIM:(h + 1) * HEAD_DIM] for h in range(B_HEADS)], axis=0)
    kn8 = jnp.concatenate([kn[:, (h // 4) * HEAD_DIM:(h // 4 + 1) * HEAD_DIM] for h in range(B_HEADS)], axis=0)
    vn8 = jnp.concatenate([vn[:, (h // 4) * HEAD_DIM:(h // 4 + 1) * HEAD_DIM] for h in range(B_HEADS)], axis=0)
    nrows = B_HEADS * t
    ridx = lax.broadcasted_iota(jnp.int32, (nrows, 1), 0)
    slope = jnp.zeros((nrows, 1), F32)
    sink = jnp.zeros((nrows, 1), F32)
    for h in range(B_HEADS):
        in_h = (ridx >= h * t) & (ridx < (h + 1) * t)
        slope = jnp.where(in_h, 2.0 ** (-(h + 1)), slope)
        sink = jnp.where(in_h, sinks_ref[h], sink)
    s_c = lax.dot_general(qs.astype(BF16), ck_ref[...].astype(BF16), (((1,), (1,)), ((), ())),
                          preferred_element_type=F32)
    rr = lax.broadcasted_iota(jnp.int32, (nrows, ncols), 0)
    col = lax.broadcasted_iota(jnp.int32, (nrows, ncols), 1)
    pos = (col // 2) % CHUNK
    own = ((col // per_seq) == (rr % t)) & ((col % 2) == (rr // (4 * t))) & (pos >= 1)
    s_c = s_c * (HEAD_DIM ** -0.5) - slope * (CHUNK - pos).astype(F32)
    s_c = jnp.where(own, s_c, NEG)
    s_n = jnp.sum(qs * kn8, axis=-1, keepdims=True) * (HEAD_DIM ** -0.5)
    mx = jnp.maximum(jnp.maximum(jnp.max(s_c, axis=-1, keepdims=True), s_n), sink)
    e_c = jnp.exp(s_c - mx)
    e_n = jnp.exp(s_n - mx)
    inv = 1.0 / (jnp.sum(e_c, axis=-1, keepdims=True) + e_n + jnp.exp(sink - mx))
    o = jnp.dot((e_c * inv).astype(BF16), cv_ref[...].astype(BF16), preferred_element_type=F32)
    o = o + (e_n * inv) * vn8
    yb_s[rows, :] = jnp.concatenate([o[h * t:(h + 1) * t] for h in range(B_HEADS)], axis=-1)

    nk_ref[...] = pltpu.roll(ck_ref[...], ncols - 2, 0)
    nv_ref[...] = pltpu.roll(cv_ref[...], ncols - 2, 0)
    for b in range(t):
        for kv in range(2):
            r = b * per_seq + per_seq - 2 + kv
            nk_ref[r:r + 1, :] = kn[b:b + 1, kv * HEAD_DIM:(kv + 1) * HEAD_DIM]
            nv_ref[r:r + 1, :] = vn[b:b + 1, kv * HEAD_DIM:(kv + 1) * HEAD_DIM]

    @pl.when(i == pl.num_programs(0) - 1)
    def _():
        yb_n = _rms(yb_s[...], gob_ref[...])
        cat = jnp.concatenate([ya_s[...], yb_n], axis=-1).astype(BF16)
        h1 = x_ref[...] + jnp.dot(cat, wo_ref[...], preferred_element_type=F32)
        xn2 = _rms(h1, gmoe_ref[...])
        gates, sel = _route(xn2, wr_ref, br_ref)
        h1_ref[...] = h1
        xn_ref[...] = xn2.astype(BF16)
        gm_ref[...] = gates
        sm_ref[...] = sel


def _decode_call(x, ck, cv, sinks, gmix, win, lng, lnb, w00, b0, goa, gob, wo, gmoe, wr, br, h1, xn, gm, sm):
    t = DEC_TILE
    per_seq = CHUNK * 2
    cache = pl.BlockSpec((t * per_seq, HEAD_DIM), lambda i: (i, 0))
    tok = lambda width: pl.BlockSpec((DEC_BATCH, width), lambda i: (N_PROMPT // DEC_BATCH, 0))
    anyspec = pl.BlockSpec(memory_space=pl.ANY)
    return pl.pallas_call(
        _decode_kernel,
        grid=(DEC_BATCH // t,),
        in_specs=[
            pl.BlockSpec(memory_space=pltpu.SMEM),
            _full((DEC_BATCH, D_MODEL)), cache, cache,
            _full((1, D_MODEL)), _full((D_MODEL, IN_WIDTH)), _full((1, A_WIDTH)), _full((1, A_WIDTH)),
            _full((1, A_WIDTH)), _full((1, A_WIDTH)), _full((1, A_WIDTH)), _full((1, B_WIDTH)),
            _full((D_MODEL, D_MODEL)), _full((1, D_MODEL)), _full((D_MODEL, LANES)), _full((1, LANES)),
            anyspec, anyspec, anyspec, anyspec,
        ],
        out_specs=[tok(D_MODEL), tok(D_MODEL), tok(LANES), tok(LANES), cache, cache, _full((DEC_BATCH, A_WIDTH))],
        out_shape=[
            jax.ShapeDtypeStruct((N_PAD, D_MODEL), F32),
            jax.ShapeDtypeStruct((N_PAD, D_MODEL), BF16),
            jax.ShapeDtypeStruct((N_PAD, LANES), F32),
            jax.ShapeDtypeStruct((N_PAD, LANES), F32),
            jax.ShapeDtypeStruct((DEC_BATCH * per_seq, HEAD_DIM), F32),
            jax.ShapeDtypeStruct((DEC_BATCH * per_seq, HEAD_DIM), F32),
            jax.ShapeDtypeStruct((DEC_BATCH, A_WIDTH), F32),
        ],
        scratch_shapes=[pltpu.VMEM((DEC_BATCH, B_WIDTH), F32), pltpu.VMEM((DEC_BATCH, KV_WIDTH), F32),
                        pltpu.VMEM((DEC_BATCH, KV_WIDTH), F32), pltpu.VMEM((DEC_BATCH, A_WIDTH), F32),
                        pltpu.VMEM((DEC_BATCH, B_WIDTH), F32)],
        input_output_aliases={16: 0, 17: 1, 18: 2, 19: 3},
        compiler_params=pltpu.CompilerParams(
            dimension_semantics=("arbitrary",), vmem_limit_bytes=VMEM_LIMIT),
        name="sample_premoe",
    )(sinks, x, ck, cv, gmix, win, lng, lnb, w00, b0, goa, gob, wo, gmoe, wr, br, h1, xn, gm, sm)


def _plan_kernel(sm_ref,
                 destm_ref, destt_ref, xbe_ref, live_ref, nxblk_ref, stab_ref, ctab_ref, astart_ref, nwin_ref,
                 base_s, pstart_s):
    ph = pl.program_id(0)
    step = pl.program_id(1)
    lane = _lane_iota((1, LANES))

    @pl.when((ph == 0) & (step == 0))
    def _():
        base_s[...] = jnp.zeros_like(base_s)

    @pl.when(ph == 0)
    def _():
        base_s[...] += jnp.sum(sm_ref[...], axis=0, keepdims=True)

    @pl.when((ph == 1) & (step == 0))
    def _():
        counts = base_s[...]
        padded = jnp.floor((counts + (DISP_CHUNK + EXP_BLOCK - 1)) * (1.0 / EXP_BLOCK)) * EXP_BLOCK
        padded = jnp.where(counts > 0.0, padded, 0.0)
        pend = padded
        for s in (1, 2, 4, 8, 16):
            pend = pend + jnp.where(lane >= s, pltpu.roll(pend, s, 1), 0.0)
        spare = (N_ROWS + lane * DISP_CHUNK).astype(F32)
        pstart_s[...] = jnp.where(counts > 0.0, pend - padded, spare)
        base_s[...] = jnp.zeros_like(base_s)
        brow = lax.broadcasted_iota(jnp.int32, (XBE_ROWS, LANES), 0).astype(F32) * EXP_BLOCK
        done = jnp.where((lane < N_EXPERTS) & (pend <= brow), 1.0, 0.0)
        be = jnp.minimum(jnp.sum(done, axis=-1, keepdims=True), N_EXPERTS - 1.0)
        xbe_ref[...] = jnp.broadcast_to(be, (XBE_ROWS, LANES)).astype(jnp.int32)
        real = jnp.clip(counts - (brow - (pend - padded)), 0.0, float(EXP_BLOCK))
        real = jnp.sum(jnp.where(lane.astype(F32) == be, real, 0.0), axis=-1, keepdims=True)
        groups = jnp.floor((real + (ROW_GROUP - 1)) * (1.0 / ROW_GROUP))
        live_ref[...] = jnp.broadcast_to(groups, (XBE_ROWS, LANES)).astype(jnp.int32)
        total = jnp.sum(jnp.where(lane == N_EXPERTS - 1, pend, 0.0), axis=-1, keepdims=True)
        nxblk_ref[...] = jnp.broadcast_to(total * (1.0 / EXP_BLOCK), (8, LANES)).astype(jnp.int32)
        stab_ref[...] = jnp.zeros_like(stab_ref)
        ctab_ref[...] = jnp.zeros_like(ctab_ref)
        astart_ref[...] = jnp.zeros_like(astart_ref)
        nwin_ref[...] = jnp.zeros_like(nwin_ref)

    @pl.when(ph == 1)
    def _():
        r = lax.broadcasted_iota(jnp.int32, (TOK_TILE, TOK_TILE), 0)
        c = lax.broadcasted_iota(jnp.int32, (TOK_TILE, TOK_TILE), 1)
        lower = jnp.where(c < r, 1.0, 0.0).astype(BF16)
        for q in range(TM // TOK_TILE):
            i = step * (TM // TOK_TILE) + q
            sel = sm_ref[q * TOK_TILE:(q + 1) * TOK_TILE, :]
            cnt = jnp.sum(sel, axis=0, keepdims=True)
            prefix = jnp.dot(lower, sel.astype(BF16), preferred_element_type=F32)
            start = pstart_s[...] + base_s[...]
            dest = jnp.where(sel > 0.0, prefix + start, -1.0)
            destm_ref[q * TOK_TILE:(q + 1) * TOK_TILE, :] = dest
            destt_ref[:, q * TOK_TILE:(q + 1) * TOK_TILE] = dest.T
            has = (cnt > 0.0) & (lane < N_EXPERTS)
            stab_ref[pl.ds(i, 1), :] = start.astype(jnp.int32)
            ctab_ref[pl.ds(i, 1), :] = jnp.where(has, cnt, 0.0).astype(jnp.int32)
            a = jnp.minimum(jnp.floor(start * (1.0 / WIN_ALIGN)) * WIN_ALIGN, float(N_ROWS - WIN))
            nw = jnp.where(has, jnp.floor((start + cnt - a + (WIN - 1)) * (1.0 / WIN)), 0.0)
            astart_ref[pl.ds(i, 1), :] = a.astype(jnp.int32)
            nwin_ref[pl.ds(i, 1), :] = nw.astype(jnp.int32)
            base_s[...] += cnt


def _plan_call(sm):
    tile = lambda ph, i: (i * ph, 0)
    tile_t = lambda ph, i: (0, i * ph)
    tab = jax.ShapeDtypeStruct((TAB_ROWS, LANES), jnp.int32)
    return pl.pallas_call(
        _plan_kernel,
        grid=(2, N_PAD // TM),
        in_specs=[pl.BlockSpec((TM, LANES), lambda ph, i: (i, 0))],
        out_specs=[
            pl.BlockSpec((TM, LANES), tile),
            pl.BlockSpec((LANES, TM), tile_t),
            _full((XBE_ROWS, LANES)), _full((XBE_ROWS, LANES)), _full((8, LANES)),
            _full((TAB_ROWS, LANES)), _full((TAB_ROWS, LANES)), _full((TAB_ROWS, LANES)), _full((TAB_ROWS, LANES)),
        ],
        out_shape=[
            jax.ShapeDtypeStruct((N_PAD, LANES), F32),
            jax.ShapeDtypeStruct((LANES, N_PAD), F32),
            jax.ShapeDtypeStruct((XBE_ROWS, LANES), jnp.int32),
            jax.ShapeDtypeStruct((XBE_ROWS, LANES), jnp.int32),
            jax.ShapeDtypeStruct((8, LANES), jnp.int32),
            tab, tab, tab, tab,
        ],
        scratch_shapes=[pltpu.VMEM((1, LANES), F32), pltpu.VMEM((1, LANES), F32)],
        compiler_params=pltpu.CompilerParams(
            dimension_semantics=("arbitrary", "arbitrary"), vmem_limit_bytes=VMEM_LIMIT),
        name="moe_plan",
    )(sm)


def _pack_rows(z):
    half = D_MODEL // 2
    lo = lax.bitcast_convert_type(z[:, :half], jnp.uint32) >> 16
    hi = lax.bitcast_convert_type(z[:, half:], jnp.uint32) & jnp.uint32(0xFFFF0000)
    return lax.bitcast_convert_type(hi | lo, jnp.int32)


def _unpack_rows(ref, rows=None):
    rows = ref.shape[0] if rows is None else rows
    flat = ref.reshape(ref.shape[0] * PACK, LANES)
    lo, hi = [], []
    for s in range(PACK):
        w = lax.bitcast_convert_type(flat[pl.ds(s, rows, stride=PACK), :], jnp.uint32)
        lo.append(lax.bitcast_convert_type(w << 16, F32))
        hi.append(lax.bitcast_convert_type(w & jnp.uint32(0xFFFF0000), F32))
    return jnp.concatenate(lo + hi, axis=-1).astype(BF16)


def _dispatch_kernel(stab_ref, ctab_ref, cmax_ref, xn_ref, destt_ref, xs_in, xs_ref,
                     stage0, stage1, stage2, sems, sem2):
    del xs_in
    i = pl.program_id(0)
    last = pl.num_programs(0) - 1
    x = xn_ref[...]
    dt = destt_ref[...]
    rio = lax.broadcasted_iota(jnp.int32, (DISP_CHUNK, 1), 0).astype(F32)

    def chunk_rows(j, stage):
        parts = []
        for e in range(N_EXPERTS):
            first = (stab_ref[i * N_EXPERTS + e] + j * DISP_CHUNK).astype(F32)
            parts.append(jnp.where(dt[e:e + 1, :] == first + rio, 1.0, 0.0).astype(BF16))
        onehot = jnp.concatenate(parts, axis=0)
        words = _pack_rows(jnp.dot(onehot, x, preferred_element_type=F32))
        for s in range(PACK):
            stage[pl.ds(s, N_EXPERTS * DISP_CHUNK, stride=PACK), :] = words[:, s * LANES:(s + 1) * LANES]

    def copy(stage, step, e, j, sem):
        first = stab_ref[step * N_EXPERTS + e] + j * DISP_CHUNK
        rows = stage.reshape(N_EXPERTS * DISP_CHUNK, PACK, LANES)
        return pltpu.make_async_copy(rows.at[pl.ds(e * DISP_CHUNK, DISP_CHUNK)],
                                     xs_ref.at[pl.ds(first, DISP_CHUNK)], sem)

    def step_body(stage, prev_stage, par):
        chunk_rows(0, stage)

        @pl.when(i > 0)
        def _():
            for e in range(N_EXPERTS):
                copy(prev_stage, i - 1, e, 0, sems.at[1 - par, 0]).wait()

        for e in range(N_EXPERTS):
            copy(stage, i, e, 0, sems.at[par, 0]).start()

        @pl.when(i == last)
        def _():
            for e in range(N_EXPERTS):
                copy(stage, i, e, 0, sems.at[par, 0]).wait()

    @pl.when(i % 2 == 0)
    def _():
        step_body(stage0, stage1, 0)

    @pl.when(i % 2 == 1)
    def _():
        step_body(stage1, stage0, 1)

    for j in range(1, TOK_TILE // DISP_CHUNK):

        @pl.when(cmax_ref[i] > j * DISP_CHUNK)
        def _(j=j):
            chunk_rows(j, stage2)
            for e in range(N_EXPERTS):

                @pl.when(ctab_ref[i * N_EXPERTS + e] > j * DISP_CHUNK)
                def _(e=e):
                    cp = copy(stage2, i, e, j, sem2)
                    cp.start()
                    cp.wait()


def _dispatch_call(stab, ctab, cmax, xn, destt, xs_zero):
    stage = pltpu.VMEM((N_EXPERTS * DISP_CHUNK * PACK, LANES), jnp.int32)
    grid_spec = pltpu.PrefetchScalarGridSpec(
        num_scalar_prefetch=3,
        grid=(N_PAD_TILES,),
        in_specs=[
            pl.BlockSpec((TOK_TILE, D_MODEL), lambda i, *_: (i, 0)),
            pl.BlockSpec((N_EXPERTS, TOK_TILE), lambda i, *_: (0, i)),
            pl.BlockSpec(memory_space=pl.ANY),
        ],
        out_specs=pl.BlockSpec(memory_space=pl.ANY),
        scratch_shapes=[stage, stage, stage, pltpu.SemaphoreType.DMA((2, N_EXPERTS)),
                        pltpu.SemaphoreType.DMA],
    )
    return pl.pallas_call(
        _dispatch_kernel,
        grid_spec=grid_spec,
        out_shape=jax.ShapeDtypeStruct((XS_ROWS, PACK, LANES), jnp.int32),
        input_output_aliases={5: 0},
        compiler_params=pltpu.CompilerParams(
            dimension_semantics=("arbitrary",), vmem_limit_bytes=VMEM_LIMIT),
        name="moe_dispatch",
    )(stab, ctab, cmax, xn, destt, xs_zero)


def _expert_kernel(blke_ref, nblk_ref, live_ref, xs_ref, wgu_hbm, bgu_ref, wdn_hbm, bdn_ref,
                   ys_ref, wgu_f, wdn_f, wgu_s, wdn_s, sems):
    b = pl.program_id(0)
    used = b < nblk_ref[0]
    prev = blke_ref[jnp.maximum(b - 1, 0)]
    fresh = used & ((b == 0) | (blke_ref[b] != prev))

    def fetch(e):
        return (pltpu.make_async_copy(wgu_hbm.at[e], wgu_f, sems.at[0]),
                pltpu.make_async_copy(wdn_hbm.at[e], wdn_f, sems.at[1]))

    @pl.when(b == 0)
    def _():
        for cp in fetch(blke_ref[0]):
            cp.start()

    @pl.when(fresh)
    def _():
        for cp in fetch(blke_ref[b]):
            cp.wait()
        wgu_s[...] = wgu_f[...].astype(BF16)
        wdn_s[...] = wdn_f[...].astype(BF16)

        nxt = lax.while_loop(lambda p: (p < nblk_ref[0]) & (blke_ref[jnp.minimum(p, N_XBLOCKS - 1)] == blke_ref[b]),
                             lambda p: p + 1, b + 1)

        @pl.when(nxt < nblk_ref[0])
        def _():
            for cp in fetch(blke_ref[jnp.minimum(nxt, N_XBLOCKS - 1)]):
                cp.start()

    for groups in range(1, EXP_BLOCK // ROW_GROUP + 1):
        rows = groups * ROW_GROUP

        @pl.when(used & (live_ref[b] == groups))
        def _(rows=rows):
            hid = jnp.dot(_unpack_rows(xs_ref, rows), wgu_s[...], preferred_element_type=F32) + bgu_ref[...]
            gate = jnp.minimum(hid[:, :D_FF], SWIGLU_LIMIT)
            up = jnp.clip(hid[:, D_FF:], -SWIGLU_LIMIT, SWIGLU_LIMIT)
            act = (up + 1.0) * gate * jax.nn.sigmoid(SWIGLU_ALPHA * gate)
            y = jnp.dot(act.astype(BF16), wdn_s[...], preferred_element_type=F32) + bdn_ref[...]
            ys_ref[0:rows, :] = y.astype(BF16)
            if rows < EXP_BLOCK:
                ys_ref[rows:, :] = jnp.zeros((EXP_BLOCK - rows, D_MODEL), BF16)

    @pl.when(jnp.logical_not(used) | (live_ref[b] == 0))
    def _():
        ys_ref[...] = jnp.zeros_like(ys_ref)


def _expert_call(blke, nblk, live, xs, wgu, bgu, wdn, bdn):
    grid_spec = pltpu.PrefetchScalarGridSpec(
        num_scalar_prefetch=3,
        grid=(N_XBLOCKS,),
        in_specs=[
            pl.BlockSpec((EXP_BLOCK, PACK, LANES), lambda b, be, *_: (b, 0, 0)),
            pl.BlockSpec(memory_space=pl.ANY),
            pl.BlockSpec((None, 1, 2 * D_FF), lambda b, be, *_: (be[b], 0, 0)),
            pl.BlockSpec(memory_space=pl.ANY),
            pl.BlockSpec((None, 1, D_MODEL), lambda b, be, *_: (be[b], 0, 0)),
        ],
        out_specs=pl.BlockSpec((EXP_BLOCK, D_MODEL), lambda b, be, *_: (b, 0)),
        scratch_shapes=[pltpu.VMEM((D_MODEL, 2 * D_FF), F32), pltpu.VMEM((D_FF, D_MODEL), F32),
                        pltpu.VMEM((D_MODEL, 2 * D_FF), BF16), pltpu.VMEM((D_FF, D_MODEL), BF16),
                        pltpu.SemaphoreType.DMA((2,))],
    )
    return pl.pallas_call(
        _expert_kernel,
        grid_spec=grid_spec,
        out_shape=jax.ShapeDtypeStruct((N_ROWS, D_MODEL), BF16),
        compiler_params=pltpu.CompilerParams(
            dimension_semantics=("arbitrary",), vmem_limit_bytes=VMEM_LIMIT),
        name="moe_experts",
    )(blke, nblk, live, xs, wgu, bgu, wdn, bdn)


def _combine_kernel(*refs):
    astart_ref, nwin_ref, over_ref = refs[0:3]
    win_refs = refs[3:3 + N_EXPERTS]
    (destm_ref, gm_ref, h1_ref, plep_ref, ples_ref, gple_ref, wpg_ref, wpp_ref, gfin_ref, ys_any,
     yp_ref, ysm_ref, moe_s, tmp_s, sem) = refs[3 + N_EXPERTS:]
    i = pl.program_id(0)
    dest = destm_ref[...]
    gates = gm_ref[...]
    lane = _lane_iota((COMB_TILE, LANES))
    lane_f = lane.astype(F32)
    lo = lane < WIN
    moe = jnp.zeros((COMB_TILE, D_MODEL), F32)
    group = 4
    for g0 in range(0, N_EXPERTS, group):
        gsel = []
        for p in range(group // 2):
            e0 = g0 + 2 * p
            a0 = astart_ref[i * N_EXPERTS + e0].astype(F32)
            a1 = astart_ref[i * N_EXPERTS + e0 + 1].astype(F32)
            rowid = jnp.where(lo, a0 + lane_f, a1 + lane_f - WIN)
            dcol = jnp.where(lo, dest[:, e0:e0 + 1], dest[:, e0 + 1:e0 + 2])
            gcol = jnp.where(lo, gates[:, e0:e0 + 1], gates[:, e0 + 1:e0 + 2])
            gsel.append(jnp.where(dcol == rowid, gcol, 0.0).astype(BF16))
        ywin = jnp.concatenate([win_refs[g0 + q][...] for q in range(group)], axis=0)
        moe = moe + jnp.dot(jnp.concatenate(gsel, axis=-1), ywin, preferred_element_type=F32)
    mrows = pl.ds(pl.multiple_of((i % TAIL_TILES) * COMB_TILE, COMB_TILE), COMB_TILE)
    moe_s[mrows, :] = moe

    @pl.when(over_ref[i] > 0)
    def _():
        tmp_s[...] = jnp.zeros_like(tmp_s)

        def per_expert(e, carry):
            a = astart_ref[i * N_EXPERTS + e]
            dcol = jnp.sum(jnp.where(lane == e, dest, 0.0), axis=-1, keepdims=True)
            gcol = jnp.sum(jnp.where(lane == e, gates, 0.0), axis=-1, keepdims=True)

            def per_window(w, carry2):
                first = a + w * WIN
                start = pl.multiple_of(jnp.minimum(first, N_ROWS - WIN), WIN_ALIGN)
                cp = pltpu.make_async_copy(ys_any.at[pl.ds(start, WIN)], tmp_s.at[pl.ds(0, WIN)], sem)
                cp.start()
                cp.wait()
                hit = lo & (dcol == start.astype(F32) + lane_f) & (dcol >= first.astype(F32))
                gsel = jnp.where(hit, gcol, 0.0).astype(BF16)
                moe_s[mrows, :] += jnp.dot(gsel, tmp_s[...], preferred_element_type=F32)
                return carry2

            return lax.fori_loop(1, nwin_ref[i * N_EXPERTS + e], per_window, carry)

        lax.fori_loop(0, N_EXPERTS, per_expert, 0)

    def tail(h1, moe_rows, ple):
        h2 = h1 + moe_rows
        hn = _rms(h2, gple_ref[...]).astype(BF16)
        gate = jax.nn.sigmoid(jnp.dot(hn, wpg_ref[...], preferred_element_type=F32))
        proj = jnp.dot(ple.astype(BF16), wpp_ref[...], preferred_element_type=F32)
        return _rms(h2 + gate * proj, gfin_ref[...])

    is_sample = i == N_CTILES - 1

    @pl.when((i % TAIL_TILES == TAIL_TILES - 1) & jnp.logical_not(is_sample))
    def _():
        yp_ref[...] = tail(h1_ref[...], moe_s[...], plep_ref[...])

    @pl.when(is_sample)
    def _():
        ysm_ref[...] = tail(h1_ref[0:DEC_BATCH, :], moe_s[0:DEC_BATCH, :], ples_ref[...])


def _combine_call(astart, nwin, over, ys, destm, gm, h1, plep, ples, gple, wpg, wpp, gfin):
    rows = TAIL_TILES * COMB_TILE
    last = N_PROMPT // rows - 1

    def win_spec(e):
        return pl.BlockSpec((pl.Element(WIN), pl.Element(D_MODEL)),
                            lambda i, a, nw, ov, e=e: (pl.multiple_of(a[i * N_EXPERTS + e], WIN_ALIGN), 0))

    grid_spec = pltpu.PrefetchScalarGridSpec(
        num_scalar_prefetch=3,
        grid=(N_CTILES,),
        in_specs=[win_spec(e) for e in range(N_EXPERTS)] + [
            pl.BlockSpec((COMB_TILE, LANES), lambda i, *_: (i, 0)),
            pl.BlockSpec((COMB_TILE, LANES), lambda i, *_: (i, 0)),
            pl.BlockSpec((rows, D_MODEL), lambda i, *_: (i // TAIL_TILES, 0)),
            pl.BlockSpec((rows, PLE_DIM), lambda i, *_: (jnp.minimum(i // TAIL_TILES, last), 0)),
            pl.BlockSpec((DEC_BATCH, PLE_DIM), lambda i, *_: (0, 0)),
            pl.BlockSpec((1, D_MODEL), lambda i, *_: (0, 0)),
            pl.BlockSpec((D_MODEL, D_MODEL), lambda i, *_: (0, 0)),
            pl.BlockSpec((PLE_DIM, D_MODEL), lambda i, *_: (0, 0)),
            pl.BlockSpec((1, D_MODEL), lambda i, *_: (0, 0)),
            pl.BlockSpec(memory_space=pl.ANY),
        ],
        out_specs=[
            pl.BlockSpec((rows, D_MODEL), lambda i, *_: (jnp.minimum(i // TAIL_TILES, last), 0)),
            pl.BlockSpec((DEC_BATCH, D_MODEL), lambda i, *_: (0, 0)),
        ],
        scratch_shapes=[pltpu.VMEM((rows, D_MODEL), F32), pltpu.VMEM((2 * WIN, D_MODEL), BF16),
                        pltpu.SemaphoreType.DMA],
    )
    return pl.pallas_call(
        _combine_kernel,
        grid_spec=grid_spec,
        out_shape=[jax.ShapeDtypeStruct((N_PROMPT, D_MODEL), F32),
                   jax.ShapeDtypeStruct((DEC_BATCH, D_MODEL), F32)],
        compiler_params=pltpu.CompilerParams(
            dimension_semantics=("arbitrary",), vmem_limit_bytes=VMEM_LIMIT),
        name="moe_combine_tail",
    )(astart, nwin, over, *([ys] * N_EXPERTS), destm, gm, h1, plep, ples, gple, wpg, wpp, gfin, ys)


def _gather_kernel(*refs):
    astart_ref, nwin_ref, over_ref = refs[0:3]
    win_refs = refs[3:3 + N_EXPERTS]
    (destm_ref, gm_ref, h1_ref, plep_ref, ples_ref, gple_ref, wpg_ref, wpp_ref, gfin_ref, ys_any,
     yp_ref, ysm_ref, moe_s, tmp_s, sem) = refs[3 + N_EXPERTS:]
    i = pl.program_id(0)
    slot = i % TAIL_TILES
    dest = destm_ref[...]
    gates = gm_ref[...]
    lane = _lane_iota((TOK_TILE, LANES))
    lane_f = lane.astype(F32)
    per_slab = LANES // CWIN
    within = (lane % CWIN).astype(F32)
    group = 2 * per_slab

    def split(gsel):
        hi = gsel.astype(BF16)
        return hi, (gsel - hi.astype(F32)).astype(BF16)

    moe = jnp.zeros((TOK_TILE, D_MODEL), F32)
    for g0 in range(0, N_EXPERTS, group):
        his, los = [], []
        for sl in range(2):
            rowid = jnp.zeros((TOK_TILE, LANES), F32)
            dcol = jnp.zeros((TOK_TILE, LANES), F32)
            gcol = jnp.zeros((TOK_TILE, LANES), F32)
            for q in range(per_slab):
                e = g0 + sl * per_slab + q
                mine = (lane >= q * CWIN) & (lane < (q + 1) * CWIN)
                rowid = jnp.where(mine, astart_ref[i * N_EXPERTS + e].astype(F32) + within, rowid)
                dcol = jnp.where(mine, dest[:, e:e + 1], dcol)
                gcol = jnp.where(mine, gates[:, e:e + 1], gcol)
            hi, lo = split(jnp.where(dcol == rowid, gcol, 0.0))
            his.append(hi)
            los.append(lo)
        ywin = jnp.concatenate([_unpack_rows(win_refs[g0 + q]) for q in range(group)], axis=0)
        both = jnp.concatenate([jnp.concatenate(his, axis=-1), jnp.concatenate(los, axis=-1)], axis=0)
        r = jnp.dot(both, ywin, preferred_element_type=F32)
        moe = moe + r[:TOK_TILE] + r[TOK_TILE:]
    mrows = pl.ds(pl.multiple_of(slot * TOK_TILE, TOK_TILE), TOK_TILE)
    moe_s[mrows, :] = moe

    @pl.when(over_ref[i] > 0)
    def _():
        def per_expert(e, carry):
            a = astart_ref[i * N_EXPERTS + e]
            dcol = jnp.sum(jnp.where(lane == e, dest, 0.0), axis=-1, keepdims=True)
            gcol = jnp.sum(jnp.where(lane == e, gates, 0.0), axis=-1, keepdims=True)

            def per_window(w, carry2):
                first = a + w * CWIN
                start = jnp.minimum(first, N_ROWS - CWIN)
                cp = pltpu.make_async_copy(ys_any.at[pl.ds(start, CWIN)], tmp_s, sem)
                cp.start()
                cp.wait()
                hit = (lane < CWIN) & (dcol == start.astype(F32) + lane_f) & (dcol >= first.astype(F32))
                hi, lo = split(jnp.where(hit, gcol, 0.0))
                rows = jnp.concatenate([_unpack_rows(tmp_s), jnp.zeros((LANES - CWIN, D_MODEL), BF16)], axis=0)
                moe_s[mrows, :] += (jnp.dot(hi, rows, preferred_element_type=F32)
                                    + jnp.dot(lo, rows, preferred_element_type=F32))
                return carry2

            return lax.fori_loop(1, nwin_ref[i * N_EXPERTS + e], per_window, carry)

        lax.fori_loop(0, N_EXPERTS, per_expert, 0)

    def tail(h1, moe_rows, ple):
        h2 = h1 + moe_rows
        hn = _rms(h2, gple_ref[...]).astype(BF16)
        gate = jax.nn.sigmoid(jnp.dot(hn, wpg_ref[...], preferred_element_type=F32))
        proj = jnp.dot(ple.astype(BF16), wpp_ref[...], preferred_element_type=F32)
        return _rms(h2 + gate * proj, gfin_ref[...])

    is_sample = i == N_TILES - 1

    @pl.when((slot == TAIL_TILES - 1) & jnp.logical_not(is_sample))
    def _():
        yp_ref[...] = tail(h1_ref[...], moe_s[...], plep_ref[...])

    @pl.when(is_sample)
    def _():
        ysm_ref[...] = tail(h1_ref[0:TOK_TILE, :], moe_s[0:TOK_TILE, :], ples_ref[...])


def _gather_call(astart, nwin, over, ys, destm, gm, h1, plep, ples, gple, wpg, wpp, gfin):
    rows = TAIL_TILES * TOK_TILE
    last = N_PROMPT // rows - 1

    def win_spec(e):
        return pl.BlockSpec((pl.Element(CWIN), pl.Element(PACK), pl.Element(LANES)),
                            lambda i, a, nw, ov, e=e: (a[i * N_EXPERTS + e], 0, 0))

    grid_spec = pltpu.PrefetchScalarGridSpec(
        num_scalar_prefetch=3,
        grid=(N_CTILES,),
        in_specs=[win_spec(e) for e in range(N_EXPERTS)] + [
            pl.BlockSpec((COMB_TILE, LANES), lambda i, *_: (i, 0)),
            pl.BlockSpec((COMB_TILE, LANES), lambda i, *_: (i, 0)),
            pl.BlockSpec((rows, D_MODEL), lambda i, *_: (i // TAIL_TILES, 0)),
            pl.BlockSpec((rows, PLE_DIM), lambda i, *_: (jnp.minimum(i // TAIL_TILES, last), 0)),
            pl.BlockSpec((DEC_BATCH, PLE_DIM), lambda i, *_: (0, 0)),
            pl.BlockSpec((1, D_MODEL), lambda i, *_: (0, 0)),
            pl.BlockSpec((D_MODEL, D_MODEL), lambda i, *_: (0, 0)),
            pl.BlockSpec((PLE_DIM, D_MODEL), lambda i, *_: (0, 0)),
            pl.BlockSpec((1, D_MODEL), lambda i, *_: (0, 0)),
            pl.BlockSpec(memory_space=pl.ANY),
        ],
        out_specs=[
            pl.BlockSpec((rows, D_MODEL), lambda i, *_: (jnp.minimum(i // TAIL_TILES, last), 0)),
            pl.BlockSpec((TOK_TILE, D_MODEL), lambda i, *_: (0, 0)),
        ],
        scratch_shapes=[pltpu.VMEM((rows, D_MODEL), F32), pltpu.VMEM((CWIN, PACK, LANES), jnp.int32),
                        pltpu.SemaphoreType.DMA],
    )
    return pl.pallas_call(
        _gather_kernel,
        grid_spec=grid_spec,
        out_shape=[jax.ShapeDtypeStruct((N_PROMPT, D_MODEL), F32),
                   jax.ShapeDtypeStruct((DEC_BATCH, D_MODEL), F32)],
        compiler_params=pltpu.CompilerParams(
            dimension_semantics=("arbitrary",), vmem_limit_bytes=VMEM_LIMIT),
        name="moe_combine_tail",
    )(astart, nwin, over, *([ys] * N_EXPERTS), destm, gm, h1, plep, ples, gple, wpg, wpp, gfin, ys)


def kernel(x_prompt, x_sample, cache_swa_k, cache_swa_v, p_prompt, p_sample, g_mix, w_in, ln_v_g, ln_v_b,
           w_sp, b_sp, sinks, g_out_a, g_out_b, w_o, g_moe, w_router, b_router, w_gu, b_gu, w_dn, b_dn,
           g_ple, w_ple_gate, w_ple_proj, g_final):
    l = 0
    row = lambda v: v.reshape(1, -1)
    win = w_in[l].astype(BF16)
    wo = w_o[l].astype(BF16)
    tril = jnp.tril(jnp.ones((CHUNK, CHUNK), dtype=bool))
    wsp = jnp.where(tril, w_sp[l], 0.0).astype(BF16)
    bsp = jnp.repeat(b_sp[l].T, HEAD_DIM, axis=1)
    w00 = row(jnp.repeat(w_sp[l][:, 0, 0], HEAD_DIM))
    b0 = row(jnp.repeat(b_sp[l][:, 0], HEAD_DIM))
    wr_hi = w_router[l].astype(BF16)
    wr_lo = (w_router[l] - wr_hi.astype(F32)).astype(BF16)
    wr = jnp.concatenate([wr_hi, wr_lo, jnp.zeros((D_MODEL, LANES - 2 * N_EXPERTS), BF16)], axis=1)
    br = row(jnp.concatenate([b_router[l], jnp.zeros((LANES - N_EXPERTS,), F32)]))
    common = (row(g_mix[l]), win, row(ln_v_g[l]), row(ln_v_b[l]))
    tail = (row(g_out_a[l]), row(g_out_b[l]), wo, row(g_moe[l]), wr, br)

    h1, xn, gm, sm, k_p, v_p = _prompt_call(
        x_prompt.reshape(N_PROMPT, D_MODEL), sinks[l], *common, wsp, bsp, *tail)
    ck = cache_swa_k[l].reshape(DEC_BATCH * CHUNK * 2, HEAD_DIM)
    cv = cache_swa_v[l].reshape(DEC_BATCH * CHUNK * 2, HEAD_DIM)
    h1, xn, gm, sm, k_s, v_s, va_s = _decode_call(
        x_sample.reshape(DEC_BATCH, D_MODEL), ck, cv, sinks[l], *common, w00, b0, *tail, h1, xn, gm, sm)

    destm, destt, xbe, live, nxblk, stab, ctab, astart, nwin = _plan_call(sm)
    flat = lambda tab, n: tab[:n, :N_EXPERTS].reshape(-1)
    del astart, nwin
    per = COMB_TILE // TOK_TILE
    first = stab[0:per * N_CTILES:per, :N_EXPERTS]
    count = sum(ctab[q:per * N_CTILES:per, :N_EXPERTS] for q in range(per))
    astart2 = jnp.minimum(first // WIN_ALIGN * WIN_ALIGN, N_ROWS - WIN)
    nwin2 = jnp.where(count > 0, (first + count - astart2 + WIN - 1) // WIN, 0)
    astart1 = astart2.reshape(-1)
    over1 = (jnp.max(nwin2, axis=1) > 1).astype(jnp.int32)
    nwin1 = nwin2.reshape(-1)
    cmax1 = jnp.max(ctab[:N_PAD_TILES, :N_EXPERTS], axis=1)

    xs = _dispatch_call(flat(stab, N_PAD_TILES), flat(ctab, N_PAD_TILES), cmax1, xn, destt[:N_EXPERTS],
                        jnp.zeros((XS_ROWS, PACK, LANES), jnp.int32))
    ys = _expert_call(xbe[:N_XBLOCKS, 0], nxblk[0, :1], live[:N_XBLOCKS, 0], xs, w_gu[l], b_gu[l].reshape(N_EXPERTS, 1, 2 * D_FF),
                      w_dn[l], b_dn[l].reshape(N_EXPERTS, 1, D_MODEL))
    y_p, y_s = _combine_call(
        astart1, nwin1, over1, ys, destm, gm, h1,
        p_prompt[l].reshape(N_PROMPT, PLE_DIM), p_sample[l].reshape(DEC_BATCH, PLE_DIM),
        row(g_ple[l]), w_ple_gate[l].astype(BF16), w_ple_proj[l].astype(BF16), row(g_final))

    kv5 = lambda a, n: a.reshape(1, n, CHUNK, 2, HEAD_DIM)
    return (y_p.reshape(BATCH, SEQ, D_MODEL), y_s.reshape(DEC_BATCH, 1, D_MODEL),
            kv5(k_p, BATCH), kv5(v_p, BATCH), kv5(k_s, DEC_BATCH), kv5(v_s, DEC_BATCH),
            va_s.reshape(1, DEC_BATCH, 1, A_WIDTH))
```

```python
import math

import jax
import jax.numpy as jnp
from jax import lax
from jax.experimental import pallas as pl
from jax.experimental.pallas import tpu as pltpu

F32 = jnp.float32
BF16 = jnp.bfloat16

D_MODEL = 1024
BATCH = 4
SEQ = 4096
DEC_BATCH = 128
HEAD_DIM = 64
A_WIDTH = 512
B_WIDTH = 512
B_HEADS = 8
KV_WIDTH = 128
IN_WIDTH = 2 * A_WIDTH + B_WIDTH + 2 * KV_WIDTH
CHUNK = 128
N_EXPERTS = 32
TOP_K = 4
D_FF = 1024
SWIGLU_ALPHA = 1.702
SWIGLU_LIMIT = 7.0
PLE_DIM = 256
EPS = 1e-5

LANES = 128
ROW_GROUP = 128
EXP_BLOCK = 512
N_PROMPT = BATCH * SEQ
N_TOK = N_PROMPT + DEC_BATCH
TOK_TILE = 128
N_TILES = N_TOK // TOK_TILE
DISP_TILE = 256
TM = 512
N_PAD = ((N_TOK + TM - 1) // TM) * TM
N_PAD_TILES = N_PAD // TOK_TILE
DISP_CHUNK = 32
N_XBLOCKS = (N_TOK * TOP_K + N_EXPERTS * (DISP_CHUNK + EXP_BLOCK - 1) + EXP_BLOCK - 1) // EXP_BLOCK
N_ROWS = N_XBLOCKS * EXP_BLOCK
XS_ROWS = N_ROWS + N_EXPERTS * DISP_CHUNK
XBE_ROWS = ((N_XBLOCKS + 7) // 8) * 8
TAB_ROWS = ((N_PAD_TILES + 7) // 8) * 8
PACK = D_MODEL // 2 // LANES
WIN = 64
WIN_ALIGN = 16
CWIN = 32
COMB_TILE = 256
N_CTILES = N_PROMPT // COMB_TILE + 1
TAIL_TILES = 2
SAMPLE_TILE = 32
DEC_TILE = 16
NEG = -1e30
VMEM_LIMIT = 56 * 1024 * 1024


def _rms(x, g):
    return x * lax.rsqrt(jnp.mean(x * x, axis=-1, keepdims=True) + EPS) * g


def _gelu(x):
    c = math.sqrt(2.0 / math.pi)
    return x * (0.5 * (1.0 + jnp.tanh(c * (x + 0.044715 * (x * x * x)))))


def _layernorm(x, g, b):
    mu = jnp.mean(x, axis=-1, keepdims=True)
    xc = x - mu
    return xc * lax.rsqrt(jnp.mean(xc * xc, axis=-1, keepdims=True) + EPS) * g + b


def _lane_iota(shape):
    return lax.broadcasted_iota(jnp.int32, shape, len(shape) - 1)


def _route(xn2, wr_ref, br_ref):
    m = xn2.shape[0]
    xh = xn2.astype(BF16)
    xl = (xn2 - xh.astype(F32)).astype(BF16)
    r = jnp.dot(jnp.concatenate([xh, xl], axis=0), wr_ref[...], preferred_element_type=F32)
    r = r[:m] + r[m:]
    lane = _lane_iota((m, LANES))
    lane_f = lane.astype(F32)
    logits = jnp.where(lane < N_EXPERTS, r + pltpu.roll(r, LANES - N_EXPERTS, 1) + br_ref[...], NEG)
    work = logits
    sel = jnp.zeros((m, LANES), F32)
    top = None
    z = None
    for _ in range(TOP_K):
        mx = jnp.max(work, axis=-1, keepdims=True)
        first = jnp.min(jnp.where(work == mx, lane_f, float(LANES)), axis=-1, keepdims=True)
        hit = lane_f == first
        sel = jnp.where(hit, 1.0, sel)
        work = jnp.where(hit, NEG, work)
        if top is None:
            top = mx
            z = jnp.ones_like(mx)
        else:
            z = z + jnp.exp(mx - top)
    gates = jnp.where(sel > 0.0, jnp.exp(logits - top) / z, 0.0)
    return gates, sel


def _prompt_kernel(sinks_ref, x_ref, gmix_ref, win_ref, lng_ref, lnb_ref, wsp_ref, bsp_ref,
                   goa_ref, gob_ref, wo_ref, gmoe_ref, wr_ref, br_ref,
                   h1_ref, xn_ref, gm_ref, sm_ref, k_ref, v_ref,
                   z_s, kv_s, cat_s):
    g = pl.program_id(0)
    j = g % (SEQ // TM)

    @pl.when(g >= N_PROMPT // TM)
    def _():
        h1_ref[...] = jnp.zeros_like(h1_ref)
        xn_ref[...] = jnp.zeros_like(xn_ref)
        gm_ref[...] = jnp.zeros_like(gm_ref)
        sm_ref[...] = jnp.zeros_like(sm_ref)

    @pl.when(g < N_PROMPT // TM)
    def _():
        _prompt_tile(j, sinks_ref, x_ref, gmix_ref, win_ref, lng_ref, lnb_ref, wsp_ref, bsp_ref,
                     goa_ref, gob_ref, wo_ref, gmoe_ref, wr_ref, br_ref,
                     h1_ref, xn_ref, gm_ref, sm_ref, k_ref, v_ref, z_s, kv_s, cat_s)


def _prompt_tile(j, sinks_ref, x_ref, gmix_ref, win_ref, lng_ref, lnb_ref, wsp_ref, bsp_ref,
                 goa_ref, gob_ref, wo_ref, gmoe_ref, wr_ref, br_ref,
                 h1_ref, xn_ref, gm_ref, sm_ref, k_ref, v_ref, z_s, kv_s, cat_s):
    @pl.when(j == 0)
    def _():
        kv_s[0:CHUNK, :] = jnp.zeros((CHUNK, 2 * KV_WIDTH), F32)

    xn = _rms(x_ref[...], gmix_ref[...]).astype(BF16)
    z_s[...] = jnp.dot(xn, win_ref[...], preferred_element_type=F32)
    kv_s[CHUNK:, :] = z_s[:, 2 * A_WIDTH + B_WIDTH:]

    lane = _lane_iota((CHUNK, LANES))
    lo = lane < HEAD_DIM
    lane2 = _lane_iota((2 * CHUNK, LANES))
    lo2 = lane2 < HEAD_DIM
    qi = lax.broadcasted_iota(jnp.int32, (CHUNK, CHUNK), 0)
    kc = lax.broadcasted_iota(jnp.int32, (CHUNK, CHUNK), 1)
    from_prev = kc > qi
    dist = jnp.where(from_prev, qi + CHUNK - kc, qi - kc).astype(F32)

    def chunk_body(c, carry):
        r0 = pl.multiple_of(c * CHUNK, CHUNK)
        rows = pl.ds(r0, CHUNK)
        u = _gelu(z_s[rows, 0:A_WIDTH])
        va = _layernorm(_gelu(z_s[rows, A_WIDTH:2 * A_WIDTH]), lng_ref[...], lnb_ref[...])
        vab = va.astype(BF16)
        slabs = []
        for p in range(A_WIDTH // LANES):
            slab = vab[:, p * LANES:(p + 1) * LANES]
            m0 = jnp.dot(wsp_ref[2 * p], slab, preferred_element_type=F32)
            m1 = jnp.dot(wsp_ref[2 * p + 1], slab, preferred_element_type=F32)
            slabs.append(jnp.where(lo, m0, m1))
        ya = u * (jnp.concatenate(slabs, axis=-1) + bsp_ref[...])
        ya_n = _rms(ya, goa_ref[...])
        k2 = kv_s[pl.ds(r0, 2 * CHUNK), 0:KV_WIDTH]
        v2 = kv_s[pl.ds(r0, 2 * CHUNK), KV_WIDTH:2 * KV_WIDTH]
        k2r = pltpu.roll(k2, HEAD_DIM, 1)
        v2r = pltpu.roll(v2, HEAD_DIM, 1)
        kd = (jnp.where(lo2, k2, k2r).astype(BF16), jnp.where(lo2, k2r, k2).astype(BF16))
        vd = (jnp.where(lo2, v2, v2r).astype(BF16), jnp.where(lo2, v2r, v2).astype(BF16))
        prev_ok = (j > 0) | (c > 0)
        masked = from_prev & jnp.logical_not(prev_ok)
        yb_slabs = []
        for kv in range(2):
            q0 = z_s[rows, 2 * A_WIDTH + (2 * kv) * LANES:2 * A_WIDTH + (2 * kv + 1) * LANES]
            q1 = z_s[rows, 2 * A_WIDTH + (2 * kv + 1) * LANES:2 * A_WIDTH + (2 * kv + 2) * LANES]
            lhs = jnp.concatenate([jnp.where(lo, q0, 0.0), jnp.where(lo, 0.0, q0),
                                   jnp.where(lo, q1, 0.0), jnp.where(lo, 0.0, q1)], axis=0).astype(BF16)
            s_all = lax.dot_general(lhs, kd[kv], (((1,), (1,)), ((), ())), preferred_element_type=F32)
            probs = []
            for i in range(4):
                h = 4 * kv + i
                slope = 2.0 ** (-(h + 1))
                sink = sinks_ref[h]
                sh = s_all[i * CHUNK:(i + 1) * CHUNK]
                s = jnp.where(from_prev, sh[:, :CHUNK], sh[:, CHUNK:]) * (HEAD_DIM ** -0.5) - slope * dist
                s = jnp.where(masked, NEG, s)
                mx = jnp.maximum(jnp.max(s, axis=-1, keepdims=True), sink)
                e = jnp.exp(s - mx)
                den = jnp.sum(e, axis=-1, keepdims=True) + jnp.exp(sink - mx)
                p = e * (1.0 / den)
                probs.append(jnp.concatenate([jnp.where(from_prev, p, 0.0), jnp.where(from_prev, 0.0, p)], axis=-1))
            pm = jnp.concatenate(probs, axis=0).astype(BF16)
            o = jnp.dot(pm, vd[kv], preferred_element_type=F32)
            yb_slabs.append(jnp.where(lo, o[0:CHUNK], o[CHUNK:2 * CHUNK]))
            yb_slabs.append(jnp.where(lo, o[2 * CHUNK:3 * CHUNK], o[3 * CHUNK:4 * CHUNK]))
        yb_n = _rms(jnp.concatenate(yb_slabs, axis=-1), gob_ref[...])
        cat_s[rows, 0:A_WIDTH] = ya_n.astype(BF16)
        cat_s[rows, A_WIDTH:] = yb_n.astype(BF16)
        return carry

    lax.fori_loop(0, TM // CHUNK, chunk_body, 0)

    kv_s[0:CHUNK, :] = kv_s[TM:TM + CHUNK, :]
    k_ref[...] = kv_s[TM:TM + CHUNK, 0:KV_WIDTH]
    v_ref[...] = kv_s[TM:TM + CHUNK, KV_WIDTH:]

    h1 = x_ref[...] + jnp.dot(cat_s[...], wo_ref[...], preferred_element_type=F32)
    h1_ref[...] = h1
    xn2 = _rms(h1, gmoe_ref[...])
    xn_ref[...] = xn2.astype(BF16)
    gates, sel = _route(xn2, wr_ref, br_ref)
    gm_ref[...] = gates
    sm_ref[...] = sel


def _full(shape):
    n = len(shape)
    return pl.BlockSpec(shape, lambda *_: (0,) * n)


def _prompt_call(x, sinks, gmix, win, lng, lnb, wsp, bsp, goa, gob, wo, gmoe, wr, br):
    real = N_PROMPT // TM
    row = lambda g: (g, 0)
    seq = lambda g: (jnp.minimum(g, real - 1) // (SEQ // TM), 0, 0)
    return pl.pallas_call(
        _prompt_kernel,
        grid=(N_PAD // TM,),
        in_specs=[
            pl.BlockSpec(memory_space=pltpu.SMEM),
            pl.BlockSpec((TM, D_MODEL), lambda g: (jnp.minimum(g, real - 1), 0)),
            _full((1, D_MODEL)), _full((D_MODEL, IN_WIDTH)), _full((1, A_WIDTH)), _full((1, A_WIDTH)),
            _full((8, CHUNK, CHUNK)), _full((CHUNK, A_WIDTH)), _full((1, A_WIDTH)), _full((1, B_WIDTH)),
            _full((D_MODEL, D_MODEL)), _full((1, D_MODEL)), _full((D_MODEL, LANES)), _full((1, LANES)),
        ],
        out_specs=[
            pl.BlockSpec((TM, D_MODEL), row),
            pl.BlockSpec((TM, D_MODEL), row),
            pl.BlockSpec((TM, LANES), row),
            pl.BlockSpec((TM, LANES), row),
            pl.BlockSpec((None, CHUNK, KV_WIDTH), seq),
            pl.BlockSpec((None, CHUNK, KV_WIDTH), seq),
        ],
        out_shape=[
            jax.ShapeDtypeStruct((N_PAD, D_MODEL), F32),
            jax.ShapeDtypeStruct((N_PAD, D_MODEL), BF16),
            jax.ShapeDtypeStruct((N_PAD, LANES), F32),
            jax.ShapeDtypeStruct((N_PAD, LANES), F32),
            jax.ShapeDtypeStruct((BATCH, CHUNK, KV_WIDTH), F32),
            jax.ShapeDtypeStruct((BATCH, CHUNK, KV_WIDTH), F32),
        ],
        scratch_shapes=[
            pltpu.VMEM((TM, IN_WIDTH), F32),
            pltpu.VMEM((TM + CHUNK, 2 * KV_WIDTH), F32),
            pltpu.VMEM((TM, D_MODEL), BF16),
        ],
        compiler_params=pltpu.CompilerParams(
            dimension_semantics=("arbitrary",), vmem_limit_bytes=VMEM_LIMIT),
        name="prompt_premoe",
    )(sinks, x, gmix, win, lng, lnb, wsp, bsp, goa, gob, wo, gmoe, wr, br)


def _sample_kernel(sinks_ref, x_ref, ck_ref, cv_ref, gmix_ref, win_ref, lng_ref, lnb_ref, w00_ref, b0_ref,
                   goa_ref, gob_ref, wo_ref, gmoe_ref, wr_ref, br_ref,
                   h1_in, xn_in, gm_in, sm_in,
                   h1_ref, xn_ref, gm_ref, sm_ref, nk_ref, nv_ref, va_ref):
    del h1_in, xn_in, gm_in, sm_in
    t = SAMPLE_TILE
    nkeys = t * CHUNK

    if True:
        x = x_ref[...]
        xn = _rms(x, gmix_ref[...]).astype(BF16)
        z = jnp.dot(xn, win_ref[...], preferred_element_type=F32)
        u = _gelu(z[:, 0:A_WIDTH])
        va = _layernorm(_gelu(z[:, A_WIDTH:2 * A_WIDTH]), lng_ref[...], lnb_ref[...])
        va_ref[...] = va
        ya_n = _rms(u * (w00_ref[...] * va + b0_ref[...]), goa_ref[...])

        knew = z[:, 2 * A_WIDTH + B_WIDTH:2 * A_WIDTH + B_WIDTH + KV_WIDTH]
        vnew = z[:, 2 * A_WIDTH + B_WIDTH + KV_WIDTH:]
        lane = _lane_iota((t, LANES))
        lo = lane < HEAD_DIM
        stacked = []
        for h in range(B_HEADS):
            q = z[:, 2 * A_WIDTH + (h // 2) * LANES:2 * A_WIDTH + (h // 2 + 1) * LANES]
            qh = jnp.where(lo if h % 2 == 0 else jnp.logical_not(lo), q, 0.0)
            if h % 2 != h // 4:
                qh = pltpu.roll(qh, HEAD_DIM, 1)
            stacked.append(qh)
        qs = jnp.concatenate(stacked, axis=0)
        rows = B_HEADS * t
        ridx = lax.broadcasted_iota(jnp.int32, (rows, 1), 0)
        slope = jnp.zeros((rows, 1), F32)
        sink = jnp.zeros((rows, 1), F32)
        for h in range(B_HEADS):
            in_h = (ridx >= h * t) & (ridx < (h + 1) * t)
            slope = jnp.where(in_h, 2.0 ** (-(h + 1)), slope)
            sink = jnp.where(in_h, sinks_ref[h], sink)
        s_c = lax.dot_general(qs.astype(BF16), ck_ref[...].astype(BF16), (((1,), (1,)), ((), ())),
                              preferred_element_type=F32)
        rsamp = lax.broadcasted_iota(jnp.int32, (rows, nkeys), 0) % t
        col = lax.broadcasted_iota(jnp.int32, (rows, nkeys), 1)
        pos = col % CHUNK
        own = ((col // CHUNK) == rsamp) & (pos >= 1)
        s_c = s_c * (HEAD_DIM ** -0.5) - slope * (CHUNK - pos).astype(F32)
        s_c = jnp.where(own, s_c, NEG)
        kn8 = jnp.concatenate([knew] * B_HEADS, axis=0)
        vn8 = jnp.concatenate([vnew] * B_HEADS, axis=0)
        s_n = jnp.sum(qs * kn8, axis=-1, keepdims=True) * (HEAD_DIM ** -0.5)
        mx = jnp.maximum(jnp.maximum(jnp.max(s_c, axis=-1, keepdims=True), s_n), sink)
        e_c = jnp.exp(s_c - mx)
        e_n = jnp.exp(s_n - mx)
        inv = 1.0 / (jnp.sum(e_c, axis=-1, keepdims=True) + e_n + jnp.exp(sink - mx))
        o = jnp.dot((e_c * inv).astype(BF16), cv_ref[...].astype(BF16), preferred_element_type=F32)
        o = o + (e_n * inv) * vn8
        yb_slabs = []
        for p in range(B_WIDTH // LANES):
            outs = []
            for half in range(2):
                h = 2 * p + half
                oh = o[h * t:(h + 1) * t]
                oh = jnp.where(lo if h // 4 == 0 else jnp.logical_not(lo), oh, 0.0)
                if half != h // 4:
                    oh = pltpu.roll(oh, HEAD_DIM, 1)
                outs.append(oh)
            yb_slabs.append(outs[0] + outs[1])
        yb_n = _rms(jnp.concatenate(yb_slabs, axis=-1), gob_ref[...])

        cat = jnp.concatenate([ya_n, yb_n], axis=-1).astype(BF16)
        h1 = x + jnp.dot(cat, wo_ref[...], preferred_element_type=F32)
        xn2 = _rms(h1, gmoe_ref[...])
        gates, sel = _route(xn2, wr_ref, br_ref)
        h1_ref[...] = h1
        xn_ref[...] = xn2.astype(BF16)
        gm_ref[...] = gates
        sm_ref[...] = sel

        nk_ref[...] = pltpu.roll(ck_ref[...], nkeys - 1, 0)
        nv_ref[...] = pltpu.roll(cv_ref[...], nkeys - 1, 0)
        for b in range(t):
            nk_ref[b * CHUNK + CHUNK - 1:b * CHUNK + CHUNK, :] = knew[b:b + 1, :]
            nv_ref[b * CHUNK + CHUNK - 1:b * CHUNK + CHUNK, :] = vnew[b:b + 1, :]


def _sample_call(x, ck, cv, sinks, gmix, win, lng, lnb, w00, b0, goa, gob, wo, gmoe, wr, br, h1, xn, gm, sm):
    t = SAMPLE_TILE
    steps = DEC_BATCH // t
    base = N_PROMPT // t
    inrow = lambda i: (i, 0)
    outrow = lambda i: (base + i, 0)
    anyspec = pl.BlockSpec(memory_space=pl.ANY)
    return pl.pallas_call(
        _sample_kernel,
        grid=(steps,),
        in_specs=[
            pl.BlockSpec(memory_space=pltpu.SMEM),
            pl.BlockSpec((t, D_MODEL), inrow),
            pl.BlockSpec((t * CHUNK, KV_WIDTH), inrow),
            pl.BlockSpec((t * CHUNK, KV_WIDTH), inrow),
            _full((1, D_MODEL)), _full((D_MODEL, IN_WIDTH)), _full((1, A_WIDTH)), _full((1, A_WIDTH)),
            _full((1, A_WIDTH)), _full((1, A_WIDTH)), _full((1, A_WIDTH)), _full((1, B_WIDTH)),
            _full((D_MODEL, D_MODEL)), _full((1, D_MODEL)), _full((D_MODEL, LANES)), _full((1, LANES)),
            anyspec, anyspec, anyspec, anyspec,
        ],
        out_specs=[
            pl.BlockSpec((t, D_MODEL), outrow),
            pl.BlockSpec((t, D_MODEL), outrow),
            pl.BlockSpec((t, LANES), outrow),
            pl.BlockSpec((t, LANES), outrow),
            pl.BlockSpec((t * CHUNK, KV_WIDTH), inrow),
            pl.BlockSpec((t * CHUNK, KV_WIDTH), inrow),
            pl.BlockSpec((t, A_WIDTH), inrow),
        ],
        out_shape=[
            jax.ShapeDtypeStruct((N_PAD, D_MODEL), F32),
            jax.ShapeDtypeStruct((N_PAD, D_MODEL), BF16),
            jax.ShapeDtypeStruct((N_PAD, LANES), F32),
            jax.ShapeDtypeStruct((N_PAD, LANES), F32),
            jax.ShapeDtypeStruct((DEC_BATCH * CHUNK, KV_WIDTH), F32),
            jax.ShapeDtypeStruct((DEC_BATCH * CHUNK, KV_WIDTH), F32),
            jax.ShapeDtypeStruct((DEC_BATCH, A_WIDTH), F32),
        ],
        input_output_aliases={16: 0, 17: 1, 18: 2, 19: 3},
        compiler_params=pltpu.CompilerParams(
            dimension_semantics=("arbitrary",), vmem_limit_bytes=VMEM_LIMIT),
        name="sample_premoe",
    )(sinks, x, ck, cv, gmix, win, lng, lnb, w00, b0, goa, gob, wo, gmoe, wr, br, h1, xn, gm, sm)


def _decode_kernel(sinks_ref, x_ref, ck_ref, cv_ref, gmix_ref, win_ref, lng_ref, lnb_ref, w00_ref, b0_ref,
                   goa_ref, gob_ref, wo_ref, gmoe_ref, wr_ref, br_ref,
                   h1_in, xn_in, gm_in, sm_in,
                   h1_ref, xn_ref, gm_ref, sm_ref, nk_ref, nv_ref, va_ref,
                   q_s, kn_s, vn_s, ya_s, yb_s):
    del h1_in, xn_in, gm_in, sm_in
    i = pl.program_id(0)
    t = DEC_TILE
    per_seq = CHUNK * 2
    ncols = t * per_seq

    @pl.when(i == 0)
    def _():
        xn = _rms(x_ref[...], gmix_ref[...]).astype(BF16)
        z = jnp.dot(xn, win_ref[...], preferred_element_type=F32)
        u = _gelu(z[:, 0:A_WIDTH])
        va = _layernorm(_gelu(z[:, A_WIDTH:2 * A_WIDTH]), lng_ref[...], lnb_ref[...])
        va_ref[...] = va
        ya_s[...] = _rms(u * (w00_ref[...] * va + b0_ref[...]), goa_ref[...])
        q_s[...] = z[:, 2 * A_WIDTH:2 * A_WIDTH + B_WIDTH]
        kn_s[...] = z[:, 2 * A_WIDTH + B_WIDTH:2 * A_WIDTH + B_WIDTH + KV_WIDTH]
        vn_s[...] = z[:, 2 * A_WIDTH + B_WIDTH + KV_WIDTH:]

    rows = pl.ds(pl.multiple_of(i * t, t), t)
    q = q_s[rows, :]
    kn = kn_s[rows, :]
    vn = vn_s[rows, :]
    qs = jnp.concatenate([q[:, h * HEAD_DIM:(h + 1) * HEAD_DIM] for h in range(B_HEADS)], axis=0)
    kn8 = jnp.concatenate([kn[:, (h // 4) * HEAD_DIM:(h // 4 + 1) * HEAD_DIM] for h in range(B_HEADS)], axis=0)
    vn8 = jnp.concatenate([vn[:, (h // 4) * HEAD_DIM:(h // 4 + 1) * HEAD_DIM] for h in range(B_HEADS)], axis=0)
    nrows = B_HEADS * t
    ridx = lax.broadcasted_iota(jnp.int32, (nrows, 1), 0)
    slope = jnp.zeros((nrows, 1), F32)
    sink = jnp.zeros((nrows, 1), F32)
    for h in range(B_HEADS):
        in_h = (ridx >= h * t) & (ridx < (h + 1) * t)
        slope = jnp.where(in_h, 2.0 ** (-(h + 1)), slope)
        sink = jnp.where(in_h, sinks_ref[h], sink)
    s_c = lax.dot_general(qs.astype(BF16), ck_ref[...].astype(BF16), (((1,), (1,)), ((), ())),
                          preferred_element_type=F32)
    rr = lax.broadcasted_iota(jnp.int32, (nrows, ncols), 0)
    col = lax.broadcasted_iota(jnp.int32, (nrows, ncols), 1)
    pos = (col // 2) % CHUNK
    own = ((col // per_seq) == (rr % t)) & ((col % 2) == (rr // (4 * t))) & (pos >= 1)
    s_c = s_c * (HEAD_DIM ** -0.5) - slope * (CHUNK - pos).astype(F32)
    s_c = jnp.where(own, s_c, NEG)
    s_n = jnp.sum(qs * kn8, axis=-1, keepdims=True) * (HEAD_DIM ** -0.5)
    mx = jnp.maximum(jnp.maximum(jnp.max(s_c, axis=-1, keepdims=True), s_n), sink)
    e_c = jnp.exp(s_c - mx)
    e_n = jnp.exp(s_n - mx)
    inv = 1.0 / (jnp.sum(e_c, axis=-1, keepdims=True) + e_n + jnp.exp(sink - mx))
    o = jnp.dot((e_c * inv).astype(BF16), cv_ref[...].astype(BF16), preferred_element_type=F32)
    o = o + (e_n * inv) * vn8
    yb_s[rows, :] = jnp.concatenate([o[h * t:(h + 1) * t] for h in range(B_HEADS)], axis=-1)

    nk_ref[...] = pltpu.roll(ck_ref[...], ncols - 2, 0)
    nv_ref[...] = pltpu.roll(cv_ref[...], ncols - 2, 0)
    for b in range(t):
        for kv in range(2):
            r = b * per_seq + per_seq - 2 + kv
            nk_ref[r:r + 1, :] = kn[b:b + 1, kv * HEAD_DIM:(kv + 1) * HEAD_DIM]
            nv_ref[r:r + 1, :] = vn[b:b + 1, kv * HEAD_DIM:(kv + 1) * HEAD_DIM]

    @pl.when(i == pl.num_programs(0) - 1)
    def _():
        yb_n = _rms(yb_s[...], gob_ref[...])
        cat = jnp.concatenate([ya_s[...], yb_n], axis=-1).astype(BF16)
        h1 = x_ref[...] + jnp.dot(cat, wo_ref[...], preferred_element_type=F32)
        xn2 = _rms(h1, gmoe_ref[...])
        gates, sel = _route(xn2, wr_ref, br_ref)
        h1_ref[...] = h1
        xn_ref[...] = xn2.astype(BF16)
        gm_ref[...] = gates
        sm_ref[...] = sel


def _decode_call(x, ck, cv, sinks, gmix, win, lng, lnb, w00, b0, goa, gob, wo, gmoe, wr, br, h1, xn, gm, sm):
    t = DEC_TILE
    per_seq = CHUNK * 2
    cache = pl.BlockSpec((t * per_seq, HEAD_DIM), lambda i: (i, 0))
    tok = lambda width: pl.BlockSpec((DEC_BATCH, width), lambda i: (N_PROMPT // DEC_BATCH, 0))
    anyspec = pl.BlockSpec(memory_space=pl.ANY)
    return pl.pallas_call(
        _decode_kernel,
        grid=(DEC_BATCH // t,),
        in_specs=[
            pl.BlockSpec(memory_space=pltpu.SMEM),
            _full((DEC_BATCH, D_MODEL)), cache, cache,
            _full((1, D_MODEL)), _full((D_MODEL, IN_WIDTH)), _full((1, A_WIDTH)), _full((1, A_WIDTH)),
            _full((1, A_WIDTH)), _full((1, A_WIDTH)), _full((1, A_WIDTH)), _full((1, B_WIDTH)),
            _full((D_MODEL, D_MODEL)), _full((1, D_MODEL)), _full((D_MODEL, LANES)), _full((1, LANES)),
            anyspec, anyspec, anyspec, anyspec,
        ],
        out_specs=[tok(D_MODEL), tok(D_MODEL), tok(LANES), tok(LANES), cache, cache, _full((DEC_BATCH, A_WIDTH))],
        out_shape=[
            jax.ShapeDtypeStruct((N_PAD, D_MODEL), F32),
            jax.ShapeDtypeStruct((N_PAD, D_MODEL), BF16),
            jax.ShapeDtypeStruct((N_PAD, LANES), F32),
            jax.ShapeDtypeStruct((N_PAD, LANES), F32),
            jax.ShapeDtypeStruct((DEC_BATCH * per_seq, HEAD_DIM), F32),
            jax.ShapeDtypeStruct((DEC_BATCH * per_seq, HEAD_DIM), F32),
            jax.ShapeDtypeStruct((DEC_BATCH, A_WIDTH), F32),
        ],
        scratch_shapes=[pltpu.VMEM((DEC_BATCH, B_WIDTH), F32), pltpu.VMEM((DEC_BATCH, KV_WIDTH), F32),
                        pltpu.VMEM((DEC_BATCH, KV_WIDTH), F32), pltpu.VMEM((DEC_BATCH, A_WIDTH), F32),
                        pltpu.VMEM((DEC_BATCH, B_WIDTH), F32)],
        input_output_aliases={16: 0, 17: 1, 18: 2, 19: 3},
        compiler_params=pltpu.CompilerParams(
            dimension_semantics=("arbitrary",), vmem_limit_bytes=VMEM_LIMIT),
        name="sample_premoe",
    )(sinks, x, ck, cv, gmix, win, lng, lnb, w00, b0, goa, gob, wo, gmoe, wr, br, h1, xn, gm, sm)


def _plan_kernel(sm_ref,
                 destm_ref, destt_ref, xbe_ref, live_ref, nxblk_ref, stab_ref, ctab_ref, astart_ref, nwin_ref,
                 base_s, pstart_s):
    ph = pl.program_id(0)
    step = pl.program_id(1)
    lane = _lane_iota((1, LANES))

    @pl.when((ph == 0) & (step == 0))
    def _():
        base_s[...] = jnp.zeros_like(base_s)

    @pl.when(ph == 0)
    def _():
        base_s[...] += jnp.sum(sm_ref[...], axis=0, keepdims=True)

    @pl.when((ph == 1) & (step == 0))
    def _():
        counts = base_s[...]
        padded = jnp.floor((counts + (DISP_CHUNK + EXP_BLOCK - 1)) * (1.0 / EXP_BLOCK)) * EXP_BLOCK
        padded = jnp.where(counts > 0.0, padded, 0.0)
        pend = padded
        for s in (1, 2, 4, 8, 16):
            pend = pend + jnp.where(lane >= s, pltpu.roll(pend, s, 1), 0.0)
        spare = (N_ROWS + lane * DISP_CHUNK).astype(F32)
        pstart_s[...] = jnp.where(counts > 0.0, pend - padded, spare)
        base_s[...] = jnp.zeros_like(base_s)
        brow = lax.broadcasted_iota(jnp.int32, (XBE_ROWS, LANES), 0).astype(F32) * EXP_BLOCK
        done = jnp.where((lane < N_EXPERTS) & (pend <= brow), 1.0, 0.0)
        be = jnp.minimum(jnp.sum(done, axis=-1, keepdims=True), N_EXPERTS - 1.0)
        xbe_ref[...] = jnp.broadcast_to(be, (XBE_ROWS, LANES)).astype(jnp.int32)
        real = jnp.clip(counts - (brow - (pend - padded)), 0.0, float(EXP_BLOCK))
        real = jnp.sum(jnp.where(lane.astype(F32) == be, real, 0.0), axis=-1, keepdims=True)
        groups = jnp.floor((real + (ROW_GROUP - 1)) * (1.0 / ROW_GROUP))
        live_ref[...] = jnp.broadcast_to(groups, (XBE_ROWS, LANES)).astype(jnp.int32)
        total = jnp.sum(jnp.where(lane == N_EXPERTS - 1, pend, 0.0), axis=-1, keepdims=True)
        nxblk_ref[...] = jnp.broadcast_to(total * (1.0 / EXP_BLOCK), (8, LANES)).astype(jnp.int32)
        stab_ref[...] = jnp.zeros_like(stab_ref)
        ctab_ref[...] = jnp.zeros_like(ctab_ref)
        astart_ref[...] = jnp.zeros_like(astart_ref)
        nwin_ref[...] = jnp.zeros_like(nwin_ref)

    @pl.when(ph == 1)
    def _():
        r = lax.broadcasted_iota(jnp.int32, (TOK_TILE, TOK_TILE), 0)
        c = lax.broadcasted_iota(jnp.int32, (TOK_TILE, TOK_TILE), 1)
        lower = jnp.where(c < r, 1.0, 0.0).astype(BF16)
        for q in range(TM // TOK_TILE):
            i = step * (TM // TOK_TILE) + q
            sel = sm_ref[q * TOK_TILE:(q + 1) * TOK_TILE, :]
            cnt = jnp.sum(sel, axis=0, keepdims=True)
            prefix = jnp.dot(lower, sel.astype(BF16), preferred_element_type=F32)
            start = pstart_s[...] + base_s[...]
            dest = jnp.where(sel > 0.0, prefix + start, -1.0)
            destm_ref[q * TOK_TILE:(q + 1) * TOK_TILE, :] = dest
            destt_ref[:, q * TOK_TILE:(q + 1) * TOK_TILE] = dest.T
            has = (cnt > 0.0) & (lane < N_EXPERTS)
            stab_ref[pl.ds(i, 1), :] = start.astype(jnp.int32)
            ctab_ref[pl.ds(i, 1), :] = jnp.where(has, cnt, 0.0).astype(jnp.int32)
            a = jnp.minimum(jnp.floor(start * (1.0 / WIN_ALIGN)) * WIN_ALIGN, float(N_ROWS - WIN))
            nw = jnp.where(has, jnp.floor((start + cnt - a + (WIN - 1)) * (1.0 / WIN)), 0.0)
            astart_ref[pl.ds(i, 1), :] = a.astype(jnp.int32)
            nwin_ref[pl.ds(i, 1), :] = nw.astype(jnp.int32)
            base_s[...] += cnt


def _plan_call(sm):
    tile = lambda ph, i: (i * ph, 0)
    tile_t = lambda ph, i: (0, i * ph)
    tab = jax.ShapeDtypeStruct((TAB_ROWS, LANES), jnp.int32)
    return pl.pallas_call(
        _plan_kernel,
        grid=(2, N_PAD // TM),
        in_specs=[pl.BlockSpec((TM, LANES), lambda ph, i: (i, 0))],
        out_specs=[
            pl.BlockSpec((TM, LANES), tile),
            pl.BlockSpec((LANES, TM), tile_t),
            _full((XBE_ROWS, LANES)), _full((XBE_ROWS, LANES)), _full((8, LANES)),
            _full((TAB_ROWS, LANES)), _full((TAB_ROWS, LANES)), _full((TAB_ROWS, LANES)), _full((TAB_ROWS, LANES)),
        ],
        out_shape=[
            jax.ShapeDtypeStruct((N_PAD, LANES), F32),
            jax.ShapeDtypeStruct((LANES, N_PAD), F32),
            jax.ShapeDtypeStruct((XBE_ROWS, LANES), jnp.int32),
            jax.ShapeDtypeStruct((XBE_ROWS, LANES), jnp.int32),
            jax.ShapeDtypeStruct((8, LANES), jnp.int32),
            tab, tab, tab, tab,
        ],
        scratch_shapes=[pltpu.VMEM((1, LANES), F32), pltpu.VMEM((1, LANES), F32)],
        compiler_params=pltpu.CompilerParams(
            dimension_semantics=("arbitrary", "arbitrary"), vmem_limit_bytes=VMEM_LIMIT),
        name="moe_plan",
    )(sm)


def _pack_rows(z):
    half = D_MODEL // 2
    lo = lax.bitcast_convert_type(z[:, :half], jnp.uint32) >> 16
    hi = lax.bitcast_convert_type(z[:, half:], jnp.uint32) & jnp.uint32(0xFFFF0000)
    return lax.bitcast_convert_type(hi | lo, jnp.int32)


def _unpack_rows(ref, rows=None):
    rows = ref.shape[0] if rows is None else rows
    flat = ref.reshape(ref.shape[0] * PACK, LANES)
    lo, hi = [], []
    for s in range(PACK):
        w = lax.bitcast_convert_type(flat[pl.ds(s, rows, stride=PACK), :], jnp.uint32)
        lo.append(lax.bitcast_convert_type(w << 16, F32))
        hi.append(lax.bitcast_convert_type(w & jnp.uint32(0xFFFF0000), F32))
    return jnp.concatenate(lo + hi, axis=-1).astype(BF16)


def _dispatch_kernel(stab_ref, ctab_ref, cmax_ref, xn_ref, destt_ref, xs_in, xs_ref,
                     stage0, stage1, stage2, sems, sem2):
    del xs_in
    i = pl.program_id(0)
    last = pl.num_programs(0) - 1
    x = xn_ref[...]
    dt = destt_ref[...]
    rio = lax.broadcasted_iota(jnp.int32, (DISP_CHUNK, 1), 0).astype(F32)

    def chunk_rows(j, stage):
        parts = []
        for e in range(N_EXPERTS):
            first = (stab_ref[i * N_EXPERTS + e] + j * DISP_CHUNK).astype(F32)
            parts.append(jnp.where(dt[e:e + 1, :] == first + rio, 1.0, 0.0).astype(BF16))
        onehot = jnp.concatenate(parts, axis=0)
        words = _pack_rows(jnp.dot(onehot, x, preferred_element_type=F32))
        for s in range(PACK):
            stage[pl.ds(s, N_EXPERTS * DISP_CHUNK, stride=PACK), :] = words[:, s * LANES:(s + 1) * LANES]

    def copy(stage, step, e, j, sem):
        first = stab_ref[step * N_EXPERTS + e] + j * DISP_CHUNK
        rows = stage.reshape(N_EXPERTS * DISP_CHUNK, PACK, LANES)
        return pltpu.make_async_copy(rows.at[pl.ds(e * DISP_CHUNK, DISP_CHUNK)],
                                     xs_ref.at[pl.ds(first, DISP_CHUNK)], sem)

    def step_body(stage, prev_stage, par):
        chunk_rows(0, stage)

        @pl.when(i > 0)
        def _():
            for e in range(N_EXPERTS):
                copy(prev_stage, i - 1, e, 0, sems.at[1 - par, 0]).wait()

        for e in range(N_EXPERTS):
            copy(stage, i, e, 0, sems.at[par, 0]).start()

        @pl.when(i == last)
        def _():
            for e in range(N_EXPERTS):
                copy(stage, i, e, 0, sems.at[par, 0]).wait()

    @pl.when(i % 2 == 0)
    def _():
        step_body(stage0, stage1, 0)

    @pl.when(i % 2 == 1)
    def _():
        step_body(stage1, stage0, 1)

    for j in range(1, TOK_TILE // DISP_CHUNK):

        @pl.when(cmax_ref[i] > j * DISP_CHUNK)
        def _(j=j):
            chunk_rows(j, stage2)
            for e in range(N_EXPERTS):

                @pl.when(ctab_ref[i * N_EXPERTS + e] > j * DISP_CHUNK)
                def _(e=e):
                    cp = copy(stage2, i, e, j, sem2)
                    cp.start()
                    cp.wait()


def _dispatch_call(stab, ctab, cmax, xn, destt, xs_zero):
    stage = pltpu.VMEM((N_EXPERTS * DISP_CHUNK * PACK, LANES), jnp.int32)
    grid_spec = pltpu.PrefetchScalarGridSpec(
        num_scalar_prefetch=3,
        grid=(N_PAD_TILES,),
        in_specs=[
            pl.BlockSpec((TOK_TILE, D_MODEL), lambda i, *_: (i, 0)),
            pl.BlockSpec((N_EXPERTS, TOK_TILE), lambda i, *_: (0, i)),
            pl.BlockSpec(memory_space=pl.ANY),
        ],
        out_specs=pl.BlockSpec(memory_space=pl.ANY),
        scratch_shapes=[stage, stage, stage, pltpu.SemaphoreType.DMA((2, N_EXPERTS)),
                        pltpu.SemaphoreType.DMA],
    )
    return pl.pallas_call(
        _dispatch_kernel,
        grid_spec=grid_spec,
        out_shape=jax.ShapeDtypeStruct((XS_ROWS, PACK, LANES), jnp.int32),
        input_output_aliases={5: 0},
        compiler_params=pltpu.CompilerParams(
            dimension_semantics=("arbitrary",), vmem_limit_bytes=VMEM_LIMIT),
        name="moe_dispatch",
    )(stab, ctab, cmax, xn, destt, xs_zero)


def _expert_kernel(blke_ref, nblk_ref, live_ref, xs_ref, wgu_hbm, bgu_ref, wdn_hbm, bdn_ref,
                   ys_ref, wgu_f, wdn_f, wgu_s, wdn_s, sems):
    b = pl.program_id(0)
    used = b < nblk_ref[0]
    prev = blke_ref[jnp.maximum(b - 1, 0)]
    fresh = used & ((b == 0) | (blke_ref[b] != prev))

    def fetch(e):
        return (pltpu.make_async_copy(wgu_hbm.at[e], wgu_f, sems.at[0]),
                pltpu.make_async_copy(wdn_hbm.at[e], wdn_f, sems.at[1]))

    @pl.when(b == 0)
    def _():
        for cp in fetch(blke_ref[0]):
            cp.start()

    @pl.when(fresh)
    def _():
        for cp in fetch(blke_ref[b]):
            cp.wait()
        wgu_s[...] = wgu_f[...].astype(BF16)
        wdn_s[...] = wdn_f[...].astype(BF16)

        nxt = lax.while_loop(lambda p: (p < nblk_ref[0]) & (blke_ref[jnp.minimum(p, N_XBLOCKS - 1)] == blke_ref[b]),
                             lambda p: p + 1, b + 1)

        @pl.when(nxt < nblk_ref[0])
        def _():
            for cp in fetch(blke_ref[jnp.minimum(nxt, N_XBLOCKS - 1)]):
                cp.start()

    for groups in range(1, EXP_BLOCK // ROW_GROUP + 1):
        rows = groups * ROW_GROUP

        @pl.when(used & (live_ref[b] == groups))
        def _(rows=rows):
            hid = jnp.dot(_unpack_rows(xs_ref, rows), wgu_s[...], preferred_element_type=F32) + bgu_ref[...]
            gate = jnp.minimum(hid[:, :D_FF], SWIGLU_LIMIT)
            up = jnp.clip(hid[:, D_FF:], -SWIGLU_LIMIT, SWIGLU_LIMIT)
            act = (up + 1.0) * gate * jax.nn.sigmoid(SWIGLU_ALPHA * gate)
            y = jnp.dot(act.astype(BF16), wdn_s[...], preferred_element_type=F32) + bdn_ref[...]
            ys_ref[0:rows, :] = y.astype(BF16)
            if rows < EXP_BLOCK:
                ys_ref[rows:, :] = jnp.zeros((EXP_BLOCK - rows, D_MODEL), BF16)

    @pl.when(jnp.logical_not(used) | (live_ref[b] == 0))
    def _():
        ys_ref[...] = jnp.zeros_like(ys_ref)


def _expert_call(blke, nblk, live, xs, wgu, bgu, wdn, bdn):
    grid_spec = pltpu.PrefetchScalarGridSpec(
        num_scalar_prefetch=3,
        grid=(N_XBLOCKS,),
        in_specs=[
            pl.BlockSpec((EXP_BLOCK, PACK, LANES), lambda b, be, *_: (b, 0, 0)),
            pl.BlockSpec(memory_space=pl.ANY),
            pl.BlockSpec((None, 1, 2 * D_FF), lambda b, be, *_: (be[b], 0, 0)),
            pl.BlockSpec(memory_space=pl.ANY),
            pl.BlockSpec((None, 1, D_MODEL), lambda b, be, *_: (be[b], 0, 0)),
        ],
        out_specs=pl.BlockSpec((EXP_BLOCK, D_MODEL), lambda b, be, *_: (b, 0)),
        scratch_shapes=[pltpu.VMEM((D_MODEL, 2 * D_FF), F32), pltpu.VMEM((D_FF, D_MODEL), F32),
                        pltpu.VMEM((D_MODEL, 2 * D_FF), BF16), pltpu.VMEM((D_FF, D_MODEL), BF16),
                        pltpu.SemaphoreType.DMA((2,))],
    )
    return pl.pallas_call(
        _expert_kernel,
        grid_spec=grid_spec,
        out_shape=jax.ShapeDtypeStruct((N_ROWS, D_MODEL), BF16),
        compiler_params=pltpu.CompilerParams(
            dimension_semantics=("arbitrary",), vmem_limit_bytes=VMEM_LIMIT),
        name="moe_experts",
    )(blke, nblk, live, xs, wgu, bgu, wdn, bdn)


def _combine_kernel(*refs):
    astart_ref, nwin_ref, over_ref = refs[0:3]
    win_refs = refs[3:3 + N_EXPERTS]
    (destm_ref, gm_ref, h1_ref, plep_ref, ples_ref, gple_ref, wpg_ref, wpp_ref, gfin_ref, ys_any,
     yp_ref, ysm_ref, moe_s, tmp_s, sem) = refs[3 + N_EXPERTS:]
    i = pl.program_id(0)
    dest = destm_ref[...]
    gates = gm_ref[...]
    lane = _lane_iota((COMB_TILE, LANES))
    lane_f = lane.astype(F32)
    lo = lane < WIN
    moe = jnp.zeros((COMB_TILE, D_MODEL), F32)
    group = 4
    for g0 in range(0, N_EXPERTS, group):
        gsel = []
        for p in range(group // 2):
            e0 = g0 + 2 * p
            a0 = astart_ref[i * N_EXPERTS + e0].astype(F32)
            a1 = astart_ref[i * N_EXPERTS + e0 + 1].astype(F32)
            rowid = jnp.where(lo, a0 + lane_f, a1 + lane_f - WIN)
            dcol = jnp.where(lo, dest[:, e0:e0 + 1], dest[:, e0 + 1:e0 + 2])
            gcol = jnp.where(lo, gates[:, e0:e0 + 1], gates[:, e0 + 1:e0 + 2])
            gsel.append(jnp.where(dcol == rowid, gcol, 0.0).astype(BF16))
        ywin = jnp.concatenate([win_refs[g0 + q][...] for q in range(group)], axis=0)
        moe = moe + jnp.dot(jnp.concatenate(gsel, axis=-1), ywin, preferred_element_type=F32)
    mrows = pl.ds(pl.multiple_of((i % TAIL_TILES) * COMB_TILE, COMB_TILE), COMB_TILE)
    moe_s[mrows, :] = moe

    @pl.when(over_ref[i] > 0)
    def _():
        tmp_s[...] = jnp.zeros_like(tmp_s)

        def per_expert(e, carry):
            a = astart_ref[i * N_EXPERTS + e]
            dcol = jnp.sum(jnp.where(lane == e, dest, 0.0), axis=-1, keepdims=True)
            gcol = jnp.sum(jnp.where(lane == e, gates, 0.0), axis=-1, keepdims=True)

            def per_window(w, carry2):
                first = a + w * WIN
                start = pl.multiple_of(jnp.minimum(first, N_ROWS - WIN), WIN_ALIGN)
                cp = pltpu.make_async_copy(ys_any.at[pl.ds(start, WIN)], tmp_s.at[pl.ds(0, WIN)], sem)
                cp.start()
                cp.wait()
                hit = lo & (dcol == start.astype(F32) + lane_f) & (dcol >= first.astype(F32))
                gsel = jnp.where(hit, gcol, 0.0).astype(BF16)
                moe_s[mrows, :] += jnp.dot(gsel, tmp_s[...], preferred_element_type=F32)
                return carry2

            return lax.fori_loop(1, nwin_ref[i * N_EXPERTS + e], per_window, carry)

        lax.fori_loop(0, N_EXPERTS, per_expert, 0)

    def tail(h1, moe_rows, ple):
        h2 = h1 + moe_rows
        hn = _rms(h2, gple_ref[...]).astype(BF16)
        gate = jax.nn.sigmoid(jnp.dot(hn, wpg_ref[...], preferred_element_type=F32))
        proj = jnp.dot(ple.astype(BF16), wpp_ref[...], preferred_element_type=F32)
        return _rms(h2 + gate * proj, gfin_ref[...])

    is_sample = i == N_CTILES - 1

    @pl.when((i % TAIL_TILES == TAIL_TILES - 1) & jnp.logical_not(is_sample))
    def _():
        yp_ref[...] = tail(h1_ref[...], moe_s[...], plep_ref[...])

    @pl.when(is_sample)
    def _():
        ysm_ref[...] = tail(h1_ref[0:DEC_BATCH, :], moe_s[0:DEC_BATCH, :], ples_ref[...])


def _combine_call(astart, nwin, over, ys, destm, gm, h1, plep, ples, gple, wpg, wpp, gfin):
    rows = TAIL_TILES * COMB_TILE
    last = N_PROMPT // rows - 1

    def win_spec(e):
        return pl.BlockSpec((pl.Element(WIN), pl.Element(D_MODEL)),
                            lambda i, a, nw, ov, e=e: (pl.multiple_of(a[i * N_EXPERTS + e], WIN_ALIGN), 0))

    grid_spec = pltpu.PrefetchScalarGridSpec(
        num_scalar_prefetch=3,
        grid=(N_CTILES,),
        in_specs=[win_spec(e) for e in range(N_EXPERTS)] + [
            pl.BlockSpec((COMB_TILE, LANES), lambda i, *_: (i, 0)),
            pl.BlockSpec((COMB_TILE, LANES), lambda i, *_: (i, 0)),
            pl.BlockSpec((rows, D_MODEL), lambda i, *_: (i // TAIL_TILES, 0)),
            pl.BlockSpec((rows, PLE_DIM), lambda i, *_: (jnp.minimum(i // TAIL_TILES, last), 0)),
            pl.BlockSpec((DEC_BATCH, PLE_DIM), lambda i, *_: (0, 0)),
            pl.BlockSpec((1, D_MODEL), lambda i, *_: (0, 0)),
            pl.BlockSpec((D_MODEL, D_MODEL), lambda i, *_: (0, 0)),
            pl.BlockSpec((PLE_DIM, D_MODEL), lambda i, *_: (0, 0)),
            pl.BlockSpec((1, D_MODEL), lambda i, *_: (0, 0)),
            pl.BlockSpec(memory_space=pl.ANY),
        ],
        out_specs=[
            pl.BlockSpec((rows, D_MODEL), lambda i, *_: (jnp.minimum(i // TAIL_TILES, last), 0)),
            pl.BlockSpec((DEC_BATCH, D_MODEL), lambda i, *_: (0, 0)),
        ],
        scratch_shapes=[pltpu.VMEM((rows, D_MODEL), F32), pltpu.VMEM((2 * WIN, D_MODEL), BF16),
                        pltpu.SemaphoreType.DMA],
    )
    return pl.pallas_call(
        _combine_kernel,
        grid_spec=grid_spec,
        out_shape=[jax.ShapeDtypeStruct((N_PROMPT, D_MODEL), F32),
                   jax.ShapeDtypeStruct((DEC_BATCH, D_MODEL), F32)],
        compiler_params=pltpu.CompilerParams(
            dimension_semantics=("arbitrary",), vmem_limit_bytes=VMEM_LIMIT),
        name="moe_combine_tail",
    )(astart, nwin, over, *([ys] * N_EXPERTS), destm, gm, h1, plep, ples, gple, wpg, wpp, gfin, ys)


def _gather_kernel(*refs):
    astart_ref, nwin_ref, over_ref = refs[0:3]
    win_refs = refs[3:3 + N_EXPERTS]
    (destm_ref, gm_ref, h1_ref, plep_ref, ples_ref, gple_ref, wpg_ref, wpp_ref, gfin_ref, ys_any,
     yp_ref, ysm_ref, moe_s, tmp_s, sem) = refs[3 + N_EXPERTS:]
    i = pl.program_id(0)
    slot = i % TAIL_TILES
    dest = destm_ref[...]
    gates = gm_ref[...]
    lane = _lane_iota((TOK_TILE, LANES))
    lane_f = lane.astype(F32)
    per_slab = LANES // CWIN
    within = (lane % CWIN).astype(F32)
    group = 2 * per_slab

    def split(gsel):
        hi = gsel.astype(BF16)
        return hi, (gsel - hi.astype(F32)).astype(BF16)

    moe = jnp.zeros((TOK_TILE, D_MODEL), F32)
    for g0 in range(0, N_EXPERTS, group):
        his, los = [], []
        for sl in range(2):
            rowid = jnp.zeros((TOK_TILE, LANES), F32)
            dcol = jnp.zeros((TOK_TILE, LANES), F32)
            gcol = jnp.zeros((TOK_TILE, LANES), F32)
            for q in range(per_slab):
                e = g0 + sl * per_slab + q
                mine = (lane >= q * CWIN) & (lane < (q + 1) * CWIN)
                rowid = jnp.where(mine, astart_ref[i * N_EXPERTS + e].astype(F32) + within, rowid)
                dcol = jnp.where(mine, dest[:, e:e + 1], dcol)
                gcol = jnp.where(mine, gates[:, e:e + 1], gcol)
            hi, lo = split(jnp.where(dcol == rowid, gcol, 0.0))
            his.append(hi)
            los.append(lo)
        ywin = jnp.concatenate([_unpack_rows(win_refs[g0 + q]) for q in range(group)], axis=0)
        both = jnp.concatenate([jnp.concatenate(his, axis=-1), jnp.concatenate(los, axis=-1)], axis=0)
        r = jnp.dot(both, ywin, preferred_element_type=F32)
        moe = moe + r[:TOK_TILE] + r[TOK_TILE:]
    mrows = pl.ds(pl.multiple_of(slot * TOK_TILE, TOK_TILE), TOK_TILE)
    moe_s[mrows, :] = moe

    @pl.when(over_ref[i] > 0)
    def _():
        def per_expert(e, carry):
            a = astart_ref[i * N_EXPERTS + e]
            dcol = jnp.sum(jnp.where(lane == e, dest, 0.0), axis=-1, keepdims=True)
            gcol = jnp.sum(jnp.where(lane == e, gates, 0.0), axis=-1, keepdims=True)

            def per_window(w, carry2):
                first = a + w * CWIN
                start = jnp.minimum(first, N_ROWS - CWIN)
                cp = pltpu.make_async_copy(ys_any.at[pl.ds(start, CWIN)], tmp_s, sem)
                cp.start()
                cp.wait()
                hit = (lane < CWIN) & (dcol == start.astype(F32) + lane_f) & (dcol >= first.astype(F32))
                hi, lo = split(jnp.where(hit, gcol, 0.0))
                rows = jnp.concatenate([_unpack_rows(tmp_s), jnp.zeros((LANES - CWIN, D_MODEL), BF16)], axis=0)
                moe_s[mrows, :] += (jnp.dot(hi, rows, preferred_element_type=F32)
                                    + jnp.dot(lo, rows, preferred_element_type=F32))
                return carry2

            return lax.fori_loop(1, nwin_ref[i * N_EXPERTS + e], per_window, carry)

        lax.fori_loop(0, N_EXPERTS, per_expert, 0)

    def tail(h1, moe_rows, ple):
        h2 = h1 + moe_rows
        hn = _rms(h2, gple_ref[...]).astype(BF16)
        gate = jax.nn.sigmoid(jnp.dot(hn, wpg_ref[...], preferred_element_type=F32))
        proj = jnp.dot(ple.astype(BF16), wpp_ref[...], preferred_element_type=F32)
        return _rms(h2 + gate * proj, gfin_ref[...])

    is_sample = i == N_TILES - 1

    @pl.when((slot == TAIL_TILES - 1) & jnp.logical_not(is_sample))
    def _():
        yp_ref[...] = tail(h1_ref[...], moe_s[...], plep_ref[...])

    @pl.when(is_sample)
    def _():
        ysm_ref[...] = tail(h1_ref[0:TOK_TILE, :], moe_s[0:TOK_TILE, :], ples_ref[...])


def _gather_call(astart, nwin, over, ys, destm, gm, h1, plep, ples, gple, wpg, wpp, gfin):
    rows = TAIL_TILES * TOK_TILE
    last = N_PROMPT // rows - 1

    def win_spec(e):
        return pl.BlockSpec((pl.Element(CWIN), pl.Element(PACK), pl.Element(LANES)),
                            lambda i, a, nw, ov, e=e: (a[i * N_EXPERTS + e], 0, 0))

    grid_spec = pltpu.PrefetchScalarGridSpec(
        num_scalar_prefetch=3,
        grid=(N_CTILES,),
        in_specs=[win_spec(e) for e in range(N_EXPERTS)] + [
            pl.BlockSpec((COMB_TILE, LANES), lambda i, *_: (i, 0)),
            pl.BlockSpec((COMB_TILE, LANES), lambda i, *_: (i, 0)),
            pl.BlockSpec((rows, D_MODEL), lambda i, *_: (i // TAIL_TILES, 0)),
            pl.BlockSpec((rows, PLE_DIM), lambda i, *_: (jnp.minimum(i // TAIL_TILES, last), 0)),
            pl.BlockSpec((DEC_BATCH, PLE_DIM), lambda i, *_: (0, 0)),
            pl.BlockSpec((1, D_MODEL), lambda i, *_: (0, 0)),
            pl.BlockSpec((D_MODEL, D_MODEL), lambda i, *_: (0, 0)),
            pl.BlockSpec((PLE_DIM, D_MODEL), lambda i, *_: (0, 0)),
            pl.BlockSpec((1, D_MODEL), lambda i, *_: (0, 0)),
            pl.BlockSpec(memory_space=pl.ANY),
        ],
        out_specs=[
            pl.BlockSpec((rows, D_MODEL), lambda i, *_: (jnp.minimum(i // TAIL_TILES, last), 0)),
            pl.BlockSpec((TOK_TILE, D_MODEL), lambda i, *_: (0, 0)),
        ],
        scratch_shapes=[pltpu.VMEM((rows, D_MODEL), F32), pltpu.VMEM((CWIN, PACK, LANES), jnp.int32),
                        pltpu.SemaphoreType.DMA],
    )
    return pl.pallas_call(
        _gather_kernel,
        grid_spec=grid_spec,
        out_shape=[jax.ShapeDtypeStruct((N_PROMPT, D_MODEL), F32),
                   jax.ShapeDtypeStruct((DEC_BATCH, D_MODEL), F32)],
        compiler_params=pltpu.CompilerParams(
            dimension_semantics=("arbitrary",), vmem_limit_bytes=VMEM_LIMIT),
        name="moe_combine_tail",
    )(astart, nwin, over, *([ys] * N_EXPERTS), destm, gm, h1, plep, ples, gple, wpg, wpp, gfin, ys)


def kernel(x_prompt, x_sample, cache_swa_k, cache_swa_v, p_prompt, p_sample, g_mix, w_in, ln_v_g, ln_v_b,
           w_sp, b_sp, sinks, g_out_a, g_out_b, w_o, g_moe, w_router, b_router, w_gu, b_gu, w_dn, b_dn,
           g_ple, w_ple_gate, w_ple_proj, g_final):
    l = 0
    row = lambda v: v.reshape(1, -1)
    win = w_in[l].astype(BF16)
    wo = w_o[l].astype(BF16)
    tril = jnp.tril(jnp.ones((CHUNK, CHUNK), dtype=bool))
    wsp = jnp.where(tril, w_sp[l], 0.0).astype(BF16)
    bsp = jnp.repeat(b_sp[l].T, HEAD_DIM, axis=1)
    w00 = row(jnp.repeat(w_sp[l][:, 0, 0], HEAD_DIM))
    b0 = row(jnp.repeat(b_sp[l][:, 0], HEAD_DIM))
    wr_hi = w_router[l].astype(BF16)
    wr_lo = (w_router[l] - wr_hi.astype(F32)).astype(BF16)
    wr = jnp.concatenate([wr_hi, wr_lo, jnp.zeros((D_MODEL, LANES - 2 * N_EXPERTS), BF16)], axis=1)
    br = row(jnp.concatenate([b_router[l], jnp.zeros((LANES - N_EXPERTS,), F32)]))
    common = (row(g_mix[l]), win, row(ln_v_g[l]), row(ln_v_b[l]))
    tail = (row(g_out_a[l]), row(g_out_b[l]), wo, row(g_moe[l]), wr, br)

    h1, xn, gm, sm, k_p, v_p = _prompt_call(
        x_prompt.reshape(N_PROMPT, D_MODEL), sinks[l], *common, wsp, bsp, *tail)
    ck = cache_swa_k[l].reshape(DEC_BATCH * CHUNK * 2, HEAD_DIM)
    cv = cache_swa_v[l].reshape(DEC_BATCH * CHUNK * 2, HEAD_DIM)
    h1, xn, gm, sm, k_s, v_s, va_s = _decode_call(
        x_sample.reshape(DEC_BATCH, D_MODEL), ck, cv, sinks[l], *common, w00, b0, *tail, h1, xn, gm, sm)

    destm, destt, xbe, live, nxblk, stab, ctab, astart, nwin = _plan_call(sm)
    flat = lambda tab, n: tab[:n, :N_EXPERTS].reshape(-1)
    del astart, nwin
    per = COMB_TILE // TOK_TILE
    first = stab[0:per * N_CTILES:per, :N_EXPERTS]
    count = sum(ctab[q:per * N_CTILES:per, :N_EXPERTS] for q in range(per))
    astart2 = jnp.minimum(first // WIN_ALIGN * WIN_ALIGN, N_ROWS - WIN)
    nwin2 = jnp.where(count > 0, (first + count - astart2 + WIN - 1) // WIN, 0)
    astart1 = astart2.reshape(-1)
    over1 = (jnp.max(nwin2, axis=1) > 1).astype(jnp.int32)
    nwin1 = nwin2.reshape(-1)
    cmax1 = jnp.max(ctab[:N_PAD_TILES, :N_EXPERTS], axis=1)

    xs = _dispatch_call(flat(stab, N_PAD_TILES), flat(ctab, N_PAD_TILES), cmax1, xn, destt[:N_EXPERTS],
                        jnp.zeros((XS_ROWS, PACK, LANES), jnp.int32))
    ys = _expert_call(xbe[:N_XBLOCKS, 0], nxblk[0, :1], live[:N_XBLOCKS, 0], xs, w_gu[l], b_gu[l].reshape(N_EXPERTS, 1, 2 * D_FF),
                      w_dn[l], b_dn[l].reshape(N_EXPERTS, 1, D_MODEL))
    y_p, y_s = _combine_call(
        astart1, nwin1, over1, ys, destm, gm, h1,
        p_prompt[l].reshape(N_PROMPT, PLE_DIM), p_sample[l].reshape(DEC_BATCH, PLE_DIM),
        row(g_ple[l]), w_ple_gate[l].astype(BF16), w_ple_proj[l].astype(BF16), row(g_final))

    kv5 = lambda a, n: a.reshape(1, n, CHUNK, 2, HEAD_DIM)
    return (y_p.reshape(BATCH, SEQ, D_MODEL), y_s.reshape(DEC_BATCH, 1, D_MODEL),
            kv5(k_p, BATCH), kv5(v_p, BATCH), kv5(k_s, DEC_BATCH), kv5(v_s, DEC_BATCH),
            va_s.reshape(1, DEC_BATCH, 1, A_WIDTH))
```

```python
import math

import jax
import jax.numpy as jnp
from jax import lax
from jax.experimental import pallas as pl
from jax.experimental.pallas import tpu as pltpu

F32 = jnp.float32
BF16 = jnp.bfloat16

D_MODEL = 1024
BATCH = 4
SEQ = 4096
DEC_BATCH = 128
HEAD_DIM = 64
A_WIDTH = 512
B_WIDTH = 512
B_HEADS = 8
KV_WIDTH = 128
IN_WIDTH = 2 * A_WIDTH + B_WIDTH + 2 * KV_WIDTH
CHUNK = 128
N_EXPERTS = 32
TOP_K = 4
D_FF = 1024
SWIGLU_ALPHA = 1.702
SWIGLU_LIMIT = 7.0
PLE_DIM = 256
EPS = 1e-5

LANES = 128
ROW_GROUP = 128
EXP_BLOCK = 512
N_PROMPT = BATCH * SEQ
N_TOK = N_PROMPT + DEC_BATCH
TOK_TILE = 128
N_TILES = N_TOK // TOK_TILE
DISP_TILE = 256
TM = 512
N_PAD = ((N_TOK + TM - 1) // TM) * TM
N_PAD_TILES = N_PAD // TOK_TILE
DISP_CHUNK = 32
N_XBLOCKS = (N_TOK * TOP_K + N_EXPERTS * (DISP_CHUNK + EXP_BLOCK - 1) + EXP_BLOCK - 1) // EXP_BLOCK
N_ROWS = N_XBLOCKS * EXP_BLOCK
XS_ROWS = N_ROWS + N_EXPERTS * DISP_CHUNK
XBE_ROWS = ((N_XBLOCKS + 7) // 8) * 8
TAB_ROWS = ((N_PAD_TILES + 7) // 8) * 8
PACK = D_MODEL // 2 // LANES
WIN = 64
WIN_ALIGN = 16
CWIN = 32
COMB_TILE = 128
N_CTILES = N_PROMPT // COMB_TILE + 1
TAIL_TILES = 4
SAMPLE_TILE = 32
DEC_TILE = 16
NEG = -1e30
VMEM_LIMIT = 56 * 1024 * 1024


def _rms(x, g):
    return x * lax.rsqrt(jnp.mean(x * x, axis=-1, keepdims=True) + EPS) * g


def _gelu(x):
    c = math.sqrt(2.0 / math.pi)
    return x * (0.5 * (1.0 + jnp.tanh(c * (x + 0.044715 * (x * x * x)))))


def _layernorm(x, g, b):
    mu = jnp.mean(x, axis=-1, keepdims=True)
    xc = x - mu
    return xc * lax.rsqrt(jnp.mean(xc * xc, axis=-1, keepdims=True) + EPS) * g + b


def _lane_iota(shape):
    return lax.broadcasted_iota(jnp.int32, shape, len(shape) - 1)


def _route(xn2, wr_ref, br_ref):
    m = xn2.shape[0]
    xh = xn2.astype(BF16)
    xl = (xn2 - xh.astype(F32)).astype(BF16)
    r = jnp.dot(jnp.concatenate([xh, xl], axis=0), wr_ref[...], preferred_element_type=F32)
    r = r[:m] + r[m:]
    lane = _lane_iota((m, LANES))
    lane_f = lane.astype(F32)
    logits = jnp.where(lane < N_EXPERTS, r + pltpu.roll(r, LANES - N_EXPERTS, 1) + br_ref[...], NEG)
    work = logits
    sel = jnp.zeros((m, LANES), F32)
    top = None
    z = None
    for _ in range(TOP_K):
        mx = jnp.max(work, axis=-1, keepdims=True)
        first = jnp.min(jnp.where(work == mx, lane_f, float(LANES)), axis=-1, keepdims=True)
        hit = lane_f == first
        sel = jnp.where(hit, 1.0, sel)
        work = jnp.where(hit, NEG, work)
        if top is None:
            top = mx
            z = jnp.ones_like(mx)
        else:
            z = z + jnp.exp(mx - top)
    gates = jnp.where(sel > 0.0, jnp.exp(logits - top) / z, 0.0)
    return gates, sel


def _prompt_kernel(sinks_ref, x_ref, gmix_ref, win_ref, lng_ref, lnb_ref, wsp_ref, bsp_ref,
                   goa_ref, gob_ref, wo_ref, gmoe_ref, wr_ref, br_ref,
                   h1_ref, xn_ref, gm_ref, sm_ref, k_ref, v_ref,
                   z_s, kv_s, cat_s):
    g = pl.program_id(0)
    j = g % (SEQ // TM)

    @pl.when(g >= N_PROMPT // TM)
    def _():
        h1_ref[...] = jnp.zeros_like(h1_ref)
        xn_ref[...] = jnp.zeros_like(xn_ref)
        gm_ref[...] = jnp.zeros_like(gm_ref)
        sm_ref[...] = jnp.zeros_like(sm_ref)

    @pl.when(g < N_PROMPT // TM)
    def _():
        _prompt_tile(j, sinks_ref, x_ref, gmix_ref, win_ref, lng_ref, lnb_ref, wsp_ref, bsp_ref,
                     goa_ref, gob_ref, wo_ref, gmoe_ref, wr_ref, br_ref,
                     h1_ref, xn_ref, gm_ref, sm_ref, k_ref, v_ref, z_s, kv_s, cat_s)


def _prompt_tile(j, sinks_ref, x_ref, gmix_ref, win_ref, lng_ref, lnb_ref, wsp_ref, bsp_ref,
                 goa_ref, gob_ref, wo_ref, gmoe_ref, wr_ref, br_ref,
                 h1_ref, xn_ref, gm_ref, sm_ref, k_ref, v_ref, z_s, kv_s, cat_s):
    @pl.when(j == 0)
    def _():
        kv_s[0:CHUNK, :] = jnp.zeros((CHUNK, 2 * KV_WIDTH), F32)

    xn = _rms(x_ref[...], gmix_ref[...]).astype(BF16)
    z_s[...] = jnp.dot(xn, win_ref[...], preferred_element_type=F32)
    kv_s[CHUNK:, :] = z_s[:, 2 * A_WIDTH + B_WIDTH:]

    lane = _lane_iota((CHUNK, LANES))
    lo = lane < HEAD_DIM
    lane2 = _lane_iota((2 * CHUNK, LANES))
    lo2 = lane2 < HEAD_DIM
    qi = lax.broadcasted_iota(jnp.int32, (CHUNK, CHUNK), 0)
    kc = lax.broadcasted_iota(jnp.int32, (CHUNK, CHUNK), 1)
    from_prev = kc > qi
    dist = jnp.where(from_prev, qi + CHUNK - kc, qi - kc).astype(F32)

    def chunk_body(c, carry):
        r0 = pl.multiple_of(c * CHUNK, CHUNK)
        rows = pl.ds(r0, CHUNK)
        u = _gelu(z_s[rows, 0:A_WIDTH])
        va = _layernorm(_gelu(z_s[rows, A_WIDTH:2 * A_WIDTH]), lng_ref[...], lnb_ref[...])
        vab = va.astype(BF16)
        slabs = []
        for p in range(A_WIDTH // LANES):
            slab = vab[:, p * LANES:(p + 1) * LANES]
            m0 = jnp.dot(wsp_ref[2 * p], slab, preferred_element_type=F32)
            m1 = jnp.dot(wsp_ref[2 * p + 1], slab, preferred_element_type=F32)
            slabs.append(jnp.where(lo, m0, m1))
        ya = u * (jnp.concatenate(slabs, axis=-1) + bsp_ref[...])
        ya_n = _rms(ya, goa_ref[...])
        k2 = kv_s[pl.ds(r0, 2 * CHUNK), 0:KV_WIDTH]
        v2 = kv_s[pl.ds(r0, 2 * CHUNK), KV_WIDTH:2 * KV_WIDTH]
        k2r = pltpu.roll(k2, HEAD_DIM, 1)
        v2r = pltpu.roll(v2, HEAD_DIM, 1)
        kd = (jnp.where(lo2, k2, k2r).astype(BF16), jnp.where(lo2, k2r, k2).astype(BF16))
        vd = (jnp.where(lo2, v2, v2r).astype(BF16), jnp.where(lo2, v2r, v2).astype(BF16))
        prev_ok = (j > 0) | (c > 0)
        masked = from_prev & jnp.logical_not(prev_ok)
        yb_slabs = []
        for kv in range(2):
            q0 = z_s[rows, 2 * A_WIDTH + (2 * kv) * LANES:2 * A_WIDTH + (2 * kv + 1) * LANES]
            q1 = z_s[rows, 2 * A_WIDTH + (2 * kv + 1) * LANES:2 * A_WIDTH + (2 * kv + 2) * LANES]
            lhs = jnp.concatenate([jnp.where(lo, q0, 0.0), jnp.where(lo, 0.0, q0),
                                   jnp.where(lo, q1, 0.0), jnp.where(lo, 0.0, q1)], axis=0).astype(BF16)
            s_all = lax.dot_general(lhs, kd[kv], (((1,), (1,)), ((), ())), preferred_element_type=F32)
            probs = []
            for i in range(4):
                h = 4 * kv + i
                slope = 2.0 ** (-(h + 1))
                sink = sinks_ref[h]
                sh = s_all[i * CHUNK:(i + 1) * CHUNK]
                s = jnp.where(from_prev, sh[:, :CHUNK], sh[:, CHUNK:]) * (HEAD_DIM ** -0.5) - slope * dist
                s = jnp.where(masked, NEG, s)
                mx = jnp.maximum(jnp.max(s, axis=-1, keepdims=True), sink)
                e = jnp.exp(s - mx)
                den = jnp.sum(e, axis=-1, keepdims=True) + jnp.exp(sink - mx)
                p = e * (1.0 / den)
                probs.append(jnp.concatenate([jnp.where(from_prev, p, 0.0), jnp.where(from_prev, 0.0, p)], axis=-1))
            pm = jnp.concatenate(probs, axis=0).astype(BF16)
            o = jnp.dot(pm, vd[kv], preferred_element_type=F32)
            yb_slabs.append(jnp.where(lo, o[0:CHUNK], o[CHUNK:2 * CHUNK]))
            yb_slabs.append(jnp.where(lo, o[2 * CHUNK:3 * CHUNK], o[3 * CHUNK:4 * CHUNK]))
        yb_n = _rms(jnp.concatenate(yb_slabs, axis=-1), gob_ref[...])
        cat_s[rows, 0:A_WIDTH] = ya_n.astype(BF16)
        cat_s[rows, A_WIDTH:] = yb_n.astype(BF16)
        return carry

    lax.fori_loop(0, TM // CHUNK, chunk_body, 0)

    kv_s[0:CHUNK, :] = kv_s[TM:TM + CHUNK, :]
    k_ref[...] = kv_s[TM:TM + CHUNK, 0:KV_WIDTH]
    v_ref[...] = kv_s[TM:TM + CHUNK, KV_WIDTH:]

    h1 = x_ref[...] + jnp.dot(cat_s[...], wo_ref[...], preferred_element_type=F32)
    h1_ref[...] = h1
    xn2 = _rms(h1, gmoe_ref[...])
    xn_ref[...] = xn2.astype(BF16)
    gates, sel = _route(xn2, wr_ref, br_ref)
    gm_ref[...] = gates
    sm_ref[...] = sel


def _full(shape):
    n = len(shape)
    return pl.BlockSpec(shape, lambda *_: (0,) * n)


def _prompt_call(x, sinks, gmix, win, lng, lnb, wsp, bsp, goa, gob, wo, gmoe, wr, br):
    real = N_PROMPT // TM
    row = lambda g: (g, 0)
    seq = lambda g: (jnp.minimum(g, real - 1) // (SEQ // TM), 0, 0)
    return pl.pallas_call(
        _prompt_kernel,
        grid=(N_PAD // TM,),
        in_specs=[
            pl.BlockSpec(memory_space=pltpu.SMEM),
            pl.BlockSpec((TM, D_MODEL), lambda g: (jnp.minimum(g, real - 1), 0)),
            _full((1, D_MODEL)), _full((D_MODEL, IN_WIDTH)), _full((1, A_WIDTH)), _full((1, A_WIDTH)),
            _full((8, CHUNK, CHUNK)), _full((CHUNK, A_WIDTH)), _full((1, A_WIDTH)), _full((1, B_WIDTH)),
            _full((D_MODEL, D_MODEL)), _full((1, D_MODEL)), _full((D_MODEL, LANES)), _full((1, LANES)),
        ],
        out_specs=[
            pl.BlockSpec((TM, D_MODEL), row),
            pl.BlockSpec((TM, D_MODEL), row),
            pl.BlockSpec((TM, LANES), row),
            pl.BlockSpec((TM, LANES), row),
            pl.BlockSpec((None, CHUNK, KV_WIDTH), seq),
            pl.BlockSpec((None, CHUNK, KV_WIDTH), seq),
        ],
        out_shape=[
            jax.ShapeDtypeStruct((N_PAD, D_MODEL), F32),
            jax.ShapeDtypeStruct((N_PAD, D_MODEL), BF16),
            jax.ShapeDtypeStruct((N_PAD, LANES), F32),
            jax.ShapeDtypeStruct((N_PAD, LANES), F32),
            jax.ShapeDtypeStruct((BATCH, CHUNK, KV_WIDTH), F32),
            jax.ShapeDtypeStruct((BATCH, CHUNK, KV_WIDTH), F32),
        ],
        scratch_shapes=[
            pltpu.VMEM((TM, IN_WIDTH), F32),
            pltpu.VMEM((TM + CHUNK, 2 * KV_WIDTH), F32),
            pltpu.VMEM((TM, D_MODEL), BF16),
        ],
        compiler_params=pltpu.CompilerParams(
            dimension_semantics=("arbitrary",), vmem_limit_bytes=VMEM_LIMIT),
        name="prompt_premoe",
    )(sinks, x, gmix, win, lng, lnb, wsp, bsp, goa, gob, wo, gmoe, wr, br)


def _sample_kernel(sinks_ref, x_ref, ck_ref, cv_ref, gmix_ref, win_ref, lng_ref, lnb_ref, w00_ref, b0_ref,
                   goa_ref, gob_ref, wo_ref, gmoe_ref, wr_ref, br_ref,
                   h1_in, xn_in, gm_in, sm_in,
                   h1_ref, xn_ref, gm_ref, sm_ref, nk_ref, nv_ref, va_ref):
    del h1_in, xn_in, gm_in, sm_in
    t = SAMPLE_TILE
    nkeys = t * CHUNK

    if True:
        x = x_ref[...]
        xn = _rms(x, gmix_ref[...]).astype(BF16)
        z = jnp.dot(xn, win_ref[...], preferred_element_type=F32)
        u = _gelu(z[:, 0:A_WIDTH])
        va = _layernorm(_gelu(z[:, A_WIDTH:2 * A_WIDTH]), lng_ref[...], lnb_ref[...])
        va_ref[...] = va
        ya_n = _rms(u * (w00_ref[...] * va + b0_ref[...]), goa_ref[...])

        knew = z[:, 2 * A_WIDTH + B_WIDTH:2 * A_WIDTH + B_WIDTH + KV_WIDTH]
        vnew = z[:, 2 * A_WIDTH + B_WIDTH + KV_WIDTH:]
        lane = _lane_iota((t, LANES))
        lo = lane < HEAD_DIM
        stacked = []
        for h in range(B_HEADS):
            q = z[:, 2 * A_WIDTH + (h // 2) * LANES:2 * A_WIDTH + (h // 2 + 1) * LANES]
            qh = jnp.where(lo if h % 2 == 0 else jnp.logical_not(lo), q, 0.0)
            if h % 2 != h // 4:
                qh = pltpu.roll(qh, HEAD_DIM, 1)
            stacked.append(qh)
        qs = jnp.concatenate(stacked, axis=0)
        rows = B_HEADS * t
        ridx = lax.broadcasted_iota(jnp.int32, (rows, 1), 0)
        slope = jnp.zeros((rows, 1), F32)
        sink = jnp.zeros((rows, 1), F32)
        for h in range(B_HEADS):
            in_h = (ridx >= h * t) & (ridx < (h + 1) * t)
            slope = jnp.where(in_h, 2.0 ** (-(h + 1)), slope)
            sink = jnp.where(in_h, sinks_ref[h], sink)
        s_c = lax.dot_general(qs.astype(BF16), ck_ref[...].astype(BF16), (((1,), (1,)), ((), ())),
                              preferred_element_type=F32)
        rsamp = lax.broadcasted_iota(jnp.int32, (rows, nkeys), 0) % t
        col = lax.broadcasted_iota(jnp.int32, (rows, nkeys), 1)
        pos = col % CHUNK
        own = ((col // CHUNK) == rsamp) & (pos >= 1)
        s_c = s_c * (HEAD_DIM ** -0.5) - slope * (CHUNK - pos).astype(F32)
        s_c = jnp.where(own, s_c, NEG)
        kn8 = jnp.concatenate([knew] * B_HEADS, axis=0)
        vn8 = jnp.concatenate([vnew] * B_HEADS, axis=0)
        s_n = jnp.sum(qs * kn8, axis=-1, keepdims=True) * (HEAD_DIM ** -0.5)
        mx = jnp.maximum(jnp.maximum(jnp.max(s_c, axis=-1, keepdims=True), s_n), sink)
        e_c = jnp.exp(s_c - mx)
        e_n = jnp.exp(s_n - mx)
        inv = 1.0 / (jnp.sum(e_c, axis=-1, keepdims=True) + e_n + jnp.exp(sink - mx))
        o = jnp.dot((e_c * inv).astype(BF16), cv_ref[...].astype(BF16), preferred_element_type=F32)
        o = o + (e_n * inv) * vn8
        yb_slabs = []
        for p in range(B_WIDTH // LANES):
            outs = []
            for half in range(2):
                h = 2 * p + half
                oh = o[h * t:(h + 1) * t]
                oh = jnp.where(lo if h // 4 == 0 else jnp.logical_not(lo), oh, 0.0)
                if half != h // 4:
                    oh = pltpu.roll(oh, HEAD_DIM, 1)
                outs.append(oh)
            yb_slabs.append(outs[0] + outs[1])
        yb_n = _rms(jnp.concatenate(yb_slabs, axis=-1), gob_ref[...])

        cat = jnp.concatenate([ya_n, yb_n], axis=-1).astype(BF16)
        h1 = x + jnp.dot(cat, wo_ref[...], preferred_element_type=F32)
        xn2 = _rms(h1, gmoe_ref[...])
        gates, sel = _route(xn2, wr_ref, br_ref)
        h1_ref[...] = h1
        xn_ref[...] = xn2.astype(BF16)
        gm_ref[...] = gates
        sm_ref[...] = sel

        nk_ref[...] = pltpu.roll(ck_ref[...], nkeys - 1, 0)
        nv_ref[...] = pltpu.roll(cv_ref[...], nkeys - 1, 0)
        for b in range(t):
            nk_ref[b * CHUNK + CHUNK - 1:b * CHUNK + CHUNK, :] = knew[b:b + 1, :]
            nv_ref[b * CHUNK + CHUNK - 1:b * CHUNK + CHUNK, :] = vnew[b:b + 1, :]


def _sample_call(x, ck, cv, sinks, gmix, win, lng, lnb, w00, b0, goa, gob, wo, gmoe, wr, br, h1, xn, gm, sm):
    t = SAMPLE_TILE
    steps = DEC_BATCH // t
    base = N_PROMPT // t
    inrow = lambda i: (i, 0)
    outrow = lambda i: (base + i, 0)
    anyspec = pl.BlockSpec(memory_space=pl.ANY)
    return pl.pallas_call(
        _sample_kernel,
        grid=(steps,),
        in_specs=[
            pl.BlockSpec(memory_space=pltpu.SMEM),
            pl.BlockSpec((t, D_MODEL), inrow),
            pl.BlockSpec((t * CHUNK, KV_WIDTH), inrow),
            pl.BlockSpec((t * CHUNK, KV_WIDTH), inrow),
            _full((1, D_MODEL)), _full((D_MODEL, IN_WIDTH)), _full((1, A_WIDTH)), _full((1, A_WIDTH)),
            _full((1, A_WIDTH)), _full((1, A_WIDTH)), _full((1, A_WIDTH)), _full((1, B_WIDTH)),
            _full((D_MODEL, D_MODEL)), _full((1, D_MODEL)), _full((D_MODEL, LANES)), _full((1, LANES)),
            anyspec, anyspec, anyspec, anyspec,
        ],
        out_specs=[
            pl.BlockSpec((t, D_MODEL), outrow),
            pl.BlockSpec((t, D_MODEL), outrow),
            pl.BlockSpec((t, LANES), outrow),
            pl.BlockSpec((t, LANES), outrow),
            pl.BlockSpec((t * CHUNK, KV_WIDTH), inrow),
            pl.BlockSpec((t * CHUNK, KV_WIDTH), inrow),
            pl.BlockSpec((t, A_WIDTH), inrow),
        ],
        out_shape=[
            jax.ShapeDtypeStruct((N_PAD, D_MODEL), F32),
            jax.ShapeDtypeStruct((N_PAD, D_MODEL), BF16),
            jax.ShapeDtypeStruct((N_PAD, LANES), F32),
            jax.ShapeDtypeStruct((N_PAD, LANES), F32),
            jax.ShapeDtypeStruct((DEC_BATCH * CHUNK, KV_WIDTH), F32),
            jax.ShapeDtypeStruct((DEC_BATCH * CHUNK, KV_WIDTH), F32),
            jax.ShapeDtypeStruct((DEC_BATCH, A_WIDTH), F32),
        ],
        input_output_aliases={16: 0, 17: 1, 18: 2, 19: 3},
        compiler_params=pltpu.CompilerParams(
            dimension_semantics=("arbitrary",), vmem_limit_bytes=VMEM_LIMIT),
        name="sample_premoe",
    )(sinks, x, ck, cv, gmix, win, lng, lnb, w00, b0, goa, gob, wo, gmoe, wr, br, h1, xn, gm, sm)


def _decode_kernel(sinks_ref, x_ref, ck_ref, cv_ref, gmix_ref, win_ref, lng_ref, lnb_ref, w00_ref, b0_ref,
                   goa_ref, gob_ref, wo_ref, gmoe_ref, wr_ref, br_ref,
                   h1_in, xn_in, gm_in, sm_in,
                   h1_ref, xn_ref, gm_ref, sm_ref, nk_ref, nv_ref, va_ref,
                   q_s, kn_s, vn_s, ya_s, yb_s):
    del h1_in, xn_in, gm_in, sm_in
    i = pl.program_id(0)
    t = DEC_TILE
    per_seq = CHUNK * 2
    ncols = t * per_seq

    @pl.when(i == 0)
    def _():
        xn = _rms(x_ref[...], gmix_ref[...]).astype(BF16)
        z = jnp.dot(xn, win_ref[...], preferred_element_type=F32)
        u = _gelu(z[:, 0:A_WIDTH])
        va = _layernorm(_gelu(z[:, A_WIDTH:2 * A_WIDTH]), lng_ref[...], lnb_ref[...])
        va_ref[...] = va
        ya_s[...] = _rms(u * (w00_ref[...] * va + b0_ref[...]), goa_ref[...])
        q_s[...] = z[:, 2 * A_WIDTH:2 * A_WIDTH + B_WIDTH]
        kn_s[...] = z[:, 2 * A_WIDTH + B_WIDTH:2 * A_WIDTH + B_WIDTH + KV_WIDTH]
        vn_s[...] = z[:, 2 * A_WIDTH + B_WIDTH + KV_WIDTH:]

    rows = pl.ds(pl.multiple_of(i * t, t), t)
    q = q_s[rows, :]
    kn = kn_s[rows, :]
    vn = vn_s[rows, :]
    qs = jnp.concatenate([q[:, h * HEAD_DIM:(h + 1) * HEAD_DIM] for h in range(B_HEADS)], axis=0)
    kn8 = jnp.concatenate([kn[:, (h // 4) * HEAD_DIM:(h // 4 + 1) * HEAD_DIM] for h in range(B_HEADS)], axis=0)
    vn8 = jnp.concatenate([vn[:, (h // 4) * HEAD_DIM:(h // 4 + 1) * HEAD_DIM] for h in range(B_HEADS)], axis=0)
    nrows = B_HEADS * t
    ridx = lax.broadcasted_iota(jnp.int32, (nrows, 1), 0)
    slope = jnp.zeros((nrows, 1), F32)
    sink = jnp.zeros((nrows, 1), F32)
    for h in range(B_HEADS):
        in_h = (ridx >= h * t) & (ridx < (h + 1) * t)
        slope = jnp.where(in_h, 2.0 ** (-(h + 1)), slope)
        sink = jnp.where(in_h, sinks_ref[h], sink)
    s_c = lax.dot_general(qs.astype(BF16), ck_ref[...].astype(BF16), (((1,), (1,)), ((), ())),
                          preferred_element_type=F32)
    rr = lax.broadcasted_iota(jnp.int32, (nrows, ncols), 0)
    col = lax.broadcasted_iota(jnp.int32, (nrows, ncols), 1)
    pos = (col // 2) % CHUNK
    own = ((col // per_seq) == (rr % t)) & ((col % 2) == (rr // (4 * t))) & (pos >= 1)
    s_c = s_c * (HEAD_DIM ** -0.5) - slope * (CHUNK - pos).astype(F32)
    s_c = jnp.where(own, s_c, NEG)
    s_n = jnp.sum(qs * kn8, axis=-1, keepdims=True) * (HEAD_DIM ** -0.5)
    mx = jnp.maximum(jnp.maximum(jnp.max(s_c, axis=-1, keepdims=True), s_n), sink)
    e_c = jnp.exp(s_c - mx)
    e_n = jnp.exp(s_n - mx)
    inv = 1.0 / (jnp.sum(e_c, axis=-1, keepdims=True) + e_n + jnp.exp(sink - mx))
    o = jnp.dot((e_c * inv).astype(BF16), cv_ref[...].astype(BF16), preferred_element_type=F32)
    o = o + (e_n * inv) * vn8
    yb_s[rows, :] = jnp.concatenate([o[h * t:(h + 1) * t] for h in range(B_HEADS)], axis=-1)

    nk_ref[...] = pltpu.roll(ck_ref[...], ncols - 2, 0)
    nv_ref[...] = pltpu.roll(cv_ref[...], ncols - 2, 0)
    for b in range(t):
        for kv in range(2):
            r = b * per_seq + per_seq - 2 + kv
            nk_ref[r:r + 1, :] = kn[b:b + 1, kv * HEAD_DIM:(kv + 1) * HEAD_DIM]
            nv_ref[r:r + 1, :] = vn[b:b + 1, kv * HEAD_DIM:(kv + 1) * HEAD_DIM]

    @pl.when(i == pl.num_programs(0) - 1)
    def _():
        yb_n = _rms(yb_s[...], gob_ref[...])
        cat = jnp.concatenate([ya_s[...], yb_n], axis=-1).astype(BF16)
        h1 = x_ref[...] + jnp.dot(cat, wo_ref[...], preferred_element_type=F32)
        xn2 = _rms(h1, gmoe_ref[...])
        gates, sel = _route(xn2, wr_ref, br_ref)
        h1_ref[...] = h1
        xn_ref[...] = xn2.astype(BF16)
        gm_ref[...] = gates
        sm_ref[...] = sel


def _decode_call(x, ck, cv, sinks, gmix, win, lng, lnb, w00, b0, goa, gob, wo, gmoe, wr, br, h1, xn, gm, sm):
    t = DEC_TILE
    per_seq = CHUNK * 2
    cache = pl.BlockSpec((t * per_seq, HEAD_DIM), lambda i: (i, 0))
    tok = lambda width: pl.BlockSpec((DEC_BATCH, width), lambda i: (N_PROMPT // DEC_BATCH, 0))
    anyspec = pl.BlockSpec(memory_space=pl.ANY)
    return pl.pallas_call(
        _decode_kernel,
        grid=(DEC_BATCH // t,),
        in_specs=[
            pl.BlockSpec(memory_space=pltpu.SMEM),
            _full((DEC_BATCH, D_MODEL)), cache, cache,
            _full((1, D_MODEL)), _full((D_MODEL, IN_WIDTH)), _full((1, A_WIDTH)), _full((1, A_WIDTH)),
            _full((1, A_WIDTH)), _full((1, A_WIDTH)), _full((1, A_WIDTH)), _full((1, B_WIDTH)),
            _full((D_MODEL, D_MODEL)), _full((1, D_MODEL)), _full((D_MODEL, LANES)), _full((1, LANES)),
            anyspec, anyspec, anyspec, anyspec,
        ],
        out_specs=[tok(D_MODEL), tok(D_MODEL), tok(LANES), tok(LANES), cache, cache, _full((DEC_BATCH, A_WIDTH))],
        out_shape=[
            jax.ShapeDtypeStruct((N_PAD, D_MODEL), F32),
            jax.ShapeDtypeStruct((N_PAD, D_MODEL), BF16),
            jax.ShapeDtypeStruct((N_PAD, LANES), F32),
            jax.ShapeDtypeStruct((N_PAD, LANES), F32),
            jax.ShapeDtypeStruct((DEC_BATCH * per_seq, HEAD_DIM), F32),
            jax.ShapeDtypeStruct((DEC_BATCH * per_seq, HEAD_DIM), F32),
            jax.ShapeDtypeStruct((DEC_BATCH, A_WIDTH), F32),
        ],
        scratch_shapes=[pltpu.VMEM((DEC_BATCH, B_WIDTH), F32), pltpu.VMEM((DEC_BATCH, KV_WIDTH), F32),
                        pltpu.VMEM((DEC_BATCH, KV_WIDTH), F32), pltpu.VMEM((DEC_BATCH, A_WIDTH), F32),
                        pltpu.VMEM((DEC_BATCH, B_WIDTH), F32)],
        input_output_aliases={16: 0, 17: 1, 18: 2, 19: 3},
        compiler_params=pltpu.CompilerParams(
            dimension_semantics=("arbitrary",), vmem_limit_bytes=VMEM_LIMIT),
        name="sample_premoe",
    )(sinks, x, ck, cv, gmix, win, lng, lnb, w00, b0, goa, gob, wo, gmoe, wr, br, h1, xn, gm, sm)


def _plan_kernel(sm_ref,
                 destm_ref, posm_ref, destt_ref, xbe_ref, live_ref, nxblk_ref, stab_ref, ctab_ref, astart_ref,
                 nwin_ref, base_s, pstart_s):
    ph = pl.program_id(0)
    step = pl.program_id(1)
    lane = _lane_iota((1, LANES))

    @pl.when((ph == 0) & (step == 0))
    def _():
        base_s[...] = jnp.zeros_like(base_s)

    @pl.when(ph == 0)
    def _():
        base_s[...] += jnp.sum(sm_ref[...], axis=0, keepdims=True)

    @pl.when((ph == 1) & (step == 0))
    def _():
        counts = base_s[...]
        padded = jnp.floor((counts + (DISP_CHUNK + EXP_BLOCK - 1)) * (1.0 / EXP_BLOCK)) * EXP_BLOCK
        padded = jnp.where(counts > 0.0, padded, 0.0)
        pend = padded
        for s in (1, 2, 4, 8, 16):
            pend = pend + jnp.where(lane >= s, pltpu.roll(pend, s, 1), 0.0)
        spare = (N_ROWS + lane * DISP_CHUNK).astype(F32)
        pstart_s[...] = jnp.where(counts > 0.0, pend - padded, spare)
        base_s[...] = jnp.zeros_like(base_s)
        brow = lax.broadcasted_iota(jnp.int32, (XBE_ROWS, LANES), 0).astype(F32) * EXP_BLOCK
        done = jnp.where((lane < N_EXPERTS) & (pend <= brow), 1.0, 0.0)
        be = jnp.minimum(jnp.sum(done, axis=-1, keepdims=True), N_EXPERTS - 1.0)
        xbe_ref[...] = jnp.broadcast_to(be, (XBE_ROWS, LANES)).astype(jnp.int32)
        real = jnp.clip(counts - (brow - (pend - padded)), 0.0, float(EXP_BLOCK))
        real = jnp.sum(jnp.where(lane.astype(F32) == be, real, 0.0), axis=-1, keepdims=True)
        groups = jnp.floor((real + (ROW_GROUP - 1)) * (1.0 / ROW_GROUP))
        live_ref[...] = jnp.broadcast_to(groups, (XBE_ROWS, LANES)).astype(jnp.int32)
        total = jnp.sum(jnp.where(lane == N_EXPERTS - 1, pend, 0.0), axis=-1, keepdims=True)
        nxblk_ref[...] = jnp.broadcast_to(total * (1.0 / EXP_BLOCK), (8, LANES)).astype(jnp.int32)
        stab_ref[...] = jnp.zeros_like(stab_ref)
        ctab_ref[...] = jnp.zeros_like(ctab_ref)
        astart_ref[...] = jnp.zeros_like(astart_ref)
        nwin_ref[...] = jnp.zeros_like(nwin_ref)

    @pl.when(ph == 1)
    def _():
        r = lax.broadcasted_iota(jnp.int32, (TOK_TILE, TOK_TILE), 0)
        c = lax.broadcasted_iota(jnp.int32, (TOK_TILE, TOK_TILE), 1)
        lower = jnp.where(c < r, 1.0, 0.0).astype(BF16)
        for q in range(TM // TOK_TILE):
            i = step * (TM // TOK_TILE) + q
            sel = sm_ref[q * TOK_TILE:(q + 1) * TOK_TILE, :]
            cnt = jnp.sum(sel, axis=0, keepdims=True)
            prefix = jnp.dot(lower, sel.astype(BF16), preferred_element_type=F32)
            start = pstart_s[...] + base_s[...]
            dest = jnp.where(sel > 0.0, prefix + start, -1.0)
            destm_ref[q * TOK_TILE:(q + 1) * TOK_TILE, :] = dest
            destt_ref[:, q * TOK_TILE:(q + 1) * TOK_TILE] = dest.T
            has = (cnt > 0.0) & (lane < N_EXPERTS)
            stab_ref[pl.ds(i, 1), :] = start.astype(jnp.int32)
            ctab_ref[pl.ds(i, 1), :] = jnp.where(has, cnt, 0.0).astype(jnp.int32)
            a = jnp.minimum(jnp.floor(start * (1.0 / WIN_ALIGN)) * WIN_ALIGN, float(N_ROWS - WIN))
            nw = jnp.where(has, jnp.floor((start + cnt - a + (WIN - 1)) * (1.0 / WIN)), 0.0)
            posm_ref[q * TOK_TILE:(q + 1) * TOK_TILE, :] = jnp.where(sel > 0.0, prefix + start - a, -1.0)
            astart_ref[pl.ds(i, 1), :] = a.astype(jnp.int32)
            nwin_ref[pl.ds(i, 1), :] = nw.astype(jnp.int32)
            base_s[...] += cnt


def _plan_call(sm):
    tile = lambda ph, i: (i * ph, 0)
    tile_t = lambda ph, i: (0, i * ph)
    tab = jax.ShapeDtypeStruct((TAB_ROWS, LANES), jnp.int32)
    return pl.pallas_call(
        _plan_kernel,
        grid=(2, N_PAD // TM),
        in_specs=[pl.BlockSpec((TM, LANES), lambda ph, i: (i, 0))],
        out_specs=[
            pl.BlockSpec((TM, LANES), tile),
            pl.BlockSpec((TM, LANES), tile),
            pl.BlockSpec((LANES, TM), tile_t),
            _full((XBE_ROWS, LANES)), _full((XBE_ROWS, LANES)), _full((8, LANES)),
            _full((TAB_ROWS, LANES)), _full((TAB_ROWS, LANES)), _full((TAB_ROWS, LANES)), _full((TAB_ROWS, LANES)),
        ],
        out_shape=[
            jax.ShapeDtypeStruct((N_PAD, LANES), F32),
            jax.ShapeDtypeStruct((N_PAD, LANES), F32),
            jax.ShapeDtypeStruct((LANES, N_PAD), F32),
            jax.ShapeDtypeStruct((XBE_ROWS, LANES), jnp.int32),
            jax.ShapeDtypeStruct((XBE_ROWS, LANES), jnp.int32),
            jax.ShapeDtypeStruct((8, LANES), jnp.int32),
            tab, tab, tab, tab,
        ],
        scratch_shapes=[pltpu.VMEM((1, LANES), F32), pltpu.VMEM((1, LANES), F32)],
        compiler_params=pltpu.CompilerParams(
            dimension_semantics=("arbitrary", "arbitrary"), vmem_limit_bytes=VMEM_LIMIT),
        name="moe_plan",
    )(sm)


def _pack_rows(z):
    half = D_MODEL // 2
    lo = lax.bitcast_convert_type(z[:, :half], jnp.uint32) >> 16
    hi = lax.bitcast_convert_type(z[:, half:], jnp.uint32) & jnp.uint32(0xFFFF0000)
    return lax.bitcast_convert_type(hi | lo, jnp.int32)


def _unpack_rows(ref, rows=None):
    rows = ref.shape[0] if rows is None else rows
    flat = ref.reshape(ref.shape[0] * PACK, LANES)
    lo, hi = [], []
    for s in range(PACK):
        w = lax.bitcast_convert_type(flat[pl.ds(s, rows, stride=PACK), :], jnp.uint32)
        lo.append(lax.bitcast_convert_type(w << 16, F32))
        hi.append(lax.bitcast_convert_type(w & jnp.uint32(0xFFFF0000), F32))
    return jnp.concatenate(lo + hi, axis=-1).astype(BF16)


def _dispatch_kernel(stab_ref, ctab_ref, cmax_ref, xn_ref, destt_ref, xs_in, xs_ref,
                     stage0, stage1, stage2, sems, sem2):
    del xs_in
    i = pl.program_id(0)
    last = pl.num_programs(0) - 1
    x = xn_ref[...]
    dt = destt_ref[...]
    rio = lax.broadcasted_iota(jnp.int32, (DISP_CHUNK, 1), 0).astype(F32)

    def chunk_rows(j, stage):
        parts = []
        for e in range(N_EXPERTS):
            first = (stab_ref[i * N_EXPERTS + e] + j * DISP_CHUNK).astype(F32)
            parts.append(jnp.where(dt[e:e + 1, :] == first + rio, 1.0, 0.0).astype(BF16))
        onehot = jnp.concatenate(parts, axis=0)
        words = _pack_rows(jnp.dot(onehot, x, preferred_element_type=F32))
        for s in range(PACK):
            stage[pl.ds(s, N_EXPERTS * DISP_CHUNK, stride=PACK), :] = words[:, s * LANES:(s + 1) * LANES]

    def copy(stage, step, e, j, sem):
        first = stab_ref[step * N_EXPERTS + e] + j * DISP_CHUNK
        rows = stage.reshape(N_EXPERTS * DISP_CHUNK, PACK, LANES)
        return pltpu.make_async_copy(rows.at[pl.ds(e * DISP_CHUNK, DISP_CHUNK)],
                                     xs_ref.at[pl.ds(first, DISP_CHUNK)], sem)

    def step_body(stage, prev_stage, par):
        chunk_rows(0, stage)

        @pl.when(i > 0)
        def _():
            for e in range(N_EXPERTS):
                copy(prev_stage, i - 1, e, 0, sems.at[1 - par, 0]).wait()

        for e in range(N_EXPERTS):
            copy(stage, i, e, 0, sems.at[par, 0]).start()

        @pl.when(i == last)
        def _():
            for e in range(N_EXPERTS):
                copy(stage, i, e, 0, sems.at[par, 0]).wait()

    @pl.when(i % 2 == 0)
    def _():
        step_body(stage0, stage1, 0)

    @pl.when(i % 2 == 1)
    def _():
        step_body(stage1, stage0, 1)

    for j in range(1, TOK_TILE // DISP_CHUNK):

        @pl.when(cmax_ref[i] > j * DISP_CHUNK)
        def _(j=j):
            chunk_rows(j, stage2)
            for e in range(N_EXPERTS):

                @pl.when(ctab_ref[i * N_EXPERTS + e] > j * DISP_CHUNK)
                def _(e=e):
                    cp = copy(stage2, i, e, j, sem2)
                    cp.start()
                    cp.wait()


def _dispatch_call(stab, ctab, cmax, xn, destt, xs_zero):
    stage = pltpu.VMEM((N_EXPERTS * DISP_CHUNK * PACK, LANES), jnp.int32)
    grid_spec = pltpu.PrefetchScalarGridSpec(
        num_scalar_prefetch=3,
        grid=(N_PAD_TILES,),
        in_specs=[
            pl.BlockSpec((TOK_TILE, D_MODEL), lambda i, *_: (i, 0)),
            pl.BlockSpec((N_EXPERTS, TOK_TILE), lambda i, *_: (0, i)),
            pl.BlockSpec(memory_space=pl.ANY),
        ],
        out_specs=pl.BlockSpec(memory_space=pl.ANY),
        scratch_shapes=[stage, stage, stage, pltpu.SemaphoreType.DMA((2, N_EXPERTS)),
                        pltpu.SemaphoreType.DMA],
    )
    return pl.pallas_call(
        _dispatch_kernel,
        grid_spec=grid_spec,
        out_shape=jax.ShapeDtypeStruct((XS_ROWS, PACK, LANES), jnp.int32),
        input_output_aliases={5: 0},
        compiler_params=pltpu.CompilerParams(
            dimension_semantics=("arbitrary",), vmem_limit_bytes=VMEM_LIMIT),
        name="moe_dispatch",
    )(stab, ctab, cmax, xn, destt, xs_zero)


def _expert_kernel(blke_ref, nblk_ref, live_ref, xs_ref, wgu_hbm, bgu_ref, wdn_hbm, bdn_ref,
                   ys_ref, wgu_f, wdn_f, wgu_s, wdn_s, sems):
    b = pl.program_id(0)
    used = b < nblk_ref[0]
    prev = blke_ref[jnp.maximum(b - 1, 0)]
    fresh = used & ((b == 0) | (blke_ref[b] != prev))

    def fetch(e):
        return (pltpu.make_async_copy(wgu_hbm.at[e], wgu_f, sems.at[0]),
                pltpu.make_async_copy(wdn_hbm.at[e], wdn_f, sems.at[1]))

    @pl.when(b == 0)
    def _():
        for cp in fetch(blke_ref[0]):
            cp.start()

    @pl.when(fresh)
    def _():
        for cp in fetch(blke_ref[b]):
            cp.wait()
        wgu_s[...] = wgu_f[...].astype(BF16)
        wdn_s[...] = wdn_f[...].astype(BF16)

        nxt = lax.while_loop(lambda p: (p < nblk_ref[0]) & (blke_ref[jnp.minimum(p, N_XBLOCKS - 1)] == blke_ref[b]),
                             lambda p: p + 1, b + 1)

        @pl.when(nxt < nblk_ref[0])
        def _():
            for cp in fetch(blke_ref[jnp.minimum(nxt, N_XBLOCKS - 1)]):
                cp.start()

    for groups in range(1, EXP_BLOCK // ROW_GROUP + 1):
        rows = groups * ROW_GROUP

        @pl.when(used & (live_ref[b] == groups))
        def _(rows=rows):
            hid = jnp.dot(_unpack_rows(xs_ref, rows), wgu_s[...], preferred_element_type=F32) + bgu_ref[...]
            gate = jnp.minimum(hid[:, :D_FF], SWIGLU_LIMIT)
            up = jnp.clip(hid[:, D_FF:], -SWIGLU_LIMIT, SWIGLU_LIMIT)
            act = (up + 1.0) * gate * jax.nn.sigmoid(SWIGLU_ALPHA * gate)
            y = jnp.dot(act.astype(BF16), wdn_s[...], preferred_element_type=F32) + bdn_ref[...]
            ys_ref[0:rows, :] = y.astype(BF16)
            if rows < EXP_BLOCK:
                ys_ref[rows:, :] = jnp.zeros((EXP_BLOCK - rows, D_MODEL), BF16)

    @pl.when(jnp.logical_not(used) | (live_ref[b] == 0))
    def _():
        ys_ref[...] = jnp.zeros_like(ys_ref)


def _expert_call(blke, nblk, live, xs, wgu, bgu, wdn, bdn):
    grid_spec = pltpu.PrefetchScalarGridSpec(
        num_scalar_prefetch=3,
        grid=(N_XBLOCKS,),
        in_specs=[
            pl.BlockSpec((EXP_BLOCK, PACK, LANES), lambda b, be, *_: (b, 0, 0)),
            pl.BlockSpec(memory_space=pl.ANY),
            pl.BlockSpec((None, 1, 2 * D_FF), lambda b, be, *_: (be[b], 0, 0)),
            pl.BlockSpec(memory_space=pl.ANY),
            pl.BlockSpec((None, 1, D_MODEL), lambda b, be, *_: (be[b], 0, 0)),
        ],
        out_specs=pl.BlockSpec((EXP_BLOCK, D_MODEL), lambda b, be, *_: (b, 0)),
        scratch_shapes=[pltpu.VMEM((D_MODEL, 2 * D_FF), F32), pltpu.VMEM((D_FF, D_MODEL), F32),
                        pltpu.VMEM((D_MODEL, 2 * D_FF), BF16), pltpu.VMEM((D_FF, D_MODEL), BF16),
                        pltpu.SemaphoreType.DMA((2,))],
    )
    return pl.pallas_call(
        _expert_kernel,
        grid_spec=grid_spec,
        out_shape=jax.ShapeDtypeStruct((N_ROWS, D_MODEL), BF16),
        compiler_params=pltpu.CompilerParams(
            dimension_semantics=("arbitrary",), vmem_limit_bytes=VMEM_LIMIT),
        name="moe_experts",
    )(blke, nblk, live, xs, wgu, bgu, wdn, bdn)


def _combine_kernel(*refs):
    astart_ref, nwin_ref, over_ref = refs[0:3]
    win_refs = refs[3:3 + N_EXPERTS]
    (destm_ref, posm_ref, gm_ref, rep_ref, h1_ref, plep_ref, ples_ref, gple_ref, wpg_ref, wpp_ref, gfin_ref,
     ys_any, yp_ref, ysm_ref, moe_s, tmp_s, sem) = refs[3 + N_EXPERTS:]
    i = pl.program_id(0)
    dest = destm_ref[...]
    gates = gm_ref[...]
    lane = _lane_iota((COMB_TILE, LANES))
    lane_f = lane.astype(F32)
    lo = lane < WIN
    rep = rep_ref[...]
    pos_rep = jnp.dot(posm_ref[...].astype(BF16), rep, preferred_element_type=F32)
    gate_rep = jnp.dot(gates.astype(BF16), rep, preferred_element_type=F32)
    within = (_lane_iota((COMB_TILE, N_EXPERTS * WIN)) % WIN).astype(F32)
    gsel = jnp.where(pos_rep == within, gate_rep, 0.0).astype(BF16)
    moe = jnp.zeros((COMB_TILE, D_MODEL), F32)
    group = 4
    for g0 in range(0, N_EXPERTS, group):
        ywin = jnp.concatenate([win_refs[g0 + q][...] for q in range(group)], axis=0)
        moe = moe + jnp.dot(gsel[:, g0 * WIN:(g0 + group) * WIN], ywin, preferred_element_type=F32)
    mrows = pl.ds(pl.multiple_of((i % TAIL_TILES) * COMB_TILE, COMB_TILE), COMB_TILE)
    moe_s[mrows, :] = moe

    @pl.when(over_ref[i] > 0)
    def _():
        tmp_s[...] = jnp.zeros_like(tmp_s)

        def per_expert(e, carry):
            a = astart_ref[i * N_EXPERTS + e]
            dcol = jnp.sum(jnp.where(lane == e, dest, 0.0), axis=-1, keepdims=True)
            gcol = jnp.sum(jnp.where(lane == e, gates, 0.0), axis=-1, keepdims=True)

            def per_window(w, carry2):
                first = a + w * WIN
                start = pl.multiple_of(jnp.minimum(first, N_ROWS - WIN), WIN_ALIGN)
                cp = pltpu.make_async_copy(ys_any.at[pl.ds(start, WIN)], tmp_s.at[pl.ds(0, WIN)], sem)
                cp.start()
                cp.wait()
                hit = lo & (dcol == start.astype(F32) + lane_f) & (dcol >= first.astype(F32))
                gsel = jnp.where(hit, gcol, 0.0).astype(BF16)
                moe_s[mrows, :] += jnp.dot(gsel, tmp_s[...], preferred_element_type=F32)
                return carry2

            return lax.fori_loop(1, nwin_ref[i * N_EXPERTS + e], per_window, carry)

        lax.fori_loop(0, N_EXPERTS, per_expert, 0)

    def tail(h1, moe_rows, ple):
        h2 = h1 + moe_rows
        hn = _rms(h2, gple_ref[...]).astype(BF16)
        gate = jax.nn.sigmoid(jnp.dot(hn, wpg_ref[...], preferred_element_type=F32))
        proj = jnp.dot(ple.astype(BF16), wpp_ref[...], preferred_element_type=F32)
        return _rms(h2 + gate * proj, gfin_ref[...])

    is_sample = i == N_CTILES - 1

    @pl.when((i % TAIL_TILES == TAIL_TILES - 1) & jnp.logical_not(is_sample))
    def _():
        yp_ref[...] = tail(h1_ref[...], moe_s[...], plep_ref[...])

    @pl.when(is_sample)
    def _():
        ysm_ref[...] = tail(h1_ref[0:DEC_BATCH, :], moe_s[0:DEC_BATCH, :], ples_ref[...])


def _combine_call(astart, nwin, over, ys, destm, posm, gm, h1, plep, ples, gple, wpg, wpp, gfin):
    assert COMB_TILE == TOK_TILE
    rep = (jnp.arange(N_EXPERTS * WIN)[None, :] // WIN == jnp.arange(LANES)[:, None]).astype(BF16)
    rows = TAIL_TILES * COMB_TILE
    last = N_PROMPT // rows - 1

    def win_spec(e):
        return pl.BlockSpec((pl.Element(WIN), pl.Element(D_MODEL)),
                            lambda i, a, nw, ov, e=e: (pl.multiple_of(a[i * N_EXPERTS + e], WIN_ALIGN), 0))

    grid_spec = pltpu.PrefetchScalarGridSpec(
        num_scalar_prefetch=3,
        grid=(N_CTILES,),
        in_specs=[win_spec(e) for e in range(N_EXPERTS)] + [
            pl.BlockSpec((COMB_TILE, LANES), lambda i, *_: (i, 0)),
            pl.BlockSpec((COMB_TILE, LANES), lambda i, *_: (i, 0)),
            pl.BlockSpec((COMB_TILE, LANES), lambda i, *_: (i, 0)),
            pl.BlockSpec((LANES, N_EXPERTS * WIN), lambda i, *_: (0, 0)),
            pl.BlockSpec((rows, D_MODEL), lambda i, *_: (i // TAIL_TILES, 0)),
            pl.BlockSpec((rows, PLE_DIM), lambda i, *_: (jnp.minimum(i // TAIL_TILES, last), 0)),
            pl.BlockSpec((DEC_BATCH, PLE_DIM), lambda i, *_: (0, 0)),
            pl.BlockSpec((1, D_MODEL), lambda i, *_: (0, 0)),
            pl.BlockSpec((D_MODEL, D_MODEL), lambda i, *_: (0, 0)),
            pl.BlockSpec((PLE_DIM, D_MODEL), lambda i, *_: (0, 0)),
            pl.BlockSpec((1, D_MODEL), lambda i, *_: (0, 0)),
            pl.BlockSpec(memory_space=pl.ANY),
        ],
        out_specs=[
            pl.BlockSpec((rows, D_MODEL), lambda i, *_: (jnp.minimum(i // TAIL_TILES, last), 0)),
            pl.BlockSpec((DEC_BATCH, D_MODEL), lambda i, *_: (0, 0)),
        ],
        scratch_shapes=[pltpu.VMEM((rows, D_MODEL), F32), pltpu.VMEM((2 * WIN, D_MODEL), BF16),
                        pltpu.SemaphoreType.DMA],
    )
    return pl.pallas_call(
        _combine_kernel,
        grid_spec=grid_spec,
        out_shape=[jax.ShapeDtypeStruct((N_PROMPT, D_MODEL), F32),
                   jax.ShapeDtypeStruct((DEC_BATCH, D_MODEL), F32)],
        compiler_params=pltpu.CompilerParams(
            dimension_semantics=("arbitrary",), vmem_limit_bytes=VMEM_LIMIT),
        name="moe_combine_tail",
    )(astart, nwin, over, *([ys] * N_EXPERTS), destm, posm, gm, rep, h1, plep, ples, gple, wpg, wpp, gfin, ys)


def _gather_kernel(*refs):
    astart_ref, nwin_ref, over_ref = refs[0:3]
    win_refs = refs[3:3 + N_EXPERTS]
    (destm_ref, gm_ref, h1_ref, plep_ref, ples_ref, gple_ref, wpg_ref, wpp_ref, gfin_ref, ys_any,
     yp_ref, ysm_ref, moe_s, tmp_s, sem) = refs[3 + N_EXPERTS:]
    i = pl.program_id(0)
    slot = i % TAIL_TILES
    dest = destm_ref[...]
    gates = gm_ref[...]
    lane = _lane_iota((TOK_TILE, LANES))
    lane_f = lane.astype(F32)
    per_slab = LANES // CWIN
    within = (lane % CWIN).astype(F32)
    group = 2 * per_slab

    def split(gsel):
        hi = gsel.astype(BF16)
        return hi, (gsel - hi.astype(F32)).astype(BF16)

    moe = jnp.zeros((TOK_TILE, D_MODEL), F32)
    for g0 in range(0, N_EXPERTS, group):
        his, los = [], []
        for sl in range(2):
            rowid = jnp.zeros((TOK_TILE, LANES), F32)
            dcol = jnp.zeros((TOK_TILE, LANES), F32)
            gcol = jnp.zeros((TOK_TILE, LANES), F32)
            for q in range(per_slab):
                e = g0 + sl * per_slab + q
                mine = (lane >= q * CWIN) & (lane < (q + 1) * CWIN)
                rowid = jnp.where(mine, astart_ref[i * N_EXPERTS + e].astype(F32) + within, rowid)
                dcol = jnp.where(mine, dest[:, e:e + 1], dcol)
                gcol = jnp.where(mine, gates[:, e:e + 1], gcol)
            hi, lo = split(jnp.where(dcol == rowid, gcol, 0.0))
            his.append(hi)
            los.append(lo)
        ywin = jnp.concatenate([_unpack_rows(win_refs[g0 + q]) for q in range(group)], axis=0)
        both = jnp.concatenate([jnp.concatenate(his, axis=-1), jnp.concatenate(los, axis=-1)], axis=0)
        r = jnp.dot(both, ywin, preferred_element_type=F32)
        moe = moe + r[:TOK_TILE] + r[TOK_TILE:]
    mrows = pl.ds(pl.multiple_of(slot * TOK_TILE, TOK_TILE), TOK_TILE)
    moe_s[mrows, :] = moe

    @pl.when(over_ref[i] > 0)
    def _():
        def per_expert(e, carry):
            a = astart_ref[i * N_EXPERTS + e]
            dcol = jnp.sum(jnp.where(lane == e, dest, 0.0), axis=-1, keepdims=True)
            gcol = jnp.sum(jnp.where(lane == e, gates, 0.0), axis=-1, keepdims=True)

            def per_window(w, carry2):
                first = a + w * CWIN
                start = jnp.minimum(first, N_ROWS - CWIN)
                cp = pltpu.make_async_copy(ys_any.at[pl.ds(start, CWIN)], tmp_s, sem)
                cp.start()
                cp.wait()
                hit = (lane < CWIN) & (dcol == start.astype(F32) + lane_f) & (dcol >= first.astype(F32))
                hi, lo = split(jnp.where(hit, gcol, 0.0))
                rows = jnp.concatenate([_unpack_rows(tmp_s), jnp.zeros((LANES - CWIN, D_MODEL), BF16)], axis=0)
                moe_s[mrows, :] += (jnp.dot(hi, rows, preferred_element_type=F32)
                                    + jnp.dot(lo, rows, preferred_element_type=F32))
                return carry2

            return lax.fori_loop(1, nwin_ref[i * N_EXPERTS + e], per_window, carry)

        lax.fori_loop(0, N_EXPERTS, per_expert, 0)

    def tail(h1, moe_rows, ple):
        h2 = h1 + moe_rows
        hn = _rms(h2, gple_ref[...]).astype(BF16)
        gate = jax.nn.sigmoid(jnp.dot(hn, wpg_ref[...], preferred_element_type=F32))
        proj = jnp.dot(ple.astype(BF16), wpp_ref[...], preferred_element_type=F32)
        return _rms(h2 + gate * proj, gfin_ref[...])

    is_sample = i == N_TILES - 1

    @pl.when((slot == TAIL_TILES - 1) & jnp.logical_not(is_sample))
    def _():
        yp_ref[...] = tail(h1_ref[...], moe_s[...], plep_ref[...])

    @pl.when(is_sample)
    def _():
        ysm_ref[...] = tail(h1_ref[0:TOK_TILE, :], moe_s[0:TOK_TILE, :], ples_ref[...])


def _gather_call(astart, nwin, over, ys, destm, gm, h1, plep, ples, gple, wpg, wpp, gfin):
    rows = TAIL_TILES * TOK_TILE
    last = N_PROMPT // rows - 1

    def win_spec(e):
        return pl.BlockSpec((pl.Element(CWIN), pl.Element(PACK), pl.Element(LANES)),
                            lambda i, a, nw, ov, e=e: (a[i * N_EXPERTS + e], 0, 0))

    grid_spec = pltpu.PrefetchScalarGridSpec(
        num_scalar_prefetch=3,
        grid=(N_CTILES,),
        in_specs=[win_spec(e) for e in range(N_EXPERTS)] + [
            pl.BlockSpec((COMB_TILE, LANES), lambda i, *_: (i, 0)),
            pl.BlockSpec((COMB_TILE, LANES), lambda i, *_: (i, 0)),
            pl.BlockSpec((COMB_TILE, LANES), lambda i, *_: (i, 0)),
            pl.BlockSpec((LANES, N_EXPERTS * WIN), lambda i, *_: (0, 0)),
            pl.BlockSpec((rows, D_MODEL), lambda i, *_: (i // TAIL_TILES, 0)),
            pl.BlockSpec((rows, PLE_DIM), lambda i, *_: (jnp.minimum(i // TAIL_TILES, last), 0)),
            pl.BlockSpec((DEC_BATCH, PLE_DIM), lambda i, *_: (0, 0)),
            pl.BlockSpec((1, D_MODEL), lambda i, *_: (0, 0)),
            pl.BlockSpec((D_MODEL, D_MODEL), lambda i, *_: (0, 0)),
            pl.BlockSpec((PLE_DIM, D_MODEL), lambda i, *_: (0, 0)),
            pl.BlockSpec((1, D_MODEL), lambda i, *_: (0, 0)),
            pl.BlockSpec(memory_space=pl.ANY),
        ],
        out_specs=[
            pl.BlockSpec((rows, D_MODEL), lambda i, *_: (jnp.minimum(i // TAIL_TILES, last), 0)),
            pl.BlockSpec((TOK_TILE, D_MODEL), lambda i, *_: (0, 0)),
        ],
        scratch_shapes=[pltpu.VMEM((rows, D_MODEL), F32), pltpu.VMEM((CWIN, PACK, LANES), jnp.int32),
                        pltpu.SemaphoreType.DMA],
    )
    return pl.pallas_call(
        _gather_kernel,
        grid_spec=grid_spec,
        out_shape=[jax.ShapeDtypeStruct((N_PROMPT, D_MODEL), F32),
                   jax.ShapeDtypeStruct((DEC_BATCH, D_MODEL), F32)],
        compiler_params=pltpu.CompilerParams(
            dimension_semantics=("arbitrary",), vmem_limit_bytes=VMEM_LIMIT),
        name="moe_combine_tail",
    )(astart, nwin, over, *([ys] * N_EXPERTS), destm, posm, gm, rep, h1, plep, ples, gple, wpg, wpp, gfin, ys)


def kernel(x_prompt, x_sample, cache_swa_k, cache_swa_v, p_prompt, p_sample, g_mix, w_in, ln_v_g, ln_v_b,
           w_sp, b_sp, sinks, g_out_a, g_out_b, w_o, g_moe, w_router, b_router, w_gu, b_gu, w_dn, b_dn,
           g_ple, w_ple_gate, w_ple_proj, g_final):
    l = 0
    row = lambda v: v.reshape(1, -1)
    win = w_in[l].astype(BF16)
    wo = w_o[l].astype(BF16)
    tril = jnp.tril(jnp.ones((CHUNK, CHUNK), dtype=bool))
    wsp = jnp.where(tril, w_sp[l], 0.0).astype(BF16)
    bsp = jnp.repeat(b_sp[l].T, HEAD_DIM, axis=1)
    w00 = row(jnp.repeat(w_sp[l][:, 0, 0], HEAD_DIM))
    b0 = row(jnp.repeat(b_sp[l][:, 0], HEAD_DIM))
    wr_hi = w_router[l].astype(BF16)
    wr_lo = (w_router[l] - wr_hi.astype(F32)).astype(BF16)
    wr = jnp.concatenate([wr_hi, wr_lo, jnp.zeros((D_MODEL, LANES - 2 * N_EXPERTS), BF16)], axis=1)
    br = row(jnp.concatenate([b_router[l], jnp.zeros((LANES - N_EXPERTS,), F32)]))
    common = (row(g_mix[l]), win, row(ln_v_g[l]), row(ln_v_b[l]))
    tail = (row(g_out_a[l]), row(g_out_b[l]), wo, row(g_moe[l]), wr, br)

    h1, xn, gm, sm, k_p, v_p = _prompt_call(
        x_prompt.reshape(N_PROMPT, D_MODEL), sinks[l], *common, wsp, bsp, *tail)
    ck = cache_swa_k[l].reshape(DEC_BATCH * CHUNK * 2, HEAD_DIM)
    cv = cache_swa_v[l].reshape(DEC_BATCH * CHUNK * 2, HEAD_DIM)
    h1, xn, gm, sm, k_s, v_s, va_s = _decode_call(
        x_sample.reshape(DEC_BATCH, D_MODEL), ck, cv, sinks[l], *common, w00, b0, *tail, h1, xn, gm, sm)

    destm, posm, destt, xbe, live, nxblk, stab, ctab, astart, nwin = _plan_call(sm)
    flat = lambda tab, n: tab[:n, :N_EXPERTS].reshape(-1)
    del astart, nwin
    per = COMB_TILE // TOK_TILE
    first = stab[0:per * N_CTILES:per, :N_EXPERTS]
    count = sum(ctab[q:per * N_CTILES:per, :N_EXPERTS] for q in range(per))
    astart2 = jnp.minimum(first // WIN_ALIGN * WIN_ALIGN, N_ROWS - WIN)
    nwin2 = jnp.where(count > 0, (first + count - astart2 + WIN - 1) // WIN, 0)
    astart1 = astart2.reshape(-1)
    over1 = (jnp.max(nwin2, axis=1) > 1).astype(jnp.int32)
    nwin1 = nwin2.reshape(-1)
    cmax1 = jnp.max(ctab[:N_PAD_TILES, :N_EXPERTS], axis=1)

    xs = _dispatch_call(flat(stab, N_PAD_TILES), flat(ctab, N_PAD_TILES), cmax1, xn, destt[:N_EXPERTS],
                        jnp.zeros((XS_ROWS, PACK, LANES), jnp.int32))
    ys = _expert_call(xbe[:N_XBLOCKS, 0], nxblk[0, :1], live[:N_XBLOCKS, 0], xs, w_gu[l], b_gu[l].reshape(N_EXPERTS, 1, 2 * D_FF),
                      w_dn[l], b_dn[l].reshape(N_EXPERTS, 1, D_MODEL))
    y_p, y_s = _combine_call(
        astart1, nwin1, over1, ys, destm, posm, gm, h1,
        p_prompt[l].reshape(N_PROMPT, PLE_DIM), p_sample[l].reshape(DEC_BATCH, PLE_DIM),
        row(g_ple[l]), w_ple_gate[l].astype(BF16), w_ple_proj[l].astype(BF16), row(g_final))

    kv5 = lambda a, n: a.reshape(1, n, CHUNK, 2, HEAD_DIM)
    return (y_p.reshape(BATCH, SEQ, D_MODEL), y_s.reshape(DEC_BATCH, 1, D_MODEL),
            kv5(k_p, BATCH), kv5(v_p, BATCH), kv5(k_s, DEC_BATCH), kv5(v_s, DEC_BATCH),
            va_s.reshape(1, DEC_BATCH, 1, A_WIDTH))
```

```python
import math

import jax
import jax.numpy as jnp
from jax import lax
from jax.experimental import pallas as pl
from jax.experimental.pallas import tpu as pltpu

F32 = jnp.float32
BF16 = jnp.bfloat16

D_MODEL = 1024
BATCH = 4
SEQ = 4096
DEC_BATCH = 128
HEAD_DIM = 64
A_WIDTH = 512
B_WIDTH = 512
B_HEADS = 8
KV_WIDTH = 128
IN_WIDTH = 2 * A_WIDTH + B_WIDTH + 2 * KV_WIDTH
CHUNK = 128
N_EXPERTS = 32
TOP_K = 4
D_FF = 1024
SWIGLU_ALPHA = 1.702
SWIGLU_LIMIT = 7.0
PLE_DIM = 256
EPS = 1e-5

LANES = 128
VMEM_LIMIT = 56 * 1024 * 1024
N_PROMPT = BATCH * SEQ
N_TOK = N_PROMPT + DEC_BATCH
TM = 1024
N_PAD = ((N_TOK + TM - 1) // TM) * TM
TOK_TILE = 128
N_PAD_TILES = N_PAD // TOK_TILE
N_CTILES = N_PROMPT // TOK_TILE + 1
DEC_TILE = 16
EXP_BLOCK = 512
ROW_GROUP = 128
DISP_CHUNK = 32
N_XBLOCKS = (N_TOK * TOP_K + N_EXPERTS * (DISP_CHUNK + EXP_BLOCK - 1) + EXP_BLOCK - 1) // EXP_BLOCK
N_ROWS = N_XBLOCKS * EXP_BLOCK
XS_ROWS = N_ROWS + N_EXPERTS * DISP_CHUNK
XBE_ROWS = ((N_XBLOCKS + 7) // 8) * 8
TAB_ROWS = ((N_PAD_TILES + 7) // 8) * 8
PACK = D_MODEL // 2 // LANES
WIN = 64
WIN_ALIGN = 16
TAIL_TILES = 4
NEG = -1e30


def _rms(x, g):
    return x * lax.rsqrt(jnp.mean(x * x, axis=-1, keepdims=True) + EPS) * g


def _gelu(x):
    c = math.sqrt(2.0 / math.pi)
    return x * (0.5 * (1.0 + jnp.tanh(c * (x + 0.044715 * (x * x * x)))))


def _layernorm(x, g, b):
    mu = jnp.mean(x, axis=-1, keepdims=True)
    xc = x - mu
    return xc * lax.rsqrt(jnp.mean(xc * xc, axis=-1, keepdims=True) + EPS) * g + b


def _lane_iota(shape):
    return lax.broadcasted_iota(jnp.int32, shape, len(shape) - 1)


def _full(shape):
    n = len(shape)
    return pl.BlockSpec(shape, lambda *_: (0,) * n)


def _route(xn2, wr_ref, br_ref):
    m = xn2.shape[0]
    xh = xn2.astype(BF16)
    xl = (xn2 - xh.astype(F32)).astype(BF16)
    r = jnp.dot(jnp.concatenate([xh, xl], axis=0), wr_ref[...], preferred_element_type=F32)
    r = r[:m] + r[m:]
    lane = _lane_iota((m, LANES))
    lane_f = lane.astype(F32)
    logits = jnp.where(lane < N_EXPERTS, r + pltpu.roll(r, LANES - N_EXPERTS, 1) + br_ref[...], NEG)
    work = logits
    sel = jnp.zeros((m, LANES), F32)
    top = None
    z = None
    for _ in range(TOP_K):
        mx = jnp.max(work, axis=-1, keepdims=True)
        first = jnp.min(jnp.where(work == mx, lane_f, float(LANES)), axis=-1, keepdims=True)
        hit = lane_f == first
        sel = jnp.where(hit, 1.0, sel)
        work = jnp.where(hit, NEG, work)
        if top is None:
            top = mx
            z = jnp.ones_like(mx)
        else:
            z = z + jnp.exp(mx - top)
    gates = jnp.where(sel > 0.0, jnp.exp(logits - top) / z, 0.0)
    return gates, sel


def _prompt_kernel(sinks_ref, x_ref, gmix_ref, win_ref, lng_ref, lnb_ref, wsp_ref, bsp_ref,
                   goa_ref, gob_ref, wo_ref, gmoe_ref, wr_ref, br_ref,
                   h1_ref, xn_ref, gm_ref, sm_ref, k_ref, v_ref,
                   z_s, kv_s, cat_s):
    g = pl.program_id(0)

    @pl.when(g >= N_PROMPT // TM)
    def _():
        h1_ref[...] = jnp.zeros_like(h1_ref)
        xn_ref[...] = jnp.zeros_like(xn_ref)
        gm_ref[...] = jnp.zeros_like(gm_ref)
        sm_ref[...] = jnp.zeros_like(sm_ref)

    @pl.when(g < N_PROMPT // TM)
    def _():
        _prompt_tile(g % (SEQ // TM), sinks_ref, x_ref, gmix_ref, win_ref, lng_ref, lnb_ref, wsp_ref, bsp_ref,
                     goa_ref, gob_ref, wo_ref, gmoe_ref, wr_ref, br_ref,
                     h1_ref, xn_ref, gm_ref, sm_ref, k_ref, v_ref, z_s, kv_s, cat_s)


def _prompt_tile(j, sinks_ref, x_ref, gmix_ref, win_ref, lng_ref, lnb_ref, wsp_ref, bsp_ref,
                 goa_ref, gob_ref, wo_ref, gmoe_ref, wr_ref, br_ref,
                 h1_ref, xn_ref, gm_ref, sm_ref, k_ref, v_ref, z_s, kv_s, cat_s):
    @pl.when(j == 0)
    def _():
        kv_s[0:CHUNK, :] = jnp.zeros((CHUNK, 2 * KV_WIDTH), F32)

    xn = _rms(x_ref[...], gmix_ref[...]).astype(BF16)
    z_s[...] = jnp.dot(xn, win_ref[...], preferred_element_type=F32)
    kv_s[CHUNK:, :] = z_s[:, 2 * A_WIDTH + B_WIDTH:]

    lane = _lane_iota((CHUNK, LANES))
    lo = lane < HEAD_DIM
    lane2 = _lane_iota((2 * CHUNK, LANES))
    lo2 = lane2 < HEAD_DIM
    qi = lax.broadcasted_iota(jnp.int32, (CHUNK, CHUNK), 0)
    kc = lax.broadcasted_iota(jnp.int32, (CHUNK, CHUNK), 1)
    from_prev = kc > qi
    dist = jnp.where(from_prev, qi + CHUNK - kc, qi - kc).astype(F32)

    def chunk_body(c, carry):
        r0 = pl.multiple_of(c * CHUNK, CHUNK)
        rows = pl.ds(r0, CHUNK)
        u = _gelu(z_s[rows, 0:A_WIDTH])
        va = _layernorm(_gelu(z_s[rows, A_WIDTH:2 * A_WIDTH]), lng_ref[...], lnb_ref[...])
        vab = va.astype(BF16)
        slabs = []
        for p in range(A_WIDTH // LANES):
            slab = vab[:, p * LANES:(p + 1) * LANES]
            m0 = jnp.dot(wsp_ref[2 * p], slab, preferred_element_type=F32)
            m1 = jnp.dot(wsp_ref[2 * p + 1], slab, preferred_element_type=F32)
            slabs.append(jnp.where(lo, m0, m1))
        ya = u * (jnp.concatenate(slabs, axis=-1) + bsp_ref[...])
        ya_n = _rms(ya, goa_ref[...])
        k2 = kv_s[pl.ds(r0, 2 * CHUNK), 0:KV_WIDTH]
        v2 = kv_s[pl.ds(r0, 2 * CHUNK), KV_WIDTH:2 * KV_WIDTH]
        k2r = pltpu.roll(k2, HEAD_DIM, 1)
        v2r = pltpu.roll(v2, HEAD_DIM, 1)
        kd = (jnp.where(lo2, k2, k2r).astype(BF16), jnp.where(lo2, k2r, k2).astype(BF16))
        vd = (jnp.where(lo2, v2, v2r).astype(BF16), jnp.where(lo2, v2r, v2).astype(BF16))
        prev_ok = (j > 0) | (c > 0)
        masked = from_prev & jnp.logical_not(prev_ok)
        yb_slabs = []
        for kv in range(2):
            q0 = z_s[rows, 2 * A_WIDTH + (2 * kv) * LANES:2 * A_WIDTH + (2 * kv + 1) * LANES]
            q1 = z_s[rows, 2 * A_WIDTH + (2 * kv + 1) * LANES:2 * A_WIDTH + (2 * kv + 2) * LANES]
            lhs = jnp.concatenate([jnp.where(lo, q0, 0.0), jnp.where(lo, 0.0, q0),
                                   jnp.where(lo, q1, 0.0), jnp.where(lo, 0.0, q1)], axis=0).astype(BF16)
            s_all = lax.dot_general(lhs, kd[kv], (((1,), (1,)), ((), ())), preferred_element_type=F32)
            probs = []
            for i in range(4):
                h = 4 * kv + i
                slope = 2.0 ** (-(h + 1))
                sink = sinks_ref[h]
                sh = s_all[i * CHUNK:(i + 1) * CHUNK]
                s = jnp.where(from_prev, sh[:, :CHUNK], sh[:, CHUNK:]) * (HEAD_DIM ** -0.5) - slope * dist
                s = jnp.where(masked, NEG, s)
                mx = jnp.maximum(jnp.max(s, axis=-1, keepdims=True), sink)
                e = jnp.exp(s - mx)
                den = jnp.sum(e, axis=-1, keepdims=True) + jnp.exp(sink - mx)
                p = e * (1.0 / den)
                probs.append(jnp.concatenate([jnp.where(from_prev, p, 0.0), jnp.where(from_prev, 0.0, p)], axis=-1))
            pm = jnp.concatenate(probs, axis=0).astype(BF16)
            o = jnp.dot(pm, vd[kv], preferred_element_type=F32)
            yb_slabs.append(jnp.where(lo, o[0:CHUNK], o[CHUNK:2 * CHUNK]))
            yb_slabs.append(jnp.where(lo, o[2 * CHUNK:3 * CHUNK], o[3 * CHUNK:4 * CHUNK]))
        yb_n = _rms(jnp.concatenate(yb_slabs, axis=-1), gob_ref[...])
        cat_s[rows, 0:A_WIDTH] = ya_n.astype(BF16)
        cat_s[rows, A_WIDTH:] = yb_n.astype(BF16)
        return carry

    lax.fori_loop(0, TM // CHUNK, chunk_body, 0)

    kv_s[0:CHUNK, :] = kv_s[TM:TM + CHUNK, :]
    k_ref[...] = kv_s[TM:TM + CHUNK, 0:KV_WIDTH]
    v_ref[...] = kv_s[TM:TM + CHUNK, KV_WIDTH:]

    h1 = x_ref[...] + jnp.dot(cat_s[...], wo_ref[...], preferred_element_type=F32)
    h1_ref[...] = h1
    xn2 = _rms(h1, gmoe_ref[...])
    xn_ref[...] = xn2.astype(BF16)
    gates, sel = _route(xn2, wr_ref, br_ref)
    gm_ref[...] = gates
    sm_ref[...] = sel


def _prompt_call(x, sinks, gmix, win, lng, lnb, wsp, bsp, goa, gob, wo, gmoe, wr, br):
    real = N_PROMPT // TM
    row = lambda g: (g, 0)
    seq = lambda g: (jnp.minimum(g, real - 1) // (SEQ // TM), 0, 0)
    return pl.pallas_call(
        _prompt_kernel,
        grid=(N_PAD // TM,),
        in_specs=[
            pl.BlockSpec(memory_space=pltpu.SMEM),
            pl.BlockSpec((TM, D_MODEL), lambda g: (jnp.minimum(g, real - 1), 0)),
            _full((1, D_MODEL)), _full((D_MODEL, IN_WIDTH)), _full((1, A_WIDTH)), _full((1, A_WIDTH)),
            _full((8, CHUNK, CHUNK)), _full((CHUNK, A_WIDTH)), _full((1, A_WIDTH)), _full((1, B_WIDTH)),
            _full((D_MODEL, D_MODEL)), _full((1, D_MODEL)), _full((D_MODEL, LANES)), _full((1, LANES)),
        ],
        out_specs=[
            pl.BlockSpec((TM, D_MODEL), row),
            pl.BlockSpec((TM, D_MODEL), row),
            pl.BlockSpec((TM, LANES), row),
            pl.BlockSpec((TM, LANES), row),
            pl.BlockSpec((None, CHUNK, KV_WIDTH), seq),
            pl.BlockSpec((None, CHUNK, KV_WIDTH), seq),
        ],
        out_shape=[
            jax.ShapeDtypeStruct((N_PAD, D_MODEL), F32),
            jax.ShapeDtypeStruct((N_PAD, D_MODEL), BF16),
            jax.ShapeDtypeStruct((N_PAD, LANES), F32),
            jax.ShapeDtypeStruct((N_PAD, LANES), F32),
            jax.ShapeDtypeStruct((BATCH, CHUNK, KV_WIDTH), F32),
            jax.ShapeDtypeStruct((BATCH, CHUNK, KV_WIDTH), F32),
        ],
        scratch_shapes=[
            pltpu.VMEM((TM, IN_WIDTH), F32),
            pltpu.VMEM((TM + CHUNK, 2 * KV_WIDTH), F32),
            pltpu.VMEM((TM, D_MODEL), BF16),
        ],
        compiler_params=pltpu.CompilerParams(
            dimension_semantics=("arbitrary",), vmem_limit_bytes=VMEM_LIMIT),
        name="prompt_premoe",
    )(sinks, x, gmix, win, lng, lnb, wsp, bsp, goa, gob, wo, gmoe, wr, br)


def _decode_kernel(sinks_ref, x_ref, ck_ref, cv_ref, gmix_ref, win_ref, lng_ref, lnb_ref, w00_ref, b0_ref,
                   goa_ref, gob_ref, wo_ref, gmoe_ref, wr_ref, br_ref,
                   h1_in, xn_in, gm_in, sm_in,
                   h1_ref, xn_ref, gm_ref, sm_ref, nk_ref, nv_ref, va_ref,
                   q_s, kn_s, vn_s, ya_s, yb_s):
    del h1_in, xn_in, gm_in, sm_in
    i = pl.program_id(0)
    t = DEC_TILE
    per_seq = CHUNK * 2
    ncols = t * per_seq

    @pl.when(i == 0)
    def _():
        xn = _rms(x_ref[...], gmix_ref[...]).astype(BF16)
        z = jnp.dot(xn, win_ref[...], preferred_element_type=F32)
        u = _gelu(z[:, 0:A_WIDTH])
        va = _layernorm(_gelu(z[:, A_WIDTH:2 * A_WIDTH]), lng_ref[...], lnb_ref[...])
        va_ref[...] = va
        ya_s[...] = _rms(u * (w00_ref[...] * va + b0_ref[...]), goa_ref[...])
        q_s[...] = z[:, 2 * A_WIDTH:2 * A_WIDTH + B_WIDTH]
        kn_s[...] = z[:, 2 * A_WIDTH + B_WIDTH:2 * A_WIDTH + B_WIDTH + KV_WIDTH]
        vn_s[...] = z[:, 2 * A_WIDTH + B_WIDTH + KV_WIDTH:]

    rows = pl.ds(pl.multiple_of(i * t, t), t)
    q = q_s[rows, :]
    kn = kn_s[rows, :]
    vn = vn_s[rows, :]
    qs = jnp.concatenate([q[:, h * HEAD_DIM:(h + 1) * HEAD_DIM] for h in range(B_HEADS)], axis=0)
    kn8 = jnp.concatenate([kn[:, (h // 4) * HEAD_DIM:(h // 4 + 1) * HEAD_DIM] for h in range(B_HEADS)], axis=0)
    vn8 = jnp.concatenate([vn[:, (h // 4) * HEAD_DIM:(h // 4 + 1) * HEAD_DIM] for h in range(B_HEADS)], axis=0)
    nrows = B_HEADS * t
    ridx = lax.broadcasted_iota(jnp.int32, (nrows, 1), 0)
    slope = jnp.zeros((nrows, 1), F32)
    sink = jnp.zeros((nrows, 1), F32)
    for h in range(B_HEADS):
        in_h = (ridx >= h * t) & (ridx < (h + 1) * t)
        slope = jnp.where(in_h, 2.0 ** (-(h + 1)), slope)
        sink = jnp.where(in_h, sinks_ref[h], sink)
    s_c = lax.dot_general(qs.astype(BF16), ck_ref[...].astype(BF16), (((1,), (1,)), ((), ())),
                          preferred_element_type=F32)
    rr = lax.broadcasted_iota(jnp.int32, (nrows, ncols), 0)
    col = lax.broadcasted_iota(jnp.int32, (nrows, ncols), 1)
    pos = (col // 2) % CHUNK
    own = ((col // per_seq) == (rr % t)) & ((col % 2) == (rr // (4 * t))) & (pos >= 1)
    s_c = s_c * (HEAD_DIM ** -0.5) - slope * (CHUNK - pos).astype(F32)
    s_c = jnp.where(own, s_c, NEG)
    s_n = jnp.sum(qs * kn8, axis=-1, keepdims=True) * (HEAD_DIM ** -0.5)
    mx = jnp.maximum(jnp.maximum(jnp.max(s_c, axis=-1, keepdims=True), s_n), sink)
    e_c = jnp.exp(s_c - mx)
    e_n = jnp.exp(s_n - mx)
    inv = 1.0 / (jnp.sum(e_c, axis=-1, keepdims=True) + e_n + jnp.exp(sink - mx))
    o = jnp.dot((e_c * inv).astype(BF16), cv_ref[...].astype(BF16), preferred_element_type=F32)
    o = o + (e_n * inv) * vn8
    yb_s[rows, :] = jnp.concatenate([o[h * t:(h + 1) * t] for h in range(B_HEADS)], axis=-1)

    nk_ref[...] = pltpu.roll(ck_ref[...], ncols - 2, 0)
    nv_ref[...] = pltpu.roll(cv_ref[...], ncols - 2, 0)
    for b in range(t):
        for kv in range(2):
            r = b * per_seq + per_seq - 2 + kv
            nk_ref[r:r + 1, :] = kn[b:b + 1, kv * HEAD_DIM:(kv + 1) * HEAD_DIM]
            nv_ref[r:r + 1, :] = vn[b:b + 1, kv * HEAD_DIM:(kv + 1) * HEAD_DIM]

    @pl.when(i == pl.num_programs(0) - 1)
    def _():
        yb_n = _rms(yb_s[...], gob_ref[...])
        cat = jnp.concatenate([ya_s[...], yb_n], axis=-1).astype(BF16)
        h1 = x_ref[...] + jnp.dot(cat, wo_ref[...], preferred_element_type=F32)
        xn2 = _rms(h1, gmoe_ref[...])
        gates, sel = _route(xn2, wr_ref, br_ref)
        h1_ref[...] = h1
        xn_ref[...] = xn2.astype(BF16)
        gm_ref[...] = gates
        sm_ref[...] = sel


def _decode_call(x, ck, cv, sinks, gmix, win, lng, lnb, w00, b0, goa, gob, wo, gmoe, wr, br, h1, xn, gm, sm):
    t = DEC_TILE
    per_seq = CHUNK * 2
    cache = pl.BlockSpec((t * per_seq, HEAD_DIM), lambda i: (i, 0))
    tok = lambda width: pl.BlockSpec((DEC_BATCH, width), lambda i: (N_PROMPT // DEC_BATCH, 0))
    anyspec = pl.BlockSpec(memory_space=pl.ANY)
    return pl.pallas_call(
        _decode_kernel,
        grid=(DEC_BATCH // t,),
        in_specs=[
            pl.BlockSpec(memory_space=pltpu.SMEM),
            _full((DEC_BATCH, D_MODEL)), cache, cache,
            _full((1, D_MODEL)), _full((D_MODEL, IN_WIDTH)), _full((1, A_WIDTH)), _full((1, A_WIDTH)),
            _full((1, A_WIDTH)), _full((1, A_WIDTH)), _full((1, A_WIDTH)), _full((1, B_WIDTH)),
            _full((D_MODEL, D_MODEL)), _full((1, D_MODEL)), _full((D_MODEL, LANES)), _full((1, LANES)),
            anyspec, anyspec, anyspec, anyspec,
        ],
        out_specs=[tok(D_MODEL), tok(D_MODEL), tok(LANES), tok(LANES), cache, cache, _full((DEC_BATCH, A_WIDTH))],
        out_shape=[
            jax.ShapeDtypeStruct((N_PAD, D_MODEL), F32),
            jax.ShapeDtypeStruct((N_PAD, D_MODEL), BF16),
            jax.ShapeDtypeStruct((N_PAD, LANES), F32),
            jax.ShapeDtypeStruct((N_PAD, LANES), F32),
            jax.ShapeDtypeStruct((DEC_BATCH * per_seq, HEAD_DIM), F32),
            jax.ShapeDtypeStruct((DEC_BATCH * per_seq, HEAD_DIM), F32),
            jax.ShapeDtypeStruct((DEC_BATCH, A_WIDTH), F32),
        ],
        scratch_shapes=[pltpu.VMEM((DEC_BATCH, B_WIDTH), F32), pltpu.VMEM((DEC_BATCH, KV_WIDTH), F32),
                        pltpu.VMEM((DEC_BATCH, KV_WIDTH), F32), pltpu.VMEM((DEC_BATCH, A_WIDTH), F32),
                        pltpu.VMEM((DEC_BATCH, B_WIDTH), F32)],
        input_output_aliases={16: 0, 17: 1, 18: 2, 19: 3},
        compiler_params=pltpu.CompilerParams(
            dimension_semantics=("arbitrary",), vmem_limit_bytes=VMEM_LIMIT),
        name="sample_premoe",
    )(sinks, x, ck, cv, gmix, win, lng, lnb, w00, b0, goa, gob, wo, gmoe, wr, br, h1, xn, gm, sm)


def _plan_kernel(sm_ref,
                 destm_ref, posm_ref, destt_ref, xbe_ref, live_ref, nxblk_ref, stab_ref, ctab_ref, astart_ref,
                 nwin_ref, base_s, pstart_s):
    ph = pl.program_id(0)
    step = pl.program_id(1)
    lane = _lane_iota((1, LANES))

    @pl.when((ph == 0) & (step == 0))
    def _():
        base_s[...] = jnp.zeros_like(base_s)

    @pl.when(ph == 0)
    def _():
        base_s[...] += jnp.sum(sm_ref[...], axis=0, keepdims=True)

    @pl.when((ph == 1) & (step == 0))
    def _():
        counts = base_s[...]
        padded = jnp.floor((counts + (DISP_CHUNK + EXP_BLOCK - 1)) * (1.0 / EXP_BLOCK)) * EXP_BLOCK
        padded = jnp.where(counts > 0.0, padded, 0.0)
        pend = padded
        for s in (1, 2, 4, 8, 16):
            pend = pend + jnp.where(lane >= s, pltpu.roll(pend, s, 1), 0.0)
        spare = (N_ROWS + lane * DISP_CHUNK).astype(F32)
        pstart_s[...] = jnp.where(counts > 0.0, pend - padded, spare)
        base_s[...] = jnp.zeros_like(base_s)
        brow = lax.broadcasted_iota(jnp.int32, (XBE_ROWS, LANES), 0).astype(F32) * EXP_BLOCK
        done = jnp.where((lane < N_EXPERTS) & (pend <= brow), 1.0, 0.0)
        be = jnp.minimum(jnp.sum(done, axis=-1, keepdims=True), N_EXPERTS - 1.0)
        xbe_ref[...] = jnp.broadcast_to(be, (XBE_ROWS, LANES)).astype(jnp.int32)
        real = jnp.clip(counts - (brow - (pend - padded)), 0.0, float(EXP_BLOCK))
        real = jnp.sum(jnp.where(lane.astype(F32) == be, real, 0.0), axis=-1, keepdims=True)
        groups = jnp.floor((real + (ROW_GROUP - 1)) * (1.0 / ROW_GROUP))
        live_ref[...] = jnp.broadcast_to(groups, (XBE_ROWS, LANES)).astype(jnp.int32)
        total = jnp.sum(jnp.where(lane == N_EXPERTS - 1, pend, 0.0), axis=-1, keepdims=True)
        nxblk_ref[...] = jnp.broadcast_to(total * (1.0 / EXP_BLOCK), (8, LANES)).astype(jnp.int32)
        stab_ref[...] = jnp.zeros_like(stab_ref)
        ctab_ref[...] = jnp.zeros_like(ctab_ref)
        astart_ref[...] = jnp.zeros_like(astart_ref)
        nwin_ref[...] = jnp.zeros_like(nwin_ref)

    @pl.when(ph == 1)
    def _():
        r = lax.broadcasted_iota(jnp.int32, (TOK_TILE, TOK_TILE), 0)
        c = lax.broadcasted_iota(jnp.int32, (TOK_TILE, TOK_TILE), 1)
        lower = jnp.where(c < r, 1.0, 0.0).astype(BF16)
        for q in range(TM // TOK_TILE):
            i = step * (TM // TOK_TILE) + q
            sel = sm_ref[q * TOK_TILE:(q + 1) * TOK_TILE, :]
            cnt = jnp.sum(sel, axis=0, keepdims=True)
            prefix = jnp.dot(lower, sel.astype(BF16), preferred_element_type=F32)
            start = pstart_s[...] + base_s[...]
            dest = jnp.where(sel > 0.0, prefix + start, -1.0)
            destm_ref[q * TOK_TILE:(q + 1) * TOK_TILE, :] = dest
            destt_ref[:, q * TOK_TILE:(q + 1) * TOK_TILE] = dest.T
            has = (cnt > 0.0) & (lane < N_EXPERTS)
            stab_ref[pl.ds(i, 1), :] = start.astype(jnp.int32)
            ctab_ref[pl.ds(i, 1), :] = jnp.where(has, cnt, 0.0).astype(jnp.int32)
            a = jnp.minimum(jnp.floor(start * (1.0 / WIN_ALIGN)) * WIN_ALIGN, float(N_ROWS - WIN))
            nw = jnp.where(has, jnp.floor((start + cnt - a + (WIN - 1)) * (1.0 / WIN)), 0.0)
            posm_ref[q * TOK_TILE:(q + 1) * TOK_TILE, :] = jnp.where(sel > 0.0, prefix + start - a, -1.0)
            astart_ref[pl.ds(i, 1), :] = a.astype(jnp.int32)
            nwin_ref[pl.ds(i, 1), :] = nw.astype(jnp.int32)
            base_s[...] += cnt


def _plan_call(sm):
    tile = lambda ph, i: (i * ph, 0)
    tile_t = lambda ph, i: (0, i * ph)
    tab = jax.ShapeDtypeStruct((TAB_ROWS, LANES), jnp.int32)
    return pl.pallas_call(
        _plan_kernel,
        grid=(2, N_PAD // TM),
        in_specs=[pl.BlockSpec((TM, LANES), lambda ph, i: (i, 0))],
        out_specs=[
            pl.BlockSpec((TM, LANES), tile),
            pl.BlockSpec((TM, LANES), tile),
            pl.BlockSpec((LANES, TM), tile_t),
            _full((XBE_ROWS, LANES)), _full((XBE_ROWS, LANES)), _full((8, LANES)),
            _full((TAB_ROWS, LANES)), _full((TAB_ROWS, LANES)), _full((TAB_ROWS, LANES)), _full((TAB_ROWS, LANES)),
        ],
        out_shape=[
            jax.ShapeDtypeStruct((N_PAD, LANES), F32),
            jax.ShapeDtypeStruct((N_PAD, LANES), F32),
            jax.ShapeDtypeStruct((LANES, N_PAD), F32),
            jax.ShapeDtypeStruct((XBE_ROWS, LANES), jnp.int32),
            jax.ShapeDtypeStruct((XBE_ROWS, LANES), jnp.int32),
            jax.ShapeDtypeStruct((8, LANES), jnp.int32),
            tab, tab, tab, tab,
        ],
        scratch_shapes=[pltpu.VMEM((1, LANES), F32), pltpu.VMEM((1, LANES), F32)],
        compiler_params=pltpu.CompilerParams(
            dimension_semantics=("arbitrary", "arbitrary"), vmem_limit_bytes=VMEM_LIMIT),
        name="moe_plan",
    )(sm)


def _pack_rows(z):
    half = D_MODEL // 2
    lo = lax.bitcast_convert_type(z[:, :half], jnp.uint32) >> 16
    hi = lax.bitcast_convert_type(z[:, half:], jnp.uint32) & jnp.uint32(0xFFFF0000)
    return lax.bitcast_convert_type(hi | lo, jnp.int32)


def _unpack_rows(ref, rows=None):
    rows = ref.shape[0] if rows is None else rows
    flat = ref.reshape(ref.shape[0] * PACK, LANES)
    lo, hi = [], []
    for s in range(PACK):
        w = lax.bitcast_convert_type(flat[pl.ds(s, rows, stride=PACK), :], jnp.uint32)
        lo.append(lax.bitcast_convert_type(w << 16, F32))
        hi.append(lax.bitcast_convert_type(w & jnp.uint32(0xFFFF0000), F32))
    return jnp.concatenate(lo + hi, axis=-1).astype(BF16)


def _dispatch_kernel(stab_ref, ctab_ref, cmax_ref, xn_ref, destt_ref, xs_in, xs_ref,
                     stage0, stage1, stage2, sems, sem2):
    del xs_in
    i = pl.program_id(0)
    last = pl.num_programs(0) - 1
    x = xn_ref[...]
    dt = destt_ref[...]
    rio = lax.broadcasted_iota(jnp.int32, (DISP_CHUNK, 1), 0).astype(F32)

    def chunk_rows(j, stage):
        parts = []
        for e in range(N_EXPERTS):
            first = (stab_ref[i * N_EXPERTS + e] + j * DISP_CHUNK).astype(F32)
            parts.append(jnp.where(dt[e:e + 1, :] == first + rio, 1.0, 0.0).astype(BF16))
        onehot = jnp.concatenate(parts, axis=0)
        words = _pack_rows(jnp.dot(onehot, x, preferred_element_type=F32))
        for s in range(PACK):
            stage[pl.ds(s, N_EXPERTS * DISP_CHUNK, stride=PACK), :] = words[:, s * LANES:(s + 1) * LANES]

    def copy(stage, step, e, j, sem):
        first = stab_ref[step * N_EXPERTS + e] + j * DISP_CHUNK
        rows = stage.reshape(N_EXPERTS * DISP_CHUNK, PACK, LANES)
        return pltpu.make_async_copy(rows.at[pl.ds(e * DISP_CHUNK, DISP_CHUNK)],
                                     xs_ref.at[pl.ds(first, DISP_CHUNK)], sem)

    def step_body(stage, prev_stage, par):
        chunk_rows(0, stage)

        @pl.when(i > 0)
        def _():
            for e in range(N_EXPERTS):
                copy(prev_stage, i - 1, e, 0, sems.at[1 - par]).wait()

        for e in range(N_EXPERTS):
            copy(stage, i, e, 0, sems.at[par]).start()

        @pl.when(i == last)
        def _():
            for e in range(N_EXPERTS):
                copy(stage, i, e, 0, sems.at[par]).wait()

    @pl.when(i % 2 == 0)
    def _():
        step_body(stage0, stage1, 0)

    @pl.when(i % 2 == 1)
    def _():
        step_body(stage1, stage0, 1)

    for j in range(1, TOK_TILE // DISP_CHUNK):

        @pl.when(cmax_ref[i] > j * DISP_CHUNK)
        def _(j=j):
            chunk_rows(j, stage2)
            for e in range(N_EXPERTS):

                @pl.when(ctab_ref[i * N_EXPERTS + e] > j * DISP_CHUNK)
                def _(e=e):
                    cp = copy(stage2, i, e, j, sem2)
                    cp.start()
                    cp.wait()


def _dispatch_call(stab, ctab, cmax, xn, destt, xs_zero):
    stage = pltpu.VMEM((N_EXPERTS * DISP_CHUNK * PACK, LANES), jnp.int32)
    grid_spec = pltpu.PrefetchScalarGridSpec(
        num_scalar_prefetch=3,
        grid=(N_PAD_TILES,),
        in_specs=[
            pl.BlockSpec((TOK_TILE, D_MODEL), lambda i, *_: (i, 0)),
            pl.BlockSpec((N_EXPERTS, TOK_TILE), lambda i, *_: (0, i)),
            pl.BlockSpec(memory_space=pl.ANY),
        ],
        out_specs=pl.BlockSpec(memory_space=pl.ANY),
        scratch_shapes=[stage, stage, stage, pltpu.SemaphoreType.DMA((2,)), pltpu.SemaphoreType.DMA],
    )
    return pl.pallas_call(
        _dispatch_kernel,
        grid_spec=grid_spec,
        out_shape=jax.ShapeDtypeStruct((XS_ROWS, PACK, LANES), jnp.int32),
        input_output_aliases={5: 0},
        compiler_params=pltpu.CompilerParams(
            dimension_semantics=("arbitrary",), vmem_limit_bytes=VMEM_LIMIT),
        name="moe_dispatch",
    )(stab, ctab, cmax, xn, destt, xs_zero)


def _expert_kernel(blke_ref, nblk_ref, live_ref, xs_ref, wgu_hbm, bgu_ref, wdn_hbm, bdn_ref,
                   ys_ref, wgu_f, wdn_f, wgu_s, wdn_s, sems):
    b = pl.program_id(0)
    used = b < nblk_ref[0]
    prev = blke_ref[jnp.maximum(b - 1, 0)]
    fresh = used & ((b == 0) | (blke_ref[b] != prev))

    def fetch(e):
        return (pltpu.make_async_copy(wgu_hbm.at[e], wgu_f, sems.at[0]),
                pltpu.make_async_copy(wdn_hbm.at[e], wdn_f, sems.at[1]))

    @pl.when(b == 0)
    def _():
        for cp in fetch(blke_ref[0]):
            cp.start()

    @pl.when(fresh)
    def _():
        for cp in fetch(blke_ref[b]):
            cp.wait()
        wgu_s[...] = wgu_f[...].astype(BF16)
        wdn_s[...] = wdn_f[...].astype(BF16)

        nxt = lax.while_loop(lambda p: (p < nblk_ref[0]) & (blke_ref[jnp.minimum(p, N_XBLOCKS - 1)] == blke_ref[b]),
                             lambda p: p + 1, b + 1)

        @pl.when(nxt < nblk_ref[0])
        def _():
            for cp in fetch(blke_ref[jnp.minimum(nxt, N_XBLOCKS - 1)]):
                cp.start()

    for groups in range(1, EXP_BLOCK // ROW_GROUP + 1):
        rows = groups * ROW_GROUP

        @pl.when(used & (live_ref[b] == groups))
        def _(rows=rows):
            hid = jnp.dot(_unpack_rows(xs_ref, rows), wgu_s[...], preferred_element_type=F32) + bgu_ref[...]
            gate = jnp.minimum(hid[:, :D_FF], SWIGLU_LIMIT)
            up = jnp.clip(hid[:, D_FF:], -SWIGLU_LIMIT, SWIGLU_LIMIT)
            act = (up + 1.0) * gate * jax.nn.sigmoid(SWIGLU_ALPHA * gate)
            y = jnp.dot(act.astype(BF16), wdn_s[...], preferred_element_type=F32) + bdn_ref[...]
            ys_ref[0:rows, :] = y.astype(BF16)
            if rows < EXP_BLOCK:
                ys_ref[rows:, :] = jnp.zeros((EXP_BLOCK - rows, D_MODEL), BF16)

    @pl.when(jnp.logical_not(used) | (live_ref[b] == 0))
    def _():
        ys_ref[...] = jnp.zeros_like(ys_ref)


def _expert_call(blke, nblk, live, xs, wgu, bgu, wdn, bdn):
    grid_spec = pltpu.PrefetchScalarGridSpec(
        num_scalar_prefetch=3,
        grid=(N_XBLOCKS,),
        in_specs=[
            pl.BlockSpec((EXP_BLOCK, PACK, LANES), lambda b, be, *_: (b, 0, 0)),
            pl.BlockSpec(memory_space=pl.ANY),
            pl.BlockSpec((None, 1, 2 * D_FF), lambda b, be, *_: (be[b], 0, 0)),
            pl.BlockSpec(memory_space=pl.ANY),
            pl.BlockSpec((None, 1, D_MODEL), lambda b, be, *_: (be[b], 0, 0)),
        ],
        out_specs=pl.BlockSpec((EXP_BLOCK, D_MODEL), lambda b, be, *_: (b, 0)),
        scratch_shapes=[pltpu.VMEM((D_MODEL, 2 * D_FF), F32), pltpu.VMEM((D_FF, D_MODEL), F32),
                        pltpu.VMEM((D_MODEL, 2 * D_FF), BF16), pltpu.VMEM((D_FF, D_MODEL), BF16),
                        pltpu.SemaphoreType.DMA((2,))],
    )
    return pl.pallas_call(
        _expert_kernel,
        grid_spec=grid_spec,
        out_shape=jax.ShapeDtypeStruct((N_ROWS, D_MODEL), BF16),
        compiler_params=pltpu.CompilerParams(
            dimension_semantics=("arbitrary",), vmem_limit_bytes=VMEM_LIMIT),
        name="moe_experts",
    )(blke, nblk, live, xs, wgu, bgu, wdn, bdn)


def _combine_kernel(astart_ref, nwin_ref, over_ref,
                    destm_ref, posm_ref, gm_ref, rep_ref, h1_ref, plep_ref, ples_ref, gple_ref, wpg_ref, wpp_ref,
                    gfin_ref, ys_any, yp_ref, ysm_ref, wins, moe_s, tmp_s, wsems, sem):
    i = pl.program_id(0)
    slot = i % 2

    def win_copy(step, e, into):
        first = pl.multiple_of(astart_ref[step * N_EXPERTS + e], WIN_ALIGN)
        return pltpu.make_async_copy(ys_any.at[pl.ds(first, WIN)], wins.at[into, pl.ds(e * WIN, WIN)],
                                     wsems.at[into])

    @pl.when(i == 0)
    def _():
        for e in range(N_EXPERTS):
            win_copy(0, e, 0).start()

    @pl.when(i + 1 < pl.num_programs(0))
    def _():
        for e in range(N_EXPERTS):
            win_copy(i + 1, e, 1 - slot).start()

    for e in range(N_EXPERTS):
        win_copy(i, e, slot).wait()

    gates = gm_ref[...]
    rep = rep_ref[...]
    pos_rep = jnp.dot(posm_ref[...].astype(BF16), rep, preferred_element_type=F32)
    gate_rep = jnp.dot(gates.astype(BF16), rep, preferred_element_type=F32)
    within = (_lane_iota((TOK_TILE, N_EXPERTS * WIN)) % WIN).astype(F32)
    gsel = jnp.where(pos_rep == within, gate_rep, 0.0).astype(BF16)
    mrows = pl.ds(pl.multiple_of((i % TAIL_TILES) * TOK_TILE, TOK_TILE), TOK_TILE)
    moe_s[mrows, :] = jnp.dot(gsel, wins[slot], preferred_element_type=F32)

    @pl.when(over_ref[i] > 0)
    def _():
        dest = destm_ref[...]
        lane = _lane_iota((TOK_TILE, LANES))
        lane_f = lane.astype(F32)
        tmp_s[...] = jnp.zeros_like(tmp_s)

        def per_expert(e, carry):
            a = astart_ref[i * N_EXPERTS + e]
            dcol = jnp.sum(jnp.where(lane == e, dest, 0.0), axis=-1, keepdims=True)
            gcol = jnp.sum(jnp.where(lane == e, gates, 0.0), axis=-1, keepdims=True)

            def per_window(w, carry2):
                first = a + w * WIN
                start = pl.multiple_of(jnp.minimum(first, N_ROWS - WIN), WIN_ALIGN)
                cp = pltpu.make_async_copy(ys_any.at[pl.ds(start, WIN)], tmp_s.at[pl.ds(0, WIN)], sem)
                cp.start()
                cp.wait()
                hit = (lane < WIN) & (dcol == start.astype(F32) + lane_f) & (dcol >= first.astype(F32))
                sel = jnp.where(hit, gcol, 0.0).astype(BF16)
                moe_s[mrows, :] += jnp.dot(sel, tmp_s[...], preferred_element_type=F32)
                return carry2

            return lax.fori_loop(1, nwin_ref[i * N_EXPERTS + e], per_window, carry)

        lax.fori_loop(0, N_EXPERTS, per_expert, 0)

    def tail(h1, moe_rows, ple):
        h2 = h1 + moe_rows
        hn = _rms(h2, gple_ref[...]).astype(BF16)
        gate = jax.nn.sigmoid(jnp.dot(hn, wpg_ref[...], preferred_element_type=F32))
        proj = jnp.dot(ple.astype(BF16), wpp_ref[...], preferred_element_type=F32)
        return _rms(h2 + gate * proj, gfin_ref[...])

    is_sample = i == N_CTILES - 1

    @pl.when((i % TAIL_TILES == TAIL_TILES - 1) & jnp.logical_not(is_sample))
    def _():
        yp_ref[...] = tail(h1_ref[...], moe_s[...], plep_ref[...])

    @pl.when(is_sample)
    def _():
        ysm_ref[...] = tail(h1_ref[0:DEC_BATCH, :], moe_s[0:DEC_BATCH, :], ples_ref[...])


def _combine_call(astart, nwin, over, ys, destm, posm, gm, h1, plep, ples, gple, wpg, wpp, gfin):
    rep = (jnp.arange(N_EXPERTS * WIN)[None, :] // WIN == jnp.arange(LANES)[:, None]).astype(BF16)
    rows = TAIL_TILES * TOK_TILE
    last = N_PROMPT // rows - 1

    grid_spec = pltpu.PrefetchScalarGridSpec(
        num_scalar_prefetch=3,
        grid=(N_CTILES,),
        in_specs=[
            pl.BlockSpec((TOK_TILE, LANES), lambda i, *_: (i, 0)),
            pl.BlockSpec((TOK_TILE, LANES), lambda i, *_: (i, 0)),
            pl.BlockSpec((TOK_TILE, LANES), lambda i, *_: (i, 0)),
            pl.BlockSpec((LANES, N_EXPERTS * WIN), lambda i, *_: (0, 0)),
            pl.BlockSpec((rows, D_MODEL), lambda i, *_: (i // TAIL_TILES, 0)),
            pl.BlockSpec((rows, PLE_DIM), lambda i, *_: (jnp.minimum(i // TAIL_TILES, last), 0)),
            pl.BlockSpec((DEC_BATCH, PLE_DIM), lambda i, *_: (0, 0)),
            pl.BlockSpec((1, D_MODEL), lambda i, *_: (0, 0)),
            pl.BlockSpec((D_MODEL, D_MODEL), lambda i, *_: (0, 0)),
            pl.BlockSpec((PLE_DIM, D_MODEL), lambda i, *_: (0, 0)),
            pl.BlockSpec((1, D_MODEL), lambda i, *_: (0, 0)),
            pl.BlockSpec(memory_space=pl.ANY),
        ],
        out_specs=[
            pl.BlockSpec((rows, D_MODEL), lambda i, *_: (jnp.minimum(i // TAIL_TILES, last), 0)),
            pl.BlockSpec((DEC_BATCH, D_MODEL), lambda i, *_: (0, 0)),
        ],
        scratch_shapes=[pltpu.VMEM((2, N_EXPERTS * WIN, D_MODEL), BF16), pltpu.VMEM((rows, D_MODEL), F32),
                        pltpu.VMEM((2 * WIN, D_MODEL), BF16), pltpu.SemaphoreType.DMA((2,)),
                        pltpu.SemaphoreType.DMA],
    )
    return pl.pallas_call(
        _combine_kernel,
        grid_spec=grid_spec,
        out_shape=[jax.ShapeDtypeStruct((N_PROMPT, D_MODEL), F32),
                   jax.ShapeDtypeStruct((DEC_BATCH, D_MODEL), F32)],
        compiler_params=pltpu.CompilerParams(
            dimension_semantics=("arbitrary",), vmem_limit_bytes=VMEM_LIMIT),
        name="moe_combine_tail",
    )(astart, nwin, over, destm, posm, gm, rep, h1, plep, ples, gple, wpg, wpp, gfin, ys)


def kernel(x_prompt, x_sample, cache_swa_k, cache_swa_v, p_prompt, p_sample, g_mix, w_in, ln_v_g, ln_v_b,
           w_sp, b_sp, sinks, g_out_a, g_out_b, w_o, g_moe, w_router, b_router, w_gu, b_gu, w_dn, b_dn,
           g_ple, w_ple_gate, w_ple_proj, g_final):
    l = 0
    row = lambda v: v.reshape(1, -1)
    win = w_in[l].astype(BF16)
    wo = w_o[l].astype(BF16)
    tril = jnp.tril(jnp.ones((CHUNK, CHUNK), dtype=bool))
    wsp = jnp.where(tril, w_sp[l], 0.0).astype(BF16)
    bsp = jnp.repeat(b_sp[l].T, HEAD_DIM, axis=1)
    w00 = row(jnp.repeat(w_sp[l][:, 0, 0], HEAD_DIM))
    b0 = row(jnp.repeat(b_sp[l][:, 0], HEAD_DIM))
    wr_hi = w_router[l].astype(BF16)
    wr_lo = (w_router[l] - wr_hi.astype(F32)).astype(BF16)
    wr = jnp.concatenate([wr_hi, wr_lo, jnp.zeros((D_MODEL, LANES - 2 * N_EXPERTS), BF16)], axis=1)
    br = row(jnp.concatenate([b_router[l], jnp.zeros((LANES - N_EXPERTS,), F32)]))
    common = (row(g_mix[l]), win, row(ln_v_g[l]), row(ln_v_b[l]))
    tail = (row(g_out_a[l]), row(g_out_b[l]), wo, row(g_moe[l]), wr, br)

    h1, xn, gm, sm, k_p, v_p = _prompt_call(
        x_prompt.reshape(N_PROMPT, D_MODEL), sinks[l], *common, wsp, bsp, *tail)
    ck = cache_swa_k[l].reshape(DEC_BATCH * CHUNK * 2, HEAD_DIM)
    cv = cache_swa_v[l].reshape(DEC_BATCH * CHUNK * 2, HEAD_DIM)
    h1, xn, gm, sm, k_s, v_s, va_s = _decode_call(
        x_sample.reshape(DEC_BATCH, D_MODEL), ck, cv, sinks[l], *common, w00, b0, *tail, h1, xn, gm, sm)

    destm, posm, destt, xbe, live, nxblk, stab, ctab, astart, nwin = _plan_call(sm)
    flat = lambda tab, n: tab[:n, :N_EXPERTS].reshape(-1)
    over = (jnp.max(nwin[:N_CTILES, :N_EXPERTS], axis=1) > 1).astype(jnp.int32)
    cmax = jnp.max(ctab[:N_PAD_TILES, :N_EXPERTS], axis=1)

    xs = _dispatch_call(flat(stab, N_PAD_TILES), flat(ctab, N_PAD_TILES), cmax, xn, destt[:N_EXPERTS],
                        jnp.zeros((XS_ROWS, PACK, LANES), jnp.int32))
    ys = _expert_call(xbe[:N_XBLOCKS, 0], nxblk[0, :1], live[:N_XBLOCKS, 0], xs,
                      w_gu[l], b_gu[l].reshape(N_EXPERTS, 1, 2 * D_FF), w_dn[l], b_dn[l].reshape(N_EXPERTS, 1, D_MODEL))
    y_p, y_s = _combine_call(
        flat(astart, N_CTILES), flat(nwin, N_CTILES), over, ys, destm, posm, gm, h1,
        p_prompt[l].reshape(N_PROMPT, PLE_DIM), p_sample[l].reshape(DEC_BATCH, PLE_DIM),
        row(g_ple[l]), w_ple_gate[l].astype(BF16), w_ple_proj[l].astype(BF16), row(g_final))

    kv5 = lambda a, n: a.reshape(1, n, CHUNK, 2, HEAD_DIM)
    return (y_p.reshape(BATCH, SEQ, D_MODEL), y_s.reshape(DEC_BATCH, 1, D_MODEL),
            kv5(k_p, BATCH), kv5(v_p, BATCH), kv5(k_s, DEC_BATCH), kv5(v_s, DEC_BATCH),
            va_s.reshape(1, DEC_BATCH, 1, A_WIDTH))
```

```python
import math

import jax
import jax.numpy as jnp
from jax import lax
from jax.experimental import pallas as pl
from jax.experimental.pallas import tpu as pltpu

F32 = jnp.float32
BF16 = jnp.bfloat16

D_MODEL = 1024
BATCH = 4
SEQ = 4096
DEC_BATCH = 128
HEAD_DIM = 64
A_WIDTH = 512
B_WIDTH = 512
B_HEADS = 8
KV_WIDTH = 128
IN_WIDTH = 2 * A_WIDTH + B_WIDTH + 2 * KV_WIDTH
CHUNK = 128
N_EXPERTS = 32
TOP_K = 4
D_FF = 1024
SWIGLU_ALPHA = 1.702
SWIGLU_LIMIT = 7.0
PLE_DIM = 256
EPS = 1e-5

LANES = 128
VMEM_LIMIT = 56 * 1024 * 1024
N_PROMPT = BATCH * SEQ
N_TOK = N_PROMPT + DEC_BATCH
TM = 1024
N_PAD = ((N_TOK + TM - 1) // TM) * TM
TOK_TILE = 128
N_PAD_TILES = N_PAD // TOK_TILE
N_CTILES = N_PROMPT // TOK_TILE + 1
DEC_TILE = 16
EXP_BLOCK = 512
ROW_GROUP = 128
DISP_CHUNK = 32
N_XBLOCKS = (N_TOK * TOP_K + N_EXPERTS * (DISP_CHUNK + EXP_BLOCK - 1) + EXP_BLOCK - 1) // EXP_BLOCK
N_ROWS = N_XBLOCKS * EXP_BLOCK
XS_ROWS = N_ROWS + N_EXPERTS * DISP_CHUNK
XBE_ROWS = ((N_XBLOCKS + 7) // 8) * 8
TAB_ROWS = ((N_PAD_TILES + 7) // 8) * 8
PACK = D_MODEL // 2 // LANES
WIN = 64
WIN_ALIGN = 16
TAIL_TILES = 4
NEG = -1e30


def _rms(x, g):
    return x * lax.rsqrt(jnp.mean(x * x, axis=-1, keepdims=True) + EPS) * g


def _gelu(x):
    c = math.sqrt(2.0 / math.pi)
    return x * (0.5 * (1.0 + jnp.tanh(c * (x + 0.044715 * (x * x * x)))))


def _layernorm(x, g, b):
    mu = jnp.mean(x, axis=-1, keepdims=True)
    xc = x - mu
    return xc * lax.rsqrt(jnp.mean(xc * xc, axis=-1, keepdims=True) + EPS) * g + b


def _lane_iota(shape):
    return lax.broadcasted_iota(jnp.int32, shape, len(shape) - 1)


def _full(shape):
    n = len(shape)
    return pl.BlockSpec(shape, lambda *_: (0,) * n)


def _route(xn2, wr_ref, br_ref):
    m = xn2.shape[0]
    xh = xn2.astype(BF16)
    xl = (xn2 - xh.astype(F32)).astype(BF16)
    r = jnp.dot(jnp.concatenate([xh, xl], axis=0), wr_ref[...], preferred_element_type=F32)
    r = r[:m] + r[m:]
    lane = _lane_iota((m, LANES))
    lane_f = lane.astype(F32)
    logits = jnp.where(lane < N_EXPERTS, r + pltpu.roll(r, LANES - N_EXPERTS, 1) + br_ref[...], NEG)
    work = logits
    sel = jnp.zeros((m, LANES), F32)
    top = None
    z = None
    for _ in range(TOP_K):
        mx = jnp.max(work, axis=-1, keepdims=True)
        first = jnp.min(jnp.where(work == mx, lane_f, float(LANES)), axis=-1, keepdims=True)
        hit = lane_f == first
        sel = jnp.where(hit, 1.0, sel)
        work = jnp.where(hit, NEG, work)
        if top is None:
            top = mx
            z = jnp.ones_like(mx)
        else:
            z = z + jnp.exp(mx - top)
    gates = jnp.where(sel > 0.0, jnp.exp(logits - top) / z, 0.0)
    return gates, sel


def _prompt_kernel(sinks_ref, x_ref, gmix_ref, win_ref, lng_ref, lnb_ref, wsp_ref, bsp_ref,
                   goa_ref, gob_ref, wo_ref, gmoe_ref, wr_ref, br_ref,
                   h1_ref, xn_ref, gm_ref, sm_ref, k_ref, v_ref,
                   z_s, kv_s, cat_s):
    g = pl.program_id(0)

    @pl.when(g >= N_PROMPT // TM)
    def _():
        h1_ref[...] = jnp.zeros_like(h1_ref)
        xn_ref[...] = jnp.zeros_like(xn_ref)
        gm_ref[...] = jnp.zeros_like(gm_ref)
        sm_ref[...] = jnp.zeros_like(sm_ref)

    @pl.when(g < N_PROMPT // TM)
    def _():
        _prompt_tile(g % (SEQ // TM), sinks_ref, x_ref, gmix_ref, win_ref, lng_ref, lnb_ref, wsp_ref, bsp_ref,
                     goa_ref, gob_ref, wo_ref, gmoe_ref, wr_ref, br_ref,
                     h1_ref, xn_ref, gm_ref, sm_ref, k_ref, v_ref, z_s, kv_s, cat_s)


def _prompt_tile(j, sinks_ref, x_ref, gmix_ref, win_ref, lng_ref, lnb_ref, wsp_ref, bsp_ref,
                 goa_ref, gob_ref, wo_ref, gmoe_ref, wr_ref, br_ref,
                 h1_ref, xn_ref, gm_ref, sm_ref, k_ref, v_ref, z_s, kv_s, cat_s):
    @pl.when(j == 0)
    def _():
        kv_s[0:CHUNK, :] = jnp.zeros((CHUNK, 2 * KV_WIDTH), F32)

    xn = _rms(x_ref[...], gmix_ref[...]).astype(BF16)
    z_s[...] = jnp.dot(xn, win_ref[...], preferred_element_type=F32)
    kv_s[CHUNK:, :] = z_s[:, 2 * A_WIDTH + B_WIDTH:]

    lane = _lane_iota((CHUNK, LANES))
    lo = lane < HEAD_DIM
    lane2 = _lane_iota((2 * CHUNK, LANES))
    lo2 = lane2 < HEAD_DIM
    qi = lax.broadcasted_iota(jnp.int32, (CHUNK, CHUNK), 0)
    kc = lax.broadcasted_iota(jnp.int32, (CHUNK, CHUNK), 1)
    from_prev = kc > qi
    dist = jnp.where(from_prev, qi + CHUNK - kc, qi - kc).astype(F32)

    def chunk_body(c, carry):
        r0 = pl.multiple_of(c * CHUNK, CHUNK)
        rows = pl.ds(r0, CHUNK)
        u = _gelu(z_s[rows, 0:A_WIDTH])
        va = _layernorm(_gelu(z_s[rows, A_WIDTH:2 * A_WIDTH]), lng_ref[...], lnb_ref[...])
        vab = va.astype(BF16)
        slabs = []
        for p in range(A_WIDTH // LANES):
            slab = vab[:, p * LANES:(p + 1) * LANES]
            m0 = jnp.dot(wsp_ref[2 * p], slab, preferred_element_type=F32)
            m1 = jnp.dot(wsp_ref[2 * p + 1], slab, preferred_element_type=F32)
            slabs.append(jnp.where(lo, m0, m1))
        ya = u * (jnp.concatenate(slabs, axis=-1) + bsp_ref[...])
        ya_n = _rms(ya, goa_ref[...])
        k2 = kv_s[pl.ds(r0, 2 * CHUNK), 0:KV_WIDTH]
        v2 = kv_s[pl.ds(r0, 2 * CHUNK), KV_WIDTH:2 * KV_WIDTH]
        k2r = pltpu.roll(k2, HEAD_DIM, 1)
        v2r = pltpu.roll(v2, HEAD_DIM, 1)
        kd = (jnp.where(lo2, k2, k2r).astype(BF16), jnp.where(lo2, k2r, k2).astype(BF16))
        vd = (jnp.where(lo2, v2, v2r).astype(BF16), jnp.where(lo2, v2r, v2).astype(BF16))
        prev_ok = (j > 0) | (c > 0)
        masked = from_prev & jnp.logical_not(prev_ok)
        yb_slabs = []
        for kv in range(2):
            q0 = z_s[rows, 2 * A_WIDTH + (2 * kv) * LANES:2 * A_WIDTH + (2 * kv + 1) * LANES]
            q1 = z_s[rows, 2 * A_WIDTH + (2 * kv + 1) * LANES:2 * A_WIDTH + (2 * kv + 2) * LANES]
            lhs = jnp.concatenate([jnp.where(lo, q0, 0.0), jnp.where(lo, 0.0, q0),
                                   jnp.where(lo, q1, 0.0), jnp.where(lo, 0.0, q1)], axis=0).astype(BF16)
            s_all = lax.dot_general(lhs, kd[kv], (((1,), (1,)), ((), ())), preferred_element_type=F32)
            probs = []
            for i in range(4):
                h = 4 * kv + i
                slope = 2.0 ** (-(h + 1))
                sink = sinks_ref[h]
                sh = s_all[i * CHUNK:(i + 1) * CHUNK]
                s = jnp.where(from_prev, sh[:, :CHUNK], sh[:, CHUNK:]) * (HEAD_DIM ** -0.5) - slope * dist
                s = jnp.where(masked, NEG, s)
                mx = jnp.maximum(jnp.max(s, axis=-1, keepdims=True), sink)
                e = jnp.exp(s - mx)
                den = jnp.sum(e, axis=-1, keepdims=True) + jnp.exp(sink - mx)
                p = e * (1.0 / den)
                probs.append(jnp.concatenate([jnp.where(from_prev, p, 0.0), jnp.where(from_prev, 0.0, p)], axis=-1))
            pm = jnp.concatenate(probs, axis=0).astype(BF16)
            o = jnp.dot(pm, vd[kv], preferred_element_type=F32)
            yb_slabs.append(jnp.where(lo, o[0:CHUNK], o[CHUNK:2 * CHUNK]))
            yb_slabs.append(jnp.where(lo, o[2 * CHUNK:3 * CHUNK], o[3 * CHUNK:4 * CHUNK]))
        yb_n = _rms(jnp.concatenate(yb_slabs, axis=-1), gob_ref[...])
        cat_s[rows, 0:A_WIDTH] = ya_n.astype(BF16)
        cat_s[rows, A_WIDTH:] = yb_n.astype(BF16)
        return carry

    lax.fori_loop(0, TM // CHUNK, chunk_body, 0)

    kv_s[0:CHUNK, :] = kv_s[TM:TM + CHUNK, :]
    k_ref[...] = kv_s[TM:TM + CHUNK, 0:KV_WIDTH]
    v_ref[...] = kv_s[TM:TM + CHUNK, KV_WIDTH:]

    h1 = x_ref[...] + jnp.dot(cat_s[...], wo_ref[...], preferred_element_type=F32)
    h1_ref[...] = h1
    xn2 = _rms(h1, gmoe_ref[...])
    xn_ref[...] = xn2.astype(BF16)
    gates, sel = _route(xn2, wr_ref, br_ref)
    gm_ref[...] = gates
    sm_ref[...] = sel


def _prompt_call(x, sinks, gmix, win, lng, lnb, wsp, bsp, goa, gob, wo, gmoe, wr, br):
    real = N_PROMPT // TM
    row = lambda g: (g, 0)
    seq = lambda g: (jnp.minimum(g, real - 1) // (SEQ // TM), 0, 0)
    return pl.pallas_call(
        _prompt_kernel,
        grid=(N_PAD // TM,),
        in_specs=[
            pl.BlockSpec(memory_space=pltpu.SMEM),
            pl.BlockSpec((TM, D_MODEL), lambda g: (jnp.minimum(g, real - 1), 0)),
            _full((1, D_MODEL)), _full((D_MODEL, IN_WIDTH)), _full((1, A_WIDTH)), _full((1, A_WIDTH)),
            _full((8, CHUNK, CHUNK)), _full((CHUNK, A_WIDTH)), _full((1, A_WIDTH)), _full((1, B_WIDTH)),
            _full((D_MODEL, D_MODEL)), _full((1, D_MODEL)), _full((D_MODEL, LANES)), _full((1, LANES)),
        ],
        out_specs=[
            pl.BlockSpec((TM, D_MODEL), row),
            pl.BlockSpec((TM, D_MODEL), row),
            pl.BlockSpec((TM, LANES), row),
            pl.BlockSpec((TM, LANES), row),
            pl.BlockSpec((None, CHUNK, KV_WIDTH), seq),
            pl.BlockSpec((None, CHUNK, KV_WIDTH), seq),
        ],
        out_shape=[
            jax.ShapeDtypeStruct((N_PAD, D_MODEL), F32),
            jax.ShapeDtypeStruct((N_PAD, D_MODEL), BF16),
            jax.ShapeDtypeStruct((N_PAD, LANES), F32),
            jax.ShapeDtypeStruct((N_PAD, LANES), F32),
            jax.ShapeDtypeStruct((BATCH, CHUNK, KV_WIDTH), F32),
            jax.ShapeDtypeStruct((BATCH, CHUNK, KV_WIDTH), F32),
        ],
        scratch_shapes=[
            pltpu.VMEM((TM, IN_WIDTH), F32),
            pltpu.VMEM((TM + CHUNK, 2 * KV_WIDTH), F32),
            pltpu.VMEM((TM, D_MODEL), BF16),
        ],
        compiler_params=pltpu.CompilerParams(
            dimension_semantics=("arbitrary",), vmem_limit_bytes=VMEM_LIMIT),
        name="prompt_premoe",
    )(sinks, x, gmix, win, lng, lnb, wsp, bsp, goa, gob, wo, gmoe, wr, br)


def _decode_kernel(sinks_ref, x_ref, ck_ref, cv_ref, gmix_ref, win_ref, lng_ref, lnb_ref, w00_ref, b0_ref,
                   goa_ref, gob_ref, wo_ref, gmoe_ref, wr_ref, br_ref,
                   h1_in, xn_in, gm_in, sm_in,
                   h1_ref, xn_ref, gm_ref, sm_ref, nk_ref, nv_ref, va_ref,
                   q_s, kn_s, vn_s, ya_s, yb_s):
    del h1_in, xn_in, gm_in, sm_in
    i = pl.program_id(0)
    t = DEC_TILE
    per_seq = CHUNK * 2
    ncols = t * per_seq

    @pl.when(i == 0)
    def _():
        xn = _rms(x_ref[...], gmix_ref[...]).astype(BF16)
        z = jnp.dot(xn, win_ref[...], preferred_element_type=F32)
        u = _gelu(z[:, 0:A_WIDTH])
        va = _layernorm(_gelu(z[:, A_WIDTH:2 * A_WIDTH]), lng_ref[...], lnb_ref[...])
        va_ref[...] = va
        ya_s[...] = _rms(u * (w00_ref[...] * va + b0_ref[...]), goa_ref[...])
        q_s[...] = z[:, 2 * A_WIDTH:2 * A_WIDTH + B_WIDTH]
        kn_s[...] = z[:, 2 * A_WIDTH + B_WIDTH:2 * A_WIDTH + B_WIDTH + KV_WIDTH]
        vn_s[...] = z[:, 2 * A_WIDTH + B_WIDTH + KV_WIDTH:]

    rows = pl.ds(pl.multiple_of(i * t, t), t)
    q = q_s[rows, :]
    kn = kn_s[rows, :]
    vn = vn_s[rows, :]
    qs = jnp.concatenate([q[:, h * HEAD_DIM:(h + 1) * HEAD_DIM] for h in range(B_HEADS)], axis=0)
    kn8 = jnp.concatenate([kn[:, (h // 4) * HEAD_DIM:(h // 4 + 1) * HEAD_DIM] for h in range(B_HEADS)], axis=0)
    vn8 = jnp.concatenate([vn[:, (h // 4) * HEAD_DIM:(h // 4 + 1) * HEAD_DIM] for h in range(B_HEADS)], axis=0)
    nrows = B_HEADS * t
    ridx = lax.broadcasted_iota(jnp.int32, (nrows, 1), 0)
    slope = jnp.zeros((nrows, 1), F32)
    sink = jnp.zeros((nrows, 1), F32)
    for h in range(B_HEADS):
        in_h = (ridx >= h * t) & (ridx < (h + 1) * t)
        slope = jnp.where(in_h, 2.0 ** (-(h + 1)), slope)
        sink = jnp.where(in_h, sinks_ref[h], sink)
    s_c = lax.dot_general(qs.astype(BF16), ck_ref[...].astype(BF16), (((1,), (1,)), ((), ())),
                          preferred_element_type=F32)
    rr = lax.broadcasted_iota(jnp.int32, (nrows, ncols), 0)
    col = lax.broadcasted_iota(jnp.int32, (nrows, ncols), 1)
    pos = (col // 2) % CHUNK
    own = ((col // per_seq) == (rr % t)) & ((col % 2) == (rr // (4 * t))) & (pos >= 1)
    s_c = s_c * (HEAD_DIM ** -0.5) - slope * (CHUNK - pos).astype(F32)
    s_c = jnp.where(own, s_c, NEG)
    s_n = jnp.sum(qs * kn8, axis=-1, keepdims=True) * (HEAD_DIM ** -0.5)
    mx = jnp.maximum(jnp.maximum(jnp.max(s_c, axis=-1, keepdims=True), s_n), sink)
    e_c = jnp.exp(s_c - mx)
    e_n = jnp.exp(s_n - mx)
    inv = 1.0 / (jnp.sum(e_c, axis=-1, keepdims=True) + e_n + jnp.exp(sink - mx))
    o = jnp.dot((e_c * inv).astype(BF16), cv_ref[...].astype(BF16), preferred_element_type=F32)
    o = o + (e_n * inv) * vn8
    yb_s[rows, :] = jnp.concatenate([o[h * t:(h + 1) * t] for h in range(B_HEADS)], axis=-1)

    last_pos = lax.broadcasted_iota(jnp.int32, (CHUNK, KV_WIDTH), 0) == CHUNK - 1
    for b in range(t):
        for src, new, dst in ((ck_ref, kn, nk_ref), (cv_ref, vn, nv_ref)):
            win = jnp.concatenate([src[pl.ds(b * per_seq + kv, CHUNK, stride=2), :] for kv in range(2)], axis=-1)
            win = jnp.where(last_pos, new[b:b + 1, :], pltpu.roll(win, CHUNK - 1, 0))
            dst[b] = win.T

    @pl.when(i == pl.num_programs(0) - 1)
    def _():
        yb_n = _rms(yb_s[...], gob_ref[...])
        cat = jnp.concatenate([ya_s[...], yb_n], axis=-1).astype(BF16)
        h1 = x_ref[...] + jnp.dot(cat, wo_ref[...], preferred_element_type=F32)
        xn2 = _rms(h1, gmoe_ref[...])
        gates, sel = _route(xn2, wr_ref, br_ref)
        h1_ref[...] = h1
        xn_ref[...] = xn2.astype(BF16)
        gm_ref[...] = gates
        sm_ref[...] = sel


def _decode_call(x, ck, cv, sinks, gmix, win, lng, lnb, w00, b0, goa, gob, wo, gmoe, wr, br, h1, xn, gm, sm):
    t = DEC_TILE
    per_seq = CHUNK * 2
    cache = pl.BlockSpec((t * per_seq, HEAD_DIM), lambda i: (i, 0))
    tok = lambda width: pl.BlockSpec((DEC_BATCH, width), lambda i: (N_PROMPT // DEC_BATCH, 0))
    anyspec = pl.BlockSpec(memory_space=pl.ANY)
    return pl.pallas_call(
        _decode_kernel,
        grid=(DEC_BATCH // t,),
        in_specs=[
            pl.BlockSpec(memory_space=pltpu.SMEM),
            _full((DEC_BATCH, D_MODEL)), cache, cache,
            _full((1, D_MODEL)), _full((D_MODEL, IN_WIDTH)), _full((1, A_WIDTH)), _full((1, A_WIDTH)),
            _full((1, A_WIDTH)), _full((1, A_WIDTH)), _full((1, A_WIDTH)), _full((1, B_WIDTH)),
            _full((D_MODEL, D_MODEL)), _full((1, D_MODEL)), _full((D_MODEL, LANES)), _full((1, LANES)),
            anyspec, anyspec, anyspec, anyspec,
        ],
        out_specs=[tok(D_MODEL), tok(D_MODEL), tok(LANES), tok(LANES),
                   pl.BlockSpec((t, KV_WIDTH, CHUNK), lambda i: (i, 0, 0)),
                   pl.BlockSpec((t, KV_WIDTH, CHUNK), lambda i: (i, 0, 0)), _full((DEC_BATCH, A_WIDTH))],
        out_shape=[
            jax.ShapeDtypeStruct((N_PAD, D_MODEL), F32),
            jax.ShapeDtypeStruct((N_PAD, D_MODEL), BF16),
            jax.ShapeDtypeStruct((N_PAD, LANES), F32),
            jax.ShapeDtypeStruct((N_PAD, LANES), F32),
            jax.ShapeDtypeStruct((DEC_BATCH, KV_WIDTH, CHUNK), F32),
            jax.ShapeDtypeStruct((DEC_BATCH, KV_WIDTH, CHUNK), F32),
            jax.ShapeDtypeStruct((DEC_BATCH, A_WIDTH), F32),
        ],
        scratch_shapes=[pltpu.VMEM((DEC_BATCH, B_WIDTH), F32), pltpu.VMEM((DEC_BATCH, KV_WIDTH), F32),
                        pltpu.VMEM((DEC_BATCH, KV_WIDTH), F32), pltpu.VMEM((DEC_BATCH, A_WIDTH), F32),
                        pltpu.VMEM((DEC_BATCH, B_WIDTH), F32)],
        input_output_aliases={16: 0, 17: 1, 18: 2, 19: 3},
        compiler_params=pltpu.CompilerParams(
            dimension_semantics=("arbitrary",), vmem_limit_bytes=VMEM_LIMIT),
        name="sample_premoe",
    )(sinks, x, ck, cv, gmix, win, lng, lnb, w00, b0, goa, gob, wo, gmoe, wr, br, h1, xn, gm, sm)


def _plan_kernel(sm_ref,
                 destm_ref, posm_ref, destt_ref, xbe_ref, live_ref, nxblk_ref, stab_ref, ctab_ref, astart_ref,
                 nwin_ref, base_s, pstart_s):
    ph = pl.program_id(0)
    step = pl.program_id(1)
    lane = _lane_iota((1, LANES))

    @pl.when((ph == 0) & (step == 0))
    def _():
        base_s[...] = jnp.zeros_like(base_s)

    @pl.when(ph == 0)
    def _():
        base_s[...] += jnp.sum(sm_ref[...], axis=0, keepdims=True)

    @pl.when((ph == 1) & (step == 0))
    def _():
        counts = base_s[...]
        padded = jnp.floor((counts + (DISP_CHUNK + EXP_BLOCK - 1)) * (1.0 / EXP_BLOCK)) * EXP_BLOCK
        padded = jnp.where(counts > 0.0, padded, 0.0)
        pend = padded
        for s in (1, 2, 4, 8, 16):
            pend = pend + jnp.where(lane >= s, pltpu.roll(pend, s, 1), 0.0)
        spare = (N_ROWS + lane * DISP_CHUNK).astype(F32)
        pstart_s[...] = jnp.where(counts > 0.0, pend - padded, spare)
        base_s[...] = jnp.zeros_like(base_s)
        brow = lax.broadcasted_iota(jnp.int32, (XBE_ROWS, LANES), 0).astype(F32) * EXP_BLOCK
        done = jnp.where((lane < N_EXPERTS) & (pend <= brow), 1.0, 0.0)
        be = jnp.minimum(jnp.sum(done, axis=-1, keepdims=True), N_EXPERTS - 1.0)
        xbe_ref[...] = jnp.broadcast_to(be, (XBE_ROWS, LANES)).astype(jnp.int32)
        real = jnp.clip(counts - (brow - (pend - padded)), 0.0, float(EXP_BLOCK))
        real = jnp.sum(jnp.where(lane.astype(F32) == be, real, 0.0), axis=-1, keepdims=True)
        groups = jnp.floor((real + (ROW_GROUP - 1)) * (1.0 / ROW_GROUP))
        live_ref[...] = jnp.broadcast_to(groups, (XBE_ROWS, LANES)).astype(jnp.int32)
        total = jnp.sum(jnp.where(lane == N_EXPERTS - 1, pend, 0.0), axis=-1, keepdims=True)
        nxblk_ref[...] = jnp.broadcast_to(total * (1.0 / EXP_BLOCK), (8, LANES)).astype(jnp.int32)
        stab_ref[...] = jnp.zeros_like(stab_ref)
        ctab_ref[...] = jnp.zeros_like(ctab_ref)
        astart_ref[...] = jnp.zeros_like(astart_ref)
        nwin_ref[...] = jnp.zeros_like(nwin_ref)

    @pl.when(ph == 1)
    def _():
        r = lax.broadcasted_iota(jnp.int32, (TOK_TILE, TOK_TILE), 0)
        c = lax.broadcasted_iota(jnp.int32, (TOK_TILE, TOK_TILE), 1)
        lower = jnp.where(c < r, 1.0, 0.0).astype(BF16)
        for q in range(TM // TOK_TILE):
            i = step * (TM // TOK_TILE) + q
            sel = sm_ref[q * TOK_TILE:(q + 1) * TOK_TILE, :]
            cnt = jnp.sum(sel, axis=0, keepdims=True)
            prefix = jnp.dot(lower, sel.astype(BF16), preferred_element_type=F32)
            start = pstart_s[...] + base_s[...]
            dest = jnp.where(sel > 0.0, prefix + start, -1.0)
            destm_ref[q * TOK_TILE:(q + 1) * TOK_TILE, :] = dest
            destt_ref[:, q * TOK_TILE:(q + 1) * TOK_TILE] = dest.T
            has = (cnt > 0.0) & (lane < N_EXPERTS)
            stab_ref[pl.ds(i, 1), :] = start.astype(jnp.int32)
            ctab_ref[pl.ds(i, 1), :] = jnp.where(has, cnt, 0.0).astype(jnp.int32)
            a = jnp.minimum(jnp.floor(start * (1.0 / WIN_ALIGN)) * WIN_ALIGN, float(N_ROWS - WIN))
            nw = jnp.where(has, jnp.floor((start + cnt - a + (WIN - 1)) * (1.0 / WIN)), 0.0)
            posm_ref[q * TOK_TILE:(q + 1) * TOK_TILE, :] = jnp.where(sel > 0.0, prefix + start - a, -1.0)
            astart_ref[pl.ds(i, 1), :] = a.astype(jnp.int32)
            nwin_ref[pl.ds(i, 1), :] = nw.astype(jnp.int32)
            base_s[...] += cnt


def _plan_call(sm):
    tile = lambda ph, i: (i * ph, 0)
    tile_t = lambda ph, i: (0, i * ph)
    tab = jax.ShapeDtypeStruct((TAB_ROWS, LANES), jnp.int32)
    return pl.pallas_call(
        _plan_kernel,
        grid=(2, N_PAD // TM),
        in_specs=[pl.BlockSpec((TM, LANES), lambda ph, i: (i, 0))],
        out_specs=[
            pl.BlockSpec((TM, LANES), tile),
            pl.BlockSpec((TM, LANES), tile),
            pl.BlockSpec((LANES, TM), tile_t),
            _full((XBE_ROWS, LANES)), _full((XBE_ROWS, LANES)), _full((8, LANES)),
            _full((TAB_ROWS, LANES)), _full((TAB_ROWS, LANES)), _full((TAB_ROWS, LANES)), _full((TAB_ROWS, LANES)),
        ],
        out_shape=[
            jax.ShapeDtypeStruct((N_PAD, LANES), F32),
            jax.ShapeDtypeStruct((N_PAD, LANES), F32),
            jax.ShapeDtypeStruct((LANES, N_PAD), F32),
            jax.ShapeDtypeStruct((XBE_ROWS, LANES), jnp.int32),
            jax.ShapeDtypeStruct((XBE_ROWS, LANES), jnp.int32),
            jax.ShapeDtypeStruct((8, LANES), jnp.int32),
            tab, tab, tab, tab,
        ],
        scratch_shapes=[pltpu.VMEM((1, LANES), F32), pltpu.VMEM((1, LANES), F32)],
        compiler_params=pltpu.CompilerParams(
            dimension_semantics=("arbitrary", "arbitrary"), vmem_limit_bytes=VMEM_LIMIT),
        name="moe_plan",
    )(sm)


def _pack_rows(z):
    half = D_MODEL // 2
    lo = lax.bitcast_convert_type(z[:, :half], jnp.uint32) >> 16
    hi = lax.bitcast_convert_type(z[:, half:], jnp.uint32) & jnp.uint32(0xFFFF0000)
    return lax.bitcast_convert_type(hi | lo, jnp.int32)


def _unpack_rows(ref, rows=None):
    rows = ref.shape[0] if rows is None else rows
    flat = ref.reshape(ref.shape[0] * PACK, LANES)
    lo, hi = [], []
    for s in range(PACK):
        w = lax.bitcast_convert_type(flat[pl.ds(s, rows, stride=PACK), :], jnp.uint32)
        lo.append(lax.bitcast_convert_type(w << 16, F32))
        hi.append(lax.bitcast_convert_type(w & jnp.uint32(0xFFFF0000), F32))
    return jnp.concatenate(lo + hi, axis=-1).astype(BF16)


def _dispatch_kernel(stab_ref, ctab_ref, cmax_ref, xn_ref, destt_ref, xs_in, xs_ref,
                     stage0, stage1, stage2, sems, sem2):
    del xs_in
    i = pl.program_id(0)
    last = pl.num_programs(0) - 1
    x = xn_ref[...]
    dt = destt_ref[...]
    rio = lax.broadcasted_iota(jnp.int32, (DISP_CHUNK, 1), 0).astype(F32)

    def chunk_rows(j, stage):
        parts = []
        for e in range(N_EXPERTS):
            first = (stab_ref[i * N_EXPERTS + e] + j * DISP_CHUNK).astype(F32)
            parts.append(jnp.where(dt[e:e + 1, :] == first + rio, 1.0, 0.0).astype(BF16))
        onehot = jnp.concatenate(parts, axis=0)
        words = _pack_rows(jnp.dot(onehot, x, preferred_element_type=F32))
        for s in range(PACK):
            stage[pl.ds(s, N_EXPERTS * DISP_CHUNK, stride=PACK), :] = words[:, s * LANES:(s + 1) * LANES]

    def copy(stage, step, e, j, sem):
        first = stab_ref[step * N_EXPERTS + e] + j * DISP_CHUNK
        rows = stage.reshape(N_EXPERTS * DISP_CHUNK, PACK, LANES)
        return pltpu.make_async_copy(rows.at[pl.ds(e * DISP_CHUNK, DISP_CHUNK)],
                                     xs_ref.at[pl.ds(first, DISP_CHUNK)], sem)

    def step_body(stage, prev_stage, par):
        chunk_rows(0, stage)

        @pl.when(i > 0)
        def _():
            for e in range(N_EXPERTS):
                copy(prev_stage, i - 1, e, 0, sems.at[1 - par]).wait()

        for e in range(N_EXPERTS):
            copy(stage, i, e, 0, sems.at[par]).start()

        @pl.when(i == last)
        def _():
            for e in range(N_EXPERTS):
                copy(stage, i, e, 0, sems.at[par]).wait()

    @pl.when(i % 2 == 0)
    def _():
        step_body(stage0, stage1, 0)

    @pl.when(i % 2 == 1)
    def _():
        step_body(stage1, stage0, 1)

    for j in range(1, TOK_TILE // DISP_CHUNK):

        @pl.when(cmax_ref[i] > j * DISP_CHUNK)
        def _(j=j):
            chunk_rows(j, stage2)
            for e in range(N_EXPERTS):

                @pl.when(ctab_ref[i * N_EXPERTS + e] > j * DISP_CHUNK)
                def _(e=e):
                    cp = copy(stage2, i, e, j, sem2)
                    cp.start()
                    cp.wait()


def _dispatch_call(stab, ctab, cmax, xn, destt, xs_zero):
    stage = pltpu.VMEM((N_EXPERTS * DISP_CHUNK * PACK, LANES), jnp.int32)
    grid_spec = pltpu.PrefetchScalarGridSpec(
        num_scalar_prefetch=3,
        grid=(N_PAD_TILES,),
        in_specs=[
            pl.BlockSpec((TOK_TILE, D_MODEL), lambda i, *_: (i, 0)),
            pl.BlockSpec((N_EXPERTS, TOK_TILE), lambda i, *_: (0, i)),
            pl.BlockSpec(memory_space=pl.ANY),
        ],
        out_specs=pl.BlockSpec(memory_space=pl.ANY),
        scratch_shapes=[stage, stage, stage, pltpu.SemaphoreType.DMA((2,)), pltpu.SemaphoreType.DMA],
    )
    return pl.pallas_call(
        _dispatch_kernel,
        grid_spec=grid_spec,
        out_shape=jax.ShapeDtypeStruct((XS_ROWS, PACK, LANES), jnp.int32),
        input_output_aliases={5: 0},
        compiler_params=pltpu.CompilerParams(
            dimension_semantics=("arbitrary",), vmem_limit_bytes=VMEM_LIMIT),
        name="moe_dispatch",
    )(stab, ctab, cmax, xn, destt, xs_zero)


def _expert_kernel(blke_ref, nblk_ref, live_ref, xs_ref, wgu_hbm, bgu_ref, wdn_hbm, bdn_ref,
                   ys_ref, wgu_f, wdn_f, wgu_s, wdn_s, sems):
    b = pl.program_id(0)
    used = b < nblk_ref[0]
    prev = blke_ref[jnp.maximum(b - 1, 0)]
    fresh = used & ((b == 0) | (blke_ref[b] != prev))

    def fetch(e):
        return (pltpu.make_async_copy(wgu_hbm.at[e], wgu_f, sems.at[0]),
                pltpu.make_async_copy(wdn_hbm.at[e], wdn_f, sems.at[1]))

    @pl.when(b == 0)
    def _():
        for cp in fetch(blke_ref[0]):
            cp.start()

    @pl.when(fresh)
    def _():
        for cp in fetch(blke_ref[b]):
            cp.wait()
        wgu_s[...] = wgu_f[...].astype(BF16)
        wdn_s[...] = wdn_f[...].astype(BF16)

        nxt = lax.while_loop(lambda p: (p < nblk_ref[0]) & (blke_ref[jnp.minimum(p, N_XBLOCKS - 1)] == blke_ref[b]),
                             lambda p: p + 1, b + 1)

        @pl.when(nxt < nblk_ref[0])
        def _():
            for cp in fetch(blke_ref[jnp.minimum(nxt, N_XBLOCKS - 1)]):
                cp.start()

    for groups in range(1, EXP_BLOCK // ROW_GROUP + 1):
        rows = groups * ROW_GROUP

        @pl.when(used & (live_ref[b] == groups))
        def _(rows=rows):
            hid = jnp.dot(_unpack_rows(xs_ref, rows), wgu_s[...], preferred_element_type=F32) + bgu_ref[...]
            gate = jnp.minimum(hid[:, :D_FF], SWIGLU_LIMIT)
            up = jnp.clip(hid[:, D_FF:], -SWIGLU_LIMIT, SWIGLU_LIMIT)
            act = (up + 1.0) * gate * jax.nn.sigmoid(SWIGLU_ALPHA * gate)
            y = jnp.dot(act.astype(BF16), wdn_s[...], preferred_element_type=F32) + bdn_ref[...]
            ys_ref[0:rows, :] = y.astype(BF16)
            if rows < EXP_BLOCK:
                ys_ref[rows:, :] = jnp.zeros((EXP_BLOCK - rows, D_MODEL), BF16)

    @pl.when(jnp.logical_not(used) | (live_ref[b] == 0))
    def _():
        ys_ref[...] = jnp.zeros_like(ys_ref)


def _expert_call(blke, nblk, live, xs, wgu, bgu, wdn, bdn):
    grid_spec = pltpu.PrefetchScalarGridSpec(
        num_scalar_prefetch=3,
        grid=(N_XBLOCKS,),
        in_specs=[
            pl.BlockSpec((EXP_BLOCK, PACK, LANES), lambda b, be, *_: (b, 0, 0)),
            pl.BlockSpec(memory_space=pl.ANY),
            pl.BlockSpec((None, 1, 2 * D_FF), lambda b, be, *_: (be[b], 0, 0)),
            pl.BlockSpec(memory_space=pl.ANY),
            pl.BlockSpec((None, 1, D_MODEL), lambda b, be, *_: (be[b], 0, 0)),
        ],
        out_specs=pl.BlockSpec((EXP_BLOCK, D_MODEL), lambda b, be, *_: (b, 0)),
        scratch_shapes=[pltpu.VMEM((D_MODEL, 2 * D_FF), F32), pltpu.VMEM((D_FF, D_MODEL), F32),
                        pltpu.VMEM((D_MODEL, 2 * D_FF), BF16), pltpu.VMEM((D_FF, D_MODEL), BF16),
                        pltpu.SemaphoreType.DMA((2,))],
    )
    return pl.pallas_call(
        _expert_kernel,
        grid_spec=grid_spec,
        out_shape=jax.ShapeDtypeStruct((N_ROWS, D_MODEL), BF16),
        compiler_params=pltpu.CompilerParams(
            dimension_semantics=("arbitrary",), vmem_limit_bytes=VMEM_LIMIT),
        name="moe_experts",
    )(blke, nblk, live, xs, wgu, bgu, wdn, bdn)


def _combine_kernel(astart_ref, nwin_ref, over_ref,
                    destm_ref, posm_ref, gm_ref, rep_ref, h1_ref, plep_ref, ples_ref, gple_ref, wpg_ref, wpp_ref,
                    gfin_ref, ys_any, yp_ref, ysm_ref, wins, moe_s, tmp_s, wsems, sem):
    i = pl.program_id(0)
    slot = i % 2

    def win_copy(step, e, into):
        first = pl.multiple_of(astart_ref[step * N_EXPERTS + e], WIN_ALIGN)
        return pltpu.make_async_copy(ys_any.at[pl.ds(first, WIN)], wins.at[into, pl.ds(e * WIN, WIN)],
                                     wsems.at[into])

    @pl.when(i == 0)
    def _():
        for e in range(N_EXPERTS):
            win_copy(0, e, 0).start()

    @pl.when(i + 1 < pl.num_programs(0))
    def _():
        for e in range(N_EXPERTS):
            win_copy(i + 1, e, 1 - slot).start()

    for e in range(N_EXPERTS):
        win_copy(i, e, slot).wait()

    gates = gm_ref[...]
    rep = rep_ref[...]
    pos_rep = jnp.dot(posm_ref[...].astype(BF16), rep, preferred_element_type=F32)
    gate_rep = jnp.dot(gates.astype(BF16), rep, preferred_element_type=F32)
    within = (_lane_iota((TOK_TILE, N_EXPERTS * WIN)) % WIN).astype(F32)
    gsel = jnp.where(pos_rep == within, gate_rep, 0.0).astype(BF16)
    mrows = pl.ds(pl.multiple_of((i % TAIL_TILES) * TOK_TILE, TOK_TILE), TOK_TILE)
    moe_s[mrows, :] = jnp.dot(gsel, wins[slot], preferred_element_type=F32)

    @pl.when(over_ref[i] > 0)
    def _():
        dest = destm_ref[...]
        lane = _lane_iota((TOK_TILE, LANES))
        lane_f = lane.astype(F32)
        tmp_s[...] = jnp.zeros_like(tmp_s)

        def per_expert(e, carry):
            a = astart_ref[i * N_EXPERTS + e]
            dcol = jnp.sum(jnp.where(lane == e, dest, 0.0), axis=-1, keepdims=True)
            gcol = jnp.sum(jnp.where(lane == e, gates, 0.0), axis=-1, keepdims=True)

            def per_window(w, carry2):
                first = a + w * WIN
                start = pl.multiple_of(jnp.minimum(first, N_ROWS - WIN), WIN_ALIGN)
                cp = pltpu.make_async_copy(ys_any.at[pl.ds(start, WIN)], tmp_s.at[pl.ds(0, WIN)], sem)
                cp.start()
                cp.wait()
                hit = (lane < WIN) & (dcol == start.astype(F32) + lane_f) & (dcol >= first.astype(F32))
                sel = jnp.where(hit, gcol, 0.0).astype(BF16)
                moe_s[mrows, :] += jnp.dot(sel, tmp_s[...], preferred_element_type=F32)
                return carry2

            return lax.fori_loop(1, nwin_ref[i * N_EXPERTS + e], per_window, carry)

        lax.fori_loop(0, N_EXPERTS, per_expert, 0)

    def tail(h1, moe_rows, ple):
        h2 = h1 + moe_rows
        hn = _rms(h2, gple_ref[...]).astype(BF16)
        gate = jax.nn.sigmoid(jnp.dot(hn, wpg_ref[...], preferred_element_type=F32))
        proj = jnp.dot(ple.astype(BF16), wpp_ref[...], preferred_element_type=F32)
        return _rms(h2 + gate * proj, gfin_ref[...])

    is_sample = i == N_CTILES - 1

    @pl.when((i % TAIL_TILES == TAIL_TILES - 1) & jnp.logical_not(is_sample))
    def _():
        yp_ref[...] = tail(h1_ref[...], moe_s[...], plep_ref[...])

    @pl.when(is_sample)
    def _():
        ysm_ref[...] = tail(h1_ref[0:DEC_BATCH, :], moe_s[0:DEC_BATCH, :], ples_ref[...])


def _combine_call(astart, nwin, over, ys, destm, posm, gm, h1, plep, ples, gple, wpg, wpp, gfin):
    rep = (jnp.arange(N_EXPERTS * WIN)[None, :] // WIN == jnp.arange(LANES)[:, None]).astype(BF16)
    rows = TAIL_TILES * TOK_TILE
    last = N_PROMPT // rows - 1

    grid_spec = pltpu.PrefetchScalarGridSpec(
        num_scalar_prefetch=3,
        grid=(N_CTILES,),
        in_specs=[
            pl.BlockSpec((TOK_TILE, LANES), lambda i, *_: (i, 0)),
            pl.BlockSpec((TOK_TILE, LANES), lambda i, *_: (i, 0)),
            pl.BlockSpec((TOK_TILE, LANES), lambda i, *_: (i, 0)),
            pl.BlockSpec((LANES, N_EXPERTS * WIN), lambda i, *_: (0, 0)),
            pl.BlockSpec((rows, D_MODEL), lambda i, *_: (i // TAIL_TILES, 0)),
            pl.BlockSpec((rows, PLE_DIM), lambda i, *_: (jnp.minimum(i // TAIL_TILES, last), 0)),
            pl.BlockSpec((DEC_BATCH, PLE_DIM), lambda i, *_: (0, 0)),
            pl.BlockSpec((1, D_MODEL), lambda i, *_: (0, 0)),
            pl.BlockSpec((D_MODEL, D_MODEL), lambda i, *_: (0, 0)),
            pl.BlockSpec((PLE_DIM, D_MODEL), lambda i, *_: (0, 0)),
            pl.BlockSpec((1, D_MODEL), lambda i, *_: (0, 0)),
            pl.BlockSpec(memory_space=pl.ANY),
        ],
        out_specs=[
            pl.BlockSpec((rows, D_MODEL), lambda i, *_: (jnp.minimum(i // TAIL_TILES, last), 0)),
            pl.BlockSpec((DEC_BATCH, D_MODEL), lambda i, *_: (0, 0)),
        ],
        scratch_shapes=[pltpu.VMEM((2, N_EXPERTS * WIN, D_MODEL), BF16), pltpu.VMEM((rows, D_MODEL), F32),
                        pltpu.VMEM((2 * WIN, D_MODEL), BF16), pltpu.SemaphoreType.DMA((2,)),
                        pltpu.SemaphoreType.DMA],
    )
    return pl.pallas_call(
        _combine_kernel,
        grid_spec=grid_spec,
        out_shape=[jax.ShapeDtypeStruct((N_PROMPT, D_MODEL), F32),
                   jax.ShapeDtypeStruct((DEC_BATCH, D_MODEL), F32)],
        compiler_params=pltpu.CompilerParams(
            dimension_semantics=("arbitrary",), vmem_limit_bytes=VMEM_LIMIT),
        name="moe_combine_tail",
    )(astart, nwin, over, destm, posm, gm, rep, h1, plep, ples, gple, wpg, wpp, gfin, ys)


def kernel(x_prompt, x_sample, cache_swa_k, cache_swa_v, p_prompt, p_sample, g_mix, w_in, ln_v_g, ln_v_b,
           w_sp, b_sp, sinks, g_out_a, g_out_b, w_o, g_moe, w_router, b_router, w_gu, b_gu, w_dn, b_dn,
           g_ple, w_ple_gate, w_ple_proj, g_final):
    l = 0
    row = lambda v: v.reshape(1, -1)
    win = w_in[l].astype(BF16)
    wo = w_o[l].astype(BF16)
    tril = jnp.tril(jnp.ones((CHUNK, CHUNK), dtype=bool))
    wsp = jnp.where(tril, w_sp[l], 0.0).astype(BF16)
    bsp = jnp.repeat(b_sp[l].T, HEAD_DIM, axis=1)
    w00 = row(jnp.repeat(w_sp[l][:, 0, 0], HEAD_DIM))
    b0 = row(jnp.repeat(b_sp[l][:, 0], HEAD_DIM))
    wr_hi = w_router[l].astype(BF16)
    wr_lo = (w_router[l] - wr_hi.astype(F32)).astype(BF16)
    wr = jnp.concatenate([wr_hi, wr_lo, jnp.zeros((D_MODEL, LANES - 2 * N_EXPERTS), BF16)], axis=1)
    br = row(jnp.concatenate([b_router[l], jnp.zeros((LANES - N_EXPERTS,), F32)]))
    common = (row(g_mix[l]), win, row(ln_v_g[l]), row(ln_v_b[l]))
    tail = (row(g_out_a[l]), row(g_out_b[l]), wo, row(g_moe[l]), wr, br)

    h1, xn, gm, sm, k_p, v_p = _prompt_call(
        x_prompt.reshape(N_PROMPT, D_MODEL), sinks[l], *common, wsp, bsp, *tail)
    ck = cache_swa_k[l].reshape(DEC_BATCH * CHUNK * 2, HEAD_DIM)
    cv = cache_swa_v[l].reshape(DEC_BATCH * CHUNK * 2, HEAD_DIM)
    h1, xn, gm, sm, k_s, v_s, va_s = _decode_call(
        x_sample.reshape(DEC_BATCH, D_MODEL), ck, cv, sinks[l], *common, w00, b0, *tail, h1, xn, gm, sm)

    destm, posm, destt, xbe, live, nxblk, stab, ctab, astart, nwin = _plan_call(sm)
    flat = lambda tab, n: tab[:n, :N_EXPERTS].reshape(-1)
    over = (jnp.max(nwin[:N_CTILES, :N_EXPERTS], axis=1) > 1).astype(jnp.int32)
    cmax = jnp.max(ctab[:N_PAD_TILES, :N_EXPERTS], axis=1)

    xs = _dispatch_call(flat(stab, N_PAD_TILES), flat(ctab, N_PAD_TILES), cmax, xn, destt[:N_EXPERTS],
                        jnp.zeros((XS_ROWS, PACK, LANES), jnp.int32))
    ys = _expert_call(xbe[:N_XBLOCKS, 0], nxblk[0, :1], live[:N_XBLOCKS, 0], xs,
                      w_gu[l], b_gu[l].reshape(N_EXPERTS, 1, 2 * D_FF), w_dn[l], b_dn[l].reshape(N_EXPERTS, 1, D_MODEL))
    y_p, y_s = _combine_call(
        flat(astart, N_CTILES), flat(nwin, N_CTILES), over, ys, destm, posm, gm, h1,
        p_prompt[l].reshape(N_PROMPT, PLE_DIM), p_sample[l].reshape(DEC_BATCH, PLE_DIM),
        row(g_ple[l]), w_ple_gate[l].astype(BF16), w_ple_proj[l].astype(BF16), row(g_final))

    kv5 = lambda a, n: a.reshape(1, n, CHUNK, 2, HEAD_DIM)
    win5 = lambda a: jnp.transpose(a.reshape(DEC_BATCH, 2, HEAD_DIM, CHUNK), (0, 3, 1, 2))[None]
    return (y_p.reshape(BATCH, SEQ, D_MODEL), y_s.reshape(DEC_BATCH, 1, D_MODEL),
            kv5(k_p, BATCH), kv5(v_p, BATCH), win5(k_s), win5(v_s),
            va_s.reshape(1, DEC_BATCH, 1, A_WIDTH))
```

```python
import math

import jax
import jax.numpy as jnp
from jax import lax
from jax.experimental import pallas as pl
from jax.experimental.pallas import tpu as pltpu

F32 = jnp.float32
BF16 = jnp.bfloat16

D_MODEL = 1024
BATCH = 4
SEQ = 4096
DEC_BATCH = 128
HEAD_DIM = 64
A_WIDTH = 512
B_WIDTH = 512
B_HEADS = 8
KV_WIDTH = 128
IN_WIDTH = 2 * A_WIDTH + B_WIDTH + 2 * KV_WIDTH
CHUNK = 128
N_EXPERTS = 32
TOP_K = 4
D_FF = 1024
SWIGLU_ALPHA = 1.702
SWIGLU_LIMIT = 7.0
PLE_DIM = 256
EPS = 1e-5

LANES = 128
VMEM_LIMIT = 56 * 1024 * 1024
N_PROMPT = BATCH * SEQ
N_TOK = N_PROMPT + DEC_BATCH
TM = 1024
N_PAD = ((N_TOK + TM - 1) // TM) * TM
TOK_TILE = 128
N_PAD_TILES = N_PAD // TOK_TILE
N_CTILES = N_PROMPT // TOK_TILE + 1
DEC_TILE = 16
EXP_BLOCK = 512
ROW_GROUP = 128
DISP_CHUNK = 32
N_XBLOCKS = (N_TOK * TOP_K + N_EXPERTS * (DISP_CHUNK + EXP_BLOCK - 1) + EXP_BLOCK - 1) // EXP_BLOCK
N_ROWS = N_XBLOCKS * EXP_BLOCK
XS_ROWS = N_ROWS + N_EXPERTS * DISP_CHUNK
XBE_ROWS = ((N_XBLOCKS + 7) // 8) * 8
TAB_ROWS = ((N_PAD_TILES + 7) // 8) * 8
PACK = D_MODEL // 2 // LANES
WIN = 64
WIN_ALIGN = 16
TAIL_TILES = 4
NEG = -1e30


def _rms(x, g):
    return x * lax.rsqrt(jnp.mean(x * x, axis=-1, keepdims=True) + EPS) * g


def _gelu(x):
    c = math.sqrt(2.0 / math.pi)
    return x * (0.5 * (1.0 + jnp.tanh(c * (x + 0.044715 * (x * x * x)))))


def _layernorm(x, g, b):
    mu = jnp.mean(x, axis=-1, keepdims=True)
    xc = x - mu
    return xc * lax.rsqrt(jnp.mean(xc * xc, axis=-1, keepdims=True) + EPS) * g + b


def _lane_iota(shape):
    return lax.broadcasted_iota(jnp.int32, shape, len(shape) - 1)


def _full(shape):
    n = len(shape)
    return pl.BlockSpec(shape, lambda *_: (0,) * n)


def _route(xn2, wr_ref, br_ref):
    m = xn2.shape[0]
    xh = xn2.astype(BF16)
    xl = (xn2 - xh.astype(F32)).astype(BF16)
    r = jnp.dot(jnp.concatenate([xh, xl], axis=0), wr_ref[...], preferred_element_type=F32)
    r = r[:m] + r[m:]
    lane = _lane_iota((m, LANES))
    lane_f = lane.astype(F32)
    logits = jnp.where(lane < N_EXPERTS, r + pltpu.roll(r, LANES - N_EXPERTS, 1) + br_ref[...], NEG)
    work = logits
    sel = jnp.zeros((m, LANES), F32)
    top = None
    z = None
    for _ in range(TOP_K):
        mx = jnp.max(work, axis=-1, keepdims=True)
        first = jnp.min(jnp.where(work == mx, lane_f, float(LANES)), axis=-1, keepdims=True)
        hit = lane_f == first
        sel = jnp.where(hit, 1.0, sel)
        work = jnp.where(hit, NEG, work)
        if top is None:
            top = mx
            z = jnp.ones_like(mx)
        else:
            z = z + jnp.exp(mx - top)
    gates = jnp.where(sel > 0.0, jnp.exp(logits - top) / z, 0.0)
    return gates, sel


def _prompt_kernel(sinks_ref, x_ref, gmix_ref, win_ref, lng_ref, lnb_ref, wsp_ref, bsp_ref,
                   goa_ref, gob_ref, wo_ref, gmoe_ref, wr_ref, br_ref,
                   h1_ref, xn_ref, gm_ref, sm_ref, k_ref, v_ref,
                   z_s, kv_s, cat_s):
    g = pl.program_id(0)

    @pl.when(g >= N_PROMPT // TM)
    def _():
        h1_ref[...] = jnp.zeros_like(h1_ref)
        xn_ref[...] = jnp.zeros_like(xn_ref)
        gm_ref[...] = jnp.zeros_like(gm_ref)
        sm_ref[...] = jnp.zeros_like(sm_ref)

    @pl.when(g < N_PROMPT // TM)
    def _():
        _prompt_tile(g % (SEQ // TM), sinks_ref, x_ref, gmix_ref, win_ref, lng_ref, lnb_ref, wsp_ref, bsp_ref,
                     goa_ref, gob_ref, wo_ref, gmoe_ref, wr_ref, br_ref,
                     h1_ref, xn_ref, gm_ref, sm_ref, k_ref, v_ref, z_s, kv_s, cat_s)


def _prompt_tile(j, sinks_ref, x_ref, gmix_ref, win_ref, lng_ref, lnb_ref, wsp_ref, bsp_ref,
                 goa_ref, gob_ref, wo_ref, gmoe_ref, wr_ref, br_ref,
                 h1_ref, xn_ref, gm_ref, sm_ref, k_ref, v_ref, z_s, kv_s, cat_s):
    @pl.when(j == 0)
    def _():
        kv_s[0:CHUNK, :] = jnp.zeros((CHUNK, 2 * KV_WIDTH), F32)

    xn = _rms(x_ref[...], gmix_ref[...]).astype(BF16)
    z_s[...] = jnp.dot(xn, win_ref[...], preferred_element_type=F32)
    kv_s[CHUNK:, :] = z_s[:, 2 * A_WIDTH + B_WIDTH:]

    lane = _lane_iota((CHUNK, LANES))
    lo = lane < HEAD_DIM
    lane2 = _lane_iota((2 * CHUNK, LANES))
    lo2 = lane2 < HEAD_DIM
    qi = lax.broadcasted_iota(jnp.int32, (CHUNK, CHUNK), 0)
    kc = lax.broadcasted_iota(jnp.int32, (CHUNK, CHUNK), 1)
    from_prev = kc > qi
    dist = jnp.where(from_prev, qi + CHUNK - kc, qi - kc).astype(F32)

    def chunk_body(c, carry):
        r0 = pl.multiple_of(c * CHUNK, CHUNK)
        rows = pl.ds(r0, CHUNK)
        u = _gelu(z_s[rows, 0:A_WIDTH])
        va = _layernorm(_gelu(z_s[rows, A_WIDTH:2 * A_WIDTH]), lng_ref[...], lnb_ref[...])
        vab = va.astype(BF16)
        slabs = []
        for p in range(A_WIDTH // LANES):
            slab = vab[:, p * LANES:(p + 1) * LANES]
            m0 = jnp.dot(wsp_ref[2 * p], slab, preferred_element_type=F32)
            m1 = jnp.dot(wsp_ref[2 * p + 1], slab, preferred_element_type=F32)
            slabs.append(jnp.where(lo, m0, m1))
        ya = u * (jnp.concatenate(slabs, axis=-1) + bsp_ref[...])
        ya_n = _rms(ya, goa_ref[...])
        k2 = kv_s[pl.ds(r0, 2 * CHUNK), 0:KV_WIDTH]
        v2 = kv_s[pl.ds(r0, 2 * CHUNK), KV_WIDTH:2 * KV_WIDTH]
        k2r = pltpu.roll(k2, HEAD_DIM, 1)
        v2r = pltpu.roll(v2, HEAD_DIM, 1)
        kd = (jnp.where(lo2, k2, k2r).astype(BF16), jnp.where(lo2, k2r, k2).astype(BF16))
        vd = (jnp.where(lo2, v2, v2r).astype(BF16), jnp.where(lo2, v2r, v2).astype(BF16))
        prev_ok = (j > 0) | (c > 0)
        masked = from_prev & jnp.logical_not(prev_ok)
        yb_slabs = []
        for kv in range(2):
            q0 = z_s[rows, 2 * A_WIDTH + (2 * kv) * LANES:2 * A_WIDTH + (2 * kv + 1) * LANES]
            q1 = z_s[rows, 2 * A_WIDTH + (2 * kv + 1) * LANES:2 * A_WIDTH + (2 * kv + 2) * LANES]
            lhs = jnp.concatenate([jnp.where(lo, q0, 0.0), jnp.where(lo, 0.0, q0),
                                   jnp.where(lo, q1, 0.0), jnp.where(lo, 0.0, q1)], axis=0).astype(BF16)
            s_all = lax.dot_general(lhs, kd[kv], (((1,), (1,)), ((), ())), preferred_element_type=F32)
            probs = []
            for i in range(4):
                h = 4 * kv + i
                slope = 2.0 ** (-(h + 1))
                sink = sinks_ref[h]
                sh = s_all[i * CHUNK:(i + 1) * CHUNK]
                s = jnp.where(from_prev, sh[:, :CHUNK], sh[:, CHUNK:]) * (HEAD_DIM ** -0.5) - slope * dist
                s = jnp.where(masked, NEG, s)
                mx = jnp.maximum(jnp.max(s, axis=-1, keepdims=True), sink)
                e = jnp.exp(s - mx)
                den = jnp.sum(e, axis=-1, keepdims=True) + jnp.exp(sink - mx)
                p = e * (1.0 / den)
                probs.append(jnp.concatenate([jnp.where(from_prev, p, 0.0), jnp.where(from_prev, 0.0, p)], axis=-1))
            pm = jnp.concatenate(probs, axis=0).astype(BF16)
            o = jnp.dot(pm, vd[kv], preferred_element_type=F32)
            yb_slabs.append(jnp.where(lo, o[0:CHUNK], o[CHUNK:2 * CHUNK]))
            yb_slabs.append(jnp.where(lo, o[2 * CHUNK:3 * CHUNK], o[3 * CHUNK:4 * CHUNK]))
        yb_n = _rms(jnp.concatenate(yb_slabs, axis=-1), gob_ref[...])
        cat_s[rows, 0:A_WIDTH] = ya_n.astype(BF16)
        cat_s[rows, A_WIDTH:] = yb_n.astype(BF16)
        return carry

    lax.fori_loop(0, TM // CHUNK, chunk_body, 0)

    kv_s[0:CHUNK, :] = kv_s[TM:TM + CHUNK, :]
    k_ref[...] = kv_s[TM:TM + CHUNK, 0:KV_WIDTH]
    v_ref[...] = kv_s[TM:TM + CHUNK, KV_WIDTH:]

    h1 = x_ref[...] + jnp.dot(cat_s[...], wo_ref[...], preferred_element_type=F32)
    h1_ref[...] = h1
    xn2 = _rms(h1, gmoe_ref[...])
    xn_ref[...] = xn2.astype(BF16)
    gates, sel = _route(xn2, wr_ref, br_ref)
    gm_ref[...] = gates
    sm_ref[...] = sel


def _prompt_call(x, sinks, gmix, win, lng, lnb, wsp, bsp, goa, gob, wo, gmoe, wr, br):
    real = N_PROMPT // TM
    row = lambda g: (g, 0)
    seq = lambda g: (jnp.minimum(g, real - 1) // (SEQ // TM), 0, 0)
    return pl.pallas_call(
        _prompt_kernel,
        grid=(N_PAD // TM,),
        in_specs=[
            pl.BlockSpec(memory_space=pltpu.SMEM),
            pl.BlockSpec((TM, D_MODEL), lambda g: (jnp.minimum(g, real - 1), 0)),
            _full((1, D_MODEL)), _full((D_MODEL, IN_WIDTH)), _full((1, A_WIDTH)), _full((1, A_WIDTH)),
            _full((8, CHUNK, CHUNK)), _full((CHUNK, A_WIDTH)), _full((1, A_WIDTH)), _full((1, B_WIDTH)),
            _full((D_MODEL, D_MODEL)), _full((1, D_MODEL)), _full((D_MODEL, LANES)), _full((1, LANES)),
        ],
        out_specs=[
            pl.BlockSpec((TM, D_MODEL), row),
            pl.BlockSpec((TM, D_MODEL), row),
            pl.BlockSpec((TM, LANES), row),
            pl.BlockSpec((TM, LANES), row),
            pl.BlockSpec((None, CHUNK, KV_WIDTH), seq),
            pl.BlockSpec((None, CHUNK, KV_WIDTH), seq),
        ],
        out_shape=[
            jax.ShapeDtypeStruct((N_PAD, D_MODEL), F32),
            jax.ShapeDtypeStruct((N_PAD, D_MODEL), BF16),
            jax.ShapeDtypeStruct((N_PAD, LANES), F32),
            jax.ShapeDtypeStruct((N_PAD, LANES), F32),
            jax.ShapeDtypeStruct((BATCH, CHUNK, KV_WIDTH), F32),
            jax.ShapeDtypeStruct((BATCH, CHUNK, KV_WIDTH), F32),
        ],
        scratch_shapes=[
            pltpu.VMEM((TM, IN_WIDTH), F32),
            pltpu.VMEM((TM + CHUNK, 2 * KV_WIDTH), F32),
            pltpu.VMEM((TM, D_MODEL), BF16),
        ],
        compiler_params=pltpu.CompilerParams(
            dimension_semantics=("arbitrary",), vmem_limit_bytes=VMEM_LIMIT),
        name="prompt_premoe",
    )(sinks, x, gmix, win, lng, lnb, wsp, bsp, goa, gob, wo, gmoe, wr, br)


def _decode_kernel(sinks_ref, x_ref, ck_ref, cv_ref, gmix_ref, win_ref, lng_ref, lnb_ref, w00_ref, b0_ref,
                   goa_ref, gob_ref, wo_ref, gmoe_ref, wr_ref, br_ref,
                   h1_in, xn_in, gm_in, sm_in,
                   h1_ref, xn_ref, gm_ref, sm_ref, nk_ref, nv_ref, va_ref,
                   q_s, kn_s, vn_s, ya_s, yb_s, qm_s, o_s):
    del h1_in, xn_in, gm_in, sm_in
    i = pl.program_id(0)
    t = DEC_TILE

    @pl.when(i == 0)
    def _():
        xn = _rms(x_ref[...], gmix_ref[...]).astype(BF16)
        z = jnp.dot(xn, win_ref[...], preferred_element_type=F32)
        u = _gelu(z[:, 0:A_WIDTH])
        va = _layernorm(_gelu(z[:, A_WIDTH:2 * A_WIDTH]), lng_ref[...], lnb_ref[...])
        va_ref[...] = va
        ya_s[...] = _rms(u * (w00_ref[...] * va + b0_ref[...]), goa_ref[...])
        q_s[...] = z[:, 2 * A_WIDTH:2 * A_WIDTH + B_WIDTH]
        kn_s[...] = z[:, 2 * A_WIDTH + B_WIDTH:2 * A_WIDTH + B_WIDTH + KV_WIDTH]
        vn_s[...] = z[:, 2 * A_WIDTH + B_WIDTH + KV_WIDTH:]

    rows = pl.ds(pl.multiple_of(i * t, t), t)
    q = q_s[rows, :]
    kn = kn_s[rows, :]
    vn = vn_s[rows, :]
    lane = _lane_iota((t, LANES))
    lo = lane < HEAD_DIM
    stacked = []
    for h in range(B_HEADS):
        qh = jnp.where(lo if h % 2 == 0 else jnp.logical_not(lo), q[:, (h // 2) * LANES:(h // 2 + 1) * LANES], 0.0)
        stacked.append(pltpu.roll(qh, HEAD_DIM, 1) if h % 2 != h // 4 else qh)
    qm_s[...] = jnp.concatenate(stacked, axis=0)
    hrow = lax.broadcasted_iota(jnp.int32, (B_HEADS, 1), 0)
    slope = jnp.zeros((B_HEADS, 1), F32)
    sink = jnp.zeros((B_HEADS, 1), F32)
    for h in range(B_HEADS):
        slope = jnp.where(hrow == h, 2.0 ** (-(h + 1)), slope)
        sink = jnp.where(hrow == h, sinks_ref[h], sink)
    pos = _lane_iota((B_HEADS, CHUNK))
    bias = slope * (CHUNK - pos).astype(F32)
    on_kv = (_lane_iota((B_HEADS, LANES)) < HEAD_DIM) == (hrow < B_HEADS // 2)
    last_pos = _lane_iota((KV_WIDTH, CHUNK)) == CHUNK - 1
    last_row = lax.broadcasted_iota(jnp.int32, (CHUNK, KV_WIDTH), 0) == CHUNK - 1
    for b in range(t):
        qb = qm_s[pl.ds(b, B_HEADS, stride=t), :]
        kt = ck_ref[b]
        vt = cv_ref[b]
        s_c = jnp.dot(qb.astype(BF16), kt.astype(BF16), preferred_element_type=F32) * (HEAD_DIM ** -0.5) - bias
        s_c = jnp.where(pos >= 1, s_c, NEG)
        s_n = jnp.sum(qb * kn[b:b + 1, :], axis=-1, keepdims=True) * (HEAD_DIM ** -0.5)
        mx = jnp.maximum(jnp.maximum(jnp.max(s_c, axis=-1, keepdims=True), s_n), sink)
        e_c = jnp.exp(s_c - mx)
        e_n = jnp.exp(s_n - mx)
        inv = 1.0 / (jnp.sum(e_c, axis=-1, keepdims=True) + e_n + jnp.exp(sink - mx))
        o = lax.dot_general((e_c * inv).astype(BF16), vt.astype(BF16), (((1,), (1,)), ((), ())),
                            preferred_element_type=F32)
        o = o + (e_n * inv) * vn[b:b + 1, :]
        o_s[pl.ds(b, B_HEADS, stride=t), :] = jnp.where(on_kv, o, 0.0)
        nk_ref[b] = jnp.where(last_pos, jnp.where(last_row, kn[b:b + 1, :], 0.0).T, pltpu.roll(kt, CHUNK - 1, 1))
        nv_ref[b] = jnp.where(last_pos, jnp.where(last_row, vn[b:b + 1, :], 0.0).T, pltpu.roll(vt, CHUNK - 1, 1))
    slabs = []
    for p in range(B_WIDTH // LANES):
        pair = []
        for h in (2 * p, 2 * p + 1):
            oh = o_s[h * t:(h + 1) * t, :]
            pair.append(pltpu.roll(oh, HEAD_DIM, 1) if h % 2 != h // 4 else oh)
        slabs.append(pair[0] + pair[1])
    yb_s[rows, :] = jnp.concatenate(slabs, axis=-1)

    @pl.when(i == pl.num_programs(0) - 1)
    def _():
        yb_n = _rms(yb_s[...], gob_ref[...])
        cat = jnp.concatenate([ya_s[...], yb_n], axis=-1).astype(BF16)
        h1 = x_ref[...] + jnp.dot(cat, wo_ref[...], preferred_element_type=F32)
        xn2 = _rms(h1, gmoe_ref[...])
        gates, sel = _route(xn2, wr_ref, br_ref)
        h1_ref[...] = h1
        xn_ref[...] = xn2.astype(BF16)
        gm_ref[...] = gates
        sm_ref[...] = sel


def _decode_call(x, ck, cv, sinks, gmix, win, lng, lnb, w00, b0, goa, gob, wo, gmoe, wr, br, h1, xn, gm, sm):
    t = DEC_TILE
    cache = pl.BlockSpec((t, KV_WIDTH, CHUNK), lambda i: (i, 0, 0))
    tok = lambda width: pl.BlockSpec((DEC_BATCH, width), lambda i: (N_PROMPT // DEC_BATCH, 0))
    anyspec = pl.BlockSpec(memory_space=pl.ANY)
    return pl.pallas_call(
        _decode_kernel,
        grid=(DEC_BATCH // t,),
        in_specs=[
            pl.BlockSpec(memory_space=pltpu.SMEM),
            _full((DEC_BATCH, D_MODEL)), cache, cache,
            _full((1, D_MODEL)), _full((D_MODEL, IN_WIDTH)), _full((1, A_WIDTH)), _full((1, A_WIDTH)),
            _full((1, A_WIDTH)), _full((1, A_WIDTH)), _full((1, A_WIDTH)), _full((1, B_WIDTH)),
            _full((D_MODEL, D_MODEL)), _full((1, D_MODEL)), _full((D_MODEL, LANES)), _full((1, LANES)),
            anyspec, anyspec, anyspec, anyspec,
        ],
        out_specs=[tok(D_MODEL), tok(D_MODEL), tok(LANES), tok(LANES), cache, cache, _full((DEC_BATCH, A_WIDTH))],
        out_shape=[
            jax.ShapeDtypeStruct((N_PAD, D_MODEL), F32),
            jax.ShapeDtypeStruct((N_PAD, D_MODEL), BF16),
            jax.ShapeDtypeStruct((N_PAD, LANES), F32),
            jax.ShapeDtypeStruct((N_PAD, LANES), F32),
            jax.ShapeDtypeStruct((DEC_BATCH, KV_WIDTH, CHUNK), F32),
            jax.ShapeDtypeStruct((DEC_BATCH, KV_WIDTH, CHUNK), F32),
            jax.ShapeDtypeStruct((DEC_BATCH, A_WIDTH), F32),
        ],
        scratch_shapes=[pltpu.VMEM((DEC_BATCH, B_WIDTH), F32), pltpu.VMEM((DEC_BATCH, KV_WIDTH), F32),
                        pltpu.VMEM((DEC_BATCH, KV_WIDTH), F32), pltpu.VMEM((DEC_BATCH, A_WIDTH), F32),
                        pltpu.VMEM((DEC_BATCH, B_WIDTH), F32), pltpu.VMEM((B_HEADS * t, LANES), F32),
                        pltpu.VMEM((B_HEADS * t, LANES), F32)],
        input_output_aliases={16: 0, 17: 1, 18: 2, 19: 3},
        compiler_params=pltpu.CompilerParams(
            dimension_semantics=("arbitrary",), vmem_limit_bytes=VMEM_LIMIT),
        name="sample_premoe",
    )(sinks, x, ck, cv, gmix, win, lng, lnb, w00, b0, goa, gob, wo, gmoe, wr, br, h1, xn, gm, sm)


def _plan_kernel(sm_ref,
                 destm_ref, posm_ref, destt_ref, xbe_ref, live_ref, nxblk_ref, stab_ref, ctab_ref, astart_ref,
                 nwin_ref, base_s, pstart_s):
    ph = pl.program_id(0)
    step = pl.program_id(1)
    lane = _lane_iota((1, LANES))

    @pl.when((ph == 0) & (step == 0))
    def _():
        base_s[...] = jnp.zeros_like(base_s)

    @pl.when(ph == 0)
    def _():
        base_s[...] += jnp.sum(sm_ref[...], axis=0, keepdims=True)

    @pl.when((ph == 1) & (step == 0))
    def _():
        counts = base_s[...]
        padded = jnp.floor((counts + (DISP_CHUNK + EXP_BLOCK - 1)) * (1.0 / EXP_BLOCK)) * EXP_BLOCK
        padded = jnp.where(counts > 0.0, padded, 0.0)
        pend = padded
        for s in (1, 2, 4, 8, 16):
            pend = pend + jnp.where(lane >= s, pltpu.roll(pend, s, 1), 0.0)
        spare = (N_ROWS + lane * DISP_CHUNK).astype(F32)
        pstart_s[...] = jnp.where(counts > 0.0, pend - padded, spare)
        base_s[...] = jnp.zeros_like(base_s)
        brow = lax.broadcasted_iota(jnp.int32, (XBE_ROWS, LANES), 0).astype(F32) * EXP_BLOCK
        done = jnp.where((lane < N_EXPERTS) & (pend <= brow), 1.0, 0.0)
        be = jnp.minimum(jnp.sum(done, axis=-1, keepdims=True), N_EXPERTS - 1.0)
        xbe_ref[...] = jnp.broadcast_to(be, (XBE_ROWS, LANES)).astype(jnp.int32)
        real = jnp.clip(counts - (brow - (pend - padded)), 0.0, float(EXP_BLOCK))
        real = jnp.sum(jnp.where(lane.astype(F32) == be, real, 0.0), axis=-1, keepdims=True)
        groups = jnp.floor((real + (ROW_GROUP - 1)) * (1.0 / ROW_GROUP))
        live_ref[...] = jnp.broadcast_to(groups, (XBE_ROWS, LANES)).astype(jnp.int32)
        total = jnp.sum(jnp.where(lane == N_EXPERTS - 1, pend, 0.0), axis=-1, keepdims=True)
        nxblk_ref[...] = jnp.broadcast_to(total * (1.0 / EXP_BLOCK), (8, LANES)).astype(jnp.int32)
        stab_ref[...] = jnp.zeros_like(stab_ref)
        ctab_ref[...] = jnp.zeros_like(ctab_ref)
        astart_ref[...] = jnp.zeros_like(astart_ref)
        nwin_ref[...] = jnp.zeros_like(nwin_ref)

    @pl.when(ph == 1)
    def _():
        r = lax.broadcasted_iota(jnp.int32, (TOK_TILE, TOK_TILE), 0)
        c = lax.broadcasted_iota(jnp.int32, (TOK_TILE, TOK_TILE), 1)
        lower = jnp.where(c < r, 1.0, 0.0).astype(BF16)
        for q in range(TM // TOK_TILE):
            i = step * (TM // TOK_TILE) + q
            sel = sm_ref[q * TOK_TILE:(q + 1) * TOK_TILE, :]
            cnt = jnp.sum(sel, axis=0, keepdims=True)
            prefix = jnp.dot(lower, sel.astype(BF16), preferred_element_type=F32)
            start = pstart_s[...] + base_s[...]
            dest = jnp.where(sel > 0.0, prefix + start, -1.0)
            destm_ref[q * TOK_TILE:(q + 1) * TOK_TILE, :] = dest
            destt_ref[:, q * TOK_TILE:(q + 1) * TOK_TILE] = dest.T
            has = (cnt > 0.0) & (lane < N_EXPERTS)
            stab_ref[pl.ds(i, 1), :] = start.astype(jnp.int32)
            ctab_ref[pl.ds(i, 1), :] = jnp.where(has, cnt, 0.0).astype(jnp.int32)
            a = jnp.minimum(jnp.floor(start * (1.0 / WIN_ALIGN)) * WIN_ALIGN, float(N_ROWS - WIN))
            nw = jnp.where(has, jnp.floor((start + cnt - a + (WIN - 1)) * (1.0 / WIN)), 0.0)
            posm_ref[q * TOK_TILE:(q + 1) * TOK_TILE, :] = jnp.where(sel > 0.0, prefix + start - a, -1.0)
            astart_ref[pl.ds(i, 1), :] = a.astype(jnp.int32)
            nwin_ref[pl.ds(i, 1), :] = nw.astype(jnp.int32)
            base_s[...] += cnt


def _plan_call(sm):
    tile = lambda ph, i: (i * ph, 0)
    tile_t = lambda ph, i: (0, i * ph)
    tab = jax.ShapeDtypeStruct((TAB_ROWS, LANES), jnp.int32)
    return pl.pallas_call(
        _plan_kernel,
        grid=(2, N_PAD // TM),
        in_specs=[pl.BlockSpec((TM, LANES), lambda ph, i: (i, 0))],
        out_specs=[
            pl.BlockSpec((TM, LANES), tile),
            pl.BlockSpec((TM, LANES), tile),
            pl.BlockSpec((LANES, TM), tile_t),
            _full((XBE_ROWS, LANES)), _full((XBE_ROWS, LANES)), _full((8, LANES)),
            _full((TAB_ROWS, LANES)), _full((TAB_ROWS, LANES)), _full((TAB_ROWS, LANES)), _full((TAB_ROWS, LANES)),
        ],
        out_shape=[
            jax.ShapeDtypeStruct((N_PAD, LANES), F32),
            jax.ShapeDtypeStruct((N_PAD, LANES), F32),
            jax.ShapeDtypeStruct((LANES, N_PAD), F32),
            jax.ShapeDtypeStruct((XBE_ROWS, LANES), jnp.int32),
            jax.ShapeDtypeStruct((XBE_ROWS, LANES), jnp.int32),
            jax.ShapeDtypeStruct((8, LANES), jnp.int32),
            tab, tab, tab, tab,
        ],
        scratch_shapes=[pltpu.VMEM((1, LANES), F32), pltpu.VMEM((1, LANES), F32)],
        compiler_params=pltpu.CompilerParams(
            dimension_semantics=("arbitrary", "arbitrary"), vmem_limit_bytes=VMEM_LIMIT),
        name="moe_plan",
    )(sm)


def _pack_rows(z):
    half = D_MODEL // 2
    lo = lax.bitcast_convert_type(z[:, :half], jnp.uint32) >> 16
    hi = lax.bitcast_convert_type(z[:, half:], jnp.uint32) & jnp.uint32(0xFFFF0000)
    return lax.bitcast_convert_type(hi | lo, jnp.int32)


def _unpack_rows(ref, rows=None):
    rows = ref.shape[0] if rows is None else rows
    flat = ref.reshape(ref.shape[0] * PACK, LANES)
    lo, hi = [], []
    for s in range(PACK):
        w = lax.bitcast_convert_type(flat[pl.ds(s, rows, stride=PACK), :], jnp.uint32)
        lo.append(lax.bitcast_convert_type(w << 16, F32))
        hi.append(lax.bitcast_convert_type(w & jnp.uint32(0xFFFF0000), F32))
    return jnp.concatenate(lo + hi, axis=-1).astype(BF16)


def _dispatch_kernel(stab_ref, ctab_ref, cmax_ref, xn_ref, destt_ref, xs_in, xs_ref,
                     stage0, stage1, stage2, sems, sem2):
    del xs_in
    i = pl.program_id(0)
    last = pl.num_programs(0) - 1
    x = xn_ref[...]
    dt = destt_ref[...]
    rio = lax.broadcasted_iota(jnp.int32, (DISP_CHUNK, 1), 0).astype(F32)

    def chunk_rows(j, stage):
        parts = []
        for e in range(N_EXPERTS):
            first = (stab_ref[i * N_EXPERTS + e] + j * DISP_CHUNK).astype(F32)
            parts.append(jnp.where(dt[e:e + 1, :] == first + rio, 1.0, 0.0).astype(BF16))
        onehot = jnp.concatenate(parts, axis=0)
        words = _pack_rows(jnp.dot(onehot, x, preferred_element_type=F32))
        for s in range(PACK):
            stage[pl.ds(s, N_EXPERTS * DISP_CHUNK, stride=PACK), :] = words[:, s * LANES:(s + 1) * LANES]

    def copy(stage, step, e, j, sem):
        first = stab_ref[step * N_EXPERTS + e] + j * DISP_CHUNK
        rows = stage.reshape(N_EXPERTS * DISP_CHUNK, PACK, LANES)
        return pltpu.make_async_copy(rows.at[pl.ds(e * DISP_CHUNK, DISP_CHUNK)],
                                     xs_ref.at[pl.ds(first, DISP_CHUNK)], sem)

    def step_body(stage, prev_stage, par):
        chunk_rows(0, stage)

        @pl.when(i > 0)
        def _():
            for e in range(N_EXPERTS):
                copy(prev_stage, i - 1, e, 0, sems.at[1 - par]).wait()

        for e in range(N_EXPERTS):
            copy(stage, i, e, 0, sems.at[par]).start()

        @pl.when(i == last)
        def _():
            for e in range(N_EXPERTS):
                copy(stage, i, e, 0, sems.at[par]).wait()

    @pl.when(i % 2 == 0)
    def _():
        step_body(stage0, stage1, 0)

    @pl.when(i % 2 == 1)
    def _():
        step_body(stage1, stage0, 1)

    for j in range(1, TOK_TILE // DISP_CHUNK):

        @pl.when(cmax_ref[i] > j * DISP_CHUNK)
        def _(j=j):
            chunk_rows(j, stage2)
            for e in range(N_EXPERTS):

                @pl.when(ctab_ref[i * N_EXPERTS + e] > j * DISP_CHUNK)
                def _(e=e):
                    cp = copy(stage2, i, e, j, sem2)
                    cp.start()
                    cp.wait()


def _dispatch_call(stab, ctab, cmax, xn, destt, xs_zero):
    stage = pltpu.VMEM((N_EXPERTS * DISP_CHUNK * PACK, LANES), jnp.int32)
    grid_spec = pltpu.PrefetchScalarGridSpec(
        num_scalar_prefetch=3,
        grid=(N_PAD_TILES,),
        in_specs=[
            pl.BlockSpec((TOK_TILE, D_MODEL), lambda i, *_: (i, 0)),
            pl.BlockSpec((N_EXPERTS, TOK_TILE), lambda i, *_: (0, i)),
            pl.BlockSpec(memory_space=pl.ANY),
        ],
        out_specs=pl.BlockSpec(memory_space=pl.ANY),
        scratch_shapes=[stage, stage, stage, pltpu.SemaphoreType.DMA((2,)), pltpu.SemaphoreType.DMA],
    )
    return pl.pallas_call(
        _dispatch_kernel,
        grid_spec=grid_spec,
        out_shape=jax.ShapeDtypeStruct((XS_ROWS, PACK, LANES), jnp.int32),
        input_output_aliases={5: 0},
        compiler_params=pltpu.CompilerParams(
            dimension_semantics=("arbitrary",), vmem_limit_bytes=VMEM_LIMIT),
        name="moe_dispatch",
    )(stab, ctab, cmax, xn, destt, xs_zero)


def _expert_kernel(blke_ref, nblk_ref, live_ref, xs_ref, wgu_hbm, bgu_ref, wdn_hbm, bdn_ref,
                   ys_ref, wgu_f, wdn_f, wgu_s, wdn_s, sems):
    b = pl.program_id(0)
    used = b < nblk_ref[0]
    prev = blke_ref[jnp.maximum(b - 1, 0)]
    fresh = used & ((b == 0) | (blke_ref[b] != prev))

    def fetch(e):
        return (pltpu.make_async_copy(wgu_hbm.at[e], wgu_f, sems.at[0]),
                pltpu.make_async_copy(wdn_hbm.at[e], wdn_f, sems.at[1]))

    @pl.when(b == 0)
    def _():
        for cp in fetch(blke_ref[0]):
            cp.start()

    @pl.when(fresh)
    def _():
        for cp in fetch(blke_ref[b]):
            cp.wait()
        wgu_s[...] = wgu_f[...].astype(BF16)
        wdn_s[...] = wdn_f[...].astype(BF16)

        nxt = lax.while_loop(lambda p: (p < nblk_ref[0]) & (blke_ref[jnp.minimum(p, N_XBLOCKS - 1)] == blke_ref[b]),
                             lambda p: p + 1, b + 1)

        @pl.when(nxt < nblk_ref[0])
        def _():
            for cp in fetch(blke_ref[jnp.minimum(nxt, N_XBLOCKS - 1)]):
                cp.start()

    for groups in range(1, EXP_BLOCK // ROW_GROUP + 1):
        rows = groups * ROW_GROUP

        @pl.when(used & (live_ref[b] == groups))
        def _(rows=rows):
            hid = jnp.dot(_unpack_rows(xs_ref, rows), wgu_s[...], preferred_element_type=F32) + bgu_ref[...]
            gate = jnp.minimum(hid[:, :D_FF], SWIGLU_LIMIT)
            up = jnp.clip(hid[:, D_FF:], -SWIGLU_LIMIT, SWIGLU_LIMIT)
            act = (up + 1.0) * gate * jax.nn.sigmoid(SWIGLU_ALPHA * gate)
            y = jnp.dot(act.astype(BF16), wdn_s[...], preferred_element_type=F32) + bdn_ref[...]
            ys_ref[0:rows, :] = y.astype(BF16)
            if rows < EXP_BLOCK:
                ys_ref[rows:, :] = jnp.zeros((EXP_BLOCK - rows, D_MODEL), BF16)

    @pl.when(jnp.logical_not(used) | (live_ref[b] == 0))
    def _():
        ys_ref[...] = jnp.zeros_like(ys_ref)


def _expert_call(blke, nblk, live, xs, wgu, bgu, wdn, bdn):
    grid_spec = pltpu.PrefetchScalarGridSpec(
        num_scalar_prefetch=3,
        grid=(N_XBLOCKS,),
        in_specs=[
            pl.BlockSpec((EXP_BLOCK, PACK, LANES), lambda b, be, *_: (b, 0, 0)),
            pl.BlockSpec(memory_space=pl.ANY),
            pl.BlockSpec((None, 1, 2 * D_FF), lambda b, be, *_: (be[b], 0, 0)),
            pl.BlockSpec(memory_space=pl.ANY),
            pl.BlockSpec((None, 1, D_MODEL), lambda b, be, *_: (be[b], 0, 0)),
        ],
        out_specs=pl.BlockSpec((EXP_BLOCK, D_MODEL), lambda b, be, *_: (b, 0)),
        scratch_shapes=[pltpu.VMEM((D_MODEL, 2 * D_FF), F32), pltpu.VMEM((D_FF, D_MODEL), F32),
                        pltpu.VMEM((D_MODEL, 2 * D_FF), BF16), pltpu.VMEM((D_FF, D_MODEL), BF16),
                        pltpu.SemaphoreType.DMA((2,))],
    )
    return pl.pallas_call(
        _expert_kernel,
        grid_spec=grid_spec,
        out_shape=jax.ShapeDtypeStruct((N_ROWS, D_MODEL), BF16),
        compiler_params=pltpu.CompilerParams(
            dimension_semantics=("arbitrary",), vmem_limit_bytes=VMEM_LIMIT),
        name="moe_experts",
    )(blke, nblk, live, xs, wgu, bgu, wdn, bdn)


def _combine_kernel(astart_ref, nwin_ref, over_ref,
                    destm_ref, posm_ref, gm_ref, rep_ref, h1_ref, plep_ref, ples_ref, gple_ref, wpg_ref, wpp_ref,
                    gfin_ref, ys_any, yp_ref, ysm_ref, wins, moe_s, tmp_s, wsems, sem):
    i = pl.program_id(0)
    slot = i % 2

    def win_copy(step, e, into):
        first = pl.multiple_of(astart_ref[step * N_EXPERTS + e], WIN_ALIGN)
        return pltpu.make_async_copy(ys_any.at[pl.ds(first, WIN)], wins.at[into, pl.ds(e * WIN, WIN)],
                                     wsems.at[into])

    @pl.when(i == 0)
    def _():
        for e in range(N_EXPERTS):
            win_copy(0, e, 0).start()

    @pl.when(i + 1 < pl.num_programs(0))
    def _():
        for e in range(N_EXPERTS):
            win_copy(i + 1, e, 1 - slot).start()

    for e in range(N_EXPERTS):
        win_copy(i, e, slot).wait()

    gates = gm_ref[...]
    rep = rep_ref[...]
    pos_rep = jnp.dot(posm_ref[...].astype(BF16), rep, preferred_element_type=F32)
    gate_rep = jnp.dot(gates.astype(BF16), rep, preferred_element_type=F32)
    within = (_lane_iota((TOK_TILE, N_EXPERTS * WIN)) % WIN).astype(F32)
    gsel = jnp.where(pos_rep == within, gate_rep, 0.0).astype(BF16)
    mrows = pl.ds(pl.multiple_of((i % TAIL_TILES) * TOK_TILE, TOK_TILE), TOK_TILE)
    moe_s[mrows, :] = jnp.dot(gsel, wins[slot], preferred_element_type=F32)

    @pl.when(over_ref[i] > 0)
    def _():
        dest = destm_ref[...]
        lane = _lane_iota((TOK_TILE, LANES))
        lane_f = lane.astype(F32)
        tmp_s[...] = jnp.zeros_like(tmp_s)

        def per_expert(e, carry):
            a = astart_ref[i * N_EXPERTS + e]
            dcol = jnp.sum(jnp.where(lane == e, dest, 0.0), axis=-1, keepdims=True)
            gcol = jnp.sum(jnp.where(lane == e, gates, 0.0), axis=-1, keepdims=True)

            def per_window(w, carry2):
                first = a + w * WIN
                start = pl.multiple_of(jnp.minimum(first, N_ROWS - WIN), WIN_ALIGN)
                cp = pltpu.make_async_copy(ys_any.at[pl.ds(start, WIN)], tmp_s.at[pl.ds(0, WIN)], sem)
                cp.start()
                cp.wait()
                hit = (lane < WIN) & (dcol == start.astype(F32) + lane_f) & (dcol >= first.astype(F32))
                sel = jnp.where(hit, gcol, 0.0).astype(BF16)
                moe_s[mrows, :] += jnp.dot(sel, tmp_s[...], preferred_element_type=F32)
                return carry2

            return lax.fori_loop(1, nwin_ref[i * N_EXPERTS + e], per_window, carry)

        lax.fori_loop(0, N_EXPERTS, per_expert, 0)

    def tail(h1, moe_rows, ple):
        h2 = h1 + moe_rows
        hn = _rms(h2, gple_ref[...]).astype(BF16)
        gate = jax.nn.sigmoid(jnp.dot(hn, wpg_ref[...], preferred_element_type=F32))
        proj = jnp.dot(ple.astype(BF16), wpp_ref[...], preferred_element_type=F32)
        return _rms(h2 + gate * proj, gfin_ref[...])

    is_sample = i == N_CTILES - 1

    @pl.when((i % TAIL_TILES == TAIL_TILES - 1) & jnp.logical_not(is_sample))
    def _():
        yp_ref[...] = tail(h1_ref[...], moe_s[...], plep_ref[...])

    @pl.when(is_sample)
    def _():
        ysm_ref[...] = tail(h1_ref[0:DEC_BATCH, :], moe_s[0:DEC_BATCH, :], ples_ref[...])


def _combine_call(astart, nwin, over, ys, destm, posm, gm, h1, plep, ples, gple, wpg, wpp, gfin):
    rep = (jnp.arange(N_EXPERTS * WIN)[None, :] // WIN == jnp.arange(LANES)[:, None]).astype(BF16)
    rows = TAIL_TILES * TOK_TILE
    last = N_PROMPT // rows - 1

    grid_spec = pltpu.PrefetchScalarGridSpec(
        num_scalar_prefetch=3,
        grid=(N_CTILES,),
        in_specs=[
            pl.BlockSpec((TOK_TILE, LANES), lambda i, *_: (i, 0)),
            pl.BlockSpec((TOK_TILE, LANES), lambda i, *_: (i, 0)),
            pl.BlockSpec((TOK_TILE, LANES), lambda i, *_: (i, 0)),
            pl.BlockSpec((LANES, N_EXPERTS * WIN), lambda i, *_: (0, 0)),
            pl.BlockSpec((rows, D_MODEL), lambda i, *_: (i // TAIL_TILES, 0)),
            pl.BlockSpec((rows, PLE_DIM), lambda i, *_: (jnp.minimum(i // TAIL_TILES, last), 0)),
            pl.BlockSpec((DEC_BATCH, PLE_DIM), lambda i, *_: (0, 0)),
            pl.BlockSpec((1, D_MODEL), lambda i, *_: (0, 0)),
            pl.BlockSpec((D_MODEL, D_MODEL), lambda i, *_: (0, 0)),
            pl.BlockSpec((PLE_DIM, D_MODEL), lambda i, *_: (0, 0)),
            pl.BlockSpec((1, D_MODEL), lambda i, *_: (0, 0)),
            pl.BlockSpec(memory_space=pl.ANY),
        ],
        out_specs=[
            pl.BlockSpec((rows, D_MODEL), lambda i, *_: (jnp.minimum(i // TAIL_TILES, last), 0)),
            pl.BlockSpec((DEC_BATCH, D_MODEL), lambda i, *_: (0, 0)),
        ],
        scratch_shapes=[pltpu.VMEM((2, N_EXPERTS * WIN, D_MODEL), BF16), pltpu.VMEM((rows, D_MODEL), F32),
                        pltpu.VMEM((2 * WIN, D_MODEL), BF16), pltpu.SemaphoreType.DMA((2,)),
                        pltpu.SemaphoreType.DMA],
    )
    return pl.pallas_call(
        _combine_kernel,
        grid_spec=grid_spec,
        out_shape=[jax.ShapeDtypeStruct((N_PROMPT, D_MODEL), F32),
                   jax.ShapeDtypeStruct((DEC_BATCH, D_MODEL), F32)],
        compiler_params=pltpu.CompilerParams(
            dimension_semantics=("arbitrary",), vmem_limit_bytes=VMEM_LIMIT),
        name="moe_combine_tail",
    )(astart, nwin, over, destm, posm, gm, rep, h1, plep, ples, gple, wpg, wpp, gfin, ys)


def kernel(x_prompt, x_sample, cache_swa_k, cache_swa_v, p_prompt, p_sample, g_mix, w_in, ln_v_g, ln_v_b,
           w_sp, b_sp, sinks, g_out_a, g_out_b, w_o, g_moe, w_router, b_router, w_gu, b_gu, w_dn, b_dn,
           g_ple, w_ple_gate, w_ple_proj, g_final):
    l = 0
    row = lambda v: v.reshape(1, -1)
    win = w_in[l].astype(BF16)
    wo = w_o[l].astype(BF16)
    tril = jnp.tril(jnp.ones((CHUNK, CHUNK), dtype=bool))
    wsp = jnp.where(tril, w_sp[l], 0.0).astype(BF16)
    bsp = jnp.repeat(b_sp[l].T, HEAD_DIM, axis=1)
    w00 = row(jnp.repeat(w_sp[l][:, 0, 0], HEAD_DIM))
    b0 = row(jnp.repeat(b_sp[l][:, 0], HEAD_DIM))
    wr_hi = w_router[l].astype(BF16)
    wr_lo = (w_router[l] - wr_hi.astype(F32)).astype(BF16)
    wr = jnp.concatenate([wr_hi, wr_lo, jnp.zeros((D_MODEL, LANES - 2 * N_EXPERTS), BF16)], axis=1)
    br = row(jnp.concatenate([b_router[l], jnp.zeros((LANES - N_EXPERTS,), F32)]))
    common = (row(g_mix[l]), win, row(ln_v_g[l]), row(ln_v_b[l]))
    tail = (row(g_out_a[l]), row(g_out_b[l]), wo, row(g_moe[l]), wr, br)

    h1, xn, gm, sm, k_p, v_p = _prompt_call(
        x_prompt.reshape(N_PROMPT, D_MODEL), sinks[l], *common, wsp, bsp, *tail)
    chan = lambda a: jnp.transpose(a, (0, 2, 3, 1)).reshape(DEC_BATCH, KV_WIDTH, CHUNK)
    ck = chan(cache_swa_k[l])
    cv = chan(cache_swa_v[l])
    h1, xn, gm, sm, k_s, v_s, va_s = _decode_call(
        x_sample.reshape(DEC_BATCH, D_MODEL), ck, cv, sinks[l], *common, w00, b0, *tail, h1, xn, gm, sm)

    destm, posm, destt, xbe, live, nxblk, stab, ctab, astart, nwin = _plan_call(sm)
    flat = lambda tab, n: tab[:n, :N_EXPERTS].reshape(-1)
    over = (jnp.max(nwin[:N_CTILES, :N_EXPERTS], axis=1) > 1).astype(jnp.int32)
    cmax = jnp.max(ctab[:N_PAD_TILES, :N_EXPERTS], axis=1)

    xs = _dispatch_call(flat(stab, N_PAD_TILES), flat(ctab, N_PAD_TILES), cmax, xn, destt[:N_EXPERTS],
                        jnp.zeros((XS_ROWS, PACK, LANES), jnp.int32))
    ys = _expert_call(xbe[:N_XBLOCKS, 0], nxblk[0, :1], live[:N_XBLOCKS, 0], xs,
                      w_gu[l], b_gu[l].reshape(N_EXPERTS, 1, 2 * D_FF), w_dn[l], b_dn[l].reshape(N_EXPERTS, 1, D_MODEL))
    y_p, y_s = _combine_call(
        flat(astart, N_CTILES), flat(nwin, N_CTILES), over, ys, destm, posm, gm, h1,
        p_prompt[l].reshape(N_PROMPT, PLE_DIM), p_sample[l].reshape(DEC_BATCH, PLE_DIM),
        row(g_ple[l]), w_ple_gate[l].astype(BF16), w_ple_proj[l].astype(BF16), row(g_final))

    kv5 = lambda a, n: a.reshape(1, n, CHUNK, 2, HEAD_DIM)
    win5 = lambda a: jnp.transpose(a.reshape(DEC_BATCH, 2, HEAD_DIM, CHUNK), (0, 3, 1, 2))[None]
    return (y_p.reshape(BATCH, SEQ, D_MODEL), y_s.reshape(DEC_BATCH, 1, D_MODEL),
            kv5(k_p, BATCH), kv5(v_p, BATCH), win5(k_s), win5(v_s),
            va_s.reshape(1, DEC_BATCH, 1, A_WIDTH))
```

```python
import math

import jax
import jax.numpy as jnp
from jax import lax
from jax.experimental import pallas as pl
from jax.experimental.pallas import tpu as pltpu

F32 = jnp.float32
BF16 = jnp.bfloat16

D_MODEL = 1024
BATCH = 4
SEQ = 4096
DEC_BATCH = 128
HEAD_DIM = 64
A_WIDTH = 512
B_WIDTH = 512
B_HEADS = 8
KV_WIDTH = 128
IN_WIDTH = 2 * A_WIDTH + B_WIDTH + 2 * KV_WIDTH
CHUNK = 128
N_EXPERTS = 32
TOP_K = 4
D_FF = 1024
SWIGLU_ALPHA = 1.702
SWIGLU_LIMIT = 7.0
PLE_DIM = 256
EPS = 1e-5

LANES = 128
VMEM_LIMIT = 56 * 1024 * 1024
N_PROMPT = BATCH * SEQ
N_TOK = N_PROMPT + DEC_BATCH
TM = 1024
N_PAD = ((N_TOK + TM - 1) // TM) * TM
TOK_TILE = 128
N_PAD_TILES = N_PAD // TOK_TILE
N_CTILES = N_PROMPT // TOK_TILE + 1
DEC_TILE = 16
EXP_BLOCK = 512
ROW_GROUP = 128
DISP_CHUNK = 32
N_XBLOCKS = (N_TOK * TOP_K + N_EXPERTS * (DISP_CHUNK + EXP_BLOCK - 1) + EXP_BLOCK - 1) // EXP_BLOCK
N_ROWS = N_XBLOCKS * EXP_BLOCK
XS_ROWS = N_ROWS + N_EXPERTS * DISP_CHUNK
XBE_ROWS = ((N_XBLOCKS + 7) // 8) * 8
TAB_ROWS = ((N_PAD_TILES + 7) // 8) * 8
PACK = D_MODEL // 2 // LANES
WIN = 64
WIN_ALIGN = 16
TAIL_TILES = 4
NEG = -1e30


def _rms(x, g):
    return x * lax.rsqrt(jnp.mean(x * x, axis=-1, keepdims=True) + EPS) * g


def _gelu(x):
    c = math.sqrt(2.0 / math.pi)
    return x * (0.5 * (1.0 + jnp.tanh(c * (x + 0.044715 * (x * x * x)))))


def _layernorm(x, g, b):
    mu = jnp.mean(x, axis=-1, keepdims=True)
    xc = x - mu
    return xc * lax.rsqrt(jnp.mean(xc * xc, axis=-1, keepdims=True) + EPS) * g + b


def _lane_iota(shape):
    return lax.broadcasted_iota(jnp.int32, shape, len(shape) - 1)


def _full(shape):
    n = len(shape)
    return pl.BlockSpec(shape, lambda *_: (0,) * n)


def _route(xn2, wr_ref, br_ref):
    m = xn2.shape[0]
    xh = xn2.astype(BF16)
    xl = (xn2 - xh.astype(F32)).astype(BF16)
    r = jnp.dot(jnp.concatenate([xh, xl], axis=0), wr_ref[...], preferred_element_type=F32)
    r = r[:m] + r[m:]
    lane = _lane_iota((m, LANES))
    lane_f = lane.astype(F32)
    logits = jnp.where(lane < N_EXPERTS, r + pltpu.roll(r, LANES - N_EXPERTS, 1) + br_ref[...], NEG)
    work = logits
    sel = jnp.zeros((m, LANES), F32)
    top = None
    z = None
    for _ in range(TOP_K):
        mx = jnp.max(work, axis=-1, keepdims=True)
        first = jnp.min(jnp.where(work == mx, lane_f, float(LANES)), axis=-1, keepdims=True)
        hit = lane_f == first
        sel = jnp.where(hit, 1.0, sel)
        work = jnp.where(hit, NEG, work)
        if top is None:
            top = mx
            z = jnp.ones_like(mx)
        else:
            z = z + jnp.exp(mx - top)
    gates = jnp.where(sel > 0.0, jnp.exp(logits - top) / z, 0.0)
    return gates, sel


def _prompt_kernel(sinks_ref, x_ref, gmix_ref, win_ref, lng_ref, lnb_ref, wsp_ref, bsp_ref,
                   goa_ref, gob_ref, wo_ref, gmoe_ref, wr_ref, br_ref,
                   h1_ref, xn_ref, gm_ref, sm_ref, k_ref, v_ref,
                   z_s, kv_s, cat_s):
    g = pl.program_id(0)

    @pl.when(g >= N_PROMPT // TM)
    def _():
        h1_ref[...] = jnp.zeros_like(h1_ref)
        xn_ref[...] = jnp.zeros_like(xn_ref)
        gm_ref[...] = jnp.zeros_like(gm_ref)
        sm_ref[...] = jnp.zeros_like(sm_ref)

    @pl.when(g < N_PROMPT // TM)
    def _():
        _prompt_tile(g % (SEQ // TM), sinks_ref, x_ref, gmix_ref, win_ref, lng_ref, lnb_ref, wsp_ref, bsp_ref,
                     goa_ref, gob_ref, wo_ref, gmoe_ref, wr_ref, br_ref,
                     h1_ref, xn_ref, gm_ref, sm_ref, k_ref, v_ref, z_s, kv_s, cat_s)


def _prompt_tile(j, sinks_ref, x_ref, gmix_ref, win_ref, lng_ref, lnb_ref, wsp_ref, bsp_ref,
                 goa_ref, gob_ref, wo_ref, gmoe_ref, wr_ref, br_ref,
                 h1_ref, xn_ref, gm_ref, sm_ref, k_ref, v_ref, z_s, kv_s, cat_s):
    @pl.when(j == 0)
    def _():
        kv_s[0:CHUNK, :] = jnp.zeros((CHUNK, 2 * KV_WIDTH), F32)

    xn = _rms(x_ref[...], gmix_ref[...]).astype(BF16)
    z_s[...] = jnp.dot(xn, win_ref[...], preferred_element_type=F32)
    kv_s[CHUNK:, :] = z_s[:, 2 * A_WIDTH + B_WIDTH:]

    lane = _lane_iota((CHUNK, LANES))
    lo = lane < HEAD_DIM
    lane2 = _lane_iota((2 * CHUNK, LANES))
    lo2 = lane2 < HEAD_DIM
    qi = lax.broadcasted_iota(jnp.int32, (CHUNK, CHUNK), 0)
    kc = lax.broadcasted_iota(jnp.int32, (CHUNK, CHUNK), 1)
    from_prev = kc > qi
    dist = jnp.where(from_prev, qi + CHUNK - kc, qi - kc).astype(F32)

    def chunk_body(c, carry):
        r0 = pl.multiple_of(c * CHUNK, CHUNK)
        rows = pl.ds(r0, CHUNK)
        u = _gelu(z_s[rows, 0:A_WIDTH])
        va = _layernorm(_gelu(z_s[rows, A_WIDTH:2 * A_WIDTH]), lng_ref[...], lnb_ref[...])
        vab = va.astype(BF16)
        slabs = []
        for p in range(A_WIDTH // LANES):
            slab = vab[:, p * LANES:(p + 1) * LANES]
            m0 = jnp.dot(wsp_ref[2 * p], slab, preferred_element_type=F32)
            m1 = jnp.dot(wsp_ref[2 * p + 1], slab, preferred_element_type=F32)
            slabs.append(jnp.where(lo, m0, m1))
        ya = u * (jnp.concatenate(slabs, axis=-1) + bsp_ref[...])
        ya_n = _rms(ya, goa_ref[...])
        k2 = kv_s[pl.ds(r0, 2 * CHUNK), 0:KV_WIDTH]
        v2 = kv_s[pl.ds(r0, 2 * CHUNK), KV_WIDTH:2 * KV_WIDTH]
        k2r = pltpu.roll(k2, HEAD_DIM, 1)
        v2r = pltpu.roll(v2, HEAD_DIM, 1)
        kd = (jnp.where(lo2, k2, k2r).astype(BF16), jnp.where(lo2, k2r, k2).astype(BF16))
        vd = (jnp.where(lo2, v2, v2r).astype(BF16), jnp.where(lo2, v2r, v2).astype(BF16))
        prev_ok = (j > 0) | (c > 0)
        masked = from_prev & jnp.logical_not(prev_ok)
        yb_slabs = []
        for kv in range(2):
            q0 = z_s[rows, 2 * A_WIDTH + (2 * kv) * LANES:2 * A_WIDTH + (2 * kv + 1) * LANES]
            q1 = z_s[rows, 2 * A_WIDTH + (2 * kv + 1) * LANES:2 * A_WIDTH + (2 * kv + 2) * LANES]
            lhs = jnp.concatenate([jnp.where(lo, q0, 0.0), jnp.where(lo, 0.0, q0),
                                   jnp.where(lo, q1, 0.0), jnp.where(lo, 0.0, q1)], axis=0).astype(BF16)
            s_all = lax.dot_general(lhs, kd[kv], (((1,), (1,)), ((), ())), preferred_element_type=F32)
            probs = []
            for i in range(4):
                h = 4 * kv + i
                slope = 2.0 ** (-(h + 1))
                sink = sinks_ref[h]
                sh = s_all[i * CHUNK:(i + 1) * CHUNK]
                s = jnp.where(from_prev, sh[:, :CHUNK], sh[:, CHUNK:]) * (HEAD_DIM ** -0.5) - slope * dist
                s = jnp.where(masked, NEG, s)
                mx = jnp.maximum(jnp.max(s, axis=-1, keepdims=True), sink)
                e = jnp.exp(s - mx)
                den = jnp.sum(e, axis=-1, keepdims=True) + jnp.exp(sink - mx)
                p = e * (1.0 / den)
                probs.append(jnp.concatenate([jnp.where(from_prev, p, 0.0), jnp.where(from_prev, 0.0, p)], axis=-1))
            pm = jnp.concatenate(probs, axis=0).astype(BF16)
            o = jnp.dot(pm, vd[kv], preferred_element_type=F32)
            yb_slabs.append(jnp.where(lo, o[0:CHUNK], o[CHUNK:2 * CHUNK]))
            yb_slabs.append(jnp.where(lo, o[2 * CHUNK:3 * CHUNK], o[3 * CHUNK:4 * CHUNK]))
        yb_n = _rms(jnp.concatenate(yb_slabs, axis=-1), gob_ref[...])
        cat_s[rows, 0:A_WIDTH] = ya_n.astype(BF16)
        cat_s[rows, A_WIDTH:] = yb_n.astype(BF16)
        return carry

    lax.fori_loop(0, TM // CHUNK, chunk_body, 0)

    kv_s[0:CHUNK, :] = kv_s[TM:TM + CHUNK, :]
    k_ref[...] = kv_s[TM:TM + CHUNK, 0:KV_WIDTH]
    v_ref[...] = kv_s[TM:TM + CHUNK, KV_WIDTH:]

    h1 = x_ref[...] + jnp.dot(cat_s[...], wo_ref[...], preferred_element_type=F32)
    h1_ref[...] = h1
    xn2 = _rms(h1, gmoe_ref[...])
    xn_ref[...] = xn2.astype(BF16)
    gates, sel = _route(xn2, wr_ref, br_ref)
    gm_ref[...] = gates
    sm_ref[...] = sel


def _prompt_call(x, sinks, gmix, win, lng, lnb, wsp, bsp, goa, gob, wo, gmoe, wr, br):
    real = N_PROMPT // TM
    row = lambda g: (g, 0)
    seq = lambda g: (jnp.minimum(g, real - 1) // (SEQ // TM), 0, 0)
    return pl.pallas_call(
        _prompt_kernel,
        grid=(N_PAD // TM,),
        in_specs=[
            pl.BlockSpec(memory_space=pltpu.SMEM),
            pl.BlockSpec((TM, D_MODEL), lambda g: (jnp.minimum(g, real - 1), 0)),
            _full((1, D_MODEL)), _full((D_MODEL, IN_WIDTH)), _full((1, A_WIDTH)), _full((1, A_WIDTH)),
            _full((8, CHUNK, CHUNK)), _full((CHUNK, A_WIDTH)), _full((1, A_WIDTH)), _full((1, B_WIDTH)),
            _full((D_MODEL, D_MODEL)), _full((1, D_MODEL)), _full((D_MODEL, LANES)), _full((1, LANES)),
        ],
        out_specs=[
            pl.BlockSpec((TM, D_MODEL), row),
            pl.BlockSpec((TM, D_MODEL), row),
            pl.BlockSpec((TM, LANES), row),
            pl.BlockSpec((TM, LANES), row),
            pl.BlockSpec((None, CHUNK, KV_WIDTH), seq),
            pl.BlockSpec((None, CHUNK, KV_WIDTH), seq),
        ],
        out_shape=[
            jax.ShapeDtypeStruct((N_PAD, D_MODEL), F32),
            jax.ShapeDtypeStruct((N_PAD, D_MODEL), BF16),
            jax.ShapeDtypeStruct((N_PAD, LANES), F32),
            jax.ShapeDtypeStruct((N_PAD, LANES), F32),
            jax.ShapeDtypeStruct((BATCH, CHUNK, KV_WIDTH), F32),
            jax.ShapeDtypeStruct((BATCH, CHUNK, KV_WIDTH), F32),
        ],
        scratch_shapes=[
            pltpu.VMEM((TM, IN_WIDTH), F32),
            pltpu.VMEM((TM + CHUNK, 2 * KV_WIDTH), F32),
            pltpu.VMEM((TM, D_MODEL), BF16),
        ],
        compiler_params=pltpu.CompilerParams(
            dimension_semantics=("arbitrary",), vmem_limit_bytes=VMEM_LIMIT),
        name="prompt_premoe",
    )(sinks, x, gmix, win, lng, lnb, wsp, bsp, goa, gob, wo, gmoe, wr, br)


def _decode_kernel(sinks_ref, x_ref, ck_ref, cv_ref, gmix_ref, win_ref, lng_ref, lnb_ref, w00_ref, b0_ref,
                   goa_ref, gob_ref, wo_ref, gmoe_ref, wr_ref, br_ref,
                   h1_in, xn_in, gm_in, sm_in,
                   h1_ref, xn_ref, gm_ref, sm_ref, nk_ref, nv_ref, va_ref,
                   q_s, kn_s, vn_s, ya_s, yb_s, qm_s, o_s):
    del h1_in, xn_in, gm_in, sm_in
    i = pl.program_id(0)
    t = DEC_TILE

    @pl.when(i == 0)
    def _():
        xn = _rms(x_ref[...], gmix_ref[...]).astype(BF16)
        z = jnp.dot(xn, win_ref[...], preferred_element_type=F32)
        u = _gelu(z[:, 0:A_WIDTH])
        va = _layernorm(_gelu(z[:, A_WIDTH:2 * A_WIDTH]), lng_ref[...], lnb_ref[...])
        va_ref[...] = va
        ya_s[...] = _rms(u * (w00_ref[...] * va + b0_ref[...]), goa_ref[...])
        q_s[...] = z[:, 2 * A_WIDTH:2 * A_WIDTH + B_WIDTH]
        kn_s[...] = z[:, 2 * A_WIDTH + B_WIDTH:2 * A_WIDTH + B_WIDTH + KV_WIDTH]
        vn_s[...] = z[:, 2 * A_WIDTH + B_WIDTH + KV_WIDTH:]

    rows = pl.ds(pl.multiple_of(i * t, t), t)
    q = q_s[rows, :]
    kn = kn_s[rows, :]
    vn = vn_s[rows, :]
    lane = _lane_iota((t, LANES))
    lo = lane < HEAD_DIM
    stacked = []
    for h in range(B_HEADS):
        qh = jnp.where(lo if h % 2 == 0 else jnp.logical_not(lo), q[:, (h // 2) * LANES:(h // 2 + 1) * LANES], 0.0)
        stacked.append(pltpu.roll(qh, HEAD_DIM, 1) if h % 2 != h // 4 else qh)
    qm_s[...] = jnp.concatenate(stacked, axis=0)
    hrow = lax.broadcasted_iota(jnp.int32, (B_HEADS, 1), 0)
    slope = jnp.zeros((B_HEADS, 1), F32)
    sink = jnp.zeros((B_HEADS, 1), F32)
    for h in range(B_HEADS):
        slope = jnp.where(hrow == h, 2.0 ** (-(h + 1)), slope)
        sink = jnp.where(hrow == h, sinks_ref[h], sink)
    pos = _lane_iota((B_HEADS, CHUNK))
    bias = slope * (CHUNK - pos).astype(F32)
    on_kv = (_lane_iota((B_HEADS, LANES)) < HEAD_DIM) == (hrow < B_HEADS // 2)
    last_pos = _lane_iota((KV_WIDTH, CHUNK)) == CHUNK - 1
    last_row = lax.broadcasted_iota(jnp.int32, (CHUNK, KV_WIDTH), 0) == CHUNK - 1
    for b in range(t):
        qb = qm_s[pl.ds(b, B_HEADS, stride=t), :]
        kt = ck_ref[b]
        vt = cv_ref[b]
        s_c = jnp.dot(qb.astype(BF16), kt.astype(BF16), preferred_element_type=F32) * (HEAD_DIM ** -0.5) - bias
        s_c = jnp.where(pos >= 1, s_c, NEG)
        s_n = jnp.sum(qb * kn[b:b + 1, :], axis=-1, keepdims=True) * (HEAD_DIM ** -0.5)
        mx = jnp.maximum(jnp.maximum(jnp.max(s_c, axis=-1, keepdims=True), s_n), sink)
        e_c = jnp.exp(s_c - mx)
        e_n = jnp.exp(s_n - mx)
        inv = 1.0 / (jnp.sum(e_c, axis=-1, keepdims=True) + e_n + jnp.exp(sink - mx))
        o = lax.dot_general((e_c * inv).astype(BF16), vt.astype(BF16), (((1,), (1,)), ((), ())),
                            preferred_element_type=F32)
        o = o + (e_n * inv) * vn[b:b + 1, :]
        o_s[pl.ds(b, B_HEADS, stride=t), :] = jnp.where(on_kv, o, 0.0)
        nk_ref[b] = jnp.where(last_pos, jnp.where(last_row, kn[b:b + 1, :], 0.0).T, pltpu.roll(kt, CHUNK - 1, 1))
        nv_ref[b] = jnp.where(last_pos, jnp.where(last_row, vn[b:b + 1, :], 0.0).T, pltpu.roll(vt, CHUNK - 1, 1))
    slabs = []
    for p in range(B_WIDTH // LANES):
        pair = []
        for h in (2 * p, 2 * p + 1):
            oh = o_s[h * t:(h + 1) * t, :]
            pair.append(pltpu.roll(oh, HEAD_DIM, 1) if h % 2 != h // 4 else oh)
        slabs.append(pair[0] + pair[1])
    yb_s[rows, :] = jnp.concatenate(slabs, axis=-1)

    @pl.when(i == pl.num_programs(0) - 1)
    def _():
        yb_n = _rms(yb_s[...], gob_ref[...])
        cat = jnp.concatenate([ya_s[...], yb_n], axis=-1).astype(BF16)
        h1 = x_ref[...] + jnp.dot(cat, wo_ref[...], preferred_element_type=F32)
        xn2 = _rms(h1, gmoe_ref[...])
        gates, sel = _route(xn2, wr_ref, br_ref)
        h1_ref[...] = h1
        xn_ref[...] = xn2.astype(BF16)
        gm_ref[...] = gates
        sm_ref[...] = sel


def _decode_call(x, ck, cv, sinks, gmix, win, lng, lnb, w00, b0, goa, gob, wo, gmoe, wr, br, h1, xn, gm, sm):
    t = DEC_TILE
    cache = pl.BlockSpec((t, KV_WIDTH, CHUNK), lambda i: (i, 0, 0))
    tok = lambda width: pl.BlockSpec((DEC_BATCH, width), lambda i: (N_PROMPT // DEC_BATCH, 0))
    anyspec = pl.BlockSpec(memory_space=pl.ANY)
    return pl.pallas_call(
        _decode_kernel,
        grid=(DEC_BATCH // t,),
        in_specs=[
            pl.BlockSpec(memory_space=pltpu.SMEM),
            _full((DEC_BATCH, D_MODEL)), cache, cache,
            _full((1, D_MODEL)), _full((D_MODEL, IN_WIDTH)), _full((1, A_WIDTH)), _full((1, A_WIDTH)),
            _full((1, A_WIDTH)), _full((1, A_WIDTH)), _full((1, A_WIDTH)), _full((1, B_WIDTH)),
            _full((D_MODEL, D_MODEL)), _full((1, D_MODEL)), _full((D_MODEL, LANES)), _full((1, LANES)),
            anyspec, anyspec, anyspec, anyspec,
        ],
        out_specs=[tok(D_MODEL), tok(D_MODEL), tok(LANES), tok(LANES), cache, cache, _full((DEC_BATCH, A_WIDTH))],
        out_shape=[
            jax.ShapeDtypeStruct((N_PAD, D_MODEL), F32),
            jax.ShapeDtypeStruct((N_PAD, D_MODEL), BF16),
            jax.ShapeDtypeStruct((N_PAD, LANES), F32),
            jax.ShapeDtypeStruct((N_PAD, LANES), F32),
            jax.ShapeDtypeStruct((DEC_BATCH, KV_WIDTH, CHUNK), F32),
            jax.ShapeDtypeStruct((DEC_BATCH, KV_WIDTH, CHUNK), F32),
            jax.ShapeDtypeStruct((DEC_BATCH, A_WIDTH), F32),
        ],
        scratch_shapes=[pltpu.VMEM((DEC_BATCH, B_WIDTH), F32), pltpu.VMEM((DEC_BATCH, KV_WIDTH), F32),
                        pltpu.VMEM((DEC_BATCH, KV_WIDTH), F32), pltpu.VMEM((DEC_BATCH, A_WIDTH), F32),
                        pltpu.VMEM((DEC_BATCH, B_WIDTH), F32), pltpu.VMEM((B_HEADS * t, LANES), F32),
                        pltpu.VMEM((B_HEADS * t, LANES), F32)],
        input_output_aliases={16: 0, 17: 1, 18: 2, 19: 3},
        compiler_params=pltpu.CompilerParams(
            dimension_semantics=("arbitrary",), vmem_limit_bytes=VMEM_LIMIT),
        name="sample_premoe",
    )(sinks, x, ck, cv, gmix, win, lng, lnb, w00, b0, goa, gob, wo, gmoe, wr, br, h1, xn, gm, sm)


def _plan_kernel(sm_ref,
                 destm_ref, posm_ref, destt_ref, xbe_ref, live_ref, nxblk_ref, stab_ref, ctab_ref, astart_ref,
                 nwin_ref, base_s, pstart_s):
    ph = pl.program_id(0)
    step = pl.program_id(1)
    lane = _lane_iota((1, LANES))

    @pl.when((ph == 0) & (step == 0))
    def _():
        base_s[...] = jnp.zeros_like(base_s)

    @pl.when(ph == 0)
    def _():
        base_s[...] += jnp.sum(sm_ref[...], axis=0, keepdims=True)

    @pl.when((ph == 1) & (step == 0))
    def _():
        counts = base_s[...]
        padded = jnp.floor((counts + (DISP_CHUNK + EXP_BLOCK - 1)) * (1.0 / EXP_BLOCK)) * EXP_BLOCK
        padded = jnp.where(counts > 0.0, padded, 0.0)
        pend = padded
        for s in (1, 2, 4, 8, 16):
            pend = pend + jnp.where(lane >= s, pltpu.roll(pend, s, 1), 0.0)
        spare = (N_ROWS + lane * DISP_CHUNK).astype(F32)
        pstart_s[...] = jnp.where(counts > 0.0, pend - padded, spare)
        base_s[...] = jnp.zeros_like(base_s)
        brow = lax.broadcasted_iota(jnp.int32, (XBE_ROWS, LANES), 0).astype(F32) * EXP_BLOCK
        done = jnp.where((lane < N_EXPERTS) & (pend <= brow), 1.0, 0.0)
        be = jnp.minimum(jnp.sum(done, axis=-1, keepdims=True), N_EXPERTS - 1.0)
        xbe_ref[...] = jnp.broadcast_to(be, (XBE_ROWS, LANES)).astype(jnp.int32)
        real = jnp.clip(counts - (brow - (pend - padded)), 0.0, float(EXP_BLOCK))
        real = jnp.sum(jnp.where(lane.astype(F32) == be, real, 0.0), axis=-1, keepdims=True)
        groups = jnp.floor((real + (ROW_GROUP - 1)) * (1.0 / ROW_GROUP))
        live_ref[...] = jnp.broadcast_to(groups, (XBE_ROWS, LANES)).astype(jnp.int32)
        total = jnp.sum(jnp.where(lane == N_EXPERTS - 1, pend, 0.0), axis=-1, keepdims=True)
        nxblk_ref[...] = jnp.broadcast_to(total * (1.0 / EXP_BLOCK), (8, LANES)).astype(jnp.int32)
        stab_ref[...] = jnp.zeros_like(stab_ref)
        ctab_ref[...] = jnp.zeros_like(ctab_ref)
        astart_ref[...] = jnp.zeros_like(astart_ref)
        nwin_ref[...] = jnp.zeros_like(nwin_ref)

    @pl.when(ph == 1)
    def _():
        r = lax.broadcasted_iota(jnp.int32, (TOK_TILE, TOK_TILE), 0)
        c = lax.broadcasted_iota(jnp.int32, (TOK_TILE, TOK_TILE), 1)
        lower = jnp.where(c < r, 1.0, 0.0).astype(BF16)
        for q in range(TM // TOK_TILE):
            i = step * (TM // TOK_TILE) + q
            sel = sm_ref[q * TOK_TILE:(q + 1) * TOK_TILE, :]
            cnt = jnp.sum(sel, axis=0, keepdims=True)
            prefix = jnp.dot(lower, sel.astype(BF16), preferred_element_type=F32)
            start = pstart_s[...] + base_s[...]
            dest = jnp.where(sel > 0.0, prefix + start, -1.0)
            destm_ref[q * TOK_TILE:(q + 1) * TOK_TILE, :] = dest
            destt_ref[:, q * TOK_TILE:(q + 1) * TOK_TILE] = dest.T
            has = (cnt > 0.0) & (lane < N_EXPERTS)
            stab_ref[pl.ds(i, 1), :] = start.astype(jnp.int32)
            ctab_ref[pl.ds(i, 1), :] = jnp.where(has, cnt, 0.0).astype(jnp.int32)
            a = jnp.minimum(jnp.floor(start * (1.0 / WIN_ALIGN)) * WIN_ALIGN, float(N_ROWS - WIN))
            nw = jnp.where(has, jnp.floor((start + cnt - a + (WIN - 1)) * (1.0 / WIN)), 0.0)
            posm_ref[q * TOK_TILE:(q + 1) * TOK_TILE, :] = jnp.where(sel > 0.0, prefix + start - a, -1.0)
            astart_ref[pl.ds(i, 1), :] = a.astype(jnp.int32)
            nwin_ref[pl.ds(i, 1), :] = nw.astype(jnp.int32)
            base_s[...] += cnt


def _plan_call(sm):
    tile = lambda ph, i: (i * ph, 0)
    tile_t = lambda ph, i: (0, i * ph)
    tab = jax.ShapeDtypeStruct((TAB_ROWS, LANES), jnp.int32)
    return pl.pallas_call(
        _plan_kernel,
        grid=(2, N_PAD // TM),
        in_specs=[pl.BlockSpec((TM, LANES), lambda ph, i: (i, 0))],
        out_specs=[
            pl.BlockSpec((TM, LANES), tile),
            pl.BlockSpec((TM, LANES), tile),
            pl.BlockSpec((LANES, TM), tile_t),
            _full((XBE_ROWS, LANES)), _full((XBE_ROWS, LANES)), _full((8, LANES)),
            _full((TAB_ROWS, LANES)), _full((TAB_ROWS, LANES)), _full((TAB_ROWS, LANES)), _full((TAB_ROWS, LANES)),
        ],
        out_shape=[
            jax.ShapeDtypeStruct((N_PAD, LANES), F32),
            jax.ShapeDtypeStruct((N_PAD, LANES), F32),
            jax.ShapeDtypeStruct((LANES, N_PAD), F32),
            jax.ShapeDtypeStruct((XBE_ROWS, LANES), jnp.int32),
            jax.ShapeDtypeStruct((XBE_ROWS, LANES), jnp.int32),
            jax.ShapeDtypeStruct((8, LANES), jnp.int32),
            tab, tab, tab, tab,
        ],
        scratch_shapes=[pltpu.VMEM((1, LANES), F32), pltpu.VMEM((1, LANES), F32)],
        compiler_params=pltpu.CompilerParams(
            dimension_semantics=("arbitrary", "arbitrary"), vmem_limit_bytes=VMEM_LIMIT),
        name="moe_plan",
    )(sm)


def _pack_rows(z):
    half = D_MODEL // 2
    lo = lax.bitcast_convert_type(z[:, :half], jnp.uint32) >> 16
    hi = lax.bitcast_convert_type(z[:, half:], jnp.uint32) & jnp.uint32(0xFFFF0000)
    return lax.bitcast_convert_type(hi | lo, jnp.int32)


def _unpack_rows(ref, rows=None):
    rows = ref.shape[0] if rows is None else rows
    flat = ref.reshape(ref.shape[0] * PACK, LANES)
    lo, hi = [], []
    for s in range(PACK):
        w = lax.bitcast_convert_type(flat[pl.ds(s, rows, stride=PACK), :], jnp.uint32)
        lo.append(lax.bitcast_convert_type(w << 16, F32))
        hi.append(lax.bitcast_convert_type(w & jnp.uint32(0xFFFF0000), F32))
    return jnp.concatenate(lo + hi, axis=-1).astype(BF16)


def _dispatch_kernel(stab_ref, ctab_ref, cmax_ref, xn_ref, destt_ref, xs_in, xs_ref,
                     stage0, stage1, stage2, sems, sem2):
    del xs_in
    i = pl.program_id(0)
    last = pl.num_programs(0) - 1
    x = xn_ref[...]
    dt = destt_ref[...]
    rio = lax.broadcasted_iota(jnp.int32, (DISP_CHUNK, 1), 0).astype(F32)

    def chunk_rows(j, stage):
        parts = []
        for e in range(N_EXPERTS):
            first = (stab_ref[i * N_EXPERTS + e] + j * DISP_CHUNK).astype(F32)
            parts.append(jnp.where(dt[e:e + 1, :] == first + rio, 1.0, 0.0).astype(BF16))
        onehot = jnp.concatenate(parts, axis=0)
        words = _pack_rows(jnp.dot(onehot, x, preferred_element_type=F32))
        for s in range(PACK):
            stage[pl.ds(s, N_EXPERTS * DISP_CHUNK, stride=PACK), :] = words[:, s * LANES:(s + 1) * LANES]

    def copy(stage, step, e, j, sem):
        first = stab_ref[step * N_EXPERTS + e] + j * DISP_CHUNK
        rows = stage.reshape(N_EXPERTS * DISP_CHUNK, PACK, LANES)
        return pltpu.make_async_copy(rows.at[pl.ds(e * DISP_CHUNK, DISP_CHUNK)],
                                     xs_ref.at[pl.ds(first, DISP_CHUNK)], sem)

    def step_body(stage, prev_stage, par):
        chunk_rows(0, stage)

        @pl.when(i > 0)
        def _():
            for e in range(N_EXPERTS):
                copy(prev_stage, i - 1, e, 0, sems.at[1 - par]).wait()

        for e in range(N_EXPERTS):
            copy(stage, i, e, 0, sems.at[par]).start()

        @pl.when(i == last)
        def _():
            for e in range(N_EXPERTS):
                copy(stage, i, e, 0, sems.at[par]).wait()

    @pl.when(i % 2 == 0)
    def _():
        step_body(stage0, stage1, 0)

    @pl.when(i % 2 == 1)
    def _():
        step_body(stage1, stage0, 1)

    for j in range(1, TOK_TILE // DISP_CHUNK):

        @pl.when(cmax_ref[i] > j * DISP_CHUNK)
        def _(j=j):
            chunk_rows(j, stage2)
            for e in range(N_EXPERTS):

                @pl.when(ctab_ref[i * N_EXPERTS + e] > j * DISP_CHUNK)
                def _(e=e):
                    cp = copy(stage2, i, e, j, sem2)
                    cp.start()
                    cp.wait()


def _dispatch_call(stab, ctab, cmax, xn, destt, xs_zero):
    stage = pltpu.VMEM((N_EXPERTS * DISP_CHUNK * PACK, LANES), jnp.int32)
    grid_spec = pltpu.PrefetchScalarGridSpec(
        num_scalar_prefetch=3,
        grid=(N_CTILES,),
        in_specs=[
            pl.BlockSpec((TOK_TILE, D_MODEL), lambda i, *_: (i, 0)),
            pl.BlockSpec((N_EXPERTS, TOK_TILE), lambda i, *_: (0, i)),
            pl.BlockSpec(memory_space=pl.ANY),
        ],
        out_specs=pl.BlockSpec(memory_space=pl.ANY),
        scratch_shapes=[stage, stage, stage, pltpu.SemaphoreType.DMA((2,)), pltpu.SemaphoreType.DMA],
    )
    return pl.pallas_call(
        _dispatch_kernel,
        grid_spec=grid_spec,
        out_shape=jax.ShapeDtypeStruct((XS_ROWS, PACK, LANES), jnp.int32),
        input_output_aliases={5: 0},
        compiler_params=pltpu.CompilerParams(
            dimension_semantics=("arbitrary",), vmem_limit_bytes=VMEM_LIMIT),
        name="moe_dispatch",
    )(stab, ctab, cmax, xn, destt, xs_zero)


def _expert_kernel(blke_ref, nblk_ref, live_ref, xs_ref, wgu_hbm, bgu_ref, wdn_hbm, bdn_ref,
                   ys_ref, wgu_f, wdn_f, wgu_s, wdn_s, sems):
    b = pl.program_id(0)
    used = b < nblk_ref[0]
    prev = blke_ref[jnp.maximum(b - 1, 0)]
    fresh = used & ((b == 0) | (blke_ref[b] != prev))

    def fetch(e):
        return (pltpu.make_async_copy(wgu_hbm.at[e], wgu_f, sems.at[0]),
                pltpu.make_async_copy(wdn_hbm.at[e], wdn_f, sems.at[1]))

    @pl.when(b == 0)
    def _():
        for cp in fetch(blke_ref[0]):
            cp.start()

    @pl.when(fresh)
    def _():
        for cp in fetch(blke_ref[b]):
            cp.wait()
        wgu_s[...] = wgu_f[...].astype(BF16)
        wdn_s[...] = wdn_f[...].astype(BF16)

        nxt = lax.while_loop(lambda p: (p < nblk_ref[0]) & (blke_ref[jnp.minimum(p, N_XBLOCKS - 1)] == blke_ref[b]),
                             lambda p: p + 1, b + 1)

        @pl.when(nxt < nblk_ref[0])
        def _():
            for cp in fetch(blke_ref[jnp.minimum(nxt, N_XBLOCKS - 1)]):
                cp.start()

    for groups in range(1, EXP_BLOCK // ROW_GROUP + 1):
        rows = groups * ROW_GROUP

        @pl.when(used & (live_ref[b] == groups))
        def _(rows=rows):
            hid = jnp.dot(_unpack_rows(xs_ref, rows), wgu_s[...], preferred_element_type=F32) + bgu_ref[...]
            gate = jnp.minimum(hid[:, :D_FF], SWIGLU_LIMIT)
            up = jnp.clip(hid[:, D_FF:], -SWIGLU_LIMIT, SWIGLU_LIMIT)
            act = (up + 1.0) * gate * jax.nn.sigmoid(SWIGLU_ALPHA * gate)
            y = jnp.dot(act.astype(BF16), wdn_s[...], preferred_element_type=F32) + bdn_ref[...]
            ys_ref[0:rows, :] = y.astype(BF16)
            if rows < EXP_BLOCK:
                ys_ref[rows:, :] = jnp.zeros((EXP_BLOCK - rows, D_MODEL), BF16)

    @pl.when(jnp.logical_not(used) | (live_ref[b] == 0))
    def _():
        ys_ref[...] = jnp.zeros_like(ys_ref)


def _expert_call(blke, nblk, live, xs, wgu, bgu, wdn, bdn):
    grid_spec = pltpu.PrefetchScalarGridSpec(
        num_scalar_prefetch=3,
        grid=(N_XBLOCKS,),
        in_specs=[
            pl.BlockSpec((EXP_BLOCK, PACK, LANES), lambda b, be, *_: (b, 0, 0)),
            pl.BlockSpec(memory_space=pl.ANY),
            pl.BlockSpec((None, 1, 2 * D_FF), lambda b, be, *_: (be[b], 0, 0)),
            pl.BlockSpec(memory_space=pl.ANY),
            pl.BlockSpec((None, 1, D_MODEL), lambda b, be, *_: (be[b], 0, 0)),
        ],
        out_specs=pl.BlockSpec((EXP_BLOCK, D_MODEL), lambda b, be, *_: (b, 0)),
        scratch_shapes=[pltpu.VMEM((D_MODEL, 2 * D_FF), F32), pltpu.VMEM((D_FF, D_MODEL), F32),
                        pltpu.VMEM((D_MODEL, 2 * D_FF), BF16), pltpu.VMEM((D_FF, D_MODEL), BF16),
                        pltpu.SemaphoreType.DMA((2,))],
    )
    return pl.pallas_call(
        _expert_kernel,
        grid_spec=grid_spec,
        out_shape=jax.ShapeDtypeStruct((N_ROWS, D_MODEL), BF16),
        compiler_params=pltpu.CompilerParams(
            dimension_semantics=("arbitrary",), vmem_limit_bytes=VMEM_LIMIT),
        name="moe_experts",
    )(blke, nblk, live, xs, wgu, bgu, wdn, bdn)


def _combine_kernel(astart_ref, nwin_ref, over_ref,
                    destm_ref, posm_ref, gm_ref, rep_ref, h1_ref, plep_ref, ples_ref, gple_ref, wpg_ref, wpp_ref,
                    gfin_ref, ys_any, yp_ref, ysm_ref, wins, moe_s, tmp_s, wsems, sem):
    i = pl.program_id(0)
    slot = i % 2

    def win_copy(step, e, into):
        first = pl.multiple_of(astart_ref[step * N_EXPERTS + e], WIN_ALIGN)
        return pltpu.make_async_copy(ys_any.at[pl.ds(first, WIN)], wins.at[into, pl.ds(e * WIN, WIN)],
                                     wsems.at[into])

    @pl.when(i == 0)
    def _():
        for e in range(N_EXPERTS):
            win_copy(0, e, 0).start()

    @pl.when(i + 1 < pl.num_programs(0))
    def _():
        for e in range(N_EXPERTS):
            win_copy(i + 1, e, 1 - slot).start()

    for e in range(N_EXPERTS):
        win_copy(i, e, slot).wait()

    gates = gm_ref[...]
    rep = rep_ref[...]
    pos_rep = jnp.dot(posm_ref[...].astype(BF16), rep, preferred_element_type=F32)
    gate_rep = jnp.dot(gates.astype(BF16), rep, preferred_element_type=F32)
    within = (_lane_iota((TOK_TILE, N_EXPERTS * WIN)) % WIN).astype(F32)
    gsel = jnp.where(pos_rep == within, gate_rep, 0.0).astype(BF16)
    mrows = pl.ds(pl.multiple_of((i % TAIL_TILES) * TOK_TILE, TOK_TILE), TOK_TILE)
    moe_s[mrows, :] = jnp.dot(gsel, wins[slot], preferred_element_type=F32)

    @pl.when(over_ref[i] > 0)
    def _():
        dest = destm_ref[...]
        lane = _lane_iota((TOK_TILE, LANES))
        lane_f = lane.astype(F32)
        tmp_s[...] = jnp.zeros_like(tmp_s)

        def per_expert(e, carry):
            a = astart_ref[i * N_EXPERTS + e]
            dcol = jnp.sum(jnp.where(lane == e, dest, 0.0), axis=-1, keepdims=True)
            gcol = jnp.sum(jnp.where(lane == e, gates, 0.0), axis=-1, keepdims=True)

            def per_window(w, carry2):
                first = a + w * WIN
                start = pl.multiple_of(jnp.minimum(first, N_ROWS - WIN), WIN_ALIGN)
                cp = pltpu.make_async_copy(ys_any.at[pl.ds(start, WIN)], tmp_s.at[pl.ds(0, WIN)], sem)
                cp.start()
                cp.wait()
                hit = (lane < WIN) & (dcol == start.astype(F32) + lane_f) & (dcol >= first.astype(F32))
                sel = jnp.where(hit, gcol, 0.0).astype(BF16)
                moe_s[mrows, :] += jnp.dot(sel, tmp_s[...], preferred_element_type=F32)
                return carry2

            return lax.fori_loop(1, nwin_ref[i * N_EXPERTS + e], per_window, carry)

        lax.fori_loop(0, N_EXPERTS, per_expert, 0)

    def tail(h1, moe_rows, ple):
        h2 = h1 + moe_rows
        hn = _rms(h2, gple_ref[...]).astype(BF16)
        gate = jax.nn.sigmoid(jnp.dot(hn, wpg_ref[...], preferred_element_type=F32))
        proj = jnp.dot(ple.astype(BF16), wpp_ref[...], preferred_element_type=F32)
        return _rms(h2 + gate * proj, gfin_ref[...])

    is_sample = i == N_CTILES - 1

    @pl.when((i % TAIL_TILES == TAIL_TILES - 1) & jnp.logical_not(is_sample))
    def _():
        yp_ref[...] = tail(h1_ref[...], moe_s[...], plep_ref[...])

    @pl.when(is_sample)
    def _():
        ysm_ref[...] = tail(h1_ref[0:DEC_BATCH, :], moe_s[0:DEC_BATCH, :], ples_ref[...])


def _combine_call(astart, nwin, over, ys, destm, posm, gm, h1, plep, ples, gple, wpg, wpp, gfin):
    rep = (jnp.arange(N_EXPERTS * WIN)[None, :] // WIN == jnp.arange(LANES)[:, None]).astype(BF16)
    rows = TAIL_TILES * TOK_TILE
    last = N_PROMPT // rows - 1

    grid_spec = pltpu.PrefetchScalarGridSpec(
        num_scalar_prefetch=3,
        grid=(N_CTILES,),
        in_specs=[
            pl.BlockSpec((TOK_TILE, LANES), lambda i, *_: (i, 0)),
            pl.BlockSpec((TOK_TILE, LANES), lambda i, *_: (i, 0)),
            pl.BlockSpec((TOK_TILE, LANES), lambda i, *_: (i, 0)),
            pl.BlockSpec((LANES, N_EXPERTS * WIN), lambda i, *_: (0, 0)),
            pl.BlockSpec((rows, D_MODEL), lambda i, *_: (i // TAIL_TILES, 0)),
            pl.BlockSpec((rows, PLE_DIM), lambda i, *_: (jnp.minimum(i // TAIL_TILES, last), 0)),
            pl.BlockSpec((DEC_BATCH, PLE_DIM), lambda i, *_: (0, 0)),
            pl.BlockSpec((1, D_MODEL), lambda i, *_: (0, 0)),
            pl.BlockSpec((D_MODEL, D_MODEL), lambda i, *_: (0, 0)),
            pl.BlockSpec((PLE_DIM, D_MODEL), lambda i, *_: (0, 0)),
            pl.BlockSpec((1, D_MODEL), lambda i, *_: (0, 0)),
            pl.BlockSpec(memory_space=pl.ANY),
        ],
        out_specs=[
            pl.BlockSpec((rows, D_MODEL), lambda i, *_: (jnp.minimum(i // TAIL_TILES, last), 0)),
            pl.BlockSpec((DEC_BATCH, D_MODEL), lambda i, *_: (0, 0)),
        ],
        scratch_shapes=[pltpu.VMEM((2, N_EXPERTS * WIN, D_MODEL), BF16), pltpu.VMEM((rows, D_MODEL), F32),
                        pltpu.VMEM((2 * WIN, D_MODEL), BF16), pltpu.SemaphoreType.DMA((2,)),
                        pltpu.SemaphoreType.DMA],
    )
    return pl.pallas_call(
        _combine_kernel,
        grid_spec=grid_spec,
        out_shape=[jax.ShapeDtypeStruct((N_PROMPT, D_MODEL), F32),
                   jax.ShapeDtypeStruct((DEC_BATCH, D_MODEL), F32)],
        compiler_params=pltpu.CompilerParams(
            dimension_semantics=("arbitrary",), vmem_limit_bytes=VMEM_LIMIT),
        name="moe_combine_tail",
    )(astart, nwin, over, destm, posm, gm, rep, h1, plep, ples, gple, wpg, wpp, gfin, ys)


def kernel(x_prompt, x_sample, cache_swa_k, cache_swa_v, p_prompt, p_sample, g_mix, w_in, ln_v_g, ln_v_b,
           w_sp, b_sp, sinks, g_out_a, g_out_b, w_o, g_moe, w_router, b_router, w_gu, b_gu, w_dn, b_dn,
           g_ple, w_ple_gate, w_ple_proj, g_final):
    l = 0
    row = lambda v: v.reshape(1, -1)
    win = w_in[l].astype(BF16)
    wo = w_o[l].astype(BF16)
    tril = jnp.tril(jnp.ones((CHUNK, CHUNK), dtype=bool))
    wsp = jnp.where(tril, w_sp[l], 0.0).astype(BF16)
    bsp = jnp.repeat(b_sp[l].T, HEAD_DIM, axis=1)
    w00 = row(jnp.repeat(w_sp[l][:, 0, 0], HEAD_DIM))
    b0 = row(jnp.repeat(b_sp[l][:, 0], HEAD_DIM))
    wr_hi = w_router[l].astype(BF16)
    wr_lo = (w_router[l] - wr_hi.astype(F32)).astype(BF16)
    wr = jnp.concatenate([wr_hi, wr_lo, jnp.zeros((D_MODEL, LANES - 2 * N_EXPERTS), BF16)], axis=1)
    br = row(jnp.concatenate([b_router[l], jnp.zeros((LANES - N_EXPERTS,), F32)]))
    common = (row(g_mix[l]), win, row(ln_v_g[l]), row(ln_v_b[l]))
    tail = (row(g_out_a[l]), row(g_out_b[l]), wo, row(g_moe[l]), wr, br)

    h1, xn, gm, sm, k_p, v_p = _prompt_call(
        x_prompt.reshape(N_PROMPT, D_MODEL), sinks[l], *common, wsp, bsp, *tail)
    chan = lambda a: jnp.transpose(a, (0, 2, 3, 1)).reshape(DEC_BATCH, KV_WIDTH, CHUNK)
    ck = chan(cache_swa_k[l])
    cv = chan(cache_swa_v[l])
    h1, xn, gm, sm, k_s, v_s, va_s = _decode_call(
        x_sample.reshape(DEC_BATCH, D_MODEL), ck, cv, sinks[l], *common, w00, b0, *tail, h1, xn, gm, sm)

    destm, posm, destt, xbe, live, nxblk, stab, ctab, astart, nwin = _plan_call(sm)
    flat = lambda tab, n: tab[:n, :N_EXPERTS].reshape(-1)
    over = (jnp.max(nwin[:N_CTILES, :N_EXPERTS], axis=1) > 1).astype(jnp.int32)
    cmax = jnp.max(ctab[:N_CTILES, :N_EXPERTS], axis=1)

    xs = _dispatch_call(flat(stab, N_CTILES), flat(ctab, N_CTILES), cmax, xn, destt[:N_EXPERTS],
                        jnp.zeros((XS_ROWS, PACK, LANES), jnp.int32))
    ys = _expert_call(xbe[:N_XBLOCKS, 0], nxblk[0, :1], live[:N_XBLOCKS, 0], xs,
                      w_gu[l], b_gu[l].reshape(N_EXPERTS, 1, 2 * D_FF), w_dn[l], b_dn[l].reshape(N_EXPERTS, 1, D_MODEL))
    y_p, y_s = _combine_call(
        flat(astart, N_CTILES), flat(nwin, N_CTILES), over, ys, destm, posm, gm, h1,
        p_prompt[l].reshape(N_PROMPT, PLE_DIM), p_sample[l].reshape(DEC_BATCH, PLE_DIM),
        row(g_ple[l]), w_ple_gate[l].astype(BF16), w_ple_proj[l].astype(BF16), row(g_final))

    kv5 = lambda a, n: a.reshape(1, n, CHUNK, 2, HEAD_DIM)
    win5 = lambda a: jnp.transpose(a.reshape(DEC_BATCH, 2, HEAD_DIM, CHUNK), (0, 3, 1, 2))[None]
    return (y_p.reshape(BATCH, SEQ, D_MODEL), y_s.reshape(DEC_BATCH, 1, D_MODEL),
            kv5(k_p, BATCH), kv5(v_p, BATCH), win5(k_s), win5(v_s),
            va_s.reshape(1, DEC_BATCH, 1, A_WIDTH))
```

```python
import math

import jax
import jax.numpy as jnp
from jax import lax
from jax.experimental import pallas as pl
from jax.experimental.pallas import tpu as pltpu

F32 = jnp.float32
BF16 = jnp.bfloat16

D_MODEL = 1024
BATCH = 4
SEQ = 4096
DEC_BATCH = 128
HEAD_DIM = 64
A_WIDTH = 512
B_WIDTH = 512
B_HEADS = 8
KV_WIDTH = 128
IN_WIDTH = 2 * A_WIDTH + B_WIDTH + 2 * KV_WIDTH
CHUNK = 128
N_EXPERTS = 32
TOP_K = 4
D_FF = 1024
SWIGLU_ALPHA = 1.702
SWIGLU_LIMIT = 7.0
PLE_DIM = 256
EPS = 1e-5

LANES = 128
VMEM_LIMIT = 56 * 1024 * 1024
N_PROMPT = BATCH * SEQ
N_TOK = N_PROMPT + DEC_BATCH
TM = 1024
N_PAD = ((N_TOK + TM - 1) // TM) * TM
TOK_TILE = 128
N_PAD_TILES = N_PAD // TOK_TILE
N_CTILES = N_PROMPT // TOK_TILE + 1
DEC_TILE = 16
EXP_BLOCK = 512
ROW_GROUP = 128
DISP_CHUNK = 32
N_XBLOCKS = (N_TOK * TOP_K + N_EXPERTS * (DISP_CHUNK + EXP_BLOCK - 1) + EXP_BLOCK - 1) // EXP_BLOCK
N_ROWS = N_XBLOCKS * EXP_BLOCK
XS_ROWS = N_ROWS + N_EXPERTS * DISP_CHUNK
XBE_ROWS = ((N_XBLOCKS + 7) // 8) * 8
TAB_ROWS = ((N_PAD_TILES + 7) // 8) * 8
PACK = D_MODEL // 2 // LANES
WIN = 48
WIN_ALIGN = 16
TAIL_TILES = 4
NEG = -1e30


def _rms(x, g):
    return x * lax.rsqrt(jnp.mean(x * x, axis=-1, keepdims=True) + EPS) * g


def _gelu(x):
    c = math.sqrt(2.0 / math.pi)
    return x * (0.5 * (1.0 + jnp.tanh(c * (x + 0.044715 * (x * x * x)))))


def _layernorm(x, g, b):
    mu = jnp.mean(x, axis=-1, keepdims=True)
    xc = x - mu
    return xc * lax.rsqrt(jnp.mean(xc * xc, axis=-1, keepdims=True) + EPS) * g + b


def _lane_iota(shape):
    return lax.broadcasted_iota(jnp.int32, shape, len(shape) - 1)


def _full(shape):
    n = len(shape)
    return pl.BlockSpec(shape, lambda *_: (0,) * n)


def _route(xn2, wr_ref, br_ref):
    m = xn2.shape[0]
    xh = xn2.astype(BF16)
    xl = (xn2 - xh.astype(F32)).astype(BF16)
    r = jnp.dot(jnp.concatenate([xh, xl], axis=0), wr_ref[...], preferred_element_type=F32)
    r = r[:m] + r[m:]
    lane = _lane_iota((m, LANES))
    lane_f = lane.astype(F32)
    logits = jnp.where(lane < N_EXPERTS, r + pltpu.roll(r, LANES - N_EXPERTS, 1) + br_ref[...], NEG)
    work = logits
    sel = jnp.zeros((m, LANES), F32)
    top = None
    z = None
    for _ in range(TOP_K):
        mx = jnp.max(work, axis=-1, keepdims=True)
        first = jnp.min(jnp.where(work == mx, lane_f, float(LANES)), axis=-1, keepdims=True)
        hit = lane_f == first
        sel = jnp.where(hit, 1.0, sel)
        work = jnp.where(hit, NEG, work)
        if top is None:
            top = mx
            z = jnp.ones_like(mx)
        else:
            z = z + jnp.exp(mx - top)
    gates = jnp.where(sel > 0.0, jnp.exp(logits - top) / z, 0.0)
    return gates, sel


def _prompt_kernel(sinks_ref, x_ref, gmix_ref, win_ref, lng_ref, lnb_ref, wsp_ref, bsp_ref,
                   goa_ref, gob_ref, wo_ref, gmoe_ref, wr_ref, br_ref,
                   h1_ref, xn_ref, gm_ref, sm_ref, k_ref, v_ref,
                   z_s, kv_s, cat_s):
    g = pl.program_id(0)

    @pl.when(g >= N_PROMPT // TM)
    def _():
        h1_ref[...] = jnp.zeros_like(h1_ref)
        xn_ref[...] = jnp.zeros_like(xn_ref)
        gm_ref[...] = jnp.zeros_like(gm_ref)
        sm_ref[...] = jnp.zeros_like(sm_ref)

    @pl.when(g < N_PROMPT // TM)
    def _():
        _prompt_tile(g % (SEQ // TM), sinks_ref, x_ref, gmix_ref, win_ref, lng_ref, lnb_ref, wsp_ref, bsp_ref,
                     goa_ref, gob_ref, wo_ref, gmoe_ref, wr_ref, br_ref,
                     h1_ref, xn_ref, gm_ref, sm_ref, k_ref, v_ref, z_s, kv_s, cat_s)


def _prompt_tile(j, sinks_ref, x_ref, gmix_ref, win_ref, lng_ref, lnb_ref, wsp_ref, bsp_ref,
                 goa_ref, gob_ref, wo_ref, gmoe_ref, wr_ref, br_ref,
                 h1_ref, xn_ref, gm_ref, sm_ref, k_ref, v_ref, z_s, kv_s, cat_s):
    @pl.when(j == 0)
    def _():
        kv_s[0:CHUNK, :] = jnp.zeros((CHUNK, 2 * KV_WIDTH), F32)

    xn = _rms(x_ref[...], gmix_ref[...]).astype(BF16)
    z_s[...] = jnp.dot(xn, win_ref[...], preferred_element_type=F32)
    kv_s[CHUNK:, :] = z_s[:, 2 * A_WIDTH + B_WIDTH:]

    lane = _lane_iota((CHUNK, LANES))
    lo = lane < HEAD_DIM
    lane2 = _lane_iota((2 * CHUNK, LANES))
    lo2 = lane2 < HEAD_DIM
    qi = lax.broadcasted_iota(jnp.int32, (CHUNK, CHUNK), 0)
    kc = lax.broadcasted_iota(jnp.int32, (CHUNK, CHUNK), 1)
    from_prev = kc > qi
    dist = jnp.where(from_prev, qi + CHUNK - kc, qi - kc).astype(F32)

    def chunk_body(c, carry):
        r0 = pl.multiple_of(c * CHUNK, CHUNK)
        rows = pl.ds(r0, CHUNK)
        u = _gelu(z_s[rows, 0:A_WIDTH])
        va = _layernorm(_gelu(z_s[rows, A_WIDTH:2 * A_WIDTH]), lng_ref[...], lnb_ref[...])
        vab = va.astype(BF16)
        slabs = []
        for p in range(A_WIDTH // LANES):
            slab = vab[:, p * LANES:(p + 1) * LANES]
            m0 = jnp.dot(wsp_ref[2 * p], slab, preferred_element_type=F32)
            m1 = jnp.dot(wsp_ref[2 * p + 1], slab, preferred_element_type=F32)
            slabs.append(jnp.where(lo, m0, m1))
        ya = u * (jnp.concatenate(slabs, axis=-1) + bsp_ref[...])
        ya_n = _rms(ya, goa_ref[...])
        k2 = kv_s[pl.ds(r0, 2 * CHUNK), 0:KV_WIDTH]
        v2 = kv_s[pl.ds(r0, 2 * CHUNK), KV_WIDTH:2 * KV_WIDTH]
        k2r = pltpu.roll(k2, HEAD_DIM, 1)
        v2r = pltpu.roll(v2, HEAD_DIM, 1)
        kd = (jnp.where(lo2, k2, k2r).astype(BF16), jnp.where(lo2, k2r, k2).astype(BF16))
        vd = (jnp.where(lo2, v2, v2r).astype(BF16), jnp.where(lo2, v2r, v2).astype(BF16))
        prev_ok = (j > 0) | (c > 0)
        masked = from_prev & jnp.logical_not(prev_ok)
        yb_slabs = []
        for kv in range(2):
            q0 = z_s[rows, 2 * A_WIDTH + (2 * kv) * LANES:2 * A_WIDTH + (2 * kv + 1) * LANES]
            q1 = z_s[rows, 2 * A_WIDTH + (2 * kv + 1) * LANES:2 * A_WIDTH + (2 * kv + 2) * LANES]
            lhs = jnp.concatenate([jnp.where(lo, q0, 0.0), jnp.where(lo, 0.0, q0),
                                   jnp.where(lo, q1, 0.0), jnp.where(lo, 0.0, q1)], axis=0).astype(BF16)
            s_all = lax.dot_general(lhs, kd[kv], (((1,), (1,)), ((), ())), preferred_element_type=F32)
            probs = []
            for i in range(4):
                h = 4 * kv + i
                slope = 2.0 ** (-(h + 1))
                sink = sinks_ref[h]
                sh = s_all[i * CHUNK:(i + 1) * CHUNK]
                s = jnp.where(from_prev, sh[:, :CHUNK], sh[:, CHUNK:]) * (HEAD_DIM ** -0.5) - slope * dist
                s = jnp.where(masked, NEG, s)
                mx = jnp.maximum(jnp.max(s, axis=-1, keepdims=True), sink)
                e = jnp.exp(s - mx)
                den = jnp.sum(e, axis=-1, keepdims=True) + jnp.exp(sink - mx)
                p = e * (1.0 / den)
                probs.append(jnp.concatenate([jnp.where(from_prev, p, 0.0), jnp.where(from_prev, 0.0, p)], axis=-1))
            pm = jnp.concatenate(probs, axis=0).astype(BF16)
            o = jnp.dot(pm, vd[kv], preferred_element_type=F32)
            yb_slabs.append(jnp.where(lo, o[0:CHUNK], o[CHUNK:2 * CHUNK]))
            yb_slabs.append(jnp.where(lo, o[2 * CHUNK:3 * CHUNK], o[3 * CHUNK:4 * CHUNK]))
        yb_n = _rms(jnp.concatenate(yb_slabs, axis=-1), gob_ref[...])
        cat_s[rows, 0:A_WIDTH] = ya_n.astype(BF16)
        cat_s[rows, A_WIDTH:] = yb_n.astype(BF16)
        return carry

    lax.fori_loop(0, TM // CHUNK, chunk_body, 0)

    kv_s[0:CHUNK, :] = kv_s[TM:TM + CHUNK, :]
    k_ref[...] = kv_s[TM:TM + CHUNK, 0:KV_WIDTH]
    v_ref[...] = kv_s[TM:TM + CHUNK, KV_WIDTH:]

    h1 = x_ref[...] + jnp.dot(cat_s[...], wo_ref[...], preferred_element_type=F32)
    h1_ref[...] = h1
    xn2 = _rms(h1, gmoe_ref[...])
    xn_ref[...] = xn2.astype(BF16)
    gates, sel = _route(xn2, wr_ref, br_ref)
    gm_ref[...] = gates
    sm_ref[...] = sel


def _prompt_call(x, sinks, gmix, win, lng, lnb, wsp, bsp, goa, gob, wo, gmoe, wr, br):
    real = N_PROMPT // TM
    row = lambda g: (g, 0)
    seq = lambda g: (jnp.minimum(g, real - 1) // (SEQ // TM), 0, 0)
    return pl.pallas_call(
        _prompt_kernel,
        grid=(N_PAD // TM,),
        in_specs=[
            pl.BlockSpec(memory_space=pltpu.SMEM),
            pl.BlockSpec((TM, D_MODEL), lambda g: (jnp.minimum(g, real - 1), 0)),
            _full((1, D_MODEL)), _full((D_MODEL, IN_WIDTH)), _full((1, A_WIDTH)), _full((1, A_WIDTH)),
            _full((8, CHUNK, CHUNK)), _full((CHUNK, A_WIDTH)), _full((1, A_WIDTH)), _full((1, B_WIDTH)),
            _full((D_MODEL, D_MODEL)), _full((1, D_MODEL)), _full((D_MODEL, LANES)), _full((1, LANES)),
        ],
        out_specs=[
            pl.BlockSpec((TM, D_MODEL), row),
            pl.BlockSpec((TM, D_MODEL), row),
            pl.BlockSpec((TM, LANES), row),
            pl.BlockSpec((TM, LANES), row),
            pl.BlockSpec((None, CHUNK, KV_WIDTH), seq),
            pl.BlockSpec((None, CHUNK, KV_WIDTH), seq),
        ],
        out_shape=[
            jax.ShapeDtypeStruct((N_PAD, D_MODEL), F32),
            jax.ShapeDtypeStruct((N_PAD, D_MODEL), BF16),
            jax.ShapeDtypeStruct((N_PAD, LANES), F32),
            jax.ShapeDtypeStruct((N_PAD, LANES), F32),
            jax.ShapeDtypeStruct((BATCH, CHUNK, KV_WIDTH), F32),
            jax.ShapeDtypeStruct((BATCH, CHUNK, KV_WIDTH), F32),
        ],
        scratch_shapes=[
            pltpu.VMEM((TM, IN_WIDTH), F32),
            pltpu.VMEM((TM + CHUNK, 2 * KV_WIDTH), F32),
            pltpu.VMEM((TM, D_MODEL), BF16),
        ],
        compiler_params=pltpu.CompilerParams(
            dimension_semantics=("arbitrary",), vmem_limit_bytes=VMEM_LIMIT),
        name="prompt_premoe",
    )(sinks, x, gmix, win, lng, lnb, wsp, bsp, goa, gob, wo, gmoe, wr, br)


def _decode_kernel(sinks_ref, x_ref, ck_ref, cv_ref, gmix_ref, win_ref, lng_ref, lnb_ref, w00_ref, b0_ref,
                   goa_ref, gob_ref, wo_ref, gmoe_ref, wr_ref, br_ref,
                   h1_in, xn_in, gm_in, sm_in,
                   h1_ref, xn_ref, gm_ref, sm_ref, nk_ref, nv_ref, va_ref,
                   q_s, kn_s, vn_s, ya_s, yb_s, qm_s, o_s):
    del h1_in, xn_in, gm_in, sm_in
    i = pl.program_id(0)
    t = DEC_TILE

    @pl.when(i == 0)
    def _():
        xn = _rms(x_ref[...], gmix_ref[...]).astype(BF16)
        z = jnp.dot(xn, win_ref[...], preferred_element_type=F32)
        u = _gelu(z[:, 0:A_WIDTH])
        va = _layernorm(_gelu(z[:, A_WIDTH:2 * A_WIDTH]), lng_ref[...], lnb_ref[...])
        va_ref[...] = va
        ya_s[...] = _rms(u * (w00_ref[...] * va + b0_ref[...]), goa_ref[...])
        q_s[...] = z[:, 2 * A_WIDTH:2 * A_WIDTH + B_WIDTH]
        kn_s[...] = z[:, 2 * A_WIDTH + B_WIDTH:2 * A_WIDTH + B_WIDTH + KV_WIDTH]
        vn_s[...] = z[:, 2 * A_WIDTH + B_WIDTH + KV_WIDTH:]

    rows = pl.ds(pl.multiple_of(i * t, t), t)
    q = q_s[rows, :]
    kn = kn_s[rows, :]
    vn = vn_s[rows, :]
    lane = _lane_iota((t, LANES))
    lo = lane < HEAD_DIM
    stacked = []
    for h in range(B_HEADS):
        qh = jnp.where(lo if h % 2 == 0 else jnp.logical_not(lo), q[:, (h // 2) * LANES:(h // 2 + 1) * LANES], 0.0)
        stacked.append(pltpu.roll(qh, HEAD_DIM, 1) if h % 2 != h // 4 else qh)
    qm_s[...] = jnp.concatenate(stacked, axis=0)
    hrow = lax.broadcasted_iota(jnp.int32, (B_HEADS, 1), 0)
    slope = jnp.zeros((B_HEADS, 1), F32)
    sink = jnp.zeros((B_HEADS, 1), F32)
    for h in range(B_HEADS):
        slope = jnp.where(hrow == h, 2.0 ** (-(h + 1)), slope)
        sink = jnp.where(hrow == h, sinks_ref[h], sink)
    pos = _lane_iota((B_HEADS, CHUNK))
    bias = slope * (CHUNK - pos).astype(F32)
    on_kv = (_lane_iota((B_HEADS, LANES)) < HEAD_DIM) == (hrow < B_HEADS // 2)
    last_pos = _lane_iota((KV_WIDTH, CHUNK)) == CHUNK - 1
    last_row = lax.broadcasted_iota(jnp.int32, (CHUNK, KV_WIDTH), 0) == CHUNK - 1
    for b in range(t):
        qb = qm_s[pl.ds(b, B_HEADS, stride=t), :]
        kt = ck_ref[b]
        vt = cv_ref[b]
        s_c = jnp.dot(qb.astype(BF16), kt.astype(BF16), preferred_element_type=F32) * (HEAD_DIM ** -0.5) - bias
        s_c = jnp.where(pos >= 1, s_c, NEG)
        s_n = jnp.sum(qb * kn[b:b + 1, :], axis=-1, keepdims=True) * (HEAD_DIM ** -0.5)
        mx = jnp.maximum(jnp.maximum(jnp.max(s_c, axis=-1, keepdims=True), s_n), sink)
        e_c = jnp.exp(s_c - mx)
        e_n = jnp.exp(s_n - mx)
        inv = 1.0 / (jnp.sum(e_c, axis=-1, keepdims=True) + e_n + jnp.exp(sink - mx))
        o = lax.dot_general((e_c * inv).astype(BF16), vt.astype(BF16), (((1,), (1,)), ((), ())),
                            preferred_element_type=F32)
        o = o + (e_n * inv) * vn[b:b + 1, :]
        o_s[pl.ds(b, B_HEADS, stride=t), :] = jnp.where(on_kv, o, 0.0)
        nk_ref[b] = jnp.where(last_pos, jnp.where(last_row, kn[b:b + 1, :], 0.0).T, pltpu.roll(kt, CHUNK - 1, 1))
        nv_ref[b] = jnp.where(last_pos, jnp.where(last_row, vn[b:b + 1, :], 0.0).T, pltpu.roll(vt, CHUNK - 1, 1))
    slabs = []
    for p in range(B_WIDTH // LANES):
        pair = []
        for h in (2 * p, 2 * p + 1):
            oh = o_s[h * t:(h + 1) * t, :]
            pair.append(pltpu.roll(oh, HEAD_DIM, 1) if h % 2 != h // 4 else oh)
        slabs.append(pair[0] + pair[1])
    yb_s[rows, :] = jnp.concatenate(slabs, axis=-1)

    @pl.when(i == pl.num_programs(0) - 1)
    def _():
        yb_n = _rms(yb_s[...], gob_ref[...])
        cat = jnp.concatenate([ya_s[...], yb_n], axis=-1).astype(BF16)
        h1 = x_ref[...] + jnp.dot(cat, wo_ref[...], preferred_element_type=F32)
        xn2 = _rms(h1, gmoe_ref[...])
        gates, sel = _route(xn2, wr_ref, br_ref)
        h1_ref[...] = h1
        xn_ref[...] = xn2.astype(BF16)
        gm_ref[...] = gates
        sm_ref[...] = sel


def _decode_call(x, ck, cv, sinks, gmix, win, lng, lnb, w00, b0, goa, gob, wo, gmoe, wr, br, h1, xn, gm, sm):
    t = DEC_TILE
    cache = pl.BlockSpec((t, KV_WIDTH, CHUNK), lambda i: (i, 0, 0))
    tok = lambda width: pl.BlockSpec((DEC_BATCH, width), lambda i: (N_PROMPT // DEC_BATCH, 0))
    anyspec = pl.BlockSpec(memory_space=pl.ANY)
    return pl.pallas_call(
        _decode_kernel,
        grid=(DEC_BATCH // t,),
        in_specs=[
            pl.BlockSpec(memory_space=pltpu.SMEM),
            _full((DEC_BATCH, D_MODEL)), cache, cache,
            _full((1, D_MODEL)), _full((D_MODEL, IN_WIDTH)), _full((1, A_WIDTH)), _full((1, A_WIDTH)),
            _full((1, A_WIDTH)), _full((1, A_WIDTH)), _full((1, A_WIDTH)), _full((1, B_WIDTH)),
            _full((D_MODEL, D_MODEL)), _full((1, D_MODEL)), _full((D_MODEL, LANES)), _full((1, LANES)),
            anyspec, anyspec, anyspec, anyspec,
        ],
        out_specs=[tok(D_MODEL), tok(D_MODEL), tok(LANES), tok(LANES), cache, cache, _full((DEC_BATCH, A_WIDTH))],
        out_shape=[
            jax.ShapeDtypeStruct((N_PAD, D_MODEL), F32),
            jax.ShapeDtypeStruct((N_PAD, D_MODEL), BF16),
            jax.ShapeDtypeStruct((N_PAD, LANES), F32),
            jax.ShapeDtypeStruct((N_PAD, LANES), F32),
            jax.ShapeDtypeStruct((DEC_BATCH, KV_WIDTH, CHUNK), F32),
            jax.ShapeDtypeStruct((DEC_BATCH, KV_WIDTH, CHUNK), F32),
            jax.ShapeDtypeStruct((DEC_BATCH, A_WIDTH), F32),
        ],
        scratch_shapes=[pltpu.VMEM((DEC_BATCH, B_WIDTH), F32), pltpu.VMEM((DEC_BATCH, KV_WIDTH), F32),
                        pltpu.VMEM((DEC_BATCH, KV_WIDTH), F32), pltpu.VMEM((DEC_BATCH, A_WIDTH), F32),
                        pltpu.VMEM((DEC_BATCH, B_WIDTH), F32), pltpu.VMEM((B_HEADS * t, LANES), F32),
                        pltpu.VMEM((B_HEADS * t, LANES), F32)],
        input_output_aliases={16: 0, 17: 1, 18: 2, 19: 3},
        compiler_params=pltpu.CompilerParams(
            dimension_semantics=("arbitrary",), vmem_limit_bytes=VMEM_LIMIT),
        name="sample_premoe",
    )(sinks, x, ck, cv, gmix, win, lng, lnb, w00, b0, goa, gob, wo, gmoe, wr, br, h1, xn, gm, sm)


def _plan_kernel(sm_ref,
                 destm_ref, posm_ref, destt_ref, xbe_ref, live_ref, nxblk_ref, stab_ref, ctab_ref, astart_ref,
                 nwin_ref, base_s, pstart_s):
    ph = pl.program_id(0)
    step = pl.program_id(1)
    lane = _lane_iota((1, LANES))

    @pl.when((ph == 0) & (step == 0))
    def _():
        base_s[...] = jnp.zeros_like(base_s)

    @pl.when(ph == 0)
    def _():
        base_s[...] += jnp.sum(sm_ref[...], axis=0, keepdims=True)

    @pl.when((ph == 1) & (step == 0))
    def _():
        counts = base_s[...]
        padded = jnp.floor((counts + (DISP_CHUNK + EXP_BLOCK - 1)) * (1.0 / EXP_BLOCK)) * EXP_BLOCK
        padded = jnp.where(counts > 0.0, padded, 0.0)
        pend = padded
        for s in (1, 2, 4, 8, 16):
            pend = pend + jnp.where(lane >= s, pltpu.roll(pend, s, 1), 0.0)
        spare = (N_ROWS + lane * DISP_CHUNK).astype(F32)
        pstart_s[...] = jnp.where(counts > 0.0, pend - padded, spare)
        base_s[...] = jnp.zeros_like(base_s)
        brow = lax.broadcasted_iota(jnp.int32, (XBE_ROWS, LANES), 0).astype(F32) * EXP_BLOCK
        done = jnp.where((lane < N_EXPERTS) & (pend <= brow), 1.0, 0.0)
        be = jnp.minimum(jnp.sum(done, axis=-1, keepdims=True), N_EXPERTS - 1.0)
        xbe_ref[...] = jnp.broadcast_to(be, (XBE_ROWS, LANES)).astype(jnp.int32)
        real = jnp.clip(counts - (brow - (pend - padded)), 0.0, float(EXP_BLOCK))
        real = jnp.sum(jnp.where(lane.astype(F32) == be, real, 0.0), axis=-1, keepdims=True)
        groups = jnp.floor((real + (ROW_GROUP - 1)) * (1.0 / ROW_GROUP))
        live_ref[...] = jnp.broadcast_to(groups, (XBE_ROWS, LANES)).astype(jnp.int32)
        total = jnp.sum(jnp.where(lane == N_EXPERTS - 1, pend, 0.0), axis=-1, keepdims=True)
        nxblk_ref[...] = jnp.broadcast_to(total * (1.0 / EXP_BLOCK), (8, LANES)).astype(jnp.int32)
        stab_ref[...] = jnp.zeros_like(stab_ref)
        ctab_ref[...] = jnp.zeros_like(ctab_ref)
        astart_ref[...] = jnp.zeros_like(astart_ref)
        nwin_ref[...] = jnp.zeros_like(nwin_ref)

    @pl.when(ph == 1)
    def _():
        r = lax.broadcasted_iota(jnp.int32, (TOK_TILE, TOK_TILE), 0)
        c = lax.broadcasted_iota(jnp.int32, (TOK_TILE, TOK_TILE), 1)
        lower = jnp.where(c < r, 1.0, 0.0).astype(BF16)
        for q in range(TM // TOK_TILE):
            i = step * (TM // TOK_TILE) + q
            sel = sm_ref[q * TOK_TILE:(q + 1) * TOK_TILE, :]
            cnt = jnp.sum(sel, axis=0, keepdims=True)
            prefix = jnp.dot(lower, sel.astype(BF16), preferred_element_type=F32)
            start = pstart_s[...] + base_s[...]
            dest = jnp.where(sel > 0.0, prefix + start, -1.0)
            destm_ref[q * TOK_TILE:(q + 1) * TOK_TILE, :] = dest
            destt_ref[:, q * TOK_TILE:(q + 1) * TOK_TILE] = dest.T
            has = (cnt > 0.0) & (lane < N_EXPERTS)
            stab_ref[pl.ds(i, 1), :] = start.astype(jnp.int32)
            ctab_ref[pl.ds(i, 1), :] = jnp.where(has, cnt, 0.0).astype(jnp.int32)
            a = jnp.minimum(jnp.floor(start * (1.0 / WIN_ALIGN)) * WIN_ALIGN, float(N_ROWS - WIN))
            nw = jnp.where(has, jnp.floor((start + cnt - a + (WIN - 1)) * (1.0 / WIN)), 0.0)
            posm_ref[q * TOK_TILE:(q + 1) * TOK_TILE, :] = jnp.where(sel > 0.0, prefix + start - a, -1.0)
            astart_ref[pl.ds(i, 1), :] = a.astype(jnp.int32)
            nwin_ref[pl.ds(i, 1), :] = nw.astype(jnp.int32)
            base_s[...] += cnt


def _plan_call(sm):
    tile = lambda ph, i: (i * ph, 0)
    tile_t = lambda ph, i: (0, i * ph)
    tab = jax.ShapeDtypeStruct((TAB_ROWS, LANES), jnp.int32)
    return pl.pallas_call(
        _plan_kernel,
        grid=(2, N_PAD // TM),
        in_specs=[pl.BlockSpec((TM, LANES), lambda ph, i: (i, 0))],
        out_specs=[
            pl.BlockSpec((TM, LANES), tile),
            pl.BlockSpec((TM, LANES), tile),
            pl.BlockSpec((LANES, TM), tile_t),
            _full((XBE_ROWS, LANES)), _full((XBE_ROWS, LANES)), _full((8, LANES)),
            _full((TAB_ROWS, LANES)), _full((TAB_ROWS, LANES)), _full((TAB_ROWS, LANES)), _full((TAB_ROWS, LANES)),
        ],
        out_shape=[
            jax.ShapeDtypeStruct((N_PAD, LANES), F32),
            jax.ShapeDtypeStruct((N_PAD, LANES), F32),
            jax.ShapeDtypeStruct((LANES, N_PAD), F32),
            jax.ShapeDtypeStruct((XBE_ROWS, LANES), jnp.int32),
            jax.ShapeDtypeStruct((XBE_ROWS, LANES), jnp.int32),
            jax.ShapeDtypeStruct((8, LANES), jnp.int32),
            tab, tab, tab, tab,
        ],
        scratch_shapes=[pltpu.VMEM((1, LANES), F32), pltpu.VMEM((1, LANES), F32)],
        compiler_params=pltpu.CompilerParams(
            dimension_semantics=("arbitrary", "arbitrary"), vmem_limit_bytes=VMEM_LIMIT),
        name="moe_plan",
    )(sm)


def _pack_rows(z):
    half = D_MODEL // 2
    lo = lax.bitcast_convert_type(z[:, :half], jnp.uint32) >> 16
    hi = lax.bitcast_convert_type(z[:, half:], jnp.uint32) & jnp.uint32(0xFFFF0000)
    return lax.bitcast_convert_type(hi | lo, jnp.int32)


def _unpack_rows(ref, rows=None):
    rows = ref.shape[0] if rows is None else rows
    flat = ref.reshape(ref.shape[0] * PACK, LANES)
    lo, hi = [], []
    for s in range(PACK):
        w = lax.bitcast_convert_type(flat[pl.ds(s, rows, stride=PACK), :], jnp.uint32)
        lo.append(lax.bitcast_convert_type(w << 16, F32))
        hi.append(lax.bitcast_convert_type(w & jnp.uint32(0xFFFF0000), F32))
    return jnp.concatenate(lo + hi, axis=-1).astype(BF16)


def _dispatch_kernel(stab_ref, ctab_ref, cmax_ref, xn_ref, destt_ref, xs_in, xs_ref,
                     stage0, stage1, stage2, sems, sem2):
    del xs_in
    i = pl.program_id(0)
    last = pl.num_programs(0) - 1
    x = xn_ref[...]
    dt = destt_ref[...]
    rio = lax.broadcasted_iota(jnp.int32, (DISP_CHUNK, 1), 0).astype(F32)

    def chunk_rows(j, stage):
        parts = []
        for e in range(N_EXPERTS):
            first = (stab_ref[i * N_EXPERTS + e] + j * DISP_CHUNK).astype(F32)
            parts.append(jnp.where(dt[e:e + 1, :] == first + rio, 1.0, 0.0).astype(BF16))
        onehot = jnp.concatenate(parts, axis=0)
        words = _pack_rows(jnp.dot(onehot, x, preferred_element_type=F32))
        for s in range(PACK):
            stage[pl.ds(s, N_EXPERTS * DISP_CHUNK, stride=PACK), :] = words[:, s * LANES:(s + 1) * LANES]

    def copy(stage, step, e, j, sem):
        first = stab_ref[step * N_EXPERTS + e] + j * DISP_CHUNK
        rows = stage.reshape(N_EXPERTS * DISP_CHUNK, PACK, LANES)
        return pltpu.make_async_copy(rows.at[pl.ds(e * DISP_CHUNK, DISP_CHUNK)],
                                     xs_ref.at[pl.ds(first, DISP_CHUNK)], sem)

    def step_body(stage, prev_stage, par):
        chunk_rows(0, stage)

        @pl.when(i > 0)
        def _():
            for e in range(N_EXPERTS):
                copy(prev_stage, i - 1, e, 0, sems.at[1 - par]).wait()

        for e in range(N_EXPERTS):
            copy(stage, i, e, 0, sems.at[par]).start()

        @pl.when(i == last)
        def _():
            for e in range(N_EXPERTS):
                copy(stage, i, e, 0, sems.at[par]).wait()

    @pl.when(i % 2 == 0)
    def _():
        step_body(stage0, stage1, 0)

    @pl.when(i % 2 == 1)
    def _():
        step_body(stage1, stage0, 1)

    for j in range(1, TOK_TILE // DISP_CHUNK):

        @pl.when(cmax_ref[i] > j * DISP_CHUNK)
        def _(j=j):
            chunk_rows(j, stage2)
            for e in range(N_EXPERTS):

                @pl.when(ctab_ref[i * N_EXPERTS + e] > j * DISP_CHUNK)
                def _(e=e):
                    cp = copy(stage2, i, e, j, sem2)
                    cp.start()
                    cp.wait()


def _dispatch_call(stab, ctab, cmax, xn, destt, xs_zero):
    stage = pltpu.VMEM((N_EXPERTS * DISP_CHUNK * PACK, LANES), jnp.int32)
    grid_spec = pltpu.PrefetchScalarGridSpec(
        num_scalar_prefetch=3,
        grid=(N_CTILES,),
        in_specs=[
            pl.BlockSpec((TOK_TILE, D_MODEL), lambda i, *_: (i, 0)),
            pl.BlockSpec((N_EXPERTS, TOK_TILE), lambda i, *_: (0, i)),
            pl.BlockSpec(memory_space=pl.ANY),
        ],
        out_specs=pl.BlockSpec(memory_space=pl.ANY),
        scratch_shapes=[stage, stage, stage, pltpu.SemaphoreType.DMA((2,)), pltpu.SemaphoreType.DMA],
    )
    return pl.pallas_call(
        _dispatch_kernel,
        grid_spec=grid_spec,
        out_shape=jax.ShapeDtypeStruct((XS_ROWS, PACK, LANES), jnp.int32),
        input_output_aliases={5: 0},
        compiler_params=pltpu.CompilerParams(
            dimension_semantics=("arbitrary",), vmem_limit_bytes=VMEM_LIMIT),
        name="moe_dispatch",
    )(stab, ctab, cmax, xn, destt, xs_zero)


def _expert_kernel(blke_ref, nblk_ref, live_ref, xs_ref, wgu_hbm, bgu_ref, wdn_hbm, bdn_ref,
                   ys_ref, wgu_f, wdn_f, wgu_s, wdn_s, sems):
    b = pl.program_id(0)
    used = b < nblk_ref[0]
    prev = blke_ref[jnp.maximum(b - 1, 0)]
    fresh = used & ((b == 0) | (blke_ref[b] != prev))

    def fetch(e):
        return (pltpu.make_async_copy(wgu_hbm.at[e], wgu_f, sems.at[0]),
                pltpu.make_async_copy(wdn_hbm.at[e], wdn_f, sems.at[1]))

    @pl.when(b == 0)
    def _():
        for cp in fetch(blke_ref[0]):
            cp.start()

    @pl.when(fresh)
    def _():
        for cp in fetch(blke_ref[b]):
            cp.wait()
        wgu_s[...] = wgu_f[...].astype(BF16)
        wdn_s[...] = wdn_f[...].astype(BF16)

        nxt = lax.while_loop(lambda p: (p < nblk_ref[0]) & (blke_ref[jnp.minimum(p, N_XBLOCKS - 1)] == blke_ref[b]),
                             lambda p: p + 1, b + 1)

        @pl.when(nxt < nblk_ref[0])
        def _():
            for cp in fetch(blke_ref[jnp.minimum(nxt, N_XBLOCKS - 1)]):
                cp.start()

    for groups in range(1, EXP_BLOCK // ROW_GROUP + 1):
        rows = groups * ROW_GROUP

        @pl.when(used & (live_ref[b] == groups))
        def _(rows=rows):
            hid = jnp.dot(_unpack_rows(xs_ref, rows), wgu_s[...], preferred_element_type=F32) + bgu_ref[...]
            gate = jnp.minimum(hid[:, :D_FF], SWIGLU_LIMIT)
            up = jnp.clip(hid[:, D_FF:], -SWIGLU_LIMIT, SWIGLU_LIMIT)
            act = (up + 1.0) * gate * jax.nn.sigmoid(SWIGLU_ALPHA * gate)
            y = jnp.dot(act.astype(BF16), wdn_s[...], preferred_element_type=F32) + bdn_ref[...]
            ys_ref[0:rows, :] = y.astype(BF16)
            if rows < EXP_BLOCK:
                ys_ref[rows:, :] = jnp.zeros((EXP_BLOCK - rows, D_MODEL), BF16)

    @pl.when(jnp.logical_not(used) | (live_ref[b] == 0))
    def _():
        ys_ref[...] = jnp.zeros_like(ys_ref)


def _expert_call(blke, nblk, live, xs, wgu, bgu, wdn, bdn):
    grid_spec = pltpu.PrefetchScalarGridSpec(
        num_scalar_prefetch=3,
        grid=(N_XBLOCKS,),
        in_specs=[
            pl.BlockSpec((EXP_BLOCK, PACK, LANES), lambda b, be, *_: (b, 0, 0)),
            pl.BlockSpec(memory_space=pl.ANY),
            pl.BlockSpec((None, 1, 2 * D_FF), lambda b, be, *_: (be[b], 0, 0)),
            pl.BlockSpec(memory_space=pl.ANY),
            pl.BlockSpec((None, 1, D_MODEL), lambda b, be, *_: (be[b], 0, 0)),
        ],
        out_specs=pl.BlockSpec((EXP_BLOCK, D_MODEL), lambda b, be, *_: (b, 0)),
        scratch_shapes=[pltpu.VMEM((D_MODEL, 2 * D_FF), F32), pltpu.VMEM((D_FF, D_MODEL), F32),
                        pltpu.VMEM((D_MODEL, 2 * D_FF), BF16), pltpu.VMEM((D_FF, D_MODEL), BF16),
                        pltpu.SemaphoreType.DMA((2,))],
    )
    return pl.pallas_call(
        _expert_kernel,
        grid_spec=grid_spec,
        out_shape=jax.ShapeDtypeStruct((N_ROWS, D_MODEL), BF16),
        compiler_params=pltpu.CompilerParams(
            dimension_semantics=("arbitrary",), vmem_limit_bytes=VMEM_LIMIT),
        name="moe_experts",
    )(blke, nblk, live, xs, wgu, bgu, wdn, bdn)


def _combine_kernel(astart_ref, nwin_ref, over_ref,
                    destm_ref, posm_ref, gm_ref, rep_ref, col_ref, h1_ref, plep_ref, ples_ref, gple_ref, wpg_ref, wpp_ref,
                    gfin_ref, ys_any, yp_ref, ysm_ref, wins, moe_s, tmp_s, wsems, sem):
    i = pl.program_id(0)
    slot = i % 2

    def win_copy(step, e, into):
        first = pl.multiple_of(astart_ref[step * N_EXPERTS + e], WIN_ALIGN)
        return pltpu.make_async_copy(ys_any.at[pl.ds(first, WIN)], wins.at[into, pl.ds(e * WIN, WIN)],
                                     wsems.at[into])

    @pl.when(i == 0)
    def _():
        for e in range(N_EXPERTS):
            win_copy(0, e, 0).start()

    @pl.when(i + 1 < pl.num_programs(0))
    def _():
        for e in range(N_EXPERTS):
            win_copy(i + 1, e, 1 - slot).start()

    for e in range(N_EXPERTS):
        win_copy(i, e, slot).wait()

    gates = gm_ref[...]
    rep = rep_ref[...]
    pos_rep = jnp.dot(posm_ref[...].astype(BF16), rep, preferred_element_type=F32)
    gate_rep = jnp.dot(gates.astype(BF16), rep, preferred_element_type=F32)
    gsel = jnp.where(pos_rep == col_ref[...], gate_rep, 0.0).astype(BF16)
    mrows = pl.ds(pl.multiple_of((i % TAIL_TILES) * TOK_TILE, TOK_TILE), TOK_TILE)
    moe_s[mrows, :] = jnp.dot(gsel, wins[slot], preferred_element_type=F32)

    @pl.when(over_ref[i] > 0)
    def _():
        dest = destm_ref[...]
        lane = _lane_iota((TOK_TILE, LANES))
        lane_f = lane.astype(F32)
        tmp_s[...] = jnp.zeros_like(tmp_s)

        def per_expert(e, carry):
            a = astart_ref[i * N_EXPERTS + e]
            dcol = jnp.sum(jnp.where(lane == e, dest, 0.0), axis=-1, keepdims=True)
            gcol = jnp.sum(jnp.where(lane == e, gates, 0.0), axis=-1, keepdims=True)

            def per_window(w, carry2):
                first = a + w * WIN
                start = pl.multiple_of(jnp.minimum(first, N_ROWS - WIN), WIN_ALIGN)
                cp = pltpu.make_async_copy(ys_any.at[pl.ds(start, WIN)], tmp_s.at[pl.ds(0, WIN)], sem)
                cp.start()
                cp.wait()
                hit = (lane < WIN) & (dcol == start.astype(F32) + lane_f) & (dcol >= first.astype(F32))
                sel = jnp.where(hit, gcol, 0.0).astype(BF16)
                moe_s[mrows, :] += jnp.dot(sel, tmp_s[...], preferred_element_type=F32)
                return carry2

            return lax.fori_loop(1, nwin_ref[i * N_EXPERTS + e], per_window, carry)

        lax.fori_loop(0, N_EXPERTS, per_expert, 0)

    def tail(h1, moe_rows, ple):
        h2 = h1 + moe_rows
        hn = _rms(h2, gple_ref[...]).astype(BF16)
        gate = jax.nn.sigmoid(jnp.dot(hn, wpg_ref[...], preferred_element_type=F32))
        proj = jnp.dot(ple.astype(BF16), wpp_ref[...], preferred_element_type=F32)
        return _rms(h2 + gate * proj, gfin_ref[...])

    is_sample = i == N_CTILES - 1

    @pl.when((i % TAIL_TILES == TAIL_TILES - 1) & jnp.logical_not(is_sample))
    def _():
        yp_ref[...] = tail(h1_ref[...], moe_s[...], plep_ref[...])

    @pl.when(is_sample)
    def _():
        ysm_ref[...] = tail(h1_ref[0:DEC_BATCH, :], moe_s[0:DEC_BATCH, :], ples_ref[...])


def _combine_call(astart, nwin, over, ys, destm, posm, gm, h1, plep, ples, gple, wpg, wpp, gfin):
    rep = (jnp.arange(N_EXPERTS * WIN)[None, :] // WIN == jnp.arange(LANES)[:, None]).astype(BF16)
    col = (jnp.arange(N_EXPERTS * WIN) % WIN).astype(F32).reshape(1, -1)
    rows = TAIL_TILES * TOK_TILE
    last = N_PROMPT // rows - 1

    grid_spec = pltpu.PrefetchScalarGridSpec(
        num_scalar_prefetch=3,
        grid=(N_CTILES,),
        in_specs=[
            pl.BlockSpec((TOK_TILE, LANES), lambda i, *_: (i, 0)),
            pl.BlockSpec((TOK_TILE, LANES), lambda i, *_: (i, 0)),
            pl.BlockSpec((TOK_TILE, LANES), lambda i, *_: (i, 0)),
            pl.BlockSpec((LANES, N_EXPERTS * WIN), lambda i, *_: (0, 0)),
            pl.BlockSpec((1, N_EXPERTS * WIN), lambda i, *_: (0, 0)),
            pl.BlockSpec((rows, D_MODEL), lambda i, *_: (i // TAIL_TILES, 0)),
            pl.BlockSpec((rows, PLE_DIM), lambda i, *_: (jnp.minimum(i // TAIL_TILES, last), 0)),
            pl.BlockSpec((DEC_BATCH, PLE_DIM), lambda i, *_: (0, 0)),
            pl.BlockSpec((1, D_MODEL), lambda i, *_: (0, 0)),
            pl.BlockSpec((D_MODEL, D_MODEL), lambda i, *_: (0, 0)),
            pl.BlockSpec((PLE_DIM, D_MODEL), lambda i, *_: (0, 0)),
            pl.BlockSpec((1, D_MODEL), lambda i, *_: (0, 0)),
            pl.BlockSpec(memory_space=pl.ANY),
        ],
        out_specs=[
            pl.BlockSpec((rows, D_MODEL), lambda i, *_: (jnp.minimum(i // TAIL_TILES, last), 0)),
            pl.BlockSpec((DEC_BATCH, D_MODEL), lambda i, *_: (0, 0)),
        ],
        scratch_shapes=[pltpu.VMEM((2, N_EXPERTS * WIN, D_MODEL), BF16), pltpu.VMEM((rows, D_MODEL), F32),
                        pltpu.VMEM((LANES, D_MODEL), BF16), pltpu.SemaphoreType.DMA((2,)),
                        pltpu.SemaphoreType.DMA],
    )
    return pl.pallas_call(
        _combine_kernel,
        grid_spec=grid_spec,
        out_shape=[jax.ShapeDtypeStruct((N_PROMPT, D_MODEL), F32),
                   jax.ShapeDtypeStruct((DEC_BATCH, D_MODEL), F32)],
        compiler_params=pltpu.CompilerParams(
            dimension_semantics=("arbitrary",), vmem_limit_bytes=VMEM_LIMIT),
        name="moe_combine_tail",
    )(astart, nwin, over, destm, posm, gm, rep, col, h1, plep, ples, gple, wpg, wpp, gfin, ys)


def kernel(x_prompt, x_sample, cache_swa_k, cache_swa_v, p_prompt, p_sample, g_mix, w_in, ln_v_g, ln_v_b,
           w_sp, b_sp, sinks, g_out_a, g_out_b, w_o, g_moe, w_router, b_router, w_gu, b_gu, w_dn, b_dn,
           g_ple, w_ple_gate, w_ple_proj, g_final):
    l = 0
    row = lambda v: v.reshape(1, -1)
    win = w_in[l].astype(BF16)
    wo = w_o[l].astype(BF16)
    tril = jnp.tril(jnp.ones((CHUNK, CHUNK), dtype=bool))
    wsp = jnp.where(tril, w_sp[l], 0.0).astype(BF16)
    bsp = jnp.repeat(b_sp[l].T, HEAD_DIM, axis=1)
    w00 = row(jnp.repeat(w_sp[l][:, 0, 0], HEAD_DIM))
    b0 = row(jnp.repeat(b_sp[l][:, 0], HEAD_DIM))
    wr_hi = w_router[l].astype(BF16)
    wr_lo = (w_router[l] - wr_hi.astype(F32)).astype(BF16)
    wr = jnp.concatenate([wr_hi, wr_lo, jnp.zeros((D_MODEL, LANES - 2 * N_EXPERTS), BF16)], axis=1)
    br = row(jnp.concatenate([b_router[l], jnp.zeros((LANES - N_EXPERTS,), F32)]))
    common = (row(g_mix[l]), win, row(ln_v_g[l]), row(ln_v_b[l]))
    tail = (row(g_out_a[l]), row(g_out_b[l]), wo, row(g_moe[l]), wr, br)

    h1, xn, gm, sm, k_p, v_p = _prompt_call(
        x_prompt.reshape(N_PROMPT, D_MODEL), sinks[l], *common, wsp, bsp, *tail)
    chan = lambda a: jnp.transpose(a, (0, 2, 3, 1)).reshape(DEC_BATCH, KV_WIDTH, CHUNK)
    ck = chan(cache_swa_k[l])
    cv = chan(cache_swa_v[l])
    h1, xn, gm, sm, k_s, v_s, va_s = _decode_call(
        x_sample.reshape(DEC_BATCH, D_MODEL), ck, cv, sinks[l], *common, w00, b0, *tail, h1, xn, gm, sm)

    destm, posm, destt, xbe, live, nxblk, stab, ctab, astart, nwin = _plan_call(sm)
    flat = lambda tab, n: tab[:n, :N_EXPERTS].reshape(-1)
    over = (jnp.max(nwin[:N_CTILES, :N_EXPERTS], axis=1) > 1).astype(jnp.int32)
    cmax = jnp.max(ctab[:N_CTILES, :N_EXPERTS], axis=1)

    xs = _dispatch_call(flat(stab, N_CTILES), flat(ctab, N_CTILES), cmax, xn, destt[:N_EXPERTS],
                        jnp.zeros((XS_ROWS, PACK, LANES), jnp.int32))
    ys = _expert_call(xbe[:N_XBLOCKS, 0], nxblk[0, :1], live[:N_XBLOCKS, 0], xs,
                      w_gu[l], b_gu[l].reshape(N_EXPERTS, 1, 2 * D_FF), w_dn[l], b_dn[l].reshape(N_EXPERTS, 1, D_MODEL))
    y_p, y_s = _combine_call(
        flat(astart, N_CTILES), flat(nwin, N_CTILES), over, ys, destm, posm, gm, h1,
        p_prompt[l].reshape(N_PROMPT, PLE_DIM), p_sample[l].reshape(DEC_BATCH, PLE_DIM),
        row(g_ple[l]), w_ple_gate[l].astype(BF16), w_ple_proj[l].astype(BF16), row(g_final))

    kv5 = lambda a, n: a.reshape(1, n, CHUNK, 2, HEAD_DIM)
    win5 = lambda a: jnp.transpose(a.reshape(DEC_BATCH, 2, HEAD_DIM, CHUNK), (0, 3, 1, 2))[None]
    return (y_p.reshape(BATCH, SEQ, D_MODEL), y_s.reshape(DEC_BATCH, 1, D_MODEL),
            kv5(k_p, BATCH), kv5(v_p, BATCH), win5(k_s), win5(v_s),
            va_s.reshape(1, DEC_BATCH, 1, A_WIDTH))
```

```python
import math

import jax
import jax.numpy as jnp
from jax import lax
from jax.experimental import pallas as pl
from jax.experimental.pallas import tpu as pltpu

F32 = jnp.float32
BF16 = jnp.bfloat16

D_MODEL = 1024
BATCH = 4
SEQ = 4096
DEC_BATCH = 128
HEAD_DIM = 64
A_WIDTH = 512
B_WIDTH = 512
B_HEADS = 8
KV_WIDTH = 128
IN_WIDTH = 2 * A_WIDTH + B_WIDTH + 2 * KV_WIDTH
CHUNK = 128
N_EXPERTS = 32
TOP_K = 4
D_FF = 1024
SWIGLU_ALPHA = 1.702
SWIGLU_LIMIT = 7.0
PLE_DIM = 256
EPS = 1e-5

LANES = 128
VMEM_LIMIT = 56 * 1024 * 1024
N_PROMPT = BATCH * SEQ
N_TOK = N_PROMPT + DEC_BATCH
TM = 1024
N_PAD = ((N_TOK + TM - 1) // TM) * TM
TOK_TILE = 128
N_PAD_TILES = N_PAD // TOK_TILE
N_CTILES = N_PROMPT // TOK_TILE + 1
DEC_TILE = 16
EXP_BLOCK = 512
ROW_GROUP = 128
DISP_CHUNK = 32
N_XBLOCKS = (N_TOK * TOP_K + N_EXPERTS * (DISP_CHUNK + EXP_BLOCK - 1) + EXP_BLOCK - 1) // EXP_BLOCK
N_ROWS = N_XBLOCKS * EXP_BLOCK
XS_ROWS = N_ROWS + N_EXPERTS * DISP_CHUNK
ZERO_ROWS = EXP_BLOCK
ZERO_COPIES = XS_ROWS // ZERO_ROWS
ZERO_STEPS = N_PROMPT // TM - 1
ZERO_PER_STEP = ZERO_COPIES // ZERO_STEPS
assert ZERO_PER_STEP * ZERO_STEPS == ZERO_COPIES and ZERO_COPIES * ZERO_ROWS == XS_ROWS
XBE_ROWS = ((N_XBLOCKS + 7) // 8) * 8
TAB_ROWS = ((N_PAD_TILES + 7) // 8) * 8
PACK = D_MODEL // 2 // LANES
WIN = 64
WIN_ALIGN = 16
TAIL_TILES = 4
NEG = -1e30


def _rms(x, g):
    return x * lax.rsqrt(jnp.mean(x * x, axis=-1, keepdims=True) + EPS) * g


def _gelu(x):
    c = math.sqrt(2.0 / math.pi)
    return x * (0.5 * (1.0 + jnp.tanh(c * (x + 0.044715 * (x * x * x)))))


def _layernorm(x, g, b):
    mu = jnp.mean(x, axis=-1, keepdims=True)
    xc = x - mu
    return xc * lax.rsqrt(jnp.mean(xc * xc, axis=-1, keepdims=True) + EPS) * g + b


def _lane_iota(shape):
    return lax.broadcasted_iota(jnp.int32, shape, len(shape) - 1)


def _full(shape):
    n = len(shape)
    return pl.BlockSpec(shape, lambda *_: (0,) * n)


def _route(xn2, wr_ref, br_ref):
    m = xn2.shape[0]
    xh = xn2.astype(BF16)
    xl = (xn2 - xh.astype(F32)).astype(BF16)
    r = jnp.dot(jnp.concatenate([xh, xl], axis=0), wr_ref[...], preferred_element_type=F32)
    r = r[:m] + r[m:]
    lane = _lane_iota((m, LANES))
    lane_f = lane.astype(F32)
    logits = jnp.where(lane < N_EXPERTS, r + pltpu.roll(r, LANES - N_EXPERTS, 1) + br_ref[...], NEG)
    work = logits
    sel = jnp.zeros((m, LANES), F32)
    top = None
    z = None
    for _ in range(TOP_K):
        mx = jnp.max(work, axis=-1, keepdims=True)
        first = jnp.min(jnp.where(work == mx, lane_f, float(LANES)), axis=-1, keepdims=True)
        hit = lane_f == first
        sel = jnp.where(hit, 1.0, sel)
        work = jnp.where(hit, NEG, work)
        if top is None:
            top = mx
            z = jnp.ones_like(mx)
        else:
            z = z + jnp.exp(mx - top)
    gates = jnp.where(sel > 0.0, jnp.exp(logits - top) / z, 0.0)
    return gates, sel


def _prompt_kernel(sinks_ref, x_ref, gmix_ref, win_ref, lng_ref, lnb_ref, wsp_ref, bsp_ref,
                   goa_ref, gob_ref, wo_ref, gmoe_ref, wr_ref, br_ref,
                   h1_ref, xn_ref, gm_ref, sm_ref, k_ref, v_ref, xs_ref,
                   z_s, kv_s, cat_s, zero_s, zsems):
    g = pl.program_id(0)

    def zero_copy(step, k):
        chunk = step * ZERO_PER_STEP + k
        return pltpu.make_async_copy(zero_s, xs_ref.at[pl.ds(chunk * ZERO_ROWS, ZERO_ROWS)], zsems.at[step % 2])

    @pl.when(g == 0)
    def _():
        zero_s[...] = jnp.zeros_like(zero_s)

    @pl.when((g >= 1) & (g <= ZERO_STEPS))
    def _():
        for k in range(ZERO_PER_STEP):
            zero_copy(g - 1, k).wait()

    @pl.when(g < ZERO_STEPS)
    def _():
        for k in range(ZERO_PER_STEP):
            zero_copy(g, k).start()

    @pl.when(g >= N_PROMPT // TM)
    def _():
        h1_ref[...] = jnp.zeros_like(h1_ref)
        xn_ref[...] = jnp.zeros_like(xn_ref)
        gm_ref[...] = jnp.zeros_like(gm_ref)
        sm_ref[...] = jnp.zeros_like(sm_ref)

    @pl.when(g < N_PROMPT // TM)
    def _():
        _prompt_tile(g % (SEQ // TM), sinks_ref, x_ref, gmix_ref, win_ref, lng_ref, lnb_ref, wsp_ref, bsp_ref,
                     goa_ref, gob_ref, wo_ref, gmoe_ref, wr_ref, br_ref,
                     h1_ref, xn_ref, gm_ref, sm_ref, k_ref, v_ref, z_s, kv_s, cat_s)


def _prompt_tile(j, sinks_ref, x_ref, gmix_ref, win_ref, lng_ref, lnb_ref, wsp_ref, bsp_ref,
                 goa_ref, gob_ref, wo_ref, gmoe_ref, wr_ref, br_ref,
                 h1_ref, xn_ref, gm_ref, sm_ref, k_ref, v_ref, z_s, kv_s, cat_s):
    @pl.when(j == 0)
    def _():
        kv_s[0:CHUNK, :] = jnp.zeros((CHUNK, 2 * KV_WIDTH), F32)

    xn = _rms(x_ref[...], gmix_ref[...]).astype(BF16)
    z_s[...] = jnp.dot(xn, win_ref[...], preferred_element_type=F32)
    kv_s[CHUNK:, :] = z_s[:, 2 * A_WIDTH + B_WIDTH:]

    lane = _lane_iota((CHUNK, LANES))
    lo = lane < HEAD_DIM
    lane2 = _lane_iota((2 * CHUNK, LANES))
    lo2 = lane2 < HEAD_DIM
    qi = lax.broadcasted_iota(jnp.int32, (CHUNK, CHUNK), 0)
    kc = lax.broadcasted_iota(jnp.int32, (CHUNK, CHUNK), 1)
    from_prev = kc > qi
    dist = jnp.where(from_prev, qi + CHUNK - kc, qi - kc).astype(F32)

    def chunk_body(c, carry):
        r0 = pl.multiple_of(c * CHUNK, CHUNK)
        rows = pl.ds(r0, CHUNK)
        u = _gelu(z_s[rows, 0:A_WIDTH])
        va = _layernorm(_gelu(z_s[rows, A_WIDTH:2 * A_WIDTH]), lng_ref[...], lnb_ref[...])
        vab = va.astype(BF16)
        slabs = []
        for p in range(A_WIDTH // LANES):
            slab = vab[:, p * LANES:(p + 1) * LANES]
            m0 = jnp.dot(wsp_ref[2 * p], slab, preferred_element_type=F32)
            m1 = jnp.dot(wsp_ref[2 * p + 1], slab, preferred_element_type=F32)
            slabs.append(jnp.where(lo, m0, m1))
        ya = u * (jnp.concatenate(slabs, axis=-1) + bsp_ref[...])
        ya_n = _rms(ya, goa_ref[...])
        k2 = kv_s[pl.ds(r0, 2 * CHUNK), 0:KV_WIDTH]
        v2 = kv_s[pl.ds(r0, 2 * CHUNK), KV_WIDTH:2 * KV_WIDTH]
        k2r = pltpu.roll(k2, HEAD_DIM, 1)
        v2r = pltpu.roll(v2, HEAD_DIM, 1)
        kd = (jnp.where(lo2, k2, k2r).astype(BF16), jnp.where(lo2, k2r, k2).astype(BF16))
        vd = (jnp.where(lo2, v2, v2r).astype(BF16), jnp.where(lo2, v2r, v2).astype(BF16))
        prev_ok = (j > 0) | (c > 0)
        masked = from_prev & jnp.logical_not(prev_ok)
        yb_slabs = []
        for kv in range(2):
            q0 = z_s[rows, 2 * A_WIDTH + (2 * kv) * LANES:2 * A_WIDTH + (2 * kv + 1) * LANES]
            q1 = z_s[rows, 2 * A_WIDTH + (2 * kv + 1) * LANES:2 * A_WIDTH + (2 * kv + 2) * LANES]
            lhs = jnp.concatenate([jnp.where(lo, q0, 0.0), jnp.where(lo, 0.0, q0),
                                   jnp.where(lo, q1, 0.0), jnp.where(lo, 0.0, q1)], axis=0).astype(BF16)
            s_all = lax.dot_general(lhs, kd[kv], (((1,), (1,)), ((), ())), preferred_element_type=F32)
            probs = []
            for i in range(4):
                h = 4 * kv + i
                slope = 2.0 ** (-(h + 1))
                sink = sinks_ref[h]
                sh = s_all[i * CHUNK:(i + 1) * CHUNK]
                s = jnp.where(from_prev, sh[:, :CHUNK], sh[:, CHUNK:]) * (HEAD_DIM ** -0.5) - slope * dist
                s = jnp.where(masked, NEG, s)
                mx = jnp.maximum(jnp.max(s, axis=-1, keepdims=True), sink)
                e = jnp.exp(s - mx)
                den = jnp.sum(e, axis=-1, keepdims=True) + jnp.exp(sink - mx)
                p = e * (1.0 / den)
                probs.append(jnp.concatenate([jnp.where(from_prev, p, 0.0), jnp.where(from_prev, 0.0, p)], axis=-1))
            pm = jnp.concatenate(probs, axis=0).astype(BF16)
            o = jnp.dot(pm, vd[kv], preferred_element_type=F32)
            yb_slabs.append(jnp.where(lo, o[0:CHUNK], o[CHUNK:2 * CHUNK]))
            yb_slabs.append(jnp.where(lo, o[2 * CHUNK:3 * CHUNK], o[3 * CHUNK:4 * CHUNK]))
        yb_n = _rms(jnp.concatenate(yb_slabs, axis=-1), gob_ref[...])
        cat_s[rows, 0:A_WIDTH] = ya_n.astype(BF16)
        cat_s[rows, A_WIDTH:] = yb_n.astype(BF16)
        return carry

    lax.fori_loop(0, TM // CHUNK, chunk_body, 0)

    kv_s[0:CHUNK, :] = kv_s[TM:TM + CHUNK, :]
    k_ref[...] = kv_s[TM:TM + CHUNK, 0:KV_WIDTH]
    v_ref[...] = kv_s[TM:TM + CHUNK, KV_WIDTH:]

    h1 = x_ref[...] + jnp.dot(cat_s[...], wo_ref[...], preferred_element_type=F32)
    h1_ref[...] = h1
    xn2 = _rms(h1, gmoe_ref[...])
    xn_ref[...] = xn2.astype(BF16)
    gates, sel = _route(xn2, wr_ref, br_ref)
    gm_ref[...] = gates
    sm_ref[...] = sel


def _prompt_call(x, sinks, gmix, win, lng, lnb, wsp, bsp, goa, gob, wo, gmoe, wr, br):
    real = N_PROMPT // TM
    row = lambda g: (g, 0)
    seq = lambda g: (jnp.minimum(g, real - 1) // (SEQ // TM), 0, 0)
    return pl.pallas_call(
        _prompt_kernel,
        grid=(N_PAD // TM,),
        in_specs=[
            pl.BlockSpec(memory_space=pltpu.SMEM),
            pl.BlockSpec((TM, D_MODEL), lambda g: (jnp.minimum(g, real - 1), 0)),
            _full((1, D_MODEL)), _full((D_MODEL, IN_WIDTH)), _full((1, A_WIDTH)), _full((1, A_WIDTH)),
            _full((8, CHUNK, CHUNK)), _full((CHUNK, A_WIDTH)), _full((1, A_WIDTH)), _full((1, B_WIDTH)),
            _full((D_MODEL, D_MODEL)), _full((1, D_MODEL)), _full((D_MODEL, LANES)), _full((1, LANES)),
        ],
        out_specs=[
            pl.BlockSpec((TM, D_MODEL), row),
            pl.BlockSpec((TM, D_MODEL), row),
            pl.BlockSpec((TM, LANES), row),
            pl.BlockSpec((TM, LANES), row),
            pl.BlockSpec((None, CHUNK, KV_WIDTH), seq),
            pl.BlockSpec((None, CHUNK, KV_WIDTH), seq),
            pl.BlockSpec(memory_space=pl.ANY),
        ],
        out_shape=[
            jax.ShapeDtypeStruct((N_PAD, D_MODEL), F32),
            jax.ShapeDtypeStruct((N_PAD, D_MODEL), BF16),
            jax.ShapeDtypeStruct((N_PAD, LANES), F32),
            jax.ShapeDtypeStruct((N_PAD, LANES), F32),
            jax.ShapeDtypeStruct((BATCH, CHUNK, KV_WIDTH), F32),
            jax.ShapeDtypeStruct((BATCH, CHUNK, KV_WIDTH), F32),
            jax.ShapeDtypeStruct((XS_ROWS, PACK, LANES), jnp.int32),
        ],
        scratch_shapes=[
            pltpu.VMEM((TM, IN_WIDTH), F32),
            pltpu.VMEM((TM + CHUNK, 2 * KV_WIDTH), F32),
            pltpu.VMEM((TM, D_MODEL), BF16),
            pltpu.VMEM((ZERO_ROWS, PACK, LANES), jnp.int32),
            pltpu.SemaphoreType.DMA((2,)),
        ],
        compiler_params=pltpu.CompilerParams(
            dimension_semantics=("arbitrary",), vmem_limit_bytes=VMEM_LIMIT),
        name="prompt_premoe",
    )(sinks, x, gmix, win, lng, lnb, wsp, bsp, goa, gob, wo, gmoe, wr, br)


def _decode_kernel(sinks_ref, x_ref, ck_ref, cv_ref, gmix_ref, win_ref, lng_ref, lnb_ref, w00_ref, b0_ref,
                   goa_ref, gob_ref, wo_ref, gmoe_ref, wr_ref, br_ref,
                   h1_in, xn_in, gm_in, sm_in,
                   h1_ref, xn_ref, gm_ref, sm_ref, nk_ref, nv_ref, va_ref,
                   q_s, kn_s, vn_s, ya_s, yb_s, qm_s, o_s):
    del h1_in, xn_in, gm_in, sm_in
    i = pl.program_id(0)
    t = DEC_TILE

    @pl.when(i == 0)
    def _():
        xn = _rms(x_ref[...], gmix_ref[...]).astype(BF16)
        z = jnp.dot(xn, win_ref[...], preferred_element_type=F32)
        u = _gelu(z[:, 0:A_WIDTH])
        va = _layernorm(_gelu(z[:, A_WIDTH:2 * A_WIDTH]), lng_ref[...], lnb_ref[...])
        va_ref[...] = va
        ya_s[...] = _rms(u * (w00_ref[...] * va + b0_ref[...]), goa_ref[...])
        q_s[...] = z[:, 2 * A_WIDTH:2 * A_WIDTH + B_WIDTH]
        kn_s[...] = z[:, 2 * A_WIDTH + B_WIDTH:2 * A_WIDTH + B_WIDTH + KV_WIDTH]
        vn_s[...] = z[:, 2 * A_WIDTH + B_WIDTH + KV_WIDTH:]

    rows = pl.ds(pl.multiple_of(i * t, t), t)
    q = q_s[rows, :]
    kn = kn_s[rows, :]
    vn = vn_s[rows, :]
    lane = _lane_iota((t, LANES))
    lo = lane < HEAD_DIM
    stacked = []
    for h in range(B_HEADS):
        qh = jnp.where(lo if h % 2 == 0 else jnp.logical_not(lo), q[:, (h // 2) * LANES:(h // 2 + 1) * LANES], 0.0)
        stacked.append(pltpu.roll(qh, HEAD_DIM, 1) if h % 2 != h // 4 else qh)
    qm_s[...] = jnp.concatenate(stacked, axis=0)
    hrow = lax.broadcasted_iota(jnp.int32, (B_HEADS, 1), 0)
    slope = jnp.zeros((B_HEADS, 1), F32)
    sink = jnp.zeros((B_HEADS, 1), F32)
    for h in range(B_HEADS):
        slope = jnp.where(hrow == h, 2.0 ** (-(h + 1)), slope)
        sink = jnp.where(hrow == h, sinks_ref[h], sink)
    pos = _lane_iota((B_HEADS, CHUNK))
    bias = slope * (CHUNK - pos).astype(F32)
    on_kv = (_lane_iota((B_HEADS, LANES)) < HEAD_DIM) == (hrow < B_HEADS // 2)
    last_pos = _lane_iota((KV_WIDTH, CHUNK)) == CHUNK - 1
    last_row = lax.broadcasted_iota(jnp.int32, (CHUNK, KV_WIDTH), 0) == CHUNK - 1
    for b in range(t):
        qb = qm_s[pl.ds(b, B_HEADS, stride=t), :]
        kt = ck_ref[b]
        vt = cv_ref[b]
        s_c = jnp.dot(qb.astype(BF16), kt.astype(BF16), preferred_element_type=F32) * (HEAD_DIM ** -0.5) - bias
        s_c = jnp.where(pos >= 1, s_c, NEG)
        s_n = jnp.sum(qb * kn[b:b + 1, :], axis=-1, keepdims=True) * (HEAD_DIM ** -0.5)
        mx = jnp.maximum(jnp.maximum(jnp.max(s_c, axis=-1, keepdims=True), s_n), sink)
        e_c = jnp.exp(s_c - mx)
        e_n = jnp.exp(s_n - mx)
        inv = 1.0 / (jnp.sum(e_c, axis=-1, keepdims=True) + e_n + jnp.exp(sink - mx))
        o = lax.dot_general((e_c * inv).astype(BF16), vt.astype(BF16), (((1,), (1,)), ((), ())),
                            preferred_element_type=F32)
        o = o + (e_n * inv) * vn[b:b + 1, :]
        o_s[pl.ds(b, B_HEADS, stride=t), :] = jnp.where(on_kv, o, 0.0)
        nk_ref[b] = jnp.where(last_pos, jnp.where(last_row, kn[b:b + 1, :], 0.0).T, pltpu.roll(kt, CHUNK - 1, 1))
        nv_ref[b] = jnp.where(last_pos, jnp.where(last_row, vn[b:b + 1, :], 0.0).T, pltpu.roll(vt, CHUNK - 1, 1))
    slabs = []
    for p in range(B_WIDTH // LANES):
        pair = []
        for h in (2 * p, 2 * p + 1):
            oh = o_s[h * t:(h + 1) * t, :]
            pair.append(pltpu.roll(oh, HEAD_DIM, 1) if h % 2 != h // 4 else oh)
        slabs.append(pair[0] + pair[1])
    yb_s[rows, :] = jnp.concatenate(slabs, axis=-1)

    @pl.when(i == pl.num_programs(0) - 1)
    def _():
        yb_n = _rms(yb_s[...], gob_ref[...])
        cat = jnp.concatenate([ya_s[...], yb_n], axis=-1).astype(BF16)
        h1 = x_ref[...] + jnp.dot(cat, wo_ref[...], preferred_element_type=F32)
        xn2 = _rms(h1, gmoe_ref[...])
        gates, sel = _route(xn2, wr_ref, br_ref)
        h1_ref[...] = h1
        xn_ref[...] = xn2.astype(BF16)
        gm_ref[...] = gates
        sm_ref[...] = sel


def _decode_call(x, ck, cv, sinks, gmix, win, lng, lnb, w00, b0, goa, gob, wo, gmoe, wr, br, h1, xn, gm, sm):
    t = DEC_TILE
    cache = pl.BlockSpec((t, KV_WIDTH, CHUNK), lambda i: (i, 0, 0))
    tok = lambda width: pl.BlockSpec((DEC_BATCH, width), lambda i: (N_PROMPT // DEC_BATCH, 0))
    anyspec = pl.BlockSpec(memory_space=pl.ANY)
    return pl.pallas_call(
        _decode_kernel,
        grid=(DEC_BATCH // t,),
        in_specs=[
            pl.BlockSpec(memory_space=pltpu.SMEM),
            _full((DEC_BATCH, D_MODEL)), cache, cache,
            _full((1, D_MODEL)), _full((D_MODEL, IN_WIDTH)), _full((1, A_WIDTH)), _full((1, A_WIDTH)),
            _full((1, A_WIDTH)), _full((1, A_WIDTH)), _full((1, A_WIDTH)), _full((1, B_WIDTH)),
            _full((D_MODEL, D_MODEL)), _full((1, D_MODEL)), _full((D_MODEL, LANES)), _full((1, LANES)),
            anyspec, anyspec, anyspec, anyspec,
        ],
        out_specs=[tok(D_MODEL), tok(D_MODEL), tok(LANES), tok(LANES), cache, cache, _full((DEC_BATCH, A_WIDTH))],
        out_shape=[
            jax.ShapeDtypeStruct((N_PAD, D_MODEL), F32),
            jax.ShapeDtypeStruct((N_PAD, D_MODEL), BF16),
            jax.ShapeDtypeStruct((N_PAD, LANES), F32),
            jax.ShapeDtypeStruct((N_PAD, LANES), F32),
            jax.ShapeDtypeStruct((DEC_BATCH, KV_WIDTH, CHUNK), F32),
            jax.ShapeDtypeStruct((DEC_BATCH, KV_WIDTH, CHUNK), F32),
            jax.ShapeDtypeStruct((DEC_BATCH, A_WIDTH), F32),
        ],
        scratch_shapes=[pltpu.VMEM((DEC_BATCH, B_WIDTH), F32), pltpu.VMEM((DEC_BATCH, KV_WIDTH), F32),
                        pltpu.VMEM((DEC_BATCH, KV_WIDTH), F32), pltpu.VMEM((DEC_BATCH, A_WIDTH), F32),
                        pltpu.VMEM((DEC_BATCH, B_WIDTH), F32), pltpu.VMEM((B_HEADS * t, LANES), F32),
                        pltpu.VMEM((B_HEADS * t, LANES), F32)],
        input_output_aliases={16: 0, 17: 1, 18: 2, 19: 3},
        compiler_params=pltpu.CompilerParams(
            dimension_semantics=("arbitrary",), vmem_limit_bytes=VMEM_LIMIT),
        name="sample_premoe",
    )(sinks, x, ck, cv, gmix, win, lng, lnb, w00, b0, goa, gob, wo, gmoe, wr, br, h1, xn, gm, sm)


def _plan_kernel(sm_ref,
                 destm_ref, posm_ref, destt_ref, xbe_ref, live_ref, nxblk_ref, stab_ref, ctab_ref, astart_ref,
                 nwin_ref, base_s, pstart_s):
    ph = pl.program_id(0)
    step = pl.program_id(1)
    lane = _lane_iota((1, LANES))

    @pl.when((ph == 0) & (step == 0))
    def _():
        base_s[...] = jnp.zeros_like(base_s)

    @pl.when(ph == 0)
    def _():
        base_s[...] += jnp.sum(sm_ref[...], axis=0, keepdims=True)

    @pl.when((ph == 1) & (step == 0))
    def _():
        counts = base_s[...]
        padded = jnp.floor((counts + (DISP_CHUNK + EXP_BLOCK - 1)) * (1.0 / EXP_BLOCK)) * EXP_BLOCK
        padded = jnp.where(counts > 0.0, padded, 0.0)
        pend = padded
        for s in (1, 2, 4, 8, 16):
            pend = pend + jnp.where(lane >= s, pltpu.roll(pend, s, 1), 0.0)
        spare = (N_ROWS + lane * DISP_CHUNK).astype(F32)
        pstart_s[...] = jnp.where(counts > 0.0, pend - padded, spare)
        base_s[...] = jnp.zeros_like(base_s)
        brow = lax.broadcasted_iota(jnp.int32, (XBE_ROWS, LANES), 0).astype(F32) * EXP_BLOCK
        done = jnp.where((lane < N_EXPERTS) & (pend <= brow), 1.0, 0.0)
        be = jnp.minimum(jnp.sum(done, axis=-1, keepdims=True), N_EXPERTS - 1.0)
        xbe_ref[...] = jnp.broadcast_to(be, (XBE_ROWS, LANES)).astype(jnp.int32)
        real = jnp.clip(counts - (brow - (pend - padded)), 0.0, float(EXP_BLOCK))
        real = jnp.sum(jnp.where(lane.astype(F32) == be, real, 0.0), axis=-1, keepdims=True)
        groups = jnp.floor((real + (ROW_GROUP - 1)) * (1.0 / ROW_GROUP))
        live_ref[...] = jnp.broadcast_to(groups, (XBE_ROWS, LANES)).astype(jnp.int32)
        total = jnp.sum(jnp.where(lane == N_EXPERTS - 1, pend, 0.0), axis=-1, keepdims=True)
        nxblk_ref[...] = jnp.broadcast_to(total * (1.0 / EXP_BLOCK), (8, LANES)).astype(jnp.int32)
        stab_ref[...] = jnp.zeros_like(stab_ref)
        ctab_ref[...] = jnp.zeros_like(ctab_ref)
        astart_ref[...] = jnp.zeros_like(astart_ref)
        nwin_ref[...] = jnp.zeros_like(nwin_ref)

    @pl.when(ph == 1)
    def _():
        r = lax.broadcasted_iota(jnp.int32, (TOK_TILE, TOK_TILE), 0)
        c = lax.broadcasted_iota(jnp.int32, (TOK_TILE, TOK_TILE), 1)
        lower = jnp.where(c < r, 1.0, 0.0).astype(BF16)
        for q in range(TM // TOK_TILE):
            i = step * (TM // TOK_TILE) + q
            sel = sm_ref[q * TOK_TILE:(q + 1) * TOK_TILE, :]
            cnt = jnp.sum(sel, axis=0, keepdims=True)
            prefix = jnp.dot(lower, sel.astype(BF16), preferred_element_type=F32)
            start = pstart_s[...] + base_s[...]
            dest = jnp.where(sel > 0.0, prefix + start, -1.0)
            destm_ref[q * TOK_TILE:(q + 1) * TOK_TILE, :] = dest
            destt_ref[:, q * TOK_TILE:(q + 1) * TOK_TILE] = dest.T
            has = (cnt > 0.0) & (lane < N_EXPERTS)
            stab_ref[pl.ds(i, 1), :] = start.astype(jnp.int32)
            ctab_ref[pl.ds(i, 1), :] = jnp.where(has, cnt, 0.0).astype(jnp.int32)
            a = jnp.minimum(jnp.floor(start * (1.0 / WIN_ALIGN)) * WIN_ALIGN, float(N_ROWS - WIN))
            nw = jnp.where(has, jnp.floor((start + cnt - a + (WIN - 1)) * (1.0 / WIN)), 0.0)
            posm_ref[q * TOK_TILE:(q + 1) * TOK_TILE, :] = jnp.where(sel > 0.0, prefix + start - a, -1.0)
            astart_ref[pl.ds(i, 1), :] = a.astype(jnp.int32)
            nwin_ref[pl.ds(i, 1), :] = nw.astype(jnp.int32)
            base_s[...] += cnt


def _plan_call(sm):
    tile = lambda ph, i: (i * ph, 0)
    tile_t = lambda ph, i: (0, i * ph)
    tab = jax.ShapeDtypeStruct((TAB_ROWS, LANES), jnp.int32)
    return pl.pallas_call(
        _plan_kernel,
        grid=(2, N_PAD // TM),
        in_specs=[pl.BlockSpec((TM, LANES), lambda ph, i: (i, 0))],
        out_specs=[
            pl.BlockSpec((TM, LANES), tile),
            pl.BlockSpec((TM, LANES), tile),
            pl.BlockSpec((LANES, TM), tile_t),
            _full((XBE_ROWS, LANES)), _full((XBE_ROWS, LANES)), _full((8, LANES)),
            _full((TAB_ROWS, LANES)), _full((TAB_ROWS, LANES)), _full((TAB_ROWS, LANES)), _full((TAB_ROWS, LANES)),
        ],
        out_shape=[
            jax.ShapeDtypeStruct((N_PAD, LANES), F32),
            jax.ShapeDtypeStruct((N_PAD, LANES), F32),
            jax.ShapeDtypeStruct((LANES, N_PAD), F32),
            jax.ShapeDtypeStruct((XBE_ROWS, LANES), jnp.int32),
            jax.ShapeDtypeStruct((XBE_ROWS, LANES), jnp.int32),
            jax.ShapeDtypeStruct((8, LANES), jnp.int32),
            tab, tab, tab, tab,
        ],
        scratch_shapes=[pltpu.VMEM((1, LANES), F32), pltpu.VMEM((1, LANES), F32)],
        compiler_params=pltpu.CompilerParams(
            dimension_semantics=("arbitrary", "arbitrary"), vmem_limit_bytes=VMEM_LIMIT),
        name="moe_plan",
    )(sm)


def _pack_rows(z):
    half = D_MODEL // 2
    lo = lax.bitcast_convert_type(z[:, :half], jnp.uint32) >> 16
    hi = lax.bitcast_convert_type(z[:, half:], jnp.uint32) & jnp.uint32(0xFFFF0000)
    return lax.bitcast_convert_type(hi | lo, jnp.int32)


def _unpack_rows(ref, rows=None):
    rows = ref.shape[0] if rows is None else rows
    flat = ref.reshape(ref.shape[0] * PACK, LANES)
    lo, hi = [], []
    for s in range(PACK):
        w = lax.bitcast_convert_type(flat[pl.ds(s, rows, stride=PACK), :], jnp.uint32)
        lo.append(lax.bitcast_convert_type(w << 16, F32))
        hi.append(lax.bitcast_convert_type(w & jnp.uint32(0xFFFF0000), F32))
    return jnp.concatenate(lo + hi, axis=-1).astype(BF16)


def _dispatch_kernel(stab_ref, ctab_ref, cmax_ref, xn_ref, destt_ref, xs_in, xs_ref,
                     stage0, stage1, stage2, sems, sem2):
    del xs_in
    i = pl.program_id(0)
    last = pl.num_programs(0) - 1
    x = xn_ref[...]
    dt = destt_ref[...]
    rio = lax.broadcasted_iota(jnp.int32, (DISP_CHUNK, 1), 0).astype(F32)

    def chunk_rows(j, stage):
        parts = []
        for e in range(N_EXPERTS):
            first = (stab_ref[i * N_EXPERTS + e] + j * DISP_CHUNK).astype(F32)
            parts.append(jnp.where(dt[e:e + 1, :] == first + rio, 1.0, 0.0).astype(BF16))
        onehot = jnp.concatenate(parts, axis=0)
        words = _pack_rows(jnp.dot(onehot, x, preferred_element_type=F32))
        for s in range(PACK):
            stage[pl.ds(s, N_EXPERTS * DISP_CHUNK, stride=PACK), :] = words[:, s * LANES:(s + 1) * LANES]

    def copy(stage, step, e, j, sem):
        first = stab_ref[step * N_EXPERTS + e] + j * DISP_CHUNK
        rows = stage.reshape(N_EXPERTS * DISP_CHUNK, PACK, LANES)
        return pltpu.make_async_copy(rows.at[pl.ds(e * DISP_CHUNK, DISP_CHUNK)],
                                     xs_ref.at[pl.ds(first, DISP_CHUNK)], sem)

    def step_body(stage, prev_stage, par):
        chunk_rows(0, stage)

        @pl.when(i > 0)
        def _():
            for e in range(N_EXPERTS):
                copy(prev_stage, i - 1, e, 0, sems.at[1 - par]).wait()

        for e in range(N_EXPERTS):
            copy(stage, i, e, 0, sems.at[par]).start()

        @pl.when(i == last)
        def _():
            for e in range(N_EXPERTS):
                copy(stage, i, e, 0, sems.at[par]).wait()

    @pl.when(i % 2 == 0)
    def _():
        step_body(stage0, stage1, 0)

    @pl.when(i % 2 == 1)
    def _():
        step_body(stage1, stage0, 1)

    for j in range(1, TOK_TILE // DISP_CHUNK):

        @pl.when(cmax_ref[i] > j * DISP_CHUNK)
        def _(j=j):
            chunk_rows(j, stage2)
            for e in range(N_EXPERTS):

                @pl.when(ctab_ref[i * N_EXPERTS + e] > j * DISP_CHUNK)
                def _(e=e):
                    cp = copy(stage2, i, e, j, sem2)
                    cp.start()
                    cp.wait()


def _dispatch_call(stab, ctab, cmax, xn, destt, xs_zero):
    stage = pltpu.VMEM((N_EXPERTS * DISP_CHUNK * PACK, LANES), jnp.int32)
    grid_spec = pltpu.PrefetchScalarGridSpec(
        num_scalar_prefetch=3,
        grid=(N_CTILES,),
        in_specs=[
            pl.BlockSpec((TOK_TILE, D_MODEL), lambda i, *_: (i, 0)),
            pl.BlockSpec((N_EXPERTS, TOK_TILE), lambda i, *_: (0, i)),
            pl.BlockSpec(memory_space=pl.ANY),
        ],
        out_specs=pl.BlockSpec(memory_space=pl.ANY),
        scratch_shapes=[stage, stage, stage, pltpu.SemaphoreType.DMA((2,)), pltpu.SemaphoreType.DMA],
    )
    return pl.pallas_call(
        _dispatch_kernel,
        grid_spec=grid_spec,
        out_shape=jax.ShapeDtypeStruct((XS_ROWS, PACK, LANES), jnp.int32),
        input_output_aliases={5: 0},
        compiler_params=pltpu.CompilerParams(
            dimension_semantics=("arbitrary",), vmem_limit_bytes=VMEM_LIMIT),
        name="moe_dispatch",
    )(stab, ctab, cmax, xn, destt, xs_zero)


def _expert_kernel(blke_ref, nblk_ref, live_ref, xs_ref, wgu_hbm, bgu_ref, wdn_hbm, bdn_ref,
                   ys_ref, wgu_f, wdn_f, wgu_s, wdn_s, sems):
    b = pl.program_id(0)
    used = b < nblk_ref[0]
    prev = blke_ref[jnp.maximum(b - 1, 0)]
    fresh = used & ((b == 0) | (blke_ref[b] != prev))

    def fetch(e):
        return (pltpu.make_async_copy(wgu_hbm.at[e], wgu_f, sems.at[0]),
                pltpu.make_async_copy(wdn_hbm.at[e], wdn_f, sems.at[1]))

    @pl.when(b == 0)
    def _():
        for cp in fetch(blke_ref[0]):
            cp.start()

    @pl.when(fresh)
    def _():
        for cp in fetch(blke_ref[b]):
            cp.wait()
        wgu_s[...] = wgu_f[...].astype(BF16)
        wdn_s[...] = wdn_f[...].astype(BF16)

        nxt = lax.while_loop(lambda p: (p < nblk_ref[0]) & (blke_ref[jnp.minimum(p, N_XBLOCKS - 1)] == blke_ref[b]),
                             lambda p: p + 1, b + 1)

        @pl.when(nxt < nblk_ref[0])
        def _():
            for cp in fetch(blke_ref[jnp.minimum(nxt, N_XBLOCKS - 1)]):
                cp.start()

    for groups in range(1, EXP_BLOCK // ROW_GROUP + 1):
        rows = groups * ROW_GROUP

        @pl.when(used & (live_ref[b] == groups))
        def _(rows=rows):
            hid = jnp.dot(_unpack_rows(xs_ref, rows), wgu_s[...], preferred_element_type=F32) + bgu_ref[...]
            gate = jnp.minimum(hid[:, :D_FF], SWIGLU_LIMIT)
            up = jnp.clip(hid[:, D_FF:], -SWIGLU_LIMIT, SWIGLU_LIMIT)
            act = (up + 1.0) * gate * jax.nn.sigmoid(SWIGLU_ALPHA * gate)
            y = jnp.dot(act.astype(BF16), wdn_s[...], preferred_element_type=F32) + bdn_ref[...]
            ys_ref[0:rows, :] = y.astype(BF16)
            if rows < EXP_BLOCK:
                ys_ref[rows:, :] = jnp.zeros((EXP_BLOCK - rows, D_MODEL), BF16)

    @pl.when(jnp.logical_not(used) | (live_ref[b] == 0))
    def _():
        ys_ref[...] = jnp.zeros_like(ys_ref)


def _expert_call(blke, nblk, live, xs, wgu, bgu, wdn, bdn):
    grid_spec = pltpu.PrefetchScalarGridSpec(
        num_scalar_prefetch=3,
        grid=(N_XBLOCKS,),
        in_specs=[
            pl.BlockSpec((EXP_BLOCK, PACK, LANES), lambda b, be, *_: (b, 0, 0)),
            pl.BlockSpec(memory_space=pl.ANY),
            pl.BlockSpec((None, 1, 2 * D_FF), lambda b, be, *_: (be[b], 0, 0)),
            pl.BlockSpec(memory_space=pl.ANY),
            pl.BlockSpec((None, 1, D_MODEL), lambda b, be, *_: (be[b], 0, 0)),
        ],
        out_specs=pl.BlockSpec((EXP_BLOCK, D_MODEL), lambda b, be, *_: (b, 0)),
        scratch_shapes=[pltpu.VMEM((D_MODEL, 2 * D_FF), F32), pltpu.VMEM((D_FF, D_MODEL), F32),
                        pltpu.VMEM((D_MODEL, 2 * D_FF), BF16), pltpu.VMEM((D_FF, D_MODEL), BF16),
                        pltpu.SemaphoreType.DMA((2,))],
    )
    return pl.pallas_call(
        _expert_kernel,
        grid_spec=grid_spec,
        out_shape=jax.ShapeDtypeStruct((N_ROWS, D_MODEL), BF16),
        compiler_params=pltpu.CompilerParams(
            dimension_semantics=("arbitrary",), vmem_limit_bytes=VMEM_LIMIT),
        name="moe_experts",
    )(blke, nblk, live, xs, wgu, bgu, wdn, bdn)


def _combine_kernel(astart_ref, nwin_ref, over_ref,
                    destm_ref, posm_ref, gm_ref, rep_ref, h1_ref, plep_ref, ples_ref, gple_ref, wpg_ref, wpp_ref,
                    gfin_ref, ys_any, yp_ref, ysm_ref, wins, moe_s, tmp_s, wsems, sem):
    i = pl.program_id(0)
    slot = i % 2

    def win_copy(step, e, into):
        first = pl.multiple_of(astart_ref[step * N_EXPERTS + e], WIN_ALIGN)
        return pltpu.make_async_copy(ys_any.at[pl.ds(first, WIN)], wins.at[into, pl.ds(e * WIN, WIN)],
                                     wsems.at[into])

    @pl.when(i == 0)
    def _():
        for e in range(N_EXPERTS):
            win_copy(0, e, 0).start()

    @pl.when(i + 1 < pl.num_programs(0))
    def _():
        for e in range(N_EXPERTS):
            win_copy(i + 1, e, 1 - slot).start()

    for e in range(N_EXPERTS):
        win_copy(i, e, slot).wait()

    gates = gm_ref[...]
    rep = rep_ref[...]
    pos_rep = jnp.dot(posm_ref[...].astype(BF16), rep, preferred_element_type=F32)
    gate_rep = jnp.dot(gates.astype(BF16), rep, preferred_element_type=F32)
    within = (_lane_iota((TOK_TILE, N_EXPERTS * WIN)) % WIN).astype(F32)
    gsel = jnp.where(pos_rep == within, gate_rep, 0.0).astype(BF16)
    mrows = pl.ds(pl.multiple_of((i % TAIL_TILES) * TOK_TILE, TOK_TILE), TOK_TILE)
    moe_s[mrows, :] = jnp.dot(gsel, wins[slot], preferred_element_type=F32)

    @pl.when(over_ref[i] > 0)
    def _():
        dest = destm_ref[...]
        lane = _lane_iota((TOK_TILE, LANES))
        lane_f = lane.astype(F32)
        tmp_s[...] = jnp.zeros_like(tmp_s)

        def per_expert(e, carry):
            a = astart_ref[i * N_EXPERTS + e]
            dcol = jnp.sum(jnp.where(lane == e, dest, 0.0), axis=-1, keepdims=True)
            gcol = jnp.sum(jnp.where(lane == e, gates, 0.0), axis=-1, keepdims=True)

            def per_window(w, carry2):
                first = a + w * WIN
                start = pl.multiple_of(jnp.minimum(first, N_ROWS - WIN), WIN_ALIGN)
                cp = pltpu.make_async_copy(ys_any.at[pl.ds(start, WIN)], tmp_s.at[pl.ds(0, WIN)], sem)
                cp.start()
                cp.wait()
                hit = (lane < WIN) & (dcol == start.astype(F32) + lane_f) & (dcol >= first.astype(F32))
                sel = jnp.where(hit, gcol, 0.0).astype(BF16)
                moe_s[mrows, :] += jnp.dot(sel, tmp_s[...], preferred_element_type=F32)
                return carry2

            return lax.fori_loop(1, nwin_ref[i * N_EXPERTS + e], per_window, carry)

        lax.fori_loop(0, N_EXPERTS, per_expert, 0)

    def tail(h1, moe_rows, ple):
        h2 = h1 + moe_rows
        hn = _rms(h2, gple_ref[...]).astype(BF16)
        gate = jax.nn.sigmoid(jnp.dot(hn, wpg_ref[...], preferred_element_type=F32))
        proj = jnp.dot(ple.astype(BF16), wpp_ref[...], preferred_element_type=F32)
        return _rms(h2 + gate * proj, gfin_ref[...])

    is_sample = i == N_CTILES - 1

    @pl.when((i % TAIL_TILES == TAIL_TILES - 1) & jnp.logical_not(is_sample))
    def _():
        yp_ref[...] = tail(h1_ref[...], moe_s[...], plep_ref[...])

    @pl.when(is_sample)
    def _():
        ysm_ref[...] = tail(h1_ref[0:DEC_BATCH, :], moe_s[0:DEC_BATCH, :], ples_ref[...])


def _combine_call(astart, nwin, over, ys, destm, posm, gm, h1, plep, ples, gple, wpg, wpp, gfin):
    rep = (jnp.arange(N_EXPERTS * WIN)[None, :] // WIN == jnp.arange(LANES)[:, None]).astype(BF16)
    rows = TAIL_TILES * TOK_TILE
    last = N_PROMPT // rows - 1

    grid_spec = pltpu.PrefetchScalarGridSpec(
        num_scalar_prefetch=3,
        grid=(N_CTILES,),
        in_specs=[
            pl.BlockSpec((TOK_TILE, LANES), lambda i, *_: (i, 0)),
            pl.BlockSpec((TOK_TILE, LANES), lambda i, *_: (i, 0)),
            pl.BlockSpec((TOK_TILE, LANES), lambda i, *_: (i, 0)),
            pl.BlockSpec((LANES, N_EXPERTS * WIN), lambda i, *_: (0, 0)),
            pl.BlockSpec((rows, D_MODEL), lambda i, *_: (i // TAIL_TILES, 0)),
            pl.BlockSpec((rows, PLE_DIM), lambda i, *_: (jnp.minimum(i // TAIL_TILES, last), 0)),
            pl.BlockSpec((DEC_BATCH, PLE_DIM), lambda i, *_: (0, 0)),
            pl.BlockSpec((1, D_MODEL), lambda i, *_: (0, 0)),
            pl.BlockSpec((D_MODEL, D_MODEL), lambda i, *_: (0, 0)),
            pl.BlockSpec((PLE_DIM, D_MODEL), lambda i, *_: (0, 0)),
            pl.BlockSpec((1, D_MODEL), lambda i, *_: (0, 0)),
            pl.BlockSpec(memory_space=pl.ANY),
        ],
        out_specs=[
            pl.BlockSpec((rows, D_MODEL), lambda i, *_: (jnp.minimum(i // TAIL_TILES, last), 0)),
            pl.BlockSpec((DEC_BATCH, D_MODEL), lambda i, *_: (0, 0)),
        ],
        scratch_shapes=[pltpu.VMEM((2, N_EXPERTS * WIN, D_MODEL), BF16), pltpu.VMEM((rows, D_MODEL), F32),
                        pltpu.VMEM((2 * WIN, D_MODEL), BF16), pltpu.SemaphoreType.DMA((2,)),
                        pltpu.SemaphoreType.DMA],
    )
    return pl.pallas_call(
        _combine_kernel,
        grid_spec=grid_spec,
        out_shape=[jax.ShapeDtypeStruct((N_PROMPT, D_MODEL), F32),
                   jax.ShapeDtypeStruct((DEC_BATCH, D_MODEL), F32)],
        compiler_params=pltpu.CompilerParams(
            dimension_semantics=("arbitrary",), vmem_limit_bytes=VMEM_LIMIT),
        name="moe_combine_tail",
    )(astart, nwin, over, destm, posm, gm, rep, h1, plep, ples, gple, wpg, wpp, gfin, ys)


def kernel(x_prompt, x_sample, cache_swa_k, cache_swa_v, p_prompt, p_sample, g_mix, w_in, ln_v_g, ln_v_b,
           w_sp, b_sp, sinks, g_out_a, g_out_b, w_o, g_moe, w_router, b_router, w_gu, b_gu, w_dn, b_dn,
           g_ple, w_ple_gate, w_ple_proj, g_final):
    l = 0
    row = lambda v: v.reshape(1, -1)
    win = w_in[l].astype(BF16)
    wo = w_o[l].astype(BF16)
    tril = jnp.tril(jnp.ones((CHUNK, CHUNK), dtype=bool))
    wsp = jnp.where(tril, w_sp[l], 0.0).astype(BF16)
    bsp = jnp.repeat(b_sp[l].T, HEAD_DIM, axis=1)
    w00 = row(jnp.repeat(w_sp[l][:, 0, 0], HEAD_DIM))
    b0 = row(jnp.repeat(b_sp[l][:, 0], HEAD_DIM))
    wr_hi = w_router[l].astype(BF16)
    wr_lo = (w_router[l] - wr_hi.astype(F32)).astype(BF16)
    wr = jnp.concatenate([wr_hi, wr_lo, jnp.zeros((D_MODEL, LANES - 2 * N_EXPERTS), BF16)], axis=1)
    br = row(jnp.concatenate([b_router[l], jnp.zeros((LANES - N_EXPERTS,), F32)]))
    common = (row(g_mix[l]), win, row(ln_v_g[l]), row(ln_v_b[l]))
    tail = (row(g_out_a[l]), row(g_out_b[l]), wo, row(g_moe[l]), wr, br)

    h1, xn, gm, sm, k_p, v_p, xs_zero = _prompt_call(
        x_prompt.reshape(N_PROMPT, D_MODEL), sinks[l], *common, wsp, bsp, *tail)
    chan = lambda a: jnp.transpose(a, (0, 2, 3, 1)).reshape(DEC_BATCH, KV_WIDTH, CHUNK)
    ck = chan(cache_swa_k[l])
    cv = chan(cache_swa_v[l])
    h1, xn, gm, sm, k_s, v_s, va_s = _decode_call(
        x_sample.reshape(DEC_BATCH, D_MODEL), ck, cv, sinks[l], *common, w00, b0, *tail, h1, xn, gm, sm)

    destm, posm, destt, xbe, live, nxblk, stab, ctab, astart, nwin = _plan_call(sm)
    flat = lambda tab, n: tab[:n, :N_EXPERTS].reshape(-1)
    over = (jnp.max(nwin[:N_CTILES, :N_EXPERTS], axis=1) > 1).astype(jnp.int32)
    cmax = jnp.max(ctab[:N_CTILES, :N_EXPERTS], axis=1)

    xs = _dispatch_call(flat(stab, N_CTILES), flat(ctab, N_CTILES), cmax, xn, destt[:N_EXPERTS], xs_zero)
    ys = _expert_call(xbe[:N_XBLOCKS, 0], nxblk[0, :1], live[:N_XBLOCKS, 0], xs,
                      w_gu[l], b_gu[l].reshape(N_EXPERTS, 1, 2 * D_FF), w_dn[l], b_dn[l].reshape(N_EXPERTS, 1, D_MODEL))
    y_p, y_s = _combine_call(
        flat(astart, N_CTILES), flat(nwin, N_CTILES), over, ys, destm, posm, gm, h1,
        p_prompt[l].reshape(N_PROMPT, PLE_DIM), p_sample[l].reshape(DEC_BATCH, PLE_DIM),
        row(g_ple[l]), w_ple_gate[l].astype(BF16), w_ple_proj[l].astype(BF16), row(g_final))

    kv5 = lambda a, n: a.reshape(1, n, CHUNK, 2, HEAD_DIM)
    win5 = lambda a: jnp.transpose(a.reshape(DEC_BATCH, 2, HEAD_DIM, CHUNK), (0, 3, 1, 2))[None]
    return (y_p.reshape(BATCH, SEQ, D_MODEL), y_s.reshape(DEC_BATCH, 1, D_MODEL),
            kv5(k_p, BATCH), kv5(v_p, BATCH), win5(k_s), win5(v_s),
            va_s.reshape(1, DEC_BATCH, 1, A_WIDTH))
```

```python
import math

import jax
import jax.numpy as jnp
from jax import lax
from jax.experimental import pallas as pl
from jax.experimental.pallas import tpu as pltpu

F32 = jnp.float32
BF16 = jnp.bfloat16

D_MODEL = 1024
BATCH = 4
SEQ = 4096
DEC_BATCH = 128
HEAD_DIM = 64
A_WIDTH = 512
B_WIDTH = 512
B_HEADS = 8
KV_WIDTH = 128
IN_WIDTH = 2 * A_WIDTH + B_WIDTH + 2 * KV_WIDTH
CHUNK = 128
N_EXPERTS = 32
TOP_K = 4
D_FF = 1024
SWIGLU_ALPHA = 1.702
SWIGLU_LIMIT = 7.0
PLE_DIM = 256
EPS = 1e-5

LANES = 128
VMEM_LIMIT = 56 * 1024 * 1024
N_PROMPT = BATCH * SEQ
N_TOK = N_PROMPT + DEC_BATCH
TM = 1024
N_PAD = ((N_TOK + TM - 1) // TM) * TM
TOK_TILE = 128
N_PAD_TILES = N_PAD // TOK_TILE
N_CTILES = N_PROMPT // TOK_TILE + 1
DEC_TILE = 16
EXP_BLOCK = 512
ROW_GROUP = 128
DISP_CHUNK = 32
N_XBLOCKS = (N_TOK * TOP_K + N_EXPERTS * (DISP_CHUNK + EXP_BLOCK - 1) + EXP_BLOCK - 1) // EXP_BLOCK
N_ROWS = N_XBLOCKS * EXP_BLOCK
XS_ROWS = N_ROWS + N_EXPERTS * DISP_CHUNK
ZERO_ROWS = EXP_BLOCK
ZERO_COPIES = XS_ROWS // ZERO_ROWS
ZERO_STEPS = N_PROMPT // TM - 1
ZERO_PER_STEP = ZERO_COPIES // ZERO_STEPS
assert ZERO_PER_STEP * ZERO_STEPS == ZERO_COPIES and ZERO_COPIES * ZERO_ROWS == XS_ROWS
XBE_ROWS = ((N_XBLOCKS + 7) // 8) * 8
TAB_ROWS = ((N_PAD_TILES + 7) // 8) * 8
PACK = D_MODEL // 2 // LANES
WIN = 64
WIN_ALIGN = 16
TAIL_TILES = 4
NEG = -1e30


def _rms(x, g):
    return x * lax.rsqrt(jnp.mean(x * x, axis=-1, keepdims=True) + EPS) * g


def _gelu(x):
    c = math.sqrt(2.0 / math.pi)
    return x * (0.5 * (1.0 + jnp.tanh(c * (x + 0.044715 * (x * x * x)))))


def _layernorm(x, g, b):
    mu = jnp.mean(x, axis=-1, keepdims=True)
    xc = x - mu
    return xc * lax.rsqrt(jnp.mean(xc * xc, axis=-1, keepdims=True) + EPS) * g + b


def _lane_iota(shape):
    return lax.broadcasted_iota(jnp.int32, shape, len(shape) - 1)


def _full(shape):
    n = len(shape)
    return pl.BlockSpec(shape, lambda *_: (0,) * n)


def _route(xn2, wr_ref, br_ref):
    m = xn2.shape[0]
    xh = xn2.astype(BF16)
    xl = (xn2 - xh.astype(F32)).astype(BF16)
    r = jnp.dot(jnp.concatenate([xh, xl], axis=0), wr_ref[...], preferred_element_type=F32)
    r = r[:m] + r[m:]
    lane = _lane_iota((m, LANES))
    lane_f = lane.astype(F32)
    logits = jnp.where(lane < N_EXPERTS, r + pltpu.roll(r, LANES - N_EXPERTS, 1) + br_ref[...], NEG)
    work = logits
    sel = jnp.zeros((m, LANES), F32)
    top = None
    z = None
    for _ in range(TOP_K):
        mx = jnp.max(work, axis=-1, keepdims=True)
        first = jnp.min(jnp.where(work == mx, lane_f, float(LANES)), axis=-1, keepdims=True)
        hit = lane_f == first
        sel = jnp.where(hit, 1.0, sel)
        work = jnp.where(hit, NEG, work)
        if top is None:
            top = mx
            z = jnp.ones_like(mx)
        else:
            z = z + jnp.exp(mx - top)
    gates = jnp.where(sel > 0.0, jnp.exp(logits - top) / z, 0.0)
    return gates, sel


def _prompt_kernel(sinks_ref, x_ref, gmix_ref, win_ref, lng_ref, lnb_ref, wsp_ref, bsp_ref,
                   goa_ref, gob_ref, wo_ref, gmoe_ref, wr_ref, br_ref,
                   h1_ref, xn_ref, gm_ref, sm_ref, k_ref, v_ref, xs_ref, cnt_ref,
                   z_s, kv_s, cat_s, zero_s, zsems):
    g = pl.program_id(0)

    @pl.when(g == 0)
    def _():
        cnt_ref[...] = jnp.zeros_like(cnt_ref)

    def zero_copy(step, k):
        chunk = step * ZERO_PER_STEP + k
        return pltpu.make_async_copy(zero_s, xs_ref.at[pl.ds(chunk * ZERO_ROWS, ZERO_ROWS)], zsems.at[step % 2])

    @pl.when(g == 0)
    def _():
        zero_s[...] = jnp.zeros_like(zero_s)

    @pl.when((g >= 1) & (g <= ZERO_STEPS))
    def _():
        for k in range(ZERO_PER_STEP):
            zero_copy(g - 1, k).wait()

    @pl.when(g < ZERO_STEPS)
    def _():
        for k in range(ZERO_PER_STEP):
            zero_copy(g, k).start()

    @pl.when(g >= N_PROMPT // TM)
    def _():
        h1_ref[...] = jnp.zeros_like(h1_ref)
        xn_ref[...] = jnp.zeros_like(xn_ref)
        gm_ref[...] = jnp.zeros_like(gm_ref)
        sm_ref[...] = jnp.zeros_like(sm_ref)

    @pl.when(g < N_PROMPT // TM)
    def _():
        _prompt_tile(g % (SEQ // TM), sinks_ref, x_ref, gmix_ref, win_ref, lng_ref, lnb_ref, wsp_ref, bsp_ref,
                     goa_ref, gob_ref, wo_ref, gmoe_ref, wr_ref, br_ref,
                     h1_ref, xn_ref, gm_ref, sm_ref, k_ref, v_ref, z_s, kv_s, cat_s)
        cnt_ref[...] += jnp.broadcast_to(jnp.sum(sm_ref[...], axis=0, keepdims=True), cnt_ref.shape)


def _prompt_tile(j, sinks_ref, x_ref, gmix_ref, win_ref, lng_ref, lnb_ref, wsp_ref, bsp_ref,
                 goa_ref, gob_ref, wo_ref, gmoe_ref, wr_ref, br_ref,
                 h1_ref, xn_ref, gm_ref, sm_ref, k_ref, v_ref, z_s, kv_s, cat_s):
    @pl.when(j == 0)
    def _():
        kv_s[0:CHUNK, :] = jnp.zeros((CHUNK, 2 * KV_WIDTH), F32)

    xn = _rms(x_ref[...], gmix_ref[...]).astype(BF16)
    z_s[...] = jnp.dot(xn, win_ref[...], preferred_element_type=F32)
    kv_s[CHUNK:, :] = z_s[:, 2 * A_WIDTH + B_WIDTH:]

    lane = _lane_iota((CHUNK, LANES))
    lo = lane < HEAD_DIM
    lane2 = _lane_iota((2 * CHUNK, LANES))
    lo2 = lane2 < HEAD_DIM
    qi = lax.broadcasted_iota(jnp.int32, (CHUNK, CHUNK), 0)
    kc = lax.broadcasted_iota(jnp.int32, (CHUNK, CHUNK), 1)
    from_prev = kc > qi
    dist = jnp.where(from_prev, qi + CHUNK - kc, qi - kc).astype(F32)

    def chunk_body(c, carry):
        r0 = pl.multiple_of(c * CHUNK, CHUNK)
        rows = pl.ds(r0, CHUNK)
        u = _gelu(z_s[rows, 0:A_WIDTH])
        va = _layernorm(_gelu(z_s[rows, A_WIDTH:2 * A_WIDTH]), lng_ref[...], lnb_ref[...])
        vab = va.astype(BF16)
        slabs = []
        for p in range(A_WIDTH // LANES):
            slab = vab[:, p * LANES:(p + 1) * LANES]
            m0 = jnp.dot(wsp_ref[2 * p], slab, preferred_element_type=F32)
            m1 = jnp.dot(wsp_ref[2 * p + 1], slab, preferred_element_type=F32)
            slabs.append(jnp.where(lo, m0, m1))
        ya = u * (jnp.concatenate(slabs, axis=-1) + bsp_ref[...])
        ya_n = _rms(ya, goa_ref[...])
        k2 = kv_s[pl.ds(r0, 2 * CHUNK), 0:KV_WIDTH]
        v2 = kv_s[pl.ds(r0, 2 * CHUNK), KV_WIDTH:2 * KV_WIDTH]
        k2r = pltpu.roll(k2, HEAD_DIM, 1)
        v2r = pltpu.roll(v2, HEAD_DIM, 1)
        kd = (jnp.where(lo2, k2, k2r).astype(BF16), jnp.where(lo2, k2r, k2).astype(BF16))
        vd = (jnp.where(lo2, v2, v2r).astype(BF16), jnp.where(lo2, v2r, v2).astype(BF16))
        prev_ok = (j > 0) | (c > 0)
        masked = from_prev & jnp.logical_not(prev_ok)
        yb_slabs = []
        for kv in range(2):
            q0 = z_s[rows, 2 * A_WIDTH + (2 * kv) * LANES:2 * A_WIDTH + (2 * kv + 1) * LANES]
            q1 = z_s[rows, 2 * A_WIDTH + (2 * kv + 1) * LANES:2 * A_WIDTH + (2 * kv + 2) * LANES]
            lhs = jnp.concatenate([jnp.where(lo, q0, 0.0), jnp.where(lo, 0.0, q0),
                                   jnp.where(lo, q1, 0.0), jnp.where(lo, 0.0, q1)], axis=0).astype(BF16)
            s_all = lax.dot_general(lhs, kd[kv], (((1,), (1,)), ((), ())), preferred_element_type=F32)
            probs = []
            for i in range(4):
                h = 4 * kv + i
                slope = 2.0 ** (-(h + 1))
                sink = sinks_ref[h]
                sh = s_all[i * CHUNK:(i + 1) * CHUNK]
                s = jnp.where(from_prev, sh[:, :CHUNK], sh[:, CHUNK:]) * (HEAD_DIM ** -0.5) - slope * dist
                s = jnp.where(masked, NEG, s)
                mx = jnp.maximum(jnp.max(s, axis=-1, keepdims=True), sink)
                e = jnp.exp(s - mx)
                den = jnp.sum(e, axis=-1, keepdims=True) + jnp.exp(sink - mx)
                p = e * (1.0 / den)
                probs.append(jnp.concatenate([jnp.where(from_prev, p, 0.0), jnp.where(from_prev, 0.0, p)], axis=-1))
            pm = jnp.concatenate(probs, axis=0).astype(BF16)
            o = jnp.dot(pm, vd[kv], preferred_element_type=F32)
            yb_slabs.append(jnp.where(lo, o[0:CHUNK], o[CHUNK:2 * CHUNK]))
            yb_slabs.append(jnp.where(lo, o[2 * CHUNK:3 * CHUNK], o[3 * CHUNK:4 * CHUNK]))
        yb_n = _rms(jnp.concatenate(yb_slabs, axis=-1), gob_ref[...])
        cat_s[rows, 0:A_WIDTH] = ya_n.astype(BF16)
        cat_s[rows, A_WIDTH:] = yb_n.astype(BF16)
        return carry

    lax.fori_loop(0, TM // CHUNK, chunk_body, 0)

    kv_s[0:CHUNK, :] = kv_s[TM:TM + CHUNK, :]
    k_ref[...] = kv_s[TM:TM + CHUNK, 0:KV_WIDTH]
    v_ref[...] = kv_s[TM:TM + CHUNK, KV_WIDTH:]

    h1 = x_ref[...] + jnp.dot(cat_s[...], wo_ref[...], preferred_element_type=F32)
    h1_ref[...] = h1
    xn2 = _rms(h1, gmoe_ref[...])
    xn_ref[...] = xn2.astype(BF16)
    gates, sel = _route(xn2, wr_ref, br_ref)
    gm_ref[...] = gates
    sm_ref[...] = sel


def _prompt_call(x, sinks, gmix, win, lng, lnb, wsp, bsp, goa, gob, wo, gmoe, wr, br):
    real = N_PROMPT // TM
    row = lambda g: (g, 0)
    seq = lambda g: (jnp.minimum(g, real - 1) // (SEQ // TM), 0, 0)
    return pl.pallas_call(
        _prompt_kernel,
        grid=(N_PAD // TM,),
        in_specs=[
            pl.BlockSpec(memory_space=pltpu.SMEM),
            pl.BlockSpec((TM, D_MODEL), lambda g: (jnp.minimum(g, real - 1), 0)),
            _full((1, D_MODEL)), _full((D_MODEL, IN_WIDTH)), _full((1, A_WIDTH)), _full((1, A_WIDTH)),
            _full((8, CHUNK, CHUNK)), _full((CHUNK, A_WIDTH)), _full((1, A_WIDTH)), _full((1, B_WIDTH)),
            _full((D_MODEL, D_MODEL)), _full((1, D_MODEL)), _full((D_MODEL, LANES)), _full((1, LANES)),
        ],
        out_specs=[
            pl.BlockSpec((TM, D_MODEL), row),
            pl.BlockSpec((TM, D_MODEL), row),
            pl.BlockSpec((TM, LANES), row),
            pl.BlockSpec((TM, LANES), row),
            pl.BlockSpec((None, CHUNK, KV_WIDTH), seq),
            pl.BlockSpec((None, CHUNK, KV_WIDTH), seq),
            pl.BlockSpec(memory_space=pl.ANY),
            _full((8, LANES)),
        ],
        out_shape=[
            jax.ShapeDtypeStruct((N_PAD, D_MODEL), F32),
            jax.ShapeDtypeStruct((N_PAD, D_MODEL), BF16),
            jax.ShapeDtypeStruct((N_PAD, LANES), F32),
            jax.ShapeDtypeStruct((N_PAD, LANES), F32),
            jax.ShapeDtypeStruct((BATCH, CHUNK, KV_WIDTH), F32),
            jax.ShapeDtypeStruct((BATCH, CHUNK, KV_WIDTH), F32),
            jax.ShapeDtypeStruct((XS_ROWS, PACK, LANES), jnp.int32),
            jax.ShapeDtypeStruct((8, LANES), F32),
        ],
        scratch_shapes=[
            pltpu.VMEM((TM, IN_WIDTH), F32),
            pltpu.VMEM((TM + CHUNK, 2 * KV_WIDTH), F32),
            pltpu.VMEM((TM, D_MODEL), BF16),
            pltpu.VMEM((ZERO_ROWS, PACK, LANES), jnp.int32),
            pltpu.SemaphoreType.DMA((2,)),
        ],
        compiler_params=pltpu.CompilerParams(
            dimension_semantics=("arbitrary",), vmem_limit_bytes=VMEM_LIMIT),
        name="prompt_premoe",
    )(sinks, x, gmix, win, lng, lnb, wsp, bsp, goa, gob, wo, gmoe, wr, br)


def _decode_kernel(sinks_ref, x_ref, ck_ref, cv_ref, gmix_ref, win_ref, lng_ref, lnb_ref, w00_ref, b0_ref,
                   goa_ref, gob_ref, wo_ref, gmoe_ref, wr_ref, br_ref,
                   h1_in, xn_in, gm_in, sm_in, cnt_in,
                   h1_ref, xn_ref, gm_ref, sm_ref, nk_ref, nv_ref, va_ref, cnt_ref,
                   q_s, kn_s, vn_s, ya_s, yb_s, qm_s, o_s):
    del h1_in, xn_in, gm_in, sm_in
    i = pl.program_id(0)
    t = DEC_TILE

    @pl.when(i == 0)
    def _():
        xn = _rms(x_ref[...], gmix_ref[...]).astype(BF16)
        z = jnp.dot(xn, win_ref[...], preferred_element_type=F32)
        u = _gelu(z[:, 0:A_WIDTH])
        va = _layernorm(_gelu(z[:, A_WIDTH:2 * A_WIDTH]), lng_ref[...], lnb_ref[...])
        va_ref[...] = va
        ya_s[...] = _rms(u * (w00_ref[...] * va + b0_ref[...]), goa_ref[...])
        q_s[...] = z[:, 2 * A_WIDTH:2 * A_WIDTH + B_WIDTH]
        kn_s[...] = z[:, 2 * A_WIDTH + B_WIDTH:2 * A_WIDTH + B_WIDTH + KV_WIDTH]
        vn_s[...] = z[:, 2 * A_WIDTH + B_WIDTH + KV_WIDTH:]

    rows = pl.ds(pl.multiple_of(i * t, t), t)
    q = q_s[rows, :]
    kn = kn_s[rows, :]
    vn = vn_s[rows, :]
    lane = _lane_iota((t, LANES))
    lo = lane < HEAD_DIM
    stacked = []
    for h in range(B_HEADS):
        qh = jnp.where(lo if h % 2 == 0 else jnp.logical_not(lo), q[:, (h // 2) * LANES:(h // 2 + 1) * LANES], 0.0)
        stacked.append(pltpu.roll(qh, HEAD_DIM, 1) if h % 2 != h // 4 else qh)
    qm_s[...] = jnp.concatenate(stacked, axis=0)
    hrow = lax.broadcasted_iota(jnp.int32, (B_HEADS, 1), 0)
    slope = jnp.zeros((B_HEADS, 1), F32)
    sink = jnp.zeros((B_HEADS, 1), F32)
    for h in range(B_HEADS):
        slope = jnp.where(hrow == h, 2.0 ** (-(h + 1)), slope)
        sink = jnp.where(hrow == h, sinks_ref[h], sink)
    pos = _lane_iota((B_HEADS, CHUNK))
    bias = slope * (CHUNK - pos).astype(F32)
    on_kv = (_lane_iota((B_HEADS, LANES)) < HEAD_DIM) == (hrow < B_HEADS // 2)
    last_pos = _lane_iota((KV_WIDTH, CHUNK)) == CHUNK - 1
    last_row = lax.broadcasted_iota(jnp.int32, (CHUNK, KV_WIDTH), 0) == CHUNK - 1
    for b in range(t):
        qb = qm_s[pl.ds(b, B_HEADS, stride=t), :]
        kt = ck_ref[b]
        vt = cv_ref[b]
        s_c = jnp.dot(qb.astype(BF16), kt.astype(BF16), preferred_element_type=F32) * (HEAD_DIM ** -0.5) - bias
        s_c = jnp.where(pos >= 1, s_c, NEG)
        s_n = jnp.sum(qb * kn[b:b + 1, :], axis=-1, keepdims=True) * (HEAD_DIM ** -0.5)
        mx = jnp.maximum(jnp.maximum(jnp.max(s_c, axis=-1, keepdims=True), s_n), sink)
        e_c = jnp.exp(s_c - mx)
        e_n = jnp.exp(s_n - mx)
        inv = 1.0 / (jnp.sum(e_c, axis=-1, keepdims=True) + e_n + jnp.exp(sink - mx))
        o = lax.dot_general((e_c * inv).astype(BF16), vt.astype(BF16), (((1,), (1,)), ((), ())),
                            preferred_element_type=F32)
        o = o + (e_n * inv) * vn[b:b + 1, :]
        o_s[pl.ds(b, B_HEADS, stride=t), :] = jnp.where(on_kv, o, 0.0)
        nk_ref[b] = jnp.where(last_pos, jnp.where(last_row, kn[b:b + 1, :], 0.0).T, pltpu.roll(kt, CHUNK - 1, 1))
        nv_ref[b] = jnp.where(last_pos, jnp.where(last_row, vn[b:b + 1, :], 0.0).T, pltpu.roll(vt, CHUNK - 1, 1))
    slabs = []
    for p in range(B_WIDTH // LANES):
        pair = []
        for h in (2 * p, 2 * p + 1):
            oh = o_s[h * t:(h + 1) * t, :]
            pair.append(pltpu.roll(oh, HEAD_DIM, 1) if h % 2 != h // 4 else oh)
        slabs.append(pair[0] + pair[1])
    yb_s[rows, :] = jnp.concatenate(slabs, axis=-1)

    @pl.when(i == pl.num_programs(0) - 1)
    def _():
        yb_n = _rms(yb_s[...], gob_ref[...])
        cat = jnp.concatenate([ya_s[...], yb_n], axis=-1).astype(BF16)
        h1 = x_ref[...] + jnp.dot(cat, wo_ref[...], preferred_element_type=F32)
        xn2 = _rms(h1, gmoe_ref[...])
        gates, sel = _route(xn2, wr_ref, br_ref)
        h1_ref[...] = h1
        xn_ref[...] = xn2.astype(BF16)
        gm_ref[...] = gates
        sm_ref[...] = sel
        cnt_ref[...] = cnt_in[...] + jnp.broadcast_to(jnp.sum(sel, axis=0, keepdims=True), cnt_ref.shape)


def _decode_call(x, ck, cv, sinks, gmix, win, lng, lnb, w00, b0, goa, gob, wo, gmoe, wr, br, h1, xn, gm, sm, cnt):
    t = DEC_TILE
    cache = pl.BlockSpec((t, KV_WIDTH, CHUNK), lambda i: (i, 0, 0))
    tok = lambda width: pl.BlockSpec((DEC_BATCH, width), lambda i: (N_PROMPT // DEC_BATCH, 0))
    anyspec = pl.BlockSpec(memory_space=pl.ANY)
    return pl.pallas_call(
        _decode_kernel,
        grid=(DEC_BATCH // t,),
        in_specs=[
            pl.BlockSpec(memory_space=pltpu.SMEM),
            _full((DEC_BATCH, D_MODEL)), cache, cache,
            _full((1, D_MODEL)), _full((D_MODEL, IN_WIDTH)), _full((1, A_WIDTH)), _full((1, A_WIDTH)),
            _full((1, A_WIDTH)), _full((1, A_WIDTH)), _full((1, A_WIDTH)), _full((1, B_WIDTH)),
            _full((D_MODEL, D_MODEL)), _full((1, D_MODEL)), _full((D_MODEL, LANES)), _full((1, LANES)),
            anyspec, anyspec, anyspec, anyspec, _full((8, LANES)),
        ],
        out_specs=[tok(D_MODEL), tok(D_MODEL), tok(LANES), tok(LANES), cache, cache, _full((DEC_BATCH, A_WIDTH)),
                   _full((8, LANES))],
        out_shape=[
            jax.ShapeDtypeStruct((N_PAD, D_MODEL), F32),
            jax.ShapeDtypeStruct((N_PAD, D_MODEL), BF16),
            jax.ShapeDtypeStruct((N_PAD, LANES), F32),
            jax.ShapeDtypeStruct((N_PAD, LANES), F32),
            jax.ShapeDtypeStruct((DEC_BATCH, KV_WIDTH, CHUNK), F32),
            jax.ShapeDtypeStruct((DEC_BATCH, KV_WIDTH, CHUNK), F32),
            jax.ShapeDtypeStruct((DEC_BATCH, A_WIDTH), F32),
            jax.ShapeDtypeStruct((8, LANES), F32),
        ],
        scratch_shapes=[pltpu.VMEM((DEC_BATCH, B_WIDTH), F32), pltpu.VMEM((DEC_BATCH, KV_WIDTH), F32),
                        pltpu.VMEM((DEC_BATCH, KV_WIDTH), F32), pltpu.VMEM((DEC_BATCH, A_WIDTH), F32),
                        pltpu.VMEM((DEC_BATCH, B_WIDTH), F32), pltpu.VMEM((B_HEADS * t, LANES), F32),
                        pltpu.VMEM((B_HEADS * t, LANES), F32)],
        input_output_aliases={16: 0, 17: 1, 18: 2, 19: 3},
        compiler_params=pltpu.CompilerParams(
            dimension_semantics=("arbitrary",), vmem_limit_bytes=VMEM_LIMIT),
        name="sample_premoe",
    )(sinks, x, ck, cv, gmix, win, lng, lnb, w00, b0, goa, gob, wo, gmoe, wr, br, h1, xn, gm, sm, cnt)


def _plan_kernel(sm_ref, cnt_ref,
                 destm_ref, posm_ref, destt_ref, xbe_ref, live_ref, nxblk_ref, stab_ref, ctab_ref, astart_ref,
                 nwin_ref, base_s, pstart_s):
    step = pl.program_id(0)
    lane = _lane_iota((1, LANES))

    @pl.when(step == 0)
    def _():
        counts = cnt_ref[0:1, :]
        padded = jnp.floor((counts + (DISP_CHUNK + EXP_BLOCK - 1)) * (1.0 / EXP_BLOCK)) * EXP_BLOCK
        padded = jnp.where(counts > 0.0, padded, 0.0)
        pend = padded
        for s in (1, 2, 4, 8, 16):
            pend = pend + jnp.where(lane >= s, pltpu.roll(pend, s, 1), 0.0)
        spare = (N_ROWS + lane * DISP_CHUNK).astype(F32)
        pstart_s[...] = jnp.where(counts > 0.0, pend - padded, spare)
        base_s[...] = jnp.zeros_like(base_s)
        brow = lax.broadcasted_iota(jnp.int32, (XBE_ROWS, LANES), 0).astype(F32) * EXP_BLOCK
        done = jnp.where((lane < N_EXPERTS) & (pend <= brow), 1.0, 0.0)
        be = jnp.minimum(jnp.sum(done, axis=-1, keepdims=True), N_EXPERTS - 1.0)
        xbe_ref[...] = jnp.broadcast_to(be, (XBE_ROWS, LANES)).astype(jnp.int32)
        real = jnp.clip(counts - (brow - (pend - padded)), 0.0, float(EXP_BLOCK))
        real = jnp.sum(jnp.where(lane.astype(F32) == be, real, 0.0), axis=-1, keepdims=True)
        groups = jnp.floor((real + (ROW_GROUP - 1)) * (1.0 / ROW_GROUP))
        live_ref[...] = jnp.broadcast_to(groups, (XBE_ROWS, LANES)).astype(jnp.int32)
        total = jnp.sum(jnp.where(lane == N_EXPERTS - 1, pend, 0.0), axis=-1, keepdims=True)
        nxblk_ref[...] = jnp.broadcast_to(total * (1.0 / EXP_BLOCK), (8, LANES)).astype(jnp.int32)
        stab_ref[...] = jnp.zeros_like(stab_ref)
        ctab_ref[...] = jnp.zeros_like(ctab_ref)
        astart_ref[...] = jnp.zeros_like(astart_ref)
        nwin_ref[...] = jnp.zeros_like(nwin_ref)

    def walk():
        r = lax.broadcasted_iota(jnp.int32, (TOK_TILE, TOK_TILE), 0)
        c = lax.broadcasted_iota(jnp.int32, (TOK_TILE, TOK_TILE), 1)
        lower = jnp.where(c < r, 1.0, 0.0).astype(BF16)
        for q in range(TM // TOK_TILE):
            i = step * (TM // TOK_TILE) + q
            sel = sm_ref[q * TOK_TILE:(q + 1) * TOK_TILE, :]
            cnt = jnp.sum(sel, axis=0, keepdims=True)
            prefix = jnp.dot(lower, sel.astype(BF16), preferred_element_type=F32)
            start = pstart_s[...] + base_s[...]
            dest = jnp.where(sel > 0.0, prefix + start, -1.0)
            destm_ref[q * TOK_TILE:(q + 1) * TOK_TILE, :] = dest
            destt_ref[:, q * TOK_TILE:(q + 1) * TOK_TILE] = dest.T
            has = (cnt > 0.0) & (lane < N_EXPERTS)
            stab_ref[pl.ds(i, 1), :] = start.astype(jnp.int32)
            ctab_ref[pl.ds(i, 1), :] = jnp.where(has, cnt, 0.0).astype(jnp.int32)
            a = jnp.minimum(jnp.floor(start * (1.0 / WIN_ALIGN)) * WIN_ALIGN, float(N_ROWS - WIN))
            nw = jnp.where(has, jnp.floor((start + cnt - a + (WIN - 1)) * (1.0 / WIN)), 0.0)
            posm_ref[q * TOK_TILE:(q + 1) * TOK_TILE, :] = jnp.where(sel > 0.0, prefix + start - a, -1.0)
            astart_ref[pl.ds(i, 1), :] = a.astype(jnp.int32)
            nwin_ref[pl.ds(i, 1), :] = nw.astype(jnp.int32)
            base_s[...] += cnt

    walk()


def _plan_call(sm, cnt):
    tile = lambda i: (i, 0)
    tile_t = lambda i: (0, i)
    tab = jax.ShapeDtypeStruct((TAB_ROWS, LANES), jnp.int32)
    return pl.pallas_call(
        _plan_kernel,
        grid=(N_PAD // TM,),
        in_specs=[pl.BlockSpec((TM, LANES), lambda i: (i, 0)), _full((8, LANES))],
        out_specs=[
            pl.BlockSpec((TM, LANES), tile),
            pl.BlockSpec((TM, LANES), tile),
            pl.BlockSpec((LANES, TM), tile_t),
            _full((XBE_ROWS, LANES)), _full((XBE_ROWS, LANES)), _full((8, LANES)),
            _full((TAB_ROWS, LANES)), _full((TAB_ROWS, LANES)), _full((TAB_ROWS, LANES)), _full((TAB_ROWS, LANES)),
        ],
        out_shape=[
            jax.ShapeDtypeStruct((N_PAD, LANES), F32),
            jax.ShapeDtypeStruct((N_PAD, LANES), F32),
            jax.ShapeDtypeStruct((LANES, N_PAD), F32),
            jax.ShapeDtypeStruct((XBE_ROWS, LANES), jnp.int32),
            jax.ShapeDtypeStruct((XBE_ROWS, LANES), jnp.int32),
            jax.ShapeDtypeStruct((8, LANES), jnp.int32),
            tab, tab, tab, tab,
        ],
        scratch_shapes=[pltpu.VMEM((1, LANES), F32), pltpu.VMEM((1, LANES), F32)],
        compiler_params=pltpu.CompilerParams(
            dimension_semantics=("arbitrary",), vmem_limit_bytes=VMEM_LIMIT),
        name="moe_plan",
    )(sm, cnt)


def _pack_rows(z):
    half = D_MODEL // 2
    lo = lax.bitcast_convert_type(z[:, :half], jnp.uint32) >> 16
    hi = lax.bitcast_convert_type(z[:, half:], jnp.uint32) & jnp.uint32(0xFFFF0000)
    return lax.bitcast_convert_type(hi | lo, jnp.int32)


def _unpack_rows(ref, rows=None):
    rows = ref.shape[0] if rows is None else rows
    flat = ref.reshape(ref.shape[0] * PACK, LANES)
    lo, hi = [], []
    for s in range(PACK):
        w = lax.bitcast_convert_type(flat[pl.ds(s, rows, stride=PACK), :], jnp.uint32)
        lo.append(lax.bitcast_convert_type(w << 16, F32))
        hi.append(lax.bitcast_convert_type(w & jnp.uint32(0xFFFF0000), F32))
    return jnp.concatenate(lo + hi, axis=-1).astype(BF16)


def _dispatch_kernel(stab_ref, ctab_ref, cmax_ref, xn_ref, destt_ref, xs_in, xs_ref,
                     stage0, stage1, stage2, sems, sem2):
    del xs_in
    i = pl.program_id(0)
    last = pl.num_programs(0) - 1
    x = xn_ref[...]
    dt = destt_ref[...]
    rio = lax.broadcasted_iota(jnp.int32, (DISP_CHUNK, 1), 0).astype(F32)

    def chunk_rows(j, stage):
        parts = []
        for e in range(N_EXPERTS):
            first = (stab_ref[i * N_EXPERTS + e] + j * DISP_CHUNK).astype(F32)
            parts.append(jnp.where(dt[e:e + 1, :] == first + rio, 1.0, 0.0).astype(BF16))
        onehot = jnp.concatenate(parts, axis=0)
        words = _pack_rows(jnp.dot(onehot, x, preferred_element_type=F32))
        for s in range(PACK):
            stage[pl.ds(s, N_EXPERTS * DISP_CHUNK, stride=PACK), :] = words[:, s * LANES:(s + 1) * LANES]

    def copy(stage, step, e, j, sem):
        first = stab_ref[step * N_EXPERTS + e] + j * DISP_CHUNK
        rows = stage.reshape(N_EXPERTS * DISP_CHUNK, PACK, LANES)
        return pltpu.make_async_copy(rows.at[pl.ds(e * DISP_CHUNK, DISP_CHUNK)],
                                     xs_ref.at[pl.ds(first, DISP_CHUNK)], sem)

    def step_body(stage, prev_stage, par):
        chunk_rows(0, stage)

        @pl.when(i > 0)
        def _():
            for e in range(N_EXPERTS):
                copy(prev_stage, i - 1, e, 0, sems.at[1 - par]).wait()

        for e in range(N_EXPERTS):
            copy(stage, i, e, 0, sems.at[par]).start()

        @pl.when(i == last)
        def _():
            for e in range(N_EXPERTS):
                copy(stage, i, e, 0, sems.at[par]).wait()

    @pl.when(i % 2 == 0)
    def _():
        step_body(stage0, stage1, 0)

    @pl.when(i % 2 == 1)
    def _():
        step_body(stage1, stage0, 1)

    for j in range(1, TOK_TILE // DISP_CHUNK):

        @pl.when(cmax_ref[i] > j * DISP_CHUNK)
        def _(j=j):
            chunk_rows(j, stage2)
            for e in range(N_EXPERTS):

                @pl.when(ctab_ref[i * N_EXPERTS + e] > j * DISP_CHUNK)
                def _(e=e):
                    cp = copy(stage2, i, e, j, sem2)
                    cp.start()
                    cp.wait()


def _dispatch_call(stab, ctab, cmax, xn, destt, xs_zero):
    stage = pltpu.VMEM((N_EXPERTS * DISP_CHUNK * PACK, LANES), jnp.int32)
    grid_spec = pltpu.PrefetchScalarGridSpec(
        num_scalar_prefetch=3,
        grid=(N_CTILES,),
        in_specs=[
            pl.BlockSpec((TOK_TILE, D_MODEL), lambda i, *_: (i, 0)),
            pl.BlockSpec((N_EXPERTS, TOK_TILE), lambda i, *_: (0, i)),
            pl.BlockSpec(memory_space=pl.ANY),
        ],
        out_specs=pl.BlockSpec(memory_space=pl.ANY),
        scratch_shapes=[stage, stage, stage, pltpu.SemaphoreType.DMA((2,)), pltpu.SemaphoreType.DMA],
    )
    return pl.pallas_call(
        _dispatch_kernel,
        grid_spec=grid_spec,
        out_shape=jax.ShapeDtypeStruct((XS_ROWS, PACK, LANES), jnp.int32),
        input_output_aliases={5: 0},
        compiler_params=pltpu.CompilerParams(
            dimension_semantics=("arbitrary",), vmem_limit_bytes=VMEM_LIMIT),
        name="moe_dispatch",
    )(stab, ctab, cmax, xn, destt, xs_zero)


def _expert_kernel(blke_ref, nblk_ref, live_ref, xs_ref, wgu_hbm, bgu_ref, wdn_hbm, bdn_ref,
                   ys_ref, wgu_f, wdn_f, wgu_s, wdn_s, sems):
    b = pl.program_id(0)
    used = b < nblk_ref[0]
    prev = blke_ref[jnp.maximum(b - 1, 0)]
    fresh = used & ((b == 0) | (blke_ref[b] != prev))

    def fetch(e):
        return (pltpu.make_async_copy(wgu_hbm.at[e], wgu_f, sems.at[0]),
                pltpu.make_async_copy(wdn_hbm.at[e], wdn_f, sems.at[1]))

    @pl.when((b == 0) & used)
    def _():
        for cp in fetch(blke_ref[0]):
            cp.start()

    @pl.when(fresh)
    def _():
        for cp in fetch(blke_ref[b]):
            cp.wait()
        wgu_s[...] = wgu_f[...].astype(BF16)
        wdn_s[...] = wdn_f[...].astype(BF16)

        nxt = lax.while_loop(lambda p: (p < nblk_ref[0]) & (blke_ref[jnp.minimum(p, N_XBLOCKS - 1)] == blke_ref[b]),
                             lambda p: p + 1, b + 1)

        @pl.when(nxt < nblk_ref[0])
        def _():
            for cp in fetch(blke_ref[jnp.minimum(nxt, N_XBLOCKS - 1)]):
                cp.start()

    for groups in range(1, EXP_BLOCK // ROW_GROUP + 1):
        rows = groups * ROW_GROUP

        @pl.when(used & (live_ref[b] == groups))
        def _(rows=rows):
            hid = jnp.dot(_unpack_rows(xs_ref, rows), wgu_s[...], preferred_element_type=F32) + bgu_ref[...]
            gate = jnp.minimum(hid[:, :D_FF], SWIGLU_LIMIT)
            up = jnp.clip(hid[:, D_FF:], -SWIGLU_LIMIT, SWIGLU_LIMIT)
            act = (up + 1.0) * gate * jax.nn.sigmoid(SWIGLU_ALPHA * gate)
            y = jnp.dot(act.astype(BF16), wdn_s[...], preferred_element_type=F32) + bdn_ref[...]
            ys_ref[0:rows, :] = y.astype(BF16)
            if rows < EXP_BLOCK:
                ys_ref[rows:, :] = jnp.zeros((EXP_BLOCK - rows, D_MODEL), BF16)

    @pl.when(jnp.logical_not(used) | (live_ref[b] == 0))
    def _():
        ys_ref[...] = jnp.zeros_like(ys_ref)


def _expert_call(blke, nblk, live, xs, wgu, bgu, wdn, bdn):
    grid_spec = pltpu.PrefetchScalarGridSpec(
        num_scalar_prefetch=3,
        grid=(N_XBLOCKS,),
        in_specs=[
            pl.BlockSpec((EXP_BLOCK, PACK, LANES), lambda b, be, *_: (b, 0, 0)),
            pl.BlockSpec(memory_space=pl.ANY),
            pl.BlockSpec((None, 1, 2 * D_FF), lambda b, be, *_: (be[b], 0, 0)),
            pl.BlockSpec(memory_space=pl.ANY),
            pl.BlockSpec((None, 1, D_MODEL), lambda b, be, *_: (be[b], 0, 0)),
        ],
        out_specs=pl.BlockSpec((EXP_BLOCK, D_MODEL), lambda b, be, *_: (b, 0)),
        scratch_shapes=[pltpu.VMEM((D_MODEL, 2 * D_FF), F32), pltpu.VMEM((D_FF, D_MODEL), F32),
                        pltpu.VMEM((D_MODEL, 2 * D_FF), BF16), pltpu.VMEM((D_FF, D_MODEL), BF16),
                        pltpu.SemaphoreType.DMA((2,))],
    )
    return pl.pallas_call(
        _expert_kernel,
        grid_spec=grid_spec,
        out_shape=jax.ShapeDtypeStruct((N_ROWS, D_MODEL), BF16),
        compiler_params=pltpu.CompilerParams(
            dimension_semantics=("arbitrary",), vmem_limit_bytes=VMEM_LIMIT),
        name="moe_experts",
    )(blke, nblk, live, xs, wgu, bgu, wdn, bdn)


def _combine_kernel(astart_ref, nwin_ref, over_ref,
                    destm_ref, posm_ref, gm_ref, rep_ref, h1_ref, plep_ref, ples_ref, gple_ref, wpg_ref, wpp_ref,
                    gfin_ref, ys_any, yp_ref, ysm_ref, wins, moe_s, tmp_s, wsems, sem):
    i = pl.program_id(0)
    slot = i % 2

    def win_copy(step, e, into):
        first = pl.multiple_of(astart_ref[step * N_EXPERTS + e], WIN_ALIGN)
        return pltpu.make_async_copy(ys_any.at[pl.ds(first, WIN)], wins.at[into, pl.ds(e * WIN, WIN)],
                                     wsems.at[into])

    @pl.when(i == 0)
    def _():
        for e in range(N_EXPERTS):
            win_copy(0, e, 0).start()

    @pl.when(i + 1 < pl.num_programs(0))
    def _():
        for e in range(N_EXPERTS):
            win_copy(i + 1, e, 1 - slot).start()

    for e in range(N_EXPERTS):
        win_copy(i, e, slot).wait()

    gates = gm_ref[...]
    rep = rep_ref[...]
    pos_rep = jnp.dot(posm_ref[...].astype(BF16), rep, preferred_element_type=F32)
    gate_rep = jnp.dot(gates.astype(BF16), rep, preferred_element_type=F32)
    within = (_lane_iota((TOK_TILE, N_EXPERTS * WIN)) % WIN).astype(F32)
    gsel = jnp.where(pos_rep == within, gate_rep, 0.0).astype(BF16)
    mrows = pl.ds(pl.multiple_of((i % TAIL_TILES) * TOK_TILE, TOK_TILE), TOK_TILE)
    moe_s[mrows, :] = jnp.dot(gsel, wins[slot], preferred_element_type=F32)

    @pl.when(over_ref[i] > 0)
    def _():
        dest = destm_ref[...]
        lane = _lane_iota((TOK_TILE, LANES))
        lane_f = lane.astype(F32)
        tmp_s[...] = jnp.zeros_like(tmp_s)

        def per_expert(e, carry):
            a = astart_ref[i * N_EXPERTS + e]
            dcol = jnp.sum(jnp.where(lane == e, dest, 0.0), axis=-1, keepdims=True)
            gcol = jnp.sum(jnp.where(lane == e, gates, 0.0), axis=-1, keepdims=True)

            def per_window(w, carry2):
                first = a + w * WIN
                start = pl.multiple_of(jnp.minimum(first, N_ROWS - WIN), WIN_ALIGN)
                cp = pltpu.make_async_copy(ys_any.at[pl.ds(start, WIN)], tmp_s.at[pl.ds(0, WIN)], sem)
                cp.start()
                cp.wait()
                hit = (lane < WIN) & (dcol == start.astype(F32) + lane_f) & (dcol >= first.astype(F32))
                sel = jnp.where(hit, gcol, 0.0).astype(BF16)
                moe_s[mrows, :] += jnp.dot(sel, tmp_s[...], preferred_element_type=F32)
                return carry2

            return lax.fori_loop(1, nwin_ref[i * N_EXPERTS + e], per_window, carry)

        lax.fori_loop(0, N_EXPERTS, per_expert, 0)

    def tail(h1, moe_rows, ple):
        h2 = h1 + moe_rows
        hn = _rms(h2, gple_ref[...]).astype(BF16)
        gate = jax.nn.sigmoid(jnp.dot(hn, wpg_ref[...], preferred_element_type=F32))
        proj = jnp.dot(ple.astype(BF16), wpp_ref[...], preferred_element_type=F32)
        return _rms(h2 + gate * proj, gfin_ref[...])

    is_sample = i == N_CTILES - 1

    @pl.when((i % TAIL_TILES == TAIL_TILES - 1) & jnp.logical_not(is_sample))
    def _():
        yp_ref[...] = tail(h1_ref[...], moe_s[...], plep_ref[...])

    @pl.when(is_sample)
    def _():
        ysm_ref[...] = tail(h1_ref[0:DEC_BATCH, :], moe_s[0:DEC_BATCH, :], ples_ref[...])


def _combine_call(astart, nwin, over, ys, destm, posm, gm, h1, plep, ples, gple, wpg, wpp, gfin):
    rep = (jnp.arange(N_EXPERTS * WIN)[None, :] // WIN == jnp.arange(LANES)[:, None]).astype(BF16)
    rows = TAIL_TILES * TOK_TILE
    last = N_PROMPT // rows - 1

    grid_spec = pltpu.PrefetchScalarGridSpec(
        num_scalar_prefetch=3,
        grid=(N_CTILES,),
        in_specs=[
            pl.BlockSpec((TOK_TILE, LANES), lambda i, *_: (i, 0)),
            pl.BlockSpec((TOK_TILE, LANES), lambda i, *_: (i, 0)),
            pl.BlockSpec((TOK_TILE, LANES), lambda i, *_: (i, 0)),
            pl.BlockSpec((LANES, N_EXPERTS * WIN), lambda i, *_: (0, 0)),
            pl.BlockSpec((rows, D_MODEL), lambda i, *_: (i // TAIL_TILES, 0)),
            pl.BlockSpec((rows, PLE_DIM), lambda i, *_: (jnp.minimum(i // TAIL_TILES, last), 0)),
            pl.BlockSpec((DEC_BATCH, PLE_DIM), lambda i, *_: (0, 0)),
            pl.BlockSpec((1, D_MODEL), lambda i, *_: (0, 0)),
            pl.BlockSpec((D_MODEL, D_MODEL), lambda i, *_: (0, 0)),
            pl.BlockSpec((PLE_DIM, D_MODEL), lambda i, *_: (0, 0)),
            pl.BlockSpec((1, D_MODEL), lambda i, *_: (0, 0)),
            pl.BlockSpec(memory_space=pl.ANY),
        ],
        out_specs=[
            pl.BlockSpec((rows, D_MODEL), lambda i, *_: (jnp.minimum(i // TAIL_TILES, last), 0)),
            pl.BlockSpec((DEC_BATCH, D_MODEL), lambda i, *_: (0, 0)),
        ],
        scratch_shapes=[pltpu.VMEM((2, N_EXPERTS * WIN, D_MODEL), BF16), pltpu.VMEM((rows, D_MODEL), F32),
                        pltpu.VMEM((2 * WIN, D_MODEL), BF16), pltpu.SemaphoreType.DMA((2,)),
                        pltpu.SemaphoreType.DMA],
    )
    return pl.pallas_call(
        _combine_kernel,
        grid_spec=grid_spec,
        out_shape=[jax.ShapeDtypeStruct((N_PROMPT, D_MODEL), F32),
                   jax.ShapeDtypeStruct((DEC_BATCH, D_MODEL), F32)],
        compiler_params=pltpu.CompilerParams(
            dimension_semantics=("arbitrary",), vmem_limit_bytes=VMEM_LIMIT),
        name="moe_combine_tail",
    )(astart, nwin, over, destm, posm, gm, rep, h1, plep, ples, gple, wpg, wpp, gfin, ys)


def kernel(x_prompt, x_sample, cache_swa_k, cache_swa_v, p_prompt, p_sample, g_mix, w_in, ln_v_g, ln_v_b,
           w_sp, b_sp, sinks, g_out_a, g_out_b, w_o, g_moe, w_router, b_router, w_gu, b_gu, w_dn, b_dn,
           g_ple, w_ple_gate, w_ple_proj, g_final):
    l = 0
    row = lambda v: v.reshape(1, -1)
    win = w_in[l].astype(BF16)
    wo = w_o[l].astype(BF16)
    tril = jnp.tril(jnp.ones((CHUNK, CHUNK), dtype=bool))
    wsp = jnp.where(tril, w_sp[l], 0.0).astype(BF16)
    bsp = jnp.repeat(b_sp[l].T, HEAD_DIM, axis=1)
    w00 = row(jnp.repeat(w_sp[l][:, 0, 0], HEAD_DIM))
    b0 = row(jnp.repeat(b_sp[l][:, 0], HEAD_DIM))
    wr_hi = w_router[l].astype(BF16)
    wr_lo = (w_router[l] - wr_hi.astype(F32)).astype(BF16)
    wr = jnp.concatenate([wr_hi, wr_lo, jnp.zeros((D_MODEL, LANES - 2 * N_EXPERTS), BF16)], axis=1)
    br = row(jnp.concatenate([b_router[l], jnp.zeros((LANES - N_EXPERTS,), F32)]))
    common = (row(g_mix[l]), win, row(ln_v_g[l]), row(ln_v_b[l]))
    tail = (row(g_out_a[l]), row(g_out_b[l]), wo, row(g_moe[l]), wr, br)

    h1, xn, gm, sm, k_p, v_p, xs_zero, cnt = _prompt_call(
        x_prompt.reshape(N_PROMPT, D_MODEL), sinks[l], *common, wsp, bsp, *tail)
    chan = lambda a: jnp.transpose(a, (0, 2, 3, 1)).reshape(DEC_BATCH, KV_WIDTH, CHUNK)
    ck = chan(cache_swa_k[l])
    cv = chan(cache_swa_v[l])
    h1, xn, gm, sm, k_s, v_s, va_s, cnt = _decode_call(
        x_sample.reshape(DEC_BATCH, D_MODEL), ck, cv, sinks[l], *common, w00, b0, *tail, h1, xn, gm, sm, cnt)

    destm, posm, destt, xbe, live, nxblk, stab, ctab, astart, nwin = _plan_call(sm, cnt)
    flat = lambda tab, n: tab[:n, :N_EXPERTS].reshape(-1)
    over = (jnp.max(nwin[:N_CTILES, :N_EXPERTS], axis=1) > 1).astype(jnp.int32)
    cmax = jnp.max(ctab[:N_CTILES, :N_EXPERTS], axis=1)

    xs = _dispatch_call(flat(stab, N_CTILES), flat(ctab, N_CTILES), cmax, xn, destt[:N_EXPERTS], xs_zero)
    ys = _expert_call(xbe[:N_XBLOCKS, 0], nxblk[0, :1], live[:N_XBLOCKS, 0], xs,
                      w_gu[l], b_gu[l].reshape(N_EXPERTS, 1, 2 * D_FF), w_dn[l], b_dn[l].reshape(N_EXPERTS, 1, D_MODEL))
    y_p, y_s = _combine_call(
        flat(astart, N_CTILES), flat(nwin, N_CTILES), over, ys, destm, posm, gm, h1,
        p_prompt[l].reshape(N_PROMPT, PLE_DIM), p_sample[l].reshape(DEC_BATCH, PLE_DIM),
        row(g_ple[l]), w_ple_gate[l].astype(BF16), w_ple_proj[l].astype(BF16), row(g_final))

    kv5 = lambda a, n: a.reshape(1, n, CHUNK, 2, HEAD_DIM)
    win5 = lambda a: jnp.transpose(a.reshape(DEC_BATCH, 2, HEAD_DIM, CHUNK), (0, 3, 1, 2))[None]
    return (y_p.reshape(BATCH, SEQ, D_MODEL), y_s.reshape(DEC_BATCH, 1, D_MODEL),
            kv5(k_p, BATCH), kv5(v_p, BATCH), win5(k_s), win5(v_s),
            va_s.reshape(1, DEC_BATCH, 1, A_WIDTH))
```

```python
import math

import jax
import jax.numpy as jnp
from jax import lax
from jax.experimental import pallas as pl
from jax.experimental.pallas import tpu as pltpu

F32 = jnp.float32
BF16 = jnp.bfloat16

D_MODEL = 1024
BATCH = 4
SEQ = 4096
DEC_BATCH = 128
HEAD_DIM = 64
A_WIDTH = 512
B_WIDTH = 512
B_HEADS = 8
KV_WIDTH = 128
IN_WIDTH = 2 * A_WIDTH + B_WIDTH + 2 * KV_WIDTH
CHUNK = 128
N_EXPERTS = 32
TOP_K = 4
D_FF = 1024
SWIGLU_ALPHA = 1.702
SWIGLU_LIMIT = 7.0
PLE_DIM = 256
EPS = 1e-5

LANES = 128
VMEM_LIMIT = 56 * 1024 * 1024
N_PROMPT = BATCH * SEQ
N_TOK = N_PROMPT + DEC_BATCH
TM = 1024
N_PAD = ((N_TOK + TM - 1) // TM) * TM
TOK_TILE = 128
N_PAD_TILES = N_PAD // TOK_TILE
N_CTILES = N_PROMPT // TOK_TILE + 1
DEC_TILE = 16
EXP_BLOCK = 512
ROW_GROUP = 128
DISP_CHUNK = 32
N_XBLOCKS = (N_TOK * TOP_K + N_EXPERTS * (DISP_CHUNK + EXP_BLOCK - 1) + EXP_BLOCK - 1) // EXP_BLOCK
N_ROWS = N_XBLOCKS * EXP_BLOCK
XS_ROWS = N_ROWS + N_EXPERTS * DISP_CHUNK
ZERO_ROWS = EXP_BLOCK
ZERO_COPIES = XS_ROWS // ZERO_ROWS
ZERO_STEPS = N_PROMPT // TM - 1
ZERO_PER_STEP = ZERO_COPIES // ZERO_STEPS
assert ZERO_PER_STEP * ZERO_STEPS == ZERO_COPIES and ZERO_COPIES * ZERO_ROWS == XS_ROWS
XBE_ROWS = ((N_XBLOCKS + 7) // 8) * 8
TAB_ROWS = ((N_PAD_TILES + 7) // 8) * 8
PACK = D_MODEL // 2 // LANES
WIN = 64
WIN_ALIGN = 16
TAIL_TILES = 4
NEG = -1e30


def _rms(x, g):
    return x * lax.rsqrt(jnp.mean(x * x, axis=-1, keepdims=True) + EPS) * g


def _gelu(x):
    c = math.sqrt(2.0 / math.pi)
    return x * (0.5 * (1.0 + jnp.tanh(c * (x + 0.044715 * (x * x * x)))))


def _layernorm(x, g, b):
    mu = jnp.mean(x, axis=-1, keepdims=True)
    xc = x - mu
    return xc * lax.rsqrt(jnp.mean(xc * xc, axis=-1, keepdims=True) + EPS) * g + b


def _lane_iota(shape):
    return lax.broadcasted_iota(jnp.int32, shape, len(shape) - 1)


def _full(shape):
    n = len(shape)
    return pl.BlockSpec(shape, lambda *_: (0,) * n)


def _route(xn2, wr_ref, br_ref):
    m = xn2.shape[0]
    xh = xn2.astype(BF16)
    xl = (xn2 - xh.astype(F32)).astype(BF16)
    r = jnp.dot(jnp.concatenate([xh, xl], axis=0), wr_ref[...], preferred_element_type=F32)
    r = r[:m] + r[m:]
    lane = _lane_iota((m, LANES))
    lane_f = lane.astype(F32)
    logits = jnp.where(lane < N_EXPERTS, r + pltpu.roll(r, LANES - N_EXPERTS, 1) + br_ref[...], NEG)
    work = logits
    sel = jnp.zeros((m, LANES), F32)
    top = None
    z = None
    for _ in range(TOP_K):
        mx = jnp.max(work, axis=-1, keepdims=True)
        first = jnp.min(jnp.where(work == mx, lane_f, float(LANES)), axis=-1, keepdims=True)
        hit = lane_f == first
        sel = jnp.where(hit, 1.0, sel)
        work = jnp.where(hit, NEG, work)
        if top is None:
            top = mx
            z = jnp.ones_like(mx)
        else:
            z = z + jnp.exp(mx - top)
    gates = jnp.where(sel > 0.0, jnp.exp(logits - top) / z, 0.0)
    return gates, sel


def _prompt_kernel(sinks_ref, x_ref, gmix_ref, win_ref, lng_ref, lnb_ref, wsp_ref, bsp_ref,
                   goa_ref, gob_ref, wo_ref, gmoe_ref, wr_ref, br_ref,
                   h1_ref, xn_ref, gm_ref, sm_ref, k_ref, v_ref, xs_ref, cnt_ref,
                   z_s, kv_s, cat_s, zero_s, zsems):
    g = pl.program_id(0)

    @pl.when(g == 0)
    def _():
        cnt_ref[...] = jnp.zeros_like(cnt_ref)

    def zero_copy(step, k):
        chunk = step * ZERO_PER_STEP + k
        return pltpu.make_async_copy(zero_s, xs_ref.at[pl.ds(chunk * ZERO_ROWS, ZERO_ROWS)], zsems.at[step % 2])

    @pl.when(g == 0)
    def _():
        zero_s[...] = jnp.zeros_like(zero_s)

    @pl.when((g >= 1) & (g <= ZERO_STEPS))
    def _():
        for k in range(ZERO_PER_STEP):
            zero_copy(g - 1, k).wait()

    @pl.when(g < ZERO_STEPS)
    def _():
        for k in range(ZERO_PER_STEP):
            zero_copy(g, k).start()

    @pl.when(g >= N_PROMPT // TM)
    def _():
        h1_ref[...] = jnp.zeros_like(h1_ref)
        xn_ref[...] = jnp.zeros_like(xn_ref)
        gm_ref[...] = jnp.zeros_like(gm_ref)
        sm_ref[...] = jnp.zeros_like(sm_ref)

    @pl.when(g < N_PROMPT // TM)
    def _():
        _prompt_tile(g % (SEQ // TM), sinks_ref, x_ref, gmix_ref, win_ref, lng_ref, lnb_ref, wsp_ref, bsp_ref,
                     goa_ref, gob_ref, wo_ref, gmoe_ref, wr_ref, br_ref,
                     h1_ref, xn_ref, gm_ref, sm_ref, k_ref, v_ref, z_s, kv_s, cat_s)
        cnt_ref[...] += jnp.broadcast_to(jnp.sum(sm_ref[...], axis=0, keepdims=True), cnt_ref.shape)


def _prompt_tile(j, sinks_ref, x_ref, gmix_ref, win_ref, lng_ref, lnb_ref, wsp_ref, bsp_ref,
                 goa_ref, gob_ref, wo_ref, gmoe_ref, wr_ref, br_ref,
                 h1_ref, xn_ref, gm_ref, sm_ref, k_ref, v_ref, z_s, kv_s, cat_s):
    @pl.when(j == 0)
    def _():
        kv_s[0:CHUNK, :] = jnp.zeros((CHUNK, 2 * KV_WIDTH), F32)

    xn = _rms(x_ref[...], gmix_ref[...]).astype(BF16)
    z_s[...] = jnp.dot(xn, win_ref[...], preferred_element_type=F32)
    kv_s[CHUNK:, :] = z_s[:, 2 * A_WIDTH + B_WIDTH:]

    lane = _lane_iota((CHUNK, LANES))
    lo = lane < HEAD_DIM
    lane2 = _lane_iota((2 * CHUNK, LANES))
    lo2 = lane2 < HEAD_DIM
    qi = lax.broadcasted_iota(jnp.int32, (CHUNK, CHUNK), 0)
    kc = lax.broadcasted_iota(jnp.int32, (CHUNK, CHUNK), 1)
    from_prev = kc > qi
    dist = jnp.where(from_prev, qi + CHUNK - kc, qi - kc).astype(F32)

    def chunk_body(c, carry):
        r0 = pl.multiple_of(c * CHUNK, CHUNK)
        rows = pl.ds(r0, CHUNK)
        u = _gelu(z_s[rows, 0:A_WIDTH])
        va = _layernorm(_gelu(z_s[rows, A_WIDTH:2 * A_WIDTH]), lng_ref[...], lnb_ref[...])
        vab = va.astype(BF16)
        slabs = []
        for p in range(A_WIDTH // LANES):
            slab = vab[:, p * LANES:(p + 1) * LANES]
            m = jnp.dot(wsp_ref[p], slab, preferred_element_type=F32)
            slabs.append(jnp.where(lo, m[:CHUNK], m[CHUNK:]))
        ya = u * (jnp.concatenate(slabs, axis=-1) + bsp_ref[...])
        ya_n = _rms(ya, goa_ref[...])
        k2 = kv_s[pl.ds(r0, 2 * CHUNK), 0:KV_WIDTH]
        v2 = kv_s[pl.ds(r0, 2 * CHUNK), KV_WIDTH:2 * KV_WIDTH]
        k2r = pltpu.roll(k2, HEAD_DIM, 1)
        v2r = pltpu.roll(v2, HEAD_DIM, 1)
        kd = (jnp.where(lo2, k2, k2r).astype(BF16), jnp.where(lo2, k2r, k2).astype(BF16))
        vd = (jnp.where(lo2, v2, v2r).astype(BF16), jnp.where(lo2, v2r, v2).astype(BF16))
        prev_ok = (j > 0) | (c > 0)
        masked = from_prev & jnp.logical_not(prev_ok)
        yb_slabs = []
        for kv in range(2):
            q0 = z_s[rows, 2 * A_WIDTH + (2 * kv) * LANES:2 * A_WIDTH + (2 * kv + 1) * LANES]
            q1 = z_s[rows, 2 * A_WIDTH + (2 * kv + 1) * LANES:2 * A_WIDTH + (2 * kv + 2) * LANES]
            lhs = jnp.concatenate([jnp.where(lo, q0, 0.0), jnp.where(lo, 0.0, q0),
                                   jnp.where(lo, q1, 0.0), jnp.where(lo, 0.0, q1)], axis=0).astype(BF16)
            s_all = lax.dot_general(lhs, kd[kv], (((1,), (1,)), ((), ())), preferred_element_type=F32)
            probs = []
            for i in range(4):
                h = 4 * kv + i
                slope = 2.0 ** (-(h + 1))
                sink = sinks_ref[h]
                sh = s_all[i * CHUNK:(i + 1) * CHUNK]
                s = jnp.where(from_prev, sh[:, :CHUNK], sh[:, CHUNK:]) * (HEAD_DIM ** -0.5) - slope * dist
                s = jnp.where(masked, NEG, s)
                mx = jnp.maximum(jnp.max(s, axis=-1, keepdims=True), sink)
                e = jnp.exp(s - mx)
                den = jnp.sum(e, axis=-1, keepdims=True) + jnp.exp(sink - mx)
                p = e * (1.0 / den)
                probs.append(jnp.concatenate([jnp.where(from_prev, p, 0.0), jnp.where(from_prev, 0.0, p)], axis=-1))
            pm = jnp.concatenate(probs, axis=0).astype(BF16)
            o = jnp.dot(pm, vd[kv], preferred_element_type=F32)
            yb_slabs.append(jnp.where(lo, o[0:CHUNK], o[CHUNK:2 * CHUNK]))
            yb_slabs.append(jnp.where(lo, o[2 * CHUNK:3 * CHUNK], o[3 * CHUNK:4 * CHUNK]))
        yb_n = _rms(jnp.concatenate(yb_slabs, axis=-1), gob_ref[...])
        cat_s[rows, 0:A_WIDTH] = ya_n.astype(BF16)
        cat_s[rows, A_WIDTH:] = yb_n.astype(BF16)
        return carry

    lax.fori_loop(0, TM // CHUNK, chunk_body, 0)

    kv_s[0:CHUNK, :] = kv_s[TM:TM + CHUNK, :]
    k_ref[...] = kv_s[TM:TM + CHUNK, 0:KV_WIDTH]
    v_ref[...] = kv_s[TM:TM + CHUNK, KV_WIDTH:]

    h1 = x_ref[...] + jnp.dot(cat_s[...], wo_ref[...], preferred_element_type=F32)
    h1_ref[...] = h1
    xn2 = _rms(h1, gmoe_ref[...])
    xn_ref[...] = xn2.astype(BF16)
    gates, sel = _route(xn2, wr_ref, br_ref)
    gm_ref[...] = gates
    sm_ref[...] = sel


def _prompt_call(x, sinks, gmix, win, lng, lnb, wsp, bsp, goa, gob, wo, gmoe, wr, br):
    real = N_PROMPT // TM
    row = lambda g: (g, 0)
    seq = lambda g: (jnp.minimum(g, real - 1) // (SEQ // TM), 0, 0)
    return pl.pallas_call(
        _prompt_kernel,
        grid=(N_PAD // TM,),
        in_specs=[
            pl.BlockSpec(memory_space=pltpu.SMEM),
            pl.BlockSpec((TM, D_MODEL), lambda g: (jnp.minimum(g, real - 1), 0)),
            _full((1, D_MODEL)), _full((D_MODEL, IN_WIDTH)), _full((1, A_WIDTH)), _full((1, A_WIDTH)),
            _full((4, 2 * CHUNK, CHUNK)), _full((CHUNK, A_WIDTH)), _full((1, A_WIDTH)), _full((1, B_WIDTH)),
            _full((D_MODEL, D_MODEL)), _full((1, D_MODEL)), _full((D_MODEL, LANES)), _full((1, LANES)),
        ],
        out_specs=[
            pl.BlockSpec((TM, D_MODEL), row),
            pl.BlockSpec((TM, D_MODEL), row),
            pl.BlockSpec((TM, LANES), row),
            pl.BlockSpec((TM, LANES), row),
            pl.BlockSpec((None, CHUNK, KV_WIDTH), seq),
            pl.BlockSpec((None, CHUNK, KV_WIDTH), seq),
            pl.BlockSpec(memory_space=pl.ANY),
            _full((8, LANES)),
        ],
        out_shape=[
            jax.ShapeDtypeStruct((N_PAD, D_MODEL), F32),
            jax.ShapeDtypeStruct((N_PAD, D_MODEL), BF16),
            jax.ShapeDtypeStruct((N_PAD, LANES), F32),
            jax.ShapeDtypeStruct((N_PAD, LANES), F32),
            jax.ShapeDtypeStruct((BATCH, CHUNK, KV_WIDTH), F32),
            jax.ShapeDtypeStruct((BATCH, CHUNK, KV_WIDTH), F32),
            jax.ShapeDtypeStruct((XS_ROWS, PACK, LANES), jnp.int32),
            jax.ShapeDtypeStruct((8, LANES), F32),
        ],
        scratch_shapes=[
            pltpu.VMEM((TM, IN_WIDTH), F32),
            pltpu.VMEM((TM + CHUNK, 2 * KV_WIDTH), F32),
            pltpu.VMEM((TM, D_MODEL), BF16),
            pltpu.VMEM((ZERO_ROWS, PACK, LANES), jnp.int32),
            pltpu.SemaphoreType.DMA((2,)),
        ],
        compiler_params=pltpu.CompilerParams(
            dimension_semantics=("arbitrary",), vmem_limit_bytes=VMEM_LIMIT),
        name="prompt_premoe",
    )(sinks, x, gmix, win, lng, lnb, wsp, bsp, goa, gob, wo, gmoe, wr, br)


def _decode_kernel(sinks_ref, x_ref, ck_ref, cv_ref, gmix_ref, win_ref, lng_ref, lnb_ref, w00_ref, b0_ref,
                   goa_ref, gob_ref, wo_ref, gmoe_ref, wr_ref, br_ref,
                   h1_in, xn_in, gm_in, sm_in, cnt_in,
                   h1_ref, xn_ref, gm_ref, sm_ref, nk_ref, nv_ref, va_ref, cnt_ref,
                   q_s, kn_s, vn_s, ya_s, yb_s, qm_s, o_s):
    del h1_in, xn_in, gm_in, sm_in
    i = pl.program_id(0)
    t = DEC_TILE

    @pl.when(i == 0)
    def _():
        xn = _rms(x_ref[...], gmix_ref[...]).astype(BF16)
        z = jnp.dot(xn, win_ref[...], preferred_element_type=F32)
        u = _gelu(z[:, 0:A_WIDTH])
        va = _layernorm(_gelu(z[:, A_WIDTH:2 * A_WIDTH]), lng_ref[...], lnb_ref[...])
        va_ref[...] = va
        ya_s[...] = _rms(u * (w00_ref[...] * va + b0_ref[...]), goa_ref[...])
        q_s[...] = z[:, 2 * A_WIDTH:2 * A_WIDTH + B_WIDTH]
        kn_s[...] = z[:, 2 * A_WIDTH + B_WIDTH:2 * A_WIDTH + B_WIDTH + KV_WIDTH]
        vn_s[...] = z[:, 2 * A_WIDTH + B_WIDTH + KV_WIDTH:]

    rows = pl.ds(pl.multiple_of(i * t, t), t)
    q = q_s[rows, :]
    kn = kn_s[rows, :]
    vn = vn_s[rows, :]
    lane = _lane_iota((t, LANES))
    lo = lane < HEAD_DIM
    stacked = []
    for h in range(B_HEADS):
        qh = jnp.where(lo if h % 2 == 0 else jnp.logical_not(lo), q[:, (h // 2) * LANES:(h // 2 + 1) * LANES], 0.0)
        stacked.append(pltpu.roll(qh, HEAD_DIM, 1) if h % 2 != h // 4 else qh)
    qm_s[...] = jnp.concatenate(stacked, axis=0)
    hrow = lax.broadcasted_iota(jnp.int32, (B_HEADS, 1), 0)
    slope = jnp.zeros((B_HEADS, 1), F32)
    sink = jnp.zeros((B_HEADS, 1), F32)
    for h in range(B_HEADS):
        slope = jnp.where(hrow == h, 2.0 ** (-(h + 1)), slope)
        sink = jnp.where(hrow == h, sinks_ref[h], sink)
    pos = _lane_iota((B_HEADS, CHUNK))
    bias = slope * (CHUNK - pos).astype(F32)
    on_kv = (_lane_iota((B_HEADS, LANES)) < HEAD_DIM) == (hrow < B_HEADS // 2)
    last_pos = _lane_iota((KV_WIDTH, CHUNK)) == CHUNK - 1
    last_row = lax.broadcasted_iota(jnp.int32, (CHUNK, KV_WIDTH), 0) == CHUNK - 1
    for b in range(t):
        qb = qm_s[pl.ds(b, B_HEADS, stride=t), :]
        kt = ck_ref[b]
        vt = cv_ref[b]
        s_c = jnp.dot(qb.astype(BF16), kt.astype(BF16), preferred_element_type=F32) * (HEAD_DIM ** -0.5) - bias
        s_c = jnp.where(pos >= 1, s_c, NEG)
        s_n = jnp.sum(qb * kn[b:b + 1, :], axis=-1, keepdims=True) * (HEAD_DIM ** -0.5)
        mx = jnp.maximum(jnp.maximum(jnp.max(s_c, axis=-1, keepdims=True), s_n), sink)
        e_c = jnp.exp(s_c - mx)
        e_n = jnp.exp(s_n - mx)
        inv = 1.0 / (jnp.sum(e_c, axis=-1, keepdims=True) + e_n + jnp.exp(sink - mx))
        o = lax.dot_general((e_c * inv).astype(BF16), vt.astype(BF16), (((1,), (1,)), ((), ())),
                            preferred_element_type=F32)
        o = o + (e_n * inv) * vn[b:b + 1, :]
        o_s[pl.ds(b, B_HEADS, stride=t), :] = jnp.where(on_kv, o, 0.0)
        nk_ref[b] = jnp.where(last_pos, jnp.where(last_row, kn[b:b + 1, :], 0.0).T, pltpu.roll(kt, CHUNK - 1, 1))
        nv_ref[b] = jnp.where(last_pos, jnp.where(last_row, vn[b:b + 1, :], 0.0).T, pltpu.roll(vt, CHUNK - 1, 1))
    slabs = []
    for p in range(B_WIDTH // LANES):
        pair = []
        for h in (2 * p, 2 * p + 1):
            oh = o_s[h * t:(h + 1) * t, :]
            pair.append(pltpu.roll(oh, HEAD_DIM, 1) if h % 2 != h // 4 else oh)
        slabs.append(pair[0] + pair[1])
    yb_s[rows, :] = jnp.concatenate(slabs, axis=-1)

    @pl.when(i == pl.num_programs(0) - 1)
    def _():
        yb_n = _rms(yb_s[...], gob_ref[...])
        cat = jnp.concatenate([ya_s[...], yb_n], axis=-1).astype(BF16)
        h1 = x_ref[...] + jnp.dot(cat, wo_ref[...], preferred_element_type=F32)
        xn2 = _rms(h1, gmoe_ref[...])
        gates, sel = _route(xn2, wr_ref, br_ref)
        h1_ref[...] = h1
        xn_ref[...] = xn2.astype(BF16)
        gm_ref[...] = gates
        sm_ref[...] = sel
        cnt_ref[...] = cnt_in[...] + jnp.broadcast_to(jnp.sum(sel, axis=0, keepdims=True), cnt_ref.shape)


def _decode_call(x, ck, cv, sinks, gmix, win, lng, lnb, w00, b0, goa, gob, wo, gmoe, wr, br, h1, xn, gm, sm, cnt):
    t = DEC_TILE
    cache = pl.BlockSpec((t, KV_WIDTH, CHUNK), lambda i: (i, 0, 0))
    tok = lambda width: pl.BlockSpec((DEC_BATCH, width), lambda i: (N_PROMPT // DEC_BATCH, 0))
    anyspec = pl.BlockSpec(memory_space=pl.ANY)
    return pl.pallas_call(
        _decode_kernel,
        grid=(DEC_BATCH // t,),
        in_specs=[
            pl.BlockSpec(memory_space=pltpu.SMEM),
            _full((DEC_BATCH, D_MODEL)), cache, cache,
            _full((1, D_MODEL)), _full((D_MODEL, IN_WIDTH)), _full((1, A_WIDTH)), _full((1, A_WIDTH)),
            _full((1, A_WIDTH)), _full((1, A_WIDTH)), _full((1, A_WIDTH)), _full((1, B_WIDTH)),
            _full((D_MODEL, D_MODEL)), _full((1, D_MODEL)), _full((D_MODEL, LANES)), _full((1, LANES)),
            anyspec, anyspec, anyspec, anyspec, _full((8, LANES)),
        ],
        out_specs=[tok(D_MODEL), tok(D_MODEL), tok(LANES), tok(LANES), cache, cache, _full((DEC_BATCH, A_WIDTH)),
                   _full((8, LANES))],
        out_shape=[
            jax.ShapeDtypeStruct((N_PAD, D_MODEL), F32),
            jax.ShapeDtypeStruct((N_PAD, D_MODEL), BF16),
            jax.ShapeDtypeStruct((N_PAD, LANES), F32),
            jax.ShapeDtypeStruct((N_PAD, LANES), F32),
            jax.ShapeDtypeStruct((DEC_BATCH, KV_WIDTH, CHUNK), F32),
            jax.ShapeDtypeStruct((DEC_BATCH, KV_WIDTH, CHUNK), F32),
            jax.ShapeDtypeStruct((DEC_BATCH, A_WIDTH), F32),
            jax.ShapeDtypeStruct((8, LANES), F32),
        ],
        scratch_shapes=[pltpu.VMEM((DEC_BATCH, B_WIDTH), F32), pltpu.VMEM((DEC_BATCH, KV_WIDTH), F32),
                        pltpu.VMEM((DEC_BATCH, KV_WIDTH), F32), pltpu.VMEM((DEC_BATCH, A_WIDTH), F32),
                        pltpu.VMEM((DEC_BATCH, B_WIDTH), F32), pltpu.VMEM((B_HEADS * t, LANES), F32),
                        pltpu.VMEM((B_HEADS * t, LANES), F32)],
        input_output_aliases={16: 0, 17: 1, 18: 2, 19: 3},
        compiler_params=pltpu.CompilerParams(
            dimension_semantics=("arbitrary",), vmem_limit_bytes=VMEM_LIMIT),
        name="sample_premoe",
    )(sinks, x, ck, cv, gmix, win, lng, lnb, w00, b0, goa, gob, wo, gmoe, wr, br, h1, xn, gm, sm, cnt)


def _plan_kernel(sm_ref, cnt_ref,
                 destm_ref, posm_ref, destt_ref, xbe_ref, live_ref, nxblk_ref, stab_ref, ctab_ref, astart_ref,
                 nwin_ref, base_s, pstart_s):
    step = pl.program_id(0)
    lane = _lane_iota((1, LANES))

    @pl.when(step == 0)
    def _():
        counts = cnt_ref[0:1, :]
        padded = jnp.floor((counts + (DISP_CHUNK + EXP_BLOCK - 1)) * (1.0 / EXP_BLOCK)) * EXP_BLOCK
        padded = jnp.where(counts > 0.0, padded, 0.0)
        pend = padded
        for s in (1, 2, 4, 8, 16):
            pend = pend + jnp.where(lane >= s, pltpu.roll(pend, s, 1), 0.0)
        spare = (N_ROWS + lane * DISP_CHUNK).astype(F32)
        pstart_s[...] = jnp.where(counts > 0.0, pend - padded, spare)
        base_s[...] = jnp.zeros_like(base_s)
        brow = lax.broadcasted_iota(jnp.int32, (XBE_ROWS, LANES), 0).astype(F32) * EXP_BLOCK
        done = jnp.where((lane < N_EXPERTS) & (pend <= brow), 1.0, 0.0)
        be = jnp.minimum(jnp.sum(done, axis=-1, keepdims=True), N_EXPERTS - 1.0)
        xbe_ref[...] = jnp.broadcast_to(be, (XBE_ROWS, LANES)).astype(jnp.int32)
        real = jnp.clip(counts - (brow - (pend - padded)), 0.0, float(EXP_BLOCK))
        real = jnp.sum(jnp.where(lane.astype(F32) == be, real, 0.0), axis=-1, keepdims=True)
        groups = jnp.floor((real + (ROW_GROUP - 1)) * (1.0 / ROW_GROUP))
        live_ref[...] = jnp.broadcast_to(groups, (XBE_ROWS, LANES)).astype(jnp.int32)
        total = jnp.sum(jnp.where(lane == N_EXPERTS - 1, pend, 0.0), axis=-1, keepdims=True)
        nxblk_ref[...] = jnp.broadcast_to(total * (1.0 / EXP_BLOCK), (8, LANES)).astype(jnp.int32)
        stab_ref[...] = jnp.zeros_like(stab_ref)
        ctab_ref[...] = jnp.zeros_like(ctab_ref)
        astart_ref[...] = jnp.zeros_like(astart_ref)
        nwin_ref[...] = jnp.zeros_like(nwin_ref)

    def walk():
        r = lax.broadcasted_iota(jnp.int32, (TOK_TILE, TOK_TILE), 0)
        c = lax.broadcasted_iota(jnp.int32, (TOK_TILE, TOK_TILE), 1)
        lower = jnp.where(c < r, 1.0, 0.0).astype(BF16)
        for q in range(TM // TOK_TILE):
            i = step * (TM // TOK_TILE) + q
            sel = sm_ref[q * TOK_TILE:(q + 1) * TOK_TILE, :]
            cnt = jnp.sum(sel, axis=0, keepdims=True)
            prefix = jnp.dot(lower, sel.astype(BF16), preferred_element_type=F32)
            start = pstart_s[...] + base_s[...]
            dest = jnp.where(sel > 0.0, prefix + start, -1.0)
            destm_ref[q * TOK_TILE:(q + 1) * TOK_TILE, :] = dest
            destt_ref[:, q * TOK_TILE:(q + 1) * TOK_TILE] = dest.T
            has = (cnt > 0.0) & (lane < N_EXPERTS)
            stab_ref[pl.ds(i, 1), :] = start.astype(jnp.int32)
            ctab_ref[pl.ds(i, 1), :] = jnp.where(has, cnt, 0.0).astype(jnp.int32)
            a = jnp.minimum(jnp.floor(start * (1.0 / WIN_ALIGN)) * WIN_ALIGN, float(N_ROWS - WIN))
            nw = jnp.where(has, jnp.floor((start + cnt - a + (WIN - 1)) * (1.0 / WIN)), 0.0)
            posm_ref[q * TOK_TILE:(q + 1) * TOK_TILE, :] = jnp.where(sel > 0.0, prefix + start - a, -1.0)
            astart_ref[pl.ds(i, 1), :] = a.astype(jnp.int32)
            nwin_ref[pl.ds(i, 1), :] = nw.astype(jnp.int32)
            base_s[...] += cnt

    walk()


def _plan_call(sm, cnt):
    tile = lambda i: (i, 0)
    tile_t = lambda i: (0, i)
    tab = jax.ShapeDtypeStruct((TAB_ROWS, LANES), jnp.int32)
    return pl.pallas_call(
        _plan_kernel,
        grid=(N_PAD // TM,),
        in_specs=[pl.BlockSpec((TM, LANES), lambda i: (i, 0)), _full((8, LANES))],
        out_specs=[
            pl.BlockSpec((TM, LANES), tile),
            pl.BlockSpec((TM, LANES), tile),
            pl.BlockSpec((LANES, TM), tile_t),
            _full((XBE_ROWS, LANES)), _full((XBE_ROWS, LANES)), _full((8, LANES)),
            _full((TAB_ROWS, LANES)), _full((TAB_ROWS, LANES)), _full((TAB_ROWS, LANES)), _full((TAB_ROWS, LANES)),
        ],
        out_shape=[
            jax.ShapeDtypeStruct((N_PAD, LANES), F32),
            jax.ShapeDtypeStruct((N_PAD, LANES), F32),
            jax.ShapeDtypeStruct((LANES, N_PAD), F32),
            jax.ShapeDtypeStruct((XBE_ROWS, LANES), jnp.int32),
            jax.ShapeDtypeStruct((XBE_ROWS, LANES), jnp.int32),
            jax.ShapeDtypeStruct((8, LANES), jnp.int32),
            tab, tab, tab, tab,
        ],
        scratch_shapes=[pltpu.VMEM((1, LANES), F32), pltpu.VMEM((1, LANES), F32)],
        compiler_params=pltpu.CompilerParams(
            dimension_semantics=("arbitrary",), vmem_limit_bytes=VMEM_LIMIT),
        name="moe_plan",
    )(sm, cnt)


def _pack_rows(z):
    half = D_MODEL // 2
    lo = lax.bitcast_convert_type(z[:, :half], jnp.uint32) >> 16
    hi = lax.bitcast_convert_type(z[:, half:], jnp.uint32) & jnp.uint32(0xFFFF0000)
    return lax.bitcast_convert_type(hi | lo, jnp.int32)


def _unpack_rows(ref, rows=None):
    rows = ref.shape[0] if rows is None else rows
    flat = ref.reshape(ref.shape[0] * PACK, LANES)
    lo, hi = [], []
    for s in range(PACK):
        w = lax.bitcast_convert_type(flat[pl.ds(s, rows, stride=PACK), :], jnp.uint32)
        lo.append(lax.bitcast_convert_type(w << 16, F32))
        hi.append(lax.bitcast_convert_type(w & jnp.uint32(0xFFFF0000), F32))
    return jnp.concatenate(lo + hi, axis=-1).astype(BF16)


def _dispatch_kernel(stab_ref, ctab_ref, cmax_ref, xn_ref, destt_ref, xs_in, xs_ref,
                     stage0, stage1, stage2, sems, sem2):
    del xs_in
    i = pl.program_id(0)
    last = pl.num_programs(0) - 1
    x = xn_ref[...]
    dt = destt_ref[...]
    rio = lax.broadcasted_iota(jnp.int32, (DISP_CHUNK, 1), 0).astype(F32)

    def chunk_rows(j, stage):
        parts = []
        for e in range(N_EXPERTS):
            first = (stab_ref[i * N_EXPERTS + e] + j * DISP_CHUNK).astype(F32)
            parts.append(jnp.where(dt[e:e + 1, :] == first + rio, 1.0, 0.0).astype(BF16))
        onehot = jnp.concatenate(parts, axis=0)
        words = _pack_rows(jnp.dot(onehot, x, preferred_element_type=F32))
        for s in range(PACK):
            stage[pl.ds(s, N_EXPERTS * DISP_CHUNK, stride=PACK), :] = words[:, s * LANES:(s + 1) * LANES]

    def copy(stage, step, e, j, sem):
        first = stab_ref[step * N_EXPERTS + e] + j * DISP_CHUNK
        rows = stage.reshape(N_EXPERTS * DISP_CHUNK, PACK, LANES)
        return pltpu.make_async_copy(rows.at[pl.ds(e * DISP_CHUNK, DISP_CHUNK)],
                                     xs_ref.at[pl.ds(first, DISP_CHUNK)], sem)

    def step_body(stage, prev_stage, par):
        chunk_rows(0, stage)

        @pl.when(i > 0)
        def _():
            for e in range(N_EXPERTS):
                copy(prev_stage, i - 1, e, 0, sems.at[1 - par]).wait()

        for e in range(N_EXPERTS):
            copy(stage, i, e, 0, sems.at[par]).start()

        @pl.when(i == last)
        def _():
            for e in range(N_EXPERTS):
                copy(stage, i, e, 0, sems.at[par]).wait()

    @pl.when(i % 2 == 0)
    def _():
        step_body(stage0, stage1, 0)

    @pl.when(i % 2 == 1)
    def _():
        step_body(stage1, stage0, 1)

    for j in range(1, TOK_TILE // DISP_CHUNK):

        @pl.when(cmax_ref[i] > j * DISP_CHUNK)
        def _(j=j):
            chunk_rows(j, stage2)
            for e in range(N_EXPERTS):

                @pl.when(ctab_ref[i * N_EXPERTS + e] > j * DISP_CHUNK)
                def _(e=e):
                    cp = copy(stage2, i, e, j, sem2)
                    cp.start()
                    cp.wait()


def _dispatch_call(stab, ctab, cmax, xn, destt, xs_zero):
    stage = pltpu.VMEM((N_EXPERTS * DISP_CHUNK * PACK, LANES), jnp.int32)
    grid_spec = pltpu.PrefetchScalarGridSpec(
        num_scalar_prefetch=3,
        grid=(N_CTILES,),
        in_specs=[
            pl.BlockSpec((TOK_TILE, D_MODEL), lambda i, *_: (i, 0)),
            pl.BlockSpec((N_EXPERTS, TOK_TILE), lambda i, *_: (0, i)),
            pl.BlockSpec(memory_space=pl.ANY),
        ],
        out_specs=pl.BlockSpec(memory_space=pl.ANY),
        scratch_shapes=[stage, stage, stage, pltpu.SemaphoreType.DMA((2,)), pltpu.SemaphoreType.DMA],
    )
    return pl.pallas_call(
        _dispatch_kernel,
        grid_spec=grid_spec,
        out_shape=jax.ShapeDtypeStruct((XS_ROWS, PACK, LANES), jnp.int32),
        input_output_aliases={5: 0},
        compiler_params=pltpu.CompilerParams(
            dimension_semantics=("arbitrary",), vmem_limit_bytes=VMEM_LIMIT),
        name="moe_dispatch",
    )(stab, ctab, cmax, xn, destt, xs_zero)


def _expert_kernel(blke_ref, nblk_ref, live_ref, xs_ref, wgu_hbm, bgu_ref, wdn_hbm, bdn_ref,
                   ys_ref, wgu_f, wdn_f, wgu_s, wdn_s, sems):
    b = pl.program_id(0)
    used = b < nblk_ref[0]
    prev = blke_ref[jnp.maximum(b - 1, 0)]
    fresh = used & ((b == 0) | (blke_ref[b] != prev))

    def fetch(e):
        return (pltpu.make_async_copy(wgu_hbm.at[e], wgu_f, sems.at[0]),
                pltpu.make_async_copy(wdn_hbm.at[e], wdn_f, sems.at[1]))

    @pl.when((b == 0) & used)
    def _():
        for cp in fetch(blke_ref[0]):
            cp.start()

    @pl.when(fresh)
    def _():
        for cp in fetch(blke_ref[b]):
            cp.wait()
        wgu_s[...] = wgu_f[...].astype(BF16)
        wdn_s[...] = wdn_f[...].astype(BF16)

        nxt = lax.while_loop(lambda p: (p < nblk_ref[0]) & (blke_ref[jnp.minimum(p, N_XBLOCKS - 1)] == blke_ref[b]),
                             lambda p: p + 1, b + 1)

        @pl.when(nxt < nblk_ref[0])
        def _():
            for cp in fetch(blke_ref[jnp.minimum(nxt, N_XBLOCKS - 1)]):
                cp.start()

    for groups in range(1, EXP_BLOCK // ROW_GROUP + 1):
        rows = groups * ROW_GROUP

        @pl.when(used & (live_ref[b] == groups))
        def _(rows=rows):
            hid = jnp.dot(_unpack_rows(xs_ref, rows), wgu_s[...], preferred_element_type=F32) + bgu_ref[...]
            gate = jnp.minimum(hid[:, :D_FF], SWIGLU_LIMIT)
            up = jnp.clip(hid[:, D_FF:], -SWIGLU_LIMIT, SWIGLU_LIMIT)
            act = (up + 1.0) * gate * jax.nn.sigmoid(SWIGLU_ALPHA * gate)
            y = jnp.dot(act.astype(BF16), wdn_s[...], preferred_element_type=F32) + bdn_ref[...]
            ys_ref[0:rows, :] = y.astype(BF16)
            if rows < EXP_BLOCK:
                ys_ref[rows:, :] = jnp.zeros((EXP_BLOCK - rows, D_MODEL), BF16)

    @pl.when(jnp.logical_not(used) | (live_ref[b] == 0))
    def _():
        ys_ref[...] = jnp.zeros_like(ys_ref)


def _expert_call(blke, nblk, live, xs, wgu, bgu, wdn, bdn):
    grid_spec = pltpu.PrefetchScalarGridSpec(
        num_scalar_prefetch=3,
        grid=(N_XBLOCKS,),
        in_specs=[
            pl.BlockSpec((EXP_BLOCK, PACK, LANES), lambda b, be, *_: (b, 0, 0)),
            pl.BlockSpec(memory_space=pl.ANY),
            pl.BlockSpec((None, 1, 2 * D_FF), lambda b, be, *_: (be[b], 0, 0)),
            pl.BlockSpec(memory_space=pl.ANY),
            pl.BlockSpec((None, 1, D_MODEL), lambda b, be, *_: (be[b], 0, 0)),
        ],
        out_specs=pl.BlockSpec((EXP_BLOCK, D_MODEL), lambda b, be, *_: (b, 0)),
        scratch_shapes=[pltpu.VMEM((D_MODEL, 2 * D_FF), F32), pltpu.VMEM((D_FF, D_MODEL), F32),
                        pltpu.VMEM((D_MODEL, 2 * D_FF), BF16), pltpu.VMEM((D_FF, D_MODEL), BF16),
                        pltpu.SemaphoreType.DMA((2,))],
    )
    return pl.pallas_call(
        _expert_kernel,
        grid_spec=grid_spec,
        out_shape=jax.ShapeDtypeStruct((N_ROWS, D_MODEL), BF16),
        compiler_params=pltpu.CompilerParams(
            dimension_semantics=("arbitrary",), vmem_limit_bytes=VMEM_LIMIT),
        name="moe_experts",
    )(blke, nblk, live, xs, wgu, bgu, wdn, bdn)


def _combine_kernel(astart_ref, nwin_ref, over_ref,
                    destm_ref, posm_ref, gm_ref, rep_ref, h1_ref, plep_ref, ples_ref, gple_ref, wpg_ref, wpp_ref,
                    gfin_ref, ys_any, yp_ref, ysm_ref, wins, moe_s, tmp_s, wsems, sem):
    i = pl.program_id(0)
    slot = i % 2

    def win_copy(step, e, into):
        first = pl.multiple_of(astart_ref[step * N_EXPERTS + e], WIN_ALIGN)
        return pltpu.make_async_copy(ys_any.at[pl.ds(first, WIN)], wins.at[into, pl.ds(e * WIN, WIN)],
                                     wsems.at[into])

    @pl.when(i == 0)
    def _():
        for e in range(N_EXPERTS):
            win_copy(0, e, 0).start()

    @pl.when(i + 1 < pl.num_programs(0))
    def _():
        for e in range(N_EXPERTS):
            win_copy(i + 1, e, 1 - slot).start()

    for e in range(N_EXPERTS):
        win_copy(i, e, slot).wait()

    gates = gm_ref[...]
    rep = rep_ref[...]
    pos_rep = jnp.dot(posm_ref[...].astype(BF16), rep, preferred_element_type=F32)
    gate_rep = jnp.dot(gates.astype(BF16), rep, preferred_element_type=F32)
    within = (_lane_iota((TOK_TILE, N_EXPERTS * WIN)) % WIN).astype(F32)
    gsel = jnp.where(pos_rep == within, gate_rep, 0.0).astype(BF16)
    mrows = pl.ds(pl.multiple_of((i % TAIL_TILES) * TOK_TILE, TOK_TILE), TOK_TILE)
    moe_s[mrows, :] = jnp.dot(gsel, wins[slot], preferred_element_type=F32)

    @pl.when(over_ref[i] > 0)
    def _():
        dest = destm_ref[...]
        lane = _lane_iota((TOK_TILE, LANES))
        lane_f = lane.astype(F32)
        tmp_s[...] = jnp.zeros_like(tmp_s)

        def per_expert(e, carry):
            a = astart_ref[i * N_EXPERTS + e]
            dcol = jnp.sum(jnp.where(lane == e, dest, 0.0), axis=-1, keepdims=True)
            gcol = jnp.sum(jnp.where(lane == e, gates, 0.0), axis=-1, keepdims=True)

            def per_window(w, carry2):
                first = a + w * WIN
                start = pl.multiple_of(jnp.minimum(first, N_ROWS - WIN), WIN_ALIGN)
                cp = pltpu.make_async_copy(ys_any.at[pl.ds(start, WIN)], tmp_s.at[pl.ds(0, WIN)], sem)
                cp.start()
                cp.wait()
                hit = (lane < WIN) & (dcol == start.astype(F32) + lane_f) & (dcol >= first.astype(F32))
                sel = jnp.where(hit, gcol, 0.0).astype(BF16)
                moe_s[mrows, :] += jnp.dot(sel, tmp_s[...], preferred_element_type=F32)
                return carry2

            return lax.fori_loop(1, nwin_ref[i * N_EXPERTS + e], per_window, carry)

        lax.fori_loop(0, N_EXPERTS, per_expert, 0)

    def tail(h1, moe_rows, ple):
        h2 = h1 + moe_rows
        hn = _rms(h2, gple_ref[...]).astype(BF16)
        gate = jax.nn.sigmoid(jnp.dot(hn, wpg_ref[...], preferred_element_type=F32))
        proj = jnp.dot(ple.astype(BF16), wpp_ref[...], preferred_element_type=F32)
        return _rms(h2 + gate * proj, gfin_ref[...])

    is_sample = i == N_CTILES - 1

    @pl.when((i % TAIL_TILES == TAIL_TILES - 1) & jnp.logical_not(is_sample))
    def _():
        yp_ref[...] = tail(h1_ref[...], moe_s[...], plep_ref[...])

    @pl.when(is_sample)
    def _():
        ysm_ref[...] = tail(h1_ref[0:DEC_BATCH, :], moe_s[0:DEC_BATCH, :], ples_ref[...])


def _combine_call(astart, nwin, over, ys, destm, posm, gm, h1, plep, ples, gple, wpg, wpp, gfin):
    rep = (jnp.arange(N_EXPERTS * WIN)[None, :] // WIN == jnp.arange(LANES)[:, None]).astype(BF16)
    rows = TAIL_TILES * TOK_TILE
    last = N_PROMPT // rows - 1

    grid_spec = pltpu.PrefetchScalarGridSpec(
        num_scalar_prefetch=3,
        grid=(N_CTILES,),
        in_specs=[
            pl.BlockSpec((TOK_TILE, LANES), lambda i, *_: (i, 0)),
            pl.BlockSpec((TOK_TILE, LANES), lambda i, *_: (i, 0)),
            pl.BlockSpec((TOK_TILE, LANES), lambda i, *_: (i, 0)),
            pl.BlockSpec((LANES, N_EXPERTS * WIN), lambda i, *_: (0, 0)),
            pl.BlockSpec((rows, D_MODEL), lambda i, *_: (i // TAIL_TILES, 0)),
            pl.BlockSpec((rows, PLE_DIM), lambda i, *_: (jnp.minimum(i // TAIL_TILES, last), 0)),
            pl.BlockSpec((DEC_BATCH, PLE_DIM), lambda i, *_: (0, 0)),
            pl.BlockSpec((1, D_MODEL), lambda i, *_: (0, 0)),
            pl.BlockSpec((D_MODEL, D_MODEL), lambda i, *_: (0, 0)),
            pl.BlockSpec((PLE_DIM, D_MODEL), lambda i, *_: (0, 0)),
            pl.BlockSpec((1, D_MODEL), lambda i, *_: (0, 0)),
            pl.BlockSpec(memory_space=pl.ANY),
        ],
        out_specs=[
            pl.BlockSpec((rows, D_MODEL), lambda i, *_: (jnp.minimum(i // TAIL_TILES, last), 0)),
            pl.BlockSpec((DEC_BATCH, D_MODEL), lambda i, *_: (0, 0)),
        ],
        scratch_shapes=[pltpu.VMEM((2, N_EXPERTS * WIN, D_MODEL), BF16), pltpu.VMEM((rows, D_MODEL), F32),
                        pltpu.VMEM((2 * WIN, D_MODEL), BF16), pltpu.SemaphoreType.DMA((2,)),
                        pltpu.SemaphoreType.DMA],
    )
    return pl.pallas_call(
        _combine_kernel,
        grid_spec=grid_spec,
        out_shape=[jax.ShapeDtypeStruct((N_PROMPT, D_MODEL), F32),
                   jax.ShapeDtypeStruct((DEC_BATCH, D_MODEL), F32)],
        compiler_params=pltpu.CompilerParams(
            dimension_semantics=("arbitrary",), vmem_limit_bytes=VMEM_LIMIT),
        name="moe_combine_tail",
    )(astart, nwin, over, destm, posm, gm, rep, h1, plep, ples, gple, wpg, wpp, gfin, ys)


def kernel(x_prompt, x_sample, cache_swa_k, cache_swa_v, p_prompt, p_sample, g_mix, w_in, ln_v_g, ln_v_b,
           w_sp, b_sp, sinks, g_out_a, g_out_b, w_o, g_moe, w_router, b_router, w_gu, b_gu, w_dn, b_dn,
           g_ple, w_ple_gate, w_ple_proj, g_final):
    l = 0
    row = lambda v: v.reshape(1, -1)
    win = w_in[l].astype(BF16)
    wo = w_o[l].astype(BF16)
    tril = jnp.tril(jnp.ones((CHUNK, CHUNK), dtype=bool))
    wsp = jnp.where(tril, w_sp[l], 0.0).astype(BF16).reshape(4, 2 * CHUNK, CHUNK)
    bsp = jnp.repeat(b_sp[l].T, HEAD_DIM, axis=1)
    w00 = row(jnp.repeat(w_sp[l][:, 0, 0], HEAD_DIM))
    b0 = row(jnp.repeat(b_sp[l][:, 0], HEAD_DIM))
    wr_hi = w_router[l].astype(BF16)
    wr_lo = (w_router[l] - wr_hi.astype(F32)).astype(BF16)
    wr = jnp.concatenate([wr_hi, wr_lo, jnp.zeros((D_MODEL, LANES - 2 * N_EXPERTS), BF16)], axis=1)
    br = row(jnp.concatenate([b_router[l], jnp.zeros((LANES - N_EXPERTS,), F32)]))
    common = (row(g_mix[l]), win, row(ln_v_g[l]), row(ln_v_b[l]))
    tail = (row(g_out_a[l]), row(g_out_b[l]), wo, row(g_moe[l]), wr, br)

    h1, xn, gm, sm, k_p, v_p, xs_zero, cnt = _prompt_call(
        x_prompt.reshape(N_PROMPT, D_MODEL), sinks[l], *common, wsp, bsp, *tail)
    chan = lambda a: jnp.transpose(a, (0, 2, 3, 1)).reshape(DEC_BATCH, KV_WIDTH, CHUNK)
    ck = chan(cache_swa_k[l])
    cv = chan(cache_swa_v[l])
    h1, xn, gm, sm, k_s, v_s, va_s, cnt = _decode_call(
        x_sample.reshape(DEC_BATCH, D_MODEL), ck, cv, sinks[l], *common, w00, b0, *tail, h1, xn, gm, sm, cnt)

    destm, posm, destt, xbe, live, nxblk, stab, ctab, astart, nwin = _plan_call(sm, cnt)
    flat = lambda tab, n: tab[:n, :N_EXPERTS].reshape(-1)
    over = (jnp.max(nwin[:N_CTILES, :N_EXPERTS], axis=1) > 1).astype(jnp.int32)
    cmax = jnp.max(ctab[:N_CTILES, :N_EXPERTS], axis=1)

    xs = _dispatch_call(flat(stab, N_CTILES), flat(ctab, N_CTILES), cmax, xn, destt[:N_EXPERTS], xs_zero)
    ys = _expert_call(xbe[:N_XBLOCKS, 0], nxblk[0, :1], live[:N_XBLOCKS, 0], xs,
                      w_gu[l], b_gu[l].reshape(N_EXPERTS, 1, 2 * D_FF), w_dn[l], b_dn[l].reshape(N_EXPERTS, 1, D_MODEL))
    y_p, y_s = _combine_call(
        flat(astart, N_CTILES), flat(nwin, N_CTILES), over, ys, destm, posm, gm, h1,
        p_prompt[l].reshape(N_PROMPT, PLE_DIM), p_sample[l].reshape(DEC_BATCH, PLE_DIM),
        row(g_ple[l]), w_ple_gate[l].astype(BF16), w_ple_proj[l].astype(BF16), row(g_final))

    kv5 = lambda a, n: a.reshape(1, n, CHUNK, 2, HEAD_DIM)
    win5 = lambda a: jnp.transpose(a.reshape(DEC_BATCH, 2, HEAD_DIM, CHUNK), (0, 3, 1, 2))[None]
    return (y_p.reshape(BATCH, SEQ, D_MODEL), y_s.reshape(DEC_BATCH, 1, D_MODEL),
            kv5(k_p, BATCH), kv5(v_p, BATCH), win5(k_s), win5(v_s),
            va_s.reshape(1, DEC_BATCH, 1, A_WIDTH))
```

```python
import math

import jax
import jax.numpy as jnp
from jax import lax
from jax.experimental import pallas as pl
from jax.experimental.pallas import tpu as pltpu

F32 = jnp.float32
BF16 = jnp.bfloat16

D_MODEL = 1024
BATCH = 4
SEQ = 4096
DEC_BATCH = 128
HEAD_DIM = 64
A_WIDTH = 512
B_WIDTH = 512
B_HEADS = 8
KV_WIDTH = 128
IN_WIDTH = 2 * A_WIDTH + B_WIDTH + 2 * KV_WIDTH
CHUNK = 128
N_EXPERTS = 32
TOP_K = 4
D_FF = 1024
SWIGLU_ALPHA = 1.702
SWIGLU_LIMIT = 7.0
PLE_DIM = 256
EPS = 1e-5

LANES = 128
VMEM_LIMIT = 56 * 1024 * 1024
N_PROMPT = BATCH * SEQ
N_TOK = N_PROMPT + DEC_BATCH
TM = 1024
N_PAD = ((N_TOK + TM - 1) // TM) * TM
TOK_TILE = 128
N_PAD_TILES = N_PAD // TOK_TILE
N_CTILES = N_PROMPT // TOK_TILE + 1
DEC_TILE = 16
EXP_BLOCK = 512
ROW_GROUP = 128
DISP_CHUNK = 32
N_XBLOCKS = (N_TOK * TOP_K + N_EXPERTS * (DISP_CHUNK + EXP_BLOCK - 1) + EXP_BLOCK - 1) // EXP_BLOCK
N_ROWS = N_XBLOCKS * EXP_BLOCK
XS_ROWS = N_ROWS + N_EXPERTS * DISP_CHUNK
ZERO_ROWS = EXP_BLOCK
ZERO_COPIES = XS_ROWS // ZERO_ROWS
ZERO_STEPS = N_PROMPT // TM - 1
ZERO_PER_STEP = ZERO_COPIES // ZERO_STEPS
assert ZERO_PER_STEP * ZERO_STEPS == ZERO_COPIES and ZERO_COPIES * ZERO_ROWS == XS_ROWS
XBE_ROWS = ((N_XBLOCKS + 7) // 8) * 8
TAB_ROWS = ((N_PAD_TILES + 7) // 8) * 8
PACK = D_MODEL // 2 // LANES
WIN = 64
WIN_ALIGN = 16
WIN_SLOTS = 3
TAIL_TILES = 4
NEG = -1e30


def _rms(x, g):
    return x * lax.rsqrt(jnp.mean(x * x, axis=-1, keepdims=True) + EPS) * g


def _gelu(x):
    c = math.sqrt(2.0 / math.pi)
    return x * (0.5 * (1.0 + jnp.tanh(c * (x + 0.044715 * (x * x * x)))))


def _layernorm(x, g, b):
    mu = jnp.mean(x, axis=-1, keepdims=True)
    xc = x - mu
    return xc * lax.rsqrt(jnp.mean(xc * xc, axis=-1, keepdims=True) + EPS) * g + b


def _lane_iota(shape):
    return lax.broadcasted_iota(jnp.int32, shape, len(shape) - 1)


def _full(shape):
    n = len(shape)
    return pl.BlockSpec(shape, lambda *_: (0,) * n)


def _route(xn2, wr_ref, br_ref):
    m = xn2.shape[0]
    xh = xn2.astype(BF16)
    xl = (xn2 - xh.astype(F32)).astype(BF16)
    r = jnp.dot(jnp.concatenate([xh, xl], axis=0), wr_ref[...], preferred_element_type=F32)
    r = r[:m] + r[m:]
    lane = _lane_iota((m, LANES))
    lane_f = lane.astype(F32)
    logits = jnp.where(lane < N_EXPERTS, r + pltpu.roll(r, LANES - N_EXPERTS, 1) + br_ref[...], NEG)
    work = logits
    sel = jnp.zeros((m, LANES), F32)
    top = None
    z = None
    for _ in range(TOP_K):
        mx = jnp.max(work, axis=-1, keepdims=True)
        first = jnp.min(jnp.where(work == mx, lane_f, float(LANES)), axis=-1, keepdims=True)
        hit = lane_f == first
        sel = jnp.where(hit, 1.0, sel)
        work = jnp.where(hit, NEG, work)
        if top is None:
            top = mx
            z = jnp.ones_like(mx)
        else:
            z = z + jnp.exp(mx - top)
    gates = jnp.where(sel > 0.0, jnp.exp(logits - top) / z, 0.0)
    return gates, sel


def _prompt_kernel(sinks_ref, x_ref, gmix_ref, win_ref, lng_ref, lnb_ref, wsp_ref, bsp_ref,
                   goa_ref, gob_ref, wo_ref, gmoe_ref, wr_ref, br_ref,
                   h1_ref, xn_ref, gm_ref, sm_ref, k_ref, v_ref, xs_ref, cnt_ref,
                   z_s, kv_s, cat_s, zero_s, zsems):
    g = pl.program_id(0)

    @pl.when(g == 0)
    def _():
        cnt_ref[...] = jnp.zeros_like(cnt_ref)

    def zero_copy(step, k):
        chunk = step * ZERO_PER_STEP + k
        return pltpu.make_async_copy(zero_s, xs_ref.at[pl.ds(chunk * ZERO_ROWS, ZERO_ROWS)], zsems.at[step % 2])

    @pl.when(g == 0)
    def _():
        zero_s[...] = jnp.zeros_like(zero_s)

    @pl.when((g >= 1) & (g <= ZERO_STEPS))
    def _():
        for k in range(ZERO_PER_STEP):
            zero_copy(g - 1, k).wait()

    @pl.when(g < ZERO_STEPS)
    def _():
        for k in range(ZERO_PER_STEP):
            zero_copy(g, k).start()

    @pl.when(g >= N_PROMPT // TM)
    def _():
        h1_ref[...] = jnp.zeros_like(h1_ref)
        xn_ref[...] = jnp.zeros_like(xn_ref)
        gm_ref[...] = jnp.zeros_like(gm_ref)
        sm_ref[...] = jnp.zeros_like(sm_ref)

    @pl.when(g < N_PROMPT // TM)
    def _():
        _prompt_tile(g % (SEQ // TM), sinks_ref, x_ref, gmix_ref, win_ref, lng_ref, lnb_ref, wsp_ref, bsp_ref,
                     goa_ref, gob_ref, wo_ref, gmoe_ref, wr_ref, br_ref,
                     h1_ref, xn_ref, gm_ref, sm_ref, k_ref, v_ref, z_s, kv_s, cat_s)
        cnt_ref[...] += jnp.broadcast_to(jnp.sum(sm_ref[...], axis=0, keepdims=True), cnt_ref.shape)


def _prompt_tile(j, sinks_ref, x_ref, gmix_ref, win_ref, lng_ref, lnb_ref, wsp_ref, bsp_ref,
                 goa_ref, gob_ref, wo_ref, gmoe_ref, wr_ref, br_ref,
                 h1_ref, xn_ref, gm_ref, sm_ref, k_ref, v_ref, z_s, kv_s, cat_s):
    @pl.when(j == 0)
    def _():
        kv_s[0:CHUNK, :] = jnp.zeros((CHUNK, 2 * KV_WIDTH), F32)

    xn = _rms(x_ref[...], gmix_ref[...]).astype(BF16)
    z_s[...] = jnp.dot(xn, win_ref[...], preferred_element_type=F32)
    kv_s[CHUNK:, :] = z_s[:, 2 * A_WIDTH + B_WIDTH:]

    lane = _lane_iota((CHUNK, LANES))
    lo = lane < HEAD_DIM
    lane2 = _lane_iota((2 * CHUNK, LANES))
    lo2 = lane2 < HEAD_DIM
    qi = lax.broadcasted_iota(jnp.int32, (CHUNK, CHUNK), 0)
    kc = lax.broadcasted_iota(jnp.int32, (CHUNK, CHUNK), 1)
    from_prev = kc > qi
    dist = jnp.where(from_prev, qi + CHUNK - kc, qi - kc).astype(F32)

    def chunk_body(c, carry):
        r0 = pl.multiple_of(c * CHUNK, CHUNK)
        rows = pl.ds(r0, CHUNK)
        u = _gelu(z_s[rows, 0:A_WIDTH])
        va = _layernorm(_gelu(z_s[rows, A_WIDTH:2 * A_WIDTH]), lng_ref[...], lnb_ref[...])
        vab = va.astype(BF16)
        slabs = []
        for p in range(A_WIDTH // LANES):
            slab = vab[:, p * LANES:(p + 1) * LANES]
            m = jnp.dot(wsp_ref[p], slab, preferred_element_type=F32)
            slabs.append(jnp.where(lo, m[:CHUNK], m[CHUNK:]))
        ya = u * (jnp.concatenate(slabs, axis=-1) + bsp_ref[...])
        ya_n = _rms(ya, goa_ref[...])
        k2 = kv_s[pl.ds(r0, 2 * CHUNK), 0:KV_WIDTH]
        v2 = kv_s[pl.ds(r0, 2 * CHUNK), KV_WIDTH:2 * KV_WIDTH]
        k2r = pltpu.roll(k2, HEAD_DIM, 1)
        v2r = pltpu.roll(v2, HEAD_DIM, 1)
        kd = (jnp.where(lo2, k2, k2r).astype(BF16), jnp.where(lo2, k2r, k2).astype(BF16))
        vd = (jnp.where(lo2, v2, v2r).astype(BF16), jnp.where(lo2, v2r, v2).astype(BF16))
        prev_ok = (j > 0) | (c > 0)
        masked = from_prev & jnp.logical_not(prev_ok)
        yb_slabs = []
        for kv in range(2):
            q0 = z_s[rows, 2 * A_WIDTH + (2 * kv) * LANES:2 * A_WIDTH + (2 * kv + 1) * LANES]
            q1 = z_s[rows, 2 * A_WIDTH + (2 * kv + 1) * LANES:2 * A_WIDTH + (2 * kv + 2) * LANES]
            lhs = jnp.concatenate([jnp.where(lo, q0, 0.0), jnp.where(lo, 0.0, q0),
                                   jnp.where(lo, q1, 0.0), jnp.where(lo, 0.0, q1)], axis=0).astype(BF16)
            s_all = lax.dot_general(lhs, kd[kv], (((1,), (1,)), ((), ())), preferred_element_type=F32)
            probs = []
            for i in range(4):
                h = 4 * kv + i
                slope = 2.0 ** (-(h + 1))
                sink = sinks_ref[h]
                sh = s_all[i * CHUNK:(i + 1) * CHUNK]
                s = jnp.where(from_prev, sh[:, :CHUNK], sh[:, CHUNK:]) * (HEAD_DIM ** -0.5) - slope * dist
                s = jnp.where(masked, NEG, s)
                mx = jnp.maximum(jnp.max(s, axis=-1, keepdims=True), sink)
                e = jnp.exp(s - mx)
                den = jnp.sum(e, axis=-1, keepdims=True) + jnp.exp(sink - mx)
                p = e * (1.0 / den)
                probs.append(jnp.concatenate([jnp.where(from_prev, p, 0.0), jnp.where(from_prev, 0.0, p)], axis=-1))
            pm = jnp.concatenate(probs, axis=0).astype(BF16)
            o = jnp.dot(pm, vd[kv], preferred_element_type=F32)
            yb_slabs.append(jnp.where(lo, o[0:CHUNK], o[CHUNK:2 * CHUNK]))
            yb_slabs.append(jnp.where(lo, o[2 * CHUNK:3 * CHUNK], o[3 * CHUNK:4 * CHUNK]))
        yb_n = _rms(jnp.concatenate(yb_slabs, axis=-1), gob_ref[...])
        cat_s[rows, 0:A_WIDTH] = ya_n.astype(BF16)
        cat_s[rows, A_WIDTH:] = yb_n.astype(BF16)
        return carry

    lax.fori_loop(0, TM // CHUNK, chunk_body, 0)

    kv_s[0:CHUNK, :] = kv_s[TM:TM + CHUNK, :]
    k_ref[...] = kv_s[TM:TM + CHUNK, 0:KV_WIDTH]
    v_ref[...] = kv_s[TM:TM + CHUNK, KV_WIDTH:]

    h1 = x_ref[...] + jnp.dot(cat_s[...], wo_ref[...], preferred_element_type=F32)
    h1_ref[...] = h1
    xn2 = _rms(h1, gmoe_ref[...])
    xn_ref[...] = xn2.astype(BF16)
    gates, sel = _route(xn2, wr_ref, br_ref)
    gm_ref[...] = gates
    sm_ref[...] = sel


def _prompt_call(x, sinks, gmix, win, lng, lnb, wsp, bsp, goa, gob, wo, gmoe, wr, br):
    real = N_PROMPT // TM
    row = lambda g: (g, 0)
    seq = lambda g: (jnp.minimum(g, real - 1) // (SEQ // TM), 0, 0)
    return pl.pallas_call(
        _prompt_kernel,
        grid=(N_PAD // TM,),
        in_specs=[
            pl.BlockSpec(memory_space=pltpu.SMEM),
            pl.BlockSpec((TM, D_MODEL), lambda g: (jnp.minimum(g, real - 1), 0)),
            _full((1, D_MODEL)), _full((D_MODEL, IN_WIDTH)), _full((1, A_WIDTH)), _full((1, A_WIDTH)),
            _full((4, 2 * CHUNK, CHUNK)), _full((CHUNK, A_WIDTH)), _full((1, A_WIDTH)), _full((1, B_WIDTH)),
            _full((D_MODEL, D_MODEL)), _full((1, D_MODEL)), _full((D_MODEL, LANES)), _full((1, LANES)),
        ],
        out_specs=[
            pl.BlockSpec((TM, D_MODEL), row),
            pl.BlockSpec((TM, D_MODEL), row),
            pl.BlockSpec((TM, LANES), row),
            pl.BlockSpec((TM, LANES), row),
            pl.BlockSpec((None, CHUNK, KV_WIDTH), seq),
            pl.BlockSpec((None, CHUNK, KV_WIDTH), seq),
            pl.BlockSpec(memory_space=pl.ANY),
            _full((8, LANES)),
        ],
        out_shape=[
            jax.ShapeDtypeStruct((N_PAD, D_MODEL), F32),
            jax.ShapeDtypeStruct((N_PAD, D_MODEL), BF16),
            jax.ShapeDtypeStruct((N_PAD, LANES), F32),
            jax.ShapeDtypeStruct((N_PAD, LANES), F32),
            jax.ShapeDtypeStruct((BATCH, CHUNK, KV_WIDTH), F32),
            jax.ShapeDtypeStruct((BATCH, CHUNK, KV_WIDTH), F32),
            jax.ShapeDtypeStruct((XS_ROWS, PACK, LANES), jnp.int32),
            jax.ShapeDtypeStruct((8, LANES), F32),
        ],
        scratch_shapes=[
            pltpu.VMEM((TM, IN_WIDTH), F32),
            pltpu.VMEM((TM + CHUNK, 2 * KV_WIDTH), F32),
            pltpu.VMEM((TM, D_MODEL), BF16),
            pltpu.VMEM((ZERO_ROWS, PACK, LANES), jnp.int32),
            pltpu.SemaphoreType.DMA((2,)),
        ],
        compiler_params=pltpu.CompilerParams(
            dimension_semantics=("arbitrary",), vmem_limit_bytes=VMEM_LIMIT),
        name="prompt_premoe",
    )(sinks, x, gmix, win, lng, lnb, wsp, bsp, goa, gob, wo, gmoe, wr, br)


def _decode_kernel(sinks_ref, x_ref, ck_ref, cv_ref, gmix_ref, win_ref, lng_ref, lnb_ref, w00_ref, b0_ref,
                   goa_ref, gob_ref, wo_ref, gmoe_ref, wr_ref, br_ref,
                   h1_in, xn_in, gm_in, sm_in, cnt_in,
                   h1_ref, xn_ref, gm_ref, sm_ref, nk_ref, nv_ref, va_ref, cnt_ref,
                   q_s, kn_s, vn_s, ya_s, yb_s, qm_s, o_s):
    del h1_in, xn_in, gm_in, sm_in
    i = pl.program_id(0)
    t = DEC_TILE

    @pl.when(i == 0)
    def _():
        xn = _rms(x_ref[...], gmix_ref[...]).astype(BF16)
        z = jnp.dot(xn, win_ref[...], preferred_element_type=F32)
        u = _gelu(z[:, 0:A_WIDTH])
        va = _layernorm(_gelu(z[:, A_WIDTH:2 * A_WIDTH]), lng_ref[...], lnb_ref[...])
        va_ref[...] = va
        ya_s[...] = _rms(u * (w00_ref[...] * va + b0_ref[...]), goa_ref[...])
        q_s[...] = z[:, 2 * A_WIDTH:2 * A_WIDTH + B_WIDTH]
        kn_s[...] = z[:, 2 * A_WIDTH + B_WIDTH:2 * A_WIDTH + B_WIDTH + KV_WIDTH]
        vn_s[...] = z[:, 2 * A_WIDTH + B_WIDTH + KV_WIDTH:]

    rows = pl.ds(pl.multiple_of(i * t, t), t)
    q = q_s[rows, :]
    kn = kn_s[rows, :]
    vn = vn_s[rows, :]
    lane = _lane_iota((t, LANES))
    lo = lane < HEAD_DIM
    stacked = []
    for h in range(B_HEADS):
        qh = jnp.where(lo if h % 2 == 0 else jnp.logical_not(lo), q[:, (h // 2) * LANES:(h // 2 + 1) * LANES], 0.0)
        stacked.append(pltpu.roll(qh, HEAD_DIM, 1) if h % 2 != h // 4 else qh)
    qm_s[...] = jnp.concatenate(stacked, axis=0)
    hrow = lax.broadcasted_iota(jnp.int32, (B_HEADS, 1), 0)
    slope = jnp.zeros((B_HEADS, 1), F32)
    sink = jnp.zeros((B_HEADS, 1), F32)
    for h in range(B_HEADS):
        slope = jnp.where(hrow == h, 2.0 ** (-(h + 1)), slope)
        sink = jnp.where(hrow == h, sinks_ref[h], sink)
    pos = _lane_iota((B_HEADS, CHUNK))
    bias = slope * (CHUNK - pos).astype(F32)
    on_kv = (_lane_iota((B_HEADS, LANES)) < HEAD_DIM) == (hrow < B_HEADS // 2)
    last_pos = _lane_iota((KV_WIDTH, CHUNK)) == CHUNK - 1
    last_row = lax.broadcasted_iota(jnp.int32, (CHUNK, KV_WIDTH), 0) == CHUNK - 1
    for b in range(t):
        qb = qm_s[pl.ds(b, B_HEADS, stride=t), :]
        kt = ck_ref[b]
        vt = cv_ref[b]
        s_c = jnp.dot(qb.astype(BF16), kt.astype(BF16), preferred_element_type=F32) * (HEAD_DIM ** -0.5) - bias
        s_c = jnp.where(pos >= 1, s_c, NEG)
        s_n = jnp.sum(qb * kn[b:b + 1, :], axis=-1, keepdims=True) * (HEAD_DIM ** -0.5)
        mx = jnp.maximum(jnp.maximum(jnp.max(s_c, axis=-1, keepdims=True), s_n), sink)
        e_c = jnp.exp(s_c - mx)
        e_n = jnp.exp(s_n - mx)
        inv = 1.0 / (jnp.sum(e_c, axis=-1, keepdims=True) + e_n + jnp.exp(sink - mx))
        o = lax.dot_general((e_c * inv).astype(BF16), vt.astype(BF16), (((1,), (1,)), ((), ())),
                            preferred_element_type=F32)
        o = o + (e_n * inv) * vn[b:b + 1, :]
        o_s[pl.ds(b, B_HEADS, stride=t), :] = jnp.where(on_kv, o, 0.0)
        nk_ref[b] = jnp.where(last_pos, jnp.where(last_row, kn[b:b + 1, :], 0.0).T, pltpu.roll(kt, CHUNK - 1, 1))
        nv_ref[b] = jnp.where(last_pos, jnp.where(last_row, vn[b:b + 1, :], 0.0).T, pltpu.roll(vt, CHUNK - 1, 1))
    slabs = []
    for p in range(B_WIDTH // LANES):
        pair = []
        for h in (2 * p, 2 * p + 1):
            oh = o_s[h * t:(h + 1) * t, :]
            pair.append(pltpu.roll(oh, HEAD_DIM, 1) if h % 2 != h // 4 else oh)
        slabs.append(pair[0] + pair[1])
    yb_s[rows, :] = jnp.concatenate(slabs, axis=-1)

    @pl.when(i == pl.num_programs(0) - 1)
    def _():
        yb_n = _rms(yb_s[...], gob_ref[...])
        cat = jnp.concatenate([ya_s[...], yb_n], axis=-1).astype(BF16)
        h1 = x_ref[...] + jnp.dot(cat, wo_ref[...], preferred_element_type=F32)
        xn2 = _rms(h1, gmoe_ref[...])
        gates, sel = _route(xn2, wr_ref, br_ref)
        h1_ref[...] = h1
        xn_ref[...] = xn2.astype(BF16)
        gm_ref[...] = gates
        sm_ref[...] = sel
        cnt_ref[...] = cnt_in[...] + jnp.broadcast_to(jnp.sum(sel, axis=0, keepdims=True), cnt_ref.shape)


def _decode_call(x, ck, cv, sinks, gmix, win, lng, lnb, w00, b0, goa, gob, wo, gmoe, wr, br, h1, xn, gm, sm, cnt):
    t = DEC_TILE
    cache = pl.BlockSpec((t, KV_WIDTH, CHUNK), lambda i: (i, 0, 0))
    tok = lambda width: pl.BlockSpec((DEC_BATCH, width), lambda i: (N_PROMPT // DEC_BATCH, 0))
    anyspec = pl.BlockSpec(memory_space=pl.ANY)
    return pl.pallas_call(
        _decode_kernel,
        grid=(DEC_BATCH // t,),
        in_specs=[
            pl.BlockSpec(memory_space=pltpu.SMEM),
            _full((DEC_BATCH, D_MODEL)), cache, cache,
            _full((1, D_MODEL)), _full((D_MODEL, IN_WIDTH)), _full((1, A_WIDTH)), _full((1, A_WIDTH)),
            _full((1, A_WIDTH)), _full((1, A_WIDTH)), _full((1, A_WIDTH)), _full((1, B_WIDTH)),
            _full((D_MODEL, D_MODEL)), _full((1, D_MODEL)), _full((D_MODEL, LANES)), _full((1, LANES)),
            anyspec, anyspec, anyspec, anyspec, _full((8, LANES)),
        ],
        out_specs=[tok(D_MODEL), tok(D_MODEL), tok(LANES), tok(LANES), cache, cache, _full((DEC_BATCH, A_WIDTH)),
                   _full((8, LANES))],
        out_shape=[
            jax.ShapeDtypeStruct((N_PAD, D_MODEL), F32),
            jax.ShapeDtypeStruct((N_PAD, D_MODEL), BF16),
            jax.ShapeDtypeStruct((N_PAD, LANES), F32),
            jax.ShapeDtypeStruct((N_PAD, LANES), F32),
            jax.ShapeDtypeStruct((DEC_BATCH, KV_WIDTH, CHUNK), F32),
            jax.ShapeDtypeStruct((DEC_BATCH, KV_WIDTH, CHUNK), F32),
            jax.ShapeDtypeStruct((DEC_BATCH, A_WIDTH), F32),
            jax.ShapeDtypeStruct((8, LANES), F32),
        ],
        scratch_shapes=[pltpu.VMEM((DEC_BATCH, B_WIDTH), F32), pltpu.VMEM((DEC_BATCH, KV_WIDTH), F32),
                        pltpu.VMEM((DEC_BATCH, KV_WIDTH), F32), pltpu.VMEM((DEC_BATCH, A_WIDTH), F32),
                        pltpu.VMEM((DEC_BATCH, B_WIDTH), F32), pltpu.VMEM((B_HEADS * t, LANES), F32),
                        pltpu.VMEM((B_HEADS * t, LANES), F32)],
        input_output_aliases={16: 0, 17: 1, 18: 2, 19: 3},
        compiler_params=pltpu.CompilerParams(
            dimension_semantics=("arbitrary",), vmem_limit_bytes=VMEM_LIMIT),
        name="sample_premoe",
    )(sinks, x, ck, cv, gmix, win, lng, lnb, w00, b0, goa, gob, wo, gmoe, wr, br, h1, xn, gm, sm, cnt)


def _plan_kernel(sm_ref, cnt_ref,
                 destm_ref, posm_ref, destt_ref, xbe_ref, live_ref, nxblk_ref, stab_ref, ctab_ref, astart_ref,
                 nwin_ref, base_s, pstart_s):
    step = pl.program_id(0)
    lane = _lane_iota((1, LANES))

    @pl.when(step == 0)
    def _():
        counts = cnt_ref[0:1, :]
        padded = jnp.floor((counts + (DISP_CHUNK + EXP_BLOCK - 1)) * (1.0 / EXP_BLOCK)) * EXP_BLOCK
        padded = jnp.where(counts > 0.0, padded, 0.0)
        pend = padded
        for s in (1, 2, 4, 8, 16):
            pend = pend + jnp.where(lane >= s, pltpu.roll(pend, s, 1), 0.0)
        spare = (N_ROWS + lane * DISP_CHUNK).astype(F32)
        pstart_s[...] = jnp.where(counts > 0.0, pend - padded, spare)
        base_s[...] = jnp.zeros_like(base_s)
        brow = lax.broadcasted_iota(jnp.int32, (XBE_ROWS, LANES), 0).astype(F32) * EXP_BLOCK
        done = jnp.where((lane < N_EXPERTS) & (pend <= brow), 1.0, 0.0)
        be = jnp.minimum(jnp.sum(done, axis=-1, keepdims=True), N_EXPERTS - 1.0)
        xbe_ref[...] = jnp.broadcast_to(be, (XBE_ROWS, LANES)).astype(jnp.int32)
        real = jnp.clip(counts - (brow - (pend - padded)), 0.0, float(EXP_BLOCK))
        real = jnp.sum(jnp.where(lane.astype(F32) == be, real, 0.0), axis=-1, keepdims=True)
        groups = jnp.floor((real + (ROW_GROUP - 1)) * (1.0 / ROW_GROUP))
        live_ref[...] = jnp.broadcast_to(groups, (XBE_ROWS, LANES)).astype(jnp.int32)
        total = jnp.sum(jnp.where(lane == N_EXPERTS - 1, pend, 0.0), axis=-1, keepdims=True)
        nxblk_ref[...] = jnp.broadcast_to(total * (1.0 / EXP_BLOCK), (8, LANES)).astype(jnp.int32)
        stab_ref[...] = jnp.zeros_like(stab_ref)
        ctab_ref[...] = jnp.zeros_like(ctab_ref)
        astart_ref[...] = jnp.zeros_like(astart_ref)
        nwin_ref[...] = jnp.zeros_like(nwin_ref)

    def walk():
        r = lax.broadcasted_iota(jnp.int32, (TOK_TILE, TOK_TILE), 0)
        c = lax.broadcasted_iota(jnp.int32, (TOK_TILE, TOK_TILE), 1)
        lower = jnp.where(c < r, 1.0, 0.0).astype(BF16)
        for q in range(TM // TOK_TILE):
            i = step * (TM // TOK_TILE) + q
            sel = sm_ref[q * TOK_TILE:(q + 1) * TOK_TILE, :]
            cnt = jnp.sum(sel, axis=0, keepdims=True)
            prefix = jnp.dot(lower, sel.astype(BF16), preferred_element_type=F32)
            start = pstart_s[...] + base_s[...]
            dest = jnp.where(sel > 0.0, prefix + start, -1.0)
            destm_ref[q * TOK_TILE:(q + 1) * TOK_TILE, :] = dest
            destt_ref[:, q * TOK_TILE:(q + 1) * TOK_TILE] = dest.T
            has = (cnt > 0.0) & (lane < N_EXPERTS)
            stab_ref[pl.ds(i, 1), :] = start.astype(jnp.int32)
            ctab_ref[pl.ds(i, 1), :] = jnp.where(has, cnt, 0.0).astype(jnp.int32)
            a = jnp.minimum(jnp.floor(start * (1.0 / WIN_ALIGN)) * WIN_ALIGN, float(N_ROWS - WIN))
            nw = jnp.where(has, jnp.floor((start + cnt - a + (WIN - 1)) * (1.0 / WIN)), 0.0)
            posm_ref[q * TOK_TILE:(q + 1) * TOK_TILE, :] = jnp.where(sel > 0.0, prefix + start - a, -1.0)
            astart_ref[pl.ds(i, 1), :] = a.astype(jnp.int32)
            nwin_ref[pl.ds(i, 1), :] = nw.astype(jnp.int32)
            base_s[...] += cnt

    walk()


def _plan_call(sm, cnt):
    tile = lambda i: (i, 0)
    tile_t = lambda i: (0, i)
    tab = jax.ShapeDtypeStruct((TAB_ROWS, LANES), jnp.int32)
    return pl.pallas_call(
        _plan_kernel,
        grid=(N_PAD // TM,),
        in_specs=[pl.BlockSpec((TM, LANES), lambda i: (i, 0)), _full((8, LANES))],
        out_specs=[
            pl.BlockSpec((TM, LANES), tile),
            pl.BlockSpec((TM, LANES), tile),
            pl.BlockSpec((LANES, TM), tile_t),
            _full((XBE_ROWS, LANES)), _full((XBE_ROWS, LANES)), _full((8, LANES)),
            _full((TAB_ROWS, LANES)), _full((TAB_ROWS, LANES)), _full((TAB_ROWS, LANES)), _full((TAB_ROWS, LANES)),
        ],
        out_shape=[
            jax.ShapeDtypeStruct((N_PAD, LANES), F32),
            jax.ShapeDtypeStruct((N_PAD, LANES), F32),
            jax.ShapeDtypeStruct((LANES, N_PAD), F32),
            jax.ShapeDtypeStruct((XBE_ROWS, LANES), jnp.int32),
            jax.ShapeDtypeStruct((XBE_ROWS, LANES), jnp.int32),
            jax.ShapeDtypeStruct((8, LANES), jnp.int32),
            tab, tab, tab, tab,
        ],
        scratch_shapes=[pltpu.VMEM((1, LANES), F32), pltpu.VMEM((1, LANES), F32)],
        compiler_params=pltpu.CompilerParams(
            dimension_semantics=("arbitrary",), vmem_limit_bytes=VMEM_LIMIT),
        name="moe_plan",
    )(sm, cnt)


def _pack_rows(z):
    half = D_MODEL // 2
    lo = lax.bitcast_convert_type(z[:, :half], jnp.uint32) >> 16
    hi = lax.bitcast_convert_type(z[:, half:], jnp.uint32) & jnp.uint32(0xFFFF0000)
    return lax.bitcast_convert_type(hi | lo, jnp.int32)


def _unpack_rows(ref, rows=None):
    rows = ref.shape[0] if rows is None else rows
    flat = ref.reshape(ref.shape[0] * PACK, LANES)
    lo, hi = [], []
    for s in range(PACK):
        w = lax.bitcast_convert_type(flat[pl.ds(s, rows, stride=PACK), :], jnp.uint32)
        lo.append(lax.bitcast_convert_type(w << 16, F32))
        hi.append(lax.bitcast_convert_type(w & jnp.uint32(0xFFFF0000), F32))
    return jnp.concatenate(lo + hi, axis=-1).astype(BF16)


def _dispatch_kernel(stab_ref, ctab_ref, cmax_ref, xn_ref, destt_ref, xs_in, xs_ref,
                     stage0, stage1, stage2, sems, sem2):
    del xs_in
    i = pl.program_id(0)
    last = pl.num_programs(0) - 1
    x = xn_ref[...]
    dt = destt_ref[...]
    rio = lax.broadcasted_iota(jnp.int32, (DISP_CHUNK, 1), 0).astype(F32)

    def chunk_rows(j, stage):
        parts = []
        for e in range(N_EXPERTS):
            first = (stab_ref[i * N_EXPERTS + e] + j * DISP_CHUNK).astype(F32)
            parts.append(jnp.where(dt[e:e + 1, :] == first + rio, 1.0, 0.0).astype(BF16))
        onehot = jnp.concatenate(parts, axis=0)
        words = _pack_rows(jnp.dot(onehot, x, preferred_element_type=F32))
        for s in range(PACK):
            stage[pl.ds(s, N_EXPERTS * DISP_CHUNK, stride=PACK), :] = words[:, s * LANES:(s + 1) * LANES]

    def copy(stage, step, e, j, sem):
        first = stab_ref[step * N_EXPERTS + e] + j * DISP_CHUNK
        rows = stage.reshape(N_EXPERTS * DISP_CHUNK, PACK, LANES)
        return pltpu.make_async_copy(rows.at[pl.ds(e * DISP_CHUNK, DISP_CHUNK)],
                                     xs_ref.at[pl.ds(first, DISP_CHUNK)], sem)

    def step_body(stage, prev_stage, par):
        chunk_rows(0, stage)

        @pl.when(i > 0)
        def _():
            for e in range(N_EXPERTS):
                copy(prev_stage, i - 1, e, 0, sems.at[1 - par]).wait()

        for e in range(N_EXPERTS):
            copy(stage, i, e, 0, sems.at[par]).start()

        @pl.when(i == last)
        def _():
            for e in range(N_EXPERTS):
                copy(stage, i, e, 0, sems.at[par]).wait()

    @pl.when(i % 2 == 0)
    def _():
        step_body(stage0, stage1, 0)

    @pl.when(i % 2 == 1)
    def _():
        step_body(stage1, stage0, 1)

    for j in range(1, TOK_TILE // DISP_CHUNK):

        @pl.when(cmax_ref[i] > j * DISP_CHUNK)
        def _(j=j):
            chunk_rows(j, stage2)
            for e in range(N_EXPERTS):

                @pl.when(ctab_ref[i * N_EXPERTS + e] > j * DISP_CHUNK)
                def _(e=e):
                    cp = copy(stage2, i, e, j, sem2)
                    cp.start()
                    cp.wait()


def _dispatch_call(stab, ctab, cmax, xn, destt, xs_zero):
    stage = pltpu.VMEM((N_EXPERTS * DISP_CHUNK * PACK, LANES), jnp.int32)
    grid_spec = pltpu.PrefetchScalarGridSpec(
        num_scalar_prefetch=3,
        grid=(N_CTILES,),
        in_specs=[
            pl.BlockSpec((TOK_TILE, D_MODEL), lambda i, *_: (i, 0)),
            pl.BlockSpec((N_EXPERTS, TOK_TILE), lambda i, *_: (0, i)),
            pl.BlockSpec(memory_space=pl.ANY),
        ],
        out_specs=pl.BlockSpec(memory_space=pl.ANY),
        scratch_shapes=[stage, stage, stage, pltpu.SemaphoreType.DMA((2,)), pltpu.SemaphoreType.DMA],
    )
    return pl.pallas_call(
        _dispatch_kernel,
        grid_spec=grid_spec,
        out_shape=jax.ShapeDtypeStruct((XS_ROWS, PACK, LANES), jnp.int32),
        input_output_aliases={5: 0},
        compiler_params=pltpu.CompilerParams(
            dimension_semantics=("arbitrary",), vmem_limit_bytes=VMEM_LIMIT),
        name="moe_dispatch",
    )(stab, ctab, cmax, xn, destt, xs_zero)


def _expert_kernel(blke_ref, nblk_ref, live_ref, xs_ref, wgu_hbm, bgu_ref, wdn_hbm, bdn_ref,
                   ys_ref, wgu_f, wdn_f, wgu_s, wdn_s, sems):
    b = pl.program_id(0)
    used = b < nblk_ref[0]
    prev = blke_ref[jnp.maximum(b - 1, 0)]
    fresh = used & ((b == 0) | (blke_ref[b] != prev))

    def fetch(e):
        return (pltpu.make_async_copy(wgu_hbm.at[e], wgu_f, sems.at[0]),
                pltpu.make_async_copy(wdn_hbm.at[e], wdn_f, sems.at[1]))

    @pl.when((b == 0) & used)
    def _():
        for cp in fetch(blke_ref[0]):
            cp.start()

    @pl.when(fresh)
    def _():
        for cp in fetch(blke_ref[b]):
            cp.wait()
        wgu_s[...] = wgu_f[...].astype(BF16)
        wdn_s[...] = wdn_f[...].astype(BF16)

        nxt = lax.while_loop(lambda p: (p < nblk_ref[0]) & (blke_ref[jnp.minimum(p, N_XBLOCKS - 1)] == blke_ref[b]),
                             lambda p: p + 1, b + 1)

        @pl.when(nxt < nblk_ref[0])
        def _():
            for cp in fetch(blke_ref[jnp.minimum(nxt, N_XBLOCKS - 1)]):
                cp.start()

    for groups in range(1, EXP_BLOCK // ROW_GROUP + 1):
        rows = groups * ROW_GROUP

        @pl.when(used & (live_ref[b] == groups))
        def _(rows=rows):
            hid = jnp.dot(_unpack_rows(xs_ref, rows), wgu_s[...], preferred_element_type=F32) + bgu_ref[...]
            gate = jnp.minimum(hid[:, :D_FF], SWIGLU_LIMIT)
            up = jnp.clip(hid[:, D_FF:], -SWIGLU_LIMIT, SWIGLU_LIMIT)
            act = (up + 1.0) * gate * jax.nn.sigmoid(SWIGLU_ALPHA * gate)
            y = jnp.dot(act.astype(BF16), wdn_s[...], preferred_element_type=F32) + bdn_ref[...]
            ys_ref[0:rows, :] = y.astype(BF16)
            if rows < EXP_BLOCK:
                ys_ref[rows:, :] = jnp.zeros((EXP_BLOCK - rows, D_MODEL), BF16)

    @pl.when(jnp.logical_not(used) | (live_ref[b] == 0))
    def _():
        ys_ref[...] = jnp.zeros_like(ys_ref)


def _expert_call(blke, nblk, live, xs, wgu, bgu, wdn, bdn):
    grid_spec = pltpu.PrefetchScalarGridSpec(
        num_scalar_prefetch=3,
        grid=(N_XBLOCKS,),
        in_specs=[
            pl.BlockSpec((EXP_BLOCK, PACK, LANES), lambda b, be, *_: (b, 0, 0)),
            pl.BlockSpec(memory_space=pl.ANY),
            pl.BlockSpec((None, 1, 2 * D_FF), lambda b, be, *_: (be[b], 0, 0)),
            pl.BlockSpec(memory_space=pl.ANY),
            pl.BlockSpec((None, 1, D_MODEL), lambda b, be, *_: (be[b], 0, 0)),
        ],
        out_specs=pl.BlockSpec((EXP_BLOCK, D_MODEL), lambda b, be, *_: (b, 0)),
        scratch_shapes=[pltpu.VMEM((D_MODEL, 2 * D_FF), F32), pltpu.VMEM((D_FF, D_MODEL), F32),
                        pltpu.VMEM((D_MODEL, 2 * D_FF), BF16), pltpu.VMEM((D_FF, D_MODEL), BF16),
                        pltpu.SemaphoreType.DMA((2,))],
    )
    return pl.pallas_call(
        _expert_kernel,
        grid_spec=grid_spec,
        out_shape=jax.ShapeDtypeStruct((N_ROWS, D_MODEL), BF16),
        compiler_params=pltpu.CompilerParams(
            dimension_semantics=("arbitrary",), vmem_limit_bytes=VMEM_LIMIT),
        name="moe_experts",
    )(blke, nblk, live, xs, wgu, bgu, wdn, bdn)


def _combine_kernel(astart_ref, nwin_ref, over_ref,
                    destm_ref, posm_ref, gm_ref, rep_ref, h1_ref, plep_ref, ples_ref, gple_ref, wpg_ref, wpp_ref,
                    gfin_ref, ys_any, yp_ref, ysm_ref, wins, moe_s, tmp_s, wsems, sem):
    i = pl.program_id(0)
    slot = i % WIN_SLOTS

    def win_copy(step, e, into):
        first = pl.multiple_of(astart_ref[step * N_EXPERTS + e], WIN_ALIGN)
        return pltpu.make_async_copy(ys_any.at[pl.ds(first, WIN)], wins.at[into, pl.ds(e * WIN, WIN)],
                                     wsems.at[into])

    @pl.when(i == 0)
    def _():
        for e in range(N_EXPERTS):
            win_copy(0, e, 0).start()
        for e in range(N_EXPERTS):
            win_copy(1, e, 1).start()

    for e in range(N_EXPERTS):
        win_copy(i, e, slot).wait()

    gates = gm_ref[...]
    rep = rep_ref[...]
    pos_rep = jnp.dot(posm_ref[...].astype(BF16), rep, preferred_element_type=F32)
    gate_rep = jnp.dot(gates.astype(BF16), rep, preferred_element_type=F32)
    within = (_lane_iota((TOK_TILE, N_EXPERTS * WIN)) % WIN).astype(F32)
    gsel = jnp.where(pos_rep == within, gate_rep, 0.0).astype(BF16)
    mrows = pl.ds(pl.multiple_of((i % TAIL_TILES) * TOK_TILE, TOK_TILE), TOK_TILE)
    moe_s[mrows, :] = jnp.dot(gsel, wins[slot], preferred_element_type=F32)

    @pl.when(i + 2 < pl.num_programs(0))
    def _():
        for e in range(N_EXPERTS):
            win_copy(i + 2, e, (i + 2) % WIN_SLOTS).start()

    @pl.when(over_ref[i] > 0)
    def _():
        dest = destm_ref[...]
        lane = _lane_iota((TOK_TILE, LANES))
        lane_f = lane.astype(F32)
        tmp_s[...] = jnp.zeros_like(tmp_s)

        def per_expert(e, carry):
            a = astart_ref[i * N_EXPERTS + e]
            dcol = jnp.sum(jnp.where(lane == e, dest, 0.0), axis=-1, keepdims=True)
            gcol = jnp.sum(jnp.where(lane == e, gates, 0.0), axis=-1, keepdims=True)

            def per_window(w, carry2):
                first = a + w * WIN
                start = pl.multiple_of(jnp.minimum(first, N_ROWS - WIN), WIN_ALIGN)
                cp = pltpu.make_async_copy(ys_any.at[pl.ds(start, WIN)], tmp_s.at[pl.ds(0, WIN)], sem)
                cp.start()
                cp.wait()
                hit = (lane < WIN) & (dcol == start.astype(F32) + lane_f) & (dcol >= first.astype(F32))
                sel = jnp.where(hit, gcol, 0.0).astype(BF16)
                moe_s[mrows, :] += jnp.dot(sel, tmp_s[...], preferred_element_type=F32)
                return carry2

            return lax.fori_loop(1, nwin_ref[i * N_EXPERTS + e], per_window, carry)

        lax.fori_loop(0, N_EXPERTS, per_expert, 0)

    def tail(h1, moe_rows, ple):
        h2 = h1 + moe_rows
        hn = _rms(h2, gple_ref[...]).astype(BF16)
        gate = jax.nn.sigmoid(jnp.dot(hn, wpg_ref[...], preferred_element_type=F32))
        proj = jnp.dot(ple.astype(BF16), wpp_ref[...], preferred_element_type=F32)
        return _rms(h2 + gate * proj, gfin_ref[...])

    is_sample = i == N_CTILES - 1

    @pl.when((i % TAIL_TILES == TAIL_TILES - 1) & jnp.logical_not(is_sample))
    def _():
        yp_ref[...] = tail(h1_ref[...], moe_s[...], plep_ref[...])

    @pl.when(is_sample)
    def _():
        ysm_ref[...] = tail(h1_ref[0:DEC_BATCH, :], moe_s[0:DEC_BATCH, :], ples_ref[...])


def _combine_call(astart, nwin, over, ys, destm, posm, gm, h1, plep, ples, gple, wpg, wpp, gfin):
    rep = (jnp.arange(N_EXPERTS * WIN)[None, :] // WIN == jnp.arange(LANES)[:, None]).astype(BF16)
    rows = TAIL_TILES * TOK_TILE
    last = N_PROMPT // rows - 1

    grid_spec = pltpu.PrefetchScalarGridSpec(
        num_scalar_prefetch=3,
        grid=(N_CTILES,),
        in_specs=[
            pl.BlockSpec((TOK_TILE, LANES), lambda i, *_: (i, 0)),
            pl.BlockSpec((TOK_TILE, LANES), lambda i, *_: (i, 0)),
            pl.BlockSpec((TOK_TILE, LANES), lambda i, *_: (i, 0)),
            pl.BlockSpec((LANES, N_EXPERTS * WIN), lambda i, *_: (0, 0)),
            pl.BlockSpec((rows, D_MODEL), lambda i, *_: (i // TAIL_TILES, 0)),
            pl.BlockSpec((rows, PLE_DIM), lambda i, *_: (jnp.minimum(i // TAIL_TILES, last), 0)),
            pl.BlockSpec((DEC_BATCH, PLE_DIM), lambda i, *_: (0, 0)),
            pl.BlockSpec((1, D_MODEL), lambda i, *_: (0, 0)),
            pl.BlockSpec((D_MODEL, D_MODEL), lambda i, *_: (0, 0)),
            pl.BlockSpec((PLE_DIM, D_MODEL), lambda i, *_: (0, 0)),
            pl.BlockSpec((1, D_MODEL), lambda i, *_: (0, 0)),
            pl.BlockSpec(memory_space=pl.ANY),
        ],
        out_specs=[
            pl.BlockSpec((rows, D_MODEL), lambda i, *_: (jnp.minimum(i // TAIL_TILES, last), 0)),
            pl.BlockSpec((DEC_BATCH, D_MODEL), lambda i, *_: (0, 0)),
        ],
        scratch_shapes=[pltpu.VMEM((WIN_SLOTS, N_EXPERTS * WIN, D_MODEL), BF16), pltpu.VMEM((rows, D_MODEL), F32),
                        pltpu.VMEM((2 * WIN, D_MODEL), BF16), pltpu.SemaphoreType.DMA((WIN_SLOTS,)),
                        pltpu.SemaphoreType.DMA],
    )
    return pl.pallas_call(
        _combine_kernel,
        grid_spec=grid_spec,
        out_shape=[jax.ShapeDtypeStruct((N_PROMPT, D_MODEL), F32),
                   jax.ShapeDtypeStruct((DEC_BATCH, D_MODEL), F32)],
        compiler_params=pltpu.CompilerParams(
            dimension_semantics=("arbitrary",), vmem_limit_bytes=VMEM_LIMIT),
        name="moe_combine_tail",
    )(astart, nwin, over, destm, posm, gm, rep, h1, plep, ples, gple, wpg, wpp, gfin, ys)


def kernel(x_prompt, x_sample, cache_swa_k, cache_swa_v, p_prompt, p_sample, g_mix, w_in, ln_v_g, ln_v_b,
           w_sp, b_sp, sinks, g_out_a, g_out_b, w_o, g_moe, w_router, b_router, w_gu, b_gu, w_dn, b_dn,
           g_ple, w_ple_gate, w_ple_proj, g_final):
    l = 0
    row = lambda v: v.reshape(1, -1)
    win = w_in[l].astype(BF16)
    wo = w_o[l].astype(BF16)
    tril = jnp.tril(jnp.ones((CHUNK, CHUNK), dtype=bool))
    wsp = jnp.where(tril, w_sp[l], 0.0).astype(BF16).reshape(4, 2 * CHUNK, CHUNK)
    bsp = jnp.repeat(b_sp[l].T, HEAD_DIM, axis=1)
    w00 = row(jnp.repeat(w_sp[l][:, 0, 0], HEAD_DIM))
    b0 = row(jnp.repeat(b_sp[l][:, 0], HEAD_DIM))
    wr_hi = w_router[l].astype(BF16)
    wr_lo = (w_router[l] - wr_hi.astype(F32)).astype(BF16)
    wr = jnp.concatenate([wr_hi, wr_lo, jnp.zeros((D_MODEL, LANES - 2 * N_EXPERTS), BF16)], axis=1)
    br = row(jnp.concatenate([b_router[l], jnp.zeros((LANES - N_EXPERTS,), F32)]))
    common = (row(g_mix[l]), win, row(ln_v_g[l]), row(ln_v_b[l]))
    tail = (row(g_out_a[l]), row(g_out_b[l]), wo, row(g_moe[l]), wr, br)

    h1, xn, gm, sm, k_p, v_p, xs_zero, cnt = _prompt_call(
        x_prompt.reshape(N_PROMPT, D_MODEL), sinks[l], *common, wsp, bsp, *tail)
    chan = lambda a: jnp.transpose(a, (0, 2, 3, 1)).reshape(DEC_BATCH, KV_WIDTH, CHUNK)
    ck = chan(cache_swa_k[l])
    cv = chan(cache_swa_v[l])
    h1, xn, gm, sm, k_s, v_s, va_s, cnt = _decode_call(
        x_sample.reshape(DEC_BATCH, D_MODEL), ck, cv, sinks[l], *common, w00, b0, *tail, h1, xn, gm, sm, cnt)

    destm, posm, destt, xbe, live, nxblk, stab, ctab, astart, nwin = _plan_call(sm, cnt)
    flat = lambda tab, n: tab[:n, :N_EXPERTS].reshape(-1)
    over = (jnp.max(nwin[:N_CTILES, :N_EXPERTS], axis=1) > 1).astype(jnp.int32)
    cmax = jnp.max(ctab[:N_CTILES, :N_EXPERTS], axis=1)

    xs = _dispatch_call(flat(stab, N_CTILES), flat(ctab, N_CTILES), cmax, xn, destt[:N_EXPERTS], xs_zero)
    ys = _expert_call(xbe[:N_XBLOCKS, 0], nxblk[0, :1], live[:N_XBLOCKS, 0], xs,
                      w_gu[l], b_gu[l].reshape(N_EXPERTS, 1, 2 * D_FF), w_dn[l], b_dn[l].reshape(N_EXPERTS, 1, D_MODEL))
    y_p, y_s = _combine_call(
        flat(astart, N_CTILES), flat(nwin, N_CTILES), over, ys, destm, posm, gm, h1,
        p_prompt[l].reshape(N_PROMPT, PLE_DIM), p_sample[l].reshape(DEC_BATCH, PLE_DIM),
        row(g_ple[l]), w_ple_gate[l].astype(BF16), w_ple_proj[l].astype(BF16), row(g_final))

    kv5 = lambda a, n: a.reshape(1, n, CHUNK, 2, HEAD_DIM)
    win5 = lambda a: jnp.transpose(a.reshape(DEC_BATCH, 2, HEAD_DIM, CHUNK), (0, 3, 1, 2))[None]
    return (y_p.reshape(BATCH, SEQ, D_MODEL), y_s.reshape(DEC_BATCH, 1, D_MODEL),
            kv5(k_p, BATCH), kv5(v_p, BATCH), win5(k_s), win5(v_s),
            va_s.reshape(1, DEC_BATCH, 1, A_WIDTH))
```
